```python
import math
import jax
import jax.numpy as jnp
from jax import lax
import numpy as np

D_MODEL = 2048
BATCH = 8
SEQ = 4096
DEPTH = 2

N_A_LAYERS = max(1, DEPTH // 2)
N_B_LAYERS = DEPTH - N_A_LAYERS
S5_GROUP_CH = 16
S5_GROUPS = D_MODEL // S5_GROUP_CH
S5_STATE = 64
HEAD_DIM = 128
N_HEADS = D_MODEL // HEAD_DIM
N_KV_HEADS = 4
DILATED_PATTERNS = ((128, 1), (512, 4), (2048, 16))
N_GROUPS = len(DILATED_PATTERNS)
FFN_HIDDEN = ((8 * D_MODEL // 3 + 255) // 256) * 256
EPS = 1e-6
NEG_INF = -1e30

kernel_name = "yoco_s5_dilated_attention_hybrid"


def rms_norm(x, g):
    xf = x.astype(jnp.float32)
    r = lax.rsqrt(jnp.mean(xf * xf, axis=-1, keepdims=True) + EPS)
    return (xf * r * g.astype(jnp.float32)).astype(x.dtype)


def swiglu(h, w_in, w_out):
    gate, up = jnp.split(h @ w_in, 2, axis=-1)
    return (jax.nn.silu(gate) * up) @ w_out


def s5_mixer(u, lam_re, lam_im, log_dt, b_re, b_im, c_re, c_im, d_skip, w_glu):
    bsz, seq, dm = u.shape
    f32 = jnp.float32
    ug = u.astype(f32).reshape(bsz, seq, S5_GROUPS, S5_GROUP_CH)
    lr = lam_re.astype(f32)
    li = lam_im.astype(f32)
    dt = jnp.exp(log_dt.astype(f32))[:, None]
    mag = jnp.exp(lr * dt)
    ang = li * dt
    lb_re = mag * jnp.cos(ang)
    lb_im = mag * jnp.sin(ang)
    nr = lb_re - 1.0
    den = lr * lr + li * li
    f_re = (nr * lr + lb_im * li) / den
    f_im = (lb_im * lr - nr * li) / den
    br = b_re.astype(f32)
    bi = b_im.astype(f32)
    bb_re = f_re[..., None] * br - f_im[..., None] * bi
    bb_im = f_re[..., None] * bi + f_im[..., None] * br
    bu_re = jnp.einsum('bsgc,gpc->bsgp', ug, bb_re)
    bu_im = jnp.einsum('bsgc,gpc->bsgp', ug, bb_im)
    a_re = jnp.broadcast_to(lb_re, bu_re.shape)
    a_im = jnp.broadcast_to(lb_im, bu_im.shape)

    def combine(e1, e2):
        a1r, a1i, b1r, b1i = e1
        a2r, a2i, b2r, b2i = e2
        return (a2r * a1r - a2i * a1i,
                a2r * a1i + a2i * a1r,
                a2r * b1r - a2i * b1i + b2r,
                a2r * b1i + a2i * b1r + b2i)

    _, _, xs_re, xs_im = lax.associative_scan(combine, (a_re, a_im, bu_re, bu_im), axis=1)
    y = (jnp.einsum('bsgp,gcp->bsgc', xs_re, c_re.astype(f32))
         - jnp.einsum('bsgp,gcp->bsgc', xs_im, c_im.astype(f32)))
    y = y.reshape(bsz, seq, dm) + d_skip.astype(f32) * u.astype(f32)
    z = jax.nn.gelu(y).astype(u.dtype)
    val, gate = jnp.split(z @ w_glu, 2, axis=-1)
    return val * jax.nn.sigmoid(gate)


def dilated_window_attention(q, k, v, window, dilation):
    bsz, seq, nh, hd = q.shape
    nkv = k.shape[2]
    rep = nh // nkv
    n = seq // dilation
    blk = window // dilation
    nb = -(-n // blk)
    pad = nb * blk - n

    def to_residue(t):
        h = t.shape[2]
        t = t.reshape(bsz, n, dilation, h, hd).transpose(0, 2, 1, 3, 4)
        t = t.reshape(bsz * dilation, n, h, hd)
        t = jnp.pad(t, ((0, 0), (0, pad), (0, 0), (0, 0)))
        return t.reshape(bsz * dilation, nb, blk, h, hd)

    def with_prev(t):
        prev = jnp.pad(t, ((0, 0), (1, 0), (0, 0), (0, 0), (0, 0)))[:, :-1]
        return jnp.concatenate([prev, t], axis=2)

    qr = to_residue(q).reshape(bsz * dilation, nb, blk, nkv, rep, hd)
    kb = with_prev(to_residue(k))
    vb = with_prev(to_residue(v))
    scores = jnp.einsum('znqkge,znske->znkgqs', qr, kb).astype(jnp.float32) * (hd ** -0.5)
    qi = jnp.arange(blk)[:, None]
    si = jnp.arange(2 * blk)[None, :]
    dist = qi + blk - si
    band = (dist >= 0) & (dist <= blk)
    valid = band[None] & ((jnp.arange(nb)[:, None, None] > 0) | (si[None] >= blk))
    scores = jnp.where(valid[None, :, None, None], scores, NEG_INF)
    m = jnp.max(scores, axis=-1, keepdims=True)
    p = jnp.exp(scores - m)
    l = jnp.sum(p, axis=-1, keepdims=True)
    out = jnp.einsum('znkgqs,znske->znqkge', (p / l).astype(v.dtype), vb)
    lse = (m + jnp.log(l))[..., 0]
    out = out.reshape(bsz * dilation, nb * blk, nh, hd)[:, :n]
    out = out.reshape(bsz, dilation, n, nh, hd).transpose(0, 2, 1, 3, 4).reshape(bsz, seq, nh, hd)
    lse = lse.transpose(0, 1, 4, 2, 3).reshape(bsz * dilation, nb * blk, nh)[:, :n]
    lse = lse.reshape(bsz, dilation, n, nh).transpose(0, 2, 1, 3).reshape(bsz, seq, nh)
    return out, lse


def dilated_mixer(h, k, v, w_q, w_o):
    bsz, seq, _ = h.shape
    q = (h @ w_q).reshape(bsz, seq, N_GROUPS, N_HEADS, HEAD_DIM)
    outs = []
    lses = []
    for g, (window, dilation) in enumerate(DILATED_PATTERNS):
        o, l = dilated_window_attention(q[:, :, g], k, v, window, dilation)
        outs.append(o)
        lses.append(l)
    wts = jax.nn.softmax(jnp.stack(lses, axis=0), axis=0)
    o = jnp.sum(wts[..., None] * jnp.stack(outs, axis=0).astype(jnp.float32), axis=0)
    return o.astype(h.dtype).reshape(bsz, seq, N_HEADS * HEAD_DIM) @ w_o


def _fwd_setup_inputs(seed: int = 0) -> dict:
    key = jax.random.key(seed)
    ks = jax.random.split(key, 24)
    f32 = jnp.float32

    def nrm(k, shape, scale):
        return jax.random.normal(k, shape, f32) * scale

    na, nbl, D, F = N_A_LAYERS, N_B_LAYERS, D_MODEL, FFN_HIDDEN
    G, P, C = S5_GROUPS, S5_STATE, S5_GROUP_CH
    x = nrm(ks[0], (BATCH, SEQ, D), 1.0)
    n_idx = jnp.arange(P, dtype=f32)
    s5_lam_re = -0.5 + nrm(ks[1], (na, G, P), 0.01)
    s5_lam_im = math.pi * n_idx + nrm(ks[2], (na, G, P), 0.01)
    s5_log_dt = jax.random.uniform(ks[3], (na, G), f32, math.log(1e-3), math.log(1e-1))
    s5_b_re = nrm(ks[4], (na, G, P, C), (2 * C) ** -0.5)
    s5_b_im = nrm(ks[5], (na, G, P, C), (2 * C) ** -0.5)
    s5_c_re = nrm(ks[6], (na, G, C, P), (2 * P) ** -0.5)
    s5_c_im = nrm(ks[7], (na, G, C, P), (2 * P) ** -0.5)
    s5_d = nrm(ks[8], (na, D), 1.0)
    s5_w_glu = nrm(ks[9], (na, D, 2 * D), D ** -0.5)
    a_norm_mix = 1.0 + nrm(ks[10], (na, D), 0.02)
    ffn_norm = 1.0 + nrm(ks[11], (DEPTH, D), 0.02)
    ffn_w_in = nrm(ks[12], (DEPTH, D, 2 * F), D ** -0.5)
    ffn_w_out = nrm(ks[13], (DEPTH, F, D), F ** -0.5)
    b_norm_mix = 1.0 + nrm(ks[14], (nbl, D), 0.02)
    attn_w_q = nrm(ks[15], (nbl, D, N_GROUPS * N_HEADS * HEAD_DIM), D ** -0.5)
    attn_w_o = nrm(ks[16], (nbl, N_HEADS * HEAD_DIM, D), (N_HEADS * HEAD_DIM) ** -0.5)
    kv_norm = 1.0 + nrm(ks[17], (D,), 0.02)
    w_kv = nrm(ks[18], (D, 2 * N_KV_HEADS * HEAD_DIM), D ** -0.5)
    final_norm = 1.0 + nrm(ks[19], (D,), 0.02)
    return {"x": x, "s5_lam_re": s5_lam_re, "s5_lam_im": s5_lam_im, "s5_log_dt": s5_log_dt,
            "s5_b_re": s5_b_re, "s5_b_im": s5_b_im, "s5_c_re": s5_c_re, "s5_c_im": s5_c_im,
            "s5_d": s5_d, "s5_w_glu": s5_w_glu, "a_norm_mix": a_norm_mix,
            "ffn_norm": ffn_norm, "ffn_w_in": ffn_w_in, "ffn_w_out": ffn_w_out,
            "b_norm_mix": b_norm_mix, "attn_w_q": attn_w_q, "attn_w_o": attn_w_o,
            "kv_norm": kv_norm, "w_kv": w_kv, "final_norm": final_norm}


def _fwd_reference(x, s5_lam_re, s5_lam_im, s5_log_dt, s5_b_re, s5_b_im, s5_c_re, s5_c_im,
              s5_d, s5_w_glu, a_norm_mix, ffn_norm, ffn_w_in, ffn_w_out,
              b_norm_mix, attn_w_q, attn_w_o, kv_norm, w_kv, final_norm):
    bsz, seq, _ = x.shape
    k = None
    v = None
    for layer in range(DEPTH):
        if layer < N_A_LAYERS:
            i = layer
            h = rms_norm(x, a_norm_mix[i])
            x = x + s5_mixer(h, s5_lam_re[i], s5_lam_im[i], s5_log_dt[i], s5_b_re[i], s5_b_im[i],
                             s5_c_re[i], s5_c_im[i], s5_d[i], s5_w_glu[i])
        else:
            j = layer - N_A_LAYERS
            if j == 0:
                kv = (rms_norm(x, kv_norm) @ w_kv).reshape(bsz, seq, 2, N_KV_HEADS, HEAD_DIM)
                k = kv[:, :, 0]
                v = kv[:, :, 1]
            h = rms_norm(x, b_norm_mix[j])
            x = x + dilated_mixer(h, k, v, attn_w_q[j], attn_w_o[j])
        x = x + swiglu(rms_norm(x, ffn_norm[layer]), ffn_w_in[layer], ffn_w_out[layer])
    return rms_norm(x, final_norm)


import jax as _jax
import jax.numpy as _jnp

TWIN_FORMAT = 'train_step'
FWD_PARAMS = ['x', 's5_lam_re', 's5_lam_im', 's5_log_dt', 's5_b_re', 's5_b_im', 's5_c_re', 's5_c_im', 's5_d', 's5_w_glu', 'a_norm_mix', 'ffn_norm', 'ffn_w_in', 'ffn_w_out', 'b_norm_mix', 'attn_w_q', 'attn_w_o', 'kv_norm', 'w_kv', 'final_norm']
TWIN_WEIGHTS = ['s5_lam_re', 's5_lam_im', 's5_log_dt', 's5_b_re', 's5_b_im', 's5_c_re', 's5_c_im', 's5_d', 's5_w_glu', 'a_norm_mix', 'ffn_norm', 'ffn_w_in', 'ffn_w_out', 'b_norm_mix', 'attn_w_q', 'attn_w_o', 'kv_norm', 'w_kv', 'final_norm']
TWIN_DIFF_INPUT = 'x'
TWIN_INPUTS = ['x', 's5_lam_re', 's5_lam_im', 's5_log_dt', 's5_b_re', 's5_b_im', 's5_c_re', 's5_c_im', 's5_d', 's5_w_glu', 'a_norm_mix', 'ffn_norm', 'ffn_w_in', 'ffn_w_out', 'b_norm_mix', 'attn_w_q', 'attn_w_o', 'kv_norm', 'w_kv', 'final_norm', 'loss_target', 'm_s5_lam_re', 'm_s5_lam_im', 'm_s5_log_dt', 'm_s5_b_re', 'm_s5_b_im', 'm_s5_c_re', 'm_s5_c_im', 'm_s5_d', 'm_s5_w_glu', 'm_a_norm_mix', 'm_ffn_norm', 'm_ffn_w_in', 'm_ffn_w_out', 'm_b_norm_mix', 'm_attn_w_q', 'm_attn_w_o', 'm_kv_norm', 'm_w_kv', 'm_final_norm', 'v_s5_lam_re', 'v_s5_lam_im', 'v_s5_log_dt', 'v_s5_b_re', 'v_s5_b_im', 'v_s5_c_re', 'v_s5_c_im', 'v_s5_d', 'v_s5_w_glu', 'v_a_norm_mix', 'v_ffn_norm', 'v_ffn_w_in', 'v_ffn_w_out', 'v_b_norm_mix', 'v_attn_w_q', 'v_attn_w_o', 'v_kv_norm', 'v_w_kv', 'v_final_norm']
TWIN_OUTPUTS = ['loss', 'grad_x', 'grad_s5_lam_re', 'grad_s5_lam_im', 'grad_s5_log_dt', 'grad_s5_b_re', 'grad_s5_b_im', 'grad_s5_c_re', 'grad_s5_c_im', 'grad_s5_d', 'grad_s5_w_glu', 'grad_a_norm_mix', 'grad_ffn_norm', 'grad_ffn_w_in', 'grad_ffn_w_out', 'grad_b_norm_mix', 'grad_attn_w_q', 'grad_attn_w_o', 'grad_kv_norm', 'grad_w_kv', 'grad_final_norm', 'delta_s5_lam_re', 'delta_s5_lam_im', 'delta_s5_log_dt', 'delta_s5_b_re', 'delta_s5_b_im', 'delta_s5_c_re', 'delta_s5_c_im', 'delta_s5_d', 'delta_s5_w_glu', 'delta_a_norm_mix', 'delta_ffn_norm', 'delta_ffn_w_in', 'delta_ffn_w_out', 'delta_b_norm_mix', 'delta_attn_w_q', 'delta_attn_w_o', 'delta_kv_norm', 'delta_w_kv', 'delta_final_norm', 'new_m_s5_lam_re', 'new_m_s5_lam_im', 'new_m_s5_log_dt', 'new_m_s5_b_re', 'new_m_s5_b_im', 'new_m_s5_c_re', 'new_m_s5_c_im', 'new_m_s5_d', 'new_m_s5_w_glu', 'new_m_a_norm_mix', 'new_m_ffn_norm', 'new_m_ffn_w_in', 'new_m_ffn_w_out', 'new_m_b_norm_mix', 'new_m_attn_w_q', 'new_m_attn_w_o', 'new_m_kv_norm', 'new_m_w_kv', 'new_m_final_norm', 'new_v_s5_lam_re', 'new_v_s5_lam_im', 'new_v_s5_log_dt', 'new_v_s5_b_re', 'new_v_s5_b_im', 'new_v_s5_c_re', 'new_v_s5_c_im', 'new_v_s5_d', 'new_v_s5_w_glu', 'new_v_a_norm_mix', 'new_v_ffn_norm', 'new_v_ffn_w_in', 'new_v_ffn_w_out', 'new_v_b_norm_mix', 'new_v_attn_w_q', 'new_v_attn_w_o', 'new_v_kv_norm', 'new_v_w_kv', 'new_v_final_norm']
TWIN_LEAF_KINDS = {'loss': 'loss', 'grad_x': 'grad_x', 'grad_s5_lam_re': 'grad_w', 'grad_s5_lam_im': 'grad_w', 'grad_s5_log_dt': 'grad_w', 'grad_s5_b_re': 'grad_w', 'grad_s5_b_im': 'grad_w', 'grad_s5_c_re': 'grad_w', 'grad_s5_c_im': 'grad_w', 'grad_s5_d': 'grad_w', 'grad_s5_w_glu': 'grad_w', 'grad_a_norm_mix': 'grad_w', 'grad_ffn_norm': 'grad_w', 'grad_ffn_w_in': 'grad_w', 'grad_ffn_w_out': 'grad_w', 'grad_b_norm_mix': 'grad_w', 'grad_attn_w_q': 'grad_w', 'grad_attn_w_o': 'grad_w', 'grad_kv_norm': 'grad_w', 'grad_w_kv': 'grad_w', 'grad_final_norm': 'grad_w', 'delta_s5_lam_re': 'delta_w', 'delta_s5_lam_im': 'delta_w', 'delta_s5_log_dt': 'delta_w', 'delta_s5_b_re': 'delta_w', 'delta_s5_b_im': 'delta_w', 'delta_s5_c_re': 'delta_w', 'delta_s5_c_im': 'delta_w', 'delta_s5_d': 'delta_w', 'delta_s5_w_glu': 'delta_w', 'delta_a_norm_mix': 'delta_w', 'delta_ffn_norm': 'delta_w', 'delta_ffn_w_in': 'delta_w', 'delta_ffn_w_out': 'delta_w', 'delta_b_norm_mix': 'delta_w', 'delta_attn_w_q': 'delta_w', 'delta_attn_w_o': 'delta_w', 'delta_kv_norm': 'delta_w', 'delta_w_kv': 'delta_w', 'delta_final_norm': 'delta_w', 'new_m_s5_lam_re': 'new_m', 'new_m_s5_lam_im': 'new_m', 'new_m_s5_log_dt': 'new_m', 'new_m_s5_b_re': 'new_m', 'new_m_s5_b_im': 'new_m', 'new_m_s5_c_re': 'new_m', 'new_m_s5_c_im': 'new_m', 'new_m_s5_d': 'new_m', 'new_m_s5_w_glu': 'new_m', 'new_m_a_norm_mix': 'new_m', 'new_m_ffn_norm': 'new_m', 'new_m_ffn_w_in': 'new_m', 'new_m_ffn_w_out': 'new_m', 'new_m_b_norm_mix': 'new_m', 'new_m_attn_w_q': 'new_m', 'new_m_attn_w_o': 'new_m', 'new_m_kv_norm': 'new_m', 'new_m_w_kv': 'new_m', 'new_m_final_norm': 'new_m', 'new_v_s5_lam_re': 'new_v', 'new_v_s5_lam_im': 'new_v', 'new_v_s5_log_dt': 'new_v', 'new_v_s5_b_re': 'new_v', 'new_v_s5_b_im': 'new_v', 'new_v_s5_c_re': 'new_v', 'new_v_s5_c_im': 'new_v', 'new_v_s5_d': 'new_v', 'new_v_s5_w_glu': 'new_v', 'new_v_a_norm_mix': 'new_v', 'new_v_ffn_norm': 'new_v', 'new_v_ffn_w_in': 'new_v', 'new_v_ffn_w_out': 'new_v', 'new_v_b_norm_mix': 'new_v', 'new_v_attn_w_q': 'new_v', 'new_v_attn_w_o': 'new_v', 'new_v_kv_norm': 'new_v', 'new_v_w_kv': 'new_v', 'new_v_final_norm': 'new_v'}


def _forward(args):
    return _fwd_reference(*[args[k] for k in FWD_PARAMS])


def _output_shape():
    def fwd():
        inp = _fwd_setup_inputs(0)
        return _fwd_reference(*[inp[k] for k in FWD_PARAMS])
    out = _jax.eval_shape(fwd)
    return out.shape, out.dtype

N_MICROBATCH = 1
ADAM_LR = 0.001
ADAM_B1 = 0.9
ADAM_B2 = 0.999
ADAM_EPS = 1e-08
ADAM_WD = 0.01
ADAM_STEP = 10
PER_EXAMPLE_BATCH_AXIS = {'x': 0, 'loss_target': 0}
SHARED_INPUTS = []
_WEIGHT_DTYPES = {'s5_lam_re': _jnp.float32, 's5_lam_im': _jnp.float32, 's5_log_dt': _jnp.float32, 's5_b_re': _jnp.float32, 's5_b_im': _jnp.float32, 's5_c_re': _jnp.float32, 's5_c_im': _jnp.float32, 's5_d': _jnp.float32, 's5_w_glu': _jnp.float32, 'a_norm_mix': _jnp.float32, 'ffn_norm': _jnp.float32, 'ffn_w_in': _jnp.float32, 'ffn_w_out': _jnp.float32, 'b_norm_mix': _jnp.float32, 'attn_w_q': _jnp.float32, 'attn_w_o': _jnp.float32, 'kv_norm': _jnp.float32, 'w_kv': _jnp.float32, 'final_norm': _jnp.float32}
MOMENT_SCALE = {'s5_lam_re': 1.817202e-03, 's5_lam_im': 1.992741e-03, 's5_log_dt': 1.231739e+00, 's5_b_re': 1.230614e-03, 's5_b_im': 1.211437e-03, 's5_c_re': 2.439677e-03, 's5_c_im': 2.452046e-03, 's5_d': 4.181545e-02, 's5_w_glu': 2.866793e-02, 'a_norm_mix': 4.457860e-02, 'ffn_norm': 6.394405e-02, 'ffn_w_in': 2.665775e-02, 'ffn_w_out': 4.346838e-02, 'b_norm_mix': 1.209642e-02, 'attn_w_q': 6.931532e-03, 'attn_w_o': 1.974243e-02, 'kv_norm': 2.259221e-02, 'w_kv': 3.238392e-02, 'final_norm': 1.600162e+01}


def _to_microbatches(a, axis):
    t = _jnp.moveaxis(a, axis, 0)
    t = t.reshape((N_MICROBATCH, t.shape[0] // N_MICROBATCH) + t.shape[1:])
    return _jnp.moveaxis(t, 1, axis + 1)


def setup_inputs(seed: int = 0) -> dict:
    inp = _fwd_setup_inputs(seed)
    key = _jax.random.fold_in(_jax.random.key(seed), 7919)
    shape, _ = _output_shape()
    out = dict(inp)
    out["loss_target"] = _jax.random.normal(_jax.random.fold_in(key, 0), shape, _jnp.float32)
    for i, name in enumerate(TWIN_WEIGHTS):
        w = inp[name].astype(_jnp.float32)
        if MOMENT_SCALE is None:
            s = _jnp.sqrt(_jnp.mean(_jnp.square(w)) + 1e-30)
        else:
            s = MOMENT_SCALE[name]
        km, kv = _jax.random.split(_jax.random.fold_in(key, i + 1))
        out[name] = w
        out["m_" + name] = s * _jax.random.normal(km, w.shape, _jnp.float32)
        out["v_" + name] = (s * s) * _jax.random.uniform(kv, w.shape, _jnp.float32, 0.5, 1.5)
    if N_MICROBATCH > 1:
        for name, axis in PER_EXAMPLE_BATCH_AXIS.items():
            out[name] = _to_microbatches(out[name], axis)
    return {'x': out['x'], 's5_lam_re': out['s5_lam_re'], 's5_lam_im': out['s5_lam_im'], 's5_log_dt': out['s5_log_dt'], 's5_b_re': out['s5_b_re'], 's5_b_im': out['s5_b_im'], 's5_c_re': out['s5_c_re'], 's5_c_im': out['s5_c_im'], 's5_d': out['s5_d'], 's5_w_glu': out['s5_w_glu'], 'a_norm_mix': out['a_norm_mix'], 'ffn_norm': out['ffn_norm'], 'ffn_w_in': out['ffn_w_in'], 'ffn_w_out': out['ffn_w_out'], 'b_norm_mix': out['b_norm_mix'], 'attn_w_q': out['attn_w_q'], 'attn_w_o': out['attn_w_o'], 'kv_norm': out['kv_norm'], 'w_kv': out['w_kv'], 'final_norm': out['final_norm'], 'loss_target': out['loss_target'], 'm_s5_lam_re': out['m_s5_lam_re'], 'm_s5_lam_im': out['m_s5_lam_im'], 'm_s5_log_dt': out['m_s5_log_dt'], 'm_s5_b_re': out['m_s5_b_re'], 'm_s5_b_im': out['m_s5_b_im'], 'm_s5_c_re': out['m_s5_c_re'], 'm_s5_c_im': out['m_s5_c_im'], 'm_s5_d': out['m_s5_d'], 'm_s5_w_glu': out['m_s5_w_glu'], 'm_a_norm_mix': out['m_a_norm_mix'], 'm_ffn_norm': out['m_ffn_norm'], 'm_ffn_w_in': out['m_ffn_w_in'], 'm_ffn_w_out': out['m_ffn_w_out'], 'm_b_norm_mix': out['m_b_norm_mix'], 'm_attn_w_q': out['m_attn_w_q'], 'm_attn_w_o': out['m_attn_w_o'], 'm_kv_norm': out['m_kv_norm'], 'm_w_kv': out['m_w_kv'], 'm_final_norm': out['m_final_norm'], 'v_s5_lam_re': out['v_s5_lam_re'], 'v_s5_lam_im': out['v_s5_lam_im'], 'v_s5_log_dt': out['v_s5_log_dt'], 'v_s5_b_re': out['v_s5_b_re'], 'v_s5_b_im': out['v_s5_b_im'], 'v_s5_c_re': out['v_s5_c_re'], 'v_s5_c_im': out['v_s5_c_im'], 'v_s5_d': out['v_s5_d'], 'v_s5_w_glu': out['v_s5_w_glu'], 'v_a_norm_mix': out['v_a_norm_mix'], 'v_ffn_norm': out['v_ffn_norm'], 'v_ffn_w_in': out['v_ffn_w_in'], 'v_ffn_w_out': out['v_ffn_w_out'], 'v_b_norm_mix': out['v_b_norm_mix'], 'v_attn_w_q': out['v_attn_w_q'], 'v_attn_w_o': out['v_attn_w_o'], 'v_kv_norm': out['v_kv_norm'], 'v_w_kv': out['v_w_kv'], 'v_final_norm': out['v_final_norm']}


def _loss(weights, diff, rest, loss_target):
    with _jax.named_scope("forward"):
        args = {**rest, TWIN_DIFF_INPUT: diff, **{k: w.astype(_WEIGHT_DTYPES[k]) for k, w in weights.items()}}
        y = _forward(args)
    with _jax.named_scope("loss_head"):
        err = _jnp.square(y.astype(_jnp.float32) - loss_target)
        return 0.5 * _jnp.sum(_jnp.mean(err, axis=-1)) if err.ndim else 0.5 * err


def _adamw(w, g, m, v):
    m = ADAM_B1 * m + (1.0 - ADAM_B1) * g
    v = ADAM_B2 * v + (1.0 - ADAM_B2) * _jnp.square(g)
    m_hat = m / (1.0 - ADAM_B1 ** ADAM_STEP)
    v_hat = v / (1.0 - ADAM_B2 ** ADAM_STEP)
    delta = -ADAM_LR * (m_hat / (_jnp.sqrt(v_hat) + ADAM_EPS) + ADAM_WD * w)
    return delta, m, v


def reference(x, s5_lam_re, s5_lam_im, s5_log_dt, s5_b_re, s5_b_im, s5_c_re, s5_c_im, s5_d, s5_w_glu, a_norm_mix, ffn_norm, ffn_w_in, ffn_w_out, b_norm_mix, attn_w_q, attn_w_o, kv_norm, w_kv, final_norm, loss_target, m_s5_lam_re, m_s5_lam_im, m_s5_log_dt, m_s5_b_re, m_s5_b_im, m_s5_c_re, m_s5_c_im, m_s5_d, m_s5_w_glu, m_a_norm_mix, m_ffn_norm, m_ffn_w_in, m_ffn_w_out, m_b_norm_mix, m_attn_w_q, m_attn_w_o, m_kv_norm, m_w_kv, m_final_norm, v_s5_lam_re, v_s5_lam_im, v_s5_log_dt, v_s5_b_re, v_s5_b_im, v_s5_c_re, v_s5_c_im, v_s5_d, v_s5_w_glu, v_a_norm_mix, v_ffn_norm, v_ffn_w_in, v_ffn_w_out, v_b_norm_mix, v_attn_w_q, v_attn_w_o, v_kv_norm, v_w_kv, v_final_norm):
    given = dict(x=x, s5_lam_re=s5_lam_re, s5_lam_im=s5_lam_im, s5_log_dt=s5_log_dt, s5_b_re=s5_b_re, s5_b_im=s5_b_im, s5_c_re=s5_c_re, s5_c_im=s5_c_im, s5_d=s5_d, s5_w_glu=s5_w_glu, a_norm_mix=a_norm_mix, ffn_norm=ffn_norm, ffn_w_in=ffn_w_in, ffn_w_out=ffn_w_out, b_norm_mix=b_norm_mix, attn_w_q=attn_w_q, attn_w_o=attn_w_o, kv_norm=kv_norm, w_kv=w_kv, final_norm=final_norm, loss_target=loss_target, m_s5_lam_re=m_s5_lam_re, m_s5_lam_im=m_s5_lam_im, m_s5_log_dt=m_s5_log_dt, m_s5_b_re=m_s5_b_re, m_s5_b_im=m_s5_b_im, m_s5_c_re=m_s5_c_re, m_s5_c_im=m_s5_c_im, m_s5_d=m_s5_d, m_s5_w_glu=m_s5_w_glu, m_a_norm_mix=m_a_norm_mix, m_ffn_norm=m_ffn_norm, m_ffn_w_in=m_ffn_w_in, m_ffn_w_out=m_ffn_w_out, m_b_norm_mix=m_b_norm_mix, m_attn_w_q=m_attn_w_q, m_attn_w_o=m_attn_w_o, m_kv_norm=m_kv_norm, m_w_kv=m_w_kv, m_final_norm=m_final_norm, v_s5_lam_re=v_s5_lam_re, v_s5_lam_im=v_s5_lam_im, v_s5_log_dt=v_s5_log_dt, v_s5_b_re=v_s5_b_re, v_s5_b_im=v_s5_b_im, v_s5_c_re=v_s5_c_re, v_s5_c_im=v_s5_c_im, v_s5_d=v_s5_d, v_s5_w_glu=v_s5_w_glu, v_a_norm_mix=v_a_norm_mix, v_ffn_norm=v_ffn_norm, v_ffn_w_in=v_ffn_w_in, v_ffn_w_out=v_ffn_w_out, v_b_norm_mix=v_b_norm_mix, v_attn_w_q=v_attn_w_q, v_attn_w_o=v_attn_w_o, v_kv_norm=v_kv_norm, v_w_kv=v_w_kv, v_final_norm=v_final_norm)
    weights = {n: given[n] for n in TWIN_WEIGHTS}
    shared = {n: given[n] for n in SHARED_INPUTS}
    per_example = {n: given[n] for n in ['x']}
    grad_fn = _jax.value_and_grad(_loss, argnums=(0, 1))

    def one_microbatch(ex, loss_target):
        ex = dict(ex)
        diff = ex.pop(TWIN_DIFF_INPUT)
        return grad_fn(weights, diff, {**shared, **ex}, loss_target)

    if N_MICROBATCH == 1:
        loss, (grad_w, grad_x) = one_microbatch(per_example, given["loss_target"])
    else:
        def body(carry, xs):
            loss_sum, grad_sum = carry
            l_k, (gw_k, gx_k) = one_microbatch(xs[0], xs[1])
            with _jax.named_scope("update"):
                return (loss_sum + l_k, _jax.tree.map(_jnp.add, grad_sum, gw_k)), gx_k

        init = (_jnp.zeros((), _jnp.float32), _jax.tree.map(_jnp.zeros_like, weights))
        (loss, grad_w), grad_x = _jax.lax.scan(body, init, (per_example, given["loss_target"]))
    with _jax.named_scope("update"):
        delta_w, new_m, new_v = {}, {}, {}
        for n in TWIN_WEIGHTS:
            delta_w[n], new_m[n], new_v[n] = _adamw(weights[n], grad_w[n], given["m_" + n], given["v_" + n])
    return (loss, grad_x, *[grad_w[n] for n in TWIN_WEIGHTS], *[delta_w[n] for n in TWIN_WEIGHTS],
            *[new_m[n] for n in TWIN_WEIGHTS], *[new_v[n] for n in TWIN_WEIGHTS])
```

```python
import functools
import math

import jax
import jax.numpy as jnp
from jax import lax
from jax.experimental import pallas as pl
from jax.experimental.pallas import tpu as pltpu

F32 = jnp.float32
BF16 = jnp.bfloat16

S5_GROUP_CH = 16
S5_STATE = 64
GROUPS_PER_TILE = 8
HEAD_DIM = 128
N_KV_HEADS = 4
PATTERNS = ((128, 1), (512, 4), (2048, 16))
ATT_BLK = 128
EPS = 1e-6
NEG_INF = -1e30
SCAN_T = 128
ADAM_LR = 0.001
ADAM_B1 = 0.9
ADAM_B2 = 0.999
ADAM_EPS = 1e-08
ADAM_WD = 0.01
ADAM_STEP = 10
N_CHIPS = 4
N_DEV = 8
VMEM_LIMIT = 56 * 1024 * 1024
MESH = pl.DeviceIdType.MESH
GELU_K = math.sqrt(2.0 / math.pi)
GELU_C = 0.044715


def _tile(n, pref, unit=128):
    if n <= pref:
        return n
    best = None
    t = unit
    while t <= pref:
        if n % t == 0:
            best = t
        t += unit
    assert best is not None, (n, pref, unit)
    return best


def _params(sem):
    return pltpu.CompilerParams(dimension_semantics=sem, vmem_limit_bytes=VMEM_LIMIT)


def _dot(a, b):
    return jnp.dot(a, b, preferred_element_type=F32)


def _dot_nt(a, b):
    return lax.dot_general(a, b, (((1,), (1,)), ((), ())), preferred_element_type=F32)


def _dot_tn(a, b):
    return lax.dot_general(a, b, (((0,), (0,)), ((), ())), preferred_element_type=F32)


def _mm_nn(a, w, out_dtype, name, tm=512, tn=1536, tk=2048):
    m, k = a.shape
    nb, k2, nq = w.shape
    assert k == k2
    tm, tn, tk = _tile(m, tm, 8), _tile(nq, tn), _tile(k, tk)
    per, nk = nq // tn, k // tk

    def kern(a_ref, w_ref, o_ref, *acc):
        p = _dot(a_ref[...], w_ref[...])
        if nk == 1:
            o_ref[...] = p.astype(o_ref.dtype)
        else:
            acc_ref, = acc
            kk = pl.program_id(2)

            @pl.when(kk == 0)
            def _():
                acc_ref[...] = p

            @pl.when(kk > 0)
            def _():
                acc_ref[...] += p

            @pl.when(kk == nk - 1)
            def _():
                o_ref[...] = acc_ref[...].astype(o_ref.dtype)

    return pl.pallas_call(
        kern, name=name, grid=(m // tm, nb * per, nk),
        in_specs=[pl.BlockSpec((tm, tk), lambda i, j, kk: (i, kk)),
                  pl.BlockSpec((None, tk, tn), lambda i, j, kk: (j // per, kk, j % per))],
        out_specs=pl.BlockSpec((tm, tn), lambda i, j, kk: (i, j)),
        out_shape=jax.ShapeDtypeStruct((m, nb * nq), out_dtype),
        scratch_shapes=[] if nk == 1 else [pltpu.VMEM((tm, tn), F32)],
        compiler_params=_params(("parallel", "parallel", "arbitrary")),
    )(a, w)


def _mm_nt(a, w, name, tm=512, tr=1536, tkc=2048):
    m, n = a.shape
    nb, k, nq = w.shape
    assert n == nb * nq
    tm, tr, tkc = _tile(m, tm, 8), _tile(nq, tr), _tile(k, tkc)
    per = nq // tr
    nr = nb * per

    def kern(a_ref, w_ref, o_ref, *acc):
        p = _dot_nt(a_ref[...], w_ref[...])
        if nr == 1:
            o_ref[...] = p
        else:
            acc_ref, = acc
            r = pl.program_id(2)

            @pl.when(r == 0)
            def _():
                acc_ref[...] = p

            @pl.when(r > 0)
            def _():
                acc_ref[...] += p

            @pl.when(r == nr - 1)
            def _():
                o_ref[...] = acc_ref[...]

    return pl.pallas_call(
        kern, name=name, grid=(m // tm, k // tkc, nr),
        in_specs=[pl.BlockSpec((tm, tr), lambda i, kc, r: (i, r)),
                  pl.BlockSpec((None, tkc, tr), lambda i, kc, r: (r // per, kc, r % per))],
        out_specs=pl.BlockSpec((tm, tkc), lambda i, kc, r: (i, kc)),
        out_shape=jax.ShapeDtypeStruct((m, k), F32),
        scratch_shapes=[] if nr == 1 else [pltpu.VMEM((tm, tkc), F32)],
        compiler_params=_params(("parallel", "parallel", "arbitrary")),
    )(a, w)


def _mm_tn(a, dy, nb, name, ts=512, tkk=1024, tn=1536):
    s, k = a.shape
    s2, n = dy.shape
    assert s == s2 and n % nb == 0
    nq = n // nb
    ts, tkk, tn = _tile(s, ts, 16), _tile(k, tkk), _tile(nq, tn)
    per, ns = nq // tn, s // ts

    def kern(a_ref, dy_ref, o_ref, acc_ref):
        si = pl.program_id(2)
        p = _dot_tn(a_ref[...], dy_ref[...])

        @pl.when(si == 0)
        def _():
            acc_ref[...] = p

        @pl.when(si > 0)
        def _():
            acc_ref[...] += p

        @pl.when(si == ns - 1)
        def _():
            o_ref[...] = acc_ref[...].astype(o_ref.dtype)

    return pl.pallas_call(
        kern, name=name, grid=(k // tkk, nb * per, ns),
        in_specs=[pl.BlockSpec((ts, tkk), lambda kk, j, si: (si, kk)),
                  pl.BlockSpec((ts, tn), lambda kk, j, si: (si, j))],
        out_specs=pl.BlockSpec((None, tkk, tn), lambda kk, j, si: (j // per, kk, j % per)),
        out_shape=jax.ShapeDtypeStruct((nb, k, nq), BF16),
        scratch_shapes=[pltpu.VMEM((tkk, tn), F32)],
        compiler_params=_params(("parallel", "parallel", "arbitrary")),
    )(a, dy)


ROWS = 256


def _rms(x, g):
    r = lax.rsqrt(jnp.mean(x * x, axis=-1, keepdims=True) + EPS)
    return x * r * g


def _rms_bwd(x, g, dh):
    r = lax.rsqrt(jnp.mean(x * x, axis=-1, keepdims=True) + EPS)
    xh = x * r
    dgx = dh * g
    dx = r * (dgx - xh * jnp.mean(dgx * xh, axis=-1, keepdims=True))
    return dx, dh * xh


def _sigmoid(x):
    return 1.0 / (1.0 + jnp.exp(-x))


def _gelu(y):
    return 0.5 * y * (1.0 + jnp.tanh(GELU_K * (y + GELU_C * y * y * y)))


def _gelu_grad(y):
    t = jnp.tanh(GELU_K * (y + GELU_C * y * y * y))
    return 0.5 * (1.0 + t) + 0.5 * y * (1.0 - t * t) * GELU_K * (1.0 + 3.0 * GELU_C * y * y)


def _row_spec(tm, d, col=0):
    return pl.BlockSpec((tm, d), lambda i: (i, col))


def _vec_spec(d):
    return pl.BlockSpec((1, d), lambda i: (0, 0))


def _acc_rows(ref, val, i):
    s = jnp.sum(val, axis=0, keepdims=True)

    @pl.when(i == 0)
    def _():
        ref[...] = s

    @pl.when(i > 0)
    def _():
        ref[...] += s


def _norm_f32(x, g, name):
    s, d = x.shape
    tm = _tile(s, ROWS, 8)

    def kern(x_ref, g_ref, h_ref):
        h_ref[...] = _rms(x_ref[...], g_ref[...])

    return pl.pallas_call(
        kern, name=name, grid=(s // tm,),
        in_specs=[_row_spec(tm, d), _vec_spec(d)],
        out_specs=_row_spec(tm, d),
        out_shape=jax.ShapeDtypeStruct((s, d), F32),
        compiler_params=_params(("parallel",)),
    )(x, g)


def _glu_res_norm(vg, x, g, name):
    s, d = x.shape
    tm = _tile(s, ROWS, 16)

    def kern(val_ref, gate_ref, x_ref, g_ref, x1_ref, hf_ref):
        x1 = x_ref[...] + val_ref[...] * _sigmoid(gate_ref[...])
        x1_ref[...] = x1
        hf_ref[...] = _rms(x1, g_ref[...]).astype(BF16)

    return pl.pallas_call(
        kern, name=name, grid=(s // tm,),
        in_specs=[_row_spec(tm, d, 0), _row_spec(tm, d, 1), _row_spec(tm, d), _vec_spec(d)],
        out_specs=[_row_spec(tm, d), _row_spec(tm, d)],
        out_shape=[jax.ShapeDtypeStruct((s, d), F32), jax.ShapeDtypeStruct((s, d), BF16)],
        compiler_params=_params(("parallel",)),
    )(vg, vg, x, g)


def _swiglu(gu, name):
    s, f2 = gu.shape
    f = f2 // 2
    tm, tc = _tile(s, ROWS, 16), _tile(f, 1024)
    nc = f // tc

    def kern(g_ref, u_ref, a_ref):
        g = g_ref[...]
        a_ref[...] = (g * _sigmoid(g) * u_ref[...]).astype(BF16)

    return pl.pallas_call(
        kern, name=name, grid=(s // tm, nc),
        in_specs=[pl.BlockSpec((tm, tc), lambda i, j: (i, j)),
                  pl.BlockSpec((tm, tc), lambda i, j: (i, j + nc))],
        out_specs=pl.BlockSpec((tm, tc), lambda i, j: (i, j)),
        out_shape=jax.ShapeDtypeStruct((s, f), BF16),
        compiler_params=_params(("parallel", "parallel")),
    )(gu, gu)


def _swiglu_bwd(da, gu, name):
    s, f = da.shape
    tm, tc = _tile(s, ROWS, 16), _tile(f, 1024)
    nc = f // tc

    def kern(da_ref, g_ref, u_ref, dg_ref, du_ref):
        g = g_ref[...]
        da_ = da_ref[...]
        sg = _sigmoid(g)
        dg_ref[...] = (da_ * u_ref[...] * sg * (1.0 + g * (1.0 - sg))).astype(BF16)
        du_ref[...] = (da_ * g * sg).astype(BF16)

    spec = pl.BlockSpec((tm, tc), lambda i, j: (i, j))
    return pl.pallas_call(
        kern, name=name, grid=(s // tm, nc),
        in_specs=[spec, spec, pl.BlockSpec((tm, tc), lambda i, j: (i, j + nc))],
        out_specs=[spec, spec],
        out_shape=[jax.ShapeDtypeStruct((s, f), BF16)] * 2,
        compiler_params=_params(("parallel", "parallel")),
    )(da, gu, gu)


def _res_norm(o, x, gains, name):
    s, d = x.shape
    tm = _tile(s, ROWS, 16)
    ng = len(gains)

    def kern(o_ref, x_ref, *refs):
        xn = x_ref[...] + o_ref[...]
        refs[ng][...] = xn
        for t in range(ng):
            refs[ng + 1 + t][...] = _rms(xn, refs[t][...]).astype(BF16)

    return pl.pallas_call(
        kern, name=name, grid=(s // tm,),
        in_specs=[_row_spec(tm, d), _row_spec(tm, d)] + [_vec_spec(d)] * ng,
        out_specs=[_row_spec(tm, d)] * (1 + ng),
        out_shape=[jax.ShapeDtypeStruct((s, d), F32)] + [jax.ShapeDtypeStruct((s, d), BF16)] * ng,
        compiler_params=_params(("parallel",)),
    )(o, x, *gains)


def _loss_head(o, x, g, target, name):
    s, d = x.shape
    tm = _tile(s, ROWS, 16)

    def kern(o_ref, x_ref, g_ref, t_ref, dx_ref, dxb_ref, loss_ref, dg_ref):
        i = pl.program_id(0)
        x4 = x_ref[...] + o_ref[...]
        gg = g_ref[...]
        diff = _rms(x4, gg) - t_ref[...]
        dx, dgr = _rms_bwd(x4, gg, diff * (1.0 / d))
        dx_ref[...] = dx
        dxb_ref[...] = dx.astype(BF16)
        _acc_rows(loss_ref, diff * diff * (0.5 / d), i)
        _acc_rows(dg_ref, dgr, i)

    return pl.pallas_call(
        kern, name=name, grid=(s // tm,),
        in_specs=[_row_spec(tm, d), _row_spec(tm, d), _vec_spec(d), _row_spec(tm, d)],
        out_specs=[_row_spec(tm, d), _row_spec(tm, d), _vec_spec(d), _vec_spec(d)],
        out_shape=[jax.ShapeDtypeStruct((s, d), F32), jax.ShapeDtypeStruct((s, d), BF16),
                   jax.ShapeDtypeStruct((1, d), F32), jax.ShapeDtypeStruct((1, d), F32)],
        compiler_params=_params(("arbitrary",)),
    )(o, x, g, target)


def _norm_bwd(x, gains, dhs, dres, name):
    s, d = x.shape
    tm = _tile(s, ROWS, 16)
    ng = len(gains)

    def kern(x_ref, dres_ref, *refs):
        i = pl.program_id(0)
        x_ = x_ref[...]
        acc = dres_ref[...]
        for t in range(ng):
            dx, dgr = _rms_bwd(x_, refs[t][...], refs[ng + t][...])
            acc = acc + dx
            _acc_rows(refs[2 * ng + 2 + t], dgr, i)
        refs[2 * ng][...] = acc
        refs[2 * ng + 1][...] = acc.astype(BF16)

    return pl.pallas_call(
        kern, name=name, grid=(s // tm,),
        in_specs=[_row_spec(tm, d), _row_spec(tm, d)] + [_vec_spec(d)] * ng + [_row_spec(tm, d)] * ng,
        out_specs=[_row_spec(tm, d), _row_spec(tm, d)] + [_vec_spec(d)] * ng,
        out_shape=[jax.ShapeDtypeStruct((s, d), F32), jax.ShapeDtypeStruct((s, d), BF16)]
        + [jax.ShapeDtypeStruct((1, d), F32)] * ng,
        compiler_params=_params(("arbitrary",)),
    )(x, dres, *gains, *dhs)


def _glu_bwd(dmix, vg, name):
    s, d = dmix.shape
    tm = _tile(s, ROWS, 16)

    def kern(dm_ref, val_ref, gate_ref, o_ref):
        dm = dm_ref[...]
        sg = _sigmoid(gate_ref[...])
        o_ref[:, :d] = (dm * sg).astype(BF16)
        o_ref[:, d:] = (dm * val_ref[...] * sg * (1.0 - sg)).astype(BF16)

    return pl.pallas_call(
        kern, name=name, grid=(s // tm,),
        in_specs=[_row_spec(tm, d), _row_spec(tm, d, 0), _row_spec(tm, d, 1)],
        out_specs=_row_spec(tm, 2 * d),
        out_shape=jax.ShapeDtypeStruct((s, 2 * d), BF16),
        compiler_params=_params(("parallel",)),
    )(dmix, vg, vg)


def _shift_down(v, sh, rows):
    return jnp.where(rows >= sh, pltpu.roll(v, sh, 0), 0.0)


def _shift_up(v, sh, rows, t):
    return jnp.where(rows < t - sh, pltpu.roll(v, t - sh, 0), 0.0)


def _s5_fwd(u, bbd_re, bbd_im, cbd_re, cbd_im, pw_re, pw_im, dskip, name):
    s, d = u.shape
    nt, cw, lw = bbd_re.shape
    t = pw_re.shape[0]
    nc = s // t

    def kern(u_ref, bre_ref, bim_ref, cre_ref, cim_ref, pre_ref, pim_ref, d_ref,
             xr_ref, xi_ref, y_ref, z_ref, car_ref, cai_ref):
        c = pl.program_id(1)

        @pl.when(c == 0)
        def _():
            car_ref[...] = jnp.zeros_like(car_ref)
            cai_ref[...] = jnp.zeros_like(cai_ref)

        u_ = u_ref[...]
        ub = u_.astype(BF16)
        xr = _dot(ub, bre_ref[...])
        xi = _dot(ub, bim_ref[...])
        rows = lax.broadcasted_iota(jnp.int32, (t, lw), 0)
        sh = 1
        while sh < t:
            ar = pre_ref[sh - 1:sh, :]
            ai = pim_ref[sh - 1:sh, :]
            sr = _shift_down(xr, sh, rows)
            si = _shift_down(xi, sh, rows)
            xr, xi = xr + ar * sr - ai * si, xi + ar * si + ai * sr
            sh *= 2
        cr = car_ref[...]
        ci = cai_ref[...]
        pr = pre_ref[...]
        pi = pim_ref[...]
        xr, xi = xr + pr * cr - pi * ci, xi + pr * ci + pi * cr
        car_ref[...] = xr[t - 1:t, :]
        cai_ref[...] = xi[t - 1:t, :]
        xr_ref[...] = xr
        xi_ref[...] = xi
        y = _dot(xr.astype(BF16), cre_ref[...]) - _dot(xi.astype(BF16), cim_ref[...]) + d_ref[...] * u_
        y_ref[...] = y
        z_ref[...] = _gelu(y).astype(BF16)

    tok = pl.BlockSpec((t, cw), lambda j, c: (c, j))
    st = pl.BlockSpec((t, lw), lambda j, c: (c, j))
    return pl.pallas_call(
        kern, name=name, grid=(nt, nc),
        in_specs=[tok,
                  pl.BlockSpec((None, cw, lw), lambda j, c: (j, 0, 0)),
                  pl.BlockSpec((None, cw, lw), lambda j, c: (j, 0, 0)),
                  pl.BlockSpec((None, lw, cw), lambda j, c: (j, 0, 0)),
                  pl.BlockSpec((None, lw, cw), lambda j, c: (j, 0, 0)),
                  pl.BlockSpec((t, lw), lambda j, c: (0, j)),
                  pl.BlockSpec((t, lw), lambda j, c: (0, j)),
                  pl.BlockSpec((1, cw), lambda j, c: (0, j))],
        out_specs=[st, st, tok, tok],
        out_shape=[jax.ShapeDtypeStruct((s, nt * lw), F32), jax.ShapeDtypeStruct((s, nt * lw), F32),
                   jax.ShapeDtypeStruct((s, d), F32), jax.ShapeDtypeStruct((s, d), BF16)],
        scratch_shapes=[pltpu.VMEM((1, lw), F32), pltpu.VMEM((1, lw), F32)],
        compiler_params=_params(("parallel", "arbitrary")),
    )(u, bbd_re, bbd_im, cbd_re, cbd_im, pw_re, pw_im, dskip)


def _s5_bwd(dz, y, u, xs_re, xs_im, bbd_re, bbd_im, cbd_re, cbd_im, pw_re, pw_im, pf_re, pf_im, dskip, name):
    s, d = u.shape
    nt, cw, lw = bbd_re.shape
    t = pw_re.shape[0]
    nc = s // t

    def kern(dz_ref, y_ref, u_ref, xr_ref, xi_ref, bre_ref, bim_ref, cre_ref, cim_ref,
             pre_ref, pim_ref, fre_ref, fim_ref, d_ref,
             du_ref, dd_ref, dcr_ref, dci_ref, dbr_ref, dbi_ref, dar_ref, dai_ref, car_ref, cai_ref):
        c = pl.program_id(1)

        @pl.when(c == 0)
        def _():
            car_ref[...] = jnp.zeros_like(car_ref)
            cai_ref[...] = jnp.zeros_like(cai_ref)

        u_ = u_ref[...]
        ub = u_.astype(BF16)
        dy = dz_ref[...] * _gelu_grad(y_ref[...])
        dyb = dy.astype(BF16)
        gr = _dot_nt(dyb, cre_ref[...])
        gi = -_dot_nt(dyb, cim_ref[...])
        rows = lax.broadcasted_iota(jnp.int32, (t, lw), 0)
        sh = 1
        while sh < t:
            ar = pre_ref[sh - 1:sh, :]
            ai = pim_ref[sh - 1:sh, :]
            sr = _shift_up(gr, sh, rows, t)
            si = _shift_up(gi, sh, rows, t)
            gr, gi = gr + ar * sr + ai * si, gi + ar * si - ai * sr
            sh *= 2
        cr = car_ref[...]
        ci = cai_ref[...]
        fr = fre_ref[...]
        fi = fim_ref[...]
        gr, gi = gr + fr * cr + fi * ci, gi + fr * ci - fi * cr
        gsr = jnp.where(rows < t - 1, pltpu.roll(gr, t - 1, 0), cr)
        gsi = jnp.where(rows < t - 1, pltpu.roll(gi, t - 1, 0), ci)
        car_ref[...] = gr[0:1, :]
        cai_ref[...] = gi[0:1, :]
        xr = xr_ref[...]
        xi = xi_ref[...]
        dar = jnp.sum(gsr * xr + gsi * xi, axis=0, keepdims=True)
        dai = jnp.sum(gsi * xr - gsr * xi, axis=0, keepdims=True)
        grb = gr.astype(BF16)
        gib = gi.astype(BF16)
        dbr = _dot_tn(ub, grb)
        dbi = _dot_tn(ub, gib)
        dcr = _dot_tn(dyb, xr.astype(BF16))
        dci = _dot_tn(dyb, xi.astype(BF16))
        du_ref[...] = dy * d_ref[...] + _dot_nt(grb, bre_ref[...]) + _dot_nt(gib, bim_ref[...])
        ddv = jnp.sum(dy * u_, axis=0, keepdims=True)

        @pl.when(c == 0)
        def _():
            dd_ref[...] = ddv
            dcr_ref[...] = dcr
            dci_ref[...] = dci
            dbr_ref[...] = dbr
            dbi_ref[...] = dbi
            dar_ref[...] = dar
            dai_ref[...] = dai

        @pl.when(c > 0)
        def _():
            dd_ref[...] += ddv
            dcr_ref[...] += dcr
            dci_ref[...] += dci
            dbr_ref[...] += dbr
            dbi_ref[...] += dbi
            dar_ref[...] += dar
            dai_ref[...] += dai

    tok = pl.BlockSpec((t, cw), lambda j, c: (nc - 1 - c, j))
    st = pl.BlockSpec((t, lw), lambda j, c: (nc - 1 - c, j))
    wb = pl.BlockSpec((None, cw, lw), lambda j, c: (j, 0, 0))
    wc = pl.BlockSpec((None, lw, cw), lambda j, c: (j, 0, 0))
    pw = pl.BlockSpec((t, lw), lambda j, c: (0, j))
    vec_c = pl.BlockSpec((1, cw), lambda j, c: (0, j))
    vec_l = pl.BlockSpec((1, lw), lambda j, c: (0, j))
    return pl.pallas_call(
        kern, name=name, grid=(nt, nc),
        in_specs=[tok, tok, tok, st, st, wb, wb, wc, wc, pw, pw, pw, pw, vec_c],
        out_specs=[tok, vec_c, wb, wb, wb, wb, vec_l, vec_l],
        out_shape=[jax.ShapeDtypeStruct((s, d), F32), jax.ShapeDtypeStruct((1, d), F32)]
        + [jax.ShapeDtypeStruct((nt, cw, lw), F32)] * 4
        + [jax.ShapeDtypeStruct((1, nt * lw), F32)] * 2,
        scratch_shapes=[pltpu.VMEM((1, lw), F32), pltpu.VMEM((1, lw), F32)],
        compiler_params=_params(("parallel", "arbitrary")),
    )(dz, y, u, xs_re, xs_im, bbd_re, bbd_im, cbd_re, cbd_im, pw_re, pw_im, pf_re, pf_im, dskip)


def _s5_discretize(lam_re, lam_im, log_dt, b_re, b_im):
    dt = jnp.exp(log_dt)[:, None]
    mag = jnp.exp(lam_re * dt)
    ang = lam_im * dt
    lb_re = mag * jnp.cos(ang)
    lb_im = mag * jnp.sin(ang)
    nr = lb_re - 1.0
    den = lam_re * lam_re + lam_im * lam_im
    f_re = (nr * lam_re + lb_im * lam_im) / den
    f_im = (lb_im * lam_re - nr * lam_im) / den
    bb_re = f_re[..., None] * b_re - f_im[..., None] * b_im
    bb_im = f_re[..., None] * b_im + f_im[..., None] * b_re
    return lb_re, lb_im, bb_re, bb_im


def _block_diag(w):
    nt, ng, a, b = w.shape
    eye = jnp.eye(ng, dtype=w.dtype)
    return (w[:, :, :, None, :] * eye[None, :, None, :, None]).reshape(nt, ng * a, ng * b)


def _block_diag_take(w, a, b):
    nt = w.shape[0]
    ng = GROUPS_PER_TILE
    w5 = w.reshape(nt, ng, a, ng, b)
    idx = jnp.arange(ng)
    return w5[:, idx, :, idx, :].transpose(1, 0, 2, 3).reshape(nt * ng, a, b)


def _att_masks(jb):
    qi = lax.broadcasted_iota(jnp.int32, (ATT_BLK, ATT_BLK), 0)
    si = lax.broadcasted_iota(jnp.int32, (ATT_BLK, ATT_BLK), 1)
    return si <= qi, (si >= qi) & (jb > 0)


def _att_fwd(q, kv, gi, dil, nh, name):
    s = q.shape[0]
    n = s // dil
    nb = n // ATT_BLK
    ng = q.shape[1] // (nh * HEAD_DIM)
    rep = nh // N_KV_HEADS
    qw = rep * HEAD_DIM
    scale = HEAD_DIM ** -0.5
    q2 = q.reshape(n, dil * q.shape[1])
    kv2 = kv.reshape(n, dil * kv.shape[1])

    def kern(q_ref, kc_ref, kp_ref, vc_ref, vp_ref, o_ref, l_ref):
        jb = pl.program_id(2)
        mc, mp = _att_masks(jb)
        kc, kp, vc, vp = kc_ref[...], kp_ref[...], vc_ref[...], vp_ref[...]
        for g in range(rep):
            cols = slice(g * HEAD_DIM, (g + 1) * HEAD_DIM)
            qg = q_ref[:, cols]
            sc = jnp.where(mc, _dot_nt(qg, kc) * scale, NEG_INF)
            sp = jnp.where(mp, _dot_nt(qg, kp) * scale, NEG_INF)
            m = jnp.maximum(jnp.max(sc, axis=-1, keepdims=True), jnp.max(sp, axis=-1, keepdims=True))
            pc = jnp.exp(sc - m)
            pp = jnp.exp(sp - m)
            l = jnp.sum(pc, axis=-1, keepdims=True) + jnp.sum(pp, axis=-1, keepdims=True)
            o_ref[:, cols] = _dot((pc / l).astype(BF16), vc) + _dot((pp / l).astype(BF16), vp)
            l_ref[:, cols] = jnp.broadcast_to(m + jnp.log(l), (ATT_BLK, HEAD_DIM))

    def kv_spec(off, prev):
        if prev:
            return pl.BlockSpec((ATT_BLK, HEAD_DIM), lambda r, h, jb: (jnp.maximum(jb - 1, 0), r * 2 * N_KV_HEADS + off + h))
        return pl.BlockSpec((ATT_BLK, HEAD_DIM), lambda r, h, jb: (jb, r * 2 * N_KV_HEADS + off + h))

    out_spec = pl.BlockSpec((ATT_BLK, qw), lambda r, h, jb: (jb, r * N_KV_HEADS + h))
    out, lse = pl.pallas_call(
        kern, name=name, grid=(dil, N_KV_HEADS, nb),
        in_specs=[pl.BlockSpec((ATT_BLK, qw), lambda r, h, jb: (jb, (r * ng + gi) * N_KV_HEADS + h)),
                  kv_spec(0, False), kv_spec(0, True), kv_spec(N_KV_HEADS, False), kv_spec(N_KV_HEADS, True)],
        out_specs=[out_spec, out_spec],
        out_shape=[jax.ShapeDtypeStruct((n, dil * nh * HEAD_DIM), F32)] * 2,
        compiler_params=_params(("parallel", "parallel", "arbitrary")),
    )(q2, kv2, kv2, kv2, kv2)
    return out.reshape(s, nh * HEAD_DIM), lse.reshape(s, nh * HEAD_DIM)


def _att_combine(outs, lses, name):
    s, d = outs[0].shape
    tm = _tile(s, ROWS, 16)
    ng = len(outs)

    def kern(*refs):
        ls = [refs[ng + t][...] for t in range(ng)]
        mx = functools.reduce(jnp.maximum, ls)
        es = [jnp.exp(l - mx) for l in ls]
        den = functools.reduce(lambda a, b: a + b, es)
        o = functools.reduce(lambda a, b: a + b, [es[t] / den * refs[t][...] for t in range(ng)])
        refs[2 * ng][...] = o
        refs[2 * ng + 1][...] = o.astype(BF16)

    return pl.pallas_call(
        kern, name=name, grid=(s // tm,),
        in_specs=[_row_spec(tm, d)] * (2 * ng),
        out_specs=[_row_spec(tm, d)] * 2,
        out_shape=[jax.ShapeDtypeStruct((s, d), F32), jax.ShapeDtypeStruct((s, d), BF16)],
        compiler_params=_params(("parallel",)),
    )(*outs, *lses)


def _att_bwd(q, kv, do, o, lses, gi, dil, nh, name):
    s = q.shape[0]
    n = s // dil
    nb = n // ATT_BLK
    ng = len(lses)
    rep = nh // N_KV_HEADS
    qw = rep * HEAD_DIM
    dm = nh * HEAD_DIM
    scale = HEAD_DIM ** -0.5
    q2 = q.reshape(n, dil * q.shape[1])
    kv2 = kv.reshape(n, dil * kv.shape[1])
    wide = [a.reshape(n, dil * dm) for a in (do, o, *lses)]

    def kern(q_ref, kc_ref, kp_ref, vc_ref, vp_ref, do_ref, o_ref, *refs):
        l_refs = refs[:ng]
        dq_ref, dk_ref, dv_ref, ck_ref, cv_ref = refs[ng:]
        jb = pl.program_id(2)

        @pl.when(jb == 0)
        def _():
            ck_ref[...] = jnp.zeros_like(ck_ref)
            cv_ref[...] = jnp.zeros_like(cv_ref)

        @pl.when(jb < nb)
        def _():
            mc, mp = _att_masks(jb)
            kc, kp, vc, vp = kc_ref[...], kp_ref[...], vc_ref[...], vp_ref[...]
            ls = [r[...] for r in l_refs]
            mx = functools.reduce(jnp.maximum, ls)
            den = functools.reduce(lambda a, b: a + b, [jnp.exp(l - mx) for l in ls])
            lse_g = ls[gi]
            w = jnp.exp(lse_g - mx) / den
            do_ = do_ref[...]
            dout = w * do_
            doo = do_ * o_ref[...]
            dkc = jnp.zeros((ATT_BLK, HEAD_DIM), F32)
            dkp = jnp.zeros((ATT_BLK, HEAD_DIM), F32)
            dvc = jnp.zeros((ATT_BLK, HEAD_DIM), F32)
            dvp = jnp.zeros((ATT_BLK, HEAD_DIM), F32)
            for g in range(rep):
                cols = slice(g * HEAD_DIM, (g + 1) * HEAD_DIM)
                qg = q_ref[:, cols]
                lg = lse_g[:, g * HEAD_DIM:g * HEAD_DIM + 1]
                ct = w[:, g * HEAD_DIM:g * HEAD_DIM + 1] * jnp.sum(doo[:, cols], axis=-1, keepdims=True)
                dob = dout[:, cols].astype(BF16)
                pc = jnp.exp(jnp.where(mc, _dot_nt(qg, kc) * scale, NEG_INF) - lg)
                pp = jnp.exp(jnp.where(mp, _dot_nt(qg, kp) * scale, NEG_INF) - lg)
                dsc = (pc * (_dot_nt(dob, vc) - ct) * scale).astype(BF16)
                dsp = (pp * (_dot_nt(dob, vp) - ct) * scale).astype(BF16)
                dq_ref[:, cols] = (_dot(dsc, kc) + _dot(dsp, kp)).astype(BF16)
                dkc = dkc + _dot_tn(dsc, qg)
                dkp = dkp + _dot_tn(dsp, qg)
                dvc = dvc + _dot_tn(pc.astype(BF16), dob)
                dvp = dvp + _dot_tn(pp.astype(BF16), dob)
            dk_ref[...] = ck_ref[...] + dkp
            dv_ref[...] = cv_ref[...] + dvp
            ck_ref[...] = dkc
            cv_ref[...] = dvc

        @pl.when(jb == nb)
        def _():
            dk_ref[...] = ck_ref[...]
            dv_ref[...] = cv_ref[...]

    def jq(jb):
        return jnp.minimum(jb, nb - 1)

    def kv_spec(off, prev):
        if prev:
            return pl.BlockSpec((ATT_BLK, HEAD_DIM), lambda r, h, jb: (jnp.maximum(jq(jb) - 1, 0), r * 2 * N_KV_HEADS + off + h))
        return pl.BlockSpec((ATT_BLK, HEAD_DIM), lambda r, h, jb: (jq(jb), r * 2 * N_KV_HEADS + off + h))

    wide_spec = pl.BlockSpec((ATT_BLK, qw), lambda r, h, jb: (jq(jb), r * N_KV_HEADS + h))
    dkv_spec = pl.BlockSpec((ATT_BLK, HEAD_DIM), lambda r, h, jb: (jnp.maximum(jb - 1, 0), r * N_KV_HEADS + h))
    n_q_groups = q.shape[1] // dm
    dq, dk, dv = pl.pallas_call(
        kern, name=name, grid=(dil, N_KV_HEADS, nb + 1),
        in_specs=[pl.BlockSpec((ATT_BLK, qw), lambda r, h, jb: (jq(jb), (r * n_q_groups + gi) * N_KV_HEADS + h)),
                  kv_spec(0, False), kv_spec(0, True), kv_spec(N_KV_HEADS, False), kv_spec(N_KV_HEADS, True)]
        + [wide_spec] * (2 + ng),
        out_specs=[wide_spec, dkv_spec, dkv_spec],
        out_shape=[jax.ShapeDtypeStruct((n, dil * dm), BF16),
                   jax.ShapeDtypeStruct((n, dil * N_KV_HEADS * HEAD_DIM), F32),
                   jax.ShapeDtypeStruct((n, dil * N_KV_HEADS * HEAD_DIM), F32)],
        scratch_shapes=[pltpu.VMEM((ATT_BLK, HEAD_DIM), F32), pltpu.VMEM((ATT_BLK, HEAD_DIM), F32)],
        compiler_params=_params(("parallel", "parallel", "arbitrary")),
    )(q2, kv2, kv2, kv2, kv2, *wide)
    return dq.reshape(s, dm), dk.reshape(s, -1), dv.reshape(s, -1)


def _dkv_sum(dks, dvs, name):
    s, w = dks[0].shape
    tm = _tile(s, ROWS, 16)
    ng = len(dks)

    def kern(*refs):
        o_ref = refs[2 * ng]
        o_ref[:, :w] = functools.reduce(lambda a, b: a + b, [refs[t][...] for t in range(ng)]).astype(BF16)
        o_ref[:, w:] = functools.reduce(lambda a, b: a + b, [refs[ng + t][...] for t in range(ng)]).astype(BF16)

    return pl.pallas_call(
        kern, name=name, grid=(s // tm,),
        in_specs=[_row_spec(tm, w)] * (2 * ng),
        out_specs=_row_spec(tm, 2 * w),
        out_shape=jax.ShapeDtypeStruct((s, 2 * w), BF16),
        compiler_params=_params(("parallel",)),
    )(*dks, *dvs)


def _ffn_fwd(hf, w_in, w_out, tag):
    gu = _mm_nn(hf, w_in, F32, f"{tag}_in", tn=1408)
    a = _swiglu(gu, f"{tag}_act")
    o = _mm_nn(a, w_out, F32, f"{tag}_out", tn=1024, tk=1408)
    return gu, a, o


def _ffn_bwd(dresb, hf, gu, a, w_in, w_out, tag):
    da = _mm_nt(dresb, w_out, f"{tag}_out_dx", tr=2048, tkc=1408)
    dw_out = _mm_tn(a, dresb, 1, f"{tag}_out_dw", tkk=1408, tn=1024)
    dgate, dup = _swiglu_bwd(da, gu, f"{tag}_act_bwd")
    dgu = jnp.concatenate([dgate, dup], axis=1)
    dhf = _mm_nt(dgu, w_in, f"{tag}_in_dx", tr=1408)
    dw_in = _mm_tn(hf, dgu, w_in.shape[0], f"{tag}_in_dw", tn=1408)
    return dhf, dw_in, dw_out


def _local_step(x, target, small, big):
    s, d = x.shape
    nh = d // HEAD_DIM
    n_groups = d // S5_GROUP_CH
    nt = n_groups // GROUPS_PER_TILE
    p_, c_ = S5_STATE, S5_GROUP_CH

    disc_in = (small["lam_re"], small["lam_im"], small["log_dt"], small["b_re"], small["b_im"])
    (lb_re, lb_im, bb_re, bb_im), disc_vjp = jax.vjp(_s5_discretize, *disc_in)
    del lb_re, lb_im
    dt = jnp.exp(small["log_dt"])[:, None]
    steps = jnp.arange(1, SCAN_T + 1, dtype=F32)[:, None, None]
    pmag = jnp.exp(steps * (small["lam_re"] * dt)[None])
    pang = steps * (small["lam_im"] * dt)[None]
    pw_re = (pmag * jnp.cos(pang)).reshape(SCAN_T, n_groups * p_)
    pw_im = (pmag * jnp.sin(pang)).reshape(SCAN_T, n_groups * p_)
    pf_re, pf_im = pw_re[::-1], pw_im[::-1]
    bbd_re = _block_diag(bb_re.transpose(0, 2, 1).reshape(nt, GROUPS_PER_TILE, c_, p_)).astype(BF16)
    bbd_im = _block_diag(bb_im.transpose(0, 2, 1).reshape(nt, GROUPS_PER_TILE, c_, p_)).astype(BF16)
    cbd_re = _block_diag(small["c_re"].transpose(0, 2, 1).reshape(nt, GROUPS_PER_TILE, p_, c_)).astype(BF16)
    cbd_im = _block_diag(small["c_im"].transpose(0, 2, 1).reshape(nt, GROUPS_PER_TILE, p_, c_)).astype(BF16)

    h0 = _norm_f32(x, small["a_norm"], "s5_norm")
    xs_re, xs_im, y0, z = _s5_fwd(h0, bbd_re, bbd_im, cbd_re, cbd_im, pw_re, pw_im, small["s5_d"], "s5_fwd")
    vg = _mm_nn(z, big["w_glu"], F32, "glu_mm", tn=1024)
    x1, hf0 = _glu_res_norm(vg, x, small["ffn_norm0"], "glu_res_norm")
    gu0, a0, o0 = _ffn_fwd(hf0, big["w_in0"], big["w_out0"], "ffn0")
    x2, kvn, h1 = _res_norm(o0, x1, [small["kv_norm"], small["b_norm"]], "ffn0_res_norm")
    kv = _mm_nn(kvn, big["w_kv"], BF16, "kv_mm", tn=1024)
    q = _mm_nn(h1, big["w_q"], BF16, "q_mm", tn=1536)
    outs, lses = [], []
    for gi, (window, dil) in enumerate(PATTERNS):
        assert window // dil == ATT_BLK
        og, lg = _att_fwd(q, kv, gi, dil, nh, f"att_fwd{gi}")
        outs.append(og)
        lses.append(lg)
    oatt, oattb = _att_combine(outs, lses, "att_combine")
    ao = _mm_nn(oattb, big["w_o"], F32, "o_mm", tn=1024)
    x3, hf1 = _res_norm(ao, x2, [small["ffn_norm1"]], "att_res_norm")
    gu1, a1, o1 = _ffn_fwd(hf1, big["w_in1"], big["w_out1"], "ffn1")
    dres, dresb, loss_rows, d_final = _loss_head(o1, x3, small["final_norm"], target, "loss_head")

    dhf1, d_w_in1, d_w_out1 = _ffn_bwd(dresb, hf1, gu1, a1, big["w_in1"], big["w_out1"], "ffn1")
    dres, dresb, d_ffn_norm1 = _norm_bwd(x3, [small["ffn_norm1"]], [dhf1], dres, "ffn1_norm_bwd")
    doatt = _mm_nt(dresb, big["w_o"], "o_dx", tr=2048)
    d_w_o = _mm_tn(oattb, dresb, 1, "o_dw", tn=1024)
    dqs, dks, dvs = [], [], []
    for gi, (window, dil) in enumerate(PATTERNS):
        dq_g, dk_g, dv_g = _att_bwd(q, kv, doatt, oatt, lses, gi, dil, nh, f"att_bwd{gi}")
        dqs.append(dq_g)
        dks.append(dk_g)
        dvs.append(dv_g)
    dq = jnp.concatenate(dqs, axis=1)
    dkv = _dkv_sum(dks, dvs, "dkv_sum")
    dh1 = _mm_nt(dq, big["w_q"], "q_dx", tr=1536)
    d_w_q = _mm_tn(h1, dq, big["w_q"].shape[0], "q_dw", tn=1536)
    dkvn = _mm_nt(dkv, big["w_kv"], "kv_dx", tr=1024)
    d_w_kv = _mm_tn(kvn, dkv, 1, "kv_dw", tn=1024)
    dres, dresb, d_b_norm, d_kv_norm = _norm_bwd(
        x2, [small["b_norm"], small["kv_norm"]], [dh1, dkvn], dres, "att_norm_bwd")
    dhf0, d_w_in0, d_w_out0 = _ffn_bwd(dresb, hf0, gu0, a0, big["w_in0"], big["w_out0"], "ffn0")
    dres, dresb, d_ffn_norm0 = _norm_bwd(x1, [small["ffn_norm0"]], [dhf0], dres, "ffn0_norm_bwd")
    del dresb
    dvg = _glu_bwd(dres, vg, "glu_bwd")
    dz = _mm_nt(dvg, big["w_glu"], "glu_dx", tr=1024)
    d_w_glu = _mm_tn(z, dvg, big["w_glu"].shape[0], "glu_dw", tn=1024)
    dh0, d_s5_d, dcr, dci_neg, dbr, dbi, dar, dai = _s5_bwd(
        dz, y0, h0, xs_re, xs_im, bbd_re, bbd_im, cbd_re, cbd_im, pw_re, pw_im, pf_re, pf_im,
        small["s5_d"], "s5_bwd")
    grad_x, _, d_a_norm = _norm_bwd(x, [small["a_norm"]], [dh0], dres, "s5_norm_bwd")

    d_bb_re = _block_diag_take(dbr, c_, p_).transpose(0, 2, 1)
    d_bb_im = _block_diag_take(dbi, c_, p_).transpose(0, 2, 1)
    d_c_re = _block_diag_take(dcr, c_, p_)
    d_c_im = -_block_diag_take(dci_neg, c_, p_)
    d_lam_re, d_lam_im, d_log_dt, d_b_re, d_b_im = disc_vjp(
        (dar.reshape(n_groups, p_), dai.reshape(n_groups, p_), d_bb_re, d_bb_im))

    d_small = dict(lam_re=d_lam_re, lam_im=d_lam_im, log_dt=d_log_dt, b_re=d_b_re, b_im=d_b_im,
                   c_re=d_c_re, c_im=d_c_im, s5_d=d_s5_d, a_norm=d_a_norm, ffn_norm0=d_ffn_norm0,
                   ffn_norm1=d_ffn_norm1, b_norm=d_b_norm, kv_norm=d_kv_norm, final_norm=d_final)
    d_big = dict(w_glu=d_w_glu, w_in0=d_w_in0, w_in1=d_w_in1, w_out0=d_w_out0, w_out1=d_w_out1,
                 w_q=d_w_q, w_o=d_w_o, w_kv=d_w_kv)
    return loss_rows, grad_x, d_small, d_big


def _place():
    x, y, c = lax.axis_index("x"), lax.axis_index("y"), lax.axis_index("c")
    return x, y, c, [(1 - x, y), (x, 1 - y), (1 - x, 1 - y)]


_ANY = pl.BlockSpec(memory_space=pl.ANY)


def _all_gather_chips(shards):
    n = len(shards)

    def body(*refs):
        ins, outs = refs[:n], refs[n:2 * n]
        send_sems, recv_sems, loc_sems = refs[2 * n:]
        x, y, c, peers = _place()
        me = 2 * x + y
        copies = []
        for a in range(n):
            mine = pltpu.make_async_copy(ins[a], outs[a].at[me], loc_sems.at[a])
            mine.start()
            copies.append(mine)
            for k, (px, py) in enumerate(peers):
                cp = pltpu.make_async_remote_copy(
                    src_ref=ins[a], dst_ref=outs[a].at[me], send_sem=send_sems.at[3 * a + k],
                    recv_sem=recv_sems.at[3 * a + k], device_id=(px, py, c), device_id_type=MESH)
                cp.start()
                copies.append(cp)
        for cp in copies:
            cp.wait()

    return pl.pallas_call(
        body, name="all_gather_chips",
        in_specs=[_ANY] * n, out_specs=[_ANY] * n,
        out_shape=[jax.ShapeDtypeStruct((N_CHIPS,) + a.shape, a.dtype) for a in shards],
        scratch_shapes=[pltpu.SemaphoreType.DMA((3 * n,)), pltpu.SemaphoreType.DMA((3 * n,)),
                        pltpu.SemaphoreType.DMA((n,))],
    )(*shards)


def _scatter_to_chips(grads):
    n = len(grads)

    def body(*refs):
        ins, outs = refs[:n], refs[n:2 * n]
        send_sems, recv_sems = refs[2 * n:]
        x, y, c, peers = _place()
        copies = []
        for a in range(n):
            for k, (px, py) in enumerate(peers):
                cp = pltpu.make_async_remote_copy(
                    src_ref=ins[a].at[2 * px + py], dst_ref=outs[a].at[k], send_sem=send_sems.at[3 * a + k],
                    recv_sem=recv_sems.at[3 * a + k], device_id=(px, py, c), device_id_type=MESH)
                cp.start()
                copies.append(cp)
        for cp in copies:
            cp.wait()

    return pl.pallas_call(
        body, name="scatter_to_chips",
        in_specs=[_ANY] * n, out_specs=[_ANY] * n,
        out_shape=[jax.ShapeDtypeStruct((3,) + a.shape[1:], a.dtype) for a in grads],
        scratch_shapes=[pltpu.SemaphoreType.DMA((3 * n,)), pltpu.SemaphoreType.DMA((3 * n,))],
    )(*grads)


def _swap_cores(parts):
    n = len(parts)

    def body(*refs):
        ins, outs = refs[:n], refs[n:2 * n]
        send_sems, recv_sems = refs[2 * n:]
        x, y, c = lax.axis_index("x"), lax.axis_index("y"), lax.axis_index("c")
        copies = []
        for a in range(n):
            cp = pltpu.make_async_remote_copy(
                src_ref=ins[a], dst_ref=outs[a], send_sem=send_sems.at[a], recv_sem=recv_sems.at[a],
                device_id=(x, y, 1 - c), device_id_type=MESH)
            cp.start()
            copies.append(cp)
        for cp in copies:
            cp.wait()

    return pl.pallas_call(
        body, name="swap_cores",
        in_specs=[_ANY] * n, out_specs=[_ANY] * n,
        out_shape=[jax.ShapeDtypeStruct(a.shape, a.dtype) for a in parts],
        scratch_shapes=[pltpu.SemaphoreType.DMA((n,)), pltpu.SemaphoreType.DMA((n,))],
    )(*parts)


def _all_reduce_small(v):
    nd, r, w = v.shape
    assert nd == N_DEV

    def body(v_ref, out_ref, land_ref, red_ref, send1, recv1, send2, recv2):
        x, y, c = lax.axis_index("x"), lax.axis_index("y"), lax.axis_index("c")
        me = 4 * x + 2 * y + c
        peers = []
        for k in range(1, N_DEV):
            kx, ky, kc = (k >> 2) & 1, (k >> 1) & 1, k & 1
            peers.append((1 - x if kx else x, 1 - y if ky else y, 1 - c if kc else c))
        first = []
        for k, (px, py, pc) in enumerate(peers):
            cp = pltpu.make_async_remote_copy(
                src_ref=v_ref.at[4 * px + 2 * py + pc], dst_ref=land_ref.at[me], send_sem=send1.at[k],
                recv_sem=recv1.at[k], device_id=(px, py, pc), device_id_type=MESH)
            cp.start()
            first.append(cp)
        land_ref[me] = v_ref[me]
        for cp in first:
            cp.wait()
        acc = land_ref[0]
        for j in range(1, N_DEV):
            acc = acc + land_ref[j]
        red_ref[...] = acc
        second = []
        for k, (px, py, pc) in enumerate(peers):
            cp = pltpu.make_async_remote_copy(
                src_ref=red_ref, dst_ref=out_ref.at[me], send_sem=send2.at[k],
                recv_sem=recv2.at[k], device_id=(px, py, pc), device_id_type=MESH)
            cp.start()
            second.append(cp)
        out_ref[me] = acc
        for cp in second:
            cp.wait()

    vmem = pl.BlockSpec(memory_space=pltpu.VMEM)
    return pl.pallas_call(
        body, name="all_reduce_small",
        in_specs=[vmem], out_specs=vmem,
        out_shape=jax.ShapeDtypeStruct((nd, r, w), F32),
        scratch_shapes=[pltpu.VMEM((nd, r, w), F32), pltpu.VMEM((r, w), F32)]
        + [pltpu.SemaphoreType.DMA((N_DEV - 1,))] * 4,
        compiler_params=pltpu.CompilerParams(vmem_limit_bytes=VMEM_LIMIT),
    )(v)


def _adam_math(w, g, m, v):
    m = ADAM_B1 * m + (1.0 - ADAM_B1) * g
    v = ADAM_B2 * v + (1.0 - ADAM_B2) * (g * g)
    m_hat = m / (1.0 - ADAM_B1 ** ADAM_STEP)
    v_hat = v / (1.0 - ADAM_B2 ** ADAM_STEP)
    delta = -ADAM_LR * (m_hat / (jnp.sqrt(v_hat) + ADAM_EPS) + ADAM_WD * w)
    return delta, m, v


def _sum_blocks(own, got, chip, name):
    _, r, c = own.shape
    tm = _tile(r, ROWS, 16)

    def kern(chip_ref, own_ref, got_ref, o_ref):
        del chip_ref
        acc = own_ref[...].astype(F32)
        for k in range(3):
            acc = acc + got_ref[k].astype(F32)
        o_ref[...] = acc

    return pl.pallas_call(
        kern, name=name,
        grid_spec=pltpu.PrefetchScalarGridSpec(
            num_scalar_prefetch=1, grid=(r // tm,),
            in_specs=[pl.BlockSpec((None, tm, c), lambda i, ch: (ch[0], i, 0)),
                      pl.BlockSpec((3, tm, c), lambda i, ch: (0, i, 0))],
            out_specs=pl.BlockSpec((tm, c), lambda i, ch: (i, 0))),
        out_shape=jax.ShapeDtypeStruct((r, c), F32),
        compiler_params=_params(("parallel",)),
    )(chip, own, got)


def _adamw(p0, p1, w, m, v, name):
    r, c = w.shape
    tm = _tile(r, 128, 8)

    def kern(p0_ref, p1_ref, w_ref, m_ref, v_ref, g_ref, d_ref, mo_ref, vo_ref):
        g = p0_ref[...] + p1_ref[...]
        g_ref[...] = g
        d_ref[...], mo_ref[...], vo_ref[...] = _adam_math(w_ref[...], g, m_ref[...], v_ref[...])

    spec = pl.BlockSpec((tm, c), lambda i: (i, 0))
    return pl.pallas_call(
        kern, name=name, grid=(r // tm,),
        in_specs=[spec] * 5, out_specs=[spec] * 4,
        out_shape=[jax.ShapeDtypeStruct((r, c), F32)] * 4,
        compiler_params=_params(("parallel",)),
    )(p0, p1, w, m, v)


def _adamw_small(g, w, m, v, name):
    def kern(g_ref, w_ref, m_ref, v_ref, d_ref, mo_ref, vo_ref):
        d_ref[...], mo_ref[...], vo_ref[...] = _adam_math(w_ref[...], g_ref[...], m_ref[...], v_ref[...])

    return pl.pallas_call(
        kern, name=name,
        out_shape=[jax.ShapeDtypeStruct(g.shape, F32)] * 3,
        compiler_params=pltpu.CompilerParams(vmem_limit_bytes=VMEM_LIMIT),
    )(g, w, m, v)


def _pack(arrays, rows):
    flat = jnp.concatenate([a.reshape(-1).astype(F32) for a in arrays])
    return jnp.pad(flat, (0, rows * 128 - flat.shape[0])).reshape(rows, 128)


def _unpack(packed, shapes):
    flat = packed.reshape(-1)
    out, off = [], 0
    for shp in shapes:
        size = math.prod(shp)
        out.append(flat[off:off + size].reshape(shp))
        off += size
    return out


_REPLICATED = ["s5_lam_re", "s5_lam_im", "s5_log_dt", "s5_b_re", "s5_b_im", "s5_c_re", "s5_c_im",
               "ffn_norm", "b_norm_mix", "kv_norm", "final_norm"]
_CHIP_VECTORS = ["s5_d", "a_norm_mix"]
_BIG = ["s5_w_glu", "ffn_w_in", "ffn_w_out", "attn_w_q", "attn_w_o", "w_kv"]
_WEIGHT_ORDER = ["s5_lam_re", "s5_lam_im", "s5_log_dt", "s5_b_re", "s5_b_im", "s5_c_re", "s5_c_im", "s5_d",
                 "s5_w_glu", "a_norm_mix", "ffn_norm", "ffn_w_in", "ffn_w_out", "b_norm_mix", "attn_w_q",
                 "attn_w_o", "kv_norm", "w_kv", "final_norm"]


def _step(x, loss_target, w, m, v):
    s, d = x.shape[1], x.shape[2]
    chip = 2 * lax.axis_index("x") + lax.axis_index("y")

    col_sharded = dict(w_glu=w["s5_w_glu"][0], w_in0=w["ffn_w_in"][0], w_in1=w["ffn_w_in"][1], w_q=w["attn_w_q"][0])
    row_sharded = dict(w_out0=w["ffn_w_out"][0], w_out1=w["ffn_w_out"][1], w_o=w["attn_w_o"][0], w_kv=w["w_kv"])
    names = list(col_sharded) + list(row_sharded)
    shards = [a.astype(BF16) for a in col_sharded.values()] + [a.astype(BF16) for a in row_sharded.values()]
    shards += [w["s5_d"], w["a_norm_mix"]]
    gathered = _all_gather_chips(shards)
    big = {}
    for name, g in zip(names, gathered):
        big[name] = g if name in col_sharded else g.reshape(1, -1, g.shape[-1])
    s5_d_full = gathered[-2].reshape(1, d)
    a_norm_full = gathered[-1].reshape(1, d)

    small = dict(lam_re=w["s5_lam_re"][0], lam_im=w["s5_lam_im"][0], log_dt=w["s5_log_dt"][0],
                 b_re=w["s5_b_re"][0], b_im=w["s5_b_im"][0], c_re=w["s5_c_re"][0], c_im=w["s5_c_im"][0],
                 s5_d=s5_d_full, a_norm=a_norm_full, ffn_norm0=w["ffn_norm"][0:1], ffn_norm1=w["ffn_norm"][1:2],
                 b_norm=w["b_norm_mix"], kv_norm=w["kv_norm"].reshape(1, d), final_norm=w["final_norm"].reshape(1, d))

    loss_rows, grad_x, d_small, d_big = _local_step(x[0], loss_target[0], small, big)

    g4 = [d_big[n] if n in col_sharded else d_big[n].reshape(N_CHIPS, -1, d_big[n].shape[-1]) for n in names]
    g4 = [g.reshape(N_CHIPS, -1, g.shape[-1]) for g in g4]
    got = _scatter_to_chips(g4)
    chip_arr = jnp.reshape(chip, (1,)).astype(jnp.int32)
    partial = [_sum_blocks(o, r, chip_arr, f"sum_{n}") for n, o, r in zip(names, g4, got)]
    other = _swap_cores(partial)
    part = dict(zip(names, zip(partial, other)))

    def two_layers(name0, name1):
        (a0, b0), (a1, b1) = part[name0], part[name1]
        return jnp.concatenate([a0, a1], axis=0), jnp.concatenate([b0, b1], axis=0)

    big_parts = {"s5_w_glu": part["w_glu"], "ffn_w_in": two_layers("w_in0", "w_in1"),
                 "ffn_w_out": two_layers("w_out0", "w_out1"), "attn_w_q": part["w_q"],
                 "attn_w_o": part["w_o"], "w_kv": part["w_kv"]}
    result = {}
    for name in _BIG:
        p0, p1 = big_parts[name]
        shape = w[name].shape
        flat = lambda a: a.reshape(-1, shape[-1])
        outs = _adamw(p0, p1, flat(w[name]), flat(m[name]), flat(v[name]), f"adamw_{name}")
        result[name] = [o.reshape(shape) for o in outs]

    rep_grads = [d_small["lam_re"], d_small["lam_im"], d_small["log_dt"], d_small["b_re"], d_small["b_im"],
                 d_small["c_re"], d_small["c_im"],
                 jnp.concatenate([d_small["ffn_norm0"], d_small["ffn_norm1"]], axis=0),
                 d_small["b_norm"], d_small["kv_norm"], d_small["final_norm"]]
    vec_grads = [d_small["s5_d"], d_small["a_norm"]]
    to_reduce = rep_grads + vec_grads + [jnp.sum(loss_rows).reshape(1)]
    total = sum(math.prod(a.shape) for a in to_reduce)
    rows_per = -(-total // (N_DEV * 128 * 8)) * 8
    reduced = _all_reduce_small(_pack(to_reduce, N_DEV * rows_per).reshape(N_DEV, rows_per, 128))
    red = _unpack(reduced, [a.shape for a in to_reduce])
    loss = red[-1][0]
    g_small = dict(zip(_REPLICATED, [r.reshape(w[n].shape) for r, n in zip(red[:len(rep_grads)], _REPLICATED)]))
    for n, r in zip(_CHIP_VECTORS, red[len(rep_grads):-1]):
        g_small[n] = lax.dynamic_slice_in_dim(r.reshape(1, d), chip * (d // N_CHIPS), d // N_CHIPS, axis=1)
    small_names = _REPLICATED + _CHIP_VECTORS
    n_small = sum(math.prod(w[n].shape) for n in small_names)
    rows_small = -(-n_small // (128 * 8)) * 8
    packed = [_pack([src[n] for n in small_names], rows_small) for src in (g_small, w, m, v)]
    upd = _adamw_small(*packed, "adamw_small")
    shapes = [w[n].shape for n in small_names]
    for n, dl, mo, vo in zip(small_names, *[_unpack(u, shapes) for u in upd]):
        result[n] = [g_small[n], dl, mo, vo]

    cols = [[result[n][t] for n in _WEIGHT_ORDER] for t in range(4)]
    return (loss, grad_x.reshape(x.shape), *cols[0], *cols[1], *cols[2], *cols[3])


def kernel(x, s5_lam_re, s5_lam_im, s5_log_dt, s5_b_re, s5_b_im, s5_c_re, s5_c_im, s5_d, s5_w_glu, a_norm_mix, ffn_norm, ffn_w_in, ffn_w_out, b_norm_mix, attn_w_q, attn_w_o, kv_norm, w_kv, final_norm, loss_target, m_s5_lam_re, m_s5_lam_im, m_s5_log_dt, m_s5_b_re, m_s5_b_im, m_s5_c_re, m_s5_c_im, m_s5_d, m_s5_w_glu, m_a_norm_mix, m_ffn_norm, m_ffn_w_in, m_ffn_w_out, m_b_norm_mix, m_attn_w_q, m_attn_w_o, m_kv_norm, m_w_kv, m_final_norm, v_s5_lam_re, v_s5_lam_im, v_s5_log_dt, v_s5_b_re, v_s5_b_im, v_s5_c_re, v_s5_c_im, v_s5_d, v_s5_w_glu, v_a_norm_mix, v_ffn_norm, v_ffn_w_in, v_ffn_w_out, v_b_norm_mix, v_attn_w_q, v_attn_w_o, v_kv_norm, v_w_kv, v_final_norm):
    w = dict(zip(_WEIGHT_ORDER, (s5_lam_re, s5_lam_im, s5_log_dt, s5_b_re, s5_b_im, s5_c_re, s5_c_im, s5_d, s5_w_glu, a_norm_mix, ffn_norm, ffn_w_in, ffn_w_out, b_norm_mix, attn_w_q, attn_w_o, kv_norm, w_kv, final_norm)))
    m = dict(zip(_WEIGHT_ORDER, (m_s5_lam_re, m_s5_lam_im, m_s5_log_dt, m_s5_b_re, m_s5_b_im, m_s5_c_re, m_s5_c_im, m_s5_d, m_s5_w_glu, m_a_norm_mix, m_ffn_norm, m_ffn_w_in, m_ffn_w_out, m_b_norm_mix, m_attn_w_q, m_attn_w_o, m_kv_norm, m_w_kv, m_final_norm)))
    v = dict(zip(_WEIGHT_ORDER, (v_s5_lam_re, v_s5_lam_im, v_s5_log_dt, v_s5_b_re, v_s5_b_im, v_s5_c_re, v_s5_c_im, v_s5_d, v_s5_w_glu, v_a_norm_mix, v_ffn_norm, v_ffn_w_in, v_ffn_w_out, v_b_norm_mix, v_attn_w_q, v_attn_w_o, v_kv_norm, v_w_kv, v_final_norm)))
    return _step(x, loss_target, w, m, v)
```

```python
import functools
import math

import jax
import jax.numpy as jnp
from jax import lax
from jax.experimental import pallas as pl
from jax.experimental.pallas import tpu as pltpu

F32 = jnp.float32
BF16 = jnp.bfloat16

S5_GROUP_CH = 16
S5_STATE = 64
GROUPS_PER_TILE = 8
HEAD_DIM = 128
N_KV_HEADS = 4
PATTERNS = ((128, 1), (512, 4), (2048, 16))
ATT_BLK = 128
EPS = 1e-6
NEG_INF = -1e30
SCAN_T = 128
ADAM_LR = 0.001
ADAM_B1 = 0.9
ADAM_B2 = 0.999
ADAM_EPS = 1e-08
ADAM_WD = 0.01
ADAM_STEP = 10
N_CHIPS = 4
N_DEV = 8
VMEM_LIMIT = 56 * 1024 * 1024
MESH = pl.DeviceIdType.MESH
GELU_K = math.sqrt(2.0 / math.pi)
GELU_C = 0.044715


def _tile(n, pref, unit=128):
    if n <= pref:
        return n
    best = None
    t = unit
    while t <= pref:
        if n % t == 0:
            best = t
        t += unit
    assert best is not None, (n, pref, unit)
    return best


def _params(sem):
    return pltpu.CompilerParams(dimension_semantics=sem, vmem_limit_bytes=VMEM_LIMIT)


def _dot(a, b):
    return jnp.dot(a, b, preferred_element_type=F32)


def _dot_nt(a, b):
    return lax.dot_general(a, b, (((1,), (1,)), ((), ())), preferred_element_type=F32)


def _dot_tn(a, b):
    return lax.dot_general(a, b, (((0,), (0,)), ((), ())), preferred_element_type=F32)


def _mm_nn(a, w, out_dtype, name, tm=512, tn=1536, tk=2048):
    m, k = a.shape
    nb, k2, nq = w.shape
    assert k == k2
    tm, tn, tk = _tile(m, tm, 8), _tile(nq, tn), _tile(k, tk)
    per, nk = nq // tn, k // tk

    def kern(a_ref, w_ref, o_ref, *acc):
        p = _dot(a_ref[...], w_ref[...])
        if nk == 1:
            o_ref[...] = p.astype(o_ref.dtype)
        else:
            acc_ref, = acc
            kk = pl.program_id(2)

            @pl.when(kk == 0)
            def _():
                acc_ref[...] = p

            @pl.when(kk > 0)
            def _():
                acc_ref[...] += p

            @pl.when(kk == nk - 1)
            def _():
                o_ref[...] = acc_ref[...].astype(o_ref.dtype)

    return pl.pallas_call(
        kern, name=name, grid=(m // tm, nb * per, nk),
        in_specs=[pl.BlockSpec((tm, tk), lambda i, j, kk: (i, kk)),
                  pl.BlockSpec((None, tk, tn), lambda i, j, kk: (j // per, kk, j % per))],
        out_specs=pl.BlockSpec((tm, tn), lambda i, j, kk: (i, j)),
        out_shape=jax.ShapeDtypeStruct((m, nb * nq), out_dtype),
        scratch_shapes=[] if nk == 1 else [pltpu.VMEM((tm, tn), F32)],
        compiler_params=_params(("parallel", "parallel", "arbitrary")),
    )(a, w)


def _mm_nt(a, w, name, tm=512, tr=1536, tkc=2048):
    m, n = a.shape
    nb, k, nq = w.shape
    assert n == nb * nq
    tm, tr, tkc = _tile(m, tm, 8), _tile(nq, tr), _tile(k, tkc)
    per = nq // tr
    nr = nb * per

    def kern(a_ref, w_ref, o_ref, *acc):
        p = _dot_nt(a_ref[...], w_ref[...])
        if nr == 1:
            o_ref[...] = p
        else:
            acc_ref, = acc
            r = pl.program_id(2)

            @pl.when(r == 0)
            def _():
                acc_ref[...] = p

            @pl.when(r > 0)
            def _():
                acc_ref[...] += p

            @pl.when(r == nr - 1)
            def _():
                o_ref[...] = acc_ref[...]

    return pl.pallas_call(
        kern, name=name, grid=(m // tm, k // tkc, nr),
        in_specs=[pl.BlockSpec((tm, tr), lambda i, kc, r: (i, r)),
                  pl.BlockSpec((None, tkc, tr), lambda i, kc, r: (r // per, kc, r % per))],
        out_specs=pl.BlockSpec((tm, tkc), lambda i, kc, r: (i, kc)),
        out_shape=jax.ShapeDtypeStruct((m, k), F32),
        scratch_shapes=[] if nr == 1 else [pltpu.VMEM((tm, tkc), F32)],
        compiler_params=_params(("parallel", "parallel", "arbitrary")),
    )(a, w)


def _mm_tn(a, dy, nb, name, ts=512, tkk=1024, tn=1536):
    s, k = a.shape
    s2, n = dy.shape
    assert s == s2 and n % nb == 0
    nq = n // nb
    ts, tkk, tn = _tile(s, ts, 16), _tile(k, tkk), _tile(nq, tn)
    per, ns = nq // tn, s // ts

    def kern(a_ref, dy_ref, o_ref, acc_ref):
        si = pl.program_id(2)
        p = _dot_tn(a_ref[...], dy_ref[...])

        @pl.when(si == 0)
        def _():
            acc_ref[...] = p

        @pl.when(si > 0)
        def _():
            acc_ref[...] += p

        @pl.when(si == ns - 1)
        def _():
            o_ref[...] = acc_ref[...].astype(o_ref.dtype)

    return pl.pallas_call(
        kern, name=name, grid=(k // tkk, nb * per, ns),
        in_specs=[pl.BlockSpec((ts, tkk), lambda kk, j, si: (si, kk)),
                  pl.BlockSpec((ts, tn), lambda kk, j, si: (si, j))],
        out_specs=pl.BlockSpec((None, tkk, tn), lambda kk, j, si: (j // per, kk, j % per)),
        out_shape=jax.ShapeDtypeStruct((nb, k, nq), BF16),
        scratch_shapes=[pltpu.VMEM((tkk, tn), F32)],
        compiler_params=_params(("parallel", "parallel", "arbitrary")),
    )(a, dy)


ROWS = 256


def _rms(x, g):
    r = lax.rsqrt(jnp.mean(x * x, axis=-1, keepdims=True) + EPS)
    return x * r * g


def _rms_bwd(x, g, dh):
    r = lax.rsqrt(jnp.mean(x * x, axis=-1, keepdims=True) + EPS)
    xh = x * r
    dgx = dh * g
    dx = r * (dgx - xh * jnp.mean(dgx * xh, axis=-1, keepdims=True))
    return dx, dh * xh


def _sigmoid(x):
    return 1.0 / (1.0 + jnp.exp(-x))


def _gelu(y):
    return 0.5 * y * (1.0 + jnp.tanh(GELU_K * (y + GELU_C * y * y * y)))


def _gelu_grad(y):
    t = jnp.tanh(GELU_K * (y + GELU_C * y * y * y))
    return 0.5 * (1.0 + t) + 0.5 * y * (1.0 - t * t) * GELU_K * (1.0 + 3.0 * GELU_C * y * y)


def _row_spec(tm, d, col=0):
    return pl.BlockSpec((tm, d), lambda i: (i, col))


def _vec_spec(d):
    return pl.BlockSpec((1, d), lambda i: (0, 0))


def _acc_rows(ref, val, i):
    s = jnp.sum(val, axis=0, keepdims=True)

    @pl.when(i == 0)
    def _():
        ref[...] = s

    @pl.when(i > 0)
    def _():
        ref[...] += s


def _norm_f32(x, g, name):
    s, d = x.shape
    tm = _tile(s, ROWS, 8)

    def kern(x_ref, g_ref, h_ref):
        h_ref[...] = _rms(x_ref[...], g_ref[...])

    return pl.pallas_call(
        kern, name=name, grid=(s // tm,),
        in_specs=[_row_spec(tm, d), _vec_spec(d)],
        out_specs=_row_spec(tm, d),
        out_shape=jax.ShapeDtypeStruct((s, d), F32),
        compiler_params=_params(("parallel",)),
    )(x, g)


def _glu_res_norm(vg, x, g, name):
    s, d = x.shape
    tm = _tile(s, ROWS, 16)

    def kern(val_ref, gate_ref, x_ref, g_ref, x1_ref, hf_ref):
        x1 = x_ref[...] + val_ref[...] * _sigmoid(gate_ref[...])
        x1_ref[...] = x1
        hf_ref[...] = _rms(x1, g_ref[...]).astype(BF16)

    return pl.pallas_call(
        kern, name=name, grid=(s // tm,),
        in_specs=[_row_spec(tm, d, 0), _row_spec(tm, d, 1), _row_spec(tm, d), _vec_spec(d)],
        out_specs=[_row_spec(tm, d), _row_spec(tm, d)],
        out_shape=[jax.ShapeDtypeStruct((s, d), F32), jax.ShapeDtypeStruct((s, d), BF16)],
        compiler_params=_params(("parallel",)),
    )(vg, vg, x, g)


def _swiglu(gu, name):
    s, f2 = gu.shape
    f = f2 // 2
    tm, tc = _tile(s, ROWS, 16), _tile(f, 1024)
    nc = f // tc

    def kern(g_ref, u_ref, a_ref):
        g = g_ref[...]
        a_ref[...] = (g * _sigmoid(g) * u_ref[...]).astype(BF16)

    return pl.pallas_call(
        kern, name=name, grid=(s // tm, nc),
        in_specs=[pl.BlockSpec((tm, tc), lambda i, j: (i, j)),
                  pl.BlockSpec((tm, tc), lambda i, j: (i, j + nc))],
        out_specs=pl.BlockSpec((tm, tc), lambda i, j: (i, j)),
        out_shape=jax.ShapeDtypeStruct((s, f), BF16),
        compiler_params=_params(("parallel", "parallel")),
    )(gu, gu)


def _swiglu_bwd(da, gu, name):
    s, f = da.shape
    tm, tc = _tile(s, ROWS, 16), _tile(f, 1024)
    nc = f // tc

    def kern(da_ref, g_ref, u_ref, dg_ref, du_ref):
        g = g_ref[...]
        da_ = da_ref[...]
        sg = _sigmoid(g)
        dg_ref[...] = (da_ * u_ref[...] * sg * (1.0 + g * (1.0 - sg))).astype(BF16)
        du_ref[...] = (da_ * g * sg).astype(BF16)

    spec = pl.BlockSpec((tm, tc), lambda i, j: (i, j))
    return pl.pallas_call(
        kern, name=name, grid=(s // tm, nc),
        in_specs=[spec, spec, pl.BlockSpec((tm, tc), lambda i, j: (i, j + nc))],
        out_specs=[spec, spec],
        out_shape=[jax.ShapeDtypeStruct((s, f), BF16)] * 2,
        compiler_params=_params(("parallel", "parallel")),
    )(da, gu, gu)


def _res_norm(o, x, gains, name):
    s, d = x.shape
    tm = _tile(s, ROWS, 16)
    ng = len(gains)

    def kern(o_ref, x_ref, *refs):
        xn = x_ref[...] + o_ref[...]
        refs[ng][...] = xn
        for t in range(ng):
            refs[ng + 1 + t][...] = _rms(xn, refs[t][...]).astype(BF16)

    return pl.pallas_call(
        kern, name=name, grid=(s // tm,),
        in_specs=[_row_spec(tm, d), _row_spec(tm, d)] + [_vec_spec(d)] * ng,
        out_specs=[_row_spec(tm, d)] * (1 + ng),
        out_shape=[jax.ShapeDtypeStruct((s, d), F32)] + [jax.ShapeDtypeStruct((s, d), BF16)] * ng,
        compiler_params=_params(("parallel",)),
    )(o, x, *gains)


def _loss_head(o, x, g, target, name):
    s, d = x.shape
    tm = _tile(s, ROWS, 16)

    def kern(o_ref, x_ref, g_ref, t_ref, dx_ref, dxb_ref, loss_ref, dg_ref):
        i = pl.program_id(0)
        x4 = x_ref[...] + o_ref[...]
        gg = g_ref[...]
        diff = _rms(x4, gg) - t_ref[...]
        dx, dgr = _rms_bwd(x4, gg, diff * (1.0 / d))
        dx_ref[...] = dx
        dxb_ref[...] = dx.astype(BF16)
        _acc_rows(loss_ref, diff * diff * (0.5 / d), i)
        _acc_rows(dg_ref, dgr, i)

    return pl.pallas_call(
        kern, name=name, grid=(s // tm,),
        in_specs=[_row_spec(tm, d), _row_spec(tm, d), _vec_spec(d), _row_spec(tm, d)],
        out_specs=[_row_spec(tm, d), _row_spec(tm, d), _vec_spec(d), _vec_spec(d)],
        out_shape=[jax.ShapeDtypeStruct((s, d), F32), jax.ShapeDtypeStruct((s, d), BF16),
                   jax.ShapeDtypeStruct((1, d), F32), jax.ShapeDtypeStruct((1, d), F32)],
        compiler_params=_params(("arbitrary",)),
    )(o, x, g, target)


def _norm_bwd(x, gains, dhs, dres, name):
    s, d = x.shape
    tm = _tile(s, ROWS, 16)
    ng = len(gains)

    def kern(x_ref, dres_ref, *refs):
        i = pl.program_id(0)
        x_ = x_ref[...]
        acc = dres_ref[...]
        for t in range(ng):
            dx, dgr = _rms_bwd(x_, refs[t][...], refs[ng + t][...])
            acc = acc + dx
            _acc_rows(refs[2 * ng + 2 + t], dgr, i)
        refs[2 * ng][...] = acc
        refs[2 * ng + 1][...] = acc.astype(BF16)

    return pl.pallas_call(
        kern, name=name, grid=(s // tm,),
        in_specs=[_row_spec(tm, d), _row_spec(tm, d)] + [_vec_spec(d)] * ng + [_row_spec(tm, d)] * ng,
        out_specs=[_row_spec(tm, d), _row_spec(tm, d)] + [_vec_spec(d)] * ng,
        out_shape=[jax.ShapeDtypeStruct((s, d), F32), jax.ShapeDtypeStruct((s, d), BF16)]
        + [jax.ShapeDtypeStruct((1, d), F32)] * ng,
        compiler_params=_params(("arbitrary",)),
    )(x, dres, *gains, *dhs)


def _glu_bwd(dmix, vg, name):
    s, d = dmix.shape
    tm = _tile(s, ROWS, 16)

    def kern(dm_ref, val_ref, gate_ref, o_ref):
        dm = dm_ref[...]
        sg = _sigmoid(gate_ref[...])
        o_ref[:, :d] = (dm * sg).astype(BF16)
        o_ref[:, d:] = (dm * val_ref[...] * sg * (1.0 - sg)).astype(BF16)

    return pl.pallas_call(
        kern, name=name, grid=(s // tm,),
        in_specs=[_row_spec(tm, d), _row_spec(tm, d, 0), _row_spec(tm, d, 1)],
        out_specs=_row_spec(tm, 2 * d),
        out_shape=jax.ShapeDtypeStruct((s, 2 * d), BF16),
        compiler_params=_params(("parallel",)),
    )(dmix, vg, vg)


def _shift_down(v, sh, rows):
    return jnp.where(rows >= sh, pltpu.roll(v, sh, 0), 0.0)


def _shift_up(v, sh, rows, t):
    return jnp.where(rows < t - sh, pltpu.roll(v, t - sh, 0), 0.0)


def _s5_fwd(u, bbd_re, bbd_im, cbd_re, cbd_im, pw_re, pw_im, dskip, name):
    s, d = u.shape
    nt, cw, lw = bbd_re.shape
    t = pw_re.shape[0]
    nc = s // t

    def kern(u_ref, bre_ref, bim_ref, cre_ref, cim_ref, pre_ref, pim_ref, d_ref,
             xr_ref, xi_ref, y_ref, z_ref, car_ref, cai_ref):
        c = pl.program_id(1)

        @pl.when(c == 0)
        def _():
            car_ref[...] = jnp.zeros_like(car_ref)
            cai_ref[...] = jnp.zeros_like(cai_ref)

        u_ = u_ref[...]
        ub = u_.astype(BF16)
        xr = _dot(ub, bre_ref[...])
        xi = _dot(ub, bim_ref[...])
        rows = lax.broadcasted_iota(jnp.int32, (t, lw), 0)
        sh = 1
        while sh < t:
            ar = pre_ref[sh - 1:sh, :]
            ai = pim_ref[sh - 1:sh, :]
            sr = _shift_down(xr, sh, rows)
            si = _shift_down(xi, sh, rows)
            xr, xi = xr + ar * sr - ai * si, xi + ar * si + ai * sr
            sh *= 2
        cr = car_ref[...]
        ci = cai_ref[...]
        pr = pre_ref[...]
        pi = pim_ref[...]
        xr, xi = xr + pr * cr - pi * ci, xi + pr * ci + pi * cr
        car_ref[...] = xr[t - 1:t, :]
        cai_ref[...] = xi[t - 1:t, :]
        xr_ref[...] = xr
        xi_ref[...] = xi
        y = _dot(xr.astype(BF16), cre_ref[...]) - _dot(xi.astype(BF16), cim_ref[...]) + d_ref[...] * u_
        y_ref[...] = y
        z_ref[...] = _gelu(y).astype(BF16)

    tok = pl.BlockSpec((t, cw), lambda j, c: (c, j))
    st = pl.BlockSpec((t, lw), lambda j, c: (c, j))
    return pl.pallas_call(
        kern, name=name, grid=(nt, nc),
        in_specs=[tok,
                  pl.BlockSpec((None, cw, lw), lambda j, c: (j, 0, 0)),
                  pl.BlockSpec((None, cw, lw), lambda j, c: (j, 0, 0)),
                  pl.BlockSpec((None, lw, cw), lambda j, c: (j, 0, 0)),
                  pl.BlockSpec((None, lw, cw), lambda j, c: (j, 0, 0)),
                  pl.BlockSpec((t, lw), lambda j, c: (0, j)),
                  pl.BlockSpec((t, lw), lambda j, c: (0, j)),
                  pl.BlockSpec((1, cw), lambda j, c: (0, j))],
        out_specs=[st, st, tok, tok],
        out_shape=[jax.ShapeDtypeStruct((s, nt * lw), F32), jax.ShapeDtypeStruct((s, nt * lw), F32),
                   jax.ShapeDtypeStruct((s, d), F32), jax.ShapeDtypeStruct((s, d), BF16)],
        scratch_shapes=[pltpu.VMEM((1, lw), F32), pltpu.VMEM((1, lw), F32)],
        compiler_params=_params(("parallel", "arbitrary")),
    )(u, bbd_re, bbd_im, cbd_re, cbd_im, pw_re, pw_im, dskip)


def _s5_bwd(dz, y, u, xs_re, xs_im, bbd_re, bbd_im, cbd_re, cbd_im, pw_re, pw_im, pf_re, pf_im, dskip, name):
    s, d = u.shape
    nt, cw, lw = bbd_re.shape
    t = pw_re.shape[0]
    nc = s // t

    def kern(dz_ref, y_ref, u_ref, xr_ref, xi_ref, bre_ref, bim_ref, cre_ref, cim_ref,
             pre_ref, pim_ref, fre_ref, fim_ref, d_ref,
             du_ref, dd_ref, dcr_ref, dci_ref, dbr_ref, dbi_ref, dar_ref, dai_ref, car_ref, cai_ref):
        c = pl.program_id(1)

        @pl.when(c == 0)
        def _():
            car_ref[...] = jnp.zeros_like(car_ref)
            cai_ref[...] = jnp.zeros_like(cai_ref)

        u_ = u_ref[...]
        ub = u_.astype(BF16)
        dy = dz_ref[...] * _gelu_grad(y_ref[...])
        dyb = dy.astype(BF16)
        gr = _dot_nt(dyb, cre_ref[...])
        gi = -_dot_nt(dyb, cim_ref[...])
        rows = lax.broadcasted_iota(jnp.int32, (t, lw), 0)
        sh = 1
        while sh < t:
            ar = pre_ref[sh - 1:sh, :]
            ai = pim_ref[sh - 1:sh, :]
            sr = _shift_up(gr, sh, rows, t)
            si = _shift_up(gi, sh, rows, t)
            gr, gi = gr + ar * sr + ai * si, gi + ar * si - ai * sr
            sh *= 2
        cr = car_ref[...]
        ci = cai_ref[...]
        fr = fre_ref[...]
        fi = fim_ref[...]
        gr, gi = gr + fr * cr + fi * ci, gi + fr * ci - fi * cr
        gsr = jnp.where(rows < t - 1, pltpu.roll(gr, t - 1, 0), cr)
        gsi = jnp.where(rows < t - 1, pltpu.roll(gi, t - 1, 0), ci)
        car_ref[...] = gr[0:1, :]
        cai_ref[...] = gi[0:1, :]
        xr = xr_ref[...]
        xi = xi_ref[...]
        dar = jnp.sum(gsr * xr + gsi * xi, axis=0, keepdims=True)
        dai = jnp.sum(gsi * xr - gsr * xi, axis=0, keepdims=True)
        grb = gr.astype(BF16)
        gib = gi.astype(BF16)
        dbr = _dot_tn(ub, grb)
        dbi = _dot_tn(ub, gib)
        dcr = _dot_tn(dyb, xr.astype(BF16))
        dci = _dot_tn(dyb, xi.astype(BF16))
        du_ref[...] = dy * d_ref[...] + _dot_nt(grb, bre_ref[...]) + _dot_nt(gib, bim_ref[...])
        ddv = jnp.sum(dy * u_, axis=0, keepdims=True)

        @pl.when(c == 0)
        def _():
            dd_ref[...] = ddv
            dcr_ref[...] = dcr
            dci_ref[...] = dci
            dbr_ref[...] = dbr
            dbi_ref[...] = dbi
            dar_ref[...] = dar
            dai_ref[...] = dai

        @pl.when(c > 0)
        def _():
            dd_ref[...] += ddv
            dcr_ref[...] += dcr
            dci_ref[...] += dci
            dbr_ref[...] += dbr
            dbi_ref[...] += dbi
            dar_ref[...] += dar
            dai_ref[...] += dai

    tok = pl.BlockSpec((t, cw), lambda j, c: (nc - 1 - c, j))
    st = pl.BlockSpec((t, lw), lambda j, c: (nc - 1 - c, j))
    wb = pl.BlockSpec((None, cw, lw), lambda j, c: (j, 0, 0))
    wc = pl.BlockSpec((None, lw, cw), lambda j, c: (j, 0, 0))
    pw = pl.BlockSpec((t, lw), lambda j, c: (0, j))
    vec_c = pl.BlockSpec((1, cw), lambda j, c: (0, j))
    vec_l = pl.BlockSpec((1, lw), lambda j, c: (0, j))
    return pl.pallas_call(
        kern, name=name, grid=(nt, nc),
        in_specs=[tok, tok, tok, st, st, wb, wb, wc, wc, pw, pw, pw, pw, vec_c],
        out_specs=[tok, vec_c, wb, wb, wb, wb, vec_l, vec_l],
        out_shape=[jax.ShapeDtypeStruct((s, d), F32), jax.ShapeDtypeStruct((1, d), F32)]
        + [jax.ShapeDtypeStruct((nt, cw, lw), F32)] * 4
        + [jax.ShapeDtypeStruct((1, nt * lw), F32)] * 2,
        scratch_shapes=[pltpu.VMEM((1, lw), F32), pltpu.VMEM((1, lw), F32)],
        compiler_params=_params(("parallel", "arbitrary")),
    )(dz, y, u, xs_re, xs_im, bbd_re, bbd_im, cbd_re, cbd_im, pw_re, pw_im, pf_re, pf_im, dskip)


def _s5_discretize(lam_re, lam_im, log_dt, b_re, b_im):
    dt = jnp.exp(log_dt)[:, None]
    mag = jnp.exp(lam_re * dt)
    ang = lam_im * dt
    lb_re = mag * jnp.cos(ang)
    lb_im = mag * jnp.sin(ang)
    nr = lb_re - 1.0
    den = lam_re * lam_re + lam_im * lam_im
    f_re = (nr * lam_re + lb_im * lam_im) / den
    f_im = (lb_im * lam_re - nr * lam_im) / den
    bb_re = f_re[..., None] * b_re - f_im[..., None] * b_im
    bb_im = f_re[..., None] * b_im + f_im[..., None] * b_re
    return lb_re, lb_im, bb_re, bb_im


def _block_diag(w):
    nt, ng, a, b = w.shape
    eye = jnp.eye(ng, dtype=w.dtype)
    return (w[:, :, :, None, :] * eye[None, :, None, :, None]).reshape(nt, ng * a, ng * b)


def _block_diag_take(w, a, b):
    nt = w.shape[0]
    ng = GROUPS_PER_TILE
    w5 = w.reshape(nt, ng, a, ng, b)
    idx = jnp.arange(ng)
    return w5[:, idx, :, idx, :].transpose(1, 0, 2, 3).reshape(nt * ng, a, b)


def _att_masks(jb):
    qi = lax.broadcasted_iota(jnp.int32, (ATT_BLK, ATT_BLK), 0)
    si = lax.broadcasted_iota(jnp.int32, (ATT_BLK, ATT_BLK), 1)
    return si <= qi, (si >= qi) & (jb > 0)


def _att_fwd(q, kv, gi, dil, nh, name):
    s = q.shape[0]
    n = s // dil
    nb = n // ATT_BLK
    ng = q.shape[1] // (nh * HEAD_DIM)
    rep = nh // N_KV_HEADS
    qw = rep * HEAD_DIM
    scale = HEAD_DIM ** -0.5
    q2 = q.reshape(n, dil * q.shape[1])
    kv2 = kv.reshape(n, dil * kv.shape[1])

    def kern(q_ref, kc_ref, kp_ref, vc_ref, vp_ref, o_ref, l_ref):
        jb = pl.program_id(2)
        mc, mp = _att_masks(jb)
        kc, kp, vc, vp = kc_ref[...], kp_ref[...], vc_ref[...], vp_ref[...]
        for g in range(rep):
            cols = slice(g * HEAD_DIM, (g + 1) * HEAD_DIM)
            qg = q_ref[:, cols]
            sc = jnp.where(mc, _dot_nt(qg, kc) * scale, NEG_INF)
            sp = jnp.where(mp, _dot_nt(qg, kp) * scale, NEG_INF)
            m = jnp.maximum(jnp.max(sc, axis=-1, keepdims=True), jnp.max(sp, axis=-1, keepdims=True))
            pc = jnp.exp(sc - m)
            pp = jnp.exp(sp - m)
            l = jnp.sum(pc, axis=-1, keepdims=True) + jnp.sum(pp, axis=-1, keepdims=True)
            o_ref[:, cols] = _dot((pc / l).astype(BF16), vc) + _dot((pp / l).astype(BF16), vp)
            l_ref[:, cols] = jnp.broadcast_to(m + jnp.log(l), (ATT_BLK, HEAD_DIM))

    def kv_spec(off, prev):
        if prev:
            return pl.BlockSpec((ATT_BLK, HEAD_DIM), lambda r, h, jb: (jnp.maximum(jb - 1, 0), r * 2 * N_KV_HEADS + off + h))
        return pl.BlockSpec((ATT_BLK, HEAD_DIM), lambda r, h, jb: (jb, r * 2 * N_KV_HEADS + off + h))

    out_spec = pl.BlockSpec((ATT_BLK, qw), lambda r, h, jb: (jb, r * N_KV_HEADS + h))
    out, lse = pl.pallas_call(
        kern, name=name, grid=(dil, N_KV_HEADS, nb),
        in_specs=[pl.BlockSpec((ATT_BLK, qw), lambda r, h, jb: (jb, (r * ng + gi) * N_KV_HEADS + h)),
                  kv_spec(0, False), kv_spec(0, True), kv_spec(N_KV_HEADS, False), kv_spec(N_KV_HEADS, True)],
        out_specs=[out_spec, out_spec],
        out_shape=[jax.ShapeDtypeStruct((n, dil * nh * HEAD_DIM), F32)] * 2,
        compiler_params=_params(("parallel", "parallel", "arbitrary")),
    )(q2, kv2, kv2, kv2, kv2)
    return out.reshape(s, nh * HEAD_DIM), lse.reshape(s, nh * HEAD_DIM)


def _att_combine(outs, lses, name):
    s, d = outs[0].shape
    tm = _tile(s, ROWS, 16)
    ng = len(outs)

    def kern(*refs):
        ls = [refs[ng + t][...] for t in range(ng)]
        mx = functools.reduce(jnp.maximum, ls)
        es = [jnp.exp(l - mx) for l in ls]
        den = functools.reduce(lambda a, b: a + b, es)
        o = functools.reduce(lambda a, b: a + b, [es[t] / den * refs[t][...] for t in range(ng)])
        refs[2 * ng][...] = o
        refs[2 * ng + 1][...] = o.astype(BF16)

    return pl.pallas_call(
        kern, name=name, grid=(s // tm,),
        in_specs=[_row_spec(tm, d)] * (2 * ng),
        out_specs=[_row_spec(tm, d)] * 2,
        out_shape=[jax.ShapeDtypeStruct((s, d), F32), jax.ShapeDtypeStruct((s, d), BF16)],
        compiler_params=_params(("parallel",)),
    )(*outs, *lses)


def _att_bwd(q, kv, do, o, lses, gi, dil, nh, name):
    s = q.shape[0]
    n = s // dil
    nb = n // ATT_BLK
    ng = len(lses)
    rep = nh // N_KV_HEADS
    qw = rep * HEAD_DIM
    dm = nh * HEAD_DIM
    scale = HEAD_DIM ** -0.5
    q2 = q.reshape(n, dil * q.shape[1])
    kv2 = kv.reshape(n, dil * kv.shape[1])
    wide = [a.reshape(n, dil * dm) for a in (do, o, *lses)]

    def kern(q_ref, kc_ref, kp_ref, vc_ref, vp_ref, do_ref, o_ref, *refs):
        l_refs = refs[:ng]
        dq_ref, dk_ref, dv_ref, ck_ref, cv_ref = refs[ng:]
        jb = pl.program_id(2)

        @pl.when(jb == 0)
        def _():
            ck_ref[...] = jnp.zeros_like(ck_ref)
            cv_ref[...] = jnp.zeros_like(cv_ref)

        @pl.when(jb < nb)
        def _():
            mc, mp = _att_masks(jb)
            kc, kp, vc, vp = kc_ref[...], kp_ref[...], vc_ref[...], vp_ref[...]
            ls = [r[...] for r in l_refs]
            mx = functools.reduce(jnp.maximum, ls)
            den = functools.reduce(lambda a, b: a + b, [jnp.exp(l - mx) for l in ls])
            lse_g = ls[gi]
            w = jnp.exp(lse_g - mx) / den
            do_ = do_ref[...]
            dout = w * do_
            doo = do_ * o_ref[...]
            dkc = jnp.zeros((ATT_BLK, HEAD_DIM), F32)
            dkp = jnp.zeros((ATT_BLK, HEAD_DIM), F32)
            dvc = jnp.zeros((ATT_BLK, HEAD_DIM), F32)
            dvp = jnp.zeros((ATT_BLK, HEAD_DIM), F32)
            for g in range(rep):
                cols = slice(g * HEAD_DIM, (g + 1) * HEAD_DIM)
                qg = q_ref[:, cols]
                lg = lse_g[:, g * HEAD_DIM:g * HEAD_DIM + 1]
                ct = w[:, g * HEAD_DIM:g * HEAD_DIM + 1] * jnp.sum(doo[:, cols], axis=-1, keepdims=True)
                dob = dout[:, cols].astype(BF16)
                pc = jnp.exp(jnp.where(mc, _dot_nt(qg, kc) * scale, NEG_INF) - lg)
                pp = jnp.exp(jnp.where(mp, _dot_nt(qg, kp) * scale, NEG_INF) - lg)
                dsc = (pc * (_dot_nt(dob, vc) - ct) * scale).astype(BF16)
                dsp = (pp * (_dot_nt(dob, vp) - ct) * scale).astype(BF16)
                dq_ref[:, cols] = (_dot(dsc, kc) + _dot(dsp, kp)).astype(BF16)
                dkc = dkc + _dot_tn(dsc, qg)
                dkp = dkp + _dot_tn(dsp, qg)
                dvc = dvc + _dot_tn(pc.astype(BF16), dob)
                dvp = dvp + _dot_tn(pp.astype(BF16), dob)
            dk_ref[...] = ck_ref[...] + dkp
            dv_ref[...] = cv_ref[...] + dvp
            ck_ref[...] = dkc
            cv_ref[...] = dvc

        @pl.when(jb == nb)
        def _():
            dk_ref[...] = ck_ref[...]
            dv_ref[...] = cv_ref[...]

    def jq(jb):
        return jnp.minimum(jb, nb - 1)

    def kv_spec(off, prev):
        if prev:
            return pl.BlockSpec((ATT_BLK, HEAD_DIM), lambda r, h, jb: (jnp.maximum(jq(jb) - 1, 0), r * 2 * N_KV_HEADS + off + h))
        return pl.BlockSpec((ATT_BLK, HEAD_DIM), lambda r, h, jb: (jq(jb), r * 2 * N_KV_HEADS + off + h))

    wide_spec = pl.BlockSpec((ATT_BLK, qw), lambda r, h, jb: (jq(jb), r * N_KV_HEADS + h))
    dkv_spec = pl.BlockSpec((ATT_BLK, HEAD_DIM), lambda r, h, jb: (jnp.maximum(jb - 1, 0), r * N_KV_HEADS + h))
    n_q_groups = q.shape[1] // dm
    dq, dk, dv = pl.pallas_call(
        kern, name=name, grid=(dil, N_KV_HEADS, nb + 1),
        in_specs=[pl.BlockSpec((ATT_BLK, qw), lambda r, h, jb: (jq(jb), (r * n_q_groups + gi) * N_KV_HEADS + h)),
                  kv_spec(0, False), kv_spec(0, True), kv_spec(N_KV_HEADS, False), kv_spec(N_KV_HEADS, True)]
        + [wide_spec] * (2 + ng),
        out_specs=[wide_spec, dkv_spec, dkv_spec],
        out_shape=[jax.ShapeDtypeStruct((n, dil * dm), BF16),
                   jax.ShapeDtypeStruct((n, dil * N_KV_HEADS * HEAD_DIM), F32),
                   jax.ShapeDtypeStruct((n, dil * N_KV_HEADS * HEAD_DIM), F32)],
        scratch_shapes=[pltpu.VMEM((ATT_BLK, HEAD_DIM), F32), pltpu.VMEM((ATT_BLK, HEAD_DIM), F32)],
        compiler_params=_params(("parallel", "parallel", "arbitrary")),
    )(q2, kv2, kv2, kv2, kv2, *wide)
    return dq.reshape(s, dm), dk.reshape(s, -1), dv.reshape(s, -1)


def _dkv_sum(dks, dvs, name):
    s, w = dks[0].shape
    tm = _tile(s, ROWS, 16)
    ng = len(dks)

    def kern(*refs):
        o_ref = refs[2 * ng]
        o_ref[:, :w] = functools.reduce(lambda a, b: a + b, [refs[t][...] for t in range(ng)]).astype(BF16)
        o_ref[:, w:] = functools.reduce(lambda a, b: a + b, [refs[ng + t][...] for t in range(ng)]).astype(BF16)

    return pl.pallas_call(
        kern, name=name, grid=(s // tm,),
        in_specs=[_row_spec(tm, w)] * (2 * ng),
        out_specs=_row_spec(tm, 2 * w),
        out_shape=jax.ShapeDtypeStruct((s, 2 * w), BF16),
        compiler_params=_params(("parallel",)),
    )(*dks, *dvs)


def _ffn_fwd(hf, weight, layer, tag):
    gu = _mm_nn(hf, weight(f"w_in{layer}", hf), F32, f"{tag}_in", tn=1408)
    a = _swiglu(gu, f"{tag}_act")
    o = _mm_nn(a, weight(f"w_out{layer}", a), F32, f"{tag}_out", tn=1024, tk=1408)
    return gu, a, o


def _ffn_bwd(dresb, hf, gu, a, weight, emit, layer, tag):
    w_in, w_out = weight(f"w_in{layer}", None), weight(f"w_out{layer}", None)
    emit(f"w_out{layer}", _mm_tn(a, dresb, 1, f"{tag}_out_dw", tkk=1408, tn=1024))
    da = _mm_nt(dresb, w_out, f"{tag}_out_dx", tr=2048, tkc=1408)
    dgate, dup = _swiglu_bwd(da, gu, f"{tag}_act_bwd")
    dgu = jnp.concatenate([dgate, dup], axis=1)
    emit(f"w_in{layer}", _mm_tn(hf, dgu, w_in.shape[0], f"{tag}_in_dw", tn=1408))
    return _mm_nt(dgu, w_in, f"{tag}_in_dx", tr=1408)


def _local_step(x, target, small, weight, emit):
    s, d = x.shape
    nh = d // HEAD_DIM
    n_groups = d // S5_GROUP_CH
    nt = n_groups // GROUPS_PER_TILE
    p_, c_ = S5_STATE, S5_GROUP_CH

    disc_in = (small["lam_re"], small["lam_im"], small["log_dt"], small["b_re"], small["b_im"])
    (lb_re, lb_im, bb_re, bb_im), disc_vjp = jax.vjp(_s5_discretize, *disc_in)
    del lb_re, lb_im
    dt = jnp.exp(small["log_dt"])[:, None]
    steps = jnp.arange(1, SCAN_T + 1, dtype=F32)[:, None, None]
    pmag = jnp.exp(steps * (small["lam_re"] * dt)[None])
    pang = steps * (small["lam_im"] * dt)[None]
    pw_re = (pmag * jnp.cos(pang)).reshape(SCAN_T, n_groups * p_)
    pw_im = (pmag * jnp.sin(pang)).reshape(SCAN_T, n_groups * p_)
    pf_re, pf_im = pw_re[::-1], pw_im[::-1]
    bbd_re = _block_diag(bb_re.transpose(0, 2, 1).reshape(nt, GROUPS_PER_TILE, c_, p_)).astype(BF16)
    bbd_im = _block_diag(bb_im.transpose(0, 2, 1).reshape(nt, GROUPS_PER_TILE, c_, p_)).astype(BF16)
    cbd_re = _block_diag(small["c_re"].transpose(0, 2, 1).reshape(nt, GROUPS_PER_TILE, p_, c_)).astype(BF16)
    cbd_im = _block_diag(small["c_im"].transpose(0, 2, 1).reshape(nt, GROUPS_PER_TILE, p_, c_)).astype(BF16)

    h0 = _norm_f32(x, small["a_norm"], "s5_norm")
    xs_re, xs_im, y0, z = _s5_fwd(h0, bbd_re, bbd_im, cbd_re, cbd_im, pw_re, pw_im, small["s5_d"], "s5_fwd")
    vg = _mm_nn(z, weight("w_glu", z), F32, "glu_mm", tn=1024)
    x1, hf0 = _glu_res_norm(vg, x, small["ffn_norm0"], "glu_res_norm")
    gu0, a0, o0 = _ffn_fwd(hf0, weight, 0, "ffn0")
    x2, kvn, h1 = _res_norm(o0, x1, [small["kv_norm"], small["b_norm"]], "ffn0_res_norm")
    kv = _mm_nn(kvn, weight("w_kv", kvn), BF16, "kv_mm", tn=1024)
    q = _mm_nn(h1, weight("w_q", kv), BF16, "q_mm", tn=1536)
    outs, lses = [], []
    for gi, (window, dil) in enumerate(PATTERNS):
        assert window // dil == ATT_BLK
        og, lg = _att_fwd(q, kv, gi, dil, nh, f"att_fwd{gi}")
        outs.append(og)
        lses.append(lg)
    oatt, oattb = _att_combine(outs, lses, "att_combine")
    ao = _mm_nn(oattb, weight("w_o", oattb), F32, "o_mm", tn=1024)
    x3, hf1 = _res_norm(ao, x2, [small["ffn_norm1"]], "att_res_norm")
    gu1, a1, o1 = _ffn_fwd(hf1, weight, 1, "ffn1")
    dres, dresb, loss_rows, d_final = _loss_head(o1, x3, small["final_norm"], target, "loss_head")

    dhf1 = _ffn_bwd(dresb, hf1, gu1, a1, weight, emit, 1, "ffn1")
    dres, dresb, d_ffn_norm1 = _norm_bwd(x3, [small["ffn_norm1"]], [dhf1], dres, "ffn1_norm_bwd")
    emit("w_o", _mm_tn(oattb, dresb, 1, "o_dw", tn=1024))
    doatt = _mm_nt(dresb, weight("w_o", None), "o_dx", tr=2048)
    dqs, dks, dvs = [], [], []
    for gi, (window, dil) in enumerate(PATTERNS):
        dq_g, dk_g, dv_g = _att_bwd(q, kv, doatt, oatt, lses, gi, dil, nh, f"att_bwd{gi}")
        dqs.append(dq_g)
        dks.append(dk_g)
        dvs.append(dv_g)
    dq = jnp.concatenate(dqs, axis=1)
    dkv = _dkv_sum(dks, dvs, "dkv_sum")
    w_q = weight("w_q", None)
    emit("w_q", _mm_tn(h1, dq, w_q.shape[0], "q_dw", tn=1536))
    emit("w_kv", _mm_tn(kvn, dkv, 1, "kv_dw", tn=1024))
    dh1 = _mm_nt(dq, w_q, "q_dx", tr=1536)
    dkvn = _mm_nt(dkv, weight("w_kv", None), "kv_dx", tr=1024)
    dres, dresb, d_b_norm, d_kv_norm = _norm_bwd(
        x2, [small["b_norm"], small["kv_norm"]], [dh1, dkvn], dres, "att_norm_bwd")
    dhf0 = _ffn_bwd(dresb, hf0, gu0, a0, weight, emit, 0, "ffn0")
    dres, dresb, d_ffn_norm0 = _norm_bwd(x1, [small["ffn_norm0"]], [dhf0], dres, "ffn0_norm_bwd")
    del dresb
    dvg = _glu_bwd(dres, vg, "glu_bwd")
    w_glu = weight("w_glu", None)
    emit("w_glu", _mm_tn(z, dvg, w_glu.shape[0], "glu_dw", tn=1024))
    dz = _mm_nt(dvg, w_glu, "glu_dx", tr=1024)
    dh0, d_s5_d, dcr, dci_neg, dbr, dbi, dar, dai = _s5_bwd(
        dz, y0, h0, xs_re, xs_im, bbd_re, bbd_im, cbd_re, cbd_im, pw_re, pw_im, pf_re, pf_im,
        small["s5_d"], "s5_bwd")
    grad_x, _, d_a_norm = _norm_bwd(x, [small["a_norm"]], [dh0], dres, "s5_norm_bwd")

    d_bb_re = _block_diag_take(dbr, c_, p_).transpose(0, 2, 1)
    d_bb_im = _block_diag_take(dbi, c_, p_).transpose(0, 2, 1)
    d_c_re = _block_diag_take(dcr, c_, p_)
    d_c_im = -_block_diag_take(dci_neg, c_, p_)
    d_lam_re, d_lam_im, d_log_dt, d_b_re, d_b_im = disc_vjp(
        (dar.reshape(n_groups, p_), dai.reshape(n_groups, p_), d_bb_re, d_bb_im))

    d_small = dict(lam_re=d_lam_re, lam_im=d_lam_im, log_dt=d_log_dt, b_re=d_b_re, b_im=d_b_im,
                   c_re=d_c_re, c_im=d_c_im, s5_d=d_s5_d, a_norm=d_a_norm, ffn_norm0=d_ffn_norm0,
                   ffn_norm1=d_ffn_norm1, b_norm=d_b_norm, kv_norm=d_kv_norm, final_norm=d_final)
    return loss_rows, grad_x, d_small


def _place():
    x, y, c = lax.axis_index("x"), lax.axis_index("y"), lax.axis_index("c")
    return x, y, c, [(1 - x, y), (x, 1 - y), (1 - x, 1 - y)]


_ANY = pl.BlockSpec(memory_space=pl.ANY)


def _all_gather_chips(shards):
    n = len(shards)

    def body(*refs):
        ins, outs = refs[:n], refs[n:2 * n]
        send_sems, recv_sems, loc_sems = refs[2 * n:]
        x, y, c, peers = _place()
        me = 2 * x + y
        copies = []
        for a in range(n):
            mine = pltpu.make_async_copy(ins[a], outs[a].at[me], loc_sems.at[a])
            mine.start()
            copies.append(mine)
            for k, (px, py) in enumerate(peers):
                cp = pltpu.make_async_remote_copy(
                    src_ref=ins[a], dst_ref=outs[a].at[me], send_sem=send_sems.at[3 * a + k],
                    recv_sem=recv_sems.at[3 * a + k], device_id=(px, py, c), device_id_type=MESH)
                cp.start()
                copies.append(cp)
        for cp in copies:
            cp.wait()

    return pl.pallas_call(
        body, name="all_gather_chips",
        in_specs=[_ANY] * n, out_specs=[_ANY] * n,
        out_shape=[jax.ShapeDtypeStruct((N_CHIPS,) + a.shape, a.dtype) for a in shards],
        scratch_shapes=[pltpu.SemaphoreType.DMA((3 * n,)), pltpu.SemaphoreType.DMA((3 * n,)),
                        pltpu.SemaphoreType.DMA((n,))],
    )(*shards)


_HBM =pl.BlockSpec(memory_space=pltpu.HBM)
_SEM = pl.BlockSpec(memory_space=pltpu.SEMAPHORE)
_EFFECT = pltpu.SideEffectType.DATAFLOW_SIDE_EFFECTING


def _in_hbm(a):
    return pltpu.with_memory_space_constraint(a, pltpu.HBM)


def _place_own(shards):
    n = len(shards)

    def body(*refs):
        ins, outs, sems = refs[:n], refs[n:2 * n], refs[2 * n]
        me = 2 * lax.axis_index("x") + lax.axis_index("y")
        copies = [pltpu.make_async_copy(ins[a], outs[a].at[me], sems.at[a]) for a in range(n)]
        for cp in copies:
            cp.start()
        for cp in copies:
            cp.wait()

    return pl.pallas_call(
        body, name="place_own",
        in_specs=[_ANY] * n, out_specs=[_ANY] * n,
        out_shape=[jax.ShapeDtypeStruct((N_CHIPS,) + a.shape, a.dtype) for a in shards],
        scratch_shapes=[pltpu.SemaphoreType.DMA((n,))],
    )(*shards)


def _gather_start(shards, lands):
    n = len(shards)

    def body(*refs):
        ins, land = refs[:n], refs[n:2 * n]
        send, recv = refs[2 * n:3 * n], refs[3 * n:4 * n]
        token = refs[6 * n]
        x, y, c, peers = _place()
        me = 2 * x + y
        for a in range(n):
            for k, (px, py) in enumerate(peers):
                pltpu.make_async_remote_copy(
                    src_ref=ins[a], dst_ref=land[a].at[me], send_sem=send[a].at[k], recv_sem=recv[a].at[k],
                    device_id=(px, py, c), device_id_type=MESH).start()
        token[...] = jnp.zeros_like(token)

    outs = pl.pallas_call(
        body, name="gather_start",
        out_shape=[pltpu.SemaphoreType.DMA((3,))] * (2 * n)
        + [pltpu.HBM(a.shape, a.dtype) for a in shards] + [pltpu.HBM(a.shape, a.dtype) for a in lands]
        + [jax.ShapeDtypeStruct((8, 128), F32)],
        in_specs=[_HBM] * (2 * n),
        out_specs=[_SEM] * (2 * n) + [_HBM] * (2 * n) + [pl.BlockSpec(memory_space=pltpu.VMEM)],
        input_output_aliases={i: 2 * n + i for i in range(2 * n)},
        compiler_params=pltpu.CompilerParams(has_side_effects=_EFFECT),
    )(*[_in_hbm(a) for a in shards], *[_in_hbm(a) for a in lands])
    return outs[:n], outs[n:2 * n], outs[2 * n:3 * n], outs[3 * n:4 * n], outs[4 * n]


def _gather_wait(shard, land, send, recv, after, name):
    def body(shard_ref, land_ref, send_sem, recv_sem, after_ref, shard_out, land_out):
        del after_ref, shard_out, land_out
        x, y, c, peers = _place()
        for k, (px, py) in enumerate(peers):
            cp = pltpu.make_async_remote_copy(
                src_ref=shard_ref, dst_ref=land_ref.at[2 * px + py], send_sem=send_sem.at[k],
                recv_sem=recv_sem.at[k], device_id=(px, py, c), device_id_type=MESH)
            cp.wait_send()
            cp.wait_recv()

    return pl.pallas_call(
        body, name=name,
        out_shape=(pltpu.HBM(shard.shape, shard.dtype), pltpu.HBM(land.shape, land.dtype)),
        in_specs=(_HBM, _HBM, _SEM, _SEM, _ANY), out_specs=(_HBM, _HBM),
        input_output_aliases={0: 0, 1: 1},
        compiler_params=pltpu.CompilerParams(has_side_effects=_EFFECT),
    )(shard, land, send, recv, after)[1]


def _scatter_start(g, name):
    def body(g_ref, land_ref, send, recv, g_out, land_out):
        del g_out, land_out
        x, y, c, peers = _place()
        for k, (px, py) in enumerate(peers):
            pltpu.make_async_remote_copy(
                src_ref=g_ref.at[2 * px + py], dst_ref=land_ref.at[k], send_sem=send.at[k], recv_sem=recv.at[k],
                device_id=(px, py, c), device_id_type=MESH).start()

    land = lax.empty((3,) + g.shape[1:], g.dtype)
    return pl.pallas_call(
        body, name=name,
        out_shape=(pltpu.SemaphoreType.DMA((3,)), pltpu.SemaphoreType.DMA((3,)),
                   pltpu.HBM(g.shape, g.dtype), pltpu.HBM(land.shape, land.dtype)),
        in_specs=(_HBM, _HBM), out_specs=(_SEM, _SEM, _HBM, _HBM),
        input_output_aliases={0: 2, 1: 3},
        compiler_params=pltpu.CompilerParams(has_side_effects=_EFFECT),
    )(_in_hbm(g), _in_hbm(land))


def _scatter_wait(started, after):
    n = len(started)

    def body(*refs):
        gs, lands = refs[:n], refs[n:2 * n]
        sends, recvs = refs[2 * n:3 * n], refs[3 * n:4 * n]
        x, y, c, peers = _place()
        for a in range(n):
            for k, (px, py) in enumerate(peers):
                cp = pltpu.make_async_remote_copy(
                    src_ref=gs[a].at[2 * px + py], dst_ref=lands[a].at[k], send_sem=sends[a].at[k],
                    recv_sem=recvs[a].at[k], device_id=(px, py, c), device_id_type=MESH)
                cp.wait_send()
                cp.wait_recv()

    gs = [s[2] for s in started]
    lands = [s[3] for s in started]
    outs = pl.pallas_call(
        body, name="scatter_wait",
        out_shape=[pltpu.HBM(a.shape, a.dtype) for a in gs + lands],
        in_specs=[_HBM] * (2 * n) + [_SEM] * (2 * n) + [_ANY], out_specs=[_HBM] * (2 * n),
        input_output_aliases={i: i for i in range(2 * n)},
        compiler_params=pltpu.CompilerParams(has_side_effects=_EFFECT),
    )(*gs, *lands, *[s[0] for s in started], *[s[1] for s in started], after)
    return outs[:n], outs[n:]


def _swap_cores(parts):
    n = len(parts)

    def body(*refs):
        ins, outs = refs[:n], refs[n:2 * n]
        send_sems, recv_sems = refs[2 * n:]
        x, y, c = lax.axis_index("x"), lax.axis_index("y"), lax.axis_index("c")
        copies = []
        for a in range(n):
            cp = pltpu.make_async_remote_copy(
                src_ref=ins[a], dst_ref=outs[a], send_sem=send_sems.at[a], recv_sem=recv_sems.at[a],
                device_id=(x, y, 1 - c), device_id_type=MESH)
            cp.start()
            copies.append(cp)
        for cp in copies:
            cp.wait()

    return pl.pallas_call(
        body, name="swap_cores",
        in_specs=[_ANY] * n, out_specs=[_ANY] * n,
        out_shape=[jax.ShapeDtypeStruct(a.shape, a.dtype) for a in parts],
        scratch_shapes=[pltpu.SemaphoreType.DMA((n,)), pltpu.SemaphoreType.DMA((n,))],
    )(*parts)


def _all_reduce_small(v):
    nd, r, w = v.shape
    assert nd == N_DEV

    def body(v_ref, out_ref, land_ref, red_ref, send1, recv1, send2, recv2):
        x, y, c = lax.axis_index("x"), lax.axis_index("y"), lax.axis_index("c")
        me = 4 * x + 2 * y + c
        peers = []
        for k in range(1, N_DEV):
            kx, ky, kc = (k >> 2) & 1, (k >> 1) & 1, k & 1
            peers.append((1 - x if kx else x, 1 - y if ky else y, 1 - c if kc else c))
        first = []
        for k, (px, py, pc) in enumerate(peers):
            cp = pltpu.make_async_remote_copy(
                src_ref=v_ref.at[4 * px + 2 * py + pc], dst_ref=land_ref.at[me], send_sem=send1.at[k],
                recv_sem=recv1.at[k], device_id=(px, py, pc), device_id_type=MESH)
            cp.start()
            first.append(cp)
        land_ref[me] = v_ref[me]
        for cp in first:
            cp.wait()
        acc = land_ref[0]
        for j in range(1, N_DEV):
            acc = acc + land_ref[j]
        red_ref[...] = acc
        second = []
        for k, (px, py, pc) in enumerate(peers):
            cp = pltpu.make_async_remote_copy(
                src_ref=red_ref, dst_ref=out_ref.at[me], send_sem=send2.at[k],
                recv_sem=recv2.at[k], device_id=(px, py, pc), device_id_type=MESH)
            cp.start()
            second.append(cp)
        out_ref[me] = acc
        for cp in second:
            cp.wait()

    vmem = pl.BlockSpec(memory_space=pltpu.VMEM)
    return pl.pallas_call(
        body, name="all_reduce_small",
        in_specs=[vmem], out_specs=vmem,
        out_shape=jax.ShapeDtypeStruct((nd, r, w), F32),
        scratch_shapes=[pltpu.VMEM((nd, r, w), F32), pltpu.VMEM((r, w), F32)]
        + [pltpu.SemaphoreType.DMA((N_DEV - 1,))] * 4,
        compiler_params=pltpu.CompilerParams(vmem_limit_bytes=VMEM_LIMIT),
    )(v)


def _adam_math(w, g, m, v):
    m = ADAM_B1 * m + (1.0 - ADAM_B1) * g
    v = ADAM_B2 * v + (1.0 - ADAM_B2) * (g * g)
    m_hat = m / (1.0 - ADAM_B1 ** ADAM_STEP)
    v_hat = v / (1.0 - ADAM_B2 ** ADAM_STEP)
    delta = -ADAM_LR * (m_hat / (jnp.sqrt(v_hat) + ADAM_EPS) + ADAM_WD * w)
    return delta, m, v


def _sum_blocks(own, got, chip, name):
    _, r, c = own.shape
    tm = _tile(r, ROWS, 16)

    def kern(chip_ref, own_ref, got_ref, o_ref):
        del chip_ref
        acc = own_ref[...].astype(F32)
        for k in range(3):
            acc = acc + got_ref[k].astype(F32)
        o_ref[...] = acc

    return pl.pallas_call(
        kern, name=name,
        grid_spec=pltpu.PrefetchScalarGridSpec(
            num_scalar_prefetch=1, grid=(r // tm,),
            in_specs=[pl.BlockSpec((None, tm, c), lambda i, ch: (ch[0], i, 0)),
                      pl.BlockSpec((3, tm, c), lambda i, ch: (0, i, 0))],
            out_specs=pl.BlockSpec((tm, c), lambda i, ch: (i, 0))),
        out_shape=jax.ShapeDtypeStruct((r, c), F32),
        compiler_params=_params(("parallel",)),
    )(chip, own, got)


def _adamw(p0, p1, w, m, v, name):
    r, c = w.shape
    tm = _tile(r, 128, 8)

    def kern(p0_ref, p1_ref, w_ref, m_ref, v_ref, g_ref, d_ref, mo_ref, vo_ref):
        g = p0_ref[...] + p1_ref[...]
        g_ref[...] = g
        d_ref[...], mo_ref[...], vo_ref[...] = _adam_math(w_ref[...], g, m_ref[...], v_ref[...])

    spec = pl.BlockSpec((tm, c), lambda i: (i, 0))
    return pl.pallas_call(
        kern, name=name, grid=(r // tm,),
        in_specs=[spec] * 5, out_specs=[spec] * 4,
        out_shape=[jax.ShapeDtypeStruct((r, c), F32)] * 4,
        compiler_params=_params(("parallel",)),
    )(p0, p1, w, m, v)


def _adamw_small(g, w, m, v, name):
    def kern(g_ref, w_ref, m_ref, v_ref, d_ref, mo_ref, vo_ref):
        d_ref[...], mo_ref[...], vo_ref[...] = _adam_math(w_ref[...], g_ref[...], m_ref[...], v_ref[...])

    return pl.pallas_call(
        kern, name=name,
        out_shape=[jax.ShapeDtypeStruct(g.shape, F32)] * 3,
        compiler_params=pltpu.CompilerParams(vmem_limit_bytes=VMEM_LIMIT),
    )(g, w, m, v)


def _pack(arrays, rows):
    flat = jnp.concatenate([a.reshape(-1).astype(F32) for a in arrays])
    return jnp.pad(flat, (0, rows * 128 - flat.shape[0])).reshape(rows, 128)


def _unpack(packed, shapes):
    flat = packed.reshape(-1)
    out, off = [], 0
    for shp in shapes:
        size = math.prod(shp)
        out.append(flat[off:off + size].reshape(shp))
        off += size
    return out


_REPLICATED = ["s5_lam_re", "s5_lam_im", "s5_log_dt", "s5_b_re", "s5_b_im", "s5_c_re", "s5_c_im",
               "ffn_norm", "b_norm_mix", "kv_norm", "final_norm"]
_CHIP_VECTORS = ["s5_d", "a_norm_mix"]
_BIG = ["s5_w_glu", "ffn_w_in", "ffn_w_out", "attn_w_q", "attn_w_o", "w_kv"]
_WEIGHT_ORDER = ["s5_lam_re", "s5_lam_im", "s5_log_dt", "s5_b_re", "s5_b_im", "s5_c_re", "s5_c_im", "s5_d",
                 "s5_w_glu", "a_norm_mix", "ffn_norm", "ffn_w_in", "ffn_w_out", "b_norm_mix", "attn_w_q",
                 "attn_w_o", "kv_norm", "w_kv", "final_norm"]


def _step(x, loss_target, w, m, v):
    s, d = x.shape[1], x.shape[2]
    chip = 2 * lax.axis_index("x") + lax.axis_index("y")

    col_sharded = dict(w_glu=w["s5_w_glu"][0], w_in0=w["ffn_w_in"][0], w_in1=w["ffn_w_in"][1], w_q=w["attn_w_q"][0])
    row_sharded = dict(w_out0=w["ffn_w_out"][0], w_out1=w["ffn_w_out"][1], w_o=w["attn_w_o"][0], w_kv=w["w_kv"])
    names = list(col_sharded) + list(row_sharded)
    shards = [a.astype(BF16) for a in col_sharded.values()] + [a.astype(BF16) for a in row_sharded.values()]
    vectors = _all_gather_chips([w["s5_d"], w["a_norm_mix"]])
    send, recv, shard_thru, land_thru, token = _gather_start(shards, _place_own(shards))
    s5_d_full = vectors[0].reshape(1, d)
    a_norm_full = vectors[1].reshape(1, d) + token[0, 0]
    arrived = {}

    def weight(name, after):
        if name not in arrived:
            i = names.index(name)
            land = _gather_wait(shard_thru[i], land_thru[i], send[i], recv[i], after, f"gather_wait_{name}")
            arrived[name] = land if name in col_sharded else land.reshape(1, -1, land.shape[-1])
        return arrived[name]

    started = {}

    def emit(name, dw):
        started[name] = _scatter_start(dw.reshape(N_CHIPS, -1, dw.shape[-1]), f"scatter_start_{name}")

    small = dict(lam_re=w["s5_lam_re"][0], lam_im=w["s5_lam_im"][0], log_dt=w["s5_log_dt"][0],
                 b_re=w["s5_b_re"][0], b_im=w["s5_b_im"][0], c_re=w["s5_c_re"][0], c_im=w["s5_c_im"][0],
                 s5_d=s5_d_full, a_norm=a_norm_full, ffn_norm0=w["ffn_norm"][0:1], ffn_norm1=w["ffn_norm"][1:2],
                 b_norm=w["b_norm_mix"], kv_norm=w["kv_norm"].reshape(1, d), final_norm=w["final_norm"].reshape(1, d))

    loss_rows, grad_x, d_small = _local_step(x[0], loss_target[0], small, weight, emit)

    g4, got = _scatter_wait([started[n] for n in names], grad_x)
    chip_arr = jnp.reshape(chip, (1,)).astype(jnp.int32)
    partial = [_sum_blocks(o, r, chip_arr, f"sum_{n}") for n, o, r in zip(names, g4, got)]
    other = _swap_cores(partial)
    part = dict(zip(names, zip(partial, other)))

    def two_layers(name0, name1):
        (a0, b0), (a1, b1) = part[name0], part[name1]
        return jnp.concatenate([a0, a1], axis=0), jnp.concatenate([b0, b1], axis=0)

    big_parts = {"s5_w_glu": part["w_glu"], "ffn_w_in": two_layers("w_in0", "w_in1"),
                 "ffn_w_out": two_layers("w_out0", "w_out1"), "attn_w_q": part["w_q"],
                 "attn_w_o": part["w_o"], "w_kv": part["w_kv"]}
    result = {}
    for name in _BIG:
        p0, p1 = big_parts[name]
        shape = w[name].shape
        flat = lambda a: a.reshape(-1, shape[-1])
        outs = _adamw(p0, p1, flat(w[name]), flat(m[name]), flat(v[name]), f"adamw_{name}")
        result[name] = [o.reshape(shape) for o in outs]

    rep_grads = [d_small["lam_re"], d_small["lam_im"], d_small["log_dt"], d_small["b_re"], d_small["b_im"],
                 d_small["c_re"], d_small["c_im"],
                 jnp.concatenate([d_small["ffn_norm0"], d_small["ffn_norm1"]], axis=0),
                 d_small["b_norm"], d_small["kv_norm"], d_small["final_norm"]]
    vec_grads = [d_small["s5_d"], d_small["a_norm"]]
    to_reduce = rep_grads + vec_grads + [jnp.sum(loss_rows).reshape(1)]
    total = sum(math.prod(a.shape) for a in to_reduce)
    rows_per = -(-total // (N_DEV * 128 * 8)) * 8
    reduced = _all_reduce_small(_pack(to_reduce, N_DEV * rows_per).reshape(N_DEV, rows_per, 128))
    red = _unpack(reduced, [a.shape for a in to_reduce])
    loss = red[-1][0]
    g_small = dict(zip(_REPLICATED, [r.reshape(w[n].shape) for r, n in zip(red[:len(rep_grads)], _REPLICATED)]))
    for n, r in zip(_CHIP_VECTORS, red[len(rep_grads):-1]):
        g_small[n] = lax.dynamic_slice_in_dim(r.reshape(1, d), chip * (d // N_CHIPS), d // N_CHIPS, axis=1)
    small_names = _REPLICATED + _CHIP_VECTORS
    n_small = sum(math.prod(w[n].shape) for n in small_names)
    rows_small = -(-n_small // (128 * 8)) * 8
    packed = [_pack([src[n] for n in small_names], rows_small) for src in (g_small, w, m, v)]
    upd = _adamw_small(*packed, "adamw_small")
    shapes = [w[n].shape for n in small_names]
    for n, dl, mo, vo in zip(small_names, *[_unpack(u, shapes) for u in upd]):
        result[n] = [g_small[n], dl, mo, vo]

    cols = [[result[n][t] for n in _WEIGHT_ORDER] for t in range(4)]
    return (loss, grad_x.reshape(x.shape), *cols[0], *cols[1], *cols[2], *cols[3])


def kernel(x, s5_lam_re, s5_lam_im, s5_log_dt, s5_b_re, s5_b_im, s5_c_re, s5_c_im, s5_d, s5_w_glu, a_norm_mix, ffn_norm, ffn_w_in, ffn_w_out, b_norm_mix, attn_w_q, attn_w_o, kv_norm, w_kv, final_norm, loss_target, m_s5_lam_re, m_s5_lam_im, m_s5_log_dt, m_s5_b_re, m_s5_b_im, m_s5_c_re, m_s5_c_im, m_s5_d, m_s5_w_glu, m_a_norm_mix, m_ffn_norm, m_ffn_w_in, m_ffn_w_out, m_b_norm_mix, m_attn_w_q, m_attn_w_o, m_kv_norm, m_w_kv, m_final_norm, v_s5_lam_re, v_s5_lam_im, v_s5_log_dt, v_s5_b_re, v_s5_b_im, v_s5_c_re, v_s5_c_im, v_s5_d, v_s5_w_glu, v_a_norm_mix, v_ffn_norm, v_ffn_w_in, v_ffn_w_out, v_b_norm_mix, v_attn_w_q, v_attn_w_o, v_kv_norm, v_w_kv, v_final_norm):
    w = dict(zip(_WEIGHT_ORDER, (s5_lam_re, s5_lam_im, s5_log_dt, s5_b_re, s5_b_im, s5_c_re, s5_c_im, s5_d, s5_w_glu, a_norm_mix, ffn_norm, ffn_w_in, ffn_w_out, b_norm_mix, attn_w_q, attn_w_o, kv_norm, w_kv, final_norm)))
    m = dict(zip(_WEIGHT_ORDER, (m_s5_lam_re, m_s5_lam_im, m_s5_log_dt, m_s5_b_re, m_s5_b_im, m_s5_c_re, m_s5_c_im, m_s5_d, m_s5_w_glu, m_a_norm_mix, m_ffn_norm, m_ffn_w_in, m_ffn_w_out, m_b_norm_mix, m_attn_w_q, m_attn_w_o, m_kv_norm, m_w_kv, m_final_norm)))
    v = dict(zip(_WEIGHT_ORDER, (v_s5_lam_re, v_s5_lam_im, v_s5_log_dt, v_s5_b_re, v_s5_b_im, v_s5_c_re, v_s5_c_im, v_s5_d, v_s5_w_glu, v_a_norm_mix, v_ffn_norm, v_ffn_w_in, v_ffn_w_out, v_b_norm_mix, v_attn_w_q, v_attn_w_o, v_kv_norm, v_w_kv, v_final_norm)))
    return _step(x, loss_target, w, m, v)
```

```python
import functools
import math

import jax
import jax.numpy as jnp
from jax import lax
from jax.experimental import pallas as pl
from jax.experimental.pallas import tpu as pltpu

F32 = jnp.float32
BF16 = jnp.bfloat16

S5_GROUP_CH = 16
S5_STATE = 64
GROUPS_PER_TILE = 8
HEAD_DIM = 128
N_KV_HEADS = 4
PATTERNS = ((128, 1), (512, 4), (2048, 16))
ATT_BLK = 128
EPS = 1e-6
NEG_INF = -1e30
SCAN_T = 128
ADAM_LR = 0.001
ADAM_B1 = 0.9
ADAM_B2 = 0.999
ADAM_EPS = 1e-08
ADAM_WD = 0.01
ADAM_STEP = 10
N_CHIPS = 4
N_DEV = 8
VMEM_LIMIT = 56 * 1024 * 1024
MESH = pl.DeviceIdType.MESH
GELU_K = math.sqrt(2.0 / math.pi)
GELU_C = 0.044715


def _tile(n, pref, unit=128):
    if n <= pref:
        return n
    best = None
    t = unit
    while t <= pref:
        if n % t == 0:
            best = t
        t += unit
    assert best is not None, (n, pref, unit)
    return best


def _params(sem):
    return pltpu.CompilerParams(dimension_semantics=sem, vmem_limit_bytes=VMEM_LIMIT)


def _dot(a, b):
    return jnp.dot(a, b, preferred_element_type=F32)


def _dot_nt(a, b):
    return lax.dot_general(a, b, (((1,), (1,)), ((), ())), preferred_element_type=F32)


def _dot_tn(a, b):
    return lax.dot_general(a, b, (((0,), (0,)), ((), ())), preferred_element_type=F32)


def _mm_nn(a, w, out_dtype, name, tm=512, tn=1536, tk=2048):
    m, k = a.shape
    nb, k2, nq = w.shape
    assert k == k2
    tm, tn, tk = _tile(m, tm, 8), _tile(nq, tn), _tile(k, tk)
    per, nk = nq // tn, k // tk

    def kern(a_ref, w_ref, o_ref, *acc):
        p = _dot(a_ref[...], w_ref[...])
        if nk == 1:
            o_ref[...] = p.astype(o_ref.dtype)
        else:
            acc_ref, = acc
            kk = pl.program_id(2)

            @pl.when(kk == 0)
            def _():
                acc_ref[...] = p

            @pl.when(kk > 0)
            def _():
                acc_ref[...] += p

            @pl.when(kk == nk - 1)
            def _():
                o_ref[...] = acc_ref[...].astype(o_ref.dtype)

    return pl.pallas_call(
        kern, name=name, grid=(m // tm, nb * per, nk),
        in_specs=[pl.BlockSpec((tm, tk), lambda i, j, kk: (i, kk)),
                  pl.BlockSpec((None, tk, tn), lambda i, j, kk: (j // per, kk, j % per))],
        out_specs=pl.BlockSpec((tm, tn), lambda i, j, kk: (i, j)),
        out_shape=jax.ShapeDtypeStruct((m, nb * nq), out_dtype),
        scratch_shapes=[] if nk == 1 else [pltpu.VMEM((tm, tn), F32)],
        compiler_params=_params(("parallel", "parallel", "arbitrary")),
    )(a, w)


def _mm_nt(a, w, name, tm=512, tr=1536, tkc=2048):
    m, n = a.shape
    nb, k, nq = w.shape
    assert n == nb * nq
    tm, tr, tkc = _tile(m, tm, 8), _tile(nq, tr), _tile(k, tkc)
    per = nq // tr
    nr = nb * per

    def kern(a_ref, w_ref, o_ref, *acc):
        p = _dot_nt(a_ref[...], w_ref[...])
        if nr == 1:
            o_ref[...] = p
        else:
            acc_ref, = acc
            r = pl.program_id(2)

            @pl.when(r == 0)
            def _():
                acc_ref[...] = p

            @pl.when(r > 0)
            def _():
                acc_ref[...] += p

            @pl.when(r == nr - 1)
            def _():
                o_ref[...] = acc_ref[...]

    return pl.pallas_call(
        kern, name=name, grid=(m // tm, k // tkc, nr),
        in_specs=[pl.BlockSpec((tm, tr), lambda i, kc, r: (i, r)),
                  pl.BlockSpec((None, tkc, tr), lambda i, kc, r: (r // per, kc, r % per))],
        out_specs=pl.BlockSpec((tm, tkc), lambda i, kc, r: (i, kc)),
        out_shape=jax.ShapeDtypeStruct((m, k), F32),
        scratch_shapes=[] if nr == 1 else [pltpu.VMEM((tm, tkc), F32)],
        compiler_params=_params(("parallel", "parallel", "arbitrary")),
    )(a, w)


def _mm_tn(a, dy, nb, name, ts=512, tkk=1024, tn=1536):
    s, k = a.shape
    s2, n = dy.shape
    assert s == s2 and n % nb == 0
    nq = n // nb
    ts, tkk, tn = _tile(s, ts, 16), _tile(k, tkk), _tile(nq, tn)
    per, ns = nq // tn, s // ts

    def kern(a_ref, dy_ref, o_ref, acc_ref):
        si = pl.program_id(2)
        p = _dot_tn(a_ref[...], dy_ref[...])

        @pl.when(si == 0)
        def _():
            acc_ref[...] = p

        @pl.when(si > 0)
        def _():
            acc_ref[...] += p

        @pl.when(si == ns - 1)
        def _():
            o_ref[...] = acc_ref[...].astype(o_ref.dtype)

    return pl.pallas_call(
        kern, name=name, grid=(k // tkk, nb * per, ns),
        in_specs=[pl.BlockSpec((ts, tkk), lambda kk, j, si: (si, kk)),
                  pl.BlockSpec((ts, tn), lambda kk, j, si: (si, j))],
        out_specs=pl.BlockSpec((None, tkk, tn), lambda kk, j, si: (j // per, kk, j % per)),
        out_shape=jax.ShapeDtypeStruct((nb, k, nq), BF16),
        scratch_shapes=[pltpu.VMEM((tkk, tn), F32)],
        compiler_params=_params(("parallel", "parallel", "arbitrary")),
    )(a, dy)


ROWS = 256


def _rms(x, g):
    r = lax.rsqrt(jnp.mean(x * x, axis=-1, keepdims=True) + EPS)
    return x * r * g


def _rms_bwd(x, g, dh):
    r = lax.rsqrt(jnp.mean(x * x, axis=-1, keepdims=True) + EPS)
    xh = x * r
    dgx = dh * g
    dx = r * (dgx - xh * jnp.mean(dgx * xh, axis=-1, keepdims=True))
    return dx, dh * xh


def _sigmoid(x):
    return 1.0 / (1.0 + jnp.exp(-x))


def _gelu(y):
    return 0.5 * y * (1.0 + jnp.tanh(GELU_K * (y + GELU_C * y * y * y)))


def _gelu_grad(y):
    t = jnp.tanh(GELU_K * (y + GELU_C * y * y * y))
    return 0.5 * (1.0 + t) + 0.5 * y * (1.0 - t * t) * GELU_K * (1.0 + 3.0 * GELU_C * y * y)


def _row_spec(tm, d, col=0):
    return pl.BlockSpec((tm, d), lambda i: (i, col))


def _vec_spec(d):
    return pl.BlockSpec((1, d), lambda i: (0, 0))


def _acc_rows(ref, val, i):
    s = jnp.sum(val, axis=0, keepdims=True)

    @pl.when(i == 0)
    def _():
        ref[...] = s

    @pl.when(i > 0)
    def _():
        ref[...] += s


def _norm_f32(x, g, name):
    s, d = x.shape
    tm = _tile(s, ROWS, 8)

    def kern(x_ref, g_ref, h_ref):
        h_ref[...] = _rms(x_ref[...], g_ref[...])

    return pl.pallas_call(
        kern, name=name, grid=(s // tm,),
        in_specs=[_row_spec(tm, d), _vec_spec(d)],
        out_specs=_row_spec(tm, d),
        out_shape=jax.ShapeDtypeStruct((s, d), F32),
        compiler_params=_params(("parallel",)),
    )(x, g)


def _glu_res_norm(vg, x, g, name):
    s, d = x.shape
    tm = _tile(s, ROWS, 16)

    def kern(val_ref, gate_ref, x_ref, g_ref, x1_ref, hf_ref):
        x1 = x_ref[...] + val_ref[...] * _sigmoid(gate_ref[...])
        x1_ref[...] = x1
        hf_ref[...] = _rms(x1, g_ref[...]).astype(BF16)

    return pl.pallas_call(
        kern, name=name, grid=(s // tm,),
        in_specs=[_row_spec(tm, d, 0), _row_spec(tm, d, 1), _row_spec(tm, d), _vec_spec(d)],
        out_specs=[_row_spec(tm, d), _row_spec(tm, d)],
        out_shape=[jax.ShapeDtypeStruct((s, d), F32), jax.ShapeDtypeStruct((s, d), BF16)],
        compiler_params=_params(("parallel",)),
    )(vg, vg, x, g)


def _swiglu(gu, name):
    s, f2 = gu.shape
    f = f2 // 2
    tm, tc = _tile(s, ROWS, 16), _tile(f, 1024)
    nc = f // tc

    def kern(g_ref, u_ref, a_ref):
        g = g_ref[...]
        a_ref[...] = (g * _sigmoid(g) * u_ref[...]).astype(BF16)

    return pl.pallas_call(
        kern, name=name, grid=(s // tm, nc),
        in_specs=[pl.BlockSpec((tm, tc), lambda i, j: (i, j)),
                  pl.BlockSpec((tm, tc), lambda i, j: (i, j + nc))],
        out_specs=pl.BlockSpec((tm, tc), lambda i, j: (i, j)),
        out_shape=jax.ShapeDtypeStruct((s, f), BF16),
        compiler_params=_params(("parallel", "parallel")),
    )(gu, gu)


def _swiglu_bwd(da, gu, name):
    s, f = da.shape
    tm, tc = _tile(s, ROWS, 16), _tile(f, 1024)
    nc = f // tc

    def kern(da_ref, g_ref, u_ref, dg_ref, du_ref):
        g = g_ref[...]
        da_ = da_ref[...]
        sg = _sigmoid(g)
        dg_ref[...] = (da_ * u_ref[...] * sg * (1.0 + g * (1.0 - sg))).astype(BF16)
        du_ref[...] = (da_ * g * sg).astype(BF16)

    spec = pl.BlockSpec((tm, tc), lambda i, j: (i, j))
    return pl.pallas_call(
        kern, name=name, grid=(s // tm, nc),
        in_specs=[spec, spec, pl.BlockSpec((tm, tc), lambda i, j: (i, j + nc))],
        out_specs=[spec, spec],
        out_shape=[jax.ShapeDtypeStruct((s, f), BF16)] * 2,
        compiler_params=_params(("parallel", "parallel")),
    )(da, gu, gu)


def _res_norm(o, x, gains, name):
    s, d = x.shape
    tm = _tile(s, ROWS, 16)
    ng = len(gains)

    def kern(o_ref, x_ref, *refs):
        xn = x_ref[...] + o_ref[...]
        refs[ng][...] = xn
        for t in range(ng):
            refs[ng + 1 + t][...] = _rms(xn, refs[t][...]).astype(BF16)

    return pl.pallas_call(
        kern, name=name, grid=(s // tm,),
        in_specs=[_row_spec(tm, d), _row_spec(tm, d)] + [_vec_spec(d)] * ng,
        out_specs=[_row_spec(tm, d)] * (1 + ng),
        out_shape=[jax.ShapeDtypeStruct((s, d), F32)] + [jax.ShapeDtypeStruct((s, d), BF16)] * ng,
        compiler_params=_params(("parallel",)),
    )(o, x, *gains)


def _loss_head(o, x, g, target, name):
    s, d = x.shape
    tm = _tile(s, ROWS, 16)

    def kern(o_ref, x_ref, g_ref, t_ref, dx_ref, dxb_ref, loss_ref, dg_ref):
        i = pl.program_id(0)
        x4 = x_ref[...] + o_ref[...]
        gg = g_ref[...]
        diff = _rms(x4, gg) - t_ref[...]
        dx, dgr = _rms_bwd(x4, gg, diff * (1.0 / d))
        dx_ref[...] = dx
        dxb_ref[...] = dx.astype(BF16)
        _acc_rows(loss_ref, diff * diff * (0.5 / d), i)
        _acc_rows(dg_ref, dgr, i)

    return pl.pallas_call(
        kern, name=name, grid=(s // tm,),
        in_specs=[_row_spec(tm, d), _row_spec(tm, d), _vec_spec(d), _row_spec(tm, d)],
        out_specs=[_row_spec(tm, d), _row_spec(tm, d), _vec_spec(d), _vec_spec(d)],
        out_shape=[jax.ShapeDtypeStruct((s, d), F32), jax.ShapeDtypeStruct((s, d), BF16),
                   jax.ShapeDtypeStruct((1, d), F32), jax.ShapeDtypeStruct((1, d), F32)],
        compiler_params=_params(("arbitrary",)),
    )(o, x, g, target)


def _norm_bwd(x, gains, dhs, dres, name):
    s, d = x.shape
    tm = _tile(s, ROWS, 16)
    ng = len(gains)

    def kern(x_ref, dres_ref, *refs):
        i = pl.program_id(0)
        x_ = x_ref[...]
        acc = dres_ref[...]
        for t in range(ng):
            dx, dgr = _rms_bwd(x_, refs[t][...], refs[ng + t][...])
            acc = acc + dx
            _acc_rows(refs[2 * ng + 2 + t], dgr, i)
        refs[2 * ng][...] = acc
        refs[2 * ng + 1][...] = acc.astype(BF16)

    return pl.pallas_call(
        kern, name=name, grid=(s // tm,),
        in_specs=[_row_spec(tm, d), _row_spec(tm, d)] + [_vec_spec(d)] * ng + [_row_spec(tm, d)] * ng,
        out_specs=[_row_spec(tm, d), _row_spec(tm, d)] + [_vec_spec(d)] * ng,
        out_shape=[jax.ShapeDtypeStruct((s, d), F32), jax.ShapeDtypeStruct((s, d), BF16)]
        + [jax.ShapeDtypeStruct((1, d), F32)] * ng,
        compiler_params=_params(("arbitrary",)),
    )(x, dres, *gains, *dhs)


def _glu_bwd(dmix, vg, name):
    s, d = dmix.shape
    tm = _tile(s, ROWS, 16)

    def kern(dm_ref, val_ref, gate_ref, o_ref):
        dm = dm_ref[...]
        sg = _sigmoid(gate_ref[...])
        o_ref[:, :d] = (dm * sg).astype(BF16)
        o_ref[:, d:] = (dm * val_ref[...] * sg * (1.0 - sg)).astype(BF16)

    return pl.pallas_call(
        kern, name=name, grid=(s // tm,),
        in_specs=[_row_spec(tm, d), _row_spec(tm, d, 0), _row_spec(tm, d, 1)],
        out_specs=_row_spec(tm, 2 * d),
        out_shape=jax.ShapeDtypeStruct((s, 2 * d), BF16),
        compiler_params=_params(("parallel",)),
    )(dmix, vg, vg)


def _shift_down(v, sh, rows):
    return jnp.where(rows >= sh, pltpu.roll(v, sh, 0), 0.0)


def _shift_up(v, sh, rows, t):
    return jnp.where(rows < t - sh, pltpu.roll(v, t - sh, 0), 0.0)


def _s5_fwd(u, bbd_re, bbd_im, cbd_re, cbd_im, pw_re, pw_im, dskip, name):
    s, d = u.shape
    nt, cw, lw = bbd_re.shape
    t = pw_re.shape[0]
    nc = s // t

    def kern(u_ref, bre_ref, bim_ref, cre_ref, cim_ref, pre_ref, pim_ref, d_ref,
             xr_ref, xi_ref, y_ref, z_ref, car_ref, cai_ref):
        c = pl.program_id(1)

        @pl.when(c == 0)
        def _():
            car_ref[...] = jnp.zeros_like(car_ref)
            cai_ref[...] = jnp.zeros_like(cai_ref)

        u_ = u_ref[...]
        ub = u_.astype(BF16)
        xr = _dot(ub, bre_ref[...])
        xi = _dot(ub, bim_ref[...])
        rows = lax.broadcasted_iota(jnp.int32, (t, lw), 0)
        sh = 1
        while sh < t:
            ar = pre_ref[sh - 1:sh, :]
            ai = pim_ref[sh - 1:sh, :]
            sr = _shift_down(xr, sh, rows)
            si = _shift_down(xi, sh, rows)
            xr, xi = xr + ar * sr - ai * si, xi + ar * si + ai * sr
            sh *= 2
        cr = car_ref[...]
        ci = cai_ref[...]
        pr = pre_ref[...]
        pi = pim_ref[...]
        xr, xi = xr + pr * cr - pi * ci, xi + pr * ci + pi * cr
        car_ref[...] = xr[t - 1:t, :]
        cai_ref[...] = xi[t - 1:t, :]
        xr_ref[...] = xr
        xi_ref[...] = xi
        y = _dot(xr.astype(BF16), cre_ref[...]) - _dot(xi.astype(BF16), cim_ref[...]) + d_ref[...] * u_
        y_ref[...] = y
        z_ref[...] = _gelu(y).astype(BF16)

    tok = pl.BlockSpec((t, cw), lambda j, c: (c, j))
    st = pl.BlockSpec((t, lw), lambda j, c: (c, j))
    return pl.pallas_call(
        kern, name=name, grid=(nt, nc),
        in_specs=[tok,
                  pl.BlockSpec((None, cw, lw), lambda j, c: (j, 0, 0)),
                  pl.BlockSpec((None, cw, lw), lambda j, c: (j, 0, 0)),
                  pl.BlockSpec((None, lw, cw), lambda j, c: (j, 0, 0)),
                  pl.BlockSpec((None, lw, cw), lambda j, c: (j, 0, 0)),
                  pl.BlockSpec((t, lw), lambda j, c: (0, j)),
                  pl.BlockSpec((t, lw), lambda j, c: (0, j)),
                  pl.BlockSpec((1, cw), lambda j, c: (0, j))],
        out_specs=[st, st, tok, tok],
        out_shape=[jax.ShapeDtypeStruct((s, nt * lw), F32), jax.ShapeDtypeStruct((s, nt * lw), F32),
                   jax.ShapeDtypeStruct((s, d), F32), jax.ShapeDtypeStruct((s, d), BF16)],
        scratch_shapes=[pltpu.VMEM((1, lw), F32), pltpu.VMEM((1, lw), F32)],
        compiler_params=_params(("parallel", "arbitrary")),
    )(u, bbd_re, bbd_im, cbd_re, cbd_im, pw_re, pw_im, dskip)


def _s5_bwd(dz, y, u, xs_re, xs_im, bbd_re, bbd_im, cbd_re, cbd_im, pw_re, pw_im, pf_re, pf_im, dskip, name):
    s, d = u.shape
    nt, cw, lw = bbd_re.shape
    t = pw_re.shape[0]
    nc = s // t

    def kern(dz_ref, y_ref, u_ref, xr_ref, xi_ref, bre_ref, bim_ref, cre_ref, cim_ref,
             pre_ref, pim_ref, fre_ref, fim_ref, d_ref,
             du_ref, dd_ref, dcr_ref, dci_ref, dbr_ref, dbi_ref, dar_ref, dai_ref, car_ref, cai_ref):
        c = pl.program_id(1)

        @pl.when(c == 0)
        def _():
            car_ref[...] = jnp.zeros_like(car_ref)
            cai_ref[...] = jnp.zeros_like(cai_ref)

        u_ = u_ref[...]
        ub = u_.astype(BF16)
        dy = dz_ref[...] * _gelu_grad(y_ref[...])
        dyb = dy.astype(BF16)
        gr = _dot_nt(dyb, cre_ref[...])
        gi = -_dot_nt(dyb, cim_ref[...])
        rows = lax.broadcasted_iota(jnp.int32, (t, lw), 0)
        sh = 1
        while sh < t:
            ar = pre_ref[sh - 1:sh, :]
            ai = pim_ref[sh - 1:sh, :]
            sr = _shift_up(gr, sh, rows, t)
            si = _shift_up(gi, sh, rows, t)
            gr, gi = gr + ar * sr + ai * si, gi + ar * si - ai * sr
            sh *= 2
        cr = car_ref[...]
        ci = cai_ref[...]
        fr = fre_ref[...]
        fi = fim_ref[...]
        gr, gi = gr + fr * cr + fi * ci, gi + fr * ci - fi * cr
        gsr = jnp.where(rows < t - 1, pltpu.roll(gr, t - 1, 0), cr)
        gsi = jnp.where(rows < t - 1, pltpu.roll(gi, t - 1, 0), ci)
        car_ref[...] = gr[0:1, :]
        cai_ref[...] = gi[0:1, :]
        xr = xr_ref[...]
        xi = xi_ref[...]
        dar = jnp.sum(gsr * xr + gsi * xi, axis=0, keepdims=True)
        dai = jnp.sum(gsi * xr - gsr * xi, axis=0, keepdims=True)
        grb = gr.astype(BF16)
        gib = gi.astype(BF16)
        dbr = _dot_tn(ub, grb)
        dbi = _dot_tn(ub, gib)
        dcr = _dot_tn(dyb, xr.astype(BF16))
        dci = _dot_tn(dyb, xi.astype(BF16))
        du_ref[...] = dy * d_ref[...] + _dot_nt(grb, bre_ref[...]) + _dot_nt(gib, bim_ref[...])
        ddv = jnp.sum(dy * u_, axis=0, keepdims=True)

        @pl.when(c == 0)
        def _():
            dd_ref[...] = ddv
            dcr_ref[...] = dcr
            dci_ref[...] = dci
            dbr_ref[...] = dbr
            dbi_ref[...] = dbi
            dar_ref[...] = dar
            dai_ref[...] = dai

        @pl.when(c > 0)
        def _():
            dd_ref[...] += ddv
            dcr_ref[...] += dcr
            dci_ref[...] += dci
            dbr_ref[...] += dbr
            dbi_ref[...] += dbi
            dar_ref[...] += dar
            dai_ref[...] += dai

    tok = pl.BlockSpec((t, cw), lambda j, c: (nc - 1 - c, j))
    st = pl.BlockSpec((t, lw), lambda j, c: (nc - 1 - c, j))
    wb = pl.BlockSpec((None, cw, lw), lambda j, c: (j, 0, 0))
    wc = pl.BlockSpec((None, lw, cw), lambda j, c: (j, 0, 0))
    pw = pl.BlockSpec((t, lw), lambda j, c: (0, j))
    vec_c = pl.BlockSpec((1, cw), lambda j, c: (0, j))
    vec_l = pl.BlockSpec((1, lw), lambda j, c: (0, j))
    return pl.pallas_call(
        kern, name=name, grid=(nt, nc),
        in_specs=[tok, tok, tok, st, st, wb, wb, wc, wc, pw, pw, pw, pw, vec_c],
        out_specs=[tok, vec_c, wb, wb, wb, wb, vec_l, vec_l],
        out_shape=[jax.ShapeDtypeStruct((s, d), F32), jax.ShapeDtypeStruct((1, d), F32)]
        + [jax.ShapeDtypeStruct((nt, cw, lw), F32)] * 4
        + [jax.ShapeDtypeStruct((1, nt * lw), F32)] * 2,
        scratch_shapes=[pltpu.VMEM((1, lw), F32), pltpu.VMEM((1, lw), F32)],
        compiler_params=_params(("parallel", "arbitrary")),
    )(dz, y, u, xs_re, xs_im, bbd_re, bbd_im, cbd_re, cbd_im, pw_re, pw_im, pf_re, pf_im, dskip)


def _s5_discretize(lam_re, lam_im, log_dt, b_re, b_im):
    dt = jnp.exp(log_dt)[:, None]
    mag = jnp.exp(lam_re * dt)
    ang = lam_im * dt
    lb_re = mag * jnp.cos(ang)
    lb_im = mag * jnp.sin(ang)
    nr = lb_re - 1.0
    den = lam_re * lam_re + lam_im * lam_im
    f_re = (nr * lam_re + lb_im * lam_im) / den
    f_im = (lb_im * lam_re - nr * lam_im) / den
    bb_re = f_re[..., None] * b_re - f_im[..., None] * b_im
    bb_im = f_re[..., None] * b_im + f_im[..., None] * b_re
    return lb_re, lb_im, bb_re, bb_im


def _block_diag(w):
    nt, ng, a, b = w.shape
    eye = jnp.eye(ng, dtype=w.dtype)
    return (w[:, :, :, None, :] * eye[None, :, None, :, None]).reshape(nt, ng * a, ng * b)


def _block_diag_take(w, a, b):
    nt = w.shape[0]
    ng = GROUPS_PER_TILE
    w5 = w.reshape(nt, ng, a, ng, b)
    idx = jnp.arange(ng)
    return w5[:, idx, :, idx, :].transpose(1, 0, 2, 3).reshape(nt * ng, a, b)


def _att_masks(jb):
    qi = lax.broadcasted_iota(jnp.int32, (ATT_BLK, ATT_BLK), 0)
    si = lax.broadcasted_iota(jnp.int32, (ATT_BLK, ATT_BLK), 1)
    return si <= qi, (si >= qi) & (jb > 0)


def _att_fwd(q, kv, gi, dil, nh, name):
    s = q.shape[0]
    n = s // dil
    nb = n // ATT_BLK
    ng = q.shape[1] // (nh * HEAD_DIM)
    rep = nh // N_KV_HEADS
    qw = rep * HEAD_DIM
    scale = HEAD_DIM ** -0.5
    q2 = q.reshape(n, dil * q.shape[1])
    kv2 = kv.reshape(n, dil * kv.shape[1])

    def kern(q_ref, kc_ref, kp_ref, vc_ref, vp_ref, o_ref, l_ref):
        jb = pl.program_id(2)
        mc, mp = _att_masks(jb)
        kc, kp, vc, vp = kc_ref[...], kp_ref[...], vc_ref[...], vp_ref[...]
        for g in range(rep):
            cols = slice(g * HEAD_DIM, (g + 1) * HEAD_DIM)
            qg = q_ref[:, cols]
            sc = jnp.where(mc, _dot_nt(qg, kc) * scale, NEG_INF)
            sp = jnp.where(mp, _dot_nt(qg, kp) * scale, NEG_INF)
            m = jnp.maximum(jnp.max(sc, axis=-1, keepdims=True), jnp.max(sp, axis=-1, keepdims=True))
            pc = jnp.exp(sc - m)
            pp = jnp.exp(sp - m)
            l = jnp.sum(pc, axis=-1, keepdims=True) + jnp.sum(pp, axis=-1, keepdims=True)
            o_ref[:, cols] = _dot((pc / l).astype(BF16), vc) + _dot((pp / l).astype(BF16), vp)
            l_ref[:, cols] = jnp.broadcast_to(m + jnp.log(l), (ATT_BLK, HEAD_DIM))

    def kv_spec(off, prev):
        if prev:
            return pl.BlockSpec((ATT_BLK, HEAD_DIM), lambda r, h, jb: (jnp.maximum(jb - 1, 0), r * 2 * N_KV_HEADS + off + h))
        return pl.BlockSpec((ATT_BLK, HEAD_DIM), lambda r, h, jb: (jb, r * 2 * N_KV_HEADS + off + h))

    out_spec = pl.BlockSpec((ATT_BLK, qw), lambda r, h, jb: (jb, r * N_KV_HEADS + h))
    out, lse = pl.pallas_call(
        kern, name=name, grid=(dil, N_KV_HEADS, nb),
        in_specs=[pl.BlockSpec((ATT_BLK, qw), lambda r, h, jb: (jb, (r * ng + gi) * N_KV_HEADS + h)),
                  kv_spec(0, False), kv_spec(0, True), kv_spec(N_KV_HEADS, False), kv_spec(N_KV_HEADS, True)],
        out_specs=[out_spec, out_spec],
        out_shape=[jax.ShapeDtypeStruct((n, dil * nh * HEAD_DIM), F32)] * 2,
        compiler_params=_params(("parallel", "parallel", "arbitrary")),
    )(q2, kv2, kv2, kv2, kv2)
    return out.reshape(s, nh * HEAD_DIM), lse.reshape(s, nh * HEAD_DIM)


def _att_combine(outs, lses, name):
    s, d = outs[0].shape
    tm = _tile(s, ROWS, 16)
    ng = len(outs)

    def kern(*refs):
        ls = [refs[ng + t][...] for t in range(ng)]
        mx = functools.reduce(jnp.maximum, ls)
        es = [jnp.exp(l - mx) for l in ls]
        den = functools.reduce(lambda a, b: a + b, es)
        o = functools.reduce(lambda a, b: a + b, [es[t] / den * refs[t][...] for t in range(ng)])
        refs[2 * ng][...] = o
        refs[2 * ng + 1][...] = o.astype(BF16)

    return pl.pallas_call(
        kern, name=name, grid=(s // tm,),
        in_specs=[_row_spec(tm, d)] * (2 * ng),
        out_specs=[_row_spec(tm, d)] * 2,
        out_shape=[jax.ShapeDtypeStruct((s, d), F32), jax.ShapeDtypeStruct((s, d), BF16)],
        compiler_params=_params(("parallel",)),
    )(*outs, *lses)


def _att_bwd(q, kv, do, o, lses, gi, dil, nh, name):
    s = q.shape[0]
    n = s // dil
    nb = n // ATT_BLK
    ng = len(lses)
    rep = nh // N_KV_HEADS
    qw = rep * HEAD_DIM
    dm = nh * HEAD_DIM
    scale = HEAD_DIM ** -0.5
    q2 = q.reshape(n, dil * q.shape[1])
    kv2 = kv.reshape(n, dil * kv.shape[1])
    wide = [a.reshape(n, dil * dm) for a in (do, o, *lses)]

    def kern(q_ref, kc_ref, kp_ref, vc_ref, vp_ref, do_ref, o_ref, *refs):
        l_refs = refs[:ng]
        dq_ref, dk_ref, dv_ref, ck_ref, cv_ref = refs[ng:]
        jb = pl.program_id(2)

        @pl.when(jb == 0)
        def _():
            ck_ref[...] = jnp.zeros_like(ck_ref)
            cv_ref[...] = jnp.zeros_like(cv_ref)

        @pl.when(jb < nb)
        def _():
            mc, mp = _att_masks(jb)
            kc, kp, vc, vp = kc_ref[...], kp_ref[...], vc_ref[...], vp_ref[...]
            ls = [r[...] for r in l_refs]
            mx = functools.reduce(jnp.maximum, ls)
            den = functools.reduce(lambda a, b: a + b, [jnp.exp(l - mx) for l in ls])
            lse_g = ls[gi]
            w = jnp.exp(lse_g - mx) / den
            do_ = do_ref[...]
            dout = w * do_
            doo = do_ * o_ref[...]
            dkc = jnp.zeros((ATT_BLK, HEAD_DIM), F32)
            dkp = jnp.zeros((ATT_BLK, HEAD_DIM), F32)
            dvc = jnp.zeros((ATT_BLK, HEAD_DIM), F32)
            dvp = jnp.zeros((ATT_BLK, HEAD_DIM), F32)
            for g in range(rep):
                cols = slice(g * HEAD_DIM, (g + 1) * HEAD_DIM)
                qg = q_ref[:, cols]
                lg = lse_g[:, g * HEAD_DIM:g * HEAD_DIM + 1]
                ct = w[:, g * HEAD_DIM:g * HEAD_DIM + 1] * jnp.sum(doo[:, cols], axis=-1, keepdims=True)
                dob = dout[:, cols].astype(BF16)
                pc = jnp.exp(jnp.where(mc, _dot_nt(qg, kc) * scale, NEG_INF) - lg)
                pp = jnp.exp(jnp.where(mp, _dot_nt(qg, kp) * scale, NEG_INF) - lg)
                dsc = (pc * (_dot_nt(dob, vc) - ct) * scale).astype(BF16)
                dsp = (pp * (_dot_nt(dob, vp) - ct) * scale).astype(BF16)
                dq_ref[:, cols] = (_dot(dsc, kc) + _dot(dsp, kp)).astype(BF16)
                dkc = dkc + _dot_tn(dsc, qg)
                dkp = dkp + _dot_tn(dsp, qg)
                dvc = dvc + _dot_tn(pc.astype(BF16), dob)
                dvp = dvp + _dot_tn(pp.astype(BF16), dob)
            dk_ref[...] = ck_ref[...] + dkp
            dv_ref[...] = cv_ref[...] + dvp
            ck_ref[...] = dkc
            cv_ref[...] = dvc

        @pl.when(jb == nb)
        def _():
            dk_ref[...] = ck_ref[...]
            dv_ref[...] = cv_ref[...]

    def jq(jb):
        return jnp.minimum(jb, nb - 1)

    def kv_spec(off, prev):
        if prev:
            return pl.BlockSpec((ATT_BLK, HEAD_DIM), lambda r, h, jb: (jnp.maximum(jq(jb) - 1, 0), r * 2 * N_KV_HEADS + off + h))
        return pl.BlockSpec((ATT_BLK, HEAD_DIM), lambda r, h, jb: (jq(jb), r * 2 * N_KV_HEADS + off + h))

    wide_spec = pl.BlockSpec((ATT_BLK, qw), lambda r, h, jb: (jq(jb), r * N_KV_HEADS + h))
    dkv_spec = pl.BlockSpec((ATT_BLK, HEAD_DIM), lambda r, h, jb: (jnp.maximum(jb - 1, 0), r * N_KV_HEADS + h))
    n_q_groups = q.shape[1] // dm
    dq, dk, dv = pl.pallas_call(
        kern, name=name, grid=(dil, N_KV_HEADS, nb + 1),
        in_specs=[pl.BlockSpec((ATT_BLK, qw), lambda r, h, jb: (jq(jb), (r * n_q_groups + gi) * N_KV_HEADS + h)),
                  kv_spec(0, False), kv_spec(0, True), kv_spec(N_KV_HEADS, False), kv_spec(N_KV_HEADS, True)]
        + [wide_spec] * (2 + ng),
        out_specs=[wide_spec, dkv_spec, dkv_spec],
        out_shape=[jax.ShapeDtypeStruct((n, dil * dm), BF16),
                   jax.ShapeDtypeStruct((n, dil * N_KV_HEADS * HEAD_DIM), F32),
                   jax.ShapeDtypeStruct((n, dil * N_KV_HEADS * HEAD_DIM), F32)],
        scratch_shapes=[pltpu.VMEM((ATT_BLK, HEAD_DIM), F32), pltpu.VMEM((ATT_BLK, HEAD_DIM), F32)],
        compiler_params=_params(("parallel", "parallel", "arbitrary")),
    )(q2, kv2, kv2, kv2, kv2, *wide)
    return dq.reshape(s, dm), dk.reshape(s, -1), dv.reshape(s, -1)


def _dkv_sum(dks, dvs, name):
    s, w = dks[0].shape
    tm = _tile(s, ROWS, 16)
    ng = len(dks)

    def kern(*refs):
        o_ref = refs[2 * ng]
        o_ref[:, :w] = functools.reduce(lambda a, b: a + b, [refs[t][...] for t in range(ng)]).astype(BF16)
        o_ref[:, w:] = functools.reduce(lambda a, b: a + b, [refs[ng + t][...] for t in range(ng)]).astype(BF16)

    return pl.pallas_call(
        kern, name=name, grid=(s // tm,),
        in_specs=[_row_spec(tm, w)] * (2 * ng),
        out_specs=_row_spec(tm, 2 * w),
        out_shape=jax.ShapeDtypeStruct((s, 2 * w), BF16),
        compiler_params=_params(("parallel",)),
    )(*dks, *dvs)


def _ffn_fwd(hf, weight, layer, tag):
    gu = _mm_nn(hf, weight(f"w_in{layer}", hf), F32, f"{tag}_in", tn=1408)
    a = _swiglu(gu, f"{tag}_act")
    o = _mm_nn(a, weight(f"w_out{layer}", a), F32, f"{tag}_out", tn=1024, tk=1408)
    return gu, a, o


def _ffn_bwd(dresb, hf, gu, a, weight, emit, layer, tag):
    w_in, w_out = weight(f"w_in{layer}", None), weight(f"w_out{layer}", None)
    zero = emit(f"w_out{layer}", _mm_tn(a, dresb, 1, f"{tag}_out_dw", tkk=1408, tn=1024))
    da = _mm_nt(dresb, w_out, f"{tag}_out_dx", tr=2048, tkc=1408)
    dgate, dup = _swiglu_bwd(da, gu, f"{tag}_act_bwd")
    dgu = jnp.concatenate([dgate, dup], axis=1)
    zero = zero + emit(f"w_in{layer}", _mm_tn(hf, dgu, w_in.shape[0], f"{tag}_in_dw", tn=1408))
    return _mm_nt(dgu, w_in, f"{tag}_in_dx", tr=1408), zero


def _local_step(x, target, small, weight, emit):
    s, d = x.shape
    nh = d // HEAD_DIM
    n_groups = d // S5_GROUP_CH
    nt = n_groups // GROUPS_PER_TILE
    p_, c_ = S5_STATE, S5_GROUP_CH

    disc_in = (small["lam_re"], small["lam_im"], small["log_dt"], small["b_re"], small["b_im"])
    (lb_re, lb_im, bb_re, bb_im), disc_vjp = jax.vjp(_s5_discretize, *disc_in)
    del lb_re, lb_im
    dt = jnp.exp(small["log_dt"])[:, None]
    steps = jnp.arange(1, SCAN_T + 1, dtype=F32)[:, None, None]
    pmag = jnp.exp(steps * (small["lam_re"] * dt)[None])
    pang = steps * (small["lam_im"] * dt)[None]
    pw_re = (pmag * jnp.cos(pang)).reshape(SCAN_T, n_groups * p_)
    pw_im = (pmag * jnp.sin(pang)).reshape(SCAN_T, n_groups * p_)
    pf_re, pf_im = pw_re[::-1], pw_im[::-1]
    bbd_re = _block_diag(bb_re.transpose(0, 2, 1).reshape(nt, GROUPS_PER_TILE, c_, p_)).astype(BF16)
    bbd_im = _block_diag(bb_im.transpose(0, 2, 1).reshape(nt, GROUPS_PER_TILE, c_, p_)).astype(BF16)
    cbd_re = _block_diag(small["c_re"].transpose(0, 2, 1).reshape(nt, GROUPS_PER_TILE, p_, c_)).astype(BF16)
    cbd_im = _block_diag(small["c_im"].transpose(0, 2, 1).reshape(nt, GROUPS_PER_TILE, p_, c_)).astype(BF16)

    h0 = _norm_f32(x, small["a_norm"], "s5_norm")
    xs_re, xs_im, y0, z = _s5_fwd(h0, bbd_re, bbd_im, cbd_re, cbd_im, pw_re, pw_im, small["s5_d"], "s5_fwd")
    vg = _mm_nn(z, weight("w_glu", z), F32, "glu_mm", tn=1024)
    x1, hf0 = _glu_res_norm(vg, x, small["ffn_norm0"], "glu_res_norm")
    gu0, a0, o0 = _ffn_fwd(hf0, weight, 0, "ffn0")
    x2, kvn, h1 = _res_norm(o0, x1, [small["kv_norm"], small["b_norm"]], "ffn0_res_norm")
    kv = _mm_nn(kvn, weight("w_kv", kvn), BF16, "kv_mm", tn=1024)
    q = _mm_nn(h1, weight("w_q", kv), BF16, "q_mm", tn=1536)
    outs, lses = [], []
    for gi, (window, dil) in enumerate(PATTERNS):
        assert window // dil == ATT_BLK
        og, lg = _att_fwd(q, kv, gi, dil, nh, f"att_fwd{gi}")
        outs.append(og)
        lses.append(lg)
    oatt, oattb = _att_combine(outs, lses, "att_combine")
    ao = _mm_nn(oattb, weight("w_o", oattb), F32, "o_mm", tn=1024)
    x3, hf1 = _res_norm(ao, x2, [small["ffn_norm1"]], "att_res_norm")
    gu1, a1, o1 = _ffn_fwd(hf1, weight, 1, "ffn1")
    dres, dresb, loss_rows, d_final = _loss_head(o1, x3, small["final_norm"], target, "loss_head")

    dhf1, zero = _ffn_bwd(dresb, hf1, gu1, a1, weight, emit, 1, "ffn1")
    dres, dresb, d_ffn_norm1 = _norm_bwd(x3, [small["ffn_norm1"] + zero], [dhf1], dres, "ffn1_norm_bwd")
    zero = emit("w_o", _mm_tn(oattb, dresb, 1, "o_dw", tn=1024))
    doatt = _mm_nt(dresb, weight("w_o", None), "o_dx", tr=2048)
    dqs, dks, dvs = [], [], []
    for gi, (window, dil) in enumerate(PATTERNS):
        dq_g, dk_g, dv_g = _att_bwd(q, kv, doatt, oatt, lses, gi, dil, nh, f"att_bwd{gi}")
        dqs.append(dq_g)
        dks.append(dk_g)
        dvs.append(dv_g)
    dq = jnp.concatenate(dqs, axis=1)
    dkv = _dkv_sum(dks, dvs, "dkv_sum")
    w_q = weight("w_q", None)
    zero = zero + emit("w_q", _mm_tn(h1, dq, w_q.shape[0], "q_dw", tn=1536))
    zero = zero + emit("w_kv", _mm_tn(kvn, dkv, 1, "kv_dw", tn=1024))
    dh1 = _mm_nt(dq, w_q, "q_dx", tr=1536)
    dkvn = _mm_nt(dkv, weight("w_kv", None), "kv_dx", tr=1024)
    dres, dresb, d_b_norm, d_kv_norm = _norm_bwd(
        x2, [small["b_norm"] + zero, small["kv_norm"]], [dh1, dkvn], dres, "att_norm_bwd")
    dhf0, zero = _ffn_bwd(dresb, hf0, gu0, a0, weight, emit, 0, "ffn0")
    dres, dresb, d_ffn_norm0 = _norm_bwd(x1, [small["ffn_norm0"] + zero], [dhf0], dres, "ffn0_norm_bwd")
    del dresb
    dvg = _glu_bwd(dres, vg, "glu_bwd")
    w_glu = weight("w_glu", None)
    zero = emit("w_glu", _mm_tn(z, dvg, w_glu.shape[0], "glu_dw", tn=1024))
    dz = _mm_nt(dvg, w_glu, "glu_dx", tr=1024)
    dh0, d_s5_d, dcr, dci_neg, dbr, dbi, dar, dai = _s5_bwd(
        dz, y0, h0, xs_re, xs_im, bbd_re, bbd_im, cbd_re, cbd_im, pw_re, pw_im, pf_re, pf_im,
        small["s5_d"] + zero, "s5_bwd")
    grad_x, _, d_a_norm = _norm_bwd(x, [small["a_norm"]], [dh0], dres, "s5_norm_bwd")

    d_bb_re = _block_diag_take(dbr, c_, p_).transpose(0, 2, 1)
    d_bb_im = _block_diag_take(dbi, c_, p_).transpose(0, 2, 1)
    d_c_re = _block_diag_take(dcr, c_, p_)
    d_c_im = -_block_diag_take(dci_neg, c_, p_)
    d_lam_re, d_lam_im, d_log_dt, d_b_re, d_b_im = disc_vjp(
        (dar.reshape(n_groups, p_), dai.reshape(n_groups, p_), d_bb_re, d_bb_im))

    d_small = dict(lam_re=d_lam_re, lam_im=d_lam_im, log_dt=d_log_dt, b_re=d_b_re, b_im=d_b_im,
                   c_re=d_c_re, c_im=d_c_im, s5_d=d_s5_d, a_norm=d_a_norm, ffn_norm0=d_ffn_norm0,
                   ffn_norm1=d_ffn_norm1, b_norm=d_b_norm, kv_norm=d_kv_norm, final_norm=d_final)
    return loss_rows, grad_x, d_small


def _place():
    x, y, c = lax.axis_index("x"), lax.axis_index("y"), lax.axis_index("c")
    return x, y, c, [(1 - x, y), (x, 1 - y), (1 - x, 1 - y)]


_ANY = pl.BlockSpec(memory_space=pl.ANY)


_HBM = pl.BlockSpec(memory_space=pltpu.HBM)
_SEM = pl.BlockSpec(memory_space=pltpu.SEMAPHORE)
_EFFECT = pltpu.SideEffectType.DATAFLOW_SIDE_EFFECTING


def _in_hbm(a):
    return pltpu.with_memory_space_constraint(a, pltpu.HBM)


def _cast_place(shard, chip, name):
    r, c = shard.shape
    tm = _tile(r, ROWS, 16)

    def kern(chip_ref, x_ref, o_ref):
        del chip_ref
        o_ref[...] = x_ref[...].astype(BF16)

    return pl.pallas_call(
        kern, name=name,
        grid_spec=pltpu.PrefetchScalarGridSpec(
            num_scalar_prefetch=1, grid=(r // tm,),
            in_specs=[pl.BlockSpec((tm, c), lambda i, ch: (i, 0))],
            out_specs=pl.BlockSpec((None, tm, c), lambda i, ch: (ch[0], i, 0))),
        out_shape=jax.ShapeDtypeStruct((N_CHIPS, r, c), BF16),
        compiler_params=_params(("parallel",)),
    )(chip, shard)


def _gather_start(lands):
    n = len(lands)

    def body(*refs):
        land = refs[:n]
        send, recv = refs[n:2 * n], refs[2 * n:3 * n]
        x, y, c, peers = _place()
        me = 2 * x + y
        for a in range(n):
            for k, (px, py) in enumerate(peers):
                pltpu.make_async_remote_copy(
                    src_ref=land[a].at[me], dst_ref=land[a].at[me], send_sem=send[a].at[k], recv_sem=recv[a].at[k],
                    device_id=(px, py, c), device_id_type=MESH).start()

    outs = pl.pallas_call(
        body, name="gather_start",
        out_shape=[pltpu.SemaphoreType.DMA((3,))] * (2 * n) + [pltpu.HBM(a.shape, a.dtype) for a in lands],
        in_specs=[_HBM] * n, out_specs=[_SEM] * (2 * n) + [_HBM] * n,
        input_output_aliases={i: 2 * n + i for i in range(n)},
        compiler_params=pltpu.CompilerParams(has_side_effects=_EFFECT),
    )(*[_in_hbm(a) for a in lands])
    return outs[:n], outs[n:2 * n], outs[2 * n:]


def _gather_wait(lands, sends, recvs, after, name):
    n = len(lands)

    def body(*refs):
        land, send, recv = refs[:n], refs[n:2 * n], refs[2 * n:3 * n]
        x, y, c, peers = _place()
        me = 2 * x + y
        for a in range(n):
            for k, (px, py) in enumerate(peers):
                cp = pltpu.make_async_remote_copy(
                    src_ref=land[a].at[me], dst_ref=land[a].at[2 * px + py], send_sem=send[a].at[k],
                    recv_sem=recv[a].at[k], device_id=(px, py, c), device_id_type=MESH)
                cp.wait_send()
                cp.wait_recv()

    return pl.pallas_call(
        body, name=name,
        out_shape=[pltpu.HBM(a.shape, a.dtype) for a in lands],
        in_specs=[_HBM] * n + [_SEM] * (2 * n) + [_ANY], out_specs=[_HBM] * n,
        input_output_aliases={i: i for i in range(n)},
        compiler_params=pltpu.CompilerParams(has_side_effects=_EFFECT),
    )(*lands, *sends, *recvs, after)


def _scatter_start(g, name):
    def body(g_ref, land_ref, send, recv, g_out, land_out, token):
        del g_out, land_out
        x, y, c, peers = _place()
        for k, (px, py) in enumerate(peers):
            pltpu.make_async_remote_copy(
                src_ref=g_ref.at[2 * px + py], dst_ref=land_ref.at[k], send_sem=send.at[k], recv_sem=recv.at[k],
                device_id=(px, py, c), device_id_type=MESH).start()
        token[...] = jnp.zeros_like(token)

    land = lax.empty((3,) + g.shape[1:], g.dtype)
    return pl.pallas_call(
        body, name=name,
        out_shape=(pltpu.SemaphoreType.DMA((3,)), pltpu.SemaphoreType.DMA((3,)),
                   pltpu.HBM(g.shape, g.dtype), pltpu.HBM(land.shape, land.dtype),
                   jax.ShapeDtypeStruct((8, 128), F32)),
        in_specs=(_HBM, _HBM), out_specs=(_SEM, _SEM, _HBM, _HBM, pl.BlockSpec(memory_space=pltpu.VMEM)),
        input_output_aliases={0: 2, 1: 3},
        compiler_params=pltpu.CompilerParams(has_side_effects=_EFFECT),
    )(_in_hbm(g), _in_hbm(land))


def _scatter_wait(started, after):
    n = len(started)

    def body(*refs):
        gs, lands = refs[:n], refs[n:2 * n]
        sends, recvs = refs[2 * n:3 * n], refs[3 * n:4 * n]
        x, y, c, peers = _place()
        for a in range(n):
            for k, (px, py) in enumerate(peers):
                cp = pltpu.make_async_remote_copy(
                    src_ref=gs[a].at[2 * px + py], dst_ref=lands[a].at[k], send_sem=sends[a].at[k],
                    recv_sem=recvs[a].at[k], device_id=(px, py, c), device_id_type=MESH)
                cp.wait_send()
                cp.wait_recv()

    gs = [s[2] for s in started]
    lands = [s[3] for s in started]
    outs = pl.pallas_call(
        body, name="scatter_wait",
        out_shape=[pltpu.HBM(a.shape, a.dtype) for a in gs + lands],
        in_specs=[_HBM] * (2 * n) + [_SEM] * (2 * n) + [_ANY], out_specs=[_HBM] * (2 * n),
        input_output_aliases={i: i for i in range(2 * n)},
        compiler_params=pltpu.CompilerParams(has_side_effects=_EFFECT),
    )(*gs, *lands, *[s[0] for s in started], *[s[1] for s in started], after)
    return outs[:n], outs[n:]


def _swap_cores(parts):
    n = len(parts)

    def body(*refs):
        ins, outs = refs[:n], refs[n:2 * n]
        send_sems, recv_sems = refs[2 * n:]
        x, y, c = lax.axis_index("x"), lax.axis_index("y"), lax.axis_index("c")
        copies = []
        for a in range(n):
            cp = pltpu.make_async_remote_copy(
                src_ref=ins[a], dst_ref=outs[a], send_sem=send_sems.at[a], recv_sem=recv_sems.at[a],
                device_id=(x, y, 1 - c), device_id_type=MESH)
            cp.start()
            copies.append(cp)
        for cp in copies:
            cp.wait()

    return pl.pallas_call(
        body, name="swap_cores",
        in_specs=[_ANY] * n, out_specs=[_ANY] * n,
        out_shape=[jax.ShapeDtypeStruct(a.shape, a.dtype) for a in parts],
        scratch_shapes=[pltpu.SemaphoreType.DMA((n,)), pltpu.SemaphoreType.DMA((n,))],
    )(*parts)


def _all_reduce_small(v):
    nd, r, w = v.shape
    assert nd == N_DEV

    def body(v_ref, out_ref, land_ref, red_ref, send1, recv1, send2, recv2):
        x, y, c = lax.axis_index("x"), lax.axis_index("y"), lax.axis_index("c")
        me = 4 * x + 2 * y + c
        peers = []
        for k in range(1, N_DEV):
            kx, ky, kc = (k >> 2) & 1, (k >> 1) & 1, k & 1
            peers.append((1 - x if kx else x, 1 - y if ky else y, 1 - c if kc else c))
        first = []
        for k, (px, py, pc) in enumerate(peers):
            cp = pltpu.make_async_remote_copy(
                src_ref=v_ref.at[4 * px + 2 * py + pc], dst_ref=land_ref.at[me], send_sem=send1.at[k],
                recv_sem=recv1.at[k], device_id=(px, py, pc), device_id_type=MESH)
            cp.start()
            first.append(cp)
        land_ref[me] = v_ref[me]
        for cp in first:
            cp.wait()
        acc = land_ref[0]
        for j in range(1, N_DEV):
            acc = acc + land_ref[j]
        red_ref[...] = acc
        second = []
        for k, (px, py, pc) in enumerate(peers):
            cp = pltpu.make_async_remote_copy(
                src_ref=red_ref, dst_ref=out_ref.at[me], send_sem=send2.at[k],
                recv_sem=recv2.at[k], device_id=(px, py, pc), device_id_type=MESH)
            cp.start()
            second.append(cp)
        out_ref[me] = acc
        for cp in second:
            cp.wait()

    vmem = pl.BlockSpec(memory_space=pltpu.VMEM)
    return pl.pallas_call(
        body, name="all_reduce_small",
        in_specs=[vmem], out_specs=vmem,
        out_shape=jax.ShapeDtypeStruct((nd, r, w), F32),
        scratch_shapes=[pltpu.VMEM((nd, r, w), F32), pltpu.VMEM((r, w), F32)]
        + [pltpu.SemaphoreType.DMA((N_DEV - 1,))] * 4,
        compiler_params=pltpu.CompilerParams(vmem_limit_bytes=VMEM_LIMIT),
    )(v)


def _adam_math(w, g, m, v):
    m = ADAM_B1 * m + (1.0 - ADAM_B1) * g
    v = ADAM_B2 * v + (1.0 - ADAM_B2) * (g * g)
    m_hat = m / (1.0 - ADAM_B1 ** ADAM_STEP)
    v_hat = v / (1.0 - ADAM_B2 ** ADAM_STEP)
    delta = -ADAM_LR * (m_hat / (jnp.sqrt(v_hat) + ADAM_EPS) + ADAM_WD * w)
    return delta, m, v


def _sum_blocks(own, got, chip, name):
    _, r, c = own.shape
    tm = _tile(r, ROWS, 16)

    def kern(chip_ref, own_ref, got_ref, o_ref):
        del chip_ref
        acc = own_ref[...].astype(F32)
        for k in range(3):
            acc = acc + got_ref[k].astype(F32)
        o_ref[...] = acc

    return pl.pallas_call(
        kern, name=name,
        grid_spec=pltpu.PrefetchScalarGridSpec(
            num_scalar_prefetch=1, grid=(r // tm,),
            in_specs=[pl.BlockSpec((None, tm, c), lambda i, ch: (ch[0], i, 0)),
                      pl.BlockSpec((3, tm, c), lambda i, ch: (0, i, 0))],
            out_specs=pl.BlockSpec((tm, c), lambda i, ch: (i, 0))),
        out_shape=jax.ShapeDtypeStruct((r, c), F32),
        compiler_params=_params(("parallel",)),
    )(chip, own, got)


def _adamw(p0, p1, w, m, v, name):
    r, c = w.shape
    tm = _tile(r, 128, 8)

    def kern(p0_ref, p1_ref, w_ref, m_ref, v_ref, g_ref, d_ref, mo_ref, vo_ref):
        g = p0_ref[...] + p1_ref[...]
        g_ref[...] = g
        d_ref[...], mo_ref[...], vo_ref[...] = _adam_math(w_ref[...], g, m_ref[...], v_ref[...])

    spec = pl.BlockSpec((tm, c), lambda i: (i, 0))
    return pl.pallas_call(
        kern, name=name, grid=(r // tm,),
        in_specs=[spec] * 5, out_specs=[spec] * 4,
        out_shape=[jax.ShapeDtypeStruct((r, c), F32)] * 4,
        compiler_params=_params(("parallel",)),
    )(p0, p1, w, m, v)


def _adamw_small(g, w, m, v, name):
    def kern(g_ref, w_ref, m_ref, v_ref, d_ref, mo_ref, vo_ref):
        d_ref[...], mo_ref[...], vo_ref[...] = _adam_math(w_ref[...], g_ref[...], m_ref[...], v_ref[...])

    return pl.pallas_call(
        kern, name=name,
        out_shape=[jax.ShapeDtypeStruct(g.shape, F32)] * 3,
        compiler_params=pltpu.CompilerParams(vmem_limit_bytes=VMEM_LIMIT),
    )(g, w, m, v)


def _pack(arrays, rows):
    flat = jnp.concatenate([a.reshape(-1).astype(F32) for a in arrays])
    return jnp.pad(flat, (0, rows * 128 - flat.shape[0])).reshape(rows, 128)


def _unpack(packed, shapes):
    flat = packed.reshape(-1)
    out, off = [], 0
    for shp in shapes:
        size = math.prod(shp)
        out.append(flat[off:off + size].reshape(shp))
        off += size
    return out


_REPLICATED = ["s5_lam_re", "s5_lam_im", "s5_log_dt", "s5_b_re", "s5_b_im", "s5_c_re", "s5_c_im",
               "ffn_norm", "b_norm_mix", "kv_norm", "final_norm"]
_CHIP_VECTORS = ["s5_d", "a_norm_mix"]
_BIG = ["s5_w_glu", "ffn_w_in", "ffn_w_out", "attn_w_q", "attn_w_o", "w_kv"]
_WEIGHT_ORDER = ["s5_lam_re", "s5_lam_im", "s5_log_dt", "s5_b_re", "s5_b_im", "s5_c_re", "s5_c_im", "s5_d",
                 "s5_w_glu", "a_norm_mix", "ffn_norm", "ffn_w_in", "ffn_w_out", "b_norm_mix", "attn_w_q",
                 "attn_w_o", "kv_norm", "w_kv", "final_norm"]


def _step(x, loss_target, w, m, v):
    s, d = x.shape[1], x.shape[2]
    chip = 2 * lax.axis_index("x") + lax.axis_index("y")

    col_sharded = dict(w_glu=w["s5_w_glu"][0], w_in0=w["ffn_w_in"][0], w_in1=w["ffn_w_in"][1], w_q=w["attn_w_q"][0])
    row_sharded = dict(w_out0=w["ffn_w_out"][0], w_out1=w["ffn_w_out"][1], w_o=w["attn_w_o"][0], w_kv=w["w_kv"])
    names = list(col_sharded) + list(row_sharded)
    chip_arr = jnp.reshape(chip, (1,)).astype(jnp.int32)
    vector_lands = [lax.dynamic_update_slice(jnp.zeros((N_CHIPS,) + w[n].shape, F32), w[n][None], (chip, 0, 0))
                    for n in _CHIP_VECTORS]
    lands = [_cast_place(a, chip_arr, f"cast_{n}") for n, a in {**col_sharded, **row_sharded}.items()]
    send, recv, land_thru = _gather_start(vector_lands + lands)
    vectors = _gather_wait(land_thru[:2], send[:2], recv[:2], jnp.zeros((8, 128), F32), "gather_wait_vectors")
    s5_d_full = vectors[0].reshape(1, d)
    a_norm_full = vectors[1].reshape(1, d)
    arrived = {}

    def weight(name, after):
        if name not in arrived:
            i = 2 + names.index(name)
            land, = _gather_wait([land_thru[i]], [send[i]], [recv[i]], after, f"gather_wait_{name}")
            arrived[name] = land if name in col_sharded else land.reshape(1, -1, land.shape[-1])
        return arrived[name]

    started = {}

    def emit(name, dw):
        outs = _scatter_start(dw.reshape(N_CHIPS, -1, dw.shape[-1]), f"scatter_start_{name}")
        started[name] = outs[:4]
        return outs[4][0, 0]

    small = dict(lam_re=w["s5_lam_re"][0], lam_im=w["s5_lam_im"][0], log_dt=w["s5_log_dt"][0],
                 b_re=w["s5_b_re"][0], b_im=w["s5_b_im"][0], c_re=w["s5_c_re"][0], c_im=w["s5_c_im"][0],
                 s5_d=s5_d_full, a_norm=a_norm_full, ffn_norm0=w["ffn_norm"][0:1], ffn_norm1=w["ffn_norm"][1:2],
                 b_norm=w["b_norm_mix"], kv_norm=w["kv_norm"].reshape(1, d), final_norm=w["final_norm"].reshape(1, d))

    loss_rows, grad_x, d_small = _local_step(x[0], loss_target[0], small, weight, emit)

    g4, got = _scatter_wait([started[n] for n in names], grad_x)
    chip_arr = jnp.reshape(chip, (1,)).astype(jnp.int32)
    partial = [_sum_blocks(o, r, chip_arr, f"sum_{n}") for n, o, r in zip(names, g4, got)]
    other = _swap_cores(partial)
    part = dict(zip(names, zip(partial, other)))

    def two_layers(name0, name1):
        (a0, b0), (a1, b1) = part[name0], part[name1]
        return jnp.concatenate([a0, a1], axis=0), jnp.concatenate([b0, b1], axis=0)

    big_parts = {"s5_w_glu": part["w_glu"], "ffn_w_in": two_layers("w_in0", "w_in1"),
                 "ffn_w_out": two_layers("w_out0", "w_out1"), "attn_w_q": part["w_q"],
                 "attn_w_o": part["w_o"], "w_kv": part["w_kv"]}
    result = {}
    for name in _BIG:
        p0, p1 = big_parts[name]
        shape = w[name].shape
        flat = lambda a: a.reshape(-1, shape[-1])
        outs = _adamw(p0, p1, flat(w[name]), flat(m[name]), flat(v[name]), f"adamw_{name}")
        result[name] = [o.reshape(shape) for o in outs]

    rep_grads = [d_small["lam_re"], d_small["lam_im"], d_small["log_dt"], d_small["b_re"], d_small["b_im"],
                 d_small["c_re"], d_small["c_im"],
                 jnp.concatenate([d_small["ffn_norm0"], d_small["ffn_norm1"]], axis=0),
                 d_small["b_norm"], d_small["kv_norm"], d_small["final_norm"]]
    vec_grads = [d_small["s5_d"], d_small["a_norm"]]
    to_reduce = rep_grads + vec_grads + [jnp.sum(loss_rows).reshape(1)]
    total = sum(math.prod(a.shape) for a in to_reduce)
    rows_per = -(-total // (N_DEV * 128 * 8)) * 8
    reduced = _all_reduce_small(_pack(to_reduce, N_DEV * rows_per).reshape(N_DEV, rows_per, 128))
    red = _unpack(reduced, [a.shape for a in to_reduce])
    loss = red[-1][0]
    g_small = dict(zip(_REPLICATED, [r.reshape(w[n].shape) for r, n in zip(red[:len(rep_grads)], _REPLICATED)]))
    for n, r in zip(_CHIP_VECTORS, red[len(rep_grads):-1]):
        g_small[n] = lax.dynamic_slice_in_dim(r.reshape(1, d), chip * (d // N_CHIPS), d // N_CHIPS, axis=1)
    small_names = _REPLICATED + _CHIP_VECTORS
    n_small = sum(math.prod(w[n].shape) for n in small_names)
    rows_small = -(-n_small // (128 * 8)) * 8
    packed = [_pack([src[n] for n in small_names], rows_small) for src in (g_small, w, m, v)]
    upd = _adamw_small(*packed, "adamw_small")
    shapes = [w[n].shape for n in small_names]
    for n, dl, mo, vo in zip(small_names, *[_unpack(u, shapes) for u in upd]):
        result[n] = [g_small[n], dl, mo, vo]

    cols = [[result[n][t] for n in _WEIGHT_ORDER] for t in range(4)]
    return (loss, grad_x.reshape(x.shape), *cols[0], *cols[1], *cols[2], *cols[3])


def kernel(x, s5_lam_re, s5_lam_im, s5_log_dt, s5_b_re, s5_b_im, s5_c_re, s5_c_im, s5_d, s5_w_glu, a_norm_mix, ffn_norm, ffn_w_in, ffn_w_out, b_norm_mix, attn_w_q, attn_w_o, kv_norm, w_kv, final_norm, loss_target, m_s5_lam_re, m_s5_lam_im, m_s5_log_dt, m_s5_b_re, m_s5_b_im, m_s5_c_re, m_s5_c_im, m_s5_d, m_s5_w_glu, m_a_norm_mix, m_ffn_norm, m_ffn_w_in, m_ffn_w_out, m_b_norm_mix, m_attn_w_q, m_attn_w_o, m_kv_norm, m_w_kv, m_final_norm, v_s5_lam_re, v_s5_lam_im, v_s5_log_dt, v_s5_b_re, v_s5_b_im, v_s5_c_re, v_s5_c_im, v_s5_d, v_s5_w_glu, v_a_norm_mix, v_ffn_norm, v_ffn_w_in, v_ffn_w_out, v_b_norm_mix, v_attn_w_q, v_attn_w_o, v_kv_norm, v_w_kv, v_final_norm):
    w = dict(zip(_WEIGHT_ORDER, (s5_lam_re, s5_lam_im, s5_log_dt, s5_b_re, s5_b_im, s5_c_re, s5_c_im, s5_d, s5_w_glu, a_norm_mix, ffn_norm, ffn_w_in, ffn_w_out, b_norm_mix, attn_w_q, attn_w_o, kv_norm, w_kv, final_norm)))
    m = dict(zip(_WEIGHT_ORDER, (m_s5_lam_re, m_s5_lam_im, m_s5_log_dt, m_s5_b_re, m_s5_b_im, m_s5_c_re, m_s5_c_im, m_s5_d, m_s5_w_glu, m_a_norm_mix, m_ffn_norm, m_ffn_w_in, m_ffn_w_out, m_b_norm_mix, m_attn_w_q, m_attn_w_o, m_kv_norm, m_w_kv, m_final_norm)))
    v = dict(zip(_WEIGHT_ORDER, (v_s5_lam_re, v_s5_lam_im, v_s5_log_dt, v_s5_b_re, v_s5_b_im, v_s5_c_re, v_s5_c_im, v_s5_d, v_s5_w_glu, v_a_norm_mix, v_ffn_norm, v_ffn_w_in, v_ffn_w_out, v_b_norm_mix, v_attn_w_q, v_attn_w_o, v_kv_norm, v_w_kv, v_final_norm)))
    return _step(x, loss_target, w, m, v)
```

```python
import functools
import math

import jax
import jax.numpy as jnp
from jax import lax
from jax.experimental import pallas as pl
from jax.experimental.pallas import tpu as pltpu

F32 = jnp.float32
BF16 = jnp.bfloat16

S5_GROUP_CH = 16
S5_STATE = 64
GROUPS_PER_TILE = 8
HEAD_DIM = 128
N_KV_HEADS = 4
PATTERNS = ((128, 1), (512, 4), (2048, 16))
ATT_BLK = 128
EPS = 1e-6
NEG_INF = -1e30
SCAN_T = 128
ADAM_LR = 0.001
ADAM_B1 = 0.9
ADAM_B2 = 0.999
ADAM_EPS = 1e-08
ADAM_WD = 0.01
ADAM_STEP = 10
N_CHIPS = 4
N_DEV = 8
VMEM_LIMIT = 56 * 1024 * 1024
MESH = pl.DeviceIdType.MESH
GELU_K = math.sqrt(2.0 / math.pi)
GELU_C = 0.044715


def _tile(n, pref, unit=128):
    if n <= pref:
        return n
    best = None
    t = unit
    while t <= pref:
        if n % t == 0:
            best = t
        t += unit
    assert best is not None, (n, pref, unit)
    return best


def _params(sem):
    return pltpu.CompilerParams(dimension_semantics=sem, vmem_limit_bytes=VMEM_LIMIT)


def _dot(a, b):
    return jnp.dot(a, b, preferred_element_type=F32)


def _dot_nt(a, b):
    return lax.dot_general(a, b, (((1,), (1,)), ((), ())), preferred_element_type=F32)


def _dot_tn(a, b):
    return lax.dot_general(a, b, (((0,), (0,)), ((), ())), preferred_element_type=F32)


def _mm_nn(a, w, out_dtype, name, tm=512, tn=1536, tk=2048):
    m, k = a.shape
    nb, k2, nq = w.shape
    assert k == k2
    tm, tn, tk = _tile(m, tm, 8), _tile(nq, tn), _tile(k, tk)
    per, nk = nq // tn, k // tk

    def kern(a_ref, w_ref, o_ref, *acc):
        p = _dot(a_ref[...], w_ref[...])
        if nk == 1:
            o_ref[...] = p.astype(o_ref.dtype)
        else:
            acc_ref, = acc
            kk = pl.program_id(2)

            @pl.when(kk == 0)
            def _():
                acc_ref[...] = p

            @pl.when(kk > 0)
            def _():
                acc_ref[...] += p

            @pl.when(kk == nk - 1)
            def _():
                o_ref[...] = acc_ref[...].astype(o_ref.dtype)

    return pl.pallas_call(
        kern, name=name, grid=(nb * per, m // tm, nk),
        in_specs=[pl.BlockSpec((tm, tk), lambda j, i, kk: (i, kk)),
                  pl.BlockSpec((None, tk, tn), lambda j, i, kk: (j // per, kk, j % per))],
        out_specs=pl.BlockSpec((tm, tn), lambda j, i, kk: (i, j)),
        out_shape=jax.ShapeDtypeStruct((m, nb * nq), out_dtype),
        scratch_shapes=[] if nk == 1 else [pltpu.VMEM((tm, tn), F32)],
        compiler_params=_params(("parallel", "parallel", "arbitrary")),
    )(a, w)


def _paired_block(r, per, nb):
    j = r // 2
    return (r % 2) * (nb // 2) + j // per, j % per


def _mm_nt(a, w, name, tm=512, tr=1536, tkc=2048, paired=False):
    m, n = a.shape
    nb, k, nq = w.shape
    assert n == nb * nq
    tm, tr, tkc = _tile(m, tm, 8), _tile(nq, tr), _tile(k, tkc)
    per = nq // tr
    nr = nb * per

    def w_block(r):
        return _paired_block(r, per, nb) if paired else (r // per, r % per)

    def kern(a_ref, w_ref, o_ref, *acc):
        p = _dot_nt(a_ref[...], w_ref[...])
        if nr == 1:
            o_ref[...] = p
        else:
            acc_ref, = acc
            r = pl.program_id(2)

            @pl.when(r == 0)
            def _():
                acc_ref[...] = p

            @pl.when(r > 0)
            def _():
                acc_ref[...] += p

            @pl.when(r == nr - 1)
            def _():
                o_ref[...] = acc_ref[...]

    return pl.pallas_call(
        kern, name=name, grid=(m // tm, k // tkc, nr),
        in_specs=[pl.BlockSpec((tm, tr), lambda i, kc, r: (i, r)),
                  pl.BlockSpec((None, tkc, tr), lambda i, kc, r: (w_block(r)[0], kc, w_block(r)[1]))],
        out_specs=pl.BlockSpec((tm, tkc), lambda i, kc, r: (i, kc)),
        out_shape=jax.ShapeDtypeStruct((m, k), F32),
        scratch_shapes=[] if nr == 1 else [pltpu.VMEM((tm, tkc), F32)],
        compiler_params=_params(("parallel", "parallel", "arbitrary")),
    )(a, w)


def _mm_tn(a, dy, nb, name, ts=512, tkk=1024, tn=1536, paired=False):
    s, k = a.shape
    s2, n = dy.shape
    assert s == s2 and n % nb == 0
    nq = n // nb
    ts, tkk, tn = _tile(s, ts, 16), _tile(k, tkk), _tile(nq, tn)
    per, ns = nq // tn, s // ts

    def w_block(j):
        return _paired_block(j, per, nb) if paired else (j // per, j % per)

    def kern(a_ref, dy_ref, o_ref, acc_ref):
        si = pl.program_id(2)
        p = _dot_tn(a_ref[...], dy_ref[...])

        @pl.when(si == 0)
        def _():
            acc_ref[...] = p

        @pl.when(si > 0)
        def _():
            acc_ref[...] += p

        @pl.when(si == ns - 1)
        def _():
            o_ref[...] = acc_ref[...].astype(o_ref.dtype)

    return pl.pallas_call(
        kern, name=name, grid=(k // tkk, nb * per, ns),
        in_specs=[pl.BlockSpec((ts, tkk), lambda kk, j, si: (si, kk)),
                  pl.BlockSpec((ts, tn), lambda kk, j, si: (si, j))],
        out_specs=pl.BlockSpec((None, tkk, tn), lambda kk, j, si: (w_block(j)[0], kk, w_block(j)[1])),
        out_shape=jax.ShapeDtypeStruct((nb, k, nq), BF16),
        scratch_shapes=[pltpu.VMEM((tkk, tn), F32)],
        compiler_params=_params(("parallel", "parallel", "arbitrary")),
    )(a, dy)


ROWS = 256


def _rms(x, g):
    r = lax.rsqrt(jnp.mean(x * x, axis=-1, keepdims=True) + EPS)
    return x * r * g


def _rms_bwd(x, g, dh):
    r = lax.rsqrt(jnp.mean(x * x, axis=-1, keepdims=True) + EPS)
    xh = x * r
    dgx = dh * g
    dx = r * (dgx - xh * jnp.mean(dgx * xh, axis=-1, keepdims=True))
    return dx, dh * xh


def _sigmoid(x):
    return 1.0 / (1.0 + jnp.exp(-x))


def _gelu(y):
    return 0.5 * y * (1.0 + jnp.tanh(GELU_K * (y + GELU_C * y * y * y)))


def _gelu_grad(y):
    t = jnp.tanh(GELU_K * (y + GELU_C * y * y * y))
    return 0.5 * (1.0 + t) + 0.5 * y * (1.0 - t * t) * GELU_K * (1.0 + 3.0 * GELU_C * y * y)


def _row_spec(tm, d, col=0):
    return pl.BlockSpec((tm, d), lambda i: (i, col))


def _vec_spec(d):
    return pl.BlockSpec((1, d), lambda i: (0, 0))


def _acc_rows(ref, val, i):
    s = jnp.sum(val, axis=0, keepdims=True)

    @pl.when(i == 0)
    def _():
        ref[...] = s

    @pl.when(i > 0)
    def _():
        ref[...] += s


def _norm_f32(x, g, name):
    s, d = x.shape
    tm = _tile(s, ROWS, 8)

    def kern(x_ref, g_ref, h_ref):
        h_ref[...] = _rms(x_ref[...], g_ref[...])

    return pl.pallas_call(
        kern, name=name, grid=(s // tm,),
        in_specs=[_row_spec(tm, d), _vec_spec(d)],
        out_specs=_row_spec(tm, d),
        out_shape=jax.ShapeDtypeStruct((s, d), F32),
        compiler_params=_params(("parallel",)),
    )(x, g)


def _glu_res_norm(vg, x, g, name):
    s, d = x.shape
    tm = _tile(s, ROWS, 16)

    def kern(val_ref, gate_ref, x_ref, g_ref, x1_ref, hf_ref):
        x1 = x_ref[...] + val_ref[...] * _sigmoid(gate_ref[...])
        x1_ref[...] = x1
        hf_ref[...] = _rms(x1, g_ref[...]).astype(BF16)

    return pl.pallas_call(
        kern, name=name, grid=(s // tm,),
        in_specs=[_row_spec(tm, d, 0), _row_spec(tm, d, 1), _row_spec(tm, d), _vec_spec(d)],
        out_specs=[_row_spec(tm, d), _row_spec(tm, d)],
        out_shape=[jax.ShapeDtypeStruct((s, d), F32), jax.ShapeDtypeStruct((s, d), BF16)],
        compiler_params=_params(("parallel",)),
    )(vg, vg, x, g)


FFN_TILE = 1408


def _ffn_in_act(hf, w_in, name, tm=512):
    s, k = hf.shape
    nb, _, nq = w_in.shape
    tm, tn = _tile(s, tm, 16), _tile(nq, FFN_TILE)
    per = nq // tn
    nf = (nb // 2) * per

    def kern(h_ref, wg_ref, wu_ref, a_ref, gu_ref):
        h = h_ref[...]
        g = _dot(h, wg_ref[...])
        u = _dot(h, wu_ref[...])
        a_ref[...] = (g * _sigmoid(g) * u).astype(BF16)
        gu_ref[:, :tn] = g.astype(BF16)
        gu_ref[:, tn:] = u.astype(BF16)

    return pl.pallas_call(
        kern, name=name, grid=(nf, s // tm),
        in_specs=[pl.BlockSpec((tm, k), lambda j, i: (i, 0)),
                  pl.BlockSpec((None, k, tn), lambda j, i: (j // per, 0, j % per)),
                  pl.BlockSpec((None, k, tn), lambda j, i: (nb // 2 + j // per, 0, j % per))],
        out_specs=[pl.BlockSpec((tm, tn), lambda j, i: (i, j)),
                   pl.BlockSpec((tm, 2 * tn), lambda j, i: (i, j))],
        out_shape=[jax.ShapeDtypeStruct((s, nf * tn), BF16), jax.ShapeDtypeStruct((s, 2 * nf * tn), BF16)],
        compiler_params=_params(("parallel", "parallel")),
    )(hf, w_in, w_in)


def _ffn_out_dx_act(dresb, w_out, gu, name, tm=512):
    s, d = dresb.shape
    f = w_out.shape[1]
    tm = _tile(s, tm, 16)
    tn = _tile(f // 2, FFN_TILE)

    def kern(d_ref, w_ref, gu_ref, o_ref):
        da = _dot_nt(d_ref[...], w_ref[...])
        g = gu_ref[:, :tn].astype(F32)
        u = gu_ref[:, tn:].astype(F32)
        sg = _sigmoid(g)
        o_ref[:, :tn] = (da * u * sg * (1.0 + g * (1.0 - sg))).astype(BF16)
        o_ref[:, tn:] = (da * g * sg).astype(BF16)

    pair = pl.BlockSpec((tm, 2 * tn), lambda j, i: (i, j))
    return pl.pallas_call(
        kern, name=name, grid=(f // tn, s // tm),
        in_specs=[pl.BlockSpec((tm, d), lambda j, i: (i, 0)),
                  pl.BlockSpec((None, tn, d), lambda j, i: (0, j, 0)), pair],
        out_specs=pair,
        out_shape=jax.ShapeDtypeStruct((s, 2 * f), BF16),
        compiler_params=_params(("parallel", "parallel")),
    )(dresb, w_out, gu)


def _res_norm(o, x, gains, name):
    s, d = x.shape
    tm = _tile(s, ROWS, 16)
    ng = len(gains)

    def kern(o_ref, x_ref, *refs):
        xn = x_ref[...] + o_ref[...]
        refs[ng][...] = xn
        for t in range(ng):
            refs[ng + 1 + t][...] = _rms(xn, refs[t][...]).astype(BF16)

    return pl.pallas_call(
        kern, name=name, grid=(s // tm,),
        in_specs=[_row_spec(tm, d), _row_spec(tm, d)] + [_vec_spec(d)] * ng,
        out_specs=[_row_spec(tm, d)] * (1 + ng),
        out_shape=[jax.ShapeDtypeStruct((s, d), F32)] + [jax.ShapeDtypeStruct((s, d), BF16)] * ng,
        compiler_params=_params(("parallel",)),
    )(o, x, *gains)


def _loss_head(o, x, g, target, name):
    s, d = x.shape
    tm = _tile(s, ROWS, 16)

    def kern(o_ref, x_ref, g_ref, t_ref, dx_ref, dxb_ref, loss_ref, dg_ref):
        i = pl.program_id(0)
        x4 = x_ref[...] + o_ref[...]
        gg = g_ref[...]
        diff = _rms(x4, gg) - t_ref[...]
        dx, dgr = _rms_bwd(x4, gg, diff * (1.0 / d))
        dx_ref[...] = dx
        dxb_ref[...] = dx.astype(BF16)
        _acc_rows(loss_ref, diff * diff * (0.5 / d), i)
        _acc_rows(dg_ref, dgr, i)

    return pl.pallas_call(
        kern, name=name, grid=(s // tm,),
        in_specs=[_row_spec(tm, d), _row_spec(tm, d), _vec_spec(d), _row_spec(tm, d)],
        out_specs=[_row_spec(tm, d), _row_spec(tm, d), _vec_spec(d), _vec_spec(d)],
        out_shape=[jax.ShapeDtypeStruct((s, d), F32), jax.ShapeDtypeStruct((s, d), BF16),
                   jax.ShapeDtypeStruct((1, d), F32), jax.ShapeDtypeStruct((1, d), F32)],
        compiler_params=_params(("arbitrary",)),
    )(o, x, g, target)


def _norm_bwd(x, gains, dhs, dres, name):
    s, d = x.shape
    tm = _tile(s, ROWS, 16)
    ng = len(gains)

    def kern(x_ref, dres_ref, *refs):
        i = pl.program_id(0)
        x_ = x_ref[...]
        acc = dres_ref[...]
        for t in range(ng):
            dx, dgr = _rms_bwd(x_, refs[t][...], refs[ng + t][...])
            acc = acc + dx
            _acc_rows(refs[2 * ng + 2 + t], dgr, i)
        refs[2 * ng][...] = acc
        refs[2 * ng + 1][...] = acc.astype(BF16)

    return pl.pallas_call(
        kern, name=name, grid=(s // tm,),
        in_specs=[_row_spec(tm, d), _row_spec(tm, d)] + [_vec_spec(d)] * ng + [_row_spec(tm, d)] * ng,
        out_specs=[_row_spec(tm, d), _row_spec(tm, d)] + [_vec_spec(d)] * ng,
        out_shape=[jax.ShapeDtypeStruct((s, d), F32), jax.ShapeDtypeStruct((s, d), BF16)]
        + [jax.ShapeDtypeStruct((1, d), F32)] * ng,
        compiler_params=_params(("arbitrary",)),
    )(x, dres, *gains, *dhs)


def _glu_bwd(dmix, vg, name):
    s, d = dmix.shape
    tm = _tile(s, ROWS, 16)

    def kern(dm_ref, val_ref, gate_ref, o_ref):
        dm = dm_ref[...]
        sg = _sigmoid(gate_ref[...])
        o_ref[:, :d] = (dm * sg).astype(BF16)
        o_ref[:, d:] = (dm * val_ref[...] * sg * (1.0 - sg)).astype(BF16)

    return pl.pallas_call(
        kern, name=name, grid=(s // tm,),
        in_specs=[_row_spec(tm, d), _row_spec(tm, d, 0), _row_spec(tm, d, 1)],
        out_specs=_row_spec(tm, 2 * d),
        out_shape=jax.ShapeDtypeStruct((s, 2 * d), BF16),
        compiler_params=_params(("parallel",)),
    )(dmix, vg, vg)


def _shift_down(v, sh, rows):
    return jnp.where(rows >= sh, pltpu.roll(v, sh, 0), 0.0)


def _shift_up(v, sh, rows, t):
    return jnp.where(rows < t - sh, pltpu.roll(v, t - sh, 0), 0.0)


def _s5_fwd(u, bbd_re, bbd_im, cbd_re, cbd_im, pw_re, pw_im, dskip, name):
    s, d = u.shape
    nt, cw, lw = bbd_re.shape
    t = pw_re.shape[0]
    nc = s // t

    def kern(u_ref, bre_ref, bim_ref, cre_ref, cim_ref, pre_ref, pim_ref, d_ref,
             xr_ref, xi_ref, y_ref, z_ref, car_ref, cai_ref):
        c = pl.program_id(1)

        @pl.when(c == 0)
        def _():
            car_ref[...] = jnp.zeros_like(car_ref)
            cai_ref[...] = jnp.zeros_like(cai_ref)

        u_ = u_ref[...]
        ub = u_.astype(BF16)
        xr = _dot(ub, bre_ref[...])
        xi = _dot(ub, bim_ref[...])
        rows = lax.broadcasted_iota(jnp.int32, (t, lw), 0)
        sh = 1
        while sh < t:
            ar = pre_ref[sh - 1:sh, :]
            ai = pim_ref[sh - 1:sh, :]
            sr = _shift_down(xr, sh, rows)
            si = _shift_down(xi, sh, rows)
            xr, xi = xr + ar * sr - ai * si, xi + ar * si + ai * sr
            sh *= 2
        cr = car_ref[...]
        ci = cai_ref[...]
        pr = pre_ref[...]
        pi = pim_ref[...]
        xr, xi = xr + pr * cr - pi * ci, xi + pr * ci + pi * cr
        car_ref[...] = xr[t - 1:t, :]
        cai_ref[...] = xi[t - 1:t, :]
        xr_ref[...] = xr
        xi_ref[...] = xi
        y = _dot(xr.astype(BF16), cre_ref[...]) - _dot(xi.astype(BF16), cim_ref[...]) + d_ref[...] * u_
        y_ref[...] = y
        z_ref[...] = _gelu(y).astype(BF16)

    tok = pl.BlockSpec((t, cw), lambda j, c: (c, j))
    st = pl.BlockSpec((t, lw), lambda j, c: (c, j))
    return pl.pallas_call(
        kern, name=name, grid=(nt, nc),
        in_specs=[tok,
                  pl.BlockSpec((None, cw, lw), lambda j, c: (j, 0, 0)),
                  pl.BlockSpec((None, cw, lw), lambda j, c: (j, 0, 0)),
                  pl.BlockSpec((None, lw, cw), lambda j, c: (j, 0, 0)),
                  pl.BlockSpec((None, lw, cw), lambda j, c: (j, 0, 0)),
                  pl.BlockSpec((t, lw), lambda j, c: (0, j)),
                  pl.BlockSpec((t, lw), lambda j, c: (0, j)),
                  pl.BlockSpec((1, cw), lambda j, c: (0, j))],
        out_specs=[st, st, tok, tok],
        out_shape=[jax.ShapeDtypeStruct((s, nt * lw), F32), jax.ShapeDtypeStruct((s, nt * lw), F32),
                   jax.ShapeDtypeStruct((s, d), F32), jax.ShapeDtypeStruct((s, d), BF16)],
        scratch_shapes=[pltpu.VMEM((1, lw), F32), pltpu.VMEM((1, lw), F32)],
        compiler_params=_params(("parallel", "arbitrary")),
    )(u, bbd_re, bbd_im, cbd_re, cbd_im, pw_re, pw_im, dskip)


def _s5_bwd(dz, y, u, xs_re, xs_im, bbd_re, bbd_im, cbd_re, cbd_im, pw_re, pw_im, pf_re, pf_im, dskip, name):
    s, d = u.shape
    nt, cw, lw = bbd_re.shape
    t = pw_re.shape[0]
    nc = s // t

    def kern(dz_ref, y_ref, u_ref, xr_ref, xi_ref, bre_ref, bim_ref, cre_ref, cim_ref,
             pre_ref, pim_ref, fre_ref, fim_ref, d_ref,
             du_ref, dd_ref, dcr_ref, dci_ref, dbr_ref, dbi_ref, dar_ref, dai_ref, car_ref, cai_ref):
        c = pl.program_id(1)

        @pl.when(c == 0)
        def _():
            car_ref[...] = jnp.zeros_like(car_ref)
            cai_ref[...] = jnp.zeros_like(cai_ref)

        u_ = u_ref[...]
        ub = u_.astype(BF16)
        dy = dz_ref[...] * _gelu_grad(y_ref[...])
        dyb = dy.astype(BF16)
        gr = _dot_nt(dyb, cre_ref[...])
        gi = -_dot_nt(dyb, cim_ref[...])
        rows = lax.broadcasted_iota(jnp.int32, (t, lw), 0)
        sh = 1
        while sh < t:
            ar = pre_ref[sh - 1:sh, :]
            ai = pim_ref[sh - 1:sh, :]
            sr = _shift_up(gr, sh, rows, t)
            si = _shift_up(gi, sh, rows, t)
            gr, gi = gr + ar * sr + ai * si, gi + ar * si - ai * sr
            sh *= 2
        cr = car_ref[...]
        ci = cai_ref[...]
        fr = fre_ref[...]
        fi = fim_ref[...]
        gr, gi = gr + fr * cr + fi * ci, gi + fr * ci - fi * cr
        gsr = jnp.where(rows < t - 1, pltpu.roll(gr, t - 1, 0), cr)
        gsi = jnp.where(rows < t - 1, pltpu.roll(gi, t - 1, 0), ci)
        car_ref[...] = gr[0:1, :]
        cai_ref[...] = gi[0:1, :]
        xr = xr_ref[...]
        xi = xi_ref[...]
        dar = jnp.sum(gsr * xr + gsi * xi, axis=0, keepdims=True)
        dai = jnp.sum(gsi * xr - gsr * xi, axis=0, keepdims=True)
        grb = gr.astype(BF16)
        gib = gi.astype(BF16)
        dbr = _dot_tn(ub, grb)
        dbi = _dot_tn(ub, gib)
        dcr = _dot_tn(dyb, xr.astype(BF16))
        dci = _dot_tn(dyb, xi.astype(BF16))
        du_ref[...] = dy * d_ref[...] + _dot_nt(grb, bre_ref[...]) + _dot_nt(gib, bim_ref[...])
        ddv = jnp.sum(dy * u_, axis=0, keepdims=True)

        @pl.when(c == 0)
        def _():
            dd_ref[...] = ddv
            dcr_ref[...] = dcr
            dci_ref[...] = dci
            dbr_ref[...] = dbr
            dbi_ref[...] = dbi
            dar_ref[...] = dar
            dai_ref[...] = dai

        @pl.when(c > 0)
        def _():
            dd_ref[...] += ddv
            dcr_ref[...] += dcr
            dci_ref[...] += dci
            dbr_ref[...] += dbr
            dbi_ref[...] += dbi
            dar_ref[...] += dar
            dai_ref[...] += dai

    tok = pl.BlockSpec((t, cw), lambda j, c: (nc - 1 - c, j))
    st = pl.BlockSpec((t, lw), lambda j, c: (nc - 1 - c, j))
    wb = pl.BlockSpec((None, cw, lw), lambda j, c: (j, 0, 0))
    wc = pl.BlockSpec((None, lw, cw), lambda j, c: (j, 0, 0))
    pw = pl.BlockSpec((t, lw), lambda j, c: (0, j))
    vec_c = pl.BlockSpec((1, cw), lambda j, c: (0, j))
    vec_l = pl.BlockSpec((1, lw), lambda j, c: (0, j))
    return pl.pallas_call(
        kern, name=name, grid=(nt, nc),
        in_specs=[tok, tok, tok, st, st, wb, wb, wc, wc, pw, pw, pw, pw, vec_c],
        out_specs=[tok, vec_c, wb, wb, wb, wb, vec_l, vec_l],
        out_shape=[jax.ShapeDtypeStruct((s, d), F32), jax.ShapeDtypeStruct((1, d), F32)]
        + [jax.ShapeDtypeStruct((nt, cw, lw), F32)] * 4
        + [jax.ShapeDtypeStruct((1, nt * lw), F32)] * 2,
        scratch_shapes=[pltpu.VMEM((1, lw), F32), pltpu.VMEM((1, lw), F32)],
        compiler_params=_params(("parallel", "arbitrary")),
    )(dz, y, u, xs_re, xs_im, bbd_re, bbd_im, cbd_re, cbd_im, pw_re, pw_im, pf_re, pf_im, dskip)


def _s5_discretize(lam_re, lam_im, log_dt, b_re, b_im):
    dt = jnp.exp(log_dt)[:, None]
    mag = jnp.exp(lam_re * dt)
    ang = lam_im * dt
    lb_re = mag * jnp.cos(ang)
    lb_im = mag * jnp.sin(ang)
    nr = lb_re - 1.0
    den = lam_re * lam_re + lam_im * lam_im
    f_re = (nr * lam_re + lb_im * lam_im) / den
    f_im = (lb_im * lam_re - nr * lam_im) / den
    bb_re = f_re[..., None] * b_re - f_im[..., None] * b_im
    bb_im = f_re[..., None] * b_im + f_im[..., None] * b_re
    return lb_re, lb_im, bb_re, bb_im


def _block_diag(w):
    nt, ng, a, b = w.shape
    eye = jnp.eye(ng, dtype=w.dtype)
    return (w[:, :, :, None, :] * eye[None, :, None, :, None]).reshape(nt, ng * a, ng * b)


def _block_diag_take(w, a, b):
    nt = w.shape[0]
    ng = GROUPS_PER_TILE
    w5 = w.reshape(nt, ng, a, ng, b)
    idx = jnp.arange(ng)
    return w5[:, idx, :, idx, :].transpose(1, 0, 2, 3).reshape(nt * ng, a, b)


def _att_combine(outs, lses, name):
    s, d = outs[0].shape
    tm = _tile(s, ROWS, 16)
    ng = len(outs)

    def kern(*refs):
        ls = [refs[ng + t][...] for t in range(ng)]
        mx = functools.reduce(jnp.maximum, ls)
        es = [jnp.exp(l - mx) for l in ls]
        den = functools.reduce(lambda a, b: a + b, es)
        o = functools.reduce(lambda a, b: a + b, [es[t] / den * refs[t][...] for t in range(ng)])
        refs[2 * ng][...] = o
        refs[2 * ng + 1][...] = o.astype(BF16)

    return pl.pallas_call(
        kern, name=name, grid=(s // tm,),
        in_specs=[_row_spec(tm, d)] * (2 * ng),
        out_specs=[_row_spec(tm, d)] * 2,
        out_shape=[jax.ShapeDtypeStruct((s, d), F32), jax.ShapeDtypeStruct((s, d), BF16)],
        compiler_params=_params(("parallel",)),
    )(*outs, *lses)


DIL_MAX = 16


def _slab(r):
    return 4 * (r % 4) + r // 4


def _to_slabs(xs, name):
    s, w = xs[0].shape
    n = s // DIL_MAX
    nx = len(xs)

    def kern(*refs):
        o_ref = refs[nx]
        for r in range(DIL_MAX):
            rows = [x_ref[pl.ds(r, n, stride=DIL_MAX), :] for x_ref in refs[:nx]]
            o_ref[_slab(r) * n:(_slab(r) + 1) * n, :] = functools.reduce(lambda a, b: a + b, rows)

    spec = pl.BlockSpec((s, 128), lambda i: (0, i))
    return pl.pallas_call(
        kern, name=name, grid=(w // 128,), in_specs=[spec] * nx, out_specs=spec,
        out_shape=jax.ShapeDtypeStruct((s, w), F32), compiler_params=_params(("parallel",)),
    )(*xs)


def _from_slabs(x, name):
    s, w = x.shape
    n = s // DIL_MAX

    def kern(x_ref, o_ref, ob_ref):
        for r in range(DIL_MAX):
            o_ref[pl.ds(r, n, stride=DIL_MAX), :] = x_ref[_slab(r) * n:(_slab(r) + 1) * n, :]
        ob_ref[...] = o_ref[...].astype(BF16)

    spec = pl.BlockSpec((s, 128), lambda i: (0, i))
    return pl.pallas_call(
        kern, name=name, grid=(w // 128,), in_specs=[spec], out_specs=[spec, spec],
        out_shape=[jax.ShapeDtypeStruct((s, w), F32), jax.ShapeDtypeStruct((s, w), BF16)],
        compiler_params=_params(("parallel",)),
    )(x)


def _norm_bf16(x, gains, name):
    s, d = x.shape
    tm = _tile(s, ROWS, 16)
    ng = len(gains)

    def kern(x_ref, *refs):
        x_ = x_ref[...]
        for t in range(ng):
            refs[ng + t][...] = _rms(x_, refs[t][...]).astype(BF16)

    return pl.pallas_call(
        kern, name=name, grid=(s // tm,),
        in_specs=[_row_spec(tm, d)] + [_vec_spec(d)] * ng, out_specs=[_row_spec(tm, d)] * ng,
        out_shape=[jax.ShapeDtypeStruct((s, d), BF16)] * ng,
        compiler_params=_params(("parallel",)),
    )(x, *gains)


def _att_geometry(dil, s):
    g = DIL_MAX // dil
    return s // DIL_MAX, g, max(ATT_BLK // g, 16)


def _att_masks(g, j, jb):
    b = g * j

    def pos(i):
        sl, jj = i // j, i % j
        off = {1: 0, 4: sl, 16: sl // 4 + 4 * (sl % 4)}[g]
        return g * jj + off

    dist = pos(lax.broadcasted_iota(jnp.int32, (b, b), 0)) - pos(lax.broadcasted_iota(jnp.int32, (b, b), 1))
    cur = (dist >= 0) & (dist <= ATT_BLK)
    prev = (dist + b >= 0) & (dist + b <= ATT_BLK) & (jb > 0)
    return cur, prev


def _att_fwd(q, kv, gi, dil, nh, name):
    s = q.shape[0]
    n, g, j = _att_geometry(dil, s)
    b = g * j
    ng = q.shape[1] // (nh * HEAD_DIM)
    rep = nh // N_KV_HEADS
    qw = rep * HEAD_DIM
    scale = HEAD_DIM ** -0.5
    q3 = q.reshape(DIL_MAX, n, q.shape[1])
    kv3 = kv.reshape(DIL_MAX, n, kv.shape[1])

    def kern(q_ref, kc_ref, kp_ref, vc_ref, vp_ref, o_ref, l_ref):
        jb = pl.program_id(2)
        mc, mp = _att_masks(g, j, jb)
        kc, kp, vc, vp = [r[...].reshape(b, HEAD_DIM) for r in (kc_ref, kp_ref, vc_ref, vp_ref)]
        qa = q_ref[...].reshape(b, qw)
        for h in range(rep):
            cols = slice(h * HEAD_DIM, (h + 1) * HEAD_DIM)
            qh = qa[:, cols]
            sc = jnp.where(mc, _dot_nt(qh, kc) * scale, NEG_INF)
            sp = jnp.where(mp, _dot_nt(qh, kp) * scale, NEG_INF)
            m = jnp.maximum(jnp.max(sc, axis=-1, keepdims=True), jnp.max(sp, axis=-1, keepdims=True))
            pc = jnp.exp(sc - m)
            pp = jnp.exp(sp - m)
            l = jnp.sum(pc, axis=-1, keepdims=True) + jnp.sum(pp, axis=-1, keepdims=True)
            out = _dot((pc / l).astype(BF16), vc) + _dot((pp / l).astype(BF16), vp)
            o_ref[:, :, cols] = out.reshape(g, j, HEAD_DIM)
            l_ref[:, :, cols] = jnp.broadcast_to(m + jnp.log(l), (b, HEAD_DIM)).reshape(g, j, HEAD_DIM)

    def kv_spec(off, prev):
        if prev:
            return pl.BlockSpec((g, j, HEAD_DIM), lambda c, h, jb: (c, jnp.maximum(jb - 1, 0), off + h))
        return pl.BlockSpec((g, j, HEAD_DIM), lambda c, h, jb: (c, jb, off + h))

    out_spec = pl.BlockSpec((g, j, qw), lambda c, h, jb: (c, jb, h))
    out, lse = pl.pallas_call(
        kern, name=name, grid=(dil, N_KV_HEADS, n // j),
        in_specs=[pl.BlockSpec((g, j, qw), lambda c, h, jb: (c, jb, gi * N_KV_HEADS + h)),
                  kv_spec(0, False), kv_spec(0, True), kv_spec(N_KV_HEADS, False), kv_spec(N_KV_HEADS, True)],
        out_specs=[out_spec, out_spec],
        out_shape=[jax.ShapeDtypeStruct((DIL_MAX, n, nh * HEAD_DIM), F32)] * 2,
        compiler_params=_params(("parallel", "parallel", "arbitrary")),
    )(q3, kv3, kv3, kv3, kv3)
    del ng
    return out.reshape(s, nh * HEAD_DIM), lse.reshape(s, nh * HEAD_DIM)


def _att_bwd(q, kv, do, o, lses, gi, dil, nh, name):
    s = q.shape[0]
    n, g, j = _att_geometry(dil, s)
    b = g * j
    nb = n // j
    ng = len(lses)
    rep = nh // N_KV_HEADS
    qw = rep * HEAD_DIM
    dm = nh * HEAD_DIM
    scale = HEAD_DIM ** -0.5
    q3 = q.reshape(DIL_MAX, n, q.shape[1])
    kv3 = kv.reshape(DIL_MAX, n, kv.shape[1])
    wide = [a.reshape(DIL_MAX, n, dm) for a in (do, o, *lses)]

    def kern(q_ref, kc_ref, kp_ref, vc_ref, vp_ref, do_ref, o_ref, *refs):
        l_refs = refs[:ng]
        dq_ref, dk_ref, dv_ref, ck_ref, cv_ref = refs[ng:]
        jb = pl.program_id(2)

        @pl.when(jb == 0)
        def _():
            ck_ref[...] = jnp.zeros_like(ck_ref)
            cv_ref[...] = jnp.zeros_like(cv_ref)

        @pl.when(jb < nb)
        def _():
            mc, mp = _att_masks(g, j, jb)
            kc, kp, vc, vp = [r[...].reshape(b, HEAD_DIM) for r in (kc_ref, kp_ref, vc_ref, vp_ref)]
            qa = q_ref[...].reshape(b, qw)
            ls = [r[...].reshape(b, qw) for r in l_refs]
            mx = functools.reduce(jnp.maximum, ls)
            den = functools.reduce(lambda a, c: a + c, [jnp.exp(l - mx) for l in ls])
            lse_g = ls[gi]
            w = jnp.exp(lse_g - mx) / den
            do_ = do_ref[...].reshape(b, qw)
            dout = w * do_
            doo = do_ * o_ref[...].reshape(b, qw)
            dkc = jnp.zeros((b, HEAD_DIM), F32)
            dkp = jnp.zeros((b, HEAD_DIM), F32)
            dvc = jnp.zeros((b, HEAD_DIM), F32)
            dvp = jnp.zeros((b, HEAD_DIM), F32)
            for h in range(rep):
                cols = slice(h * HEAD_DIM, (h + 1) * HEAD_DIM)
                qh = qa[:, cols]
                lg = lse_g[:, h * HEAD_DIM:h * HEAD_DIM + 1]
                ct = w[:, h * HEAD_DIM:h * HEAD_DIM + 1] * jnp.sum(doo[:, cols], axis=-1, keepdims=True)
                dob = dout[:, cols].astype(BF16)
                pc = jnp.exp(jnp.where(mc, _dot_nt(qh, kc) * scale, NEG_INF) - lg)
                pp = jnp.exp(jnp.where(mp, _dot_nt(qh, kp) * scale, NEG_INF) - lg)
                dsc = (pc * (_dot_nt(dob, vc) - ct) * scale).astype(BF16)
                dsp = (pp * (_dot_nt(dob, vp) - ct) * scale).astype(BF16)
                dq_ref[:, :, cols] = (_dot(dsc, kc) + _dot(dsp, kp)).astype(BF16).reshape(g, j, HEAD_DIM)
                dkc = dkc + _dot_tn(dsc, qh)
                dkp = dkp + _dot_tn(dsp, qh)
                dvc = dvc + _dot_tn(pc.astype(BF16), dob)
                dvp = dvp + _dot_tn(pp.astype(BF16), dob)
            dk_ref[...] = (ck_ref[...] + dkp).reshape(g, j, HEAD_DIM)
            dv_ref[...] = (cv_ref[...] + dvp).reshape(g, j, HEAD_DIM)
            ck_ref[...] = dkc
            cv_ref[...] = dvc

        @pl.when(jb == nb)
        def _():
            dk_ref[...] = ck_ref[...].reshape(g, j, HEAD_DIM)
            dv_ref[...] = cv_ref[...].reshape(g, j, HEAD_DIM)

    def jq(jb):
        return jnp.minimum(jb, nb - 1)

    def kv_spec(off, prev):
        if prev:
            return pl.BlockSpec((g, j, HEAD_DIM), lambda c, h, jb: (c, jnp.maximum(jq(jb) - 1, 0), off + h))
        return pl.BlockSpec((g, j, HEAD_DIM), lambda c, h, jb: (c, jq(jb), off + h))

    wide_spec = pl.BlockSpec((g, j, qw), lambda c, h, jb: (c, jq(jb), h))
    dkv_spec = pl.BlockSpec((g, j, HEAD_DIM), lambda c, h, jb: (c, jnp.maximum(jb - 1, 0), h))
    dq, dk, dv = pl.pallas_call(
        kern, name=name, grid=(dil, N_KV_HEADS, nb + 1),
        in_specs=[pl.BlockSpec((g, j, qw), lambda c, h, jb: (c, jq(jb), gi * N_KV_HEADS + h)),
                  kv_spec(0, False), kv_spec(0, True), kv_spec(N_KV_HEADS, False), kv_spec(N_KV_HEADS, True)]
        + [wide_spec] * (2 + ng),
        out_specs=[wide_spec, dkv_spec, dkv_spec],
        out_shape=[jax.ShapeDtypeStruct((DIL_MAX, n, dm), BF16),
                   jax.ShapeDtypeStruct((DIL_MAX, n, N_KV_HEADS * HEAD_DIM), F32),
                   jax.ShapeDtypeStruct((DIL_MAX, n, N_KV_HEADS * HEAD_DIM), F32)],
        scratch_shapes=[pltpu.VMEM((b, HEAD_DIM), F32), pltpu.VMEM((b, HEAD_DIM), F32)],
        compiler_params=_params(("parallel", "parallel", "arbitrary")),
    )(q3, kv3, kv3, kv3, kv3, *wide)
    return dq.reshape(s, dm), dk.reshape(s, -1), dv.reshape(s, -1)


def _dkv_sum(dks, dvs, name):
    s, w = dks[0].shape
    tm = _tile(s, ROWS, 16)
    ng = len(dks)

    def kern(*refs):
        o_ref = refs[2 * ng]
        o_ref[:, :w] = functools.reduce(lambda a, b: a + b, [refs[t][...] for t in range(ng)]).astype(BF16)
        o_ref[:, w:] = functools.reduce(lambda a, b: a + b, [refs[ng + t][...] for t in range(ng)]).astype(BF16)

    return pl.pallas_call(
        kern, name=name, grid=(s // tm,),
        in_specs=[_row_spec(tm, w)] * (2 * ng),
        out_specs=_row_spec(tm, 2 * w),
        out_shape=jax.ShapeDtypeStruct((s, 2 * w), BF16),
        compiler_params=_params(("parallel",)),
    )(*dks, *dvs)


def _ffn_fwd(hf, weight, layer, tag):
    a, gu = _ffn_in_act(hf, weight(f"w_in{layer}", hf), f"{tag}_in")
    o = _mm_nn(a, weight(f"w_out{layer}", a), F32, f"{tag}_out", tn=1024, tk=1408)
    return gu, a, o


def _ffn_bwd(dresb, hf, gu, a, weight, emit, layer, tag):
    w_in, w_out = weight(f"w_in{layer}", None), weight(f"w_out{layer}", None)
    zero = emit(f"w_out{layer}", _mm_tn(a, dresb, 1, f"{tag}_out_dw", tkk=1408, tn=1024))
    dgu = _ffn_out_dx_act(dresb, w_out, gu, f"{tag}_out_dx")
    zero = zero + emit(f"w_in{layer}",
                       _mm_tn(hf, dgu, w_in.shape[0], f"{tag}_in_dw", tn=FFN_TILE, paired=True))
    return _mm_nt(dgu, w_in, f"{tag}_in_dx", tr=FFN_TILE, paired=True), zero


def _local_step(x, target, small, weight, emit):
    s, d = x.shape
    nh = d // HEAD_DIM
    n_groups = d // S5_GROUP_CH
    nt = n_groups // GROUPS_PER_TILE
    p_, c_ = S5_STATE, S5_GROUP_CH

    disc_in = (small["lam_re"], small["lam_im"], small["log_dt"], small["b_re"], small["b_im"])
    (lb_re, lb_im, bb_re, bb_im), disc_vjp = jax.vjp(_s5_discretize, *disc_in)
    del lb_re, lb_im
    dt = jnp.exp(small["log_dt"])[:, None]
    steps = jnp.arange(1, SCAN_T + 1, dtype=F32)[:, None, None]
    pmag = jnp.exp(steps * (small["lam_re"] * dt)[None])
    pang = steps * (small["lam_im"] * dt)[None]
    pw_re = (pmag * jnp.cos(pang)).reshape(SCAN_T, n_groups * p_)
    pw_im = (pmag * jnp.sin(pang)).reshape(SCAN_T, n_groups * p_)
    pf_re, pf_im = pw_re[::-1], pw_im[::-1]
    bbd_re = _block_diag(bb_re.transpose(0, 2, 1).reshape(nt, GROUPS_PER_TILE, c_, p_)).astype(BF16)
    bbd_im = _block_diag(bb_im.transpose(0, 2, 1).reshape(nt, GROUPS_PER_TILE, c_, p_)).astype(BF16)
    cbd_re = _block_diag(small["c_re"].transpose(0, 2, 1).reshape(nt, GROUPS_PER_TILE, p_, c_)).astype(BF16)
    cbd_im = _block_diag(small["c_im"].transpose(0, 2, 1).reshape(nt, GROUPS_PER_TILE, p_, c_)).astype(BF16)

    h0 = _norm_f32(x, small["a_norm"], "s5_norm")
    xs_re, xs_im, y0, z = _s5_fwd(h0, bbd_re, bbd_im, cbd_re, cbd_im, pw_re, pw_im, small["s5_d"], "s5_fwd")
    vg = _mm_nn(z, weight("w_glu", z), F32, "glu_mm", tn=1024)
    x1, hf0 = _glu_res_norm(vg, x, small["ffn_norm0"], "glu_res_norm")
    gu0, a0, o0 = _ffn_fwd(hf0, weight, 0, "ffn0")
    x2 = _to_slabs([o0, x1], "to_slabs")
    kvn, h1 = _norm_bf16(x2, [small["kv_norm"], small["b_norm"]], "att_norms")
    kv = _mm_nn(kvn, weight("w_kv", kvn), BF16, "kv_mm", tn=1024)
    q = _mm_nn(h1, weight("w_q", kv), BF16, "q_mm", tn=1536)
    outs, lses = [], []
    for gi, (window, dil) in enumerate(PATTERNS):
        assert window // dil == ATT_BLK
        og, lg = _att_fwd(q, kv, gi, dil, nh, f"att_fwd{gi}")
        outs.append(og)
        lses.append(lg)
    oatt, oattb = _att_combine(outs, lses, "att_combine")
    ao = _mm_nn(oattb, weight("w_o", oattb), F32, "o_mm", tn=1024)
    x3, hf1 = _res_norm(ao, x2, [small["ffn_norm1"]], "att_res_norm")
    gu1, a1, o1 = _ffn_fwd(hf1, weight, 1, "ffn1")
    dres, dresb, loss_rows, d_final = _loss_head(
        o1, x3, small["final_norm"], _to_slabs([target], "target_to_slabs"), "loss_head")

    dhf1, zero = _ffn_bwd(dresb, hf1, gu1, a1, weight, emit, 1, "ffn1")
    dres, dresb, d_ffn_norm1 = _norm_bwd(x3, [small["ffn_norm1"] + zero], [dhf1], dres, "ffn1_norm_bwd")
    zero = emit("w_o", _mm_tn(oattb, dresb, 1, "o_dw", tn=1024))
    doatt = _mm_nt(dresb, weight("w_o", None), "o_dx", tr=2048)
    dqs, dks, dvs = [], [], []
    for gi, (window, dil) in enumerate(PATTERNS):
        dq_g, dk_g, dv_g = _att_bwd(q, kv, doatt, oatt, lses, gi, dil, nh, f"att_bwd{gi}")
        dqs.append(dq_g)
        dks.append(dk_g)
        dvs.append(dv_g)
    dq = jnp.concatenate(dqs, axis=1)
    dkv = _dkv_sum(dks, dvs, "dkv_sum")
    w_q = weight("w_q", None)
    zero = zero + emit("w_q", _mm_tn(h1, dq, w_q.shape[0], "q_dw", tn=1536))
    zero = zero + emit("w_kv", _mm_tn(kvn, dkv, 1, "kv_dw", tn=1024))
    dh1 = _mm_nt(dq, w_q, "q_dx", tr=1536)
    dkvn = _mm_nt(dkv, weight("w_kv", None), "kv_dx", tr=1024)
    dres, dresb, d_b_norm, d_kv_norm = _norm_bwd(
        x2, [small["b_norm"] + zero, small["kv_norm"]], [dh1, dkvn], dres, "att_norm_bwd")
    dres, dresb = _from_slabs(dres, "from_slabs")
    dhf0, zero = _ffn_bwd(dresb, hf0, gu0, a0, weight, emit, 0, "ffn0")
    dres, dresb, d_ffn_norm0 = _norm_bwd(x1, [small["ffn_norm0"] + zero], [dhf0], dres, "ffn0_norm_bwd")
    del dresb
    dvg = _glu_bwd(dres, vg, "glu_bwd")
    w_glu = weight("w_glu", None)
    zero = emit("w_glu", _mm_tn(z, dvg, w_glu.shape[0], "glu_dw", tn=1024))
    dz = _mm_nt(dvg, w_glu, "glu_dx", tr=1024)
    dh0, d_s5_d, dcr, dci_neg, dbr, dbi, dar, dai = _s5_bwd(
        dz, y0, h0, xs_re, xs_im, bbd_re, bbd_im, cbd_re, cbd_im, pw_re, pw_im, pf_re, pf_im,
        small["s5_d"] + zero, "s5_bwd")
    grad_x, _, d_a_norm = _norm_bwd(x, [small["a_norm"]], [dh0], dres, "s5_norm_bwd")

    d_bb_re = _block_diag_take(dbr, c_, p_).transpose(0, 2, 1)
    d_bb_im = _block_diag_take(dbi, c_, p_).transpose(0, 2, 1)
    d_c_re = _block_diag_take(dcr, c_, p_)
    d_c_im = -_block_diag_take(dci_neg, c_, p_)
    d_lam_re, d_lam_im, d_log_dt, d_b_re, d_b_im = disc_vjp(
        (dar.reshape(n_groups, p_), dai.reshape(n_groups, p_), d_bb_re, d_bb_im))

    d_small = dict(lam_re=d_lam_re, lam_im=d_lam_im, log_dt=d_log_dt, b_re=d_b_re, b_im=d_b_im,
                   c_re=d_c_re, c_im=d_c_im, s5_d=d_s5_d, a_norm=d_a_norm, ffn_norm0=d_ffn_norm0,
                   ffn_norm1=d_ffn_norm1, b_norm=d_b_norm, kv_norm=d_kv_norm, final_norm=d_final)
    return loss_rows, grad_x, d_small


def _place():
    x, y, c = lax.axis_index("x"), lax.axis_index("y"), lax.axis_index("c")
    return x, y, c, [(1 - x, y), (x, 1 - y), (1 - x, 1 - y)]


_ANY = pl.BlockSpec(memory_space=pl.ANY)


_HBM = pl.BlockSpec(memory_space=pltpu.HBM)
_SEM = pl.BlockSpec(memory_space=pltpu.SEMAPHORE)
_EFFECT = pltpu.SideEffectType.DATAFLOW_SIDE_EFFECTING


def _in_hbm(a):
    return pltpu.with_memory_space_constraint(a, pltpu.HBM)


def _cast_place(shard, chip, name):
    r, c = shard.shape
    tm = _tile(r, ROWS, 16)

    def kern(chip_ref, x_ref, o_ref):
        del chip_ref
        o_ref[...] = x_ref[...].astype(BF16)

    return pl.pallas_call(
        kern, name=name,
        grid_spec=pltpu.PrefetchScalarGridSpec(
            num_scalar_prefetch=1, grid=(r // tm,),
            in_specs=[pl.BlockSpec((tm, c), lambda i, ch: (i, 0))],
            out_specs=pl.BlockSpec((None, tm, c), lambda i, ch: (ch[0], i, 0))),
        out_shape=jax.ShapeDtypeStruct((N_CHIPS, r, c), BF16),
        compiler_params=_params(("parallel",)),
    )(chip, shard)


def _gather_start(lands):
    n = len(lands)

    def body(*refs):
        land = refs[:n]
        send, recv = refs[n:2 * n], refs[2 * n:3 * n]
        x, y, c, peers = _place()
        me = 2 * x + y
        for a in range(n):
            for k, (px, py) in enumerate(peers):
                pltpu.make_async_remote_copy(
                    src_ref=land[a].at[me], dst_ref=land[a].at[me], send_sem=send[a].at[k], recv_sem=recv[a].at[k],
                    device_id=(px, py, c), device_id_type=MESH).start()

    outs = pl.pallas_call(
        body, name="gather_start",
        out_shape=[pltpu.SemaphoreType.DMA((3,))] * (2 * n) + [pltpu.HBM(a.shape, a.dtype) for a in lands],
        in_specs=[_HBM] * n, out_specs=[_SEM] * (2 * n) + [_HBM] * n,
        input_output_aliases={i: 2 * n + i for i in range(n)},
        compiler_params=pltpu.CompilerParams(has_side_effects=_EFFECT),
    )(*[_in_hbm(a) for a in lands])
    return outs[:n], outs[n:2 * n], outs[2 * n:]


def _gather_wait(lands, sends, recvs, after, name):
    n = len(lands)

    def body(*refs):
        land, send, recv = refs[:n], refs[n:2 * n], refs[2 * n:3 * n]
        x, y, c, peers = _place()
        me = 2 * x + y
        for a in range(n):
            for k, (px, py) in enumerate(peers):
                cp = pltpu.make_async_remote_copy(
                    src_ref=land[a].at[me], dst_ref=land[a].at[2 * px + py], send_sem=send[a].at[k],
                    recv_sem=recv[a].at[k], device_id=(px, py, c), device_id_type=MESH)
                cp.wait_send()
                cp.wait_recv()

    return pl.pallas_call(
        body, name=name,
        out_shape=[pltpu.HBM(a.shape, a.dtype) for a in lands],
        in_specs=[_HBM] * n + [_SEM] * (2 * n) + [_ANY], out_specs=[_HBM] * n,
        input_output_aliases={i: i for i in range(n)},
        compiler_params=pltpu.CompilerParams(has_side_effects=_EFFECT),
    )(*lands, *sends, *recvs, after)


def _scatter_start(g, name):
    def body(g_ref, land_ref, send, recv, g_out, land_out, token):
        del g_out, land_out
        x, y, c, peers = _place()
        for k, (px, py) in enumerate(peers):
            pltpu.make_async_remote_copy(
                src_ref=g_ref.at[2 * px + py], dst_ref=land_ref.at[k], send_sem=send.at[k], recv_sem=recv.at[k],
                device_id=(px, py, c), device_id_type=MESH).start()
        token[...] = jnp.zeros_like(token)

    land = lax.empty((3,) + g.shape[1:], g.dtype)
    return pl.pallas_call(
        body, name=name,
        out_shape=(pltpu.SemaphoreType.DMA((3,)), pltpu.SemaphoreType.DMA((3,)),
                   pltpu.HBM(g.shape, g.dtype), pltpu.HBM(land.shape, land.dtype),
                   jax.ShapeDtypeStruct((8, 128), F32)),
        in_specs=(_HBM, _HBM), out_specs=(_SEM, _SEM, _HBM, _HBM, pl.BlockSpec(memory_space=pltpu.VMEM)),
        input_output_aliases={0: 2, 1: 3},
        compiler_params=pltpu.CompilerParams(has_side_effects=_EFFECT),
    )(_in_hbm(g), _in_hbm(land))


def _scatter_wait(started, after):
    n = len(started)

    def body(*refs):
        gs, lands = refs[:n], refs[n:2 * n]
        sends, recvs = refs[2 * n:3 * n], refs[3 * n:4 * n]
        x, y, c, peers = _place()
        for a in range(n):
            for k, (px, py) in enumerate(peers):
                cp = pltpu.make_async_remote_copy(
                    src_ref=gs[a].at[2 * px + py], dst_ref=lands[a].at[k], send_sem=sends[a].at[k],
                    recv_sem=recvs[a].at[k], device_id=(px, py, c), device_id_type=MESH)
                cp.wait_send()
                cp.wait_recv()

    gs = [s[2] for s in started]
    lands = [s[3] for s in started]
    outs = pl.pallas_call(
        body, name="scatter_wait",
        out_shape=[pltpu.HBM(a.shape, a.dtype) for a in gs + lands],
        in_specs=[_HBM] * (2 * n) + [_SEM] * (2 * n) + [_ANY], out_specs=[_HBM] * (2 * n),
        input_output_aliases={i: i for i in range(2 * n)},
        compiler_params=pltpu.CompilerParams(has_side_effects=_EFFECT),
    )(*gs, *lands, *[s[0] for s in started], *[s[1] for s in started], after)
    return outs[:n], outs[n:]


def _swap_cores(parts):
    n = len(parts)

    def body(*refs):
        ins, outs = refs[:n], refs[n:2 * n]
        send_sems, recv_sems = refs[2 * n:]
        x, y, c = lax.axis_index("x"), lax.axis_index("y"), lax.axis_index("c")
        copies = []
        for a in range(n):
            cp = pltpu.make_async_remote_copy(
                src_ref=ins[a], dst_ref=outs[a], send_sem=send_sems.at[a], recv_sem=recv_sems.at[a],
                device_id=(x, y, 1 - c), device_id_type=MESH)
            cp.start()
            copies.append(cp)
        for cp in copies:
            cp.wait()

    return pl.pallas_call(
        body, name="swap_cores",
        in_specs=[_ANY] * n, out_specs=[_ANY] * n,
        out_shape=[jax.ShapeDtypeStruct(a.shape, a.dtype) for a in parts],
        scratch_shapes=[pltpu.SemaphoreType.DMA((n,)), pltpu.SemaphoreType.DMA((n,))],
    )(*parts)


def _all_reduce_small(v):
    nd, r, w = v.shape
    assert nd == N_DEV

    def body(v_ref, out_ref, land_ref, red_ref, send1, recv1, send2, recv2):
        x, y, c = lax.axis_index("x"), lax.axis_index("y"), lax.axis_index("c")
        me = 4 * x + 2 * y + c
        peers = []
        for k in range(1, N_DEV):
            kx, ky, kc = (k >> 2) & 1, (k >> 1) & 1, k & 1
            peers.append((1 - x if kx else x, 1 - y if ky else y, 1 - c if kc else c))
        first = []
        for k, (px, py, pc) in enumerate(peers):
            cp = pltpu.make_async_remote_copy(
                src_ref=v_ref.at[4 * px + 2 * py + pc], dst_ref=land_ref.at[me], send_sem=send1.at[k],
                recv_sem=recv1.at[k], device_id=(px, py, pc), device_id_type=MESH)
            cp.start()
            first.append(cp)
        land_ref[me] = v_ref[me]
        for cp in first:
            cp.wait()
        acc = land_ref[0]
        for j in range(1, N_DEV):
            acc = acc + land_ref[j]
        red_ref[...] = acc
        second = []
        for k, (px, py, pc) in enumerate(peers):
            cp = pltpu.make_async_remote_copy(
                src_ref=red_ref, dst_ref=out_ref.at[me], send_sem=send2.at[k],
                recv_sem=recv2.at[k], device_id=(px, py, pc), device_id_type=MESH)
            cp.start()
            second.append(cp)
        out_ref[me] = acc
        for cp in second:
            cp.wait()

    vmem = pl.BlockSpec(memory_space=pltpu.VMEM)
    return pl.pallas_call(
        body, name="all_reduce_small",
        in_specs=[vmem], out_specs=vmem,
        out_shape=jax.ShapeDtypeStruct((nd, r, w), F32),
        scratch_shapes=[pltpu.VMEM((nd, r, w), F32), pltpu.VMEM((r, w), F32)]
        + [pltpu.SemaphoreType.DMA((N_DEV - 1,))] * 4,
        compiler_params=pltpu.CompilerParams(vmem_limit_bytes=VMEM_LIMIT),
    )(v)


def _adam_math(w, g, m, v):
    m = ADAM_B1 * m + (1.0 - ADAM_B1) * g
    v = ADAM_B2 * v + (1.0 - ADAM_B2) * (g * g)
    m_hat = m / (1.0 - ADAM_B1 ** ADAM_STEP)
    v_hat = v / (1.0 - ADAM_B2 ** ADAM_STEP)
    delta = -ADAM_LR * (m_hat / (jnp.sqrt(v_hat) + ADAM_EPS) + ADAM_WD * w)
    return delta, m, v


def _sum_blocks(own, got, chip, name):
    _, r, c = own.shape
    tm = _tile(r, ROWS, 16)

    def kern(chip_ref, own_ref, got_ref, o_ref):
        del chip_ref
        acc = own_ref[...].astype(F32)
        for k in range(3):
            acc = acc + got_ref[k].astype(F32)
        o_ref[...] = acc

    return pl.pallas_call(
        kern, name=name,
        grid_spec=pltpu.PrefetchScalarGridSpec(
            num_scalar_prefetch=1, grid=(r // tm,),
            in_specs=[pl.BlockSpec((None, tm, c), lambda i, ch: (ch[0], i, 0)),
                      pl.BlockSpec((3, tm, c), lambda i, ch: (0, i, 0))],
            out_specs=pl.BlockSpec((tm, c), lambda i, ch: (i, 0))),
        out_shape=jax.ShapeDtypeStruct((r, c), F32),
        compiler_params=_params(("parallel",)),
    )(chip, own, got)


def _adamw(p0, p1, w, m, v, name):
    r, c = w.shape
    tm = _tile(r, 128, 8)

    def kern(p0_ref, p1_ref, w_ref, m_ref, v_ref, g_ref, d_ref, mo_ref, vo_ref):
        g = p0_ref[...] + p1_ref[...]
        g_ref[...] = g
        d_ref[...], mo_ref[...], vo_ref[...] = _adam_math(w_ref[...], g, m_ref[...], v_ref[...])

    spec = pl.BlockSpec((tm, c), lambda i: (i, 0))
    return pl.pallas_call(
        kern, name=name, grid=(r // tm,),
        in_specs=[spec] * 5, out_specs=[spec] * 4,
        out_shape=[jax.ShapeDtypeStruct((r, c), F32)] * 4,
        compiler_params=_params(("parallel",)),
    )(p0, p1, w, m, v)


def _adamw_small(g, w, m, v, name):
    def kern(g_ref, w_ref, m_ref, v_ref, d_ref, mo_ref, vo_ref):
        d_ref[...], mo_ref[...], vo_ref[...] = _adam_math(w_ref[...], g_ref[...], m_ref[...], v_ref[...])

    return pl.pallas_call(
        kern, name=name,
        out_shape=[jax.ShapeDtypeStruct(g.shape, F32)] * 3,
        compiler_params=pltpu.CompilerParams(vmem_limit_bytes=VMEM_LIMIT),
    )(g, w, m, v)


def _pack(arrays, rows):
    flat = jnp.concatenate([a.reshape(-1).astype(F32) for a in arrays])
    return jnp.pad(flat, (0, rows * 128 - flat.shape[0])).reshape(rows, 128)


def _unpack(packed, shapes):
    flat = packed.reshape(-1)
    out, off = [], 0
    for shp in shapes:
        size = math.prod(shp)
        out.append(flat[off:off + size].reshape(shp))
        off += size
    return out


_REPLICATED = ["s5_lam_re", "s5_lam_im", "s5_log_dt", "s5_b_re", "s5_b_im", "s5_c_re", "s5_c_im",
               "ffn_norm", "b_norm_mix", "kv_norm", "final_norm"]
_CHIP_VECTORS = ["s5_d", "a_norm_mix"]
_BIG = ["s5_w_glu", "ffn_w_in", "ffn_w_out", "attn_w_q", "attn_w_o", "w_kv"]
_WEIGHT_ORDER = ["s5_lam_re", "s5_lam_im", "s5_log_dt", "s5_b_re", "s5_b_im", "s5_c_re", "s5_c_im", "s5_d",
                 "s5_w_glu", "a_norm_mix", "ffn_norm", "ffn_w_in", "ffn_w_out", "b_norm_mix", "attn_w_q",
                 "attn_w_o", "kv_norm", "w_kv", "final_norm"]


def _step(x, loss_target, w, m, v):
    s, d = x.shape[1], x.shape[2]
    chip = 2 * lax.axis_index("x") + lax.axis_index("y")

    col_sharded = dict(w_glu=w["s5_w_glu"][0], w_in0=w["ffn_w_in"][0], w_in1=w["ffn_w_in"][1], w_q=w["attn_w_q"][0])
    row_sharded = dict(w_out0=w["ffn_w_out"][0], w_out1=w["ffn_w_out"][1], w_o=w["attn_w_o"][0], w_kv=w["w_kv"])
    names = ["w_glu", "w_in0", "w_out0", "w_kv", "w_q", "w_o", "w_in1", "w_out1"]
    local = {**col_sharded, **row_sharded}
    chip_arr = jnp.reshape(chip, (1,)).astype(jnp.int32)
    vector_lands = [lax.dynamic_update_slice(jnp.zeros((N_CHIPS,) + w[n].shape, F32), w[n][None], (chip, 0, 0))
                    for n in _CHIP_VECTORS]
    lands = [_cast_place(local[n], chip_arr, f"cast_{n}") for n in names]
    send, recv, land_thru = _gather_start(vector_lands + lands)
    vectors = _gather_wait(land_thru[:2], send[:2], recv[:2], jnp.zeros((8, 128), F32), "gather_wait_vectors")
    s5_d_full = vectors[0].reshape(1, d)
    a_norm_full = vectors[1].reshape(1, d)
    arrived = {}

    def weight(name, after):
        if name not in arrived:
            i = 2 + names.index(name)
            land, = _gather_wait([land_thru[i]], [send[i]], [recv[i]], after, f"gather_wait_{name}")
            arrived[name] = land if name in col_sharded else land.reshape(1, -1, land.shape[-1])
        return arrived[name]

    started = {}

    def emit(name, dw):
        outs = _scatter_start(dw.reshape(N_CHIPS, -1, dw.shape[-1]), f"scatter_start_{name}")
        started[name] = outs[:4]
        return outs[4][0, 0]

    small = dict(lam_re=w["s5_lam_re"][0], lam_im=w["s5_lam_im"][0], log_dt=w["s5_log_dt"][0],
                 b_re=w["s5_b_re"][0], b_im=w["s5_b_im"][0], c_re=w["s5_c_re"][0], c_im=w["s5_c_im"][0],
                 s5_d=s5_d_full, a_norm=a_norm_full, ffn_norm0=w["ffn_norm"][0:1], ffn_norm1=w["ffn_norm"][1:2],
                 b_norm=w["b_norm_mix"], kv_norm=w["kv_norm"].reshape(1, d), final_norm=w["final_norm"].reshape(1, d))

    loss_rows, grad_x, d_small = _local_step(x[0], loss_target[0], small, weight, emit)

    g4, got = _scatter_wait([started[n] for n in names], grad_x)
    chip_arr = jnp.reshape(chip, (1,)).astype(jnp.int32)
    partial = [_sum_blocks(o, r, chip_arr, f"sum_{n}") for n, o, r in zip(names, g4, got)]
    other = _swap_cores(partial)
    part = dict(zip(names, zip(partial, other)))

    def two_layers(name0, name1):
        (a0, b0), (a1, b1) = part[name0], part[name1]
        return jnp.concatenate([a0, a1], axis=0), jnp.concatenate([b0, b1], axis=0)

    big_parts = {"s5_w_glu": part["w_glu"], "ffn_w_in": two_layers("w_in0", "w_in1"),
                 "ffn_w_out": two_layers("w_out0", "w_out1"), "attn_w_q": part["w_q"],
                 "attn_w_o": part["w_o"], "w_kv": part["w_kv"]}
    result = {}
    for name in _BIG:
        p0, p1 = big_parts[name]
        shape = w[name].shape
        flat = lambda a: a.reshape(-1, shape[-1])
        outs = _adamw(p0, p1, flat(w[name]), flat(m[name]), flat(v[name]), f"adamw_{name}")
        result[name] = [o.reshape(shape) for o in outs]

    rep_grads = [d_small["lam_re"], d_small["lam_im"], d_small["log_dt"], d_small["b_re"], d_small["b_im"],
                 d_small["c_re"], d_small["c_im"],
                 jnp.concatenate([d_small["ffn_norm0"], d_small["ffn_norm1"]], axis=0),
                 d_small["b_norm"], d_small["kv_norm"], d_small["final_norm"]]
    vec_grads = [d_small["s5_d"], d_small["a_norm"]]
    to_reduce = rep_grads + vec_grads + [jnp.sum(loss_rows).reshape(1)]
    total = sum(math.prod(a.shape) for a in to_reduce)
    rows_per = -(-total // (N_DEV * 128 * 8)) * 8
    reduced = _all_reduce_small(_pack(to_reduce, N_DEV * rows_per).reshape(N_DEV, rows_per, 128))
    red = _unpack(reduced, [a.shape for a in to_reduce])
    loss = red[-1][0]
    g_small = dict(zip(_REPLICATED, [r.reshape(w[n].shape) for r, n in zip(red[:len(rep_grads)], _REPLICATED)]))
    for n, r in zip(_CHIP_VECTORS, red[len(rep_grads):-1]):
        g_small[n] = lax.dynamic_slice_in_dim(r.reshape(1, d), chip * (d // N_CHIPS), d // N_CHIPS, axis=1)
    small_names = _REPLICATED + _CHIP_VECTORS
    n_small = sum(math.prod(w[n].shape) for n in small_names)
    rows_small = -(-n_small // (128 * 8)) * 8
    packed = [_pack([src[n] for n in small_names], rows_small) for src in (g_small, w, m, v)]
    upd = _adamw_small(*packed, "adamw_small")
    shapes = [w[n].shape for n in small_names]
    for n, dl, mo, vo in zip(small_names, *[_unpack(u, shapes) for u in upd]):
        result[n] = [g_small[n], dl, mo, vo]

    cols = [[result[n][t] for n in _WEIGHT_ORDER] for t in range(4)]
    return (loss, grad_x.reshape(x.shape), *cols[0], *cols[1], *cols[2], *cols[3])


def kernel(x, s5_lam_re, s5_lam_im, s5_log_dt, s5_b_re, s5_b_im, s5_c_re, s5_c_im, s5_d, s5_w_glu, a_norm_mix, ffn_norm, ffn_w_in, ffn_w_out, b_norm_mix, attn_w_q, attn_w_o, kv_norm, w_kv, final_norm, loss_target, m_s5_lam_re, m_s5_lam_im, m_s5_log_dt, m_s5_b_re, m_s5_b_im, m_s5_c_re, m_s5_c_im, m_s5_d, m_s5_w_glu, m_a_norm_mix, m_ffn_norm, m_ffn_w_in, m_ffn_w_out, m_b_norm_mix, m_attn_w_q, m_attn_w_o, m_kv_norm, m_w_kv, m_final_norm, v_s5_lam_re, v_s5_lam_im, v_s5_log_dt, v_s5_b_re, v_s5_b_im, v_s5_c_re, v_s5_c_im, v_s5_d, v_s5_w_glu, v_a_norm_mix, v_ffn_norm, v_ffn_w_in, v_ffn_w_out, v_b_norm_mix, v_attn_w_q, v_attn_w_o, v_kv_norm, v_w_kv, v_final_norm):
    w = dict(zip(_WEIGHT_ORDER, (s5_lam_re, s5_lam_im, s5_log_dt, s5_b_re, s5_b_im, s5_c_re, s5_c_im, s5_d, s5_w_glu, a_norm_mix, ffn_norm, ffn_w_in, ffn_w_out, b_norm_mix, attn_w_q, attn_w_o, kv_norm, w_kv, final_norm)))
    m = dict(zip(_WEIGHT_ORDER, (m_s5_lam_re, m_s5_lam_im, m_s5_log_dt, m_s5_b_re, m_s5_b_im, m_s5_c_re, m_s5_c_im, m_s5_d, m_s5_w_glu, m_a_norm_mix, m_ffn_norm, m_ffn_w_in, m_ffn_w_out, m_b_norm_mix, m_attn_w_q, m_attn_w_o, m_kv_norm, m_w_kv, m_final_norm)))
    v = dict(zip(_WEIGHT_ORDER, (v_s5_lam_re, v_s5_lam_im, v_s5_log_dt, v_s5_b_re, v_s5_b_im, v_s5_c_re, v_s5_c_im, v_s5_d, v_s5_w_glu, v_a_norm_mix, v_ffn_norm, v_ffn_w_in, v_ffn_w_out, v_b_norm_mix, v_attn_w_q, v_attn_w_o, v_kv_norm, v_w_kv, v_final_norm)))
    return _step(x, loss_target, w, m, v)
```

```python
import functools
import math

import jax
import jax.numpy as jnp
from jax import lax
from jax.experimental import pallas as pl
from jax.experimental.pallas import tpu as pltpu

F32 = jnp.float32
BF16 = jnp.bfloat16

S5_GROUP_CH = 16
S5_STATE = 64
GROUPS_PER_TILE = 8
HEAD_DIM = 128
N_KV_HEADS = 4
PATTERNS = ((128, 1), (512, 4), (2048, 16))
ATT_BLK = 128
EPS = 1e-6
NEG_INF = -1e30
SCAN_T = 256
ADAM_LR = 0.001
ADAM_B1 = 0.9
ADAM_B2 = 0.999
ADAM_EPS = 1e-08
ADAM_WD = 0.01
ADAM_STEP = 10
N_CHIPS = 4
N_DEV = 8
VMEM_LIMIT = 56 * 1024 * 1024
MESH = pl.DeviceIdType.MESH
GELU_K = math.sqrt(2.0 / math.pi)
GELU_C = 0.044715


def _tile(n, pref, unit=128):
    if n <= pref:
        return n
    best = None
    t = unit
    while t <= pref:
        if n % t == 0:
            best = t
        t += unit
    assert best is not None, (n, pref, unit)
    return best


def _params(sem):
    return pltpu.CompilerParams(dimension_semantics=sem, vmem_limit_bytes=VMEM_LIMIT)


def _dot(a, b):
    return jnp.dot(a, b, preferred_element_type=F32)


def _dot_nt(a, b):
    return lax.dot_general(a, b, (((1,), (1,)), ((), ())), preferred_element_type=F32)


def _dot_tn(a, b):
    return lax.dot_general(a, b, (((0,), (0,)), ((), ())), preferred_element_type=F32)


def _mm_nn(a, w, out_dtype, name, tm=512, tn=1536, tk=2048):
    m, k = a.shape
    nb, k2, nq = w.shape
    assert k == k2
    tm, tn, tk = _tile(m, tm, 8), _tile(nq, tn), _tile(k, tk)
    per, nk = nq // tn, k // tk

    def kern(a_ref, w_ref, o_ref, *acc):
        p = _dot(a_ref[...], w_ref[...])
        if nk == 1:
            o_ref[...] = p.astype(o_ref.dtype)
        else:
            acc_ref, = acc
            kk = pl.program_id(2)

            @pl.when(kk == 0)
            def _():
                acc_ref[...] = p

            @pl.when(kk > 0)
            def _():
                acc_ref[...] += p

            @pl.when(kk == nk - 1)
            def _():
                o_ref[...] = acc_ref[...].astype(o_ref.dtype)

    return pl.pallas_call(
        kern, name=name, grid=(nb * per, m // tm, nk),
        in_specs=[pl.BlockSpec((tm, tk), lambda j, i, kk: (i, kk)),
                  pl.BlockSpec((None, tk, tn), lambda j, i, kk: (j // per, kk, j % per))],
        out_specs=pl.BlockSpec((tm, tn), lambda j, i, kk: (i, j)),
        out_shape=jax.ShapeDtypeStruct((m, nb * nq), out_dtype),
        scratch_shapes=[] if nk == 1 else [pltpu.VMEM((tm, tn), F32)],
        compiler_params=_params(("parallel", "parallel", "arbitrary")),
    )(a, w)


def _paired_block(r, per, nb):
    j = r // 2
    return (r % 2) * (nb // 2) + j // per, j % per


def _mm_nt(a, w, name, tm=512, tr=1536, tkc=2048, paired=False):
    m, n = a.shape
    nb, k, nq = w.shape
    assert n == nb * nq
    tm, tr, tkc = _tile(m, tm, 8), _tile(nq, tr), _tile(k, tkc)
    per = nq // tr
    nr = nb * per

    def w_block(r):
        return _paired_block(r, per, nb) if paired else (r // per, r % per)

    def kern(a_ref, w_ref, o_ref, *acc):
        p = _dot_nt(a_ref[...], w_ref[...])
        if nr == 1:
            o_ref[...] = p
        else:
            acc_ref, = acc
            r = pl.program_id(2)

            @pl.when(r == 0)
            def _():
                acc_ref[...] = p

            @pl.when(r > 0)
            def _():
                acc_ref[...] += p

            @pl.when(r == nr - 1)
            def _():
                o_ref[...] = acc_ref[...]

    return pl.pallas_call(
        kern, name=name, grid=(m // tm, k // tkc, nr),
        in_specs=[pl.BlockSpec((tm, tr), lambda i, kc, r: (i, r)),
                  pl.BlockSpec((None, tkc, tr), lambda i, kc, r: (w_block(r)[0], kc, w_block(r)[1]))],
        out_specs=pl.BlockSpec((tm, tkc), lambda i, kc, r: (i, kc)),
        out_shape=jax.ShapeDtypeStruct((m, k), F32),
        scratch_shapes=[] if nr == 1 else [pltpu.VMEM((tm, tkc), F32)],
        compiler_params=_params(("parallel", "parallel", "arbitrary")),
    )(a, w)


def _mm_tn(a, dy, nb, name, ts=1024, tkk=1024, tn=1536, paired=False):
    s, k = a.shape
    s2, n = dy.shape
    assert s == s2 and n % nb == 0
    nq = n // nb
    ts, tkk, tn = _tile(s, ts, 16), _tile(k, tkk), _tile(nq, tn)
    per, ns = nq // tn, s // ts

    def w_block(j):
        return _paired_block(j, per, nb) if paired else (j // per, j % per)

    def kern(a_ref, dy_ref, o_ref, acc_ref):
        si = pl.program_id(2)
        p = _dot_tn(a_ref[...], dy_ref[...])

        @pl.when(si == 0)
        def _():
            acc_ref[...] = p

        @pl.when(si > 0)
        def _():
            acc_ref[...] += p

        @pl.when(si == ns - 1)
        def _():
            o_ref[...] = acc_ref[...].astype(o_ref.dtype)

    return pl.pallas_call(
        kern, name=name, grid=(k // tkk, nb * per, ns),
        in_specs=[pl.BlockSpec((ts, tkk), lambda kk, j, si: (si, kk)),
                  pl.BlockSpec((ts, tn), lambda kk, j, si: (si, j))],
        out_specs=pl.BlockSpec((None, tkk, tn), lambda kk, j, si: (w_block(j)[0], kk, w_block(j)[1])),
        out_shape=jax.ShapeDtypeStruct((nb, k, nq), BF16),
        scratch_shapes=[pltpu.VMEM((tkk, tn), F32)],
        compiler_params=_params(("parallel", "parallel", "arbitrary")),
    )(a, dy)


ROWS = 256


def _rms(x, g):
    r = lax.rsqrt(jnp.mean(x * x, axis=-1, keepdims=True) + EPS)
    return x * r * g


def _rms_bwd(x, g, dh):
    r = lax.rsqrt(jnp.mean(x * x, axis=-1, keepdims=True) + EPS)
    xh = x * r
    dgx = dh * g
    dx = r * (dgx - xh * jnp.mean(dgx * xh, axis=-1, keepdims=True))
    return dx, dh * xh


def _sigmoid(x):
    return 1.0 / (1.0 + jnp.exp(-x))


def _gelu(y):
    return 0.5 * y * (1.0 + jnp.tanh(GELU_K * (y + GELU_C * y * y * y)))


def _gelu_grad(y):
    t = jnp.tanh(GELU_K * (y + GELU_C * y * y * y))
    return 0.5 * (1.0 + t) + 0.5 * y * (1.0 - t * t) * GELU_K * (1.0 + 3.0 * GELU_C * y * y)


def _row_spec(tm, d, col=0):
    return pl.BlockSpec((tm, d), lambda i: (i, col))


def _vec_spec(d):
    return pl.BlockSpec((1, d), lambda i: (0, 0))


def _acc_rows(ref, val, i):
    s = jnp.sum(val, axis=0, keepdims=True)

    @pl.when(i == 0)
    def _():
        ref[...] = s

    @pl.when(i > 0)
    def _():
        ref[...] += s


def _norm_f32(x, g, name):
    s, d = x.shape
    tm = _tile(s, ROWS, 8)

    def kern(x_ref, g_ref, h_ref):
        h_ref[...] = _rms(x_ref[...], g_ref[...])

    return pl.pallas_call(
        kern, name=name, grid=(s // tm,),
        in_specs=[_row_spec(tm, d), _vec_spec(d)],
        out_specs=_row_spec(tm, d),
        out_shape=jax.ShapeDtypeStruct((s, d), F32),
        compiler_params=_params(("parallel",)),
    )(x, g)


def _glu_res_norm(vg, x, g, name):
    s, d = x.shape
    tm = _tile(s, ROWS, 16)

    def kern(val_ref, gate_ref, x_ref, g_ref, x1_ref, hf_ref):
        x1 = x_ref[...] + val_ref[...] * _sigmoid(gate_ref[...])
        x1_ref[...] = x1
        hf_ref[...] = _rms(x1, g_ref[...]).astype(BF16)

    return pl.pallas_call(
        kern, name=name, grid=(s // tm,),
        in_specs=[_row_spec(tm, d, 0), _row_spec(tm, d, 1), _row_spec(tm, d), _vec_spec(d)],
        out_specs=[_row_spec(tm, d), _row_spec(tm, d)],
        out_shape=[jax.ShapeDtypeStruct((s, d), F32), jax.ShapeDtypeStruct((s, d), BF16)],
        compiler_params=_params(("parallel",)),
    )(vg, vg, x, g)


FFN_TILE = 1408


def _ffn_in_act(hf, w_in, name, tm=512):
    s, k = hf.shape
    nb, _, nq = w_in.shape
    tm, tn = _tile(s, tm, 16), _tile(nq, FFN_TILE)
    per = nq // tn
    nf = (nb // 2) * per

    def kern(h_ref, wg_ref, wu_ref, a_ref, gu_ref):
        h = h_ref[...]
        g = _dot(h, wg_ref[...])
        u = _dot(h, wu_ref[...])
        a_ref[...] = (g * _sigmoid(g) * u).astype(BF16)
        gu_ref[:, :tn] = g.astype(BF16)
        gu_ref[:, tn:] = u.astype(BF16)

    return pl.pallas_call(
        kern, name=name, grid=(nf, s // tm),
        in_specs=[pl.BlockSpec((tm, k), lambda j, i: (i, 0)),
                  pl.BlockSpec((None, k, tn), lambda j, i: (j // per, 0, j % per)),
                  pl.BlockSpec((None, k, tn), lambda j, i: (nb // 2 + j // per, 0, j % per))],
        out_specs=[pl.BlockSpec((tm, tn), lambda j, i: (i, j)),
                   pl.BlockSpec((tm, 2 * tn), lambda j, i: (i, j))],
        out_shape=[jax.ShapeDtypeStruct((s, nf * tn), BF16), jax.ShapeDtypeStruct((s, 2 * nf * tn), BF16)],
        compiler_params=_params(("parallel", "parallel")),
    )(hf, w_in, w_in)


def _ffn_out_dx_act(dresb, w_out, gu, name, tm=512):
    s, d = dresb.shape
    f = w_out.shape[1]
    tm = _tile(s, tm, 16)
    tn = _tile(f // 2, FFN_TILE)

    def kern(d_ref, w_ref, gu_ref, o_ref):
        da = _dot_nt(d_ref[...], w_ref[...])
        g = gu_ref[:, :tn].astype(F32)
        u = gu_ref[:, tn:].astype(F32)
        sg = _sigmoid(g)
        o_ref[:, :tn] = (da * u * sg * (1.0 + g * (1.0 - sg))).astype(BF16)
        o_ref[:, tn:] = (da * g * sg).astype(BF16)

    pair = pl.BlockSpec((tm, 2 * tn), lambda j, i: (i, j))
    return pl.pallas_call(
        kern, name=name, grid=(f // tn, s // tm),
        in_specs=[pl.BlockSpec((tm, d), lambda j, i: (i, 0)),
                  pl.BlockSpec((None, tn, d), lambda j, i: (0, j, 0)), pair],
        out_specs=pair,
        out_shape=jax.ShapeDtypeStruct((s, 2 * f), BF16),
        compiler_params=_params(("parallel", "parallel")),
    )(dresb, w_out, gu)


def _res_norm(o, x, gains, name):
    s, d = x.shape
    tm = _tile(s, ROWS, 16)
    ng = len(gains)

    def kern(o_ref, x_ref, *refs):
        xn = x_ref[...] + o_ref[...]
        refs[ng][...] = xn
        for t in range(ng):
            refs[ng + 1 + t][...] = _rms(xn, refs[t][...]).astype(BF16)

    return pl.pallas_call(
        kern, name=name, grid=(s // tm,),
        in_specs=[_row_spec(tm, d), _row_spec(tm, d)] + [_vec_spec(d)] * ng,
        out_specs=[_row_spec(tm, d)] * (1 + ng),
        out_shape=[jax.ShapeDtypeStruct((s, d), F32)] + [jax.ShapeDtypeStruct((s, d), BF16)] * ng,
        compiler_params=_params(("parallel",)),
    )(o, x, *gains)


def _loss_head(o, x, g, target, name):
    s, d = x.shape
    tm = _tile(s, ROWS, 16)

    def kern(o_ref, x_ref, g_ref, t_ref, dx_ref, dxb_ref, loss_ref, dg_ref):
        i = pl.program_id(0)
        x4 = x_ref[...] + o_ref[...]
        gg = g_ref[...]
        diff = _rms(x4, gg) - t_ref[...]
        dx, dgr = _rms_bwd(x4, gg, diff * (1.0 / d))
        dx_ref[...] = dx
        dxb_ref[...] = dx.astype(BF16)
        _acc_rows(loss_ref, diff * diff * (0.5 / d), i)
        _acc_rows(dg_ref, dgr, i)

    return pl.pallas_call(
        kern, name=name, grid=(s // tm,),
        in_specs=[_row_spec(tm, d), _row_spec(tm, d), _vec_spec(d), _row_spec(tm, d)],
        out_specs=[_row_spec(tm, d), _row_spec(tm, d), _vec_spec(d), _vec_spec(d)],
        out_shape=[jax.ShapeDtypeStruct((s, d), F32), jax.ShapeDtypeStruct((s, d), BF16),
                   jax.ShapeDtypeStruct((1, d), F32), jax.ShapeDtypeStruct((1, d), F32)],
        compiler_params=_params(("arbitrary",)),
    )(o, x, g, target)


def _norm_bwd(x, gains, dhs, dres, name):
    s, d = x.shape
    tm = _tile(s, ROWS, 16)
    ng = len(gains)

    def kern(x_ref, dres_ref, *refs):
        i = pl.program_id(0)
        x_ = x_ref[...]
        acc = dres_ref[...]
        for t in range(ng):
            dx, dgr = _rms_bwd(x_, refs[t][...], refs[ng + t][...])
            acc = acc + dx
            _acc_rows(refs[2 * ng + 2 + t], dgr, i)
        refs[2 * ng][...] = acc
        refs[2 * ng + 1][...] = acc.astype(BF16)

    return pl.pallas_call(
        kern, name=name, grid=(s // tm,),
        in_specs=[_row_spec(tm, d), _row_spec(tm, d)] + [_vec_spec(d)] * ng + [_row_spec(tm, d)] * ng,
        out_specs=[_row_spec(tm, d), _row_spec(tm, d)] + [_vec_spec(d)] * ng,
        out_shape=[jax.ShapeDtypeStruct((s, d), F32), jax.ShapeDtypeStruct((s, d), BF16)]
        + [jax.ShapeDtypeStruct((1, d), F32)] * ng,
        compiler_params=_params(("arbitrary",)),
    )(x, dres, *gains, *dhs)


def _glu_bwd(dmix, vg, name):
    s, d = dmix.shape
    tm = _tile(s, ROWS, 16)

    def kern(dm_ref, val_ref, gate_ref, o_ref):
        dm = dm_ref[...]
        sg = _sigmoid(gate_ref[...])
        o_ref[:, :d] = (dm * sg).astype(BF16)
        o_ref[:, d:] = (dm * val_ref[...] * sg * (1.0 - sg)).astype(BF16)

    return pl.pallas_call(
        kern, name=name, grid=(s // tm,),
        in_specs=[_row_spec(tm, d), _row_spec(tm, d, 0), _row_spec(tm, d, 1)],
        out_specs=_row_spec(tm, 2 * d),
        out_shape=jax.ShapeDtypeStruct((s, 2 * d), BF16),
        compiler_params=_params(("parallel",)),
    )(dmix, vg, vg)


SUB = 8


def _local_scan(vr, vi, pre_ref, pim_ref, reverse):
    sub = lax.broadcasted_iota(jnp.int32, vr.shape, 1)
    sign = -1.0 if reverse else 1.0
    for sh in (1, 2, 4):
        ar = pre_ref[sh - 1:sh, :][None]
        ai = sign * pim_ref[sh - 1:sh, :][None]
        keep = sub < SUB - sh if reverse else sub >= sh
        sr = jnp.where(keep, pltpu.roll(vr, SUB - sh if reverse else sh, 1), 0.0)
        si = jnp.where(keep, pltpu.roll(vi, SUB - sh if reverse else sh, 1), 0.0)
        vr, vi = vr + ar * sr - ai * si, vi + ar * si + ai * sr
    return vr, vi


def _s5_fwd(u, bbd_re, bbd_im, cbd_re, cbd_im, pw_re, pw_im, dskip, name):
    s, d = u.shape
    nt, cw, lw = bbd_re.shape
    t = _tile(s, SCAN_T, SUB)
    nc = s // t
    ng = t // SUB

    def kern(u_ref, bre_ref, bim_ref, cre_ref, cim_ref, pre_ref, pim_ref, d_ref,
             xr_ref, xi_ref, y_ref, z_ref, car_ref, cai_ref):
        c = pl.program_id(1)

        @pl.when(c == 0)
        def _():
            car_ref[...] = jnp.zeros_like(car_ref)
            cai_ref[...] = jnp.zeros_like(cai_ref)

        u_ = u_ref[...]
        ub = u_.astype(BF16)
        vr, vi = _local_scan(_dot(ub, bre_ref[...]).reshape(ng, SUB, lw),
                             _dot(ub, bim_ref[...]).reshape(ng, SUB, lw), pre_ref, pim_ref, False)
        cr = car_ref[...]
        ci = cai_ref[...]
        pr = pre_ref[...]
        pi = pim_ref[...]
        for gidx in range(ng):
            rows = slice(gidx * SUB, (gidx + 1) * SUB)
            gr = vr[gidx] + pr * cr - pi * ci
            gi = vi[gidx] + pr * ci + pi * cr
            xr_ref[rows, :] = gr
            xi_ref[rows, :] = gi
            cr, ci = gr[SUB - 1:SUB, :], gi[SUB - 1:SUB, :]
        car_ref[...] = cr
        cai_ref[...] = ci
        y = (_dot(xr_ref[...].astype(BF16), cre_ref[...]) - _dot(xi_ref[...].astype(BF16), cim_ref[...])
             + d_ref[...] * u_)
        y_ref[...] = y
        z_ref[...] = _gelu(y).astype(BF16)

    tok = pl.BlockSpec((t, cw), lambda j, c: (c, j))
    st = pl.BlockSpec((t, lw), lambda j, c: (c, j))
    return pl.pallas_call(
        kern, name=name, grid=(nt, nc),
        in_specs=[tok,
                  pl.BlockSpec((None, cw, lw), lambda j, c: (j, 0, 0)),
                  pl.BlockSpec((None, cw, lw), lambda j, c: (j, 0, 0)),
                  pl.BlockSpec((None, lw, cw), lambda j, c: (j, 0, 0)),
                  pl.BlockSpec((None, lw, cw), lambda j, c: (j, 0, 0)),
                  pl.BlockSpec((SUB, lw), lambda j, c: (0, j)),
                  pl.BlockSpec((SUB, lw), lambda j, c: (0, j)),
                  pl.BlockSpec((1, cw), lambda j, c: (0, j))],
        out_specs=[st, st, tok, tok],
        out_shape=[jax.ShapeDtypeStruct((s, nt * lw), F32), jax.ShapeDtypeStruct((s, nt * lw), F32),
                   jax.ShapeDtypeStruct((s, d), F32), jax.ShapeDtypeStruct((s, d), BF16)],
        scratch_shapes=[pltpu.VMEM((1, lw), F32), pltpu.VMEM((1, lw), F32)],
        compiler_params=_params(("parallel", "arbitrary")),
    )(u, bbd_re, bbd_im, cbd_re, cbd_im, pw_re, pw_im, dskip)


def _s5_bwd(dz, y, u, xs_re, xs_im, bbd_re, bbd_im, cbd_re, cbd_im, pw_re, pw_im, pf_re, pf_im, dskip, name):
    s, d = u.shape
    nt, cw, lw = bbd_re.shape
    t = _tile(s, SCAN_T, SUB)
    nc = s // t
    ng = t // SUB

    def kern(dz_ref, y_ref, u_ref, xr_ref, xi_ref, bre_ref, bim_ref, cre_ref, cim_ref,
             pre_ref, pim_ref, fre_ref, fim_ref, d_ref,
             du_ref, dd_ref, dcr_ref, dci_ref, dbr_ref, dbi_ref, dar_ref, dai_ref,
             car_ref, cai_ref, gr_ref, gi_ref):
        c = pl.program_id(1)

        @pl.when(c == 0)
        def _():
            car_ref[...] = jnp.zeros_like(car_ref)
            cai_ref[...] = jnp.zeros_like(cai_ref)

        u_ = u_ref[...]
        ub = u_.astype(BF16)
        dy = dz_ref[...] * _gelu_grad(y_ref[...])
        dyb = dy.astype(BF16)
        vr, vi = _local_scan(_dot_nt(dyb, cre_ref[...]).reshape(ng, SUB, lw),
                             (-_dot_nt(dyb, cim_ref[...])).reshape(ng, SUB, lw), pre_ref, pim_ref, True)
        later_r = car_ref[...]
        later_i = cai_ref[...]
        cr, ci = later_r, later_i
        fr = fre_ref[...]
        fi = fim_ref[...]
        for gidx in reversed(range(ng)):
            rows = slice(gidx * SUB, (gidx + 1) * SUB)
            ar = vr[gidx] + fr * cr + fi * ci
            ai = vi[gidx] + fr * ci - fi * cr
            gr_ref[rows, :] = ar
            gi_ref[rows, :] = ai
            cr, ci = ar[0:1, :], ai[0:1, :]
        car_ref[...] = cr
        cai_ref[...] = ci
        gr = gr_ref[...]
        gi = gi_ref[...]
        rows = lax.broadcasted_iota(jnp.int32, (t, lw), 0)
        gsr = jnp.where(rows < t - 1, pltpu.roll(gr, t - 1, 0), later_r)
        gsi = jnp.where(rows < t - 1, pltpu.roll(gi, t - 1, 0), later_i)
        xr = xr_ref[...]
        xi = xi_ref[...]
        dar = jnp.sum(gsr * xr + gsi * xi, axis=0, keepdims=True)
        dai = jnp.sum(gsi * xr - gsr * xi, axis=0, keepdims=True)
        grb = gr.astype(BF16)
        gib = gi.astype(BF16)
        dbr = _dot_tn(ub, grb)
        dbi = _dot_tn(ub, gib)
        dcr = _dot_tn(dyb, xr.astype(BF16))
        dci = _dot_tn(dyb, xi.astype(BF16))
        du_ref[...] = dy * d_ref[...] + _dot_nt(grb, bre_ref[...]) + _dot_nt(gib, bim_ref[...])
        ddv = jnp.sum(dy * u_, axis=0, keepdims=True)

        @pl.when(c == 0)
        def _():
            dd_ref[...] = ddv
            dcr_ref[...] = dcr
            dci_ref[...] = dci
            dbr_ref[...] = dbr
            dbi_ref[...] = dbi
            dar_ref[...] = dar
            dai_ref[...] = dai

        @pl.when(c > 0)
        def _():
            dd_ref[...] += ddv
            dcr_ref[...] += dcr
            dci_ref[...] += dci
            dbr_ref[...] += dbr
            dbi_ref[...] += dbi
            dar_ref[...] += dar
            dai_ref[...] += dai

    tok = pl.BlockSpec((t, cw), lambda j, c: (nc - 1 - c, j))
    st = pl.BlockSpec((t, lw), lambda j, c: (nc - 1 - c, j))
    wb = pl.BlockSpec((None, cw, lw), lambda j, c: (j, 0, 0))
    wc = pl.BlockSpec((None, lw, cw), lambda j, c: (j, 0, 0))
    pw = pl.BlockSpec((SUB, lw), lambda j, c: (0, j))
    vec_c = pl.BlockSpec((1, cw), lambda j, c: (0, j))
    vec_l = pl.BlockSpec((1, lw), lambda j, c: (0, j))
    return pl.pallas_call(
        kern, name=name, grid=(nt, nc),
        in_specs=[tok, tok, tok, st, st, wb, wb, wc, wc, pw, pw, pw, pw, vec_c],
        out_specs=[tok, vec_c, wb, wb, wb, wb, vec_l, vec_l],
        out_shape=[jax.ShapeDtypeStruct((s, d), F32), jax.ShapeDtypeStruct((1, d), F32)]
        + [jax.ShapeDtypeStruct((nt, cw, lw), F32)] * 4
        + [jax.ShapeDtypeStruct((1, nt * lw), F32)] * 2,
        scratch_shapes=[pltpu.VMEM((1, lw), F32), pltpu.VMEM((1, lw), F32),
                        pltpu.VMEM((t, lw), F32), pltpu.VMEM((t, lw), F32)],
        compiler_params=_params(("parallel", "arbitrary")),
    )(dz, y, u, xs_re, xs_im, bbd_re, bbd_im, cbd_re, cbd_im, pw_re, pw_im, pf_re, pf_im, dskip)


def _s5_discretize(lam_re, lam_im, log_dt, b_re, b_im):
    dt = jnp.exp(log_dt)[:, None]
    mag = jnp.exp(lam_re * dt)
    ang = lam_im * dt
    lb_re = mag * jnp.cos(ang)
    lb_im = mag * jnp.sin(ang)
    nr = lb_re - 1.0
    den = lam_re * lam_re + lam_im * lam_im
    f_re = (nr * lam_re + lb_im * lam_im) / den
    f_im = (lb_im * lam_re - nr * lam_im) / den
    bb_re = f_re[..., None] * b_re - f_im[..., None] * b_im
    bb_im = f_re[..., None] * b_im + f_im[..., None] * b_re
    return lb_re, lb_im, bb_re, bb_im


def _block_diag(w):
    nt, ng, a, b = w.shape
    eye = jnp.eye(ng, dtype=w.dtype)
    return (w[:, :, :, None, :] * eye[None, :, None, :, None]).reshape(nt, ng * a, ng * b)


def _block_diag_take(w, a, b):
    nt = w.shape[0]
    ng = GROUPS_PER_TILE
    w5 = w.reshape(nt, ng, a, ng, b)
    idx = jnp.arange(ng)
    return w5[:, idx, :, idx, :].transpose(1, 0, 2, 3).reshape(nt * ng, a, b)


def _att_combine(outs, lses, name):
    s, d = outs[0].shape
    tm = _tile(s, ROWS, 16)
    ng = len(outs)

    def kern(*refs):
        ls = [refs[ng + t][...] for t in range(ng)]
        mx = functools.reduce(jnp.maximum, ls)
        es = [jnp.exp(l - mx) for l in ls]
        den = functools.reduce(lambda a, b: a + b, es)
        o = functools.reduce(lambda a, b: a + b, [es[t] / den * refs[t][...] for t in range(ng)])
        refs[2 * ng][...] = o
        refs[2 * ng + 1][...] = o.astype(BF16)

    return pl.pallas_call(
        kern, name=name, grid=(s // tm,),
        in_specs=[_row_spec(tm, d)] * (2 * ng),
        out_specs=[_row_spec(tm, d)] * 2,
        out_shape=[jax.ShapeDtypeStruct((s, d), F32), jax.ShapeDtypeStruct((s, d), BF16)],
        compiler_params=_params(("parallel",)),
    )(*outs, *lses)


DIL_MAX = 16


def _slab(r):
    return 4 * (r % 4) + r // 4


def _to_slabs(xs, name):
    s, w = xs[0].shape
    n = s // DIL_MAX
    nx = len(xs)

    def kern(*refs):
        o_ref = refs[nx]
        for r in range(DIL_MAX):
            rows = [x_ref[pl.ds(r, n, stride=DIL_MAX), :] for x_ref in refs[:nx]]
            o_ref[_slab(r) * n:(_slab(r) + 1) * n, :] = functools.reduce(lambda a, b: a + b, rows)

    spec = pl.BlockSpec((s, 128), lambda i: (0, i))
    return pl.pallas_call(
        kern, name=name, grid=(w // 128,), in_specs=[spec] * nx, out_specs=spec,
        out_shape=jax.ShapeDtypeStruct((s, w), F32), compiler_params=_params(("parallel",)),
    )(*xs)


def _from_slabs(x, name):
    s, w = x.shape
    n = s // DIL_MAX

    def kern(x_ref, o_ref, ob_ref):
        for r in range(DIL_MAX):
            o_ref[pl.ds(r, n, stride=DIL_MAX), :] = x_ref[_slab(r) * n:(_slab(r) + 1) * n, :]
        ob_ref[...] = o_ref[...].astype(BF16)

    spec = pl.BlockSpec((s, 128), lambda i: (0, i))
    return pl.pallas_call(
        kern, name=name, grid=(w // 128,), in_specs=[spec], out_specs=[spec, spec],
        out_shape=[jax.ShapeDtypeStruct((s, w), F32), jax.ShapeDtypeStruct((s, w), BF16)],
        compiler_params=_params(("parallel",)),
    )(x)


def _norm_bf16(x, gains, name):
    s, d = x.shape
    tm = _tile(s, ROWS, 16)
    ng = len(gains)

    def kern(x_ref, *refs):
        x_ = x_ref[...]
        for t in range(ng):
            refs[ng + t][...] = _rms(x_, refs[t][...]).astype(BF16)

    return pl.pallas_call(
        kern, name=name, grid=(s // tm,),
        in_specs=[_row_spec(tm, d)] + [_vec_spec(d)] * ng, out_specs=[_row_spec(tm, d)] * ng,
        out_shape=[jax.ShapeDtypeStruct((s, d), BF16)] * ng,
        compiler_params=_params(("parallel",)),
    )(x, *gains)


def _att_geometry(dil, s):
    g = DIL_MAX // dil
    return s // DIL_MAX, g, max(ATT_BLK // g, 16)


def _att_mask(g, j, jb, rep):
    b = g * j

    def pos(i):
        sl, jj = i // j, i % j
        off = {1: 0, 4: sl, 16: sl // 4 + 4 * (sl % 4)}[g]
        return g * jj + off

    qi = lax.broadcasted_iota(jnp.int32, (rep * b, 2 * b), 0) % b
    ki = lax.broadcasted_iota(jnp.int32, (rep * b, 2 * b), 1)
    prev = ki < b
    dist = pos(qi) - pos(ki % b) + jnp.where(prev, b, 0)
    return (dist >= 0) & (dist <= ATT_BLK) & (jnp.logical_not(prev) | (jb > 0))


def _stack_heads(x, rep):
    return jnp.concatenate([x[:, h * HEAD_DIM:(h + 1) * HEAD_DIM] for h in range(rep)], axis=0)


def _att_fwd(q, kv, gi, dil, nh, name):
    s = q.shape[0]
    n, g, j = _att_geometry(dil, s)
    b = g * j
    ng = q.shape[1] // (nh * HEAD_DIM)
    rep = nh // N_KV_HEADS
    qw = rep * HEAD_DIM
    scale = HEAD_DIM ** -0.5
    q3 = q.reshape(DIL_MAX, n, q.shape[1])
    kv3 = kv.reshape(DIL_MAX, n, kv.shape[1])

    def kern(q_ref, kc_ref, kp_ref, vc_ref, vp_ref, o_ref, l_ref):
        jb = pl.program_id(2)
        kc, kp, vc, vp = [r[...].reshape(b, HEAD_DIM) for r in (kc_ref, kp_ref, vc_ref, vp_ref)]
        k2 = jnp.concatenate([kp, kc], axis=0)
        v2 = jnp.concatenate([vp, vc], axis=0)
        qs = _stack_heads(q_ref[...].reshape(b, qw), rep)
        sc = jnp.where(_att_mask(g, j, jb, rep), _dot_nt(qs, k2) * scale, NEG_INF)
        m = jnp.max(sc, axis=-1, keepdims=True)
        p = jnp.exp(sc - m)
        l = jnp.sum(p, axis=-1, keepdims=True)
        out = _dot((p / l).astype(BF16), v2)
        lse = jnp.broadcast_to(m + jnp.log(l), (rep * b, HEAD_DIM))
        for h in range(rep):
            cols = slice(h * HEAD_DIM, (h + 1) * HEAD_DIM)
            o_ref[:, :, cols] = out[h * b:(h + 1) * b].reshape(g, j, HEAD_DIM)
            l_ref[:, :, cols] = lse[h * b:(h + 1) * b].reshape(g, j, HEAD_DIM)

    def kv_spec(off, prev):
        if prev:
            return pl.BlockSpec((g, j, HEAD_DIM), lambda c, h, jb: (c, jnp.maximum(jb - 1, 0), off + h))
        return pl.BlockSpec((g, j, HEAD_DIM), lambda c, h, jb: (c, jb, off + h))

    out_spec = pl.BlockSpec((g, j, qw), lambda c, h, jb: (c, jb, h))
    out, lse = pl.pallas_call(
        kern, name=name, grid=(dil, N_KV_HEADS, n // j),
        in_specs=[pl.BlockSpec((g, j, qw), lambda c, h, jb: (c, jb, gi * N_KV_HEADS + h)),
                  kv_spec(0, False), kv_spec(0, True), kv_spec(N_KV_HEADS, False), kv_spec(N_KV_HEADS, True)],
        out_specs=[out_spec, out_spec],
        out_shape=[jax.ShapeDtypeStruct((DIL_MAX, n, nh * HEAD_DIM), F32)] * 2,
        compiler_params=_params(("parallel", "parallel", "arbitrary")),
    )(q3, kv3, kv3, kv3, kv3)
    del ng
    return out.reshape(s, nh * HEAD_DIM), lse.reshape(s, nh * HEAD_DIM)


def _att_bwd(q, kv, do, o, lses, gi, dil, nh, name):
    s = q.shape[0]
    n, g, j = _att_geometry(dil, s)
    b = g * j
    nb = n // j
    ng = len(lses)
    rep = nh // N_KV_HEADS
    qw = rep * HEAD_DIM
    dm = nh * HEAD_DIM
    scale = HEAD_DIM ** -0.5
    q3 = q.reshape(DIL_MAX, n, q.shape[1])
    kv3 = kv.reshape(DIL_MAX, n, kv.shape[1])
    wide = [a.reshape(DIL_MAX, n, dm) for a in (do, o, *lses)]

    def kern(q_ref, kc_ref, kp_ref, vc_ref, vp_ref, do_ref, o_ref, *refs):
        l_refs = refs[:ng]
        dq_ref, dk_ref, dv_ref, ck_ref, cv_ref = refs[ng:]
        jb = pl.program_id(2)

        @pl.when(jb == 0)
        def _():
            ck_ref[...] = jnp.zeros_like(ck_ref)
            cv_ref[...] = jnp.zeros_like(cv_ref)

        @pl.when(jb < nb)
        def _():
            kc, kp, vc, vp = [r[...].reshape(b, HEAD_DIM) for r in (kc_ref, kp_ref, vc_ref, vp_ref)]
            k2 = jnp.concatenate([kp, kc], axis=0)
            v2 = jnp.concatenate([vp, vc], axis=0)
            qs = _stack_heads(q_ref[...].reshape(b, qw), rep)
            ls = [_stack_heads(r[...].reshape(b, qw), rep) for r in l_refs]
            mx = functools.reduce(jnp.maximum, ls)
            den = functools.reduce(lambda a, c: a + c, [jnp.exp(l - mx) for l in ls])
            lse_g = ls[gi]
            w = jnp.exp(lse_g - mx) / den
            do_ = _stack_heads(do_ref[...].reshape(b, qw), rep)
            ct = w[:, 0:1] * jnp.sum(do_ * _stack_heads(o_ref[...].reshape(b, qw), rep), axis=-1, keepdims=True)
            dob = (w * do_).astype(BF16)
            p = jnp.exp(jnp.where(_att_mask(g, j, jb, rep), _dot_nt(qs, k2) * scale, NEG_INF) - lse_g[:, 0:1])
            ds = (p * (_dot_nt(dob, v2) - ct) * scale).astype(BF16)
            dq = (_dot(ds, k2)).astype(BF16)
            for h in range(rep):
                dq_ref[:, :, h * HEAD_DIM:(h + 1) * HEAD_DIM] = dq[h * b:(h + 1) * b].reshape(g, j, HEAD_DIM)
            dk2 = _dot_tn(ds, qs)
            dv2 = _dot_tn(p.astype(BF16), dob)
            dk_ref[...] = (ck_ref[...] + dk2[:b]).reshape(g, j, HEAD_DIM)
            dv_ref[...] = (cv_ref[...] + dv2[:b]).reshape(g, j, HEAD_DIM)
            ck_ref[...] = dk2[b:]
            cv_ref[...] = dv2[b:]

        @pl.when(jb == nb)
        def _():
            dk_ref[...] = ck_ref[...].reshape(g, j, HEAD_DIM)
            dv_ref[...] = cv_ref[...].reshape(g, j, HEAD_DIM)

    def jq(jb):
        return jnp.minimum(jb, nb - 1)

    def kv_spec(off, prev):
        if prev:
            return pl.BlockSpec((g, j, HEAD_DIM), lambda c, h, jb: (c, jnp.maximum(jq(jb) - 1, 0), off + h))
        return pl.BlockSpec((g, j, HEAD_DIM), lambda c, h, jb: (c, jq(jb), off + h))

    wide_spec = pl.BlockSpec((g, j, qw), lambda c, h, jb: (c, jq(jb), h))
    dkv_spec = pl.BlockSpec((g, j, HEAD_DIM), lambda c, h, jb: (c, jnp.maximum(jb - 1, 0), h))
    dq, dk, dv = pl.pallas_call(
        kern, name=name, grid=(dil, N_KV_HEADS, nb + 1),
        in_specs=[pl.BlockSpec((g, j, qw), lambda c, h, jb: (c, jq(jb), gi * N_KV_HEADS + h)),
                  kv_spec(0, False), kv_spec(0, True), kv_spec(N_KV_HEADS, False), kv_spec(N_KV_HEADS, True)]
        + [wide_spec] * (2 + ng),
        out_specs=[wide_spec, dkv_spec, dkv_spec],
        out_shape=[jax.ShapeDtypeStruct((DIL_MAX, n, dm), BF16),
                   jax.ShapeDtypeStruct((DIL_MAX, n, N_KV_HEADS * HEAD_DIM), F32),
                   jax.ShapeDtypeStruct((DIL_MAX, n, N_KV_HEADS * HEAD_DIM), F32)],
        scratch_shapes=[pltpu.VMEM((b, HEAD_DIM), F32), pltpu.VMEM((b, HEAD_DIM), F32)],
        compiler_params=_params(("parallel", "parallel", "arbitrary")),
    )(q3, kv3, kv3, kv3, kv3, *wide)
    return dq.reshape(s, dm), dk.reshape(s, -1), dv.reshape(s, -1)


def _dkv_sum(dks, dvs, name):
    s, w = dks[0].shape
    tm = _tile(s, ROWS, 16)
    ng = len(dks)

    def kern(*refs):
        o_ref = refs[2 * ng]
        o_ref[:, :w] = functools.reduce(lambda a, b: a + b, [refs[t][...] for t in range(ng)]).astype(BF16)
        o_ref[:, w:] = functools.reduce(lambda a, b: a + b, [refs[ng + t][...] for t in range(ng)]).astype(BF16)

    return pl.pallas_call(
        kern, name=name, grid=(s // tm,),
        in_specs=[_row_spec(tm, w)] * (2 * ng),
        out_specs=_row_spec(tm, 2 * w),
        out_shape=jax.ShapeDtypeStruct((s, 2 * w), BF16),
        compiler_params=_params(("parallel",)),
    )(*dks, *dvs)


def _ffn_fwd(hf, weight, layer, tag):
    a, gu = _ffn_in_act(hf, weight(f"w_in{layer}", hf), f"{tag}_in")
    o = _mm_nn(a, weight(f"w_out{layer}", a), F32, f"{tag}_out", tn=1024, tk=2816)
    return gu, a, o


def _ffn_bwd(dresb, hf, gu, a, weight, emit, layer, tag):
    w_in, w_out = weight(f"w_in{layer}", None), weight(f"w_out{layer}", None)
    zero = emit(f"w_out{layer}", _mm_tn(a, dresb, 1, f"{tag}_out_dw", tkk=1408, tn=1024))
    dgu = _ffn_out_dx_act(dresb, w_out, gu, f"{tag}_out_dx")
    zero = zero + emit(f"w_in{layer}",
                       _mm_tn(hf, dgu, w_in.shape[0], f"{tag}_in_dw", tn=FFN_TILE, paired=True))
    return _mm_nt(dgu, w_in, f"{tag}_in_dx", tr=FFN_TILE, paired=True), zero


def _local_step(x, target, small, weight, emit):
    s, d = x.shape
    nh = d // HEAD_DIM
    n_groups = d // S5_GROUP_CH
    nt = n_groups // GROUPS_PER_TILE
    p_, c_ = S5_STATE, S5_GROUP_CH

    disc_in = (small["lam_re"], small["lam_im"], small["log_dt"], small["b_re"], small["b_im"])
    (lb_re, lb_im, bb_re, bb_im), disc_vjp = jax.vjp(_s5_discretize, *disc_in)
    del lb_re, lb_im
    dt = jnp.exp(small["log_dt"])[:, None]
    def pole_powers(exponents):
        k = exponents[:, None, None]
        mag = jnp.exp(k * (small["lam_re"] * dt)[None])
        ang = k * (small["lam_im"] * dt)[None]
        return ((mag * jnp.cos(ang)).reshape(SUB, n_groups * p_),
                (mag * jnp.sin(ang)).reshape(SUB, n_groups * p_))

    pw_re, pw_im = pole_powers(jnp.arange(1, SUB + 1, dtype=F32))
    pf_re, pf_im = pole_powers(jnp.arange(SUB, 0, -1, dtype=F32))
    bbd_re = _block_diag(bb_re.transpose(0, 2, 1).reshape(nt, GROUPS_PER_TILE, c_, p_)).astype(BF16)
    bbd_im = _block_diag(bb_im.transpose(0, 2, 1).reshape(nt, GROUPS_PER_TILE, c_, p_)).astype(BF16)
    cbd_re = _block_diag(small["c_re"].transpose(0, 2, 1).reshape(nt, GROUPS_PER_TILE, p_, c_)).astype(BF16)
    cbd_im = _block_diag(small["c_im"].transpose(0, 2, 1).reshape(nt, GROUPS_PER_TILE, p_, c_)).astype(BF16)

    h0 = _norm_f32(x, small["a_norm"], "s5_norm")
    xs_re, xs_im, y0, z = _s5_fwd(h0, bbd_re, bbd_im, cbd_re, cbd_im, pw_re, pw_im, small["s5_d"], "s5_fwd")
    vg = _mm_nn(z, weight("w_glu", z), F32, "glu_mm", tn=1024)
    x1, hf0 = _glu_res_norm(vg, x, small["ffn_norm0"], "glu_res_norm")
    gu0, a0, o0 = _ffn_fwd(hf0, weight, 0, "ffn0")
    x2 = _to_slabs([o0, x1], "to_slabs")
    kvn, h1 = _norm_bf16(x2, [small["kv_norm"], small["b_norm"]], "att_norms")
    kv = _mm_nn(kvn, weight("w_kv", kvn), BF16, "kv_mm", tn=1024)
    q = _mm_nn(h1, weight("w_q", kv), BF16, "q_mm", tn=1536)
    outs, lses = [], []
    for gi, (window, dil) in enumerate(PATTERNS):
        assert window // dil == ATT_BLK
        og, lg = _att_fwd(q, kv, gi, dil, nh, f"att_fwd{gi}")
        outs.append(og)
        lses.append(lg)
    oatt, oattb = _att_combine(outs, lses, "att_combine")
    ao = _mm_nn(oattb, weight("w_o", oattb), F32, "o_mm", tn=1024)
    x3, hf1 = _res_norm(ao, x2, [small["ffn_norm1"]], "att_res_norm")
    gu1, a1, o1 = _ffn_fwd(hf1, weight, 1, "ffn1")
    dres, dresb, loss_rows, d_final = _loss_head(
        o1, x3, small["final_norm"], _to_slabs([target], "target_to_slabs"), "loss_head")

    dhf1, zero = _ffn_bwd(dresb, hf1, gu1, a1, weight, emit, 1, "ffn1")
    dres, dresb, d_ffn_norm1 = _norm_bwd(x3, [small["ffn_norm1"] + zero], [dhf1], dres, "ffn1_norm_bwd")
    zero = emit("w_o", _mm_tn(oattb, dresb, 1, "o_dw", tn=1024))
    doatt = _mm_nt(dresb, weight("w_o", None), "o_dx", tr=2048)
    dqs, dks, dvs = [], [], []
    for gi, (window, dil) in enumerate(PATTERNS):
        dq_g, dk_g, dv_g = _att_bwd(q, kv, doatt, oatt, lses, gi, dil, nh, f"att_bwd{gi}")
        dqs.append(dq_g)
        dks.append(dk_g)
        dvs.append(dv_g)
    dq = jnp.concatenate(dqs, axis=1)
    dkv = _dkv_sum(dks, dvs, "dkv_sum")
    w_q = weight("w_q", None)
    zero = zero + emit("w_q", _mm_tn(h1, dq, w_q.shape[0], "q_dw", tn=1536))
    zero = zero + emit("w_kv", _mm_tn(kvn, dkv, 1, "kv_dw", tn=1024))
    dh1 = _mm_nt(dq, w_q, "q_dx", tr=1536)
    dkvn = _mm_nt(dkv, weight("w_kv", None), "kv_dx", tr=1024)
    dres, dresb, d_b_norm, d_kv_norm = _norm_bwd(
        x2, [small["b_norm"] + zero, small["kv_norm"]], [dh1, dkvn], dres, "att_norm_bwd")
    dres, dresb = _from_slabs(dres, "from_slabs")
    dhf0, zero = _ffn_bwd(dresb, hf0, gu0, a0, weight, emit, 0, "ffn0")
    dres, dresb, d_ffn_norm0 = _norm_bwd(x1, [small["ffn_norm0"] + zero], [dhf0], dres, "ffn0_norm_bwd")
    del dresb
    dvg = _glu_bwd(dres, vg, "glu_bwd")
    w_glu = weight("w_glu", None)
    zero = emit("w_glu", _mm_tn(z, dvg, w_glu.shape[0], "glu_dw", tn=1024))
    dz = _mm_nt(dvg, w_glu, "glu_dx", tr=1024)
    dh0, d_s5_d, dcr, dci_neg, dbr, dbi, dar, dai = _s5_bwd(
        dz, y0, h0, xs_re, xs_im, bbd_re, bbd_im, cbd_re, cbd_im, pw_re, pw_im, pf_re, pf_im,
        small["s5_d"] + zero, "s5_bwd")
    grad_x, _, d_a_norm = _norm_bwd(x, [small["a_norm"]], [dh0], dres, "s5_norm_bwd")

    d_bb_re = _block_diag_take(dbr, c_, p_).transpose(0, 2, 1)
    d_bb_im = _block_diag_take(dbi, c_, p_).transpose(0, 2, 1)
    d_c_re = _block_diag_take(dcr, c_, p_)
    d_c_im = -_block_diag_take(dci_neg, c_, p_)
    d_lam_re, d_lam_im, d_log_dt, d_b_re, d_b_im = disc_vjp(
        (dar.reshape(n_groups, p_), dai.reshape(n_groups, p_), d_bb_re, d_bb_im))

    d_small = dict(lam_re=d_lam_re, lam_im=d_lam_im, log_dt=d_log_dt, b_re=d_b_re, b_im=d_b_im,
                   c_re=d_c_re, c_im=d_c_im, s5_d=d_s5_d, a_norm=d_a_norm, ffn_norm0=d_ffn_norm0,
                   ffn_norm1=d_ffn_norm1, b_norm=d_b_norm, kv_norm=d_kv_norm, final_norm=d_final)
    return loss_rows, grad_x, d_small


def _place():
    x, y, c = lax.axis_index("x"), lax.axis_index("y"), lax.axis_index("c")
    return x, y, c, [(1 - x, y), (x, 1 - y), (1 - x, 1 - y)]


_ANY = pl.BlockSpec(memory_space=pl.ANY)


_HBM = pl.BlockSpec(memory_space=pltpu.HBM)
_SEM = pl.BlockSpec(memory_space=pltpu.SEMAPHORE)
_EFFECT = pltpu.SideEffectType.DATAFLOW_SIDE_EFFECTING


def _in_hbm(a):
    return pltpu.with_memory_space_constraint(a, pltpu.HBM)


def _cast_place(shards, layer, chip, name):
    _, r, c = shards.shape
    tm = _tile(r, ROWS, 16)

    def kern(chip_ref, x_ref, o_ref):
        del chip_ref
        o_ref[...] = x_ref[...].astype(BF16)

    return pl.pallas_call(
        kern, name=name,
        grid_spec=pltpu.PrefetchScalarGridSpec(
            num_scalar_prefetch=1, grid=(r // tm,),
            in_specs=[pl.BlockSpec((None, tm, c), lambda i, ch: (layer, i, 0))],
            out_specs=pl.BlockSpec((None, tm, c), lambda i, ch: (ch[0], i, 0))),
        out_shape=jax.ShapeDtypeStruct((N_CHIPS, r, c), BF16),
        compiler_params=_params(("parallel",)),
    )(chip, shards)


def _gather_start(lands):
    n = len(lands)

    def body(*refs):
        land = refs[:n]
        send, recv = refs[n:2 * n], refs[2 * n:3 * n]
        x, y, c, peers = _place()
        me = 2 * x + y
        for a in range(n):
            for k, (px, py) in enumerate(peers):
                pltpu.make_async_remote_copy(
                    src_ref=land[a].at[me], dst_ref=land[a].at[me], send_sem=send[a].at[k], recv_sem=recv[a].at[k],
                    device_id=(px, py, c), device_id_type=MESH).start()

    outs = pl.pallas_call(
        body, name="gather_start",
        out_shape=[pltpu.SemaphoreType.DMA((3,))] * (2 * n) + [pltpu.HBM(a.shape, a.dtype) for a in lands],
        in_specs=[_HBM] * n, out_specs=[_SEM] * (2 * n) + [_HBM] * n,
        input_output_aliases={i: 2 * n + i for i in range(n)},
        compiler_params=pltpu.CompilerParams(has_side_effects=_EFFECT),
    )(*[_in_hbm(a) for a in lands])
    return outs[:n], outs[n:2 * n], outs[2 * n:]


def _gather_wait(lands, sends, recvs, after, name):
    n = len(lands)

    def body(*refs):
        land, send, recv = refs[:n], refs[n:2 * n], refs[2 * n:3 * n]
        x, y, c, peers = _place()
        me = 2 * x + y
        for a in range(n):
            for k, (px, py) in enumerate(peers):
                cp = pltpu.make_async_remote_copy(
                    src_ref=land[a].at[me], dst_ref=land[a].at[2 * px + py], send_sem=send[a].at[k],
                    recv_sem=recv[a].at[k], device_id=(px, py, c), device_id_type=MESH)
                cp.wait_send()
                cp.wait_recv()

    return pl.pallas_call(
        body, name=name,
        out_shape=[pltpu.HBM(a.shape, a.dtype) for a in lands],
        in_specs=[_HBM] * n + [_SEM] * (2 * n) + [_ANY], out_specs=[_HBM] * n,
        input_output_aliases={i: i for i in range(n)},
        compiler_params=pltpu.CompilerParams(has_side_effects=_EFFECT),
    )(*lands, *sends, *recvs, after)


def _scatter_start(g, name):
    def body(g_ref, land_ref, send, recv, g_out, land_out, token):
        del g_out, land_out
        x, y, c, peers = _place()
        for k, (px, py) in enumerate(peers):
            pltpu.make_async_remote_copy(
                src_ref=g_ref.at[2 * px + py], dst_ref=land_ref.at[k], send_sem=send.at[k], recv_sem=recv.at[k],
                device_id=(px, py, c), device_id_type=MESH).start()
        token[...] = jnp.zeros_like(token)

    land = lax.empty((3,) + g.shape[1:], g.dtype)
    return pl.pallas_call(
        body, name=name,
        out_shape=(pltpu.SemaphoreType.DMA((3,)), pltpu.SemaphoreType.DMA((3,)),
                   pltpu.HBM(g.shape, g.dtype), pltpu.HBM(land.shape, land.dtype),
                   jax.ShapeDtypeStruct((8, 128), F32)),
        in_specs=(_HBM, _HBM), out_specs=(_SEM, _SEM, _HBM, _HBM, pl.BlockSpec(memory_space=pltpu.VMEM)),
        input_output_aliases={0: 2, 1: 3},
        compiler_params=pltpu.CompilerParams(has_side_effects=_EFFECT),
    )(_in_hbm(g), _in_hbm(land))


def _scatter_wait(started, after):
    n = len(started)

    def body(*refs):
        gs, lands = refs[:n], refs[n:2 * n]
        sends, recvs = refs[2 * n:3 * n], refs[3 * n:4 * n]
        x, y, c, peers = _place()
        for a in range(n):
            for k, (px, py) in enumerate(peers):
                cp = pltpu.make_async_remote_copy(
                    src_ref=gs[a].at[2 * px + py], dst_ref=lands[a].at[k], send_sem=sends[a].at[k],
                    recv_sem=recvs[a].at[k], device_id=(px, py, c), device_id_type=MESH)
                cp.wait_send()
                cp.wait_recv()

    gs = [s[2] for s in started]
    lands = [s[3] for s in started]
    outs = pl.pallas_call(
        body, name="scatter_wait",
        out_shape=[pltpu.HBM(a.shape, a.dtype) for a in gs + lands],
        in_specs=[_HBM] * (2 * n) + [_SEM] * (2 * n) + [_ANY], out_specs=[_HBM] * (2 * n),
        input_output_aliases={i: i for i in range(2 * n)},
        compiler_params=pltpu.CompilerParams(has_side_effects=_EFFECT),
    )(*gs, *lands, *[s[0] for s in started], *[s[1] for s in started], after)
    return outs[:n], outs[n:]


def _swap_cores(parts):
    n = len(parts)

    def body(*refs):
        ins, outs = refs[:n], refs[n:2 * n]
        send_sems, recv_sems = refs[2 * n:]
        x, y, c = lax.axis_index("x"), lax.axis_index("y"), lax.axis_index("c")
        copies = []
        for a in range(n):
            cp = pltpu.make_async_remote_copy(
                src_ref=ins[a], dst_ref=outs[a], send_sem=send_sems.at[a], recv_sem=recv_sems.at[a],
                device_id=(x, y, 1 - c), device_id_type=MESH)
            cp.start()
            copies.append(cp)
        for cp in copies:
            cp.wait()

    return pl.pallas_call(
        body, name="swap_cores",
        in_specs=[_ANY] * n, out_specs=[_ANY] * n,
        out_shape=[jax.ShapeDtypeStruct(a.shape, a.dtype) for a in parts],
        scratch_shapes=[pltpu.SemaphoreType.DMA((n,)), pltpu.SemaphoreType.DMA((n,))],
    )(*parts)


def _all_reduce_small(v):
    nd, r, w = v.shape
    assert nd == N_DEV

    def body(v_ref, out_ref, land_ref, red_ref, send1, recv1, send2, recv2):
        x, y, c = lax.axis_index("x"), lax.axis_index("y"), lax.axis_index("c")
        me = 4 * x + 2 * y + c
        peers = []
        for k in range(1, N_DEV):
            kx, ky, kc = (k >> 2) & 1, (k >> 1) & 1, k & 1
            peers.append((1 - x if kx else x, 1 - y if ky else y, 1 - c if kc else c))
        first = []
        for k, (px, py, pc) in enumerate(peers):
            cp = pltpu.make_async_remote_copy(
                src_ref=v_ref.at[4 * px + 2 * py + pc], dst_ref=land_ref.at[me], send_sem=send1.at[k],
                recv_sem=recv1.at[k], device_id=(px, py, pc), device_id_type=MESH)
            cp.start()
            first.append(cp)
        land_ref[me] = v_ref[me]
        for cp in first:
            cp.wait()
        acc = land_ref[0]
        for j in range(1, N_DEV):
            acc = acc + land_ref[j]
        red_ref[...] = acc
        second = []
        for k, (px, py, pc) in enumerate(peers):
            cp = pltpu.make_async_remote_copy(
                src_ref=red_ref, dst_ref=out_ref.at[me], send_sem=send2.at[k],
                recv_sem=recv2.at[k], device_id=(px, py, pc), device_id_type=MESH)
            cp.start()
            second.append(cp)
        out_ref[me] = acc
        for cp in second:
            cp.wait()

    vmem = pl.BlockSpec(memory_space=pltpu.VMEM)
    return pl.pallas_call(
        body, name="all_reduce_small",
        in_specs=[vmem], out_specs=vmem,
        out_shape=jax.ShapeDtypeStruct((nd, r, w), F32),
        scratch_shapes=[pltpu.VMEM((nd, r, w), F32), pltpu.VMEM((r, w), F32)]
        + [pltpu.SemaphoreType.DMA((N_DEV - 1,))] * 4,
        compiler_params=pltpu.CompilerParams(vmem_limit_bytes=VMEM_LIMIT),
    )(v)


def _adam_math(w, g, m, v):
    m = ADAM_B1 * m + (1.0 - ADAM_B1) * g
    v = ADAM_B2 * v + (1.0 - ADAM_B2) * (g * g)
    m_hat = m / (1.0 - ADAM_B1 ** ADAM_STEP)
    v_hat = v / (1.0 - ADAM_B2 ** ADAM_STEP)
    delta = -ADAM_LR * (m_hat / (jnp.sqrt(v_hat) + ADAM_EPS) + ADAM_WD * w)
    return delta, m, v


def _sum_blocks(own, got, chip, name):
    _, r, c = own.shape
    tm = _tile(r, ROWS, 16)

    def kern(chip_ref, own_ref, got_ref, o_ref):
        del chip_ref
        acc = own_ref[...].astype(F32)
        for k in range(3):
            acc = acc + got_ref[k].astype(F32)
        o_ref[...] = acc

    return pl.pallas_call(
        kern, name=name,
        grid_spec=pltpu.PrefetchScalarGridSpec(
            num_scalar_prefetch=1, grid=(r // tm,),
            in_specs=[pl.BlockSpec((None, tm, c), lambda i, ch: (ch[0], i, 0)),
                      pl.BlockSpec((3, tm, c), lambda i, ch: (0, i, 0))],
            out_specs=pl.BlockSpec((tm, c), lambda i, ch: (i, 0))),
        out_shape=jax.ShapeDtypeStruct((r, c), F32),
        compiler_params=_params(("parallel",)),
    )(chip, own, got)


def _adamw(parts, w, m, v, name):
    nl, r, c = w.shape
    assert len(parts) == nl
    tm = _tile(r, 128, 8)

    def kern(*refs):
        p_refs = refs[:2 * nl]
        w_ref, m_ref, v_ref, g_ref, d_ref, mo_ref, vo_ref = refs[2 * nl:]
        layer = pl.program_id(0)
        g = p_refs[0][...] + p_refs[1][...]
        for ll in range(1, nl):
            g = jnp.where(layer == ll, p_refs[2 * ll][...] + p_refs[2 * ll + 1][...], g)
        g_ref[...] = g
        d_ref[...], mo_ref[...], vo_ref[...] = _adam_math(w_ref[...], g, m_ref[...], v_ref[...])

    def part_spec(ll):
        return pl.BlockSpec((tm, c), lambda l, i: (jnp.where(l == ll, i, 0), 0))

    spec = pl.BlockSpec((None, tm, c), lambda l, i: (l, i, 0))
    return pl.pallas_call(
        kern, name=name, grid=(nl, r // tm),
        in_specs=[part_spec(ll) for ll in range(nl) for _ in range(2)] + [spec] * 3, out_specs=[spec] * 4,
        out_shape=[jax.ShapeDtypeStruct((nl, r, c), F32)] * 4,
        compiler_params=_params(("arbitrary", "parallel")),
    )(*[p for pair in parts for p in pair], w, m, v)


def _adamw_small(g, w, m, v, name):
    def kern(g_ref, w_ref, m_ref, v_ref, d_ref, mo_ref, vo_ref):
        d_ref[...], mo_ref[...], vo_ref[...] = _adam_math(w_ref[...], g_ref[...], m_ref[...], v_ref[...])

    return pl.pallas_call(
        kern, name=name,
        out_shape=[jax.ShapeDtypeStruct(g.shape, F32)] * 3,
        compiler_params=pltpu.CompilerParams(vmem_limit_bytes=VMEM_LIMIT),
    )(g, w, m, v)


def _pack(arrays, rows):
    flat = jnp.concatenate([a.reshape(-1).astype(F32) for a in arrays])
    return jnp.pad(flat, (0, rows * 128 - flat.shape[0])).reshape(rows, 128)


def _unpack(packed, shapes):
    flat = packed.reshape(-1)
    out, off = [], 0
    for shp in shapes:
        size = math.prod(shp)
        out.append(flat[off:off + size].reshape(shp))
        off += size
    return out


_REPLICATED = ["s5_lam_re", "s5_lam_im", "s5_log_dt", "s5_b_re", "s5_b_im", "s5_c_re", "s5_c_im",
               "ffn_norm", "b_norm_mix", "kv_norm", "final_norm"]
_CHIP_VECTORS = ["s5_d", "a_norm_mix"]
_BIG = ["s5_w_glu", "ffn_w_in", "ffn_w_out", "attn_w_q", "attn_w_o", "w_kv"]
_WEIGHT_ORDER = ["s5_lam_re", "s5_lam_im", "s5_log_dt", "s5_b_re", "s5_b_im", "s5_c_re", "s5_c_im", "s5_d",
                 "s5_w_glu", "a_norm_mix", "ffn_norm", "ffn_w_in", "ffn_w_out", "b_norm_mix", "attn_w_q",
                 "attn_w_o", "kv_norm", "w_kv", "final_norm"]


def _step(x, loss_target, w, m, v):
    s, d = x.shape[1], x.shape[2]
    chip = 2 * lax.axis_index("x") + lax.axis_index("y")

    col_sharded = dict(w_glu=("s5_w_glu", 0), w_in0=("ffn_w_in", 0), w_in1=("ffn_w_in", 1), w_q=("attn_w_q", 0))
    row_sharded = dict(w_out0=("ffn_w_out", 0), w_out1=("ffn_w_out", 1), w_o=("attn_w_o", 0), w_kv=("w_kv", 0))
    names = ["w_glu", "w_in0", "w_out0", "w_kv", "w_q", "w_o", "w_in1", "w_out1"]
    local = {**col_sharded, **row_sharded}

    def layers(a):
        return a.reshape((-1,) + a.shape[-2:])

    chip_arr = jnp.reshape(chip, (1,)).astype(jnp.int32)
    vector_lands = [lax.dynamic_update_slice(jnp.zeros((N_CHIPS,) + w[n].shape, F32), w[n][None], (chip, 0, 0))
                    for n in _CHIP_VECTORS]
    lands = [_cast_place(layers(w[local[n][0]]), local[n][1], chip_arr, f"cast_{n}") for n in names]
    send, recv, land_thru = _gather_start(vector_lands + lands)
    vectors = _gather_wait(land_thru[:2], send[:2], recv[:2], jnp.zeros((8, 128), F32), "gather_wait_vectors")
    s5_d_full = vectors[0].reshape(1, d)
    a_norm_full = vectors[1].reshape(1, d)
    arrived = {}

    def weight(name, after):
        if name not in arrived:
            i = 2 + names.index(name)
            land, = _gather_wait([land_thru[i]], [send[i]], [recv[i]], after, f"gather_wait_{name}")
            arrived[name] = land if name in col_sharded else land.reshape(1, -1, land.shape[-1])
        return arrived[name]

    started = {}

    def emit(name, dw):
        outs = _scatter_start(dw.reshape(N_CHIPS, -1, dw.shape[-1]), f"scatter_start_{name}")
        started[name] = outs[:4]
        return outs[4][0, 0]

    small = dict(lam_re=w["s5_lam_re"][0], lam_im=w["s5_lam_im"][0], log_dt=w["s5_log_dt"][0],
                 b_re=w["s5_b_re"][0], b_im=w["s5_b_im"][0], c_re=w["s5_c_re"][0], c_im=w["s5_c_im"][0],
                 s5_d=s5_d_full, a_norm=a_norm_full, ffn_norm0=w["ffn_norm"][0:1], ffn_norm1=w["ffn_norm"][1:2],
                 b_norm=w["b_norm_mix"], kv_norm=w["kv_norm"].reshape(1, d), final_norm=w["final_norm"].reshape(1, d))

    loss_rows, grad_x, d_small = _local_step(x[0], loss_target[0], small, weight, emit)

    g4, got = _scatter_wait([started[n] for n in names], grad_x)
    partial = [_sum_blocks(o, r, chip_arr, f"sum_{n}") for n, o, r in zip(names, g4, got)]
    other = _swap_cores(partial)
    part = dict(zip(names, zip(partial, other)))
    result = {}
    for name in _BIG:
        parts = [part[n] for n in sorted((n for n in names if local[n][0] == name), key=lambda n: local[n][1])]
        outs = _adamw(parts, layers(w[name]), layers(m[name]), layers(v[name]), f"adamw_{name}")
        result[name] = [o.reshape(w[name].shape) for o in outs]

    rep_grads = [d_small["lam_re"], d_small["lam_im"], d_small["log_dt"], d_small["b_re"], d_small["b_im"],
                 d_small["c_re"], d_small["c_im"],
                 jnp.concatenate([d_small["ffn_norm0"], d_small["ffn_norm1"]], axis=0),
                 d_small["b_norm"], d_small["kv_norm"], d_small["final_norm"]]
    vec_grads = [d_small["s5_d"], d_small["a_norm"]]
    to_reduce = rep_grads + vec_grads + [jnp.sum(loss_rows).reshape(1)]
    total = sum(math.prod(a.shape) for a in to_reduce)
    rows_per = -(-total // (N_DEV * 128 * 8)) * 8
    reduced = _all_reduce_small(_pack(to_reduce, N_DEV * rows_per).reshape(N_DEV, rows_per, 128))
    red = _unpack(reduced, [a.shape for a in to_reduce])
    loss = red[-1][0]
    g_small = dict(zip(_REPLICATED, [r.reshape(w[n].shape) for r, n in zip(red[:len(rep_grads)], _REPLICATED)]))
    for n, r in zip(_CHIP_VECTORS, red[len(rep_grads):-1]):
        g_small[n] = lax.dynamic_slice_in_dim(r.reshape(1, d), chip * (d // N_CHIPS), d // N_CHIPS, axis=1)
    small_names = _REPLICATED + _CHIP_VECTORS
    n_small = sum(math.prod(w[n].shape) for n in small_names)
    rows_small = -(-n_small // (128 * 8)) * 8
    packed = [_pack([src[n] for n in small_names], rows_small) for src in (g_small, w, m, v)]
    upd = _adamw_small(*packed, "adamw_small")
    shapes = [w[n].shape for n in small_names]
    for n, dl, mo, vo in zip(small_names, *[_unpack(u, shapes) for u in upd]):
        result[n] = [g_small[n], dl, mo, vo]

    cols = [[result[n][t] for n in _WEIGHT_ORDER] for t in range(4)]
    return (loss, grad_x.reshape(x.shape), *cols[0], *cols[1], *cols[2], *cols[3])


def kernel(x, s5_lam_re, s5_lam_im, s5_log_dt, s5_b_re, s5_b_im, s5_c_re, s5_c_im, s5_d, s5_w_glu, a_norm_mix, ffn_norm, ffn_w_in, ffn_w_out, b_norm_mix, attn_w_q, attn_w_o, kv_norm, w_kv, final_norm, loss_target, m_s5_lam_re, m_s5_lam_im, m_s5_log_dt, m_s5_b_re, m_s5_b_im, m_s5_c_re, m_s5_c_im, m_s5_d, m_s5_w_glu, m_a_norm_mix, m_ffn_norm, m_ffn_w_in, m_ffn_w_out, m_b_norm_mix, m_attn_w_q, m_attn_w_o, m_kv_norm, m_w_kv, m_final_norm, v_s5_lam_re, v_s5_lam_im, v_s5_log_dt, v_s5_b_re, v_s5_b_im, v_s5_c_re, v_s5_c_im, v_s5_d, v_s5_w_glu, v_a_norm_mix, v_ffn_norm, v_ffn_w_in, v_ffn_w_out, v_b_norm_mix, v_attn_w_q, v_attn_w_o, v_kv_norm, v_w_kv, v_final_norm):
    w = dict(zip(_WEIGHT_ORDER, (s5_lam_re, s5_lam_im, s5_log_dt, s5_b_re, s5_b_im, s5_c_re, s5_c_im, s5_d, s5_w_glu, a_norm_mix, ffn_norm, ffn_w_in, ffn_w_out, b_norm_mix, attn_w_q, attn_w_o, kv_norm, w_kv, final_norm)))
    m = dict(zip(_WEIGHT_ORDER, (m_s5_lam_re, m_s5_lam_im, m_s5_log_dt, m_s5_b_re, m_s5_b_im, m_s5_c_re, m_s5_c_im, m_s5_d, m_s5_w_glu, m_a_norm_mix, m_ffn_norm, m_ffn_w_in, m_ffn_w_out, m_b_norm_mix, m_attn_w_q, m_attn_w_o, m_kv_norm, m_w_kv, m_final_norm)))
    v = dict(zip(_WEIGHT_ORDER, (v_s5_lam_re, v_s5_lam_im, v_s5_log_dt, v_s5_b_re, v_s5_b_im, v_s5_c_re, v_s5_c_im, v_s5_d, v_s5_w_glu, v_a_norm_mix, v_ffn_norm, v_ffn_w_in, v_ffn_w_out, v_b_norm_mix, v_attn_w_q, v_attn_w_o, v_kv_norm, v_w_kv, v_final_norm)))
    return _step(x, loss_target, w, m, v)
```

```python
import functools
import math

import jax
import jax.numpy as jnp
from jax import lax
from jax.experimental import pallas as pl
from jax.experimental.pallas import tpu as pltpu

F32 = jnp.float32
BF16 = jnp.bfloat16

S5_GROUP_CH = 16
S5_STATE = 64
GROUPS_PER_TILE = 8
HEAD_DIM = 128
N_KV_HEADS = 4
PATTERNS = ((128, 1), (512, 4), (2048, 16))
ATT_BLK = 128
EPS = 1e-6
NEG_INF = -1e30
SCAN_T = 256
ADAM_LR = 0.001
ADAM_B1 = 0.9
ADAM_B2 = 0.999
ADAM_EPS = 1e-08
ADAM_WD = 0.01
ADAM_STEP = 10
N_CHIPS = 4
N_DEV = 8
VMEM_LIMIT = 56 * 1024 * 1024
MESH = pl.DeviceIdType.MESH
GELU_K = math.sqrt(2.0 / math.pi)
GELU_C = 0.044715


def _tile(n, pref, unit=128):
    if n <= pref:
        return n
    best = None
    t = unit
    while t <= pref:
        if n % t == 0:
            best = t
        t += unit
    assert best is not None, (n, pref, unit)
    return best


def _params(sem):
    return pltpu.CompilerParams(dimension_semantics=sem, vmem_limit_bytes=VMEM_LIMIT)


def _dot(a, b):
    return jnp.dot(a, b, preferred_element_type=F32)


def _dot_nt(a, b):
    return lax.dot_general(a, b, (((1,), (1,)), ((), ())), preferred_element_type=F32)


def _dot_tn(a, b):
    return lax.dot_general(a, b, (((0,), (0,)), ((), ())), preferred_element_type=F32)


def _mm_nn(a, w, out_dtype, name, tm=512, tn=1536, tk=2048):
    m, k = a.shape
    nb, k2, nq = w.shape
    assert k == k2
    tm, tn, tk = _tile(m, tm, 8), _tile(nq, tn), _tile(k, tk)
    per, nk = nq // tn, k // tk

    def kern(a_ref, w_ref, o_ref, *acc):
        p = _dot(a_ref[...], w_ref[...])
        if nk == 1:
            o_ref[...] = p.astype(o_ref.dtype)
        else:
            acc_ref, = acc
            kk = pl.program_id(2)

            @pl.when(kk == 0)
            def _():
                acc_ref[...] = p

            @pl.when(kk > 0)
            def _():
                acc_ref[...] += p

            @pl.when(kk == nk - 1)
            def _():
                o_ref[...] = acc_ref[...].astype(o_ref.dtype)

    return pl.pallas_call(
        kern, name=name, grid=(nb * per, m // tm, nk),
        in_specs=[pl.BlockSpec((tm, tk), lambda j, i, kk: (i, kk)),
                  pl.BlockSpec((None, tk, tn), lambda j, i, kk: (j // per, kk, j % per))],
        out_specs=pl.BlockSpec((tm, tn), lambda j, i, kk: (i, j)),
        out_shape=jax.ShapeDtypeStruct((m, nb * nq), out_dtype),
        scratch_shapes=[] if nk == 1 else [pltpu.VMEM((tm, tn), F32)],
        compiler_params=_params(("parallel", "parallel", "arbitrary")),
    )(a, w)


def _paired_block(r, per, nb):
    j = r // 2
    return (r % 2) * (nb // 2) + j // per, j % per


def _mm_nt(a, w, name, tm=512, tr=1536, tkc=1024, paired=False):
    m, n = a.shape
    nb, k, nq = w.shape
    assert n == nb * nq
    tm, tr, tkc = _tile(m, tm, 8), _tile(nq, tr), _tile(k, tkc)
    per = nq // tr

    def kern(a_ref, w_ref, o_ref):
        acc = None
        for r in range(nb * per):
            blk, tile = _paired_block(r, per, nb) if paired else (r // per, r % per)
            p = _dot_nt(a_ref[:, r * tr:(r + 1) * tr], w_ref[blk, :, tile * tr:(tile + 1) * tr])
            acc = p if acc is None else acc + p
        o_ref[...] = acc

    return pl.pallas_call(
        kern, name=name, grid=(k // tkc, m // tm),
        in_specs=[pl.BlockSpec((tm, n), lambda kc, i: (i, 0)),
                  pl.BlockSpec((nb, tkc, nq), lambda kc, i: (0, kc, 0))],
        out_specs=pl.BlockSpec((tm, tkc), lambda kc, i: (i, kc)),
        out_shape=jax.ShapeDtypeStruct((m, k), F32),
        compiler_params=_params(("parallel", "parallel")),
    )(a, w)


def _mm_tn(a, dy, nb, name, tkk=512, tn=1536, paired=False):
    s, k = a.shape
    s2, n = dy.shape
    assert s == s2 and n % nb == 0
    nq = n // nb
    tkk, tn = _tile(k, tkk), _tile(nq, tn)
    per = nq // tn

    def w_block(j):
        return _paired_block(j, per, nb) if paired else (j // per, j % per)

    def kern(a_ref, dy_ref, o_ref):
        o_ref[...] = _dot_tn(a_ref[...], dy_ref[...]).astype(o_ref.dtype)

    return pl.pallas_call(
        kern, name=name, grid=(nb * per, k // tkk),
        in_specs=[pl.BlockSpec((s, tkk), lambda j, kk: (0, kk)),
                  pl.BlockSpec((s, tn), lambda j, kk: (0, j))],
        out_specs=pl.BlockSpec((None, tkk, tn), lambda j, kk: (w_block(j)[0], kk, w_block(j)[1])),
        out_shape=jax.ShapeDtypeStruct((nb, k, nq), BF16),
        compiler_params=_params(("parallel", "parallel")),
    )(a, dy)


ROWS = 256


def _rms(x, g):
    r = lax.rsqrt(jnp.mean(x * x, axis=-1, keepdims=True) + EPS)
    return x * r * g


def _rms_bwd(x, g, dh):
    r = lax.rsqrt(jnp.mean(x * x, axis=-1, keepdims=True) + EPS)
    xh = x * r
    dgx = dh * g
    dx = r * (dgx - xh * jnp.mean(dgx * xh, axis=-1, keepdims=True))
    return dx, dh * xh


def _sigmoid(x):
    return 1.0 / (1.0 + jnp.exp(-x))


def _gelu(y):
    return 0.5 * y * (1.0 + jnp.tanh(GELU_K * (y + GELU_C * y * y * y)))


def _gelu_grad(y):
    t = jnp.tanh(GELU_K * (y + GELU_C * y * y * y))
    return 0.5 * (1.0 + t) + 0.5 * y * (1.0 - t * t) * GELU_K * (1.0 + 3.0 * GELU_C * y * y)


def _row_spec(tm, d, col=0):
    return pl.BlockSpec((tm, d), lambda i: (i, col))


def _vec_spec(d):
    return pl.BlockSpec((1, d), lambda i: (0, 0))


def _acc_rows(ref, val, i):
    s = jnp.sum(val, axis=0, keepdims=True)

    @pl.when(i == 0)
    def _():
        ref[...] = s

    @pl.when(i > 0)
    def _():
        ref[...] += s


def _norm_f32(x, g, name):
    s, d = x.shape
    tm = _tile(s, ROWS, 8)

    def kern(x_ref, g_ref, h_ref):
        h_ref[...] = _rms(x_ref[...], g_ref[...])

    return pl.pallas_call(
        kern, name=name, grid=(s // tm,),
        in_specs=[_row_spec(tm, d), _vec_spec(d)],
        out_specs=_row_spec(tm, d),
        out_shape=jax.ShapeDtypeStruct((s, d), F32),
        compiler_params=_params(("parallel",)),
    )(x, g)


def _glu_res_norm(vg, x, g, name):
    s, d = x.shape
    tm = _tile(s, ROWS, 16)

    def kern(val_ref, gate_ref, x_ref, g_ref, x1_ref, hf_ref):
        x1 = x_ref[...] + val_ref[...] * _sigmoid(gate_ref[...])
        x1_ref[...] = x1
        hf_ref[...] = _rms(x1, g_ref[...]).astype(BF16)

    return pl.pallas_call(
        kern, name=name, grid=(s // tm,),
        in_specs=[_row_spec(tm, d, 0), _row_spec(tm, d, 1), _row_spec(tm, d), _vec_spec(d)],
        out_specs=[_row_spec(tm, d), _row_spec(tm, d)],
        out_shape=[jax.ShapeDtypeStruct((s, d), F32), jax.ShapeDtypeStruct((s, d), BF16)],
        compiler_params=_params(("parallel",)),
    )(vg, vg, x, g)


FFN_TILE = 1408


def _ffn_in_act(hf, w_in, name, tm=512):
    s, k = hf.shape
    nb, _, nq = w_in.shape
    tm, tn = _tile(s, tm, 16), _tile(nq, FFN_TILE)
    per = nq // tn
    nf = (nb // 2) * per

    def kern(h_ref, wg_ref, wu_ref, a_ref, gu_ref):
        h = h_ref[...]
        g = _dot(h, wg_ref[...])
        u = _dot(h, wu_ref[...])
        a_ref[...] = (g * _sigmoid(g) * u).astype(BF16)
        gu_ref[:, :tn] = g.astype(BF16)
        gu_ref[:, tn:] = u.astype(BF16)

    return pl.pallas_call(
        kern, name=name, grid=(nf, s // tm),
        in_specs=[pl.BlockSpec((tm, k), lambda j, i: (i, 0)),
                  pl.BlockSpec((None, k, tn), lambda j, i: (j // per, 0, j % per)),
                  pl.BlockSpec((None, k, tn), lambda j, i: (nb // 2 + j // per, 0, j % per))],
        out_specs=[pl.BlockSpec((tm, tn), lambda j, i: (i, j)),
                   pl.BlockSpec((tm, 2 * tn), lambda j, i: (i, j))],
        out_shape=[jax.ShapeDtypeStruct((s, nf * tn), BF16), jax.ShapeDtypeStruct((s, 2 * nf * tn), BF16)],
        compiler_params=_params(("parallel", "parallel")),
    )(hf, w_in, w_in)


def _ffn_out_dx_act(dresb, w_out, gu, name, tm=512):
    s, d = dresb.shape
    f = w_out.shape[1]
    tm = _tile(s, tm, 16)
    tn = _tile(f // 2, FFN_TILE)

    def kern(d_ref, w_ref, gu_ref, o_ref):
        da = _dot_nt(d_ref[...], w_ref[...])
        g = gu_ref[:, :tn].astype(F32)
        u = gu_ref[:, tn:].astype(F32)
        sg = _sigmoid(g)
        o_ref[:, :tn] = (da * u * sg * (1.0 + g * (1.0 - sg))).astype(BF16)
        o_ref[:, tn:] = (da * g * sg).astype(BF16)

    pair = pl.BlockSpec((tm, 2 * tn), lambda j, i: (i, j))
    return pl.pallas_call(
        kern, name=name, grid=(f // tn, s // tm),
        in_specs=[pl.BlockSpec((tm, d), lambda j, i: (i, 0)),
                  pl.BlockSpec((None, tn, d), lambda j, i: (0, j, 0)), pair],
        out_specs=pair,
        out_shape=jax.ShapeDtypeStruct((s, 2 * f), BF16),
        compiler_params=_params(("parallel", "parallel")),
    )(dresb, w_out, gu)


def _res_norm(o, x, gains, name):
    s, d = x.shape
    tm = _tile(s, ROWS, 16)
    ng = len(gains)

    def kern(o_ref, x_ref, *refs):
        xn = x_ref[...] + o_ref[...]
        refs[ng][...] = xn
        for t in range(ng):
            refs[ng + 1 + t][...] = _rms(xn, refs[t][...]).astype(BF16)

    return pl.pallas_call(
        kern, name=name, grid=(s // tm,),
        in_specs=[_row_spec(tm, d), _row_spec(tm, d)] + [_vec_spec(d)] * ng,
        out_specs=[_row_spec(tm, d)] * (1 + ng),
        out_shape=[jax.ShapeDtypeStruct((s, d), F32)] + [jax.ShapeDtypeStruct((s, d), BF16)] * ng,
        compiler_params=_params(("parallel",)),
    )(o, x, *gains)


def _loss_head(o, x, g, target, name):
    s, d = x.shape
    tm = _tile(s, ROWS, 16)

    def kern(o_ref, x_ref, g_ref, t_ref, dx_ref, dxb_ref, loss_ref, dg_ref):
        i = pl.program_id(0)
        x4 = x_ref[...] + o_ref[...]
        gg = g_ref[...]
        diff = _rms(x4, gg) - t_ref[...]
        dx, dgr = _rms_bwd(x4, gg, diff * (1.0 / d))
        dx_ref[...] = dx
        dxb_ref[...] = dx.astype(BF16)
        _acc_rows(loss_ref, diff * diff * (0.5 / d), i)
        _acc_rows(dg_ref, dgr, i)

    return pl.pallas_call(
        kern, name=name, grid=(s // tm,),
        in_specs=[_row_spec(tm, d), _row_spec(tm, d), _vec_spec(d), _row_spec(tm, d)],
        out_specs=[_row_spec(tm, d), _row_spec(tm, d), _vec_spec(d), _vec_spec(d)],
        out_shape=[jax.ShapeDtypeStruct((s, d), F32), jax.ShapeDtypeStruct((s, d), BF16),
                   jax.ShapeDtypeStruct((1, d), F32), jax.ShapeDtypeStruct((1, d), F32)],
        compiler_params=_params(("arbitrary",)),
    )(o, x, g, target)


def _norm_bwd(x, gains, dhs, dres, name):
    s, d = x.shape
    tm = _tile(s, ROWS, 16)
    ng = len(gains)

    def kern(x_ref, dres_ref, *refs):
        i = pl.program_id(0)
        x_ = x_ref[...]
        acc = dres_ref[...]
        for t in range(ng):
            dx, dgr = _rms_bwd(x_, refs[t][...], refs[ng + t][...])
            acc = acc + dx
            _acc_rows(refs[2 * ng + 2 + t], dgr, i)
        refs[2 * ng][...] = acc
        refs[2 * ng + 1][...] = acc.astype(BF16)

    return pl.pallas_call(
        kern, name=name, grid=(s // tm,),
        in_specs=[_row_spec(tm, d), _row_spec(tm, d)] + [_vec_spec(d)] * ng + [_row_spec(tm, d)] * ng,
        out_specs=[_row_spec(tm, d), _row_spec(tm, d)] + [_vec_spec(d)] * ng,
        out_shape=[jax.ShapeDtypeStruct((s, d), F32), jax.ShapeDtypeStruct((s, d), BF16)]
        + [jax.ShapeDtypeStruct((1, d), F32)] * ng,
        compiler_params=_params(("arbitrary",)),
    )(x, dres, *gains, *dhs)


def _glu_bwd(dmix, vg, name):
    s, d = dmix.shape
    tm = _tile(s, ROWS, 16)

    def kern(dm_ref, val_ref, gate_ref, o_ref):
        dm = dm_ref[...]
        sg = _sigmoid(gate_ref[...])
        o_ref[:, :d] = (dm * sg).astype(BF16)
        o_ref[:, d:] = (dm * val_ref[...] * sg * (1.0 - sg)).astype(BF16)

    return pl.pallas_call(
        kern, name=name, grid=(s // tm,),
        in_specs=[_row_spec(tm, d), _row_spec(tm, d, 0), _row_spec(tm, d, 1)],
        out_specs=_row_spec(tm, 2 * d),
        out_shape=jax.ShapeDtypeStruct((s, 2 * d), BF16),
        compiler_params=_params(("parallel",)),
    )(dmix, vg, vg)


SUB = 8


def _local_scan(vr, vi, pre_ref, pim_ref, reverse):
    sub = lax.broadcasted_iota(jnp.int32, vr.shape, 1)
    sign = -1.0 if reverse else 1.0
    for sh in (1, 2, 4):
        ar = pre_ref[sh - 1:sh, :][None]
        ai = sign * pim_ref[sh - 1:sh, :][None]
        keep = sub < SUB - sh if reverse else sub >= sh
        sr = jnp.where(keep, pltpu.roll(vr, SUB - sh if reverse else sh, 1), 0.0)
        si = jnp.where(keep, pltpu.roll(vi, SUB - sh if reverse else sh, 1), 0.0)
        vr, vi = vr + ar * sr - ai * si, vi + ar * si + ai * sr
    return vr, vi


def _s5_fwd(u, bbd_re, bbd_im, cbd_re, cbd_im, pw_re, pw_im, dskip, name):
    s, d = u.shape
    nt, cw, lw = bbd_re.shape
    t = _tile(s, SCAN_T, SUB)
    nc = s // t
    ng = t // SUB

    def kern(u_ref, bre_ref, bim_ref, cre_ref, cim_ref, pre_ref, pim_ref, d_ref,
             xr_ref, xi_ref, y_ref, z_ref, car_ref, cai_ref):
        c = pl.program_id(1)

        @pl.when(c == 0)
        def _():
            car_ref[...] = jnp.zeros_like(car_ref)
            cai_ref[...] = jnp.zeros_like(cai_ref)

        u_ = u_ref[...]
        ub = u_.astype(BF16)
        vr, vi = _local_scan(_dot(ub, bre_ref[...]).reshape(ng, SUB, lw),
                             _dot(ub, bim_ref[...]).reshape(ng, SUB, lw), pre_ref, pim_ref, False)
        cr = car_ref[...]
        ci = cai_ref[...]
        pr = pre_ref[...]
        pi = pim_ref[...]
        for gidx in range(ng):
            rows = slice(gidx * SUB, (gidx + 1) * SUB)
            gr = vr[gidx] + pr * cr - pi * ci
            gi = vi[gidx] + pr * ci + pi * cr
            xr_ref[rows, :] = gr
            xi_ref[rows, :] = gi
            cr, ci = gr[SUB - 1:SUB, :], gi[SUB - 1:SUB, :]
        car_ref[...] = cr
        cai_ref[...] = ci
        y = (_dot(xr_ref[...].astype(BF16), cre_ref[...]) - _dot(xi_ref[...].astype(BF16), cim_ref[...])
             + d_ref[...] * u_)
        y_ref[...] = y
        z_ref[...] = _gelu(y).astype(BF16)

    tok = pl.BlockSpec((t, cw), lambda j, c: (c, j))
    st = pl.BlockSpec((t, lw), lambda j, c: (c, j))
    return pl.pallas_call(
        kern, name=name, grid=(nt, nc),
        in_specs=[tok,
                  pl.BlockSpec((None, cw, lw), lambda j, c: (j, 0, 0)),
                  pl.BlockSpec((None, cw, lw), lambda j, c: (j, 0, 0)),
                  pl.BlockSpec((None, lw, cw), lambda j, c: (j, 0, 0)),
                  pl.BlockSpec((None, lw, cw), lambda j, c: (j, 0, 0)),
                  pl.BlockSpec((SUB, lw), lambda j, c: (0, j)),
                  pl.BlockSpec((SUB, lw), lambda j, c: (0, j)),
                  pl.BlockSpec((1, cw), lambda j, c: (0, j))],
        out_specs=[st, st, tok, tok],
        out_shape=[jax.ShapeDtypeStruct((s, nt * lw), F32), jax.ShapeDtypeStruct((s, nt * lw), F32),
                   jax.ShapeDtypeStruct((s, d), F32), jax.ShapeDtypeStruct((s, d), BF16)],
        scratch_shapes=[pltpu.VMEM((1, lw), F32), pltpu.VMEM((1, lw), F32)],
        compiler_params=_params(("parallel", "arbitrary")),
    )(u, bbd_re, bbd_im, cbd_re, cbd_im, pw_re, pw_im, dskip)


def _s5_bwd(dz, y, u, xs_re, xs_im, bbd_re, bbd_im, cbd_re, cbd_im, pw_re, pw_im, pf_re, pf_im, dskip, name):
    s, d = u.shape
    nt, cw, lw = bbd_re.shape
    t = _tile(s, SCAN_T, SUB)
    nc = s // t
    ng = t // SUB

    def kern(dz_ref, y_ref, u_ref, xr_ref, xi_ref, bre_ref, bim_ref, cre_ref, cim_ref,
             pre_ref, pim_ref, fre_ref, fim_ref, d_ref,
             du_ref, dd_ref, dcr_ref, dci_ref, dbr_ref, dbi_ref, dar_ref, dai_ref,
             car_ref, cai_ref, gr_ref, gi_ref):
        c = pl.program_id(1)

        @pl.when(c == 0)
        def _():
            car_ref[...] = jnp.zeros_like(car_ref)
            cai_ref[...] = jnp.zeros_like(cai_ref)

        u_ = u_ref[...]
        ub = u_.astype(BF16)
        dy = dz_ref[...] * _gelu_grad(y_ref[...])
        dyb = dy.astype(BF16)
        vr, vi = _local_scan(_dot_nt(dyb, cre_ref[...]).reshape(ng, SUB, lw),
                             (-_dot_nt(dyb, cim_ref[...])).reshape(ng, SUB, lw), pre_ref, pim_ref, True)
        later_r = car_ref[...]
        later_i = cai_ref[...]
        cr, ci = later_r, later_i
        fr = fre_ref[...]
        fi = fim_ref[...]
        for gidx in reversed(range(ng)):
            rows = slice(gidx * SUB, (gidx + 1) * SUB)
            ar = vr[gidx] + fr * cr + fi * ci
            ai = vi[gidx] + fr * ci - fi * cr
            gr_ref[rows, :] = ar
            gi_ref[rows, :] = ai
            cr, ci = ar[0:1, :], ai[0:1, :]
        car_ref[...] = cr
        cai_ref[...] = ci
        gr = gr_ref[...]
        gi = gi_ref[...]
        rows = lax.broadcasted_iota(jnp.int32, (t, lw), 0)
        gsr = jnp.where(rows < t - 1, pltpu.roll(gr, t - 1, 0), later_r)
        gsi = jnp.where(rows < t - 1, pltpu.roll(gi, t - 1, 0), later_i)
        xr = xr_ref[...]
        xi = xi_ref[...]
        dar = jnp.sum(gsr * xr + gsi * xi, axis=0, keepdims=True)
        dai = jnp.sum(gsi * xr - gsr * xi, axis=0, keepdims=True)
        grb = gr.astype(BF16)
        gib = gi.astype(BF16)
        dbr = _dot_tn(ub, grb)
        dbi = _dot_tn(ub, gib)
        dcr = _dot_tn(dyb, xr.astype(BF16))
        dci = _dot_tn(dyb, xi.astype(BF16))
        du_ref[...] = dy * d_ref[...] + _dot_nt(grb, bre_ref[...]) + _dot_nt(gib, bim_ref[...])
        ddv = jnp.sum(dy * u_, axis=0, keepdims=True)

        @pl.when(c == 0)
        def _():
            dd_ref[...] = ddv
            dcr_ref[...] = dcr
            dci_ref[...] = dci
            dbr_ref[...] = dbr
            dbi_ref[...] = dbi
            dar_ref[...] = dar
            dai_ref[...] = dai

        @pl.when(c > 0)
        def _():
            dd_ref[...] += ddv
            dcr_ref[...] += dcr
            dci_ref[...] += dci
            dbr_ref[...] += dbr
            dbi_ref[...] += dbi
            dar_ref[...] += dar
            dai_ref[...] += dai

    tok = pl.BlockSpec((t, cw), lambda j, c: (nc - 1 - c, j))
    st = pl.BlockSpec((t, lw), lambda j, c: (nc - 1 - c, j))
    wb = pl.BlockSpec((None, cw, lw), lambda j, c: (j, 0, 0))
    wc = pl.BlockSpec((None, lw, cw), lambda j, c: (j, 0, 0))
    pw = pl.BlockSpec((SUB, lw), lambda j, c: (0, j))
    vec_c = pl.BlockSpec((1, cw), lambda j, c: (0, j))
    vec_l = pl.BlockSpec((1, lw), lambda j, c: (0, j))
    return pl.pallas_call(
        kern, name=name, grid=(nt, nc),
        in_specs=[tok, tok, tok, st, st, wb, wb, wc, wc, pw, pw, pw, pw, vec_c],
        out_specs=[tok, vec_c, wb, wb, wb, wb, vec_l, vec_l],
        out_shape=[jax.ShapeDtypeStruct((s, d), F32), jax.ShapeDtypeStruct((1, d), F32)]
        + [jax.ShapeDtypeStruct((nt, cw, lw), F32)] * 4
        + [jax.ShapeDtypeStruct((1, nt * lw), F32)] * 2,
        scratch_shapes=[pltpu.VMEM((1, lw), F32), pltpu.VMEM((1, lw), F32),
                        pltpu.VMEM((t, lw), F32), pltpu.VMEM((t, lw), F32)],
        compiler_params=_params(("parallel", "arbitrary")),
    )(dz, y, u, xs_re, xs_im, bbd_re, bbd_im, cbd_re, cbd_im, pw_re, pw_im, pf_re, pf_im, dskip)


def _s5_discretize(lam_re, lam_im, log_dt, b_re, b_im):
    dt = jnp.exp(log_dt)[:, None]
    mag = jnp.exp(lam_re * dt)
    ang = lam_im * dt
    lb_re = mag * jnp.cos(ang)
    lb_im = mag * jnp.sin(ang)
    nr = lb_re - 1.0
    den = lam_re * lam_re + lam_im * lam_im
    f_re = (nr * lam_re + lb_im * lam_im) / den
    f_im = (lb_im * lam_re - nr * lam_im) / den
    bb_re = f_re[..., None] * b_re - f_im[..., None] * b_im
    bb_im = f_re[..., None] * b_im + f_im[..., None] * b_re
    return lb_re, lb_im, bb_re, bb_im


def _block_diag(w):
    nt, ng, a, b = w.shape
    eye = jnp.eye(ng, dtype=w.dtype)
    return (w[:, :, :, None, :] * eye[None, :, None, :, None]).reshape(nt, ng * a, ng * b)


def _block_diag_take(w, a, b):
    nt = w.shape[0]
    ng = GROUPS_PER_TILE
    w5 = w.reshape(nt, ng, a, ng, b)
    idx = jnp.arange(ng)
    return w5[:, idx, :, idx, :].transpose(1, 0, 2, 3).reshape(nt * ng, a, b)


def _att_combine(outs, lses, name):
    s, d = outs[0].shape
    tm = _tile(s, ROWS, 16)
    ng = len(outs)

    def kern(*refs):
        ls = [refs[ng + t][...] for t in range(ng)]
        mx = functools.reduce(jnp.maximum, ls)
        es = [jnp.exp(l - mx) for l in ls]
        den = functools.reduce(lambda a, b: a + b, es)
        o = functools.reduce(lambda a, b: a + b, [es[t] / den * refs[t][...] for t in range(ng)])
        refs[2 * ng][...] = o
        refs[2 * ng + 1][...] = o.astype(BF16)

    return pl.pallas_call(
        kern, name=name, grid=(s // tm,),
        in_specs=[_row_spec(tm, d)] * (2 * ng),
        out_specs=[_row_spec(tm, d)] * 2,
        out_shape=[jax.ShapeDtypeStruct((s, d), F32), jax.ShapeDtypeStruct((s, d), BF16)],
        compiler_params=_params(("parallel",)),
    )(*outs, *lses)


DIL_MAX = 16


def _slab(r):
    return 4 * (r % 4) + r // 4


def _to_slabs(xs, name):
    s, w = xs[0].shape
    n = s // DIL_MAX
    nx = len(xs)

    def kern(*refs):
        o_ref = refs[nx]
        for r in range(DIL_MAX):
            rows = [x_ref[pl.ds(r, n, stride=DIL_MAX), :] for x_ref in refs[:nx]]
            o_ref[_slab(r) * n:(_slab(r) + 1) * n, :] = functools.reduce(lambda a, b: a + b, rows)

    spec = pl.BlockSpec((s, 128), lambda i: (0, i))
    return pl.pallas_call(
        kern, name=name, grid=(w // 128,), in_specs=[spec] * nx, out_specs=spec,
        out_shape=jax.ShapeDtypeStruct((s, w), F32), compiler_params=_params(("parallel",)),
    )(*xs)


def _from_slabs(x, name):
    s, w = x.shape
    n = s // DIL_MAX

    def kern(x_ref, o_ref, ob_ref):
        for r in range(DIL_MAX):
            o_ref[pl.ds(r, n, stride=DIL_MAX), :] = x_ref[_slab(r) * n:(_slab(r) + 1) * n, :]
        ob_ref[...] = o_ref[...].astype(BF16)

    spec = pl.BlockSpec((s, 128), lambda i: (0, i))
    return pl.pallas_call(
        kern, name=name, grid=(w // 128,), in_specs=[spec], out_specs=[spec, spec],
        out_shape=[jax.ShapeDtypeStruct((s, w), F32), jax.ShapeDtypeStruct((s, w), BF16)],
        compiler_params=_params(("parallel",)),
    )(x)


def _norm_bf16(x, gains, name):
    s, d = x.shape
    tm = _tile(s, ROWS, 16)
    ng = len(gains)

    def kern(x_ref, *refs):
        x_ = x_ref[...]
        for t in range(ng):
            refs[ng + t][...] = _rms(x_, refs[t][...]).astype(BF16)

    return pl.pallas_call(
        kern, name=name, grid=(s // tm,),
        in_specs=[_row_spec(tm, d)] + [_vec_spec(d)] * ng, out_specs=[_row_spec(tm, d)] * ng,
        out_shape=[jax.ShapeDtypeStruct((s, d), BF16)] * ng,
        compiler_params=_params(("parallel",)),
    )(x, *gains)


def _att_geometry(dil, s):
    g = DIL_MAX // dil
    return s // DIL_MAX, g, max(ATT_BLK // g, 16)


def _att_mask(g, j, jb, rep):
    b = g * j

    def pos(i):
        sl, jj = i // j, i % j
        off = {1: 0, 4: sl, 16: sl // 4 + 4 * (sl % 4)}[g]
        return g * jj + off

    qi = lax.broadcasted_iota(jnp.int32, (rep * b, 2 * b), 0) % b
    ki = lax.broadcasted_iota(jnp.int32, (rep * b, 2 * b), 1)
    prev = ki < b
    dist = pos(qi) - pos(ki % b) + jnp.where(prev, b, 0)
    return (dist >= 0) & (dist <= ATT_BLK) & (jnp.logical_not(prev) | (jb > 0))


def _stack_heads(x, rep):
    return jnp.concatenate([x[:, h * HEAD_DIM:(h + 1) * HEAD_DIM] for h in range(rep)], axis=0)


def _att_fwd(q, kv, gi, dil, nh, name):
    s = q.shape[0]
    n, g, j = _att_geometry(dil, s)
    b = g * j
    ng = q.shape[1] // (nh * HEAD_DIM)
    rep = nh // N_KV_HEADS
    qw = rep * HEAD_DIM
    scale = HEAD_DIM ** -0.5
    q3 = q.reshape(DIL_MAX, n, q.shape[1])
    kv3 = kv.reshape(DIL_MAX, n, kv.shape[1])

    kw = N_KV_HEADS * HEAD_DIM

    def kern(q_ref, kc_ref, kp_ref, vc_ref, vp_ref, o_ref, l_ref):
        jb = pl.program_id(1)
        mask = _att_mask(g, j, jb, rep)
        for kh in range(N_KV_HEADS):
            hs = slice(kh * HEAD_DIM, (kh + 1) * HEAD_DIM)
            kc, kp, vc, vp = [r[:, :, hs].reshape(b, HEAD_DIM) for r in (kc_ref, kp_ref, vc_ref, vp_ref)]
            k2 = jnp.concatenate([kp, kc], axis=0)
            v2 = jnp.concatenate([vp, vc], axis=0)
            qs = _stack_heads(q_ref[:, :, kh * qw:(kh + 1) * qw].reshape(b, qw), rep)
            sc = jnp.where(mask, _dot_nt(qs, k2) * scale, NEG_INF)
            m = jnp.max(sc, axis=-1, keepdims=True)
            p = jnp.exp(sc - m)
            l = jnp.sum(p, axis=-1, keepdims=True)
            out = _dot((p / l).astype(BF16), v2)
            lse = jnp.broadcast_to(m + jnp.log(l), (rep * b, HEAD_DIM))
            for h in range(rep):
                cols = slice(kh * qw + h * HEAD_DIM, kh * qw + (h + 1) * HEAD_DIM)
                o_ref[:, :, cols] = out[h * b:(h + 1) * b].reshape(g, j, HEAD_DIM)
                l_ref[:, :, cols] = lse[h * b:(h + 1) * b].reshape(g, j, HEAD_DIM)

    def kv_spec(col, prev):
        if prev:
            return pl.BlockSpec((g, j, kw), lambda c, jb: (c, jnp.maximum(jb - 1, 0), col))
        return pl.BlockSpec((g, j, kw), lambda c, jb: (c, jb, col))

    out_spec = pl.BlockSpec((g, j, nh * HEAD_DIM), lambda c, jb: (c, jb, 0))
    out, lse = pl.pallas_call(
        kern, name=name, grid=(dil, n // j),
        in_specs=[pl.BlockSpec((g, j, nh * HEAD_DIM), lambda c, jb: (c, jb, gi)),
                  kv_spec(0, False), kv_spec(0, True), kv_spec(1, False), kv_spec(1, True)],
        out_specs=[out_spec, out_spec],
        out_shape=[jax.ShapeDtypeStruct((DIL_MAX, n, nh * HEAD_DIM), F32)] * 2,
        compiler_params=_params(("parallel", "arbitrary")),
    )(q3, kv3, kv3, kv3, kv3)
    del ng
    return out.reshape(s, nh * HEAD_DIM), lse.reshape(s, nh * HEAD_DIM)


def _att_bwd(q, kv, do, o, lses, gi, dil, nh, name):
    s = q.shape[0]
    n, g, j = _att_geometry(dil, s)
    b = g * j
    nb = n // j
    ng = len(lses)
    rep = nh // N_KV_HEADS
    qw = rep * HEAD_DIM
    dm = nh * HEAD_DIM
    scale = HEAD_DIM ** -0.5
    q3 = q.reshape(DIL_MAX, n, q.shape[1])
    kv3 = kv.reshape(DIL_MAX, n, kv.shape[1])
    wide = [a.reshape(DIL_MAX, n, dm) for a in (do, o, *lses)]
    kw = N_KV_HEADS * HEAD_DIM

    def kern(q_ref, kc_ref, kp_ref, vc_ref, vp_ref, do_ref, o_ref, *refs):
        l_refs = refs[:ng]
        dq_ref, dk_ref, dv_ref, ck_ref, cv_ref = refs[ng:]
        jb = pl.program_id(1)

        @pl.when(jb == 0)
        def _():
            ck_ref[...] = jnp.zeros_like(ck_ref)
            cv_ref[...] = jnp.zeros_like(cv_ref)

        @pl.when(jb < nb)
        def _():
            mask = _att_mask(g, j, jb, rep)
            for kh in range(N_KV_HEADS):
                hs = slice(kh * HEAD_DIM, (kh + 1) * HEAD_DIM)
                ws = slice(kh * qw, (kh + 1) * qw)

                def stacked(ref):
                    return _stack_heads(ref[:, :, ws].reshape(b, qw), rep)

                kc, kp, vc, vp = [r[:, :, hs].reshape(b, HEAD_DIM) for r in (kc_ref, kp_ref, vc_ref, vp_ref)]
                k2 = jnp.concatenate([kp, kc], axis=0)
                v2 = jnp.concatenate([vp, vc], axis=0)
                qs = stacked(q_ref)
                ls = [stacked(r) for r in l_refs]
                mx = functools.reduce(jnp.maximum, ls)
                den = functools.reduce(lambda a, c: a + c, [jnp.exp(l - mx) for l in ls])
                lse_g = ls[gi]
                w = jnp.exp(lse_g - mx) / den
                do_ = stacked(do_ref)
                ct = w[:, 0:1] * jnp.sum(do_ * stacked(o_ref), axis=-1, keepdims=True)
                dob = (w * do_).astype(BF16)
                p = jnp.exp(jnp.where(mask, _dot_nt(qs, k2) * scale, NEG_INF) - lse_g[:, 0:1])
                ds = (p * (_dot_nt(dob, v2) - ct) * scale).astype(BF16)
                dq = (_dot(ds, k2)).astype(BF16)
                for h in range(rep):
                    cols = slice(kh * qw + h * HEAD_DIM, kh * qw + (h + 1) * HEAD_DIM)
                    dq_ref[:, :, cols] = dq[h * b:(h + 1) * b].reshape(g, j, HEAD_DIM)
                dk2 = _dot_tn(ds, qs)
                dv2 = _dot_tn(p.astype(BF16), dob)
                dk_ref[:, :, hs] = (ck_ref[:, hs] + dk2[:b]).reshape(g, j, HEAD_DIM)
                dv_ref[:, :, hs] = (cv_ref[:, hs] + dv2[:b]).reshape(g, j, HEAD_DIM)
                ck_ref[:, hs] = dk2[b:]
                cv_ref[:, hs] = dv2[b:]

        @pl.when(jb == nb)
        def _():
            dk_ref[...] = ck_ref[...].reshape(g, j, kw)
            dv_ref[...] = cv_ref[...].reshape(g, j, kw)

    def jq(jb):
        return jnp.minimum(jb, nb - 1)

    def kv_spec(col, prev):
        if prev:
            return pl.BlockSpec((g, j, kw), lambda c, jb: (c, jnp.maximum(jq(jb) - 1, 0), col))
        return pl.BlockSpec((g, j, kw), lambda c, jb: (c, jq(jb), col))

    wide_spec = pl.BlockSpec((g, j, dm), lambda c, jb: (c, jq(jb), 0))
    dkv_spec = pl.BlockSpec((g, j, kw), lambda c, jb: (c, jnp.maximum(jb - 1, 0), 0))
    dq, dk, dv = pl.pallas_call(
        kern, name=name, grid=(dil, nb + 1),
        in_specs=[pl.BlockSpec((g, j, dm), lambda c, jb: (c, jq(jb), gi)),
                  kv_spec(0, False), kv_spec(0, True), kv_spec(1, False), kv_spec(1, True)]
        + [wide_spec] * (2 + ng),
        out_specs=[wide_spec, dkv_spec, dkv_spec],
        out_shape=[jax.ShapeDtypeStruct((DIL_MAX, n, dm), BF16),
                   jax.ShapeDtypeStruct((DIL_MAX, n, kw), F32),
                   jax.ShapeDtypeStruct((DIL_MAX, n, kw), F32)],
        scratch_shapes=[pltpu.VMEM((b, kw), F32), pltpu.VMEM((b, kw), F32)],
        compiler_params=_params(("parallel", "arbitrary")),
    )(q3, kv3, kv3, kv3, kv3, *wide)
    return dq.reshape(s, dm), dk.reshape(s, -1), dv.reshape(s, -1)


def _dkv_sum(dks, dvs, name):
    s, w = dks[0].shape
    tm = _tile(s, ROWS, 16)
    ng = len(dks)

    def kern(*refs):
        o_ref = refs[2 * ng]
        o_ref[:, :w] = functools.reduce(lambda a, b: a + b, [refs[t][...] for t in range(ng)]).astype(BF16)
        o_ref[:, w:] = functools.reduce(lambda a, b: a + b, [refs[ng + t][...] for t in range(ng)]).astype(BF16)

    return pl.pallas_call(
        kern, name=name, grid=(s // tm,),
        in_specs=[_row_spec(tm, w)] * (2 * ng),
        out_specs=_row_spec(tm, 2 * w),
        out_shape=jax.ShapeDtypeStruct((s, 2 * w), BF16),
        compiler_params=_params(("parallel",)),
    )(*dks, *dvs)


def _ffn_fwd(hf, weight, layer, tag):
    a, gu = _ffn_in_act(hf, weight(f"w_in{layer}", hf), f"{tag}_in")
    o = _mm_nn(a, weight(f"w_out{layer}", a), F32, f"{tag}_out", tn=1024, tk=5632)
    return gu, a, o


def _ffn_bwd(dresb, hf, gu, a, weight, emit, layer, tag):
    w_in, w_out = weight(f"w_in{layer}", None), weight(f"w_out{layer}", None)
    zero = emit(f"w_out{layer}", _mm_tn(a, dresb, 1, f"{tag}_out_dw", tn=1024))
    dgu = _ffn_out_dx_act(dresb, w_out, gu, f"{tag}_out_dx")
    zero = zero + emit(f"w_in{layer}",
                       _mm_tn(hf, dgu, w_in.shape[0], f"{tag}_in_dw", tn=FFN_TILE, paired=True))
    return _mm_nt(dgu, w_in, f"{tag}_in_dx", tr=FFN_TILE, tkc=512, paired=True), zero


def _local_step(x, target, small, weight, emit):
    s, d = x.shape
    nh = d // HEAD_DIM
    n_groups = d // S5_GROUP_CH
    nt = n_groups // GROUPS_PER_TILE
    p_, c_ = S5_STATE, S5_GROUP_CH

    disc_in = (small["lam_re"], small["lam_im"], small["log_dt"], small["b_re"], small["b_im"])
    (lb_re, lb_im, bb_re, bb_im), disc_vjp = jax.vjp(_s5_discretize, *disc_in)
    del lb_re, lb_im
    dt = jnp.exp(small["log_dt"])[:, None]
    def pole_powers(exponents):
        k = exponents[:, None, None]
        mag = jnp.exp(k * (small["lam_re"] * dt)[None])
        ang = k * (small["lam_im"] * dt)[None]
        return ((mag * jnp.cos(ang)).reshape(SUB, n_groups * p_),
                (mag * jnp.sin(ang)).reshape(SUB, n_groups * p_))

    pw_re, pw_im = pole_powers(jnp.arange(1, SUB + 1, dtype=F32))
    pf_re, pf_im = pole_powers(jnp.arange(SUB, 0, -1, dtype=F32))
    bbd_re = _block_diag(bb_re.transpose(0, 2, 1).reshape(nt, GROUPS_PER_TILE, c_, p_)).astype(BF16)
    bbd_im = _block_diag(bb_im.transpose(0, 2, 1).reshape(nt, GROUPS_PER_TILE, c_, p_)).astype(BF16)
    cbd_re = _block_diag(small["c_re"].transpose(0, 2, 1).reshape(nt, GROUPS_PER_TILE, p_, c_)).astype(BF16)
    cbd_im = _block_diag(small["c_im"].transpose(0, 2, 1).reshape(nt, GROUPS_PER_TILE, p_, c_)).astype(BF16)

    h0 = _norm_f32(x, small["a_norm"], "s5_norm")
    xs_re, xs_im, y0, z = _s5_fwd(h0, bbd_re, bbd_im, cbd_re, cbd_im, pw_re, pw_im, small["s5_d"], "s5_fwd")
    vg = _mm_nn(z, weight("w_glu", z), F32, "glu_mm", tn=1024)
    x1, hf0 = _glu_res_norm(vg, x, small["ffn_norm0"], "glu_res_norm")
    gu0, a0, o0 = _ffn_fwd(hf0, weight, 0, "ffn0")
    x2 = _to_slabs([o0, x1], "to_slabs")
    kvn, h1 = _norm_bf16(x2, [small["kv_norm"], small["b_norm"]], "att_norms")
    kv = _mm_nn(kvn, weight("w_kv", kvn), BF16, "kv_mm", tn=1024)
    q = _mm_nn(h1, weight("w_q", kv), BF16, "q_mm", tn=1536)
    outs, lses = [], []
    for gi, (window, dil) in enumerate(PATTERNS):
        assert window // dil == ATT_BLK
        og, lg = _att_fwd(q, kv, gi, dil, nh, f"att_fwd{gi}")
        outs.append(og)
        lses.append(lg)
    oatt, oattb = _att_combine(outs, lses, "att_combine")
    ao = _mm_nn(oattb, weight("w_o", oattb), F32, "o_mm", tn=1024)
    x3, hf1 = _res_norm(ao, x2, [small["ffn_norm1"]], "att_res_norm")
    gu1, a1, o1 = _ffn_fwd(hf1, weight, 1, "ffn1")
    dres, dresb, loss_rows, d_final = _loss_head(
        o1, x3, small["final_norm"], _to_slabs([target], "target_to_slabs"), "loss_head")

    dhf1, zero = _ffn_bwd(dresb, hf1, gu1, a1, weight, emit, 1, "ffn1")
    dres, dresb, d_ffn_norm1 = _norm_bwd(x3, [small["ffn_norm1"] + zero], [dhf1], dres, "ffn1_norm_bwd")
    zero = emit("w_o", _mm_tn(oattb, dresb, 1, "o_dw", tn=1024))
    doatt = _mm_nt(dresb, weight("w_o", None), "o_dx", tr=2048)
    dqs, dks, dvs = [], [], []
    for gi, (window, dil) in enumerate(PATTERNS):
        dq_g, dk_g, dv_g = _att_bwd(q, kv, doatt, oatt, lses, gi, dil, nh, f"att_bwd{gi}")
        dqs.append(dq_g)
        dks.append(dk_g)
        dvs.append(dv_g)
    dq = jnp.concatenate(dqs, axis=1)
    dkv = _dkv_sum(dks, dvs, "dkv_sum")
    w_q = weight("w_q", None)
    zero = zero + emit("w_q", _mm_tn(h1, dq, w_q.shape[0], "q_dw", tn=1536))
    zero = zero + emit("w_kv", _mm_tn(kvn, dkv, 1, "kv_dw", tn=1024))
    dh1 = _mm_nt(dq, w_q, "q_dx", tr=1536)
    dkvn = _mm_nt(dkv, weight("w_kv", None), "kv_dx", tr=1024)
    dres, dresb, d_b_norm, d_kv_norm = _norm_bwd(
        x2, [small["b_norm"] + zero, small["kv_norm"]], [dh1, dkvn], dres, "att_norm_bwd")
    dres, dresb = _from_slabs(dres, "from_slabs")
    dhf0, zero = _ffn_bwd(dresb, hf0, gu0, a0, weight, emit, 0, "ffn0")
    dres, dresb, d_ffn_norm0 = _norm_bwd(x1, [small["ffn_norm0"] + zero], [dhf0], dres, "ffn0_norm_bwd")
    del dresb
    dvg = _glu_bwd(dres, vg, "glu_bwd")
    w_glu = weight("w_glu", None)
    zero = emit("w_glu", _mm_tn(z, dvg, w_glu.shape[0], "glu_dw", tn=1024))
    dz = _mm_nt(dvg, w_glu, "glu_dx", tr=1024)
    dh0, d_s5_d, dcr, dci_neg, dbr, dbi, dar, dai = _s5_bwd(
        dz, y0, h0, xs_re, xs_im, bbd_re, bbd_im, cbd_re, cbd_im, pw_re, pw_im, pf_re, pf_im,
        small["s5_d"] + zero, "s5_bwd")
    grad_x, _, d_a_norm = _norm_bwd(x, [small["a_norm"]], [dh0], dres, "s5_norm_bwd")

    d_bb_re = _block_diag_take(dbr, c_, p_).transpose(0, 2, 1)
    d_bb_im = _block_diag_take(dbi, c_, p_).transpose(0, 2, 1)
    d_c_re = _block_diag_take(dcr, c_, p_)
    d_c_im = -_block_diag_take(dci_neg, c_, p_)
    d_lam_re, d_lam_im, d_log_dt, d_b_re, d_b_im = disc_vjp(
        (dar.reshape(n_groups, p_), dai.reshape(n_groups, p_), d_bb_re, d_bb_im))

    d_small = dict(lam_re=d_lam_re, lam_im=d_lam_im, log_dt=d_log_dt, b_re=d_b_re, b_im=d_b_im,
                   c_re=d_c_re, c_im=d_c_im, s5_d=d_s5_d, a_norm=d_a_norm, ffn_norm0=d_ffn_norm0,
                   ffn_norm1=d_ffn_norm1, b_norm=d_b_norm, kv_norm=d_kv_norm, final_norm=d_final)
    return loss_rows, grad_x, d_small


def _place():
    x, y, c = lax.axis_index("x"), lax.axis_index("y"), lax.axis_index("c")
    return x, y, c, [(1 - x, y), (x, 1 - y), (1 - x, 1 - y)]


_ANY = pl.BlockSpec(memory_space=pl.ANY)


_HBM = pl.BlockSpec(memory_space=pltpu.HBM)
_SEM = pl.BlockSpec(memory_space=pltpu.SEMAPHORE)
_EFFECT = pltpu.SideEffectType.DATAFLOW_SIDE_EFFECTING


def _in_hbm(a):
    return pltpu.with_memory_space_constraint(a, pltpu.HBM)


def _cast_place(shards, layer, chip, name):
    _, r, c = shards.shape
    tm = _tile(r, ROWS, 16)

    def kern(chip_ref, x_ref, o_ref):
        del chip_ref
        o_ref[...] = x_ref[...].astype(BF16)

    return pl.pallas_call(
        kern, name=name,
        grid_spec=pltpu.PrefetchScalarGridSpec(
            num_scalar_prefetch=1, grid=(r // tm,),
            in_specs=[pl.BlockSpec((None, tm, c), lambda i, ch: (layer, i, 0))],
            out_specs=pl.BlockSpec((None, tm, c), lambda i, ch: (ch[0], i, 0))),
        out_shape=jax.ShapeDtypeStruct((N_CHIPS, r, c), BF16),
        compiler_params=_params(("parallel",)),
    )(chip, shards)


def _gather_start(lands, name):
    n = len(lands)

    def body(*refs):
        land = refs[:n]
        send, recv = refs[n:2 * n], refs[2 * n:3 * n]
        x, y, c, peers = _place()
        me = 2 * x + y
        for a in range(n):
            for k, (px, py) in enumerate(peers):
                pltpu.make_async_remote_copy(
                    src_ref=land[a].at[me], dst_ref=land[a].at[me], send_sem=send[a].at[k], recv_sem=recv[a].at[k],
                    device_id=(px, py, c), device_id_type=MESH).start()

    outs = pl.pallas_call(
        body, name=name,
        out_shape=[pltpu.SemaphoreType.DMA((3,))] * (2 * n) + [pltpu.HBM(a.shape, a.dtype) for a in lands],
        in_specs=[_HBM] * n, out_specs=[_SEM] * (2 * n) + [_HBM] * n,
        input_output_aliases={i: 2 * n + i for i in range(n)},
        compiler_params=pltpu.CompilerParams(has_side_effects=_EFFECT),
    )(*[_in_hbm(a) for a in lands])
    return outs[:n], outs[n:2 * n], outs[2 * n:]


def _gather_wait(lands, sends, recvs, after, name):
    n = len(lands)

    def body(*refs):
        land, send, recv = refs[:n], refs[n:2 * n], refs[2 * n:3 * n]
        x, y, c, peers = _place()
        me = 2 * x + y
        for a in range(n):
            for k, (px, py) in enumerate(peers):
                cp = pltpu.make_async_remote_copy(
                    src_ref=land[a].at[me], dst_ref=land[a].at[2 * px + py], send_sem=send[a].at[k],
                    recv_sem=recv[a].at[k], device_id=(px, py, c), device_id_type=MESH)
                cp.wait_send()
                cp.wait_recv()

    return pl.pallas_call(
        body, name=name,
        out_shape=[pltpu.HBM(a.shape, a.dtype) for a in lands],
        in_specs=[_HBM] * n + [_SEM] * (2 * n) + [_ANY], out_specs=[_HBM] * n,
        input_output_aliases={i: i for i in range(n)},
        compiler_params=pltpu.CompilerParams(has_side_effects=_EFFECT),
    )(*lands, *sends, *recvs, after)


def _scatter_start(g, name):
    def body(g_ref, land_ref, send, recv, g_out, land_out, token):
        del g_out, land_out
        x, y, c, peers = _place()
        for k, (px, py) in enumerate(peers):
            pltpu.make_async_remote_copy(
                src_ref=g_ref.at[2 * px + py], dst_ref=land_ref.at[k], send_sem=send.at[k], recv_sem=recv.at[k],
                device_id=(px, py, c), device_id_type=MESH).start()
        token[...] = jnp.zeros_like(token)

    land = lax.empty((3,) + g.shape[1:], g.dtype)
    return pl.pallas_call(
        body, name=name,
        out_shape=(pltpu.SemaphoreType.DMA((3,)), pltpu.SemaphoreType.DMA((3,)),
                   pltpu.HBM(g.shape, g.dtype), pltpu.HBM(land.shape, land.dtype),
                   jax.ShapeDtypeStruct((8, 128), F32)),
        in_specs=(_HBM, _HBM), out_specs=(_SEM, _SEM, _HBM, _HBM, pl.BlockSpec(memory_space=pltpu.VMEM)),
        input_output_aliases={0: 2, 1: 3},
        compiler_params=pltpu.CompilerParams(has_side_effects=_EFFECT),
    )(_in_hbm(g), _in_hbm(land))


def _scatter_wait(started, after):
    n = len(started)

    def body(*refs):
        gs, lands = refs[:n], refs[n:2 * n]
        sends, recvs = refs[2 * n:3 * n], refs[3 * n:4 * n]
        x, y, c, peers = _place()
        for a in range(n):
            for k, (px, py) in enumerate(peers):
                cp = pltpu.make_async_remote_copy(
                    src_ref=gs[a].at[2 * px + py], dst_ref=lands[a].at[k], send_sem=sends[a].at[k],
                    recv_sem=recvs[a].at[k], device_id=(px, py, c), device_id_type=MESH)
                cp.wait_send()
                cp.wait_recv()

    gs = [s[2] for s in started]
    lands = [s[3] for s in started]
    outs = pl.pallas_call(
        body, name="scatter_wait",
        out_shape=[pltpu.HBM(a.shape, a.dtype) for a in gs + lands],
        in_specs=[_HBM] * (2 * n) + [_SEM] * (2 * n) + [_ANY], out_specs=[_HBM] * (2 * n),
        input_output_aliases={i: i for i in range(2 * n)},
        compiler_params=pltpu.CompilerParams(has_side_effects=_EFFECT),
    )(*gs, *lands, *[s[0] for s in started], *[s[1] for s in started], after)
    return outs[:n], outs[n:]


def _sibling():
    return lax.axis_index("x"), lax.axis_index("y"), 1 - lax.axis_index("c")


def _swap_start(parts):
    n = len(parts)

    def body(*refs):
        ins, land = refs[:n], refs[n:2 * n]
        send, recv = refs[2 * n], refs[2 * n + 1]
        token = refs[4 * n + 2]
        for a in range(n):
            pltpu.make_async_remote_copy(
                src_ref=ins[a], dst_ref=land[a], send_sem=send.at[a], recv_sem=recv.at[a],
                device_id=_sibling(), device_id_type=MESH).start()
        token[...] = jnp.zeros_like(token)

    lands = [lax.empty(a.shape, a.dtype) for a in parts]
    outs = pl.pallas_call(
        body, name="swap_start",
        out_shape=[pltpu.SemaphoreType.DMA((n,))] * 2 + [pltpu.HBM(a.shape, a.dtype) for a in parts + lands]
        + [jax.ShapeDtypeStruct((8, 128), F32)],
        in_specs=[_HBM] * (2 * n),
        out_specs=[_SEM] * 2 + [_HBM] * (2 * n) + [pl.BlockSpec(memory_space=pltpu.VMEM)],
        input_output_aliases={i: 2 + i for i in range(2 * n)},
        compiler_params=pltpu.CompilerParams(has_side_effects=_EFFECT),
    )(*[_in_hbm(a) for a in parts + lands])
    return outs[0], outs[1], outs[2:2 + n], outs[2 + n:2 + 2 * n], outs[2 + 2 * n]


def _swap_wait(parts, lands, send, recv, after):
    n = len(parts)

    def body(*refs):
        ins, land = refs[:n], refs[n:2 * n]
        send_sem, recv_sem = refs[2 * n], refs[2 * n + 1]
        for a in range(n):
            cp = pltpu.make_async_remote_copy(
                src_ref=ins[a], dst_ref=land[a], send_sem=send_sem.at[a], recv_sem=recv_sem.at[a],
                device_id=_sibling(), device_id_type=MESH)
            cp.wait_send()
            cp.wait_recv()

    outs = pl.pallas_call(
        body, name="swap_wait",
        out_shape=[pltpu.HBM(a.shape, a.dtype) for a in list(parts) + list(lands)],
        in_specs=[_HBM] * (2 * n) + [_SEM] * 2 + [_ANY], out_specs=[_HBM] * (2 * n),
        input_output_aliases={i: i for i in range(2 * n)},
        compiler_params=pltpu.CompilerParams(has_side_effects=_EFFECT),
    )(*parts, *lands, send, recv, after)
    return outs[:n], outs[n:]


def _all_reduce_small(v):
    nd, r, w = v.shape
    assert nd == N_DEV

    def body(v_ref, out_ref, land_ref, red_ref, send1, recv1, send2, recv2):
        x, y, c = lax.axis_index("x"), lax.axis_index("y"), lax.axis_index("c")
        me = 4 * x + 2 * y + c
        peers = []
        for k in range(1, N_DEV):
            kx, ky, kc = (k >> 2) & 1, (k >> 1) & 1, k & 1
            peers.append((1 - x if kx else x, 1 - y if ky else y, 1 - c if kc else c))
        first = []
        for k, (px, py, pc) in enumerate(peers):
            cp = pltpu.make_async_remote_copy(
                src_ref=v_ref.at[4 * px + 2 * py + pc], dst_ref=land_ref.at[me], send_sem=send1.at[k],
                recv_sem=recv1.at[k], device_id=(px, py, pc), device_id_type=MESH)
            cp.start()
            first.append(cp)
        land_ref[me] = v_ref[me]
        for cp in first:
            cp.wait()
        acc = land_ref[0]
        for j in range(1, N_DEV):
            acc = acc + land_ref[j]
        red_ref[...] = acc
        second = []
        for k, (px, py, pc) in enumerate(peers):
            cp = pltpu.make_async_remote_copy(
                src_ref=red_ref, dst_ref=out_ref.at[me], send_sem=send2.at[k],
                recv_sem=recv2.at[k], device_id=(px, py, pc), device_id_type=MESH)
            cp.start()
            second.append(cp)
        out_ref[me] = acc
        for cp in second:
            cp.wait()

    vmem = pl.BlockSpec(memory_space=pltpu.VMEM)
    return pl.pallas_call(
        body, name="all_reduce_small",
        in_specs=[vmem], out_specs=vmem,
        out_shape=jax.ShapeDtypeStruct((nd, r, w), F32),
        scratch_shapes=[pltpu.VMEM((nd, r, w), F32), pltpu.VMEM((r, w), F32)]
        + [pltpu.SemaphoreType.DMA((N_DEV - 1,))] * 4,
        compiler_params=pltpu.CompilerParams(vmem_limit_bytes=VMEM_LIMIT),
    )(v)


def _adam_math(w, g, m, v):
    m = ADAM_B1 * m + (1.0 - ADAM_B1) * g
    v = ADAM_B2 * v + (1.0 - ADAM_B2) * (g * g)
    m_hat = m / (1.0 - ADAM_B1 ** ADAM_STEP)
    v_hat = v / (1.0 - ADAM_B2 ** ADAM_STEP)
    delta = -ADAM_LR * (m_hat / (jnp.sqrt(v_hat) + ADAM_EPS) + ADAM_WD * w)
    return delta, m, v


def _sum_blocks(own, got, chip, name):
    _, r, c = own.shape
    tm = _tile(r, ROWS, 16)

    def kern(chip_ref, own_ref, got_ref, o_ref):
        del chip_ref
        acc = own_ref[...].astype(F32)
        for k in range(3):
            acc = acc + got_ref[k].astype(F32)
        o_ref[...] = acc

    return pl.pallas_call(
        kern, name=name,
        grid_spec=pltpu.PrefetchScalarGridSpec(
            num_scalar_prefetch=1, grid=(r // tm,),
            in_specs=[pl.BlockSpec((None, tm, c), lambda i, ch: (ch[0], i, 0)),
                      pl.BlockSpec((3, tm, c), lambda i, ch: (0, i, 0))],
            out_specs=pl.BlockSpec((tm, c), lambda i, ch: (i, 0))),
        out_shape=jax.ShapeDtypeStruct((r, c), F32),
        compiler_params=_params(("parallel",)),
    )(chip, own, got)


def _adamw(parts, w, m, v, name):
    nl, r, c = w.shape
    assert len(parts) == nl
    tm = _tile(r, 128, 8)

    def kern(*refs):
        p_refs = refs[:2 * nl]
        w_ref, m_ref, v_ref, g_ref, d_ref, mo_ref, vo_ref = refs[2 * nl:]
        layer = pl.program_id(0)
        g = p_refs[0][...] + p_refs[1][...]
        for ll in range(1, nl):
            g = jnp.where(layer == ll, p_refs[2 * ll][...] + p_refs[2 * ll + 1][...], g)
        g_ref[...] = g
        d_ref[...], mo_ref[...], vo_ref[...] = _adam_math(w_ref[...], g, m_ref[...], v_ref[...])

    def part_spec(ll):
        return pl.BlockSpec((tm, c), lambda l, i: (jnp.where(l == ll, i, 0), 0))

    spec = pl.BlockSpec((None, tm, c), lambda l, i: (l, i, 0))
    return pl.pallas_call(
        kern, name=name, grid=(nl, r // tm),
        in_specs=[part_spec(ll) for ll in range(nl) for _ in range(2)] + [spec] * 3, out_specs=[spec] * 4,
        out_shape=[jax.ShapeDtypeStruct((nl, r, c), F32)] * 4,
        compiler_params=_params(("arbitrary", "parallel")),
    )(*[p for pair in parts for p in pair], w, m, v)


def _adamw_small(g, w, m, v, name):
    def kern(g_ref, w_ref, m_ref, v_ref, d_ref, mo_ref, vo_ref):
        d_ref[...], mo_ref[...], vo_ref[...] = _adam_math(w_ref[...], g_ref[...], m_ref[...], v_ref[...])

    return pl.pallas_call(
        kern, name=name,
        out_shape=[jax.ShapeDtypeStruct(g.shape, F32)] * 3,
        compiler_params=pltpu.CompilerParams(vmem_limit_bytes=VMEM_LIMIT),
    )(g, w, m, v)


def _pack(arrays, rows):
    flat = jnp.concatenate([a.reshape(-1).astype(F32) for a in arrays])
    return jnp.pad(flat, (0, rows * 128 - flat.shape[0])).reshape(rows, 128)


def _unpack(packed, shapes):
    flat = packed.reshape(-1)
    out, off = [], 0
    for shp in shapes:
        size = math.prod(shp)
        out.append(flat[off:off + size].reshape(shp))
        off += size
    return out


_REPLICATED = ["s5_lam_re", "s5_lam_im", "s5_log_dt", "s5_b_re", "s5_b_im", "s5_c_re", "s5_c_im",
               "ffn_norm", "b_norm_mix", "kv_norm", "final_norm"]
_CHIP_VECTORS = ["s5_d", "a_norm_mix"]
_BIG = ["s5_w_glu", "ffn_w_in", "ffn_w_out", "attn_w_q", "attn_w_o", "w_kv"]
_WEIGHT_ORDER = ["s5_lam_re", "s5_lam_im", "s5_log_dt", "s5_b_re", "s5_b_im", "s5_c_re", "s5_c_im", "s5_d",
                 "s5_w_glu", "a_norm_mix", "ffn_norm", "ffn_w_in", "ffn_w_out", "b_norm_mix", "attn_w_q",
                 "attn_w_o", "kv_norm", "w_kv", "final_norm"]


def _step(x, loss_target, w, m, v):
    s, d = x.shape[1], x.shape[2]
    chip = 2 * lax.axis_index("x") + lax.axis_index("y")

    col_sharded = dict(w_glu=("s5_w_glu", 0), w_in0=("ffn_w_in", 0), w_in1=("ffn_w_in", 1), w_q=("attn_w_q", 0))
    row_sharded = dict(w_out0=("ffn_w_out", 0), w_out1=("ffn_w_out", 1), w_o=("attn_w_o", 0), w_kv=("w_kv", 0))
    names = ["w_glu", "w_in0", "w_out0", "w_kv", "w_q", "w_o", "w_in1", "w_out1"]
    local = {**col_sharded, **row_sharded}

    def layers(a):
        return a.reshape((-1,) + a.shape[-2:])

    chip_arr = jnp.reshape(chip, (1,)).astype(jnp.int32)
    vector_lands = [lax.dynamic_update_slice(jnp.zeros((N_CHIPS,) + w[n].shape, F32), w[n][None], (chip, 0, 0))
                    for n in _CHIP_VECTORS]
    first = 2
    lands = [_cast_place(layers(w[local[n][0]]), local[n][1], chip_arr, f"cast_{n}") for n in names[:first]]
    send, recv, land_thru = _gather_start(vector_lands + lands, "gather_start_first")
    lands = [_cast_place(layers(w[local[n][0]]), local[n][1], chip_arr, f"cast_{n}") for n in names[first:]]
    later = _gather_start(lands, "gather_start_rest")
    send, recv, land_thru = [list(a) + list(b) for a, b in zip((send, recv, land_thru), later)]
    vectors = _gather_wait(land_thru[:2], send[:2], recv[:2], jnp.zeros((8, 128), F32), "gather_wait_vectors")
    s5_d_full = vectors[0].reshape(1, d)
    a_norm_full = vectors[1].reshape(1, d)
    arrived = {}

    def weight(name, after):
        if name not in arrived:
            i = 2 + names.index(name)
            land, = _gather_wait([land_thru[i]], [send[i]], [recv[i]], after, f"gather_wait_{name}")
            arrived[name] = land if name in col_sharded else land.reshape(1, -1, land.shape[-1])
        return arrived[name]

    started = {}

    def emit(name, dw):
        outs = _scatter_start(dw.reshape(N_CHIPS, -1, dw.shape[-1]), f"scatter_start_{name}")
        started[name] = outs[:4]
        return outs[4][0, 0]

    small = dict(lam_re=w["s5_lam_re"][0], lam_im=w["s5_lam_im"][0], log_dt=w["s5_log_dt"][0],
                 b_re=w["s5_b_re"][0], b_im=w["s5_b_im"][0], c_re=w["s5_c_re"][0], c_im=w["s5_c_im"][0],
                 s5_d=s5_d_full, a_norm=a_norm_full, ffn_norm0=w["ffn_norm"][0:1], ffn_norm1=w["ffn_norm"][1:2],
                 b_norm=w["b_norm_mix"], kv_norm=w["kv_norm"].reshape(1, d), final_norm=w["final_norm"].reshape(1, d))

    loss_rows, grad_x, d_small = _local_step(x[0], loss_target[0], small, weight, emit)

    g4, got = _scatter_wait([started[n] for n in names], grad_x)
    partial = [_sum_blocks(o, r, chip_arr, f"sum_{n}") for n, o, r in zip(names, g4, got)]
    swap_send, swap_recv, partial, swap_land, swap_token = _swap_start(partial)
    result = {}

    rep_grads = [d_small["lam_re"], d_small["lam_im"], d_small["log_dt"], d_small["b_re"], d_small["b_im"],
                 d_small["c_re"], d_small["c_im"],
                 jnp.concatenate([d_small["ffn_norm0"], d_small["ffn_norm1"]], axis=0),
                 d_small["b_norm"], d_small["kv_norm"], d_small["final_norm"]]
    vec_grads = [d_small["s5_d"], d_small["a_norm"]]
    to_reduce = rep_grads + vec_grads + [jnp.sum(loss_rows).reshape(1) + swap_token[0, 0]]
    total = sum(math.prod(a.shape) for a in to_reduce)
    rows_per = -(-total // (N_DEV * 128 * 8)) * 8
    reduced = _all_reduce_small(_pack(to_reduce, N_DEV * rows_per).reshape(N_DEV, rows_per, 128))
    red = _unpack(reduced, [a.shape for a in to_reduce])
    loss = red[-1][0]
    g_small = dict(zip(_REPLICATED, [r.reshape(w[n].shape) for r, n in zip(red[:len(rep_grads)], _REPLICATED)]))
    for n, r in zip(_CHIP_VECTORS, red[len(rep_grads):-1]):
        g_small[n] = lax.dynamic_slice_in_dim(r.reshape(1, d), chip * (d // N_CHIPS), d // N_CHIPS, axis=1)
    small_names = _REPLICATED + _CHIP_VECTORS
    n_small = sum(math.prod(w[n].shape) for n in small_names)
    rows_small = -(-n_small // (128 * 8)) * 8
    packed = [_pack([src[n] for n in small_names], rows_small) for src in (g_small, w, m, v)]
    upd = _adamw_small(*packed, "adamw_small")
    shapes = [w[n].shape for n in small_names]
    for n, dl, mo, vo in zip(small_names, *[_unpack(u, shapes) for u in upd]):
        result[n] = [g_small[n], dl, mo, vo]

    partial, other = _swap_wait(partial, swap_land, swap_send, swap_recv, upd[0])
    part = dict(zip(names, zip(partial, other)))
    for name in _BIG:
        parts = [part[n] for n in sorted((n for n in names if local[n][0] == name), key=lambda n: local[n][1])]
        outs = _adamw(parts, layers(w[name]), layers(m[name]), layers(v[name]), f"adamw_{name}")
        result[name] = [o.reshape(w[name].shape) for o in outs]

    cols = [[result[n][t] for n in _WEIGHT_ORDER] for t in range(4)]
    return (loss, grad_x.reshape(x.shape), *cols[0], *cols[1], *cols[2], *cols[3])


def kernel(x, s5_lam_re, s5_lam_im, s5_log_dt, s5_b_re, s5_b_im, s5_c_re, s5_c_im, s5_d, s5_w_glu, a_norm_mix, ffn_norm, ffn_w_in, ffn_w_out, b_norm_mix, attn_w_q, attn_w_o, kv_norm, w_kv, final_norm, loss_target, m_s5_lam_re, m_s5_lam_im, m_s5_log_dt, m_s5_b_re, m_s5_b_im, m_s5_c_re, m_s5_c_im, m_s5_d, m_s5_w_glu, m_a_norm_mix, m_ffn_norm, m_ffn_w_in, m_ffn_w_out, m_b_norm_mix, m_attn_w_q, m_attn_w_o, m_kv_norm, m_w_kv, m_final_norm, v_s5_lam_re, v_s5_lam_im, v_s5_log_dt, v_s5_b_re, v_s5_b_im, v_s5_c_re, v_s5_c_im, v_s5_d, v_s5_w_glu, v_a_norm_mix, v_ffn_norm, v_ffn_w_in, v_ffn_w_out, v_b_norm_mix, v_attn_w_q, v_attn_w_o, v_kv_norm, v_w_kv, v_final_norm):
    w = dict(zip(_WEIGHT_ORDER, (s5_lam_re, s5_lam_im, s5_log_dt, s5_b_re, s5_b_im, s5_c_re, s5_c_im, s5_d, s5_w_glu, a_norm_mix, ffn_norm, ffn_w_in, ffn_w_out, b_norm_mix, attn_w_q, attn_w_o, kv_norm, w_kv, final_norm)))
    m = dict(zip(_WEIGHT_ORDER, (m_s5_lam_re, m_s5_lam_im, m_s5_log_dt, m_s5_b_re, m_s5_b_im, m_s5_c_re, m_s5_c_im, m_s5_d, m_s5_w_glu, m_a_norm_mix, m_ffn_norm, m_ffn_w_in, m_ffn_w_out, m_b_norm_mix, m_attn_w_q, m_attn_w_o, m_kv_norm, m_w_kv, m_final_norm)))
    v = dict(zip(_WEIGHT_ORDER, (v_s5_lam_re, v_s5_lam_im, v_s5_log_dt, v_s5_b_re, v_s5_b_im, v_s5_c_re, v_s5_c_im, v_s5_d, v_s5_w_glu, v_a_norm_mix, v_ffn_norm, v_ffn_w_in, v_ffn_w_out, v_b_norm_mix, v_attn_w_q, v_attn_w_o, v_kv_norm, v_w_kv, v_final_norm)))
    return _step(x, loss_target, w, m, v)
```

```python
import functools
import math

import jax
import jax.numpy as jnp
from jax import lax
from jax.experimental import pallas as pl
from jax.experimental.pallas import tpu as pltpu

F32 = jnp.float32
BF16 = jnp.bfloat16

S5_GROUP_CH = 16
S5_STATE = 64
GROUPS_PER_TILE = 8
HEAD_DIM = 128
N_KV_HEADS = 4
PATTERNS = ((128, 1), (512, 4), (2048, 16))
ATT_BLK = 128
EPS = 1e-6
NEG_INF = -1e30
SCAN_T = 256
ADAM_LR = 0.001
ADAM_B1 = 0.9
ADAM_B2 = 0.999
ADAM_EPS = 1e-08
ADAM_WD = 0.01
ADAM_STEP = 10
N_CHIPS = 4
N_DEV = 8
VMEM_LIMIT = 56 * 1024 * 1024
MESH = pl.DeviceIdType.MESH
GELU_K = math.sqrt(2.0 / math.pi)
GELU_C = 0.044715


def _tile(n, pref, unit=128):
    if n <= pref:
        return n
    best = None
    t = unit
    while t <= pref:
        if n % t == 0:
            best = t
        t += unit
    assert best is not None, (n, pref, unit)
    return best


def _params(sem):
    return pltpu.CompilerParams(dimension_semantics=sem, vmem_limit_bytes=VMEM_LIMIT)


def _dot(a, b):
    return jnp.dot(a, b, preferred_element_type=F32)


def _dot_nt(a, b):
    return lax.dot_general(a, b, (((1,), (1,)), ((), ())), preferred_element_type=F32)


def _dot_tn(a, b):
    return lax.dot_general(a, b, (((0,), (0,)), ((), ())), preferred_element_type=F32)


def _mm_nn(a, w, out_dtype, name, tm=512, tn=1536, tk=2048):
    m, k = a.shape
    nb, k2, nq = w.shape
    assert k == k2
    tm, tn, tk = _tile(m, tm, 8), _tile(nq, tn), _tile(k, tk)
    per, nk = nq // tn, k // tk

    def kern(a_ref, w_ref, o_ref, *acc):
        p = _dot(a_ref[...], w_ref[...])
        if nk == 1:
            o_ref[...] = p.astype(o_ref.dtype)
        else:
            acc_ref, = acc
            kk = pl.program_id(2)

            @pl.when(kk == 0)
            def _():
                acc_ref[...] = p

            @pl.when(kk > 0)
            def _():
                acc_ref[...] += p

            @pl.when(kk == nk - 1)
            def _():
                o_ref[...] = acc_ref[...].astype(o_ref.dtype)

    return pl.pallas_call(
        kern, name=name, grid=(nb * per, m // tm, nk),
        in_specs=[pl.BlockSpec((tm, tk), lambda j, i, kk: (i, kk)),
                  pl.BlockSpec((None, tk, tn), lambda j, i, kk: (j // per, kk, j % per))],
        out_specs=pl.BlockSpec((tm, tn), lambda j, i, kk: (i, j)),
        out_shape=jax.ShapeDtypeStruct((m, nb * nq), out_dtype),
        scratch_shapes=[] if nk == 1 else [pltpu.VMEM((tm, tn), F32)],
        compiler_params=_params(("parallel", "parallel", "arbitrary")),
    )(a, w)


def _paired_block(r, per, nb):
    j = r // 2
    return (r % 2) * (nb // 2) + j // per, j % per


def _mm_nt(a, w, name, tm=512, tr=1536, tkc=1024, paired=False):
    m, n = a.shape
    nb, k, nq = w.shape
    assert n == nb * nq
    tm, tr, tkc = _tile(m, tm, 8), _tile(nq, tr), _tile(k, tkc)
    per = nq // tr

    def kern(a_ref, w_ref, o_ref):
        acc = None
        for r in range(nb * per):
            blk, tile = _paired_block(r, per, nb) if paired else (r // per, r % per)
            p = _dot_nt(a_ref[:, r * tr:(r + 1) * tr], w_ref[blk, :, tile * tr:(tile + 1) * tr])
            acc = p if acc is None else acc + p
        o_ref[...] = acc

    return pl.pallas_call(
        kern, name=name, grid=(k // tkc, m // tm),
        in_specs=[pl.BlockSpec((tm, n), lambda kc, i: (i, 0)),
                  pl.BlockSpec((nb, tkc, nq), lambda kc, i: (0, kc, 0))],
        out_specs=pl.BlockSpec((tm, tkc), lambda kc, i: (i, kc)),
        out_shape=jax.ShapeDtypeStruct((m, k), F32),
        compiler_params=_params(("parallel", "parallel")),
    )(a, w)


def _mm_tn(a, dy, nb, name, tkk=512, tn=1536, paired=False):
    s, k = a.shape
    s2, n = dy.shape
    assert s == s2 and n % nb == 0
    nq = n // nb
    tkk, tn = _tile(k, tkk), _tile(nq, tn)
    per = nq // tn

    def w_block(j):
        return _paired_block(j, per, nb) if paired else (j // per, j % per)

    def kern(a_ref, dy_ref, o_ref):
        o_ref[...] = _dot_tn(a_ref[...], dy_ref[...]).astype(o_ref.dtype)

    return pl.pallas_call(
        kern, name=name, grid=(nb * per, k // tkk),
        in_specs=[pl.BlockSpec((s, tkk), lambda j, kk: (0, kk)),
                  pl.BlockSpec((s, tn), lambda j, kk: (0, j))],
        out_specs=pl.BlockSpec((None, tkk, tn), lambda j, kk: (w_block(j)[0], kk, w_block(j)[1])),
        out_shape=jax.ShapeDtypeStruct((nb, k, nq), BF16),
        compiler_params=_params(("parallel", "parallel")),
    )(a, dy)


ROWS = 256


def _rms(x, g):
    r = lax.rsqrt(jnp.mean(x * x, axis=-1, keepdims=True) + EPS)
    return x * r * g


def _rms_bwd(x, g, dh):
    r = lax.rsqrt(jnp.mean(x * x, axis=-1, keepdims=True) + EPS)
    xh = x * r
    dgx = dh * g
    dx = r * (dgx - xh * jnp.mean(dgx * xh, axis=-1, keepdims=True))
    return dx, dh * xh


def _sigmoid(x):
    return 1.0 / (1.0 + jnp.exp(-x))


def _gelu(y):
    return 0.5 * y * (1.0 + jnp.tanh(GELU_K * (y + GELU_C * y * y * y)))


def _gelu_grad(y):
    t = jnp.tanh(GELU_K * (y + GELU_C * y * y * y))
    return 0.5 * (1.0 + t) + 0.5 * y * (1.0 - t * t) * GELU_K * (1.0 + 3.0 * GELU_C * y * y)


def _row_spec(tm, d, col=0):
    return pl.BlockSpec((tm, d), lambda i: (i, col))


def _vec_spec(d):
    return pl.BlockSpec((1, d), lambda i: (0, 0))


def _acc_rows(ref, val, i):
    s = jnp.sum(val, axis=0, keepdims=True)

    @pl.when(i == 0)
    def _():
        ref[...] = s

    @pl.when(i > 0)
    def _():
        ref[...] += s


def _norm_f32(x, g, name):
    s, d = x.shape
    tm = _tile(s, ROWS, 8)

    def kern(x_ref, g_ref, h_ref):
        h_ref[...] = _rms(x_ref[...], g_ref[...])

    return pl.pallas_call(
        kern, name=name, grid=(s // tm,),
        in_specs=[_row_spec(tm, d), _vec_spec(d)],
        out_specs=_row_spec(tm, d),
        out_shape=jax.ShapeDtypeStruct((s, d), F32),
        compiler_params=_params(("parallel",)),
    )(x, g)


def _glu_res_norm(vg, x, g, name):
    s, d = x.shape
    tm = _tile(s, ROWS, 16)

    def kern(val_ref, gate_ref, x_ref, g_ref, x1_ref, hf_ref):
        x1 = x_ref[...] + val_ref[...] * _sigmoid(gate_ref[...])
        x1_ref[...] = x1
        hf_ref[...] = _rms(x1, g_ref[...]).astype(BF16)

    return pl.pallas_call(
        kern, name=name, grid=(s // tm,),
        in_specs=[_row_spec(tm, d, 0), _row_spec(tm, d, 1), _row_spec(tm, d), _vec_spec(d)],
        out_specs=[_row_spec(tm, d), _row_spec(tm, d)],
        out_shape=[jax.ShapeDtypeStruct((s, d), F32), jax.ShapeDtypeStruct((s, d), BF16)],
        compiler_params=_params(("parallel",)),
    )(vg, vg, x, g)


FFN_TILE = 1408


def _ffn_in_act(hf, w_in, name, tm=512):
    s, k = hf.shape
    nb, _, nq = w_in.shape
    tm, tn = _tile(s, tm, 16), _tile(nq, FFN_TILE)
    per = nq // tn
    nf = (nb // 2) * per

    def kern(h_ref, wg_ref, wu_ref, a_ref, gu_ref):
        h = h_ref[...]
        g = _dot(h, wg_ref[...])
        u = _dot(h, wu_ref[...])
        a_ref[...] = (g * _sigmoid(g) * u).astype(BF16)
        gu_ref[:, :tn] = g.astype(BF16)
        gu_ref[:, tn:] = u.astype(BF16)

    return pl.pallas_call(
        kern, name=name, grid=(nf, s // tm),
        in_specs=[pl.BlockSpec((tm, k), lambda j, i: (i, 0)),
                  pl.BlockSpec((None, k, tn), lambda j, i: (j // per, 0, j % per)),
                  pl.BlockSpec((None, k, tn), lambda j, i: (nb // 2 + j // per, 0, j % per))],
        out_specs=[pl.BlockSpec((tm, tn), lambda j, i: (i, j)),
                   pl.BlockSpec((tm, 2 * tn), lambda j, i: (i, j))],
        out_shape=[jax.ShapeDtypeStruct((s, nf * tn), BF16), jax.ShapeDtypeStruct((s, 2 * nf * tn), BF16)],
        compiler_params=_params(("parallel", "parallel")),
    )(hf, w_in, w_in)


def _ffn_out_dx_act(dresb, w_out, gu, name, tm=512):
    s, d = dresb.shape
    f = w_out.shape[1]
    tm = _tile(s, tm, 16)
    tn = _tile(f // 2, FFN_TILE)

    def kern(d_ref, w_ref, gu_ref, o_ref):
        da = _dot_nt(d_ref[...], w_ref[...])
        g = gu_ref[:, :tn].astype(F32)
        u = gu_ref[:, tn:].astype(F32)
        sg = _sigmoid(g)
        o_ref[:, :tn] = (da * u * sg * (1.0 + g * (1.0 - sg))).astype(BF16)
        o_ref[:, tn:] = (da * g * sg).astype(BF16)

    pair = pl.BlockSpec((tm, 2 * tn), lambda j, i: (i, j))
    return pl.pallas_call(
        kern, name=name, grid=(f // tn, s // tm),
        in_specs=[pl.BlockSpec((tm, d), lambda j, i: (i, 0)),
                  pl.BlockSpec((None, tn, d), lambda j, i: (0, j, 0)), pair],
        out_specs=pair,
        out_shape=jax.ShapeDtypeStruct((s, 2 * f), BF16),
        compiler_params=_params(("parallel", "parallel")),
    )(dresb, w_out, gu)


def _res_norm(o, x, gains, name):
    s, d = x.shape
    tm = _tile(s, ROWS, 16)
    ng = len(gains)

    def kern(o_ref, x_ref, *refs):
        xn = x_ref[...] + o_ref[...]
        refs[ng][...] = xn
        for t in range(ng):
            refs[ng + 1 + t][...] = _rms(xn, refs[t][...]).astype(BF16)

    return pl.pallas_call(
        kern, name=name, grid=(s // tm,),
        in_specs=[_row_spec(tm, d), _row_spec(tm, d)] + [_vec_spec(d)] * ng,
        out_specs=[_row_spec(tm, d)] * (1 + ng),
        out_shape=[jax.ShapeDtypeStruct((s, d), F32)] + [jax.ShapeDtypeStruct((s, d), BF16)] * ng,
        compiler_params=_params(("parallel",)),
    )(o, x, *gains)


def _loss_head(o, x, g, target, name):
    s, d = x.shape
    tm = _tile(s, ROWS, 16)

    def kern(o_ref, x_ref, g_ref, t_ref, dx_ref, dxb_ref, loss_ref, dg_ref):
        i = pl.program_id(0)
        x4 = x_ref[...] + o_ref[...]
        gg = g_ref[...]
        diff = _rms(x4, gg) - t_ref[...]
        dx, dgr = _rms_bwd(x4, gg, diff * (1.0 / d))
        dx_ref[...] = dx
        dxb_ref[...] = dx.astype(BF16)
        _acc_rows(loss_ref, diff * diff * (0.5 / d), i)
        _acc_rows(dg_ref, dgr, i)

    return pl.pallas_call(
        kern, name=name, grid=(s // tm,),
        in_specs=[_row_spec(tm, d), _row_spec(tm, d), _vec_spec(d), _row_spec(tm, d)],
        out_specs=[_row_spec(tm, d), _row_spec(tm, d), _vec_spec(d), _vec_spec(d)],
        out_shape=[jax.ShapeDtypeStruct((s, d), F32), jax.ShapeDtypeStruct((s, d), BF16),
                   jax.ShapeDtypeStruct((1, d), F32), jax.ShapeDtypeStruct((1, d), F32)],
        compiler_params=_params(("arbitrary",)),
    )(o, x, g, target)


def _norm_bwd(x, gains, dhs, dres, name):
    s, d = x.shape
    tm = _tile(s, ROWS, 16)
    ng = len(gains)

    def kern(x_ref, dres_ref, *refs):
        i = pl.program_id(0)
        x_ = x_ref[...]
        acc = dres_ref[...]
        for t in range(ng):
            dx, dgr = _rms_bwd(x_, refs[t][...], refs[ng + t][...])
            acc = acc + dx
            _acc_rows(refs[2 * ng + 2 + t], dgr, i)
        refs[2 * ng][...] = acc
        refs[2 * ng + 1][...] = acc.astype(BF16)

    return pl.pallas_call(
        kern, name=name, grid=(s // tm,),
        in_specs=[_row_spec(tm, d), _row_spec(tm, d)] + [_vec_spec(d)] * ng + [_row_spec(tm, d)] * ng,
        out_specs=[_row_spec(tm, d), _row_spec(tm, d)] + [_vec_spec(d)] * ng,
        out_shape=[jax.ShapeDtypeStruct((s, d), F32), jax.ShapeDtypeStruct((s, d), BF16)]
        + [jax.ShapeDtypeStruct((1, d), F32)] * ng,
        compiler_params=_params(("arbitrary",)),
    )(x, dres, *gains, *dhs)


def _glu_bwd(dmix, vg, name):
    s, d = dmix.shape
    tm = _tile(s, ROWS, 16)

    def kern(dm_ref, val_ref, gate_ref, o_ref):
        dm = dm_ref[...]
        sg = _sigmoid(gate_ref[...])
        o_ref[:, :d] = (dm * sg).astype(BF16)
        o_ref[:, d:] = (dm * val_ref[...] * sg * (1.0 - sg)).astype(BF16)

    return pl.pallas_call(
        kern, name=name, grid=(s // tm,),
        in_specs=[_row_spec(tm, d), _row_spec(tm, d, 0), _row_spec(tm, d, 1)],
        out_specs=_row_spec(tm, 2 * d),
        out_shape=jax.ShapeDtypeStruct((s, 2 * d), BF16),
        compiler_params=_params(("parallel",)),
    )(dmix, vg, vg)


SUB = 8


def _local_scan(vr, vi, pre_ref, pim_ref, reverse):
    sub = lax.broadcasted_iota(jnp.int32, vr.shape, 1)
    sign = -1.0 if reverse else 1.0
    for sh in (1, 2, 4):
        ar = pre_ref[sh - 1:sh, :][None]
        ai = sign * pim_ref[sh - 1:sh, :][None]
        keep = sub < SUB - sh if reverse else sub >= sh
        sr = jnp.where(keep, pltpu.roll(vr, SUB - sh if reverse else sh, 1), 0.0)
        si = jnp.where(keep, pltpu.roll(vi, SUB - sh if reverse else sh, 1), 0.0)
        vr, vi = vr + ar * sr - ai * si, vi + ar * si + ai * sr
    return vr, vi


def _s5_fwd(u, bbd_re, bbd_im, cbd_re, cbd_im, pw_re, pw_im, dskip, name):
    s, d = u.shape
    nt, cw, lw = bbd_re.shape
    t = _tile(s, SCAN_T, SUB)
    nc = s // t
    ng = t // SUB

    def kern(u_ref, bre_ref, bim_ref, cre_ref, cim_ref, pre_ref, pim_ref, d_ref,
             xr_ref, xi_ref, y_ref, z_ref, car_ref, cai_ref):
        c = pl.program_id(1)

        @pl.when(c == 0)
        def _():
            car_ref[...] = jnp.zeros_like(car_ref)
            cai_ref[...] = jnp.zeros_like(cai_ref)

        u_ = u_ref[...]
        ub = u_.astype(BF16)
        vr, vi = _local_scan(_dot(ub, bre_ref[...]).reshape(ng, SUB, lw),
                             _dot(ub, bim_ref[...]).reshape(ng, SUB, lw), pre_ref, pim_ref, False)
        cr = car_ref[...]
        ci = cai_ref[...]
        pr = pre_ref[...]
        pi = pim_ref[...]
        for gidx in range(ng):
            rows = slice(gidx * SUB, (gidx + 1) * SUB)
            gr = vr[gidx] + pr * cr - pi * ci
            gi = vi[gidx] + pr * ci + pi * cr
            xr_ref[rows, :] = gr
            xi_ref[rows, :] = gi
            cr, ci = gr[SUB - 1:SUB, :], gi[SUB - 1:SUB, :]
        car_ref[...] = cr
        cai_ref[...] = ci
        y = (_dot(xr_ref[...].astype(BF16), cre_ref[...]) - _dot(xi_ref[...].astype(BF16), cim_ref[...])
             + d_ref[...] * u_)
        y_ref[...] = y
        z_ref[...] = _gelu(y).astype(BF16)

    tok = pl.BlockSpec((t, cw), lambda j, c: (c, j))
    st = pl.BlockSpec((t, lw), lambda j, c: (c, j))
    return pl.pallas_call(
        kern, name=name, grid=(nt, nc),
        in_specs=[tok,
                  pl.BlockSpec((None, cw, lw), lambda j, c: (j, 0, 0)),
                  pl.BlockSpec((None, cw, lw), lambda j, c: (j, 0, 0)),
                  pl.BlockSpec((None, lw, cw), lambda j, c: (j, 0, 0)),
                  pl.BlockSpec((None, lw, cw), lambda j, c: (j, 0, 0)),
                  pl.BlockSpec((SUB, lw), lambda j, c: (0, j)),
                  pl.BlockSpec((SUB, lw), lambda j, c: (0, j)),
                  pl.BlockSpec((1, cw), lambda j, c: (0, j))],
        out_specs=[st, st, tok, tok],
        out_shape=[jax.ShapeDtypeStruct((s, nt * lw), F32), jax.ShapeDtypeStruct((s, nt * lw), F32),
                   jax.ShapeDtypeStruct((s, d), F32), jax.ShapeDtypeStruct((s, d), BF16)],
        scratch_shapes=[pltpu.VMEM((1, lw), F32), pltpu.VMEM((1, lw), F32)],
        compiler_params=_params(("parallel", "arbitrary")),
    )(u, bbd_re, bbd_im, cbd_re, cbd_im, pw_re, pw_im, dskip)


def _s5_bwd(dz, y, u, xs_re, xs_im, bbd_re, bbd_im, cbd_re, cbd_im, pw_re, pw_im, pf_re, pf_im, dskip, name):
    s, d = u.shape
    nt, cw, lw = bbd_re.shape
    t = _tile(s, SCAN_T, SUB)
    nc = s // t
    ng = t // SUB

    def kern(dz_ref, y_ref, u_ref, xr_ref, xi_ref, bre_ref, bim_ref, cre_ref, cim_ref,
             pre_ref, pim_ref, fre_ref, fim_ref, d_ref,
             du_ref, dd_ref, dcr_ref, dci_ref, dbr_ref, dbi_ref, dar_ref, dai_ref,
             car_ref, cai_ref, gr_ref, gi_ref):
        c = pl.program_id(1)

        @pl.when(c == 0)
        def _():
            car_ref[...] = jnp.zeros_like(car_ref)
            cai_ref[...] = jnp.zeros_like(cai_ref)

        u_ = u_ref[...]
        ub = u_.astype(BF16)
        dy = dz_ref[...] * _gelu_grad(y_ref[...])
        dyb = dy.astype(BF16)
        vr, vi = _local_scan(_dot_nt(dyb, cre_ref[...]).reshape(ng, SUB, lw),
                             (-_dot_nt(dyb, cim_ref[...])).reshape(ng, SUB, lw), pre_ref, pim_ref, True)
        later_r = car_ref[...]
        later_i = cai_ref[...]
        cr, ci = later_r, later_i
        fr = fre_ref[...]
        fi = fim_ref[...]
        for gidx in reversed(range(ng)):
            rows = slice(gidx * SUB, (gidx + 1) * SUB)
            ar = vr[gidx] + fr * cr + fi * ci
            ai = vi[gidx] + fr * ci - fi * cr
            gr_ref[rows, :] = ar
            gi_ref[rows, :] = ai
            cr, ci = ar[0:1, :], ai[0:1, :]
        car_ref[...] = cr
        cai_ref[...] = ci
        gr = gr_ref[...]
        gi = gi_ref[...]
        rows = lax.broadcasted_iota(jnp.int32, (t, lw), 0)
        gsr = jnp.where(rows < t - 1, pltpu.roll(gr, t - 1, 0), later_r)
        gsi = jnp.where(rows < t - 1, pltpu.roll(gi, t - 1, 0), later_i)
        xr = xr_ref[...]
        xi = xi_ref[...]
        dar = jnp.sum(gsr * xr + gsi * xi, axis=0, keepdims=True)
        dai = jnp.sum(gsi * xr - gsr * xi, axis=0, keepdims=True)
        grb = gr.astype(BF16)
        gib = gi.astype(BF16)
        dbr = _dot_tn(ub, grb)
        dbi = _dot_tn(ub, gib)
        dcr = _dot_tn(dyb, xr.astype(BF16))
        dci = _dot_tn(dyb, xi.astype(BF16))
        du_ref[...] = dy * d_ref[...] + _dot_nt(grb, bre_ref[...]) + _dot_nt(gib, bim_ref[...])
        ddv = jnp.sum(dy * u_, axis=0, keepdims=True)

        @pl.when(c == 0)
        def _():
            dd_ref[...] = ddv
            dcr_ref[...] = dcr
            dci_ref[...] = dci
            dbr_ref[...] = dbr
            dbi_ref[...] = dbi
            dar_ref[...] = dar
            dai_ref[...] = dai

        @pl.when(c > 0)
        def _():
            dd_ref[...] += ddv
            dcr_ref[...] += dcr
            dci_ref[...] += dci
            dbr_ref[...] += dbr
            dbi_ref[...] += dbi
            dar_ref[...] += dar
            dai_ref[...] += dai

    tok = pl.BlockSpec((t, cw), lambda j, c: (nc - 1 - c, j))
    st = pl.BlockSpec((t, lw), lambda j, c: (nc - 1 - c, j))
    wb = pl.BlockSpec((None, cw, lw), lambda j, c: (j, 0, 0))
    wc = pl.BlockSpec((None, lw, cw), lambda j, c: (j, 0, 0))
    pw = pl.BlockSpec((SUB, lw), lambda j, c: (0, j))
    vec_c = pl.BlockSpec((1, cw), lambda j, c: (0, j))
    vec_l = pl.BlockSpec((1, lw), lambda j, c: (0, j))
    return pl.pallas_call(
        kern, name=name, grid=(nt, nc),
        in_specs=[tok, tok, tok, st, st, wb, wb, wc, wc, pw, pw, pw, pw, vec_c],
        out_specs=[tok, vec_c, wb, wb, wb, wb, vec_l, vec_l],
        out_shape=[jax.ShapeDtypeStruct((s, d), F32), jax.ShapeDtypeStruct((1, d), F32)]
        + [jax.ShapeDtypeStruct((nt, cw, lw), F32)] * 4
        + [jax.ShapeDtypeStruct((1, nt * lw), F32)] * 2,
        scratch_shapes=[pltpu.VMEM((1, lw), F32), pltpu.VMEM((1, lw), F32),
                        pltpu.VMEM((t, lw), F32), pltpu.VMEM((t, lw), F32)],
        compiler_params=_params(("parallel", "arbitrary")),
    )(dz, y, u, xs_re, xs_im, bbd_re, bbd_im, cbd_re, cbd_im, pw_re, pw_im, pf_re, pf_im, dskip)


def _s5_discretize(lam_re, lam_im, log_dt, b_re, b_im):
    dt = jnp.exp(log_dt)[:, None]
    mag = jnp.exp(lam_re * dt)
    ang = lam_im * dt
    lb_re = mag * jnp.cos(ang)
    lb_im = mag * jnp.sin(ang)
    nr = lb_re - 1.0
    den = lam_re * lam_re + lam_im * lam_im
    f_re = (nr * lam_re + lb_im * lam_im) / den
    f_im = (lb_im * lam_re - nr * lam_im) / den
    bb_re = f_re[..., None] * b_re - f_im[..., None] * b_im
    bb_im = f_re[..., None] * b_im + f_im[..., None] * b_re
    return lb_re, lb_im, bb_re, bb_im


def _block_diag(w):
    nt, ng, a, b = w.shape
    eye = jnp.eye(ng, dtype=w.dtype)
    return (w[:, :, :, None, :] * eye[None, :, None, :, None]).reshape(nt, ng * a, ng * b)


def _block_diag_take(w, a, b):
    nt = w.shape[0]
    ng = GROUPS_PER_TILE
    w5 = w.reshape(nt, ng, a, ng, b)
    on_diagonal = jnp.eye(ng, dtype=bool)[None, :, None, :, None]
    return jnp.sum(jnp.where(on_diagonal, w5, 0.0), axis=3).reshape(nt * ng, a, b)


def _att_combine(outs, lses, name):
    s, d = outs[0].shape
    tm = _tile(s, ROWS, 16)
    ng = len(outs)

    def kern(*refs):
        for h in range(d // HEAD_DIM):
            cols = slice(h * HEAD_DIM, (h + 1) * HEAD_DIM)
            ls = [refs[ng + t][:, h * HEAD_DIM:h * HEAD_DIM + 1] for t in range(ng)]
            mx = functools.reduce(jnp.maximum, ls)
            es = [jnp.exp(l - mx) for l in ls]
            den = functools.reduce(lambda a, b: a + b, es)
            o = functools.reduce(lambda a, b: a + b, [es[t] / den * refs[t][:, cols] for t in range(ng)])
            refs[2 * ng][:, cols] = o
            refs[2 * ng + 1][:, cols] = o.astype(BF16)

    return pl.pallas_call(
        kern, name=name, grid=(s // tm,),
        in_specs=[_row_spec(tm, d)] * (2 * ng),
        out_specs=[_row_spec(tm, d)] * 2,
        out_shape=[jax.ShapeDtypeStruct((s, d), F32), jax.ShapeDtypeStruct((s, d), BF16)],
        compiler_params=_params(("parallel",)),
    )(*outs, *lses)


DIL_MAX = 16


def _slab(r):
    return 4 * (r % 4) + r // 4


def _to_slabs(xs, name):
    s, w = xs[0].shape
    n = s // DIL_MAX
    nx = len(xs)

    def kern(*refs):
        o_ref = refs[nx]
        for r in range(DIL_MAX):
            rows = [x_ref[pl.ds(r, n, stride=DIL_MAX), :] for x_ref in refs[:nx]]
            o_ref[_slab(r) * n:(_slab(r) + 1) * n, :] = functools.reduce(lambda a, b: a + b, rows)

    spec = pl.BlockSpec((s, 128), lambda i: (0, i))
    return pl.pallas_call(
        kern, name=name, grid=(w // 128,), in_specs=[spec] * nx, out_specs=spec,
        out_shape=jax.ShapeDtypeStruct((s, w), F32), compiler_params=_params(("parallel",)),
    )(*xs)


def _from_slabs(x, name):
    s, w = x.shape
    n = s // DIL_MAX

    def kern(x_ref, o_ref, ob_ref):
        for r in range(DIL_MAX):
            o_ref[pl.ds(r, n, stride=DIL_MAX), :] = x_ref[_slab(r) * n:(_slab(r) + 1) * n, :]
        ob_ref[...] = o_ref[...].astype(BF16)

    spec = pl.BlockSpec((s, 128), lambda i: (0, i))
    return pl.pallas_call(
        kern, name=name, grid=(w // 128,), in_specs=[spec], out_specs=[spec, spec],
        out_shape=[jax.ShapeDtypeStruct((s, w), F32), jax.ShapeDtypeStruct((s, w), BF16)],
        compiler_params=_params(("parallel",)),
    )(x)


def _norm_bf16(x, gains, name):
    s, d = x.shape
    tm = _tile(s, ROWS, 16)
    ng = len(gains)

    def kern(x_ref, *refs):
        x_ = x_ref[...]
        for t in range(ng):
            refs[ng + t][...] = _rms(x_, refs[t][...]).astype(BF16)

    return pl.pallas_call(
        kern, name=name, grid=(s // tm,),
        in_specs=[_row_spec(tm, d)] + [_vec_spec(d)] * ng, out_specs=[_row_spec(tm, d)] * ng,
        out_shape=[jax.ShapeDtypeStruct((s, d), BF16)] * ng,
        compiler_params=_params(("parallel",)),
    )(x, *gains)


def _att_geometry(dil, s):
    g = DIL_MAX // dil
    return s // DIL_MAX, g, max(ATT_BLK // g, 16)


def _att_mask(g, j, jb, rep):
    b = g * j

    def pos(i):
        sl, jj = i // j, i % j
        off = {1: 0, 4: sl, 16: sl // 4 + 4 * (sl % 4)}[g]
        return g * jj + off

    qi = lax.broadcasted_iota(jnp.int32, (rep * b, 2 * b), 0) % b
    ki = lax.broadcasted_iota(jnp.int32, (rep * b, 2 * b), 1)
    prev = ki < b
    dist = pos(qi) - pos(ki % b) + jnp.where(prev, b, 0)
    return (dist >= 0) & (dist <= ATT_BLK) & (jnp.logical_not(prev) | (jb > 0))


def _stack_heads(x, rep):
    return jnp.concatenate([x[:, h * HEAD_DIM:(h + 1) * HEAD_DIM] for h in range(rep)], axis=0)


def _att_fwd(q, kv, gi, dil, nh, name):
    s = q.shape[0]
    n, g, j = _att_geometry(dil, s)
    b = g * j
    ng = q.shape[1] // (nh * HEAD_DIM)
    rep = nh // N_KV_HEADS
    qw = rep * HEAD_DIM
    scale = HEAD_DIM ** -0.5
    q3 = q.reshape(DIL_MAX, n, q.shape[1])
    kv3 = kv.reshape(DIL_MAX, n, kv.shape[1])

    kw = N_KV_HEADS * HEAD_DIM

    def kern(q_ref, kc_ref, kp_ref, vc_ref, vp_ref, o_ref, l_ref):
        jb = pl.program_id(1)
        mask = _att_mask(g, j, jb, rep)
        for kh in range(N_KV_HEADS):
            hs = slice(kh * HEAD_DIM, (kh + 1) * HEAD_DIM)
            kc, kp, vc, vp = [r[:, :, hs].reshape(b, HEAD_DIM) for r in (kc_ref, kp_ref, vc_ref, vp_ref)]
            k2 = jnp.concatenate([kp, kc], axis=0)
            v2 = jnp.concatenate([vp, vc], axis=0)
            qs = _stack_heads(q_ref[:, :, kh * qw:(kh + 1) * qw].reshape(b, qw), rep)
            sc = jnp.where(mask, _dot_nt(qs, k2) * scale, NEG_INF)
            m = jnp.max(sc, axis=-1, keepdims=True)
            p = jnp.exp(sc - m)
            l = jnp.sum(p, axis=-1, keepdims=True)
            out = _dot((p / l).astype(BF16), v2)
            lse = jnp.broadcast_to(m + jnp.log(l), (rep * b, HEAD_DIM))
            for h in range(rep):
                cols = slice(kh * qw + h * HEAD_DIM, kh * qw + (h + 1) * HEAD_DIM)
                o_ref[:, :, cols] = out[h * b:(h + 1) * b].reshape(g, j, HEAD_DIM)
                l_ref[:, :, cols] = lse[h * b:(h + 1) * b].reshape(g, j, HEAD_DIM)

    def kv_spec(col, prev):
        if prev:
            return pl.BlockSpec((g, j, kw), lambda c, jb: (c, jnp.maximum(jb - 1, 0), col))
        return pl.BlockSpec((g, j, kw), lambda c, jb: (c, jb, col))

    out_spec = pl.BlockSpec((g, j, nh * HEAD_DIM), lambda c, jb: (c, jb, 0))
    out, lse = pl.pallas_call(
        kern, name=name, grid=(dil, n // j),
        in_specs=[pl.BlockSpec((g, j, nh * HEAD_DIM), lambda c, jb: (c, jb, gi)),
                  kv_spec(0, False), kv_spec(0, True), kv_spec(1, False), kv_spec(1, True)],
        out_specs=[out_spec, out_spec],
        out_shape=[jax.ShapeDtypeStruct((DIL_MAX, n, nh * HEAD_DIM), F32)] * 2,
        compiler_params=_params(("parallel", "arbitrary")),
    )(q3, kv3, kv3, kv3, kv3)
    del ng
    return out.reshape(s, nh * HEAD_DIM), lse.reshape(s, nh * HEAD_DIM)


def _att_bwd(q, kv, do, o, lses, dq_all, gi, dil, nh, name):
    s = q.shape[0]
    n, g, j = _att_geometry(dil, s)
    b = g * j
    nb = n // j
    ng = len(lses)
    rep = nh // N_KV_HEADS
    qw = rep * HEAD_DIM
    dm = nh * HEAD_DIM
    scale = HEAD_DIM ** -0.5
    q3 = q.reshape(DIL_MAX, n, q.shape[1])
    kv3 = kv.reshape(DIL_MAX, n, kv.shape[1])
    wide = [a.reshape(DIL_MAX, n, dm) for a in (do, o, *lses)]
    kw = N_KV_HEADS * HEAD_DIM

    def kern(q_ref, kc_ref, kp_ref, vc_ref, vp_ref, do_ref, o_ref, *refs):
        l_refs = refs[:ng]
        dq_ref, dk_ref, dv_ref, ck_ref, cv_ref = refs[ng + 1:]
        jb = pl.program_id(1)

        @pl.when(jb == 0)
        def _():
            ck_ref[...] = jnp.zeros_like(ck_ref)
            cv_ref[...] = jnp.zeros_like(cv_ref)

        @pl.when(jb < nb)
        def _():
            mask = _att_mask(g, j, jb, rep)
            for kh in range(N_KV_HEADS):
                hs = slice(kh * HEAD_DIM, (kh + 1) * HEAD_DIM)
                ws = slice(kh * qw, (kh + 1) * qw)

                def stacked(ref):
                    return _stack_heads(ref[:, :, ws].reshape(b, qw), rep)

                kc, kp, vc, vp = [r[:, :, hs].reshape(b, HEAD_DIM) for r in (kc_ref, kp_ref, vc_ref, vp_ref)]
                k2 = jnp.concatenate([kp, kc], axis=0)
                v2 = jnp.concatenate([vp, vc], axis=0)
                qs = stacked(q_ref)
                ls = [stacked(r)[:, 0:1] for r in l_refs]
                mx = functools.reduce(jnp.maximum, ls)
                den = functools.reduce(lambda a, c: a + c, [jnp.exp(l - mx) for l in ls])
                lse_g = ls[gi]
                w = jnp.exp(lse_g - mx) / den
                do_ = stacked(do_ref)
                ct = w * jnp.sum(do_ * stacked(o_ref), axis=-1, keepdims=True)
                dob = (w * do_).astype(BF16)
                p = jnp.exp(jnp.where(mask, _dot_nt(qs, k2) * scale, NEG_INF) - lse_g)
                ds = (p * (_dot_nt(dob, v2) - ct) * scale).astype(BF16)
                dq = (_dot(ds, k2)).astype(BF16)
                for h in range(rep):
                    cols = slice(kh * qw + h * HEAD_DIM, kh * qw + (h + 1) * HEAD_DIM)
                    dq_ref[:, :, cols] = dq[h * b:(h + 1) * b].reshape(g, j, HEAD_DIM)
                dk2 = _dot_tn(ds, qs)
                dv2 = _dot_tn(p.astype(BF16), dob)
                dk_ref[:, :, hs] = (ck_ref[:, hs] + dk2[:b]).reshape(g, j, HEAD_DIM)
                dv_ref[:, :, hs] = (cv_ref[:, hs] + dv2[:b]).reshape(g, j, HEAD_DIM)
                ck_ref[:, hs] = dk2[b:]
                cv_ref[:, hs] = dv2[b:]

        @pl.when(jb == nb)
        def _():
            dk_ref[...] = ck_ref[...].reshape(g, j, kw)
            dv_ref[...] = cv_ref[...].reshape(g, j, kw)

    def jq(jb):
        return jnp.minimum(jb, nb - 1)

    def kv_spec(col, prev):
        if prev:
            return pl.BlockSpec((g, j, kw), lambda c, jb: (c, jnp.maximum(jq(jb) - 1, 0), col))
        return pl.BlockSpec((g, j, kw), lambda c, jb: (c, jq(jb), col))

    wide_spec = pl.BlockSpec((g, j, dm), lambda c, jb: (c, jq(jb), 0))
    dkv_spec = pl.BlockSpec((g, j, kw), lambda c, jb: (c, jnp.maximum(jb - 1, 0), 0))
    dq, dk, dv = pl.pallas_call(
        kern, name=name, grid=(dil, nb + 1),
        in_specs=[pl.BlockSpec((g, j, dm), lambda c, jb: (c, jq(jb), gi)),
                  kv_spec(0, False), kv_spec(0, True), kv_spec(1, False), kv_spec(1, True)]
        + [wide_spec] * (2 + ng) + [_ANY],
        out_specs=[pl.BlockSpec((g, j, dm), lambda c, jb: (c, jq(jb), gi)), dkv_spec, dkv_spec],
        out_shape=[jax.ShapeDtypeStruct((DIL_MAX, n, dq_all.shape[1]), BF16),
                   jax.ShapeDtypeStruct((DIL_MAX, n, kw), F32),
                   jax.ShapeDtypeStruct((DIL_MAX, n, kw), F32)],
        scratch_shapes=[pltpu.VMEM((b, kw), F32), pltpu.VMEM((b, kw), F32)],
        input_output_aliases={7 + ng: 0},
        compiler_params=_params(("parallel", "arbitrary")),
    )(q3, kv3, kv3, kv3, kv3, *wide, dq_all.reshape(DIL_MAX, n, dq_all.shape[1]))
    return dq.reshape(s, -1), dk.reshape(s, -1), dv.reshape(s, -1)


def _dkv_sum(dks, dvs, name):
    s, w = dks[0].shape
    tm = _tile(s, ROWS, 16)
    ng = len(dks)

    def kern(*refs):
        o_ref = refs[2 * ng]
        o_ref[:, :w] = functools.reduce(lambda a, b: a + b, [refs[t][...] for t in range(ng)]).astype(BF16)
        o_ref[:, w:] = functools.reduce(lambda a, b: a + b, [refs[ng + t][...] for t in range(ng)]).astype(BF16)

    return pl.pallas_call(
        kern, name=name, grid=(s // tm,),
        in_specs=[_row_spec(tm, w)] * (2 * ng),
        out_specs=_row_spec(tm, 2 * w),
        out_shape=jax.ShapeDtypeStruct((s, 2 * w), BF16),
        compiler_params=_params(("parallel",)),
    )(*dks, *dvs)


def _ffn_fwd(hf, weight, layer, tag):
    a, gu = _ffn_in_act(hf, weight(f"w_in{layer}", hf), f"{tag}_in")
    o = _mm_nn(a, weight(f"w_out{layer}", a), F32, f"{tag}_out", tn=1024, tk=5632)
    return gu, a, o


def _ffn_bwd(dresb, hf, gu, a, weight, emit, layer, tag):
    w_in, w_out = weight(f"w_in{layer}", None), weight(f"w_out{layer}", None)
    zero = emit(f"w_out{layer}", _mm_tn(a, dresb, 1, f"{tag}_out_dw", tn=1024))
    dgu = _ffn_out_dx_act(dresb, w_out, gu, f"{tag}_out_dx")
    zero = zero + emit(f"w_in{layer}",
                       _mm_tn(hf, dgu, w_in.shape[0], f"{tag}_in_dw", tn=FFN_TILE, paired=True))
    return _mm_nt(dgu, w_in, f"{tag}_in_dx", tr=FFN_TILE, tkc=512, paired=True), zero


def _local_step(x, target, small, weight, emit):
    s, d = x.shape
    nh = d // HEAD_DIM
    n_groups = d // S5_GROUP_CH
    nt = n_groups // GROUPS_PER_TILE
    p_, c_ = S5_STATE, S5_GROUP_CH

    disc_in = (small["lam_re"], small["lam_im"], small["log_dt"], small["b_re"], small["b_im"])
    (lb_re, lb_im, bb_re, bb_im), disc_vjp = jax.vjp(_s5_discretize, *disc_in)
    del lb_re, lb_im
    dt = jnp.exp(small["log_dt"])[:, None]
    def pole_powers(exponents):
        k = exponents[:, None, None]
        mag = jnp.exp(k * (small["lam_re"] * dt)[None])
        ang = k * (small["lam_im"] * dt)[None]
        return ((mag * jnp.cos(ang)).reshape(SUB, n_groups * p_),
                (mag * jnp.sin(ang)).reshape(SUB, n_groups * p_))

    pw_re, pw_im = pole_powers(jnp.arange(1, SUB + 1, dtype=F32))
    pf_re, pf_im = pole_powers(jnp.arange(SUB, 0, -1, dtype=F32))
    bbd_re = _block_diag(bb_re.transpose(0, 2, 1).reshape(nt, GROUPS_PER_TILE, c_, p_)).astype(BF16)
    bbd_im = _block_diag(bb_im.transpose(0, 2, 1).reshape(nt, GROUPS_PER_TILE, c_, p_)).astype(BF16)
    cbd_re = _block_diag(small["c_re"].transpose(0, 2, 1).reshape(nt, GROUPS_PER_TILE, p_, c_)).astype(BF16)
    cbd_im = _block_diag(small["c_im"].transpose(0, 2, 1).reshape(nt, GROUPS_PER_TILE, p_, c_)).astype(BF16)

    h0 = _norm_f32(x, small["a_norm"], "s5_norm")
    xs_re, xs_im, y0, z = _s5_fwd(h0, bbd_re, bbd_im, cbd_re, cbd_im, pw_re, pw_im, small["s5_d"], "s5_fwd")
    vg = _mm_nn(z, weight("w_glu", z), F32, "glu_mm", tn=1024)
    x1, hf0 = _glu_res_norm(vg, x, small["ffn_norm0"], "glu_res_norm")
    gu0, a0, o0 = _ffn_fwd(hf0, weight, 0, "ffn0")
    x2 = _to_slabs([o0, x1], "to_slabs")
    kvn, h1 = _norm_bf16(x2, [small["kv_norm"], small["b_norm"]], "att_norms")
    kv = _mm_nn(kvn, weight("w_kv", kvn), BF16, "kv_mm", tn=1024)
    q = _mm_nn(h1, weight("w_q", kv), BF16, "q_mm", tn=1536)
    outs, lses = [], []
    for gi, (window, dil) in enumerate(PATTERNS):
        assert window // dil == ATT_BLK
        og, lg = _att_fwd(q, kv, gi, dil, nh, f"att_fwd{gi}")
        outs.append(og)
        lses.append(lg)
    oatt, oattb = _att_combine(outs, lses, "att_combine")
    ao = _mm_nn(oattb, weight("w_o", oattb), F32, "o_mm", tn=1024)
    x3, hf1 = _res_norm(ao, x2, [small["ffn_norm1"]], "att_res_norm")
    gu1, a1, o1 = _ffn_fwd(hf1, weight, 1, "ffn1")
    dres, dresb, loss_rows, d_final = _loss_head(
        o1, x3, small["final_norm"], _to_slabs([target], "target_to_slabs"), "loss_head")

    dhf1, zero = _ffn_bwd(dresb, hf1, gu1, a1, weight, emit, 1, "ffn1")
    dres, dresb, d_ffn_norm1 = _norm_bwd(x3, [small["ffn_norm1"] + zero], [dhf1], dres, "ffn1_norm_bwd")
    zero = emit("w_o", _mm_tn(oattb, dresb, 1, "o_dw", tn=1024))
    doatt = _mm_nt(dresb, weight("w_o", None), "o_dx", tr=2048)
    dks, dvs = [], []
    dq = lax.empty(q.shape, BF16)
    for gi, (window, dil) in enumerate(PATTERNS):
        dq, dk_g, dv_g = _att_bwd(q, kv, doatt, oatt, lses, dq, gi, dil, nh, f"att_bwd{gi}")
        dks.append(dk_g)
        dvs.append(dv_g)
    dkv = _dkv_sum(dks, dvs, "dkv_sum")
    w_q = weight("w_q", None)
    zero = zero + emit("w_q", _mm_tn(h1, dq, w_q.shape[0], "q_dw", tn=1536))
    zero = zero + emit("w_kv", _mm_tn(kvn, dkv, 1, "kv_dw", tn=1024))
    dh1 = _mm_nt(dq, w_q, "q_dx", tr=1536)
    dkvn = _mm_nt(dkv, weight("w_kv", None), "kv_dx", tr=1024)
    dres, dresb, d_b_norm, d_kv_norm = _norm_bwd(
        x2, [small["b_norm"] + zero, small["kv_norm"]], [dh1, dkvn], dres, "att_norm_bwd")
    dres, dresb = _from_slabs(dres, "from_slabs")
    dhf0, zero = _ffn_bwd(dresb, hf0, gu0, a0, weight, emit, 0, "ffn0")
    dres, dresb, d_ffn_norm0 = _norm_bwd(x1, [small["ffn_norm0"] + zero], [dhf0], dres, "ffn0_norm_bwd")
    del dresb
    dvg = _glu_bwd(dres, vg, "glu_bwd")
    w_glu = weight("w_glu", None)
    zero = emit("w_glu", _mm_tn(z, dvg, w_glu.shape[0], "glu_dw", tn=1024))
    dz = _mm_nt(dvg, w_glu, "glu_dx", tr=1024)
    dh0, d_s5_d, dcr, dci_neg, dbr, dbi, dar, dai = _s5_bwd(
        dz, y0, h0, xs_re, xs_im, bbd_re, bbd_im, cbd_re, cbd_im, pw_re, pw_im, pf_re, pf_im,
        small["s5_d"] + zero, "s5_bwd")
    grad_x, _, d_a_norm = _norm_bwd(x, [small["a_norm"]], [dh0], dres, "s5_norm_bwd")

    d_bb_re = _block_diag_take(dbr, c_, p_).transpose(0, 2, 1)
    d_bb_im = _block_diag_take(dbi, c_, p_).transpose(0, 2, 1)
    d_c_re = _block_diag_take(dcr, c_, p_)
    d_c_im = -_block_diag_take(dci_neg, c_, p_)
    d_lam_re, d_lam_im, d_log_dt, d_b_re, d_b_im = disc_vjp(
        (dar.reshape(n_groups, p_), dai.reshape(n_groups, p_), d_bb_re, d_bb_im))

    d_small = dict(lam_re=d_lam_re, lam_im=d_lam_im, log_dt=d_log_dt, b_re=d_b_re, b_im=d_b_im,
                   c_re=d_c_re, c_im=d_c_im, s5_d=d_s5_d, a_norm=d_a_norm, ffn_norm0=d_ffn_norm0,
                   ffn_norm1=d_ffn_norm1, b_norm=d_b_norm, kv_norm=d_kv_norm, final_norm=d_final)
    return loss_rows, grad_x, d_small


def _place():
    x, y, c = lax.axis_index("x"), lax.axis_index("y"), lax.axis_index("c")
    return x, y, c, [(1 - x, y), (x, 1 - y), (1 - x, 1 - y)]


_ANY = pl.BlockSpec(memory_space=pl.ANY)


_HBM = pl.BlockSpec(memory_space=pltpu.HBM)
_SEM = pl.BlockSpec(memory_space=pltpu.SEMAPHORE)
_EFFECT = pltpu.SideEffectType.DATAFLOW_SIDE_EFFECTING


def _in_hbm(a):
    return pltpu.with_memory_space_constraint(a, pltpu.HBM)


def _cast_place(shards, layer, chip, name):
    _, r, c = shards.shape
    tm = _tile(r, ROWS, 16)

    def kern(chip_ref, x_ref, o_ref):
        del chip_ref
        o_ref[...] = x_ref[...].astype(BF16)

    return pl.pallas_call(
        kern, name=name,
        grid_spec=pltpu.PrefetchScalarGridSpec(
            num_scalar_prefetch=1, grid=(r // tm,),
            in_specs=[pl.BlockSpec((None, tm, c), lambda i, ch: (layer, i, 0))],
            out_specs=pl.BlockSpec((None, tm, c), lambda i, ch: (ch[0], i, 0))),
        out_shape=jax.ShapeDtypeStruct((N_CHIPS, r, c), BF16),
        compiler_params=_params(("parallel",)),
    )(chip, shards)


def _gather_start(lands, name):
    n = len(lands)

    def body(*refs):
        land = refs[:n]
        send, recv = refs[n:2 * n], refs[2 * n:3 * n]
        token = refs[4 * n]
        x, y, c, peers = _place()
        me = 2 * x + y
        for a in range(n):
            for k, (px, py) in enumerate(peers):
                pltpu.make_async_remote_copy(
                    src_ref=land[a].at[me], dst_ref=land[a].at[me], send_sem=send[a].at[k], recv_sem=recv[a].at[k],
                    device_id=(px, py, c), device_id_type=MESH).start()
        token[...] = jnp.zeros_like(token)

    outs = pl.pallas_call(
        body, name=name,
        out_shape=[pltpu.SemaphoreType.DMA((3,))] * (2 * n) + [pltpu.HBM(a.shape, a.dtype) for a in lands]
        + [jax.ShapeDtypeStruct((8, 128), F32)],
        in_specs=[_HBM] * n,
        out_specs=[_SEM] * (2 * n) + [_HBM] * n + [pl.BlockSpec(memory_space=pltpu.VMEM)],
        input_output_aliases={i: 2 * n + i for i in range(n)},
        compiler_params=pltpu.CompilerParams(has_side_effects=_EFFECT),
    )(*[_in_hbm(a) for a in lands])
    return outs[:n], outs[n:2 * n], outs[2 * n:3 * n], outs[3 * n]


def _gather_wait(lands, sends, recvs, after, name):
    n = len(lands)

    def body(*refs):
        land, send, recv = refs[:n], refs[n:2 * n], refs[2 * n:3 * n]
        x, y, c, peers = _place()
        me = 2 * x + y
        for a in range(n):
            for k, (px, py) in enumerate(peers):
                cp = pltpu.make_async_remote_copy(
                    src_ref=land[a].at[me], dst_ref=land[a].at[2 * px + py], send_sem=send[a].at[k],
                    recv_sem=recv[a].at[k], device_id=(px, py, c), device_id_type=MESH)
                cp.wait_send()
                cp.wait_recv()

    return pl.pallas_call(
        body, name=name,
        out_shape=[pltpu.HBM(a.shape, a.dtype) for a in lands],
        in_specs=[_HBM] * n + [_SEM] * (2 * n) + [_ANY], out_specs=[_HBM] * n,
        input_output_aliases={i: i for i in range(n)},
        compiler_params=pltpu.CompilerParams(has_side_effects=_EFFECT),
    )(*lands, *sends, *recvs, after)


def _scatter_start(g, name):
    def body(g_ref, land_ref, send, recv, g_out, land_out, token):
        del g_out, land_out
        x, y, c, peers = _place()
        for k, (px, py) in enumerate(peers):
            pltpu.make_async_remote_copy(
                src_ref=g_ref.at[2 * px + py], dst_ref=land_ref.at[k], send_sem=send.at[k], recv_sem=recv.at[k],
                device_id=(px, py, c), device_id_type=MESH).start()
        token[...] = jnp.zeros_like(token)

    land = lax.empty((3,) + g.shape[1:], g.dtype)
    return pl.pallas_call(
        body, name=name,
        out_shape=(pltpu.SemaphoreType.DMA((3,)), pltpu.SemaphoreType.DMA((3,)),
                   pltpu.HBM(g.shape, g.dtype), pltpu.HBM(land.shape, land.dtype),
                   jax.ShapeDtypeStruct((8, 128), F32)),
        in_specs=(_HBM, _HBM), out_specs=(_SEM, _SEM, _HBM, _HBM, pl.BlockSpec(memory_space=pltpu.VMEM)),
        input_output_aliases={0: 2, 1: 3},
        compiler_params=pltpu.CompilerParams(has_side_effects=_EFFECT),
    )(_in_hbm(g), _in_hbm(land))


def _scatter_wait(started, after):
    n = len(started)

    def body(*refs):
        gs, lands = refs[:n], refs[n:2 * n]
        sends, recvs = refs[2 * n:3 * n], refs[3 * n:4 * n]
        x, y, c, peers = _place()
        for a in range(n):
            for k, (px, py) in enumerate(peers):
                cp = pltpu.make_async_remote_copy(
                    src_ref=gs[a].at[2 * px + py], dst_ref=lands[a].at[k], send_sem=sends[a].at[k],
                    recv_sem=recvs[a].at[k], device_id=(px, py, c), device_id_type=MESH)
                cp.wait_send()
                cp.wait_recv()

    gs = [s[2] for s in started]
    lands = [s[3] for s in started]
    outs = pl.pallas_call(
        body, name="scatter_wait",
        out_shape=[pltpu.HBM(a.shape, a.dtype) for a in gs + lands],
        in_specs=[_HBM] * (2 * n) + [_SEM] * (2 * n) + [_ANY], out_specs=[_HBM] * (2 * n),
        input_output_aliases={i: i for i in range(2 * n)},
        compiler_params=pltpu.CompilerParams(has_side_effects=_EFFECT),
    )(*gs, *lands, *[s[0] for s in started], *[s[1] for s in started], after)
    return outs[:n], outs[n:]


def _sibling():
    return lax.axis_index("x"), lax.axis_index("y"), 1 - lax.axis_index("c")


def _swap_start(parts):
    n = len(parts)

    def body(*refs):
        ins, land = refs[:n], refs[n:2 * n]
        send, recv = refs[2 * n], refs[2 * n + 1]
        token = refs[4 * n + 2]
        for a in range(n):
            pltpu.make_async_remote_copy(
                src_ref=ins[a], dst_ref=land[a], send_sem=send.at[a], recv_sem=recv.at[a],
                device_id=_sibling(), device_id_type=MESH).start()
        token[...] = jnp.zeros_like(token)

    lands = [lax.empty(a.shape, a.dtype) for a in parts]
    outs = pl.pallas_call(
        body, name="swap_start",
        out_shape=[pltpu.SemaphoreType.DMA((n,))] * 2 + [pltpu.HBM(a.shape, a.dtype) for a in parts + lands]
        + [jax.ShapeDtypeStruct((8, 128), F32)],
        in_specs=[_HBM] * (2 * n),
        out_specs=[_SEM] * 2 + [_HBM] * (2 * n) + [pl.BlockSpec(memory_space=pltpu.VMEM)],
        input_output_aliases={i: 2 + i for i in range(2 * n)},
        compiler_params=pltpu.CompilerParams(has_side_effects=_EFFECT),
    )(*[_in_hbm(a) for a in parts + lands])
    return outs[0], outs[1], outs[2:2 + n], outs[2 + n:2 + 2 * n], outs[2 + 2 * n]


def _swap_wait(parts, lands, send, recv, after):
    n = len(parts)

    def body(*refs):
        ins, land = refs[:n], refs[n:2 * n]
        send_sem, recv_sem = refs[2 * n], refs[2 * n + 1]
        for a in range(n):
            cp = pltpu.make_async_remote_copy(
                src_ref=ins[a], dst_ref=land[a], send_sem=send_sem.at[a], recv_sem=recv_sem.at[a],
                device_id=_sibling(), device_id_type=MESH)
            cp.wait_send()
            cp.wait_recv()

    outs = pl.pallas_call(
        body, name="swap_wait",
        out_shape=[pltpu.HBM(a.shape, a.dtype) for a in list(parts) + list(lands)],
        in_specs=[_HBM] * (2 * n) + [_SEM] * 2 + [_ANY], out_specs=[_HBM] * (2 * n),
        input_output_aliases={i: i for i in range(2 * n)},
        compiler_params=pltpu.CompilerParams(has_side_effects=_EFFECT),
    )(*parts, *lands, send, recv, after)
    return outs[:n], outs[n:]


def _all_reduce_small(v):
    nd, r, w = v.shape
    assert nd == N_DEV

    def body(v_ref, out_ref, land_ref, red_ref, send1, recv1, send2, recv2):
        x, y, c = lax.axis_index("x"), lax.axis_index("y"), lax.axis_index("c")
        me = 4 * x + 2 * y + c
        peers = []
        for k in range(1, N_DEV):
            kx, ky, kc = (k >> 2) & 1, (k >> 1) & 1, k & 1
            peers.append((1 - x if kx else x, 1 - y if ky else y, 1 - c if kc else c))
        first = []
        for k, (px, py, pc) in enumerate(peers):
            cp = pltpu.make_async_remote_copy(
                src_ref=v_ref.at[4 * px + 2 * py + pc], dst_ref=land_ref.at[me], send_sem=send1.at[k],
                recv_sem=recv1.at[k], device_id=(px, py, pc), device_id_type=MESH)
            cp.start()
            first.append(cp)
        land_ref[me] = v_ref[me]
        for cp in first:
            cp.wait()
        acc = land_ref[0]
        for j in range(1, N_DEV):
            acc = acc + land_ref[j]
        red_ref[...] = acc
        second = []
        for k, (px, py, pc) in enumerate(peers):
            cp = pltpu.make_async_remote_copy(
                src_ref=red_ref, dst_ref=out_ref.at[me], send_sem=send2.at[k],
                recv_sem=recv2.at[k], device_id=(px, py, pc), device_id_type=MESH)
            cp.start()
            second.append(cp)
        out_ref[me] = acc
        for cp in second:
            cp.wait()

    vmem = pl.BlockSpec(memory_space=pltpu.VMEM)
    return pl.pallas_call(
        body, name="all_reduce_small",
        in_specs=[vmem], out_specs=vmem,
        out_shape=jax.ShapeDtypeStruct((nd, r, w), F32),
        scratch_shapes=[pltpu.VMEM((nd, r, w), F32), pltpu.VMEM((r, w), F32)]
        + [pltpu.SemaphoreType.DMA((N_DEV - 1,))] * 4,
        compiler_params=pltpu.CompilerParams(vmem_limit_bytes=VMEM_LIMIT),
    )(v)


def _adam_math(w, g, m, v):
    m = ADAM_B1 * m + (1.0 - ADAM_B1) * g
    v = ADAM_B2 * v + (1.0 - ADAM_B2) * (g * g)
    m_hat = m / (1.0 - ADAM_B1 ** ADAM_STEP)
    v_hat = v / (1.0 - ADAM_B2 ** ADAM_STEP)
    delta = -ADAM_LR * (m_hat / (jnp.sqrt(v_hat) + ADAM_EPS) + ADAM_WD * w)
    return delta, m, v


def _sum_blocks(own, got, chip, name):
    _, r, c = own.shape
    tm = _tile(r, ROWS, 16)

    def kern(chip_ref, own_ref, got_ref, o_ref):
        del chip_ref
        acc = own_ref[...].astype(F32)
        for k in range(3):
            acc = acc + got_ref[k].astype(F32)
        o_ref[...] = acc

    return pl.pallas_call(
        kern, name=name,
        grid_spec=pltpu.PrefetchScalarGridSpec(
            num_scalar_prefetch=1, grid=(r // tm,),
            in_specs=[pl.BlockSpec((None, tm, c), lambda i, ch: (ch[0], i, 0)),
                      pl.BlockSpec((3, tm, c), lambda i, ch: (0, i, 0))],
            out_specs=pl.BlockSpec((tm, c), lambda i, ch: (i, 0))),
        out_shape=jax.ShapeDtypeStruct((r, c), F32),
        compiler_params=_params(("parallel",)),
    )(chip, own, got)


def _adamw(parts, w, m, v, name):
    nl, r, c = w.shape
    assert len(parts) == nl
    tm = _tile(r, 128, 8)

    def kern(*refs):
        p_refs = refs[:2 * nl]
        w_ref, m_ref, v_ref, g_ref, d_ref, mo_ref, vo_ref = refs[2 * nl:]
        layer = pl.program_id(0)
        g = p_refs[0][...] + p_refs[1][...]
        for ll in range(1, nl):
            g = jnp.where(layer == ll, p_refs[2 * ll][...] + p_refs[2 * ll + 1][...], g)
        g_ref[...] = g
        d_ref[...], mo_ref[...], vo_ref[...] = _adam_math(w_ref[...], g, m_ref[...], v_ref[...])

    def part_spec(ll):
        return pl.BlockSpec((tm, c), lambda l, i: (jnp.where(l == ll, i, 0), 0))

    spec = pl.BlockSpec((None, tm, c), lambda l, i: (l, i, 0))
    return pl.pallas_call(
        kern, name=name, grid=(nl, r // tm),
        in_specs=[part_spec(ll) for ll in range(nl) for _ in range(2)] + [spec] * 3, out_specs=[spec] * 4,
        out_shape=[jax.ShapeDtypeStruct((nl, r, c), F32)] * 4,
        compiler_params=_params(("arbitrary", "parallel")),
    )(*[p for pair in parts for p in pair], w, m, v)


def _adamw_small(g, w, m, v, name):
    def kern(g_ref, w_ref, m_ref, v_ref, d_ref, mo_ref, vo_ref):
        d_ref[...], mo_ref[...], vo_ref[...] = _adam_math(w_ref[...], g_ref[...], m_ref[...], v_ref[...])

    return pl.pallas_call(
        kern, name=name,
        out_shape=[jax.ShapeDtypeStruct(g.shape, F32)] * 3,
        compiler_params=pltpu.CompilerParams(vmem_limit_bytes=VMEM_LIMIT),
    )(g, w, m, v)


def _pack(arrays, rows):
    flat = jnp.concatenate([a.reshape(-1).astype(F32) for a in arrays])
    return jnp.pad(flat, (0, rows * 128 - flat.shape[0])).reshape(rows, 128)


def _unpack(packed, shapes):
    flat = packed.reshape(-1)
    out, off = [], 0
    for shp in shapes:
        size = math.prod(shp)
        out.append(flat[off:off + size].reshape(shp))
        off += size
    return out


_REPLICATED = ["s5_lam_re", "s5_lam_im", "s5_log_dt", "s5_b_re", "s5_b_im", "s5_c_re", "s5_c_im",
               "ffn_norm", "b_norm_mix", "kv_norm", "final_norm"]
_CHIP_VECTORS = ["s5_d", "a_norm_mix"]
_BIG = ["s5_w_glu", "ffn_w_in", "ffn_w_out", "attn_w_q", "attn_w_o", "w_kv"]
_WEIGHT_ORDER = ["s5_lam_re", "s5_lam_im", "s5_log_dt", "s5_b_re", "s5_b_im", "s5_c_re", "s5_c_im", "s5_d",
                 "s5_w_glu", "a_norm_mix", "ffn_norm", "ffn_w_in", "ffn_w_out", "b_norm_mix", "attn_w_q",
                 "attn_w_o", "kv_norm", "w_kv", "final_norm"]


def _step(x, loss_target, w, m, v):
    s, d = x.shape[1], x.shape[2]
    chip = 2 * lax.axis_index("x") + lax.axis_index("y")

    col_sharded = dict(w_glu=("s5_w_glu", 0), w_in0=("ffn_w_in", 0), w_in1=("ffn_w_in", 1), w_q=("attn_w_q", 0))
    row_sharded = dict(w_out0=("ffn_w_out", 0), w_out1=("ffn_w_out", 1), w_o=("attn_w_o", 0), w_kv=("w_kv", 0))
    names = ["w_glu", "w_in0", "w_out0", "w_kv", "w_q", "w_o", "w_in1", "w_out1"]
    local = {**col_sharded, **row_sharded}

    def layers(a):
        return a.reshape((-1,) + a.shape[-2:])

    chip_arr = jnp.reshape(chip, (1,)).astype(jnp.int32)
    vector_lands = [lax.dynamic_update_slice(jnp.zeros((N_CHIPS,) + w[n].shape, F32), w[n][None], (chip, 0, 0))
                    for n in _CHIP_VECTORS]
    first = 2
    lands = [_cast_place(layers(w[local[n][0]]), local[n][1], chip_arr, f"cast_{n}") for n in names[:first]]
    send, recv, land_thru, _ = _gather_start(vector_lands + lands, "gather_start_first")
    lands = [_cast_place(layers(w[local[n][0]]), local[n][1], chip_arr, f"cast_{n}") for n in names[first:]]
    *later, rest_token = _gather_start(lands, "gather_start_rest")
    send, recv, land_thru = [list(a) + list(b) for a, b in zip((send, recv, land_thru), later)]
    vectors = _gather_wait(land_thru[:2], send[:2], recv[:2], jnp.zeros((8, 128), F32), "gather_wait_vectors")
    s5_d_full = vectors[0].reshape(1, d)
    a_norm_full = vectors[1].reshape(1, d) + rest_token[0, 0]
    arrived = {}

    def weight(name, after):
        if name not in arrived:
            i = 2 + names.index(name)
            land, = _gather_wait([land_thru[i]], [send[i]], [recv[i]], after, f"gather_wait_{name}")
            arrived[name] = land if name in col_sharded else land.reshape(1, -1, land.shape[-1])
        return arrived[name]

    started = {}

    def emit(name, dw):
        outs = _scatter_start(dw.reshape(N_CHIPS, -1, dw.shape[-1]), f"scatter_start_{name}")
        started[name] = outs[:4]
        return outs[4][0, 0]

    small = dict(lam_re=w["s5_lam_re"][0], lam_im=w["s5_lam_im"][0], log_dt=w["s5_log_dt"][0],
                 b_re=w["s5_b_re"][0], b_im=w["s5_b_im"][0], c_re=w["s5_c_re"][0], c_im=w["s5_c_im"][0],
                 s5_d=s5_d_full, a_norm=a_norm_full, ffn_norm0=w["ffn_norm"][0:1], ffn_norm1=w["ffn_norm"][1:2],
                 b_norm=w["b_norm_mix"], kv_norm=w["kv_norm"].reshape(1, d), final_norm=w["final_norm"].reshape(1, d))

    loss_rows, grad_x, d_small = _local_step(x[0], loss_target[0], small, weight, emit)

    g4, got = _scatter_wait([started[n] for n in names], grad_x)
    partial = [_sum_blocks(o, r, chip_arr, f"sum_{n}") for n, o, r in zip(names, g4, got)]
    swap_send, swap_recv, partial, swap_land, swap_token = _swap_start(partial)
    result = {}

    rep_grads = [d_small["lam_re"], d_small["lam_im"], d_small["log_dt"], d_small["b_re"], d_small["b_im"],
                 d_small["c_re"], d_small["c_im"],
                 jnp.concatenate([d_small["ffn_norm0"], d_small["ffn_norm1"]], axis=0),
                 d_small["b_norm"], d_small["kv_norm"], d_small["final_norm"]]
    vec_grads = [d_small["s5_d"], d_small["a_norm"]]
    to_reduce = rep_grads + vec_grads + [jnp.sum(loss_rows).reshape(1) + swap_token[0, 0]]
    total = sum(math.prod(a.shape) for a in to_reduce)
    rows_per = -(-total // (N_DEV * 128 * 8)) * 8
    reduced = _all_reduce_small(_pack(to_reduce, N_DEV * rows_per).reshape(N_DEV, rows_per, 128))
    red = _unpack(reduced, [a.shape for a in to_reduce])
    loss = red[-1][0]
    g_small = dict(zip(_REPLICATED, [r.reshape(w[n].shape) for r, n in zip(red[:len(rep_grads)], _REPLICATED)]))
    for n, r in zip(_CHIP_VECTORS, red[len(rep_grads):-1]):
        g_small[n] = lax.dynamic_slice_in_dim(r.reshape(1, d), chip * (d // N_CHIPS), d // N_CHIPS, axis=1)
    small_names = _REPLICATED + _CHIP_VECTORS
    n_small = sum(math.prod(w[n].shape) for n in small_names)
    rows_small = -(-n_small // (128 * 8)) * 8
    packed = [_pack([src[n] for n in small_names], rows_small) for src in (g_small, w, m, v)]
    upd = _adamw_small(*packed, "adamw_small")
    shapes = [w[n].shape for n in small_names]
    for n, dl, mo, vo in zip(small_names, *[_unpack(u, shapes) for u in upd]):
        result[n] = [g_small[n], dl, mo, vo]

    partial, other = _swap_wait(partial, swap_land, swap_send, swap_recv, upd[0])
    part = dict(zip(names, zip(partial, other)))
    for name in _BIG:
        parts = [part[n] for n in sorted((n for n in names if local[n][0] == name), key=lambda n: local[n][1])]
        outs = _adamw(parts, layers(w[name]), layers(m[name]), layers(v[name]), f"adamw_{name}")
        result[name] = [o.reshape(w[name].shape) for o in outs]

    cols = [[result[n][t] for n in _WEIGHT_ORDER] for t in range(4)]
    return (loss, grad_x.reshape(x.shape), *cols[0], *cols[1], *cols[2], *cols[3])


def kernel(x, s5_lam_re, s5_lam_im, s5_log_dt, s5_b_re, s5_b_im, s5_c_re, s5_c_im, s5_d, s5_w_glu, a_norm_mix, ffn_norm, ffn_w_in, ffn_w_out, b_norm_mix, attn_w_q, attn_w_o, kv_norm, w_kv, final_norm, loss_target, m_s5_lam_re, m_s5_lam_im, m_s5_log_dt, m_s5_b_re, m_s5_b_im, m_s5_c_re, m_s5_c_im, m_s5_d, m_s5_w_glu, m_a_norm_mix, m_ffn_norm, m_ffn_w_in, m_ffn_w_out, m_b_norm_mix, m_attn_w_q, m_attn_w_o, m_kv_norm, m_w_kv, m_final_norm, v_s5_lam_re, v_s5_lam_im, v_s5_log_dt, v_s5_b_re, v_s5_b_im, v_s5_c_re, v_s5_c_im, v_s5_d, v_s5_w_glu, v_a_norm_mix, v_ffn_norm, v_ffn_w_in, v_ffn_w_out, v_b_norm_mix, v_attn_w_q, v_attn_w_o, v_kv_norm, v_w_kv, v_final_norm):
    w = dict(zip(_WEIGHT_ORDER, (s5_lam_re, s5_lam_im, s5_log_dt, s5_b_re, s5_b_im, s5_c_re, s5_c_im, s5_d, s5_w_glu, a_norm_mix, ffn_norm, ffn_w_in, ffn_w_out, b_norm_mix, attn_w_q, attn_w_o, kv_norm, w_kv, final_norm)))
    m = dict(zip(_WEIGHT_ORDER, (m_s5_lam_re, m_s5_lam_im, m_s5_log_dt, m_s5_b_re, m_s5_b_im, m_s5_c_re, m_s5_c_im, m_s5_d, m_s5_w_glu, m_a_norm_mix, m_ffn_norm, m_ffn_w_in, m_ffn_w_out, m_b_norm_mix, m_attn_w_q, m_attn_w_o, m_kv_norm, m_w_kv, m_final_norm)))
    v = dict(zip(_WEIGHT_ORDER, (v_s5_lam_re, v_s5_lam_im, v_s5_log_dt, v_s5_b_re, v_s5_b_im, v_s5_c_re, v_s5_c_im, v_s5_d, v_s5_w_glu, v_a_norm_mix, v_ffn_norm, v_ffn_w_in, v_ffn_w_out, v_b_norm_mix, v_attn_w_q, v_attn_w_o, v_kv_norm, v_w_kv, v_final_norm)))
    return _step(x, loss_target, w, m, v)
```

```python
import functools
import math

import jax
import jax.numpy as jnp
from jax import lax
from jax.experimental import pallas as pl
from jax.experimental.pallas import tpu as pltpu

F32 = jnp.float32
BF16 = jnp.bfloat16

S5_GROUP_CH = 16
S5_STATE = 64
GROUPS_PER_TILE = 8
HEAD_DIM = 128
N_KV_HEADS = 4
PATTERNS = ((128, 1), (512, 4), (2048, 16))
ATT_BLK = 128
EPS = 1e-6
NEG_INF = -1e30
SCAN_T = 256
ADAM_LR = 0.001
ADAM_B1 = 0.9
ADAM_B2 = 0.999
ADAM_EPS = 1e-08
ADAM_WD = 0.01
ADAM_STEP = 10
N_CHIPS = 4
N_DEV = 8
VMEM_LIMIT = 56 * 1024 * 1024
MESH = pl.DeviceIdType.MESH
GELU_K = math.sqrt(2.0 / math.pi)
GELU_C = 0.044715


def _tile(n, pref, unit=128):
    if n <= pref:
        return n
    best = None
    t = unit
    while t <= pref:
        if n % t == 0:
            best = t
        t += unit
    assert best is not None, (n, pref, unit)
    return best


def _params(sem):
    return pltpu.CompilerParams(dimension_semantics=sem, vmem_limit_bytes=VMEM_LIMIT)


def _dot(a, b):
    return jnp.dot(a, b, preferred_element_type=F32)


def _dot_nt(a, b):
    return lax.dot_general(a, b, (((1,), (1,)), ((), ())), preferred_element_type=F32)


def _dot_tn(a, b):
    return lax.dot_general(a, b, (((0,), (0,)), ((), ())), preferred_element_type=F32)


def _mm_nn(a, w, out_dtype, name, tm=512, tn=1536, tk=2048):
    m, k = a.shape
    nb, k2, nq = w.shape
    assert k == k2
    tm, tn, tk = _tile(m, tm, 8), _tile(nq, tn), _tile(k, tk)
    per, nk = nq // tn, k // tk

    def kern(a_ref, w_ref, o_ref, *acc):
        p = _dot(a_ref[...], w_ref[...])
        if nk == 1:
            o_ref[...] = p.astype(o_ref.dtype)
        else:
            acc_ref, = acc
            kk = pl.program_id(2)

            @pl.when(kk == 0)
            def _():
                acc_ref[...] = p

            @pl.when(kk > 0)
            def _():
                acc_ref[...] += p

            @pl.when(kk == nk - 1)
            def _():
                o_ref[...] = acc_ref[...].astype(o_ref.dtype)

    return pl.pallas_call(
        kern, name=name, grid=(nb * per, m // tm, nk),
        in_specs=[pl.BlockSpec((tm, tk), lambda j, i, kk: (i, kk)),
                  pl.BlockSpec((None, tk, tn), lambda j, i, kk: (j // per, kk, j % per))],
        out_specs=pl.BlockSpec((tm, tn), lambda j, i, kk: (i, j)),
        out_shape=jax.ShapeDtypeStruct((m, nb * nq), out_dtype),
        scratch_shapes=[] if nk == 1 else [pltpu.VMEM((tm, tn), F32)],
        compiler_params=_params(("parallel", "parallel", "arbitrary")),
    )(a, w)


def _paired_block(r, per, nb):
    j = r // 2
    return (r % 2) * (nb // 2) + j // per, j % per


def _mm_nt(a, w, name, tm=512, tr=1536, tkc=1024, paired=False):
    m, n = a.shape
    nb, k, nq = w.shape
    assert n == nb * nq
    tm, tr, tkc = _tile(m, tm, 8), _tile(nq, tr), _tile(k, tkc)
    per = nq // tr

    def kern(a_ref, w_ref, o_ref):
        acc = None
        for r in range(nb * per):
            blk, tile = _paired_block(r, per, nb) if paired else (r // per, r % per)
            p = _dot_nt(a_ref[:, r * tr:(r + 1) * tr], w_ref[blk, :, tile * tr:(tile + 1) * tr])
            acc = p if acc is None else acc + p
        o_ref[...] = acc

    return pl.pallas_call(
        kern, name=name, grid=(k // tkc, m // tm),
        in_specs=[pl.BlockSpec((tm, n), lambda kc, i: (i, 0)),
                  pl.BlockSpec((nb, tkc, nq), lambda kc, i: (0, kc, 0))],
        out_specs=pl.BlockSpec((tm, tkc), lambda kc, i: (i, kc)),
        out_shape=jax.ShapeDtypeStruct((m, k), F32),
        compiler_params=_params(("parallel", "parallel")),
    )(a, w)


def _mm_tn(a, dy, nb, name, tkk=512, tn=1536, paired=False):
    s, k = a.shape
    s2, n = dy.shape
    assert s == s2 and n % nb == 0
    nq = n // nb
    tkk, tn = _tile(k, tkk), _tile(nq, tn)
    per = nq // tn

    def w_block(j):
        return _paired_block(j, per, nb) if paired else (j // per, j % per)

    def kern(a_ref, dy_ref, o_ref):
        o_ref[...] = _dot_tn(a_ref[...], dy_ref[...]).astype(o_ref.dtype)

    return pl.pallas_call(
        kern, name=name, grid=(nb * per, k // tkk),
        in_specs=[pl.BlockSpec((s, tkk), lambda j, kk: (0, kk)),
                  pl.BlockSpec((s, tn), lambda j, kk: (0, j))],
        out_specs=pl.BlockSpec((None, tkk, tn), lambda j, kk: (w_block(j)[0], kk, w_block(j)[1])),
        out_shape=jax.ShapeDtypeStruct((nb, k, nq), BF16),
        compiler_params=_params(("parallel", "parallel")),
    )(a, dy)


ROWS = 256


def _rms(x, g):
    r = lax.rsqrt(jnp.mean(x * x, axis=-1, keepdims=True) + EPS)
    return x * r * g


def _rms_bwd(x, g, dh):
    r = lax.rsqrt(jnp.mean(x * x, axis=-1, keepdims=True) + EPS)
    xh = x * r
    dgx = dh * g
    dx = r * (dgx - xh * jnp.mean(dgx * xh, axis=-1, keepdims=True))
    return dx, dh * xh


def _sigmoid(x):
    return 1.0 / (1.0 + jnp.exp(-x))


def _gelu(y):
    return 0.5 * y * (1.0 + jnp.tanh(GELU_K * (y + GELU_C * y * y * y)))


def _gelu_grad(y):
    t = jnp.tanh(GELU_K * (y + GELU_C * y * y * y))
    return 0.5 * (1.0 + t) + 0.5 * y * (1.0 - t * t) * GELU_K * (1.0 + 3.0 * GELU_C * y * y)


def _row_spec(tm, d, col=0):
    return pl.BlockSpec((tm, d), lambda i: (i, col))


def _vec_spec(d):
    return pl.BlockSpec((1, d), lambda i: (0, 0))


def _acc_rows(ref, val, i):
    s = jnp.sum(val, axis=0, keepdims=True)

    @pl.when(i == 0)
    def _():
        ref[...] = s

    @pl.when(i > 0)
    def _():
        ref[...] += s


def _norm_f32(x, g, name):
    s, d = x.shape
    tm = _tile(s, ROWS, 8)

    def kern(x_ref, g_ref, h_ref):
        h_ref[...] = _rms(x_ref[...], g_ref[...])

    return pl.pallas_call(
        kern, name=name, grid=(s // tm,),
        in_specs=[_row_spec(tm, d), _vec_spec(d)],
        out_specs=_row_spec(tm, d),
        out_shape=jax.ShapeDtypeStruct((s, d), F32),
        compiler_params=_params(("parallel",)),
    )(x, g)


def _glu_res_norm(vg, x, g, name):
    s, d = x.shape
    tm = _tile(s, ROWS, 16)

    def kern(val_ref, gate_ref, x_ref, g_ref, x1_ref, hf_ref):
        x1 = x_ref[...] + val_ref[...] * _sigmoid(gate_ref[...])
        x1_ref[...] = x1
        hf_ref[...] = _rms(x1, g_ref[...]).astype(BF16)

    return pl.pallas_call(
        kern, name=name, grid=(s // tm,),
        in_specs=[_row_spec(tm, d, 0), _row_spec(tm, d, 1), _row_spec(tm, d), _vec_spec(d)],
        out_specs=[_row_spec(tm, d), _row_spec(tm, d)],
        out_shape=[jax.ShapeDtypeStruct((s, d), F32), jax.ShapeDtypeStruct((s, d), BF16)],
        compiler_params=_params(("parallel",)),
    )(vg, vg, x, g)


FFN_TILE = 1408


def _ffn_in_act(hf, w_in, name, tm=512):
    s, k = hf.shape
    nb, _, nq = w_in.shape
    tm, tn = _tile(s, tm, 16), _tile(nq, FFN_TILE)
    per = nq // tn
    nf = (nb // 2) * per

    def kern(h_ref, wg_ref, wu_ref, a_ref, gu_ref):
        h = h_ref[...]
        g = _dot(h, wg_ref[...])
        u = _dot(h, wu_ref[...])
        a_ref[...] = (g * _sigmoid(g) * u).astype(BF16)
        gu_ref[:, :tn] = g.astype(BF16)
        gu_ref[:, tn:] = u.astype(BF16)

    return pl.pallas_call(
        kern, name=name, grid=(nf, s // tm),
        in_specs=[pl.BlockSpec((tm, k), lambda j, i: (i, 0)),
                  pl.BlockSpec((None, k, tn), lambda j, i: (j // per, 0, j % per)),
                  pl.BlockSpec((None, k, tn), lambda j, i: (nb // 2 + j // per, 0, j % per))],
        out_specs=[pl.BlockSpec((tm, tn), lambda j, i: (i, j)),
                   pl.BlockSpec((tm, 2 * tn), lambda j, i: (i, j))],
        out_shape=[jax.ShapeDtypeStruct((s, nf * tn), BF16), jax.ShapeDtypeStruct((s, 2 * nf * tn), BF16)],
        compiler_params=_params(("parallel", "parallel")),
    )(hf, w_in, w_in)


def _ffn_out_dx_act(dresb, w_out, gu, name, tm=512):
    s, d = dresb.shape
    f = w_out.shape[1]
    tm = _tile(s, tm, 16)
    tn = _tile(f // 2, FFN_TILE)

    def kern(d_ref, w_ref, gu_ref, o_ref):
        da = _dot_nt(d_ref[...], w_ref[...])
        g = gu_ref[:, :tn].astype(F32)
        u = gu_ref[:, tn:].astype(F32)
        sg = _sigmoid(g)
        o_ref[:, :tn] = (da * u * sg * (1.0 + g * (1.0 - sg))).astype(BF16)
        o_ref[:, tn:] = (da * g * sg).astype(BF16)

    pair = pl.BlockSpec((tm, 2 * tn), lambda j, i: (i, j))
    return pl.pallas_call(
        kern, name=name, grid=(f // tn, s // tm),
        in_specs=[pl.BlockSpec((tm, d), lambda j, i: (i, 0)),
                  pl.BlockSpec((None, tn, d), lambda j, i: (0, j, 0)), pair],
        out_specs=pair,
        out_shape=jax.ShapeDtypeStruct((s, 2 * f), BF16),
        compiler_params=_params(("parallel", "parallel")),
    )(dresb, w_out, gu)


def _res_norm(o, x, gains, name):
    s, d = x.shape
    tm = _tile(s, ROWS, 16)
    ng = len(gains)

    def kern(o_ref, x_ref, *refs):
        xn = x_ref[...] + o_ref[...]
        refs[ng][...] = xn
        for t in range(ng):
            refs[ng + 1 + t][...] = _rms(xn, refs[t][...]).astype(BF16)

    return pl.pallas_call(
        kern, name=name, grid=(s // tm,),
        in_specs=[_row_spec(tm, d), _row_spec(tm, d)] + [_vec_spec(d)] * ng,
        out_specs=[_row_spec(tm, d)] * (1 + ng),
        out_shape=[jax.ShapeDtypeStruct((s, d), F32)] + [jax.ShapeDtypeStruct((s, d), BF16)] * ng,
        compiler_params=_params(("parallel",)),
    )(o, x, *gains)


def _loss_head(o, x, g, target, name):
    s, d = x.shape
    tm = _tile(s, ROWS, 16)

    def kern(o_ref, x_ref, g_ref, t_ref, dx_ref, dxb_ref, loss_ref, dg_ref):
        i = pl.program_id(0)
        x4 = x_ref[...] + o_ref[...]
        gg = g_ref[...]
        diff = _rms(x4, gg) - t_ref[...]
        dx, dgr = _rms_bwd(x4, gg, diff * (1.0 / d))
        dx_ref[...] = dx
        dxb_ref[...] = dx.astype(BF16)
        _acc_rows(loss_ref, diff * diff * (0.5 / d), i)
        _acc_rows(dg_ref, dgr, i)

    return pl.pallas_call(
        kern, name=name, grid=(s // tm,),
        in_specs=[_row_spec(tm, d), _row_spec(tm, d), _vec_spec(d), _row_spec(tm, d)],
        out_specs=[_row_spec(tm, d), _row_spec(tm, d), _vec_spec(d), _vec_spec(d)],
        out_shape=[jax.ShapeDtypeStruct((s, d), F32), jax.ShapeDtypeStruct((s, d), BF16),
                   jax.ShapeDtypeStruct((1, d), F32), jax.ShapeDtypeStruct((1, d), F32)],
        compiler_params=_params(("arbitrary",)),
    )(o, x, g, target)


def _norm_bwd(x, gains, dhs, dres, name):
    s, d = x.shape
    tm = _tile(s, ROWS, 16)
    ng = len(gains)

    def kern(x_ref, dres_ref, *refs):
        i = pl.program_id(0)
        x_ = x_ref[...]
        acc = dres_ref[...]
        for t in range(ng):
            dx, dgr = _rms_bwd(x_, refs[t][...], refs[ng + t][...])
            acc = acc + dx
            _acc_rows(refs[2 * ng + 2 + t], dgr, i)
        refs[2 * ng][...] = acc
        refs[2 * ng + 1][...] = acc.astype(BF16)

    return pl.pallas_call(
        kern, name=name, grid=(s // tm,),
        in_specs=[_row_spec(tm, d), _row_spec(tm, d)] + [_vec_spec(d)] * ng + [_row_spec(tm, d)] * ng,
        out_specs=[_row_spec(tm, d), _row_spec(tm, d)] + [_vec_spec(d)] * ng,
        out_shape=[jax.ShapeDtypeStruct((s, d), F32), jax.ShapeDtypeStruct((s, d), BF16)]
        + [jax.ShapeDtypeStruct((1, d), F32)] * ng,
        compiler_params=_params(("arbitrary",)),
    )(x, dres, *gains, *dhs)


def _glu_bwd(dmix, vg, name):
    s, d = dmix.shape
    tm = _tile(s, ROWS, 16)

    def kern(dm_ref, val_ref, gate_ref, o_ref):
        dm = dm_ref[...]
        sg = _sigmoid(gate_ref[...])
        o_ref[:, :d] = (dm * sg).astype(BF16)
        o_ref[:, d:] = (dm * val_ref[...] * sg * (1.0 - sg)).astype(BF16)

    return pl.pallas_call(
        kern, name=name, grid=(s // tm,),
        in_specs=[_row_spec(tm, d), _row_spec(tm, d, 0), _row_spec(tm, d, 1)],
        out_specs=_row_spec(tm, 2 * d),
        out_shape=jax.ShapeDtypeStruct((s, 2 * d), BF16),
        compiler_params=_params(("parallel",)),
    )(dmix, vg, vg)


SUB = 8


def _local_scan(vr, vi, pre_ref, pim_ref, reverse):
    sub = lax.broadcasted_iota(jnp.int32, vr.shape, 1)
    sign = -1.0 if reverse else 1.0
    for sh in (1, 2, 4):
        ar = pre_ref[sh - 1:sh, :][None]
        ai = sign * pim_ref[sh - 1:sh, :][None]
        keep = sub < SUB - sh if reverse else sub >= sh
        sr = jnp.where(keep, pltpu.roll(vr, SUB - sh if reverse else sh, 1), 0.0)
        si = jnp.where(keep, pltpu.roll(vi, SUB - sh if reverse else sh, 1), 0.0)
        vr, vi = vr + ar * sr - ai * si, vi + ar * si + ai * sr
    return vr, vi


def _s5_fwd(u, bbd_re, bbd_im, cbd_re, cbd_im, pw_re, pw_im, dskip, name):
    s, d = u.shape
    nt, cw, lw = bbd_re.shape
    t = _tile(s, SCAN_T, SUB)
    nc = s // t
    ng = t // SUB

    def kern(u_ref, bre_ref, bim_ref, cre_ref, cim_ref, pre_ref, pim_ref, d_ref,
             xr_ref, xi_ref, y_ref, z_ref, car_ref, cai_ref):
        c = pl.program_id(1)

        @pl.when(c == 0)
        def _():
            car_ref[...] = jnp.zeros_like(car_ref)
            cai_ref[...] = jnp.zeros_like(cai_ref)

        u_ = u_ref[...]
        ub = u_.astype(BF16)
        vr, vi = _local_scan(_dot(ub, bre_ref[...]).reshape(ng, SUB, lw),
                             _dot(ub, bim_ref[...]).reshape(ng, SUB, lw), pre_ref, pim_ref, False)
        cr = car_ref[...]
        ci = cai_ref[...]
        pr = pre_ref[...]
        pi = pim_ref[...]
        for gidx in range(ng):
            rows = slice(gidx * SUB, (gidx + 1) * SUB)
            gr = vr[gidx] + pr * cr - pi * ci
            gi = vi[gidx] + pr * ci + pi * cr
            xr_ref[rows, :] = gr
            xi_ref[rows, :] = gi
            cr, ci = gr[SUB - 1:SUB, :], gi[SUB - 1:SUB, :]
        car_ref[...] = cr
        cai_ref[...] = ci
        y = (_dot(xr_ref[...].astype(BF16), cre_ref[...]) - _dot(xi_ref[...].astype(BF16), cim_ref[...])
             + d_ref[...] * u_)
        y_ref[...] = y
        z_ref[...] = _gelu(y).astype(BF16)

    tok = pl.BlockSpec((t, cw), lambda j, c: (c, j))
    st = pl.BlockSpec((t, lw), lambda j, c: (c, j))
    return pl.pallas_call(
        kern, name=name, grid=(nt, nc),
        in_specs=[tok,
                  pl.BlockSpec((None, cw, lw), lambda j, c: (j, 0, 0)),
                  pl.BlockSpec((None, cw, lw), lambda j, c: (j, 0, 0)),
                  pl.BlockSpec((None, lw, cw), lambda j, c: (j, 0, 0)),
                  pl.BlockSpec((None, lw, cw), lambda j, c: (j, 0, 0)),
                  pl.BlockSpec((SUB, lw), lambda j, c: (0, j)),
                  pl.BlockSpec((SUB, lw), lambda j, c: (0, j)),
                  pl.BlockSpec((1, cw), lambda j, c: (0, j))],
        out_specs=[st, st, tok, tok],
        out_shape=[jax.ShapeDtypeStruct((s, nt * lw), F32), jax.ShapeDtypeStruct((s, nt * lw), F32),
                   jax.ShapeDtypeStruct((s, d), F32), jax.ShapeDtypeStruct((s, d), BF16)],
        scratch_shapes=[pltpu.VMEM((1, lw), F32), pltpu.VMEM((1, lw), F32)],
        compiler_params=_params(("parallel", "arbitrary")),
    )(u, bbd_re, bbd_im, cbd_re, cbd_im, pw_re, pw_im, dskip)


def _s5_bwd(dz, y, u, xs_re, xs_im, bbd_re, bbd_im, cbd_re, cbd_im, pw_re, pw_im, pf_re, pf_im, dskip, name):
    s, d = u.shape
    nt, cw, lw = bbd_re.shape
    t = _tile(s, SCAN_T, SUB)
    nc = s // t
    ng = t // SUB

    def kern(dz_ref, y_ref, u_ref, xr_ref, xi_ref, bre_ref, bim_ref, cre_ref, cim_ref,
             pre_ref, pim_ref, fre_ref, fim_ref, d_ref,
             du_ref, dd_ref, dcr_ref, dci_ref, dbr_ref, dbi_ref, dar_ref, dai_ref,
             car_ref, cai_ref, gr_ref, gi_ref):
        c = pl.program_id(1)

        @pl.when(c == 0)
        def _():
            car_ref[...] = jnp.zeros_like(car_ref)
            cai_ref[...] = jnp.zeros_like(cai_ref)

        u_ = u_ref[...]
        ub = u_.astype(BF16)
        dy = dz_ref[...] * _gelu_grad(y_ref[...])
        dyb = dy.astype(BF16)
        vr, vi = _local_scan(_dot_nt(dyb, cre_ref[...]).reshape(ng, SUB, lw),
                             (-_dot_nt(dyb, cim_ref[...])).reshape(ng, SUB, lw), pre_ref, pim_ref, True)
        later_r = car_ref[...]
        later_i = cai_ref[...]
        cr, ci = later_r, later_i
        fr = fre_ref[...]
        fi = fim_ref[...]
        for gidx in reversed(range(ng)):
            rows = slice(gidx * SUB, (gidx + 1) * SUB)
            ar = vr[gidx] + fr * cr + fi * ci
            ai = vi[gidx] + fr * ci - fi * cr
            gr_ref[rows, :] = ar
            gi_ref[rows, :] = ai
            cr, ci = ar[0:1, :], ai[0:1, :]
        car_ref[...] = cr
        cai_ref[...] = ci
        gr = gr_ref[...]
        gi = gi_ref[...]
        rows = lax.broadcasted_iota(jnp.int32, (t, lw), 0)
        gsr = jnp.where(rows < t - 1, pltpu.roll(gr, t - 1, 0), later_r)
        gsi = jnp.where(rows < t - 1, pltpu.roll(gi, t - 1, 0), later_i)
        xr = xr_ref[...]
        xi = xi_ref[...]
        dar = jnp.sum(gsr * xr + gsi * xi, axis=0, keepdims=True)
        dai = jnp.sum(gsi * xr - gsr * xi, axis=0, keepdims=True)
        grb = gr.astype(BF16)
        gib = gi.astype(BF16)
        dbr = _dot_tn(ub, grb)
        dbi = _dot_tn(ub, gib)
        dcr = _dot_tn(dyb, xr.astype(BF16))
        dci = _dot_tn(dyb, xi.astype(BF16))
        du_ref[...] = dy * d_ref[...] + _dot_nt(grb, bre_ref[...]) + _dot_nt(gib, bim_ref[...])
        ddv = jnp.sum(dy * u_, axis=0, keepdims=True)

        @pl.when(c == 0)
        def _():
            dd_ref[...] = ddv
            dcr_ref[...] = dcr
            dci_ref[...] = dci
            dbr_ref[...] = dbr
            dbi_ref[...] = dbi
            dar_ref[...] = dar
            dai_ref[...] = dai

        @pl.when(c > 0)
        def _():
            dd_ref[...] += ddv
            dcr_ref[...] += dcr
            dci_ref[...] += dci
            dbr_ref[...] += dbr
            dbi_ref[...] += dbi
            dar_ref[...] += dar
            dai_ref[...] += dai

    tok = pl.BlockSpec((t, cw), lambda j, c: (nc - 1 - c, j))
    st = pl.BlockSpec((t, lw), lambda j, c: (nc - 1 - c, j))
    wb = pl.BlockSpec((None, cw, lw), lambda j, c: (j, 0, 0))
    wc = pl.BlockSpec((None, lw, cw), lambda j, c: (j, 0, 0))
    pw = pl.BlockSpec((SUB, lw), lambda j, c: (0, j))
    vec_c = pl.BlockSpec((1, cw), lambda j, c: (0, j))
    vec_l = pl.BlockSpec((1, lw), lambda j, c: (0, j))
    return pl.pallas_call(
        kern, name=name, grid=(nt, nc),
        in_specs=[tok, tok, tok, st, st, wb, wb, wc, wc, pw, pw, pw, pw, vec_c],
        out_specs=[tok, vec_c, wb, wb, wb, wb, vec_l, vec_l],
        out_shape=[jax.ShapeDtypeStruct((s, d), F32), jax.ShapeDtypeStruct((1, d), F32)]
        + [jax.ShapeDtypeStruct((nt, cw, lw), F32)] * 4
        + [jax.ShapeDtypeStruct((1, nt * lw), F32)] * 2,
        scratch_shapes=[pltpu.VMEM((1, lw), F32), pltpu.VMEM((1, lw), F32),
                        pltpu.VMEM((t, lw), F32), pltpu.VMEM((t, lw), F32)],
        compiler_params=_params(("parallel", "arbitrary")),
    )(dz, y, u, xs_re, xs_im, bbd_re, bbd_im, cbd_re, cbd_im, pw_re, pw_im, pf_re, pf_im, dskip)


def _s5_discretize(lam_re, lam_im, log_dt, b_re, b_im):
    dt = jnp.exp(log_dt)[:, None]
    mag = jnp.exp(lam_re * dt)
    ang = lam_im * dt
    lb_re = mag * jnp.cos(ang)
    lb_im = mag * jnp.sin(ang)
    nr = lb_re - 1.0
    den = lam_re * lam_re + lam_im * lam_im
    f_re = (nr * lam_re + lb_im * lam_im) / den
    f_im = (lb_im * lam_re - nr * lam_im) / den
    bb_re = f_re[..., None] * b_re - f_im[..., None] * b_im
    bb_im = f_re[..., None] * b_im + f_im[..., None] * b_re
    return lb_re, lb_im, bb_re, bb_im


def _block_diag(w):
    nt, ng, a, b = w.shape
    eye = jnp.eye(ng, dtype=w.dtype)
    return (w[:, :, :, None, :] * eye[None, :, None, :, None]).reshape(nt, ng * a, ng * b)


def _block_diag_take(w, a, b):
    nt = w.shape[0]
    ng = GROUPS_PER_TILE
    w5 = w.reshape(nt, ng, a, ng, b)
    on_diagonal = jnp.eye(ng, dtype=bool)[None, :, None, :, None]
    return jnp.sum(jnp.where(on_diagonal, w5, 0.0), axis=3).reshape(nt * ng, a, b)


def _att_combine(outs, lses, name):
    s, d = outs[0].shape
    tm = _tile(s, ROWS, 16)
    ng = len(outs)

    def kern(*refs):
        for h in range(d // HEAD_DIM):
            cols = slice(h * HEAD_DIM, (h + 1) * HEAD_DIM)
            ls = [refs[ng + t][:, h * HEAD_DIM:h * HEAD_DIM + 1] for t in range(ng)]
            mx = functools.reduce(jnp.maximum, ls)
            es = [jnp.exp(l - mx) for l in ls]
            den = functools.reduce(lambda a, b: a + b, es)
            o = functools.reduce(lambda a, b: a + b, [es[t] / den * refs[t][:, cols] for t in range(ng)])
            refs[2 * ng][:, cols] = o
            refs[2 * ng + 1][:, cols] = o.astype(BF16)

    return pl.pallas_call(
        kern, name=name, grid=(s // tm,),
        in_specs=[_row_spec(tm, d)] * (2 * ng),
        out_specs=[_row_spec(tm, d)] * 2,
        out_shape=[jax.ShapeDtypeStruct((s, d), F32), jax.ShapeDtypeStruct((s, d), BF16)],
        compiler_params=_params(("parallel",)),
    )(*outs, *lses)


DIL_MAX = 16


def _slab(r):
    return 4 * (r % 4) + r // 4


def _to_slabs(xs, name):
    s, w = xs[0].shape
    n = s // DIL_MAX
    nx = len(xs)

    def kern(*refs):
        o_ref = refs[nx]
        for r in range(DIL_MAX):
            rows = [x_ref[pl.ds(r, n, stride=DIL_MAX), :] for x_ref in refs[:nx]]
            o_ref[_slab(r) * n:(_slab(r) + 1) * n, :] = functools.reduce(lambda a, b: a + b, rows)

    spec = pl.BlockSpec((s, 128), lambda i: (0, i))
    return pl.pallas_call(
        kern, name=name, grid=(w // 128,), in_specs=[spec] * nx, out_specs=spec,
        out_shape=jax.ShapeDtypeStruct((s, w), F32), compiler_params=_params(("parallel",)),
    )(*xs)


def _from_slabs(x, name):
    s, w = x.shape
    n = s // DIL_MAX

    def kern(x_ref, o_ref, ob_ref):
        for r in range(DIL_MAX):
            o_ref[pl.ds(r, n, stride=DIL_MAX), :] = x_ref[_slab(r) * n:(_slab(r) + 1) * n, :]
        ob_ref[...] = o_ref[...].astype(BF16)

    spec = pl.BlockSpec((s, 128), lambda i: (0, i))
    return pl.pallas_call(
        kern, name=name, grid=(w // 128,), in_specs=[spec], out_specs=[spec, spec],
        out_shape=[jax.ShapeDtypeStruct((s, w), F32), jax.ShapeDtypeStruct((s, w), BF16)],
        compiler_params=_params(("parallel",)),
    )(x)


def _norm_bf16(x, gains, name):
    s, d = x.shape
    tm = _tile(s, ROWS, 16)
    ng = len(gains)

    def kern(x_ref, *refs):
        x_ = x_ref[...]
        for t in range(ng):
            refs[ng + t][...] = _rms(x_, refs[t][...]).astype(BF16)

    return pl.pallas_call(
        kern, name=name, grid=(s // tm,),
        in_specs=[_row_spec(tm, d)] + [_vec_spec(d)] * ng, out_specs=[_row_spec(tm, d)] * ng,
        out_shape=[jax.ShapeDtypeStruct((s, d), BF16)] * ng,
        compiler_params=_params(("parallel",)),
    )(x, *gains)


def _att_geometry(dil, s):
    g = DIL_MAX // dil
    return s // DIL_MAX, g, max(ATT_BLK // g, 16)


def _att_mask(g, j, jb, rep):
    b = g * j

    def pos(i):
        sl, jj = i // j, i % j
        off = {1: 0, 4: sl, 16: sl // 4 + 4 * (sl % 4)}[g]
        return g * jj + off

    qi = lax.broadcasted_iota(jnp.int32, (rep * b, 2 * b), 0) % b
    ki = lax.broadcasted_iota(jnp.int32, (rep * b, 2 * b), 1)
    prev = ki < b
    dist = pos(qi) - pos(ki % b) + jnp.where(prev, b, 0)
    return (dist >= 0) & (dist <= ATT_BLK) & (jnp.logical_not(prev) | (jb > 0))


def _stack_heads(x, rep):
    return jnp.concatenate([x[:, h * HEAD_DIM:(h + 1) * HEAD_DIM] for h in range(rep)], axis=0)


def _att_fwd(q, kv, gi, dil, nh, name):
    s = q.shape[0]
    n, g, j = _att_geometry(dil, s)
    b = g * j
    ng = q.shape[1] // (nh * HEAD_DIM)
    rep = nh // N_KV_HEADS
    qw = rep * HEAD_DIM
    scale = HEAD_DIM ** -0.5
    q3 = q.reshape(DIL_MAX, n, q.shape[1])
    kv3 = kv.reshape(DIL_MAX, n, kv.shape[1])

    kw = N_KV_HEADS * HEAD_DIM

    def kern(q_ref, kc_ref, kp_ref, vc_ref, vp_ref, o_ref, l_ref):
        jb = pl.program_id(1)
        mask = _att_mask(g, j, jb, rep)
        for kh in range(N_KV_HEADS):
            hs = slice(kh * HEAD_DIM, (kh + 1) * HEAD_DIM)
            kc, kp, vc, vp = [r[:, :, hs].reshape(b, HEAD_DIM) for r in (kc_ref, kp_ref, vc_ref, vp_ref)]
            k2 = jnp.concatenate([kp, kc], axis=0)
            v2 = jnp.concatenate([vp, vc], axis=0)
            qs = _stack_heads(q_ref[:, :, kh * qw:(kh + 1) * qw].reshape(b, qw), rep)
            sc = jnp.where(mask, _dot_nt(qs, k2) * scale, NEG_INF)
            m = jnp.max(sc, axis=-1, keepdims=True)
            p = jnp.exp(sc - m)
            l = jnp.sum(p, axis=-1, keepdims=True)
            out = _dot((p / l).astype(BF16), v2)
            lse = jnp.broadcast_to(m + jnp.log(l), (rep * b, HEAD_DIM))
            for h in range(rep):
                cols = slice(kh * qw + h * HEAD_DIM, kh * qw + (h + 1) * HEAD_DIM)
                o_ref[:, :, cols] = out[h * b:(h + 1) * b].reshape(g, j, HEAD_DIM)
                l_ref[:, :, cols] = lse[h * b:(h + 1) * b].reshape(g, j, HEAD_DIM)

    def kv_spec(col, prev):
        if prev:
            return pl.BlockSpec((g, j, kw), lambda c, jb: (c, jnp.maximum(jb - 1, 0), col))
        return pl.BlockSpec((g, j, kw), lambda c, jb: (c, jb, col))

    out_spec = pl.BlockSpec((g, j, nh * HEAD_DIM), lambda c, jb: (c, jb, 0))
    out, lse = pl.pallas_call(
        kern, name=name, grid=(dil, n // j),
        in_specs=[pl.BlockSpec((g, j, nh * HEAD_DIM), lambda c, jb: (c, jb, gi)),
                  kv_spec(0, False), kv_spec(0, True), kv_spec(1, False), kv_spec(1, True)],
        out_specs=[out_spec, out_spec],
        out_shape=[jax.ShapeDtypeStruct((DIL_MAX, n, nh * HEAD_DIM), F32)] * 2,
        compiler_params=_params(("parallel", "arbitrary")),
    )(q3, kv3, kv3, kv3, kv3)
    del ng
    return out.reshape(s, nh * HEAD_DIM), lse.reshape(s, nh * HEAD_DIM)


def _att_bwd(q, kv, do, o, lses, dq_all, gi, dil, nh, name):
    s = q.shape[0]
    n, g, j = _att_geometry(dil, s)
    b = g * j
    nb = n // j
    ng = len(lses)
    rep = nh // N_KV_HEADS
    qw = rep * HEAD_DIM
    dm = nh * HEAD_DIM
    scale = HEAD_DIM ** -0.5
    q3 = q.reshape(DIL_MAX, n, q.shape[1])
    kv3 = kv.reshape(DIL_MAX, n, kv.shape[1])
    wide = [a.reshape(DIL_MAX, n, dm) for a in (do, o, *lses)]
    kw = N_KV_HEADS * HEAD_DIM

    def kern(q_ref, kc_ref, kp_ref, vc_ref, vp_ref, do_ref, o_ref, *refs):
        l_refs = refs[:ng]
        dq_ref, dk_ref, dv_ref, ck_ref, cv_ref = refs[ng + 1:]
        jb = pl.program_id(1)

        @pl.when(jb == 0)
        def _():
            ck_ref[...] = jnp.zeros_like(ck_ref)
            cv_ref[...] = jnp.zeros_like(cv_ref)

        @pl.when(jb < nb)
        def _():
            mask = _att_mask(g, j, jb, rep)
            for kh in range(N_KV_HEADS):
                hs = slice(kh * HEAD_DIM, (kh + 1) * HEAD_DIM)
                ws = slice(kh * qw, (kh + 1) * qw)

                def stacked(ref):
                    return _stack_heads(ref[:, :, ws].reshape(b, qw), rep)

                kc, kp, vc, vp = [r[:, :, hs].reshape(b, HEAD_DIM) for r in (kc_ref, kp_ref, vc_ref, vp_ref)]
                k2 = jnp.concatenate([kp, kc], axis=0)
                v2 = jnp.concatenate([vp, vc], axis=0)
                qs = stacked(q_ref)
                ls = [stacked(r)[:, 0:1] for r in l_refs]
                mx = functools.reduce(jnp.maximum, ls)
                den = functools.reduce(lambda a, c: a + c, [jnp.exp(l - mx) for l in ls])
                lse_g = ls[gi]
                w = jnp.exp(lse_g - mx) / den
                do_ = stacked(do_ref)
                ct = w * jnp.sum(do_ * stacked(o_ref), axis=-1, keepdims=True)
                dob = (w * do_).astype(BF16)
                p = jnp.exp(jnp.where(mask, _dot_nt(qs, k2) * scale, NEG_INF) - lse_g)
                ds = (p * (_dot_nt(dob, v2) - ct) * scale).astype(BF16)
                dq = (_dot(ds, k2)).astype(BF16)
                for h in range(rep):
                    cols = slice(kh * qw + h * HEAD_DIM, kh * qw + (h + 1) * HEAD_DIM)
                    dq_ref[:, :, cols] = dq[h * b:(h + 1) * b].reshape(g, j, HEAD_DIM)
                dk2 = _dot_tn(ds, qs)
                dv2 = _dot_tn(p.astype(BF16), dob)
                dk_ref[:, :, hs] = (ck_ref[:, hs] + dk2[:b]).reshape(g, j, HEAD_DIM)
                dv_ref[:, :, hs] = (cv_ref[:, hs] + dv2[:b]).reshape(g, j, HEAD_DIM)
                ck_ref[:, hs] = dk2[b:]
                cv_ref[:, hs] = dv2[b:]

        @pl.when(jb == nb)
        def _():
            dk_ref[...] = ck_ref[...].reshape(g, j, kw)
            dv_ref[...] = cv_ref[...].reshape(g, j, kw)

    def jq(jb):
        return jnp.minimum(jb, nb - 1)

    def kv_spec(col, prev):
        if prev:
            return pl.BlockSpec((g, j, kw), lambda c, jb: (c, jnp.maximum(jq(jb) - 1, 0), col))
        return pl.BlockSpec((g, j, kw), lambda c, jb: (c, jq(jb), col))

    wide_spec = pl.BlockSpec((g, j, dm), lambda c, jb: (c, jq(jb), 0))
    dkv_spec = pl.BlockSpec((g, j, kw), lambda c, jb: (c, jnp.maximum(jb - 1, 0), 0))
    dq, dk, dv = pl.pallas_call(
        kern, name=name, grid=(dil, nb + 1),
        in_specs=[pl.BlockSpec((g, j, dm), lambda c, jb: (c, jq(jb), gi)),
                  kv_spec(0, False), kv_spec(0, True), kv_spec(1, False), kv_spec(1, True)]
        + [wide_spec] * (2 + ng) + [_ANY],
        out_specs=[pl.BlockSpec((g, j, dm), lambda c, jb: (c, jq(jb), gi)), dkv_spec, dkv_spec],
        out_shape=[jax.ShapeDtypeStruct((DIL_MAX, n, dq_all.shape[1]), BF16),
                   jax.ShapeDtypeStruct((DIL_MAX, n, kw), F32),
                   jax.ShapeDtypeStruct((DIL_MAX, n, kw), F32)],
        scratch_shapes=[pltpu.VMEM((b, kw), F32), pltpu.VMEM((b, kw), F32)],
        input_output_aliases={7 + ng: 0},
        compiler_params=_params(("parallel", "arbitrary")),
    )(q3, kv3, kv3, kv3, kv3, *wide, dq_all.reshape(DIL_MAX, n, dq_all.shape[1]))
    return dq.reshape(s, -1), dk.reshape(s, -1), dv.reshape(s, -1)


def _dkv_sum(dks, dvs, name):
    s, w = dks[0].shape
    tm = _tile(s, ROWS, 16)
    ng = len(dks)

    def kern(*refs):
        o_ref = refs[2 * ng]
        o_ref[:, :w] = functools.reduce(lambda a, b: a + b, [refs[t][...] for t in range(ng)]).astype(BF16)
        o_ref[:, w:] = functools.reduce(lambda a, b: a + b, [refs[ng + t][...] for t in range(ng)]).astype(BF16)

    return pl.pallas_call(
        kern, name=name, grid=(s // tm,),
        in_specs=[_row_spec(tm, w)] * (2 * ng),
        out_specs=_row_spec(tm, 2 * w),
        out_shape=jax.ShapeDtypeStruct((s, 2 * w), BF16),
        compiler_params=_params(("parallel",)),
    )(*dks, *dvs)


def _ffn_fwd(hf, weight, layer, tag):
    a, gu = _ffn_in_act(hf, weight(f"w_in{layer}", hf), f"{tag}_in")
    o = _mm_nn(a, weight(f"w_out{layer}", a), F32, f"{tag}_out", tn=1024, tk=5632)
    return gu, a, o


def _ffn_bwd(dresb, hf, gu, a, weight, emit, layer, tag):
    w_in, w_out = weight(f"w_in{layer}", None), weight(f"w_out{layer}", None)
    zero = emit(f"w_out{layer}", _mm_tn(a, dresb, 1, f"{tag}_out_dw", tn=1024))
    dgu = _ffn_out_dx_act(dresb, w_out, gu, f"{tag}_out_dx")
    zero = zero + emit(f"w_in{layer}",
                       _mm_tn(hf, dgu, w_in.shape[0], f"{tag}_in_dw", tn=FFN_TILE, paired=True))
    return _mm_nt(dgu, w_in, f"{tag}_in_dx", tr=FFN_TILE, tkc=512, paired=True), zero


def _local_step(x, target, small, weight, emit, stage):
    s, d = x.shape
    nh = d // HEAD_DIM
    n_groups = d // S5_GROUP_CH
    nt = n_groups // GROUPS_PER_TILE
    p_, c_ = S5_STATE, S5_GROUP_CH

    disc_in = (small["lam_re"], small["lam_im"], small["log_dt"], small["b_re"], small["b_im"])
    (lb_re, lb_im, bb_re, bb_im), disc_vjp = jax.vjp(_s5_discretize, *disc_in)
    del lb_re, lb_im
    dt = jnp.exp(small["log_dt"])[:, None]
    def pole_powers(exponents):
        k = exponents[:, None, None]
        mag = jnp.exp(k * (small["lam_re"] * dt)[None])
        ang = k * (small["lam_im"] * dt)[None]
        return ((mag * jnp.cos(ang)).reshape(SUB, n_groups * p_),
                (mag * jnp.sin(ang)).reshape(SUB, n_groups * p_))

    pw_re, pw_im = pole_powers(jnp.arange(1, SUB + 1, dtype=F32))
    pf_re, pf_im = pole_powers(jnp.arange(SUB, 0, -1, dtype=F32))
    bbd_re = _block_diag(bb_re.transpose(0, 2, 1).reshape(nt, GROUPS_PER_TILE, c_, p_)).astype(BF16)
    bbd_im = _block_diag(bb_im.transpose(0, 2, 1).reshape(nt, GROUPS_PER_TILE, c_, p_)).astype(BF16)
    cbd_re = _block_diag(small["c_re"].transpose(0, 2, 1).reshape(nt, GROUPS_PER_TILE, p_, c_)).astype(BF16)
    cbd_im = _block_diag(small["c_im"].transpose(0, 2, 1).reshape(nt, GROUPS_PER_TILE, p_, c_)).astype(BF16)

    h0 = _norm_f32(x, small["a_norm"], "s5_norm")
    xs_re, xs_im, y0, z = _s5_fwd(h0, bbd_re, bbd_im, cbd_re, cbd_im, pw_re, pw_im, small["s5_d"], "s5_fwd")
    zero = stage("glu", z)
    vg = _mm_nn(z, weight("w_glu", z), F32, "glu_mm", tn=1024)
    x1, hf0 = _glu_res_norm(vg, x, small["ffn_norm0"] + zero, "glu_res_norm")
    gu0, a0, o0 = _ffn_fwd(hf0, weight, 0, "ffn0")
    zero = stage("attention", a0)
    x2 = _to_slabs([o0, x1], "to_slabs")
    kvn, h1 = _norm_bf16(x2, [small["kv_norm"] + zero, small["b_norm"]], "att_norms")
    kv = _mm_nn(kvn, weight("w_kv", kvn), BF16, "kv_mm", tn=1024)
    q = _mm_nn(h1, weight("w_q", kv), BF16, "q_mm", tn=1536)
    outs, lses = [], []
    for gi, (window, dil) in enumerate(PATTERNS):
        assert window // dil == ATT_BLK
        og, lg = _att_fwd(q, kv, gi, dil, nh, f"att_fwd{gi}")
        outs.append(og)
        lses.append(lg)
    oatt, oattb = _att_combine(outs, lses, "att_combine")
    ao = _mm_nn(oattb, weight("w_o", oattb), F32, "o_mm", tn=1024)
    x3, hf1 = _res_norm(ao, x2, [small["ffn_norm1"] + stage("ffn1", oattb)], "att_res_norm")
    gu1, a1, o1 = _ffn_fwd(hf1, weight, 1, "ffn1")
    dres, dresb, loss_rows, d_final = _loss_head(
        o1, x3, small["final_norm"], _to_slabs([target], "target_to_slabs"), "loss_head")

    dhf1, zero = _ffn_bwd(dresb, hf1, gu1, a1, weight, emit, 1, "ffn1")
    dres, dresb, d_ffn_norm1 = _norm_bwd(x3, [small["ffn_norm1"] + zero], [dhf1], dres, "ffn1_norm_bwd")
    zero = emit("w_o", _mm_tn(oattb, dresb, 1, "o_dw", tn=1024))
    doatt = _mm_nt(dresb, weight("w_o", None), "o_dx", tr=2048)
    dks, dvs = [], []
    dq = lax.empty(q.shape, BF16)
    for gi, (window, dil) in enumerate(PATTERNS):
        dq, dk_g, dv_g = _att_bwd(q, kv, doatt, oatt, lses, dq, gi, dil, nh, f"att_bwd{gi}")
        dks.append(dk_g)
        dvs.append(dv_g)
    dkv = _dkv_sum(dks, dvs, "dkv_sum")
    w_q = weight("w_q", None)
    zero = zero + emit("w_q", _mm_tn(h1, dq, w_q.shape[0], "q_dw", tn=1536))
    zero = zero + emit("w_kv", _mm_tn(kvn, dkv, 1, "kv_dw", tn=1024))
    dh1 = _mm_nt(dq, w_q, "q_dx", tr=1536)
    dkvn = _mm_nt(dkv, weight("w_kv", None), "kv_dx", tr=1024)
    dres, dresb, d_b_norm, d_kv_norm = _norm_bwd(
        x2, [small["b_norm"] + zero, small["kv_norm"]], [dh1, dkvn], dres, "att_norm_bwd")
    dres, dresb = _from_slabs(dres, "from_slabs")
    dhf0, zero = _ffn_bwd(dresb, hf0, gu0, a0, weight, emit, 0, "ffn0")
    dres, dresb, d_ffn_norm0 = _norm_bwd(x1, [small["ffn_norm0"] + zero], [dhf0], dres, "ffn0_norm_bwd")
    del dresb
    dvg = _glu_bwd(dres, vg, "glu_bwd")
    w_glu = weight("w_glu", None)
    zero = emit("w_glu", _mm_tn(z, dvg, w_glu.shape[0], "glu_dw", tn=1024))
    dz = _mm_nt(dvg, w_glu, "glu_dx", tr=1024)
    dh0, d_s5_d, dcr, dci_neg, dbr, dbi, dar, dai = _s5_bwd(
        dz, y0, h0, xs_re, xs_im, bbd_re, bbd_im, cbd_re, cbd_im, pw_re, pw_im, pf_re, pf_im,
        small["s5_d"] + zero, "s5_bwd")
    grad_x, _, d_a_norm = _norm_bwd(x, [small["a_norm"]], [dh0], dres, "s5_norm_bwd")

    d_bb_re = _block_diag_take(dbr, c_, p_).transpose(0, 2, 1)
    d_bb_im = _block_diag_take(dbi, c_, p_).transpose(0, 2, 1)
    d_c_re = _block_diag_take(dcr, c_, p_)
    d_c_im = -_block_diag_take(dci_neg, c_, p_)
    d_lam_re, d_lam_im, d_log_dt, d_b_re, d_b_im = disc_vjp(
        (dar.reshape(n_groups, p_), dai.reshape(n_groups, p_), d_bb_re, d_bb_im))

    d_small = dict(lam_re=d_lam_re, lam_im=d_lam_im, log_dt=d_log_dt, b_re=d_b_re, b_im=d_b_im,
                   c_re=d_c_re, c_im=d_c_im, s5_d=d_s5_d, a_norm=d_a_norm, ffn_norm0=d_ffn_norm0,
                   ffn_norm1=d_ffn_norm1, b_norm=d_b_norm, kv_norm=d_kv_norm, final_norm=d_final)
    return loss_rows, grad_x, d_small


def _place():
    x, y, c = lax.axis_index("x"), lax.axis_index("y"), lax.axis_index("c")
    return x, y, c, [(1 - x, y), (x, 1 - y), (1 - x, 1 - y)]


_ANY = pl.BlockSpec(memory_space=pl.ANY)


_HBM = pl.BlockSpec(memory_space=pltpu.HBM)
_SEM = pl.BlockSpec(memory_space=pltpu.SEMAPHORE)
_EFFECT = pltpu.SideEffectType.DATAFLOW_SIDE_EFFECTING


def _in_hbm(a):
    return pltpu.with_memory_space_constraint(a, pltpu.HBM)


def _cast_place(shards, layer, chip, name):
    _, r, c = shards.shape
    tm = _tile(r, ROWS, 16)

    def kern(chip_ref, x_ref, o_ref):
        del chip_ref
        o_ref[...] = x_ref[...].astype(BF16)

    return pl.pallas_call(
        kern, name=name,
        grid_spec=pltpu.PrefetchScalarGridSpec(
            num_scalar_prefetch=1, grid=(r // tm,),
            in_specs=[pl.BlockSpec((None, tm, c), lambda i, ch: (layer, i, 0))],
            out_specs=pl.BlockSpec((None, tm, c), lambda i, ch: (ch[0], i, 0))),
        out_shape=jax.ShapeDtypeStruct((N_CHIPS, r, c), BF16),
        compiler_params=_params(("parallel",)),
    )(chip, shards)


def _my_part(land, block, c, halves):
    if not halves:
        return land.at[block]
    half = land.shape[1] // 2
    return land.at[block, pl.ds(c * half, half)]


def _gather_start(lands, name, halves):
    n = len(lands)

    def body(*refs):
        land = refs[:n]
        send, recv = refs[n:2 * n], refs[2 * n:3 * n]
        token = refs[4 * n]
        x, y, c, peers = _place()
        me = 2 * x + y
        for a in range(n):
            for k, (px, py) in enumerate(peers):
                part = _my_part(land[a], me, c, halves)
                pltpu.make_async_remote_copy(
                    src_ref=part, dst_ref=part, send_sem=send[a].at[k], recv_sem=recv[a].at[k],
                    device_id=(px, py, c), device_id_type=MESH).start()
        token[...] = jnp.zeros_like(token)

    outs = pl.pallas_call(
        body, name=name,
        out_shape=[pltpu.SemaphoreType.DMA((3,))] * (2 * n) + [pltpu.HBM(a.shape, a.dtype) for a in lands]
        + [jax.ShapeDtypeStruct((8, 128), F32)],
        in_specs=[_HBM] * n,
        out_specs=[_SEM] * (2 * n) + [_HBM] * n + [pl.BlockSpec(memory_space=pltpu.VMEM)],
        input_output_aliases={i: 2 * n + i for i in range(n)},
        compiler_params=pltpu.CompilerParams(has_side_effects=_EFFECT),
    )(*[_in_hbm(a) for a in lands])
    return outs[:n], outs[n:2 * n], outs[2 * n:3 * n], outs[3 * n]


def _gather_wait(lands, sends, recvs, after, name, halves):
    n = len(lands)

    def body(*refs):
        land, send, recv = refs[:n], refs[n:2 * n], refs[2 * n:3 * n]
        x, y, c, peers = _place()
        me = 2 * x + y
        for a in range(n):
            for k, (px, py) in enumerate(peers):
                cp = pltpu.make_async_remote_copy(
                    src_ref=_my_part(land[a], me, c, halves), dst_ref=_my_part(land[a], 2 * px + py, c, halves),
                    send_sem=send[a].at[k], recv_sem=recv[a].at[k], device_id=(px, py, c), device_id_type=MESH)
                cp.wait_send()
                cp.wait_recv()

    return pl.pallas_call(
        body, name=name,
        out_shape=[pltpu.HBM(a.shape, a.dtype) for a in lands],
        in_specs=[_HBM] * n + [_SEM] * (2 * n) + [_ANY], out_specs=[_HBM] * n,
        input_output_aliases={i: i for i in range(n)},
        compiler_params=pltpu.CompilerParams(has_side_effects=_EFFECT),
    )(*lands, *sends, *recvs, after)


def _forward_start(lands, name):
    n = len(lands)

    def body(*refs):
        land = refs[:n]
        send, recv = refs[n:2 * n], refs[2 * n:3 * n]
        token = refs[4 * n]
        x, y, c, peers = _place()
        for a in range(n):
            for k, (px, py) in enumerate(peers):
                part = _my_part(land[a], 2 * px + py, c, True)
                pltpu.make_async_remote_copy(
                    src_ref=part, dst_ref=part, send_sem=send[a].at[k], recv_sem=recv[a].at[k],
                    device_id=(x, y, 1 - c), device_id_type=MESH).start()
        token[...] = jnp.zeros_like(token)

    outs = pl.pallas_call(
        body, name=name,
        out_shape=[pltpu.SemaphoreType.DMA((3,))] * (2 * n) + [pltpu.HBM(a.shape, a.dtype) for a in lands]
        + [jax.ShapeDtypeStruct((8, 128), F32)],
        in_specs=[_HBM] * n,
        out_specs=[_SEM] * (2 * n) + [_HBM] * n + [pl.BlockSpec(memory_space=pltpu.VMEM)],
        input_output_aliases={i: 2 * n + i for i in range(n)},
        compiler_params=pltpu.CompilerParams(has_side_effects=_EFFECT),
    )(*[_in_hbm(a) for a in lands])
    return outs[:n], outs[n:2 * n], outs[2 * n:3 * n], outs[3 * n]


def _forward_wait(land, send, recv, after, name):
    def body(land_ref, send_sem, recv_sem, after_ref, land_out):
        del after_ref, land_out
        x, y, c, peers = _place()
        for k, (px, py) in enumerate(peers):
            cp = pltpu.make_async_remote_copy(
                src_ref=_my_part(land_ref, 2 * px + py, c, True), dst_ref=_my_part(land_ref, 2 * px + py, 1 - c, True),
                send_sem=send_sem.at[k], recv_sem=recv_sem.at[k], device_id=(x, y, 1 - c), device_id_type=MESH)
            cp.wait_send()
            cp.wait_recv()

    return pl.pallas_call(
        body, name=name,
        out_shape=pltpu.HBM(land.shape, land.dtype),
        in_specs=[_HBM, _SEM, _SEM, _ANY], out_specs=_HBM,
        input_output_aliases={0: 0},
        compiler_params=pltpu.CompilerParams(has_side_effects=_EFFECT),
    )(land, send, recv, after)


def _scatter_start(g, name):
    def body(g_ref, land_ref, send, recv, g_out, land_out, token):
        del g_out, land_out
        x, y, c, peers = _place()
        for k, (px, py) in enumerate(peers):
            pltpu.make_async_remote_copy(
                src_ref=g_ref.at[2 * px + py], dst_ref=land_ref.at[k], send_sem=send.at[k], recv_sem=recv.at[k],
                device_id=(px, py, c), device_id_type=MESH).start()
        token[...] = jnp.zeros_like(token)

    land = lax.empty((3,) + g.shape[1:], g.dtype)
    return pl.pallas_call(
        body, name=name,
        out_shape=(pltpu.SemaphoreType.DMA((3,)), pltpu.SemaphoreType.DMA((3,)),
                   pltpu.HBM(g.shape, g.dtype), pltpu.HBM(land.shape, land.dtype),
                   jax.ShapeDtypeStruct((8, 128), F32)),
        in_specs=(_HBM, _HBM), out_specs=(_SEM, _SEM, _HBM, _HBM, pl.BlockSpec(memory_space=pltpu.VMEM)),
        input_output_aliases={0: 2, 1: 3},
        compiler_params=pltpu.CompilerParams(has_side_effects=_EFFECT),
    )(_in_hbm(g), _in_hbm(land))


def _scatter_wait(started, after):
    n = len(started)

    def body(*refs):
        gs, lands = refs[:n], refs[n:2 * n]
        sends, recvs = refs[2 * n:3 * n], refs[3 * n:4 * n]
        x, y, c, peers = _place()
        for a in range(n):
            for k, (px, py) in enumerate(peers):
                cp = pltpu.make_async_remote_copy(
                    src_ref=gs[a].at[2 * px + py], dst_ref=lands[a].at[k], send_sem=sends[a].at[k],
                    recv_sem=recvs[a].at[k], device_id=(px, py, c), device_id_type=MESH)
                cp.wait_send()
                cp.wait_recv()

    gs = [s[2] for s in started]
    lands = [s[3] for s in started]
    outs = pl.pallas_call(
        body, name="scatter_wait",
        out_shape=[pltpu.HBM(a.shape, a.dtype) for a in gs + lands],
        in_specs=[_HBM] * (2 * n) + [_SEM] * (2 * n) + [_ANY], out_specs=[_HBM] * (2 * n),
        input_output_aliases={i: i for i in range(2 * n)},
        compiler_params=pltpu.CompilerParams(has_side_effects=_EFFECT),
    )(*gs, *lands, *[s[0] for s in started], *[s[1] for s in started], after)
    return outs[:n], outs[n:]


def _sibling():
    return lax.axis_index("x"), lax.axis_index("y"), 1 - lax.axis_index("c")


def _swap_start(parts):
    n = len(parts)

    def body(*refs):
        ins, land = refs[:n], refs[n:2 * n]
        send, recv = refs[2 * n], refs[2 * n + 1]
        token = refs[4 * n + 2]
        for a in range(n):
            pltpu.make_async_remote_copy(
                src_ref=ins[a], dst_ref=land[a], send_sem=send.at[a], recv_sem=recv.at[a],
                device_id=_sibling(), device_id_type=MESH).start()
        token[...] = jnp.zeros_like(token)

    lands = [lax.empty(a.shape, a.dtype) for a in parts]
    outs = pl.pallas_call(
        body, name="swap_start",
        out_shape=[pltpu.SemaphoreType.DMA((n,))] * 2 + [pltpu.HBM(a.shape, a.dtype) for a in parts + lands]
        + [jax.ShapeDtypeStruct((8, 128), F32)],
        in_specs=[_HBM] * (2 * n),
        out_specs=[_SEM] * 2 + [_HBM] * (2 * n) + [pl.BlockSpec(memory_space=pltpu.VMEM)],
        input_output_aliases={i: 2 + i for i in range(2 * n)},
        compiler_params=pltpu.CompilerParams(has_side_effects=_EFFECT),
    )(*[_in_hbm(a) for a in parts + lands])
    return outs[0], outs[1], outs[2:2 + n], outs[2 + n:2 + 2 * n], outs[2 + 2 * n]


def _swap_wait(parts, lands, send, recv, after):
    n = len(parts)

    def body(*refs):
        ins, land = refs[:n], refs[n:2 * n]
        send_sem, recv_sem = refs[2 * n], refs[2 * n + 1]
        for a in range(n):
            cp = pltpu.make_async_remote_copy(
                src_ref=ins[a], dst_ref=land[a], send_sem=send_sem.at[a], recv_sem=recv_sem.at[a],
                device_id=_sibling(), device_id_type=MESH)
            cp.wait_send()
            cp.wait_recv()

    outs = pl.pallas_call(
        body, name="swap_wait",
        out_shape=[pltpu.HBM(a.shape, a.dtype) for a in list(parts) + list(lands)],
        in_specs=[_HBM] * (2 * n) + [_SEM] * 2 + [_ANY], out_specs=[_HBM] * (2 * n),
        input_output_aliases={i: i for i in range(2 * n)},
        compiler_params=pltpu.CompilerParams(has_side_effects=_EFFECT),
    )(*parts, *lands, send, recv, after)
    return outs[:n], outs[n:]


def _all_reduce_small(v):
    nd, r, w = v.shape
    assert nd == N_DEV

    def body(v_ref, out_ref, land_ref, red_ref, send1, recv1, send2, recv2):
        x, y, c = lax.axis_index("x"), lax.axis_index("y"), lax.axis_index("c")
        me = 4 * x + 2 * y + c
        peers = []
        for k in range(1, N_DEV):
            kx, ky, kc = (k >> 2) & 1, (k >> 1) & 1, k & 1
            peers.append((1 - x if kx else x, 1 - y if ky else y, 1 - c if kc else c))
        first = []
        for k, (px, py, pc) in enumerate(peers):
            cp = pltpu.make_async_remote_copy(
                src_ref=v_ref.at[4 * px + 2 * py + pc], dst_ref=land_ref.at[me], send_sem=send1.at[k],
                recv_sem=recv1.at[k], device_id=(px, py, pc), device_id_type=MESH)
            cp.start()
            first.append(cp)
        land_ref[me] = v_ref[me]
        for cp in first:
            cp.wait()
        acc = land_ref[0]
        for j in range(1, N_DEV):
            acc = acc + land_ref[j]
        red_ref[...] = acc
        second = []
        for k, (px, py, pc) in enumerate(peers):
            cp = pltpu.make_async_remote_copy(
                src_ref=red_ref, dst_ref=out_ref.at[me], send_sem=send2.at[k],
                recv_sem=recv2.at[k], device_id=(px, py, pc), device_id_type=MESH)
            cp.start()
            second.append(cp)
        out_ref[me] = acc
        for cp in second:
            cp.wait()

    vmem = pl.BlockSpec(memory_space=pltpu.VMEM)
    return pl.pallas_call(
        body, name="all_reduce_small",
        in_specs=[vmem], out_specs=vmem,
        out_shape=jax.ShapeDtypeStruct((nd, r, w), F32),
        scratch_shapes=[pltpu.VMEM((nd, r, w), F32), pltpu.VMEM((r, w), F32)]
        + [pltpu.SemaphoreType.DMA((N_DEV - 1,))] * 4,
        compiler_params=pltpu.CompilerParams(vmem_limit_bytes=VMEM_LIMIT),
    )(v)


def _adam_math(w, g, m, v):
    m = ADAM_B1 * m + (1.0 - ADAM_B1) * g
    v = ADAM_B2 * v + (1.0 - ADAM_B2) * (g * g)
    m_hat = m / (1.0 - ADAM_B1 ** ADAM_STEP)
    v_hat = v / (1.0 - ADAM_B2 ** ADAM_STEP)
    delta = -ADAM_LR * (m_hat / (jnp.sqrt(v_hat) + ADAM_EPS) + ADAM_WD * w)
    return delta, m, v


def _sum_blocks(own, got, chip, name):
    _, r, c = own.shape
    tm = _tile(r, ROWS, 16)

    def kern(chip_ref, own_ref, got_ref, o_ref):
        del chip_ref
        acc = own_ref[...].astype(F32)
        for k in range(3):
            acc = acc + got_ref[k].astype(F32)
        o_ref[...] = acc

    return pl.pallas_call(
        kern, name=name,
        grid_spec=pltpu.PrefetchScalarGridSpec(
            num_scalar_prefetch=1, grid=(r // tm,),
            in_specs=[pl.BlockSpec((None, tm, c), lambda i, ch: (ch[0], i, 0)),
                      pl.BlockSpec((3, tm, c), lambda i, ch: (0, i, 0))],
            out_specs=pl.BlockSpec((tm, c), lambda i, ch: (i, 0))),
        out_shape=jax.ShapeDtypeStruct((r, c), F32),
        compiler_params=_params(("parallel",)),
    )(chip, own, got)


def _adamw(parts, w, m, v, name):
    nl, r, c = w.shape
    assert len(parts) == nl
    tm = _tile(r, 128, 8)

    def kern(*refs):
        p_refs = refs[:2 * nl]
        w_ref, m_ref, v_ref, g_ref, d_ref, mo_ref, vo_ref = refs[2 * nl:]
        layer = pl.program_id(0)
        g = p_refs[0][...] + p_refs[1][...]
        for ll in range(1, nl):
            g = jnp.where(layer == ll, p_refs[2 * ll][...] + p_refs[2 * ll + 1][...], g)
        g_ref[...] = g
        d_ref[...], mo_ref[...], vo_ref[...] = _adam_math(w_ref[...], g, m_ref[...], v_ref[...])

    def part_spec(ll):
        return pl.BlockSpec((tm, c), lambda l, i: (jnp.where(l == ll, i, 0), 0))

    spec = pl.BlockSpec((None, tm, c), lambda l, i: (l, i, 0))
    return pl.pallas_call(
        kern, name=name, grid=(nl, r // tm),
        in_specs=[part_spec(ll) for ll in range(nl) for _ in range(2)] + [spec] * 3, out_specs=[spec] * 4,
        out_shape=[jax.ShapeDtypeStruct((nl, r, c), F32)] * 4,
        compiler_params=_params(("arbitrary", "parallel")),
    )(*[p for pair in parts for p in pair], w, m, v)


def _adamw_small(g, w, m, v, name):
    def kern(g_ref, w_ref, m_ref, v_ref, d_ref, mo_ref, vo_ref):
        d_ref[...], mo_ref[...], vo_ref[...] = _adam_math(w_ref[...], g_ref[...], m_ref[...], v_ref[...])

    return pl.pallas_call(
        kern, name=name,
        out_shape=[jax.ShapeDtypeStruct(g.shape, F32)] * 3,
        compiler_params=pltpu.CompilerParams(vmem_limit_bytes=VMEM_LIMIT),
    )(g, w, m, v)


def _pack(arrays, rows):
    flat = jnp.concatenate([a.reshape(-1).astype(F32) for a in arrays])
    return jnp.pad(flat, (0, rows * 128 - flat.shape[0])).reshape(rows, 128)


def _unpack(packed, shapes):
    flat = packed.reshape(-1)
    out, off = [], 0
    for shp in shapes:
        size = math.prod(shp)
        out.append(flat[off:off + size].reshape(shp))
        off += size
    return out


_REPLICATED = ["s5_lam_re", "s5_lam_im", "s5_log_dt", "s5_b_re", "s5_b_im", "s5_c_re", "s5_c_im",
               "ffn_norm", "b_norm_mix", "kv_norm", "final_norm"]
_CHIP_VECTORS = ["s5_d", "a_norm_mix"]
_BIG = ["s5_w_glu", "ffn_w_in", "ffn_w_out", "attn_w_q", "attn_w_o", "w_kv"]
_WEIGHT_ORDER = ["s5_lam_re", "s5_lam_im", "s5_log_dt", "s5_b_re", "s5_b_im", "s5_c_re", "s5_c_im", "s5_d",
                 "s5_w_glu", "a_norm_mix", "ffn_norm", "ffn_w_in", "ffn_w_out", "b_norm_mix", "attn_w_q",
                 "attn_w_o", "kv_norm", "w_kv", "final_norm"]


def _step(x, loss_target, w, m, v):
    s, d = x.shape[1], x.shape[2]
    chip = 2 * lax.axis_index("x") + lax.axis_index("y")

    col_sharded = dict(w_glu=("s5_w_glu", 0), w_in0=("ffn_w_in", 0), w_in1=("ffn_w_in", 1), w_q=("attn_w_q", 0))
    row_sharded = dict(w_out0=("ffn_w_out", 0), w_out1=("ffn_w_out", 1), w_o=("attn_w_o", 0), w_kv=("w_kv", 0))
    names = ["w_glu", "w_in0", "w_out0", "w_kv", "w_q", "w_o", "w_in1", "w_out1"]
    local = {**col_sharded, **row_sharded}

    def layers(a):
        return a.reshape((-1,) + a.shape[-2:])

    chip_arr = jnp.reshape(chip, (1,)).astype(jnp.int32)
    vector_lands = [lax.dynamic_update_slice(jnp.zeros((N_CHIPS,) + w[n].shape, F32), w[n][None], (chip, 0, 0))
                    for n in _CHIP_VECTORS]
    v_send, v_recv, v_land, _ = _gather_start(vector_lands, "gather_start_vectors", False)
    first = 2
    lands = [_cast_place(layers(w[local[n][0]]), local[n][1], chip_arr, f"cast_{n}") for n in names[:first]]
    send, recv, land_thru, _ = _gather_start(lands, "gather_start_first", True)
    lands = [_cast_place(layers(w[local[n][0]]), local[n][1], chip_arr, f"cast_{n}") for n in names[first:]]
    *later, rest_token = _gather_start(lands, "gather_start_rest", True)
    send, recv, land_thru = [list(a) + list(b) for a, b in zip((send, recv, land_thru), later)]
    vectors = _gather_wait(v_land, v_send, v_recv, jnp.zeros((8, 128), F32), "gather_wait_vectors", False)
    s5_d_full = vectors[0].reshape(1, d)
    a_norm_full = vectors[1].reshape(1, d) + rest_token[0, 0]

    batches = dict(glu=["w_glu", "w_in0", "w_out0"], attention=["w_kv", "w_q", "w_o"], ffn1=["w_in1", "w_out1"])
    forwarded, arrived = {}, {}

    def stage(batch, after):
        idx = [names.index(n) for n in batches[batch]]
        got = _gather_wait([land_thru[i] for i in idx], [send[i] for i in idx], [recv[i] for i in idx], after,
                           f"gather_wait_{batch}", True)
        f_send, f_recv, f_land, token = _forward_start(got, f"forward_start_{batch}")
        forwarded.update(zip(batches[batch], zip(f_send, f_recv, f_land)))
        return token[0, 0]

    def weight(name, after):
        if name not in arrived:
            f_send, f_recv, f_land = forwarded[name]
            land = _forward_wait(f_land, f_send, f_recv, after, f"forward_wait_{name}")
            arrived[name] = land if name in col_sharded else land.reshape(1, -1, land.shape[-1])
        return arrived[name]

    started = {}

    def emit(name, dw):
        outs = _scatter_start(dw.reshape(N_CHIPS, -1, dw.shape[-1]), f"scatter_start_{name}")
        started[name] = outs[:4]
        return outs[4][0, 0]

    small = dict(lam_re=w["s5_lam_re"][0], lam_im=w["s5_lam_im"][0], log_dt=w["s5_log_dt"][0],
                 b_re=w["s5_b_re"][0], b_im=w["s5_b_im"][0], c_re=w["s5_c_re"][0], c_im=w["s5_c_im"][0],
                 s5_d=s5_d_full, a_norm=a_norm_full, ffn_norm0=w["ffn_norm"][0:1], ffn_norm1=w["ffn_norm"][1:2],
                 b_norm=w["b_norm_mix"], kv_norm=w["kv_norm"].reshape(1, d), final_norm=w["final_norm"].reshape(1, d))

    loss_rows, grad_x, d_small = _local_step(x[0], loss_target[0], small, weight, emit, stage)

    g4, got = _scatter_wait([started[n] for n in names], grad_x)
    partial = [_sum_blocks(o, r, chip_arr, f"sum_{n}") for n, o, r in zip(names, g4, got)]
    swap_send, swap_recv, partial, swap_land, swap_token = _swap_start(partial)
    result = {}

    rep_grads = [d_small["lam_re"], d_small["lam_im"], d_small["log_dt"], d_small["b_re"], d_small["b_im"],
                 d_small["c_re"], d_small["c_im"],
                 jnp.concatenate([d_small["ffn_norm0"], d_small["ffn_norm1"]], axis=0),
                 d_small["b_norm"], d_small["kv_norm"], d_small["final_norm"]]
    vec_grads = [d_small["s5_d"], d_small["a_norm"]]
    to_reduce = rep_grads + vec_grads + [jnp.sum(loss_rows).reshape(1) + swap_token[0, 0]]
    total = sum(math.prod(a.shape) for a in to_reduce)
    rows_per = -(-total // (N_DEV * 128 * 8)) * 8
    reduced = _all_reduce_small(_pack(to_reduce, N_DEV * rows_per).reshape(N_DEV, rows_per, 128))
    red = _unpack(reduced, [a.shape for a in to_reduce])
    loss = red[-1][0]
    g_small = dict(zip(_REPLICATED, [r.reshape(w[n].shape) for r, n in zip(red[:len(rep_grads)], _REPLICATED)]))
    for n, r in zip(_CHIP_VECTORS, red[len(rep_grads):-1]):
        g_small[n] = lax.dynamic_slice_in_dim(r.reshape(1, d), chip * (d // N_CHIPS), d // N_CHIPS, axis=1)
    small_names = _REPLICATED + _CHIP_VECTORS
    n_small = sum(math.prod(w[n].shape) for n in small_names)
    rows_small = -(-n_small // (128 * 8)) * 8
    packed = [_pack([src[n] for n in small_names], rows_small) for src in (g_small, w, m, v)]
    upd = _adamw_small(*packed, "adamw_small")
    shapes = [w[n].shape for n in small_names]
    for n, dl, mo, vo in zip(small_names, *[_unpack(u, shapes) for u in upd]):
        result[n] = [g_small[n], dl, mo, vo]

    partial, other = _swap_wait(partial, swap_land, swap_send, swap_recv, upd[0])
    part = dict(zip(names, zip(partial, other)))
    for name in _BIG:
        parts = [part[n] for n in sorted((n for n in names if local[n][0] == name), key=lambda n: local[n][1])]
        outs = _adamw(parts, layers(w[name]), layers(m[name]), layers(v[name]), f"adamw_{name}")
        result[name] = [o.reshape(w[name].shape) for o in outs]

    cols = [[result[n][t] for n in _WEIGHT_ORDER] for t in range(4)]
    return (loss, grad_x.reshape(x.shape), *cols[0], *cols[1], *cols[2], *cols[3])


def kernel(x, s5_lam_re, s5_lam_im, s5_log_dt, s5_b_re, s5_b_im, s5_c_re, s5_c_im, s5_d, s5_w_glu, a_norm_mix, ffn_norm, ffn_w_in, ffn_w_out, b_norm_mix, attn_w_q, attn_w_o, kv_norm, w_kv, final_norm, loss_target, m_s5_lam_re, m_s5_lam_im, m_s5_log_dt, m_s5_b_re, m_s5_b_im, m_s5_c_re, m_s5_c_im, m_s5_d, m_s5_w_glu, m_a_norm_mix, m_ffn_norm, m_ffn_w_in, m_ffn_w_out, m_b_norm_mix, m_attn_w_q, m_attn_w_o, m_kv_norm, m_w_kv, m_final_norm, v_s5_lam_re, v_s5_lam_im, v_s5_log_dt, v_s5_b_re, v_s5_b_im, v_s5_c_re, v_s5_c_im, v_s5_d, v_s5_w_glu, v_a_norm_mix, v_ffn_norm, v_ffn_w_in, v_ffn_w_out, v_b_norm_mix, v_attn_w_q, v_attn_w_o, v_kv_norm, v_w_kv, v_final_norm):
    w = dict(zip(_WEIGHT_ORDER, (s5_lam_re, s5_lam_im, s5_log_dt, s5_b_re, s5_b_im, s5_c_re, s5_c_im, s5_d, s5_w_glu, a_norm_mix, ffn_norm, ffn_w_in, ffn_w_out, b_norm_mix, attn_w_q, attn_w_o, kv_norm, w_kv, final_norm)))
    m = dict(zip(_WEIGHT_ORDER, (m_s5_lam_re, m_s5_lam_im, m_s5_log_dt, m_s5_b_re, m_s5_b_im, m_s5_c_re, m_s5_c_im, m_s5_d, m_s5_w_glu, m_a_norm_mix, m_ffn_norm, m_ffn_w_in, m_ffn_w_out, m_b_norm_mix, m_attn_w_q, m_attn_w_o, m_kv_norm, m_w_kv, m_final_norm)))
    v = dict(zip(_WEIGHT_ORDER, (v_s5_lam_re, v_s5_lam_im, v_s5_log_dt, v_s5_b_re, v_s5_b_im, v_s5_c_re, v_s5_c_im, v_s5_d, v_s5_w_glu, v_a_norm_mix, v_ffn_norm, v_ffn_w_in, v_ffn_w_out, v_b_norm_mix, v_attn_w_q, v_attn_w_o, v_kv_norm, v_w_kv, v_final_norm)))
    return _step(x, loss_target, w, m, v)
```

```python
import functools
import math

import jax
import jax.numpy as jnp
from jax import lax
from jax.experimental import pallas as pl
from jax.experimental.pallas import tpu as pltpu

F32 = jnp.float32
BF16 = jnp.bfloat16

S5_GROUP_CH = 16
S5_STATE = 64
GROUPS_PER_TILE = 8
HEAD_DIM = 128
N_KV_HEADS = 4
PATTERNS = ((128, 1), (512, 4), (2048, 16))
ATT_BLK = 128
EPS = 1e-6
NEG_INF = -1e30
SCAN_T = 256
ADAM_LR = 0.001
ADAM_B1 = 0.9
ADAM_B2 = 0.999
ADAM_EPS = 1e-08
ADAM_WD = 0.01
ADAM_STEP = 10
N_CHIPS = 4
N_DEV = 8
VMEM_LIMIT = 56 * 1024 * 1024
MESH = pl.DeviceIdType.MESH
GELU_K = math.sqrt(2.0 / math.pi)
GELU_C = 0.044715


def _tile(n, pref, unit=128):
    if n <= pref:
        return n
    best = None
    t = unit
    while t <= pref:
        if n % t == 0:
            best = t
        t += unit
    assert best is not None, (n, pref, unit)
    return best


def _params(sem):
    return pltpu.CompilerParams(dimension_semantics=sem, vmem_limit_bytes=VMEM_LIMIT)


def _dot(a, b):
    return jnp.dot(a, b, preferred_element_type=F32)


def _dot_nt(a, b):
    return lax.dot_general(a, b, (((1,), (1,)), ((), ())), preferred_element_type=F32)


def _dot_tn(a, b):
    return lax.dot_general(a, b, (((0,), (0,)), ((), ())), preferred_element_type=F32)


def _mm_nn(a, w, out_dtype, name, tm=512, tn=1536, tk=2048):
    m, k = a.shape
    nb, k2, nq = w.shape
    assert k == k2
    tm, tn, tk = _tile(m, tm, 8), _tile(nq, tn), _tile(k, tk)
    per, nk = nq // tn, k // tk

    def kern(a_ref, w_ref, o_ref, *acc):
        p = _dot(a_ref[...], w_ref[...])
        if nk == 1:
            o_ref[...] = p.astype(o_ref.dtype)
        else:
            acc_ref, = acc
            kk = pl.program_id(2)

            @pl.when(kk == 0)
            def _():
                acc_ref[...] = p

            @pl.when(kk > 0)
            def _():
                acc_ref[...] += p

            @pl.when(kk == nk - 1)
            def _():
                o_ref[...] = acc_ref[...].astype(o_ref.dtype)

    return pl.pallas_call(
        kern, name=name, grid=(nb * per, m // tm, nk),
        in_specs=[pl.BlockSpec((tm, tk), lambda j, i, kk: (i, kk)),
                  pl.BlockSpec((None, tk, tn), lambda j, i, kk: (j // per, kk, j % per))],
        out_specs=pl.BlockSpec((tm, tn), lambda j, i, kk: (i, j)),
        out_shape=jax.ShapeDtypeStruct((m, nb * nq), out_dtype),
        scratch_shapes=[] if nk == 1 else [pltpu.VMEM((tm, tn), F32)],
        compiler_params=_params(("parallel", "parallel", "arbitrary")),
    )(a, w)


def _paired_block(r, per, nb):
    j = r // 2
    return (r % 2) * (nb // 2) + j // per, j % per


def _mm_nt(a, w, name, tm=512, tr=1536, tkc=1024, paired=False):
    m, n = a.shape
    nb, k, nq = w.shape
    assert n == nb * nq
    tm, tr, tkc = _tile(m, tm, 8), _tile(nq, tr), _tile(k, tkc)
    per = nq // tr

    def kern(a_ref, w_ref, o_ref):
        acc = None
        for r in range(nb * per):
            blk, tile = _paired_block(r, per, nb) if paired else (r // per, r % per)
            p = _dot_nt(a_ref[:, r * tr:(r + 1) * tr], w_ref[blk, :, tile * tr:(tile + 1) * tr])
            acc = p if acc is None else acc + p
        o_ref[...] = acc

    return pl.pallas_call(
        kern, name=name, grid=(k // tkc, m // tm),
        in_specs=[pl.BlockSpec((tm, n), lambda kc, i: (i, 0)),
                  pl.BlockSpec((nb, tkc, nq), lambda kc, i: (0, kc, 0))],
        out_specs=pl.BlockSpec((tm, tkc), lambda kc, i: (i, kc)),
        out_shape=jax.ShapeDtypeStruct((m, k), F32),
        compiler_params=_params(("parallel", "parallel")),
    )(a, w)


def _mm_tn(a, dy, nb, name, tkk=512, tn=1536, paired=False):
    s, k = a.shape
    s2, n = dy.shape
    assert s == s2 and n % nb == 0
    nq = n // nb
    tkk, tn = _tile(k, tkk), _tile(nq, tn)
    per = nq // tn

    def w_block(j):
        return _paired_block(j, per, nb) if paired else (j // per, j % per)

    def kern(a_ref, dy_ref, o_ref):
        o_ref[...] = _dot_tn(a_ref[...], dy_ref[...]).astype(o_ref.dtype)

    return pl.pallas_call(
        kern, name=name, grid=(nb * per, k // tkk),
        in_specs=[pl.BlockSpec((s, tkk), lambda j, kk: (0, kk)),
                  pl.BlockSpec((s, tn), lambda j, kk: (0, j))],
        out_specs=pl.BlockSpec((None, tkk, tn), lambda j, kk: (w_block(j)[0], kk, w_block(j)[1])),
        out_shape=jax.ShapeDtypeStruct((nb, k, nq), BF16),
        compiler_params=_params(("parallel", "parallel")),
    )(a, dy)


ROWS = 256


def _rms(x, g):
    r = lax.rsqrt(jnp.mean(x * x, axis=-1, keepdims=True) + EPS)
    return x * r * g


def _rms_bwd(x, g, dh):
    r = lax.rsqrt(jnp.mean(x * x, axis=-1, keepdims=True) + EPS)
    xh = x * r
    dgx = dh * g
    dx = r * (dgx - xh * jnp.mean(dgx * xh, axis=-1, keepdims=True))
    return dx, dh * xh


def _sigmoid(x):
    return 1.0 / (1.0 + jnp.exp(-x))


def _gelu(y):
    return 0.5 * y * (1.0 + jnp.tanh(GELU_K * (y + GELU_C * y * y * y)))


def _gelu_grad(y):
    t = jnp.tanh(GELU_K * (y + GELU_C * y * y * y))
    return 0.5 * (1.0 + t) + 0.5 * y * (1.0 - t * t) * GELU_K * (1.0 + 3.0 * GELU_C * y * y)


def _row_spec(tm, d, col=0):
    return pl.BlockSpec((tm, d), lambda i: (i, col))


def _vec_spec(d):
    return pl.BlockSpec((1, d), lambda i: (0, 0))


def _acc_rows(ref, val, i):
    s = jnp.sum(val, axis=0, keepdims=True)

    @pl.when(i == 0)
    def _():
        ref[...] = s

    @pl.when(i > 0)
    def _():
        ref[...] += s


def _norm_f32(x, g, name):
    s, d = x.shape
    tm = _tile(s, ROWS, 8)

    def kern(x_ref, g_ref, h_ref):
        h_ref[...] = _rms(x_ref[...], g_ref[...])

    return pl.pallas_call(
        kern, name=name, grid=(s // tm,),
        in_specs=[_row_spec(tm, d), _vec_spec(d)],
        out_specs=_row_spec(tm, d),
        out_shape=jax.ShapeDtypeStruct((s, d), F32),
        compiler_params=_params(("parallel",)),
    )(x, g)


def _glu_res_norm(vg, x, g, name):
    s, d = x.shape
    tm = _tile(s, ROWS, 16)

    def kern(val_ref, gate_ref, x_ref, g_ref, x1_ref, hf_ref):
        x1 = x_ref[...] + val_ref[...] * _sigmoid(gate_ref[...])
        x1_ref[...] = x1
        hf_ref[...] = _rms(x1, g_ref[...]).astype(BF16)

    return pl.pallas_call(
        kern, name=name, grid=(s // tm,),
        in_specs=[_row_spec(tm, d, 0), _row_spec(tm, d, 1), _row_spec(tm, d), _vec_spec(d)],
        out_specs=[_row_spec(tm, d), _row_spec(tm, d)],
        out_shape=[jax.ShapeDtypeStruct((s, d), F32), jax.ShapeDtypeStruct((s, d), BF16)],
        compiler_params=_params(("parallel",)),
    )(vg, vg, x, g)


FFN_TILE = 1408


def _ffn_in_act(hf, w_in, name, tm=512):
    s, k = hf.shape
    nb, _, nq = w_in.shape
    tm, tn = _tile(s, tm, 16), _tile(nq, FFN_TILE)
    per = nq // tn
    nf = (nb // 2) * per

    def kern(h_ref, wg_ref, wu_ref, a_ref, gu_ref):
        h = h_ref[...]
        g = _dot(h, wg_ref[...])
        u = _dot(h, wu_ref[...])
        a_ref[...] = (g * _sigmoid(g) * u).astype(BF16)
        gu_ref[:, :tn] = g.astype(BF16)
        gu_ref[:, tn:] = u.astype(BF16)

    return pl.pallas_call(
        kern, name=name, grid=(nf, s // tm),
        in_specs=[pl.BlockSpec((tm, k), lambda j, i: (i, 0)),
                  pl.BlockSpec((None, k, tn), lambda j, i: (j // per, 0, j % per)),
                  pl.BlockSpec((None, k, tn), lambda j, i: (nb // 2 + j // per, 0, j % per))],
        out_specs=[pl.BlockSpec((tm, tn), lambda j, i: (i, j)),
                   pl.BlockSpec((tm, 2 * tn), lambda j, i: (i, j))],
        out_shape=[jax.ShapeDtypeStruct((s, nf * tn), BF16), jax.ShapeDtypeStruct((s, 2 * nf * tn), BF16)],
        compiler_params=_params(("parallel", "parallel")),
    )(hf, w_in, w_in)


def _ffn_out_dx_act(dresb, w_out, gu, name, tm=512):
    s, d = dresb.shape
    f = w_out.shape[1]
    tm = _tile(s, tm, 16)
    tn = _tile(f // 2, FFN_TILE)

    def kern(d_ref, w_ref, gu_ref, o_ref):
        da = _dot_nt(d_ref[...], w_ref[...])
        g = gu_ref[:, :tn].astype(F32)
        u = gu_ref[:, tn:].astype(F32)
        sg = _sigmoid(g)
        o_ref[:, :tn] = (da * u * sg * (1.0 + g * (1.0 - sg))).astype(BF16)
        o_ref[:, tn:] = (da * g * sg).astype(BF16)

    pair = pl.BlockSpec((tm, 2 * tn), lambda j, i: (i, j))
    return pl.pallas_call(
        kern, name=name, grid=(f // tn, s // tm),
        in_specs=[pl.BlockSpec((tm, d), lambda j, i: (i, 0)),
                  pl.BlockSpec((None, tn, d), lambda j, i: (0, j, 0)), pair],
        out_specs=pair,
        out_shape=jax.ShapeDtypeStruct((s, 2 * f), BF16),
        compiler_params=_params(("parallel", "parallel")),
    )(dresb, w_out, gu)


def _res_norm(o, x, gains, name):
    s, d = x.shape
    tm = _tile(s, ROWS, 16)
    ng = len(gains)

    def kern(o_ref, x_ref, *refs):
        xn = x_ref[...] + o_ref[...]
        refs[ng][...] = xn
        for t in range(ng):
            refs[ng + 1 + t][...] = _rms(xn, refs[t][...]).astype(BF16)

    return pl.pallas_call(
        kern, name=name, grid=(s // tm,),
        in_specs=[_row_spec(tm, d), _row_spec(tm, d)] + [_vec_spec(d)] * ng,
        out_specs=[_row_spec(tm, d)] * (1 + ng),
        out_shape=[jax.ShapeDtypeStruct((s, d), F32)] + [jax.ShapeDtypeStruct((s, d), BF16)] * ng,
        compiler_params=_params(("parallel",)),
    )(o, x, *gains)


def _loss_head(o, x, g, target, name):
    s, d = x.shape
    tm = _tile(s, ROWS, 16)

    def kern(o_ref, x_ref, g_ref, t_ref, dx_ref, dxb_ref, loss_ref, dg_ref):
        i = pl.program_id(0)
        x4 = x_ref[...] + o_ref[...]
        gg = g_ref[...]
        diff = _rms(x4, gg) - t_ref[...]
        dx, dgr = _rms_bwd(x4, gg, diff * (1.0 / d))
        dx_ref[...] = dx
        dxb_ref[...] = dx.astype(BF16)
        _acc_rows(loss_ref, diff * diff * (0.5 / d), i)
        _acc_rows(dg_ref, dgr, i)

    return pl.pallas_call(
        kern, name=name, grid=(s // tm,),
        in_specs=[_row_spec(tm, d), _row_spec(tm, d), _vec_spec(d), _row_spec(tm, d)],
        out_specs=[_row_spec(tm, d), _row_spec(tm, d), _vec_spec(d), _vec_spec(d)],
        out_shape=[jax.ShapeDtypeStruct((s, d), F32), jax.ShapeDtypeStruct((s, d), BF16),
                   jax.ShapeDtypeStruct((1, d), F32), jax.ShapeDtypeStruct((1, d), F32)],
        compiler_params=_params(("arbitrary",)),
    )(o, x, g, target)


def _norm_bwd(x, gains, dhs, dres, name):
    s, d = x.shape
    tm = _tile(s, ROWS, 16)
    ng = len(gains)

    def kern(x_ref, dres_ref, *refs):
        i = pl.program_id(0)
        x_ = x_ref[...]
        acc = dres_ref[...]
        for t in range(ng):
            dx, dgr = _rms_bwd(x_, refs[t][...], refs[ng + t][...])
            acc = acc + dx
            _acc_rows(refs[2 * ng + 2 + t], dgr, i)
        refs[2 * ng][...] = acc
        refs[2 * ng + 1][...] = acc.astype(BF16)

    return pl.pallas_call(
        kern, name=name, grid=(s // tm,),
        in_specs=[_row_spec(tm, d), _row_spec(tm, d)] + [_vec_spec(d)] * ng + [_row_spec(tm, d)] * ng,
        out_specs=[_row_spec(tm, d), _row_spec(tm, d)] + [_vec_spec(d)] * ng,
        out_shape=[jax.ShapeDtypeStruct((s, d), F32), jax.ShapeDtypeStruct((s, d), BF16)]
        + [jax.ShapeDtypeStruct((1, d), F32)] * ng,
        compiler_params=_params(("arbitrary",)),
    )(x, dres, *gains, *dhs)


def _glu_bwd(dmix, vg, name):
    s, d = dmix.shape
    tm = _tile(s, ROWS, 16)

    def kern(dm_ref, val_ref, gate_ref, o_ref):
        dm = dm_ref[...]
        sg = _sigmoid(gate_ref[...])
        o_ref[:, :d] = (dm * sg).astype(BF16)
        o_ref[:, d:] = (dm * val_ref[...] * sg * (1.0 - sg)).astype(BF16)

    return pl.pallas_call(
        kern, name=name, grid=(s // tm,),
        in_specs=[_row_spec(tm, d), _row_spec(tm, d, 0), _row_spec(tm, d, 1)],
        out_specs=_row_spec(tm, 2 * d),
        out_shape=jax.ShapeDtypeStruct((s, 2 * d), BF16),
        compiler_params=_params(("parallel",)),
    )(dmix, vg, vg)


SUB = 8


def _local_scan(vr, vi, pre_ref, pim_ref, reverse):
    sub = lax.broadcasted_iota(jnp.int32, vr.shape, 1)
    sign = -1.0 if reverse else 1.0
    for sh in (1, 2, 4):
        ar = pre_ref[sh - 1:sh, :][None]
        ai = sign * pim_ref[sh - 1:sh, :][None]
        keep = sub < SUB - sh if reverse else sub >= sh
        sr = jnp.where(keep, pltpu.roll(vr, SUB - sh if reverse else sh, 1), 0.0)
        si = jnp.where(keep, pltpu.roll(vi, SUB - sh if reverse else sh, 1), 0.0)
        vr, vi = vr + ar * sr - ai * si, vi + ar * si + ai * sr
    return vr, vi


def _s5_fwd(u, bbd_re, bbd_im, cbd_re, cbd_im, pw_re, pw_im, dskip, name):
    s, d = u.shape
    nt, cw, lw = bbd_re.shape
    t = _tile(s, SCAN_T, SUB)
    nc = s // t
    ng = t // SUB

    def kern(u_ref, bre_ref, bim_ref, cre_ref, cim_ref, pre_ref, pim_ref, d_ref,
             xr_ref, xi_ref, y_ref, z_ref, car_ref, cai_ref):
        c = pl.program_id(1)

        @pl.when(c == 0)
        def _():
            car_ref[...] = jnp.zeros_like(car_ref)
            cai_ref[...] = jnp.zeros_like(cai_ref)

        u_ = u_ref[...]
        ub = u_.astype(BF16)
        vr, vi = _local_scan(_dot(ub, bre_ref[...]).reshape(ng, SUB, lw),
                             _dot(ub, bim_ref[...]).reshape(ng, SUB, lw), pre_ref, pim_ref, False)
        cr = car_ref[...]
        ci = cai_ref[...]
        pr = pre_ref[...]
        pi = pim_ref[...]
        for gidx in range(ng):
            rows = slice(gidx * SUB, (gidx + 1) * SUB)
            gr = vr[gidx] + pr * cr - pi * ci
            gi = vi[gidx] + pr * ci + pi * cr
            xr_ref[rows, :] = gr
            xi_ref[rows, :] = gi
            cr, ci = gr[SUB - 1:SUB, :], gi[SUB - 1:SUB, :]
        car_ref[...] = cr
        cai_ref[...] = ci
        y = (_dot(xr_ref[...].astype(BF16), cre_ref[...]) - _dot(xi_ref[...].astype(BF16), cim_ref[...])
             + d_ref[...] * u_)
        y_ref[...] = y
        z_ref[...] = _gelu(y).astype(BF16)

    tok = pl.BlockSpec((t, cw), lambda j, c: (c, j))
    st = pl.BlockSpec((t, lw), lambda j, c: (c, j))
    return pl.pallas_call(
        kern, name=name, grid=(nt, nc),
        in_specs=[tok,
                  pl.BlockSpec((None, cw, lw), lambda j, c: (j, 0, 0)),
                  pl.BlockSpec((None, cw, lw), lambda j, c: (j, 0, 0)),
                  pl.BlockSpec((None, lw, cw), lambda j, c: (j, 0, 0)),
                  pl.BlockSpec((None, lw, cw), lambda j, c: (j, 0, 0)),
                  pl.BlockSpec((SUB, lw), lambda j, c: (0, j)),
                  pl.BlockSpec((SUB, lw), lambda j, c: (0, j)),
                  pl.BlockSpec((1, cw), lambda j, c: (0, j))],
        out_specs=[st, st, tok, tok],
        out_shape=[jax.ShapeDtypeStruct((s, nt * lw), F32), jax.ShapeDtypeStruct((s, nt * lw), F32),
                   jax.ShapeDtypeStruct((s, d), F32), jax.ShapeDtypeStruct((s, d), BF16)],
        scratch_shapes=[pltpu.VMEM((1, lw), F32), pltpu.VMEM((1, lw), F32)],
        compiler_params=_params(("parallel", "arbitrary")),
    )(u, bbd_re, bbd_im, cbd_re, cbd_im, pw_re, pw_im, dskip)


def _s5_bwd(dz, y, u, xs_re, xs_im, bbd_re, bbd_im, cbd_re, cbd_im, pw_re, pw_im, pf_re, pf_im, dskip, name):
    s, d = u.shape
    nt, cw, lw = bbd_re.shape
    t = _tile(s, SCAN_T, SUB)
    nc = s // t
    ng = t // SUB

    def kern(dz_ref, y_ref, u_ref, xr_ref, xi_ref, bre_ref, bim_ref, cre_ref, cim_ref,
             pre_ref, pim_ref, fre_ref, fim_ref, d_ref,
             du_ref, dd_ref, dcr_ref, dci_ref, dbr_ref, dbi_ref, dar_ref, dai_ref,
             car_ref, cai_ref, gr_ref, gi_ref):
        c = pl.program_id(1)

        @pl.when(c == 0)
        def _():
            car_ref[...] = jnp.zeros_like(car_ref)
            cai_ref[...] = jnp.zeros_like(cai_ref)

        u_ = u_ref[...]
        ub = u_.astype(BF16)
        dy = dz_ref[...] * _gelu_grad(y_ref[...])
        dyb = dy.astype(BF16)
        vr, vi = _local_scan(_dot_nt(dyb, cre_ref[...]).reshape(ng, SUB, lw),
                             (-_dot_nt(dyb, cim_ref[...])).reshape(ng, SUB, lw), pre_ref, pim_ref, True)
        later_r = car_ref[...]
        later_i = cai_ref[...]
        cr, ci = later_r, later_i
        fr = fre_ref[...]
        fi = fim_ref[...]
        for gidx in reversed(range(ng)):
            rows = slice(gidx * SUB, (gidx + 1) * SUB)
            ar = vr[gidx] + fr * cr + fi * ci
            ai = vi[gidx] + fr * ci - fi * cr
            gr_ref[rows, :] = ar
            gi_ref[rows, :] = ai
            cr, ci = ar[0:1, :], ai[0:1, :]
        car_ref[...] = cr
        cai_ref[...] = ci
        gr = gr_ref[...]
        gi = gi_ref[...]
        rows = lax.broadcasted_iota(jnp.int32, (t, lw), 0)
        gsr = jnp.where(rows < t - 1, pltpu.roll(gr, t - 1, 0), later_r)
        gsi = jnp.where(rows < t - 1, pltpu.roll(gi, t - 1, 0), later_i)
        xr = xr_ref[...]
        xi = xi_ref[...]
        dar = jnp.sum(gsr * xr + gsi * xi, axis=0, keepdims=True)
        dai = jnp.sum(gsi * xr - gsr * xi, axis=0, keepdims=True)
        grb = gr.astype(BF16)
        gib = gi.astype(BF16)
        dbr = _dot_tn(ub, grb)
        dbi = _dot_tn(ub, gib)
        dcr = _dot_tn(dyb, xr.astype(BF16))
        dci = _dot_tn(dyb, xi.astype(BF16))
        du_ref[...] = dy * d_ref[...] + _dot_nt(grb, bre_ref[...]) + _dot_nt(gib, bim_ref[...])
        ddv = jnp.sum(dy * u_, axis=0, keepdims=True)

        @pl.when(c == 0)
        def _():
            dd_ref[...] = ddv
            dcr_ref[...] = dcr
            dci_ref[...] = dci
            dbr_ref[...] = dbr
            dbi_ref[...] = dbi
            dar_ref[...] = dar
            dai_ref[...] = dai

        @pl.when(c > 0)
        def _():
            dd_ref[...] += ddv
            dcr_ref[...] += dcr
            dci_ref[...] += dci
            dbr_ref[...] += dbr
            dbi_ref[...] += dbi
            dar_ref[...] += dar
            dai_ref[...] += dai

    tok = pl.BlockSpec((t, cw), lambda j, c: (nc - 1 - c, j))
    st = pl.BlockSpec((t, lw), lambda j, c: (nc - 1 - c, j))
    wb = pl.BlockSpec((None, cw, lw), lambda j, c: (j, 0, 0))
    wc = pl.BlockSpec((None, lw, cw), lambda j, c: (j, 0, 0))
    pw = pl.BlockSpec((SUB, lw), lambda j, c: (0, j))
    vec_c = pl.BlockSpec((1, cw), lambda j, c: (0, j))
    vec_l = pl.BlockSpec((1, lw), lambda j, c: (0, j))
    return pl.pallas_call(
        kern, name=name, grid=(nt, nc),
        in_specs=[tok, tok, tok, st, st, wb, wb, wc, wc, pw, pw, pw, pw, vec_c],
        out_specs=[tok, vec_c, wb, wb, wb, wb, vec_l, vec_l],
        out_shape=[jax.ShapeDtypeStruct((s, d), F32), jax.ShapeDtypeStruct((1, d), F32)]
        + [jax.ShapeDtypeStruct((nt, cw, lw), F32)] * 4
        + [jax.ShapeDtypeStruct((1, nt * lw), F32)] * 2,
        scratch_shapes=[pltpu.VMEM((1, lw), F32), pltpu.VMEM((1, lw), F32),
                        pltpu.VMEM((t, lw), F32), pltpu.VMEM((t, lw), F32)],
        compiler_params=_params(("parallel", "arbitrary")),
    )(dz, y, u, xs_re, xs_im, bbd_re, bbd_im, cbd_re, cbd_im, pw_re, pw_im, pf_re, pf_im, dskip)


def _s5_discretize(lam_re, lam_im, log_dt, b_re, b_im):
    dt = jnp.exp(log_dt)[:, None]
    mag = jnp.exp(lam_re * dt)
    ang = lam_im * dt
    lb_re = mag * jnp.cos(ang)
    lb_im = mag * jnp.sin(ang)
    nr = lb_re - 1.0
    den = lam_re * lam_re + lam_im * lam_im
    f_re = (nr * lam_re + lb_im * lam_im) / den
    f_im = (lb_im * lam_re - nr * lam_im) / den
    bb_re = f_re[..., None] * b_re - f_im[..., None] * b_im
    bb_im = f_re[..., None] * b_im + f_im[..., None] * b_re
    return lb_re, lb_im, bb_re, bb_im


def _block_diag(w):
    nt, ng, a, b = w.shape
    eye = jnp.eye(ng, dtype=w.dtype)
    return (w[:, :, :, None, :] * eye[None, :, None, :, None]).reshape(nt, ng * a, ng * b)


def _block_diag_take(w, a, b):
    nt = w.shape[0]
    ng = GROUPS_PER_TILE
    w5 = w.reshape(nt, ng, a, ng, b)
    on_diagonal = jnp.eye(ng, dtype=bool)[None, :, None, :, None]
    return jnp.sum(jnp.where(on_diagonal, w5, 0.0), axis=3).reshape(nt * ng, a, b)


def _att_combine(outs, lses, name):
    s, d = outs[0].shape
    tm = _tile(s, ROWS, 16)
    ng = len(outs)

    def kern(*refs):
        for h in range(d // HEAD_DIM):
            cols = slice(h * HEAD_DIM, (h + 1) * HEAD_DIM)
            ls = [refs[ng + t][:, h:h + 1] for t in range(ng)]
            mx = functools.reduce(jnp.maximum, ls)
            es = [jnp.exp(l - mx) for l in ls]
            den = functools.reduce(lambda a, b: a + b, es)
            o = functools.reduce(lambda a, b: a + b, [es[t] / den * refs[t][:, cols] for t in range(ng)])
            refs[2 * ng][:, cols] = o
            refs[2 * ng + 1][:, cols] = o.astype(BF16)

    return pl.pallas_call(
        kern, name=name, grid=(s // tm,),
        in_specs=[_row_spec(tm, d)] * ng + [_row_spec(tm, d // HEAD_DIM)] * ng,
        out_specs=[_row_spec(tm, d)] * 2,
        out_shape=[jax.ShapeDtypeStruct((s, d), F32), jax.ShapeDtypeStruct((s, d), BF16)],
        compiler_params=_params(("parallel",)),
    )(*outs, *lses)


DIL_MAX = 16


def _slab(r):
    return 4 * (r % 4) + r // 4


def _to_slabs(xs, name):
    s, w = xs[0].shape
    n = s // DIL_MAX
    nx = len(xs)

    def kern(*refs):
        o_ref = refs[nx]
        for r in range(DIL_MAX):
            rows = [x_ref[pl.ds(r, n, stride=DIL_MAX), :] for x_ref in refs[:nx]]
            o_ref[_slab(r) * n:(_slab(r) + 1) * n, :] = functools.reduce(lambda a, b: a + b, rows)

    spec = pl.BlockSpec((s, 128), lambda i: (0, i))
    return pl.pallas_call(
        kern, name=name, grid=(w // 128,), in_specs=[spec] * nx, out_specs=spec,
        out_shape=jax.ShapeDtypeStruct((s, w), F32), compiler_params=_params(("parallel",)),
    )(*xs)


def _from_slabs(x, name):
    s, w = x.shape
    n = s // DIL_MAX

    def kern(x_ref, o_ref, ob_ref):
        for r in range(DIL_MAX):
            o_ref[pl.ds(r, n, stride=DIL_MAX), :] = x_ref[_slab(r) * n:(_slab(r) + 1) * n, :]
        ob_ref[...] = o_ref[...].astype(BF16)

    spec = pl.BlockSpec((s, 128), lambda i: (0, i))
    return pl.pallas_call(
        kern, name=name, grid=(w // 128,), in_specs=[spec], out_specs=[spec, spec],
        out_shape=[jax.ShapeDtypeStruct((s, w), F32), jax.ShapeDtypeStruct((s, w), BF16)],
        compiler_params=_params(("parallel",)),
    )(x)


def _norm_bf16(x, gains, name):
    s, d = x.shape
    tm = _tile(s, ROWS, 16)
    ng = len(gains)

    def kern(x_ref, *refs):
        x_ = x_ref[...]
        for t in range(ng):
            refs[ng + t][...] = _rms(x_, refs[t][...]).astype(BF16)

    return pl.pallas_call(
        kern, name=name, grid=(s // tm,),
        in_specs=[_row_spec(tm, d)] + [_vec_spec(d)] * ng, out_specs=[_row_spec(tm, d)] * ng,
        out_shape=[jax.ShapeDtypeStruct((s, d), BF16)] * ng,
        compiler_params=_params(("parallel",)),
    )(x, *gains)


def _att_geometry(dil, s):
    g = DIL_MAX // dil
    return s // DIL_MAX, g, max(ATT_BLK // g, 16)


def _att_mask(g, j, jb, rep):
    b = g * j

    def pos(i):
        sl, jj = i // j, i % j
        off = {1: 0, 4: sl, 16: sl // 4 + 4 * (sl % 4)}[g]
        return g * jj + off

    qi = lax.broadcasted_iota(jnp.int32, (rep * b, 2 * b), 0) % b
    ki = lax.broadcasted_iota(jnp.int32, (rep * b, 2 * b), 1)
    prev = ki < b
    dist = pos(qi) - pos(ki % b) + jnp.where(prev, b, 0)
    return (dist >= 0) & (dist <= ATT_BLK) & (jnp.logical_not(prev) | (jb > 0))


def _stack_heads(x, rep):
    return jnp.concatenate([x[:, h * HEAD_DIM:(h + 1) * HEAD_DIM] for h in range(rep)], axis=0)


def _att_fwd(q, kv, gi, dil, nh, name):
    s = q.shape[0]
    n, g, j = _att_geometry(dil, s)
    b = g * j
    ng = q.shape[1] // (nh * HEAD_DIM)
    rep = nh // N_KV_HEADS
    qw = rep * HEAD_DIM
    scale = HEAD_DIM ** -0.5
    q3 = q.reshape(DIL_MAX, n, q.shape[1])
    kv3 = kv.reshape(DIL_MAX, n, kv.shape[1])

    kw = N_KV_HEADS * HEAD_DIM

    def kern(q_ref, kc_ref, kp_ref, vc_ref, vp_ref, o_ref, l_ref):
        jb = pl.program_id(1)
        mask = _att_mask(g, j, jb, rep)
        for kh in range(N_KV_HEADS):
            hs = slice(kh * HEAD_DIM, (kh + 1) * HEAD_DIM)
            kc, kp, vc, vp = [r[:, :, hs].reshape(b, HEAD_DIM) for r in (kc_ref, kp_ref, vc_ref, vp_ref)]
            k2 = jnp.concatenate([kp, kc], axis=0)
            v2 = jnp.concatenate([vp, vc], axis=0)
            qs = _stack_heads(q_ref[:, :, kh * qw:(kh + 1) * qw].reshape(b, qw), rep)
            sc = jnp.where(mask, _dot_nt(qs, k2) * scale, NEG_INF)
            m = jnp.max(sc, axis=-1, keepdims=True)
            p = jnp.exp(sc - m)
            l = jnp.sum(p, axis=-1, keepdims=True)
            out = _dot((p / l).astype(BF16), v2)
            lse = m + jnp.log(l)
            for h in range(rep):
                cols = slice(kh * qw + h * HEAD_DIM, kh * qw + (h + 1) * HEAD_DIM)
                o_ref[:, :, cols] = out[h * b:(h + 1) * b].reshape(g, j, HEAD_DIM)
                l_ref[:, :, kh * rep + h:kh * rep + h + 1] = lse[h * b:(h + 1) * b].reshape(g, j, 1)

    def kv_spec(col, prev):
        if prev:
            return pl.BlockSpec((g, j, kw), lambda c, jb: (c, jnp.maximum(jb - 1, 0), col))
        return pl.BlockSpec((g, j, kw), lambda c, jb: (c, jb, col))

    out_spec = pl.BlockSpec((g, j, nh * HEAD_DIM), lambda c, jb: (c, jb, 0))
    out, lse = pl.pallas_call(
        kern, name=name, grid=(dil, n // j),
        in_specs=[pl.BlockSpec((g, j, nh * HEAD_DIM), lambda c, jb: (c, jb, gi)),
                  kv_spec(0, False), kv_spec(0, True), kv_spec(1, False), kv_spec(1, True)],
        out_specs=[out_spec, pl.BlockSpec((g, j, nh), lambda c, jb: (c, jb, 0))],
        out_shape=[jax.ShapeDtypeStruct((DIL_MAX, n, nh * HEAD_DIM), F32),
                   jax.ShapeDtypeStruct((DIL_MAX, n, nh), F32)],
        compiler_params=_params(("parallel", "arbitrary")),
    )(q3, kv3, kv3, kv3, kv3)
    del ng
    return out.reshape(s, nh * HEAD_DIM), lse.reshape(s, nh)


def _att_bwd(q, kv, do, o, lses, dq_all, gi, dil, nh, name):
    s = q.shape[0]
    n, g, j = _att_geometry(dil, s)
    b = g * j
    nb = n // j
    ng = len(lses)
    rep = nh // N_KV_HEADS
    qw = rep * HEAD_DIM
    dm = nh * HEAD_DIM
    scale = HEAD_DIM ** -0.5
    q3 = q.reshape(DIL_MAX, n, q.shape[1])
    kv3 = kv.reshape(DIL_MAX, n, kv.shape[1])
    wide = [a.reshape(DIL_MAX, n, dm) for a in (do, o)] + [a.reshape(DIL_MAX, n, nh) for a in lses]
    kw = N_KV_HEADS * HEAD_DIM

    def kern(q_ref, kc_ref, kp_ref, vc_ref, vp_ref, do_ref, o_ref, *refs):
        l_refs = refs[:ng]
        dq_ref, dk_ref, dv_ref, ck_ref, cv_ref = refs[ng + 1:]
        jb = pl.program_id(1)

        @pl.when(jb == 0)
        def _():
            ck_ref[...] = jnp.zeros_like(ck_ref)
            cv_ref[...] = jnp.zeros_like(cv_ref)

        @pl.when(jb < nb)
        def _():
            mask = _att_mask(g, j, jb, rep)
            for kh in range(N_KV_HEADS):
                hs = slice(kh * HEAD_DIM, (kh + 1) * HEAD_DIM)
                ws = slice(kh * qw, (kh + 1) * qw)

                def stacked(ref):
                    return _stack_heads(ref[:, :, ws].reshape(b, qw), rep)

                kc, kp, vc, vp = [r[:, :, hs].reshape(b, HEAD_DIM) for r in (kc_ref, kp_ref, vc_ref, vp_ref)]
                k2 = jnp.concatenate([kp, kc], axis=0)
                v2 = jnp.concatenate([vp, vc], axis=0)
                qs = stacked(q_ref)
                ls = [jnp.concatenate([r[:, :, kh * rep + h:kh * rep + h + 1].reshape(b, 1) for h in range(rep)],
                                      axis=0) for r in l_refs]
                mx = functools.reduce(jnp.maximum, ls)
                den = functools.reduce(lambda a, c: a + c, [jnp.exp(l - mx) for l in ls])
                lse_g = ls[gi]
                w = jnp.exp(lse_g - mx) / den
                do_ = stacked(do_ref)
                ct = w * jnp.sum(do_ * stacked(o_ref), axis=-1, keepdims=True)
                dob = (w * do_).astype(BF16)
                p = jnp.exp(jnp.where(mask, _dot_nt(qs, k2) * scale, NEG_INF) - lse_g)
                ds = (p * (_dot_nt(dob, v2) - ct) * scale).astype(BF16)
                dq = (_dot(ds, k2)).astype(BF16)
                for h in range(rep):
                    cols = slice(kh * qw + h * HEAD_DIM, kh * qw + (h + 1) * HEAD_DIM)
                    dq_ref[:, :, cols] = dq[h * b:(h + 1) * b].reshape(g, j, HEAD_DIM)
                dk2 = _dot_tn(ds, qs)
                dv2 = _dot_tn(p.astype(BF16), dob)
                dk_ref[:, :, hs] = (ck_ref[:, hs] + dk2[:b]).reshape(g, j, HEAD_DIM)
                dv_ref[:, :, hs] = (cv_ref[:, hs] + dv2[:b]).reshape(g, j, HEAD_DIM)
                ck_ref[:, hs] = dk2[b:]
                cv_ref[:, hs] = dv2[b:]

        @pl.when(jb == nb)
        def _():
            dk_ref[...] = ck_ref[...].reshape(g, j, kw)
            dv_ref[...] = cv_ref[...].reshape(g, j, kw)

    def jq(jb):
        return jnp.minimum(jb, nb - 1)

    def kv_spec(col, prev):
        if prev:
            return pl.BlockSpec((g, j, kw), lambda c, jb: (c, jnp.maximum(jq(jb) - 1, 0), col))
        return pl.BlockSpec((g, j, kw), lambda c, jb: (c, jq(jb), col))

    wide_spec = pl.BlockSpec((g, j, dm), lambda c, jb: (c, jq(jb), 0))
    dkv_spec = pl.BlockSpec((g, j, kw), lambda c, jb: (c, jnp.maximum(jb - 1, 0), 0))
    dq, dk, dv = pl.pallas_call(
        kern, name=name, grid=(dil, nb + 1),
        in_specs=[pl.BlockSpec((g, j, dm), lambda c, jb: (c, jq(jb), gi)),
                  kv_spec(0, False), kv_spec(0, True), kv_spec(1, False), kv_spec(1, True)]
        + [wide_spec] * 2 + [pl.BlockSpec((g, j, nh), lambda c, jb: (c, jq(jb), 0))] * ng + [_ANY],
        out_specs=[pl.BlockSpec((g, j, dm), lambda c, jb: (c, jq(jb), gi)), dkv_spec, dkv_spec],
        out_shape=[jax.ShapeDtypeStruct((DIL_MAX, n, dq_all.shape[1]), BF16),
                   jax.ShapeDtypeStruct((DIL_MAX, n, kw), F32),
                   jax.ShapeDtypeStruct((DIL_MAX, n, kw), F32)],
        scratch_shapes=[pltpu.VMEM((b, kw), F32), pltpu.VMEM((b, kw), F32)],
        input_output_aliases={7 + ng: 0},
        compiler_params=_params(("parallel", "arbitrary")),
    )(q3, kv3, kv3, kv3, kv3, *wide, dq_all.reshape(DIL_MAX, n, dq_all.shape[1]))
    return dq.reshape(s, -1), dk.reshape(s, -1), dv.reshape(s, -1)


def _dkv_sum(dks, dvs, name):
    s, w = dks[0].shape
    tm = _tile(s, ROWS, 16)
    ng = len(dks)

    def kern(*refs):
        o_ref = refs[2 * ng]
        o_ref[:, :w] = functools.reduce(lambda a, b: a + b, [refs[t][...] for t in range(ng)]).astype(BF16)
        o_ref[:, w:] = functools.reduce(lambda a, b: a + b, [refs[ng + t][...] for t in range(ng)]).astype(BF16)

    return pl.pallas_call(
        kern, name=name, grid=(s // tm,),
        in_specs=[_row_spec(tm, w)] * (2 * ng),
        out_specs=_row_spec(tm, 2 * w),
        out_shape=jax.ShapeDtypeStruct((s, 2 * w), BF16),
        compiler_params=_params(("parallel",)),
    )(*dks, *dvs)


def _ffn_fwd(hf, weight, layer, tag):
    a, gu = _ffn_in_act(hf, weight(f"w_in{layer}", hf), f"{tag}_in")
    o = _mm_nn(a, weight(f"w_out{layer}", a), F32, f"{tag}_out", tn=1024, tk=5632)
    return gu, a, o


def _ffn_bwd(dresb, hf, gu, a, weight, emit, layer, tag):
    w_in, w_out = weight(f"w_in{layer}", None), weight(f"w_out{layer}", None)
    zero = emit(f"w_out{layer}", _mm_tn(a, dresb, 1, f"{tag}_out_dw", tn=1024))
    dgu = _ffn_out_dx_act(dresb, w_out, gu, f"{tag}_out_dx")
    zero = zero + emit(f"w_in{layer}",
                       _mm_tn(hf, dgu, w_in.shape[0], f"{tag}_in_dw", tn=FFN_TILE, paired=True))
    return _mm_nt(dgu, w_in, f"{tag}_in_dx", tr=FFN_TILE, tkc=512, paired=True), zero


def _local_step(x, target, small, weight, emit, stage):
    s, d = x.shape
    nh = d // HEAD_DIM
    n_groups = d // S5_GROUP_CH
    nt = n_groups // GROUPS_PER_TILE
    p_, c_ = S5_STATE, S5_GROUP_CH

    disc_in = (small["lam_re"], small["lam_im"], small["log_dt"], small["b_re"], small["b_im"])
    (lb_re, lb_im, bb_re, bb_im), disc_vjp = jax.vjp(_s5_discretize, *disc_in)
    del lb_re, lb_im
    dt = jnp.exp(small["log_dt"])[:, None]
    def pole_powers(exponents):
        k = exponents[:, None, None]
        mag = jnp.exp(k * (small["lam_re"] * dt)[None])
        ang = k * (small["lam_im"] * dt)[None]
        return ((mag * jnp.cos(ang)).reshape(SUB, n_groups * p_),
                (mag * jnp.sin(ang)).reshape(SUB, n_groups * p_))

    pw_re, pw_im = pole_powers(jnp.arange(1, SUB + 1, dtype=F32))
    pf_re, pf_im = pole_powers(jnp.arange(SUB, 0, -1, dtype=F32))
    bbd_re = _block_diag(bb_re.transpose(0, 2, 1).reshape(nt, GROUPS_PER_TILE, c_, p_)).astype(BF16)
    bbd_im = _block_diag(bb_im.transpose(0, 2, 1).reshape(nt, GROUPS_PER_TILE, c_, p_)).astype(BF16)
    cbd_re = _block_diag(small["c_re"].transpose(0, 2, 1).reshape(nt, GROUPS_PER_TILE, p_, c_)).astype(BF16)
    cbd_im = _block_diag(small["c_im"].transpose(0, 2, 1).reshape(nt, GROUPS_PER_TILE, p_, c_)).astype(BF16)

    h0 = _norm_f32(x, small["a_norm"], "s5_norm")
    xs_re, xs_im, y0, z = _s5_fwd(h0, bbd_re, bbd_im, cbd_re, cbd_im, pw_re, pw_im, small["s5_d"], "s5_fwd")
    zero = stage("glu", z)
    vg = _mm_nn(z, weight("w_glu", z), F32, "glu_mm", tn=1024)
    x1, hf0 = _glu_res_norm(vg, x, small["ffn_norm0"] + zero, "glu_res_norm")
    gu0, a0, o0 = _ffn_fwd(hf0, weight, 0, "ffn0")
    zero = stage("attention", a0)
    x2 = _to_slabs([o0, x1], "to_slabs")
    kvn, h1 = _norm_bf16(x2, [small["kv_norm"] + zero, small["b_norm"]], "att_norms")
    kv = _mm_nn(kvn, weight("w_kv", kvn), BF16, "kv_mm", tn=1024)
    q = _mm_nn(h1, weight("w_q", kv), BF16, "q_mm", tn=1536)
    outs, lses = [], []
    for gi, (window, dil) in enumerate(PATTERNS):
        assert window // dil == ATT_BLK
        og, lg = _att_fwd(q, kv, gi, dil, nh, f"att_fwd{gi}")
        outs.append(og)
        lses.append(lg)
    oatt, oattb = _att_combine(outs, lses, "att_combine")
    ao = _mm_nn(oattb, weight("w_o", oattb), F32, "o_mm", tn=1024)
    x3, hf1 = _res_norm(ao, x2, [small["ffn_norm1"] + stage("ffn1", oattb)], "att_res_norm")
    gu1, a1, o1 = _ffn_fwd(hf1, weight, 1, "ffn1")
    dres, dresb, loss_rows, d_final = _loss_head(
        o1, x3, small["final_norm"], _to_slabs([target], "target_to_slabs"), "loss_head")

    dhf1, zero = _ffn_bwd(dresb, hf1, gu1, a1, weight, emit, 1, "ffn1")
    dres, dresb, d_ffn_norm1 = _norm_bwd(x3, [small["ffn_norm1"] + zero], [dhf1], dres, "ffn1_norm_bwd")
    zero = emit("w_o", _mm_tn(oattb, dresb, 1, "o_dw", tn=1024))
    doatt = _mm_nt(dresb, weight("w_o", None), "o_dx", tr=2048)
    dks, dvs = [], []
    dq = lax.empty(q.shape, BF16)
    for gi, (window, dil) in enumerate(PATTERNS):
        dq, dk_g, dv_g = _att_bwd(q, kv, doatt, oatt, lses, dq, gi, dil, nh, f"att_bwd{gi}")
        dks.append(dk_g)
        dvs.append(dv_g)
    dkv = _dkv_sum(dks, dvs, "dkv_sum")
    w_q = weight("w_q", None)
    zero = zero + emit("w_q", _mm_tn(h1, dq, w_q.shape[0], "q_dw", tn=1536))
    zero = zero + emit("w_kv", _mm_tn(kvn, dkv, 1, "kv_dw", tn=1024))
    dh1 = _mm_nt(dq, w_q, "q_dx", tr=1536)
    dkvn = _mm_nt(dkv, weight("w_kv", None), "kv_dx", tr=1024)
    dres, dresb, d_b_norm, d_kv_norm = _norm_bwd(
        x2, [small["b_norm"] + zero, small["kv_norm"]], [dh1, dkvn], dres, "att_norm_bwd")
    dres, dresb = _from_slabs(dres, "from_slabs")
    dhf0, zero = _ffn_bwd(dresb, hf0, gu0, a0, weight, emit, 0, "ffn0")
    dres, dresb, d_ffn_norm0 = _norm_bwd(x1, [small["ffn_norm0"] + zero], [dhf0], dres, "ffn0_norm_bwd")
    del dresb
    dvg = _glu_bwd(dres, vg, "glu_bwd")
    w_glu = weight("w_glu", None)
    zero = emit("w_glu", _mm_tn(z, dvg, w_glu.shape[0], "glu_dw", tn=1024))
    dz = _mm_nt(dvg, w_glu, "glu_dx", tr=1024)
    dh0, d_s5_d, dcr, dci_neg, dbr, dbi, dar, dai = _s5_bwd(
        dz, y0, h0, xs_re, xs_im, bbd_re, bbd_im, cbd_re, cbd_im, pw_re, pw_im, pf_re, pf_im,
        small["s5_d"] + zero, "s5_bwd")
    grad_x, _, d_a_norm = _norm_bwd(x, [small["a_norm"]], [dh0], dres, "s5_norm_bwd")

    d_bb_re = _block_diag_take(dbr, c_, p_).transpose(0, 2, 1)
    d_bb_im = _block_diag_take(dbi, c_, p_).transpose(0, 2, 1)
    d_c_re = _block_diag_take(dcr, c_, p_)
    d_c_im = -_block_diag_take(dci_neg, c_, p_)
    d_lam_re, d_lam_im, d_log_dt, d_b_re, d_b_im = disc_vjp(
        (dar.reshape(n_groups, p_), dai.reshape(n_groups, p_), d_bb_re, d_bb_im))

    d_small = dict(lam_re=d_lam_re, lam_im=d_lam_im, log_dt=d_log_dt, b_re=d_b_re, b_im=d_b_im,
                   c_re=d_c_re, c_im=d_c_im, s5_d=d_s5_d, a_norm=d_a_norm, ffn_norm0=d_ffn_norm0,
                   ffn_norm1=d_ffn_norm1, b_norm=d_b_norm, kv_norm=d_kv_norm, final_norm=d_final)
    return loss_rows, grad_x, d_small


def _place():
    x, y, c = lax.axis_index("x"), lax.axis_index("y"), lax.axis_index("c")
    return x, y, c, [(1 - x, y), (x, 1 - y), (1 - x, 1 - y)]


_ANY = pl.BlockSpec(memory_space=pl.ANY)


_HBM = pl.BlockSpec(memory_space=pltpu.HBM)
_SEM = pl.BlockSpec(memory_space=pltpu.SEMAPHORE)
_EFFECT = pltpu.SideEffectType.DATAFLOW_SIDE_EFFECTING


def _in_hbm(a):
    return pltpu.with_memory_space_constraint(a, pltpu.HBM)


def _cast_place(shards, layer, chip, name):
    _, r, c = shards.shape
    tm = _tile(r, ROWS, 16)

    def kern(chip_ref, x_ref, o_ref):
        del chip_ref
        o_ref[...] = x_ref[...].astype(BF16)

    return pl.pallas_call(
        kern, name=name,
        grid_spec=pltpu.PrefetchScalarGridSpec(
            num_scalar_prefetch=1, grid=(r // tm,),
            in_specs=[pl.BlockSpec((None, tm, c), lambda i, ch: (layer, i, 0))],
            out_specs=pl.BlockSpec((None, tm, c), lambda i, ch: (ch[0], i, 0))),
        out_shape=jax.ShapeDtypeStruct((N_CHIPS, r, c), BF16),
        compiler_params=_params(("parallel",)),
    )(chip, shards)


def _my_part(land, block, c, halves):
    if not halves:
        return land.at[block]
    half = land.shape[1] // 2
    return land.at[block, pl.ds(c * half, half)]


def _gather_start(lands, name, halves):
    n = len(lands)

    def body(*refs):
        land = refs[:n]
        send, recv = refs[n:2 * n], refs[2 * n:3 * n]
        token = refs[4 * n]
        x, y, c, peers = _place()
        me = 2 * x + y
        for a in range(n):
            for k, (px, py) in enumerate(peers):
                part = _my_part(land[a], me, c, halves)
                pltpu.make_async_remote_copy(
                    src_ref=part, dst_ref=part, send_sem=send[a].at[k], recv_sem=recv[a].at[k],
                    device_id=(px, py, c), device_id_type=MESH).start()
        token[...] = jnp.zeros_like(token)

    outs = pl.pallas_call(
        body, name=name,
        out_shape=[pltpu.SemaphoreType.DMA((3,))] * (2 * n) + [pltpu.HBM(a.shape, a.dtype) for a in lands]
        + [jax.ShapeDtypeStruct((8, 128), F32)],
        in_specs=[_HBM] * n,
        out_specs=[_SEM] * (2 * n) + [_HBM] * n + [pl.BlockSpec(memory_space=pltpu.VMEM)],
        input_output_aliases={i: 2 * n + i for i in range(n)},
        compiler_params=pltpu.CompilerParams(has_side_effects=_EFFECT),
    )(*[_in_hbm(a) for a in lands])
    return outs[:n], outs[n:2 * n], outs[2 * n:3 * n], outs[3 * n]


def _gather_wait(lands, sends, recvs, after, name, halves):
    n = len(lands)

    def body(*refs):
        land, send, recv = refs[:n], refs[n:2 * n], refs[2 * n:3 * n]
        x, y, c, peers = _place()
        me = 2 * x + y
        for a in range(n):
            for k, (px, py) in enumerate(peers):
                cp = pltpu.make_async_remote_copy(
                    src_ref=_my_part(land[a], me, c, halves), dst_ref=_my_part(land[a], 2 * px + py, c, halves),
                    send_sem=send[a].at[k], recv_sem=recv[a].at[k], device_id=(px, py, c), device_id_type=MESH)
                cp.wait_send()
                cp.wait_recv()

    return pl.pallas_call(
        body, name=name,
        out_shape=[pltpu.HBM(a.shape, a.dtype) for a in lands],
        in_specs=[_HBM] * n + [_SEM] * (2 * n) + [_ANY], out_specs=[_HBM] * n,
        input_output_aliases={i: i for i in range(n)},
        compiler_params=pltpu.CompilerParams(has_side_effects=_EFFECT),
    )(*lands, *sends, *recvs, after)


def _forward_start(lands, name):
    n = len(lands)

    def body(*refs):
        land = refs[:n]
        send, recv = refs[n:2 * n], refs[2 * n:3 * n]
        token = refs[4 * n]
        x, y, c, peers = _place()
        for a in range(n):
            for k, (px, py) in enumerate(peers):
                part = _my_part(land[a], 2 * px + py, c, True)
                pltpu.make_async_remote_copy(
                    src_ref=part, dst_ref=part, send_sem=send[a].at[k], recv_sem=recv[a].at[k],
                    device_id=(x, y, 1 - c), device_id_type=MESH).start()
        token[...] = jnp.zeros_like(token)

    outs = pl.pallas_call(
        body, name=name,
        out_shape=[pltpu.SemaphoreType.DMA((3,))] * (2 * n) + [pltpu.HBM(a.shape, a.dtype) for a in lands]
        + [jax.ShapeDtypeStruct((8, 128), F32)],
        in_specs=[_HBM] * n,
        out_specs=[_SEM] * (2 * n) + [_HBM] * n + [pl.BlockSpec(memory_space=pltpu.VMEM)],
        input_output_aliases={i: 2 * n + i for i in range(n)},
        compiler_params=pltpu.CompilerParams(has_side_effects=_EFFECT),
    )(*[_in_hbm(a) for a in lands])
    return outs[:n], outs[n:2 * n], outs[2 * n:3 * n], outs[3 * n]


def _forward_wait(land, send, recv, after, name):
    def body(land_ref, send_sem, recv_sem, after_ref, land_out):
        del after_ref, land_out
        x, y, c, peers = _place()
        for k, (px, py) in enumerate(peers):
            cp = pltpu.make_async_remote_copy(
                src_ref=_my_part(land_ref, 2 * px + py, c, True), dst_ref=_my_part(land_ref, 2 * px + py, 1 - c, True),
                send_sem=send_sem.at[k], recv_sem=recv_sem.at[k], device_id=(x, y, 1 - c), device_id_type=MESH)
            cp.wait_send()
            cp.wait_recv()

    return pl.pallas_call(
        body, name=name,
        out_shape=pltpu.HBM(land.shape, land.dtype),
        in_specs=[_HBM, _SEM, _SEM, _ANY], out_specs=_HBM,
        input_output_aliases={0: 0},
        compiler_params=pltpu.CompilerParams(has_side_effects=_EFFECT),
    )(land, send, recv, after)


def _scatter_start(g, name):
    def body(g_ref, land_ref, send, recv, g_out, land_out, token):
        del g_out, land_out
        x, y, c, peers = _place()
        for k, (px, py) in enumerate(peers):
            pltpu.make_async_remote_copy(
                src_ref=g_ref.at[2 * px + py], dst_ref=land_ref.at[k], send_sem=send.at[k], recv_sem=recv.at[k],
                device_id=(px, py, c), device_id_type=MESH).start()
        token[...] = jnp.zeros_like(token)

    land = lax.empty((3,) + g.shape[1:], g.dtype)
    return pl.pallas_call(
        body, name=name,
        out_shape=(pltpu.SemaphoreType.DMA((3,)), pltpu.SemaphoreType.DMA((3,)),
                   pltpu.HBM(g.shape, g.dtype), pltpu.HBM(land.shape, land.dtype),
                   jax.ShapeDtypeStruct((8, 128), F32)),
        in_specs=(_HBM, _HBM), out_specs=(_SEM, _SEM, _HBM, _HBM, pl.BlockSpec(memory_space=pltpu.VMEM)),
        input_output_aliases={0: 2, 1: 3},
        compiler_params=pltpu.CompilerParams(has_side_effects=_EFFECT),
    )(_in_hbm(g), _in_hbm(land))


def _scatter_wait(started, after):
    n = len(started)

    def body(*refs):
        gs, lands = refs[:n], refs[n:2 * n]
        sends, recvs = refs[2 * n:3 * n], refs[3 * n:4 * n]
        x, y, c, peers = _place()
        for a in range(n):
            for k, (px, py) in enumerate(peers):
                cp = pltpu.make_async_remote_copy(
                    src_ref=gs[a].at[2 * px + py], dst_ref=lands[a].at[k], send_sem=sends[a].at[k],
                    recv_sem=recvs[a].at[k], device_id=(px, py, c), device_id_type=MESH)
                cp.wait_send()
                cp.wait_recv()

    gs = [s[2] for s in started]
    lands = [s[3] for s in started]
    outs = pl.pallas_call(
        body, name="scatter_wait",
        out_shape=[pltpu.HBM(a.shape, a.dtype) for a in gs + lands],
        in_specs=[_HBM] * (2 * n) + [_SEM] * (2 * n) + [_ANY], out_specs=[_HBM] * (2 * n),
        input_output_aliases={i: i for i in range(2 * n)},
        compiler_params=pltpu.CompilerParams(has_side_effects=_EFFECT),
    )(*gs, *lands, *[s[0] for s in started], *[s[1] for s in started], after)
    return outs[:n], outs[n:]


def _sibling():
    return lax.axis_index("x"), lax.axis_index("y"), 1 - lax.axis_index("c")


def _swap_start(parts):
    n = len(parts)

    def body(*refs):
        ins, land = refs[:n], refs[n:2 * n]
        send, recv = refs[2 * n], refs[2 * n + 1]
        token = refs[4 * n + 2]
        for a in range(n):
            pltpu.make_async_remote_copy(
                src_ref=ins[a], dst_ref=land[a], send_sem=send.at[a], recv_sem=recv.at[a],
                device_id=_sibling(), device_id_type=MESH).start()
        token[...] = jnp.zeros_like(token)

    lands = [lax.empty(a.shape, a.dtype) for a in parts]
    outs = pl.pallas_call(
        body, name="swap_start",
        out_shape=[pltpu.SemaphoreType.DMA((n,))] * 2 + [pltpu.HBM(a.shape, a.dtype) for a in parts + lands]
        + [jax.ShapeDtypeStruct((8, 128), F32)],
        in_specs=[_HBM] * (2 * n),
        out_specs=[_SEM] * 2 + [_HBM] * (2 * n) + [pl.BlockSpec(memory_space=pltpu.VMEM)],
        input_output_aliases={i: 2 + i for i in range(2 * n)},
        compiler_params=pltpu.CompilerParams(has_side_effects=_EFFECT),
    )(*[_in_hbm(a) for a in parts + lands])
    return outs[0], outs[1], outs[2:2 + n], outs[2 + n:2 + 2 * n], outs[2 + 2 * n]


def _swap_wait(parts, lands, send, recv, after):
    n = len(parts)

    def body(*refs):
        ins, land = refs[:n], refs[n:2 * n]
        send_sem, recv_sem = refs[2 * n], refs[2 * n + 1]
        for a in range(n):
            cp = pltpu.make_async_remote_copy(
                src_ref=ins[a], dst_ref=land[a], send_sem=send_sem.at[a], recv_sem=recv_sem.at[a],
                device_id=_sibling(), device_id_type=MESH)
            cp.wait_send()
            cp.wait_recv()

    outs = pl.pallas_call(
        body, name="swap_wait",
        out_shape=[pltpu.HBM(a.shape, a.dtype) for a in list(parts) + list(lands)],
        in_specs=[_HBM] * (2 * n) + [_SEM] * 2 + [_ANY], out_specs=[_HBM] * (2 * n),
        input_output_aliases={i: i for i in range(2 * n)},
        compiler_params=pltpu.CompilerParams(has_side_effects=_EFFECT),
    )(*parts, *lands, send, recv, after)
    return outs[:n], outs[n:]


def _all_reduce_small(v):
    nd, r, w = v.shape
    assert nd == N_DEV

    def body(v_ref, out_ref, land_ref, red_ref, send1, recv1, send2, recv2):
        x, y, c = lax.axis_index("x"), lax.axis_index("y"), lax.axis_index("c")
        me = 4 * x + 2 * y + c
        peers = []
        for k in range(1, N_DEV):
            kx, ky, kc = (k >> 2) & 1, (k >> 1) & 1, k & 1
            peers.append((1 - x if kx else x, 1 - y if ky else y, 1 - c if kc else c))
        first = []
        for k, (px, py, pc) in enumerate(peers):
            cp = pltpu.make_async_remote_copy(
                src_ref=v_ref.at[4 * px + 2 * py + pc], dst_ref=land_ref.at[me], send_sem=send1.at[k],
                recv_sem=recv1.at[k], device_id=(px, py, pc), device_id_type=MESH)
            cp.start()
            first.append(cp)
        land_ref[me] = v_ref[me]
        for cp in first:
            cp.wait()
        acc = land_ref[0]
        for j in range(1, N_DEV):
            acc = acc + land_ref[j]
        red_ref[...] = acc
        second = []
        for k, (px, py, pc) in enumerate(peers):
            cp = pltpu.make_async_remote_copy(
                src_ref=red_ref, dst_ref=out_ref.at[me], send_sem=send2.at[k],
                recv_sem=recv2.at[k], device_id=(px, py, pc), device_id_type=MESH)
            cp.start()
            second.append(cp)
        out_ref[me] = acc
        for cp in second:
            cp.wait()

    vmem = pl.BlockSpec(memory_space=pltpu.VMEM)
    return pl.pallas_call(
        body, name="all_reduce_small",
        in_specs=[vmem], out_specs=vmem,
        out_shape=jax.ShapeDtypeStruct((nd, r, w), F32),
        scratch_shapes=[pltpu.VMEM((nd, r, w), F32), pltpu.VMEM((r, w), F32)]
        + [pltpu.SemaphoreType.DMA((N_DEV - 1,))] * 4,
        compiler_params=pltpu.CompilerParams(vmem_limit_bytes=VMEM_LIMIT),
    )(v)


def _adam_math(w, g, m, v):
    m = ADAM_B1 * m + (1.0 - ADAM_B1) * g
    v = ADAM_B2 * v + (1.0 - ADAM_B2) * (g * g)
    m_hat = m / (1.0 - ADAM_B1 ** ADAM_STEP)
    v_hat = v / (1.0 - ADAM_B2 ** ADAM_STEP)
    delta = -ADAM_LR * (m_hat / (jnp.sqrt(v_hat) + ADAM_EPS) + ADAM_WD * w)
    return delta, m, v


def _sum_blocks(own, got, chip, name):
    _, r, c = own.shape
    tm = _tile(r, ROWS, 16)

    def kern(chip_ref, own_ref, got_ref, o_ref):
        del chip_ref
        acc = own_ref[...].astype(F32)
        for k in range(3):
            acc = acc + got_ref[k].astype(F32)
        o_ref[...] = acc

    return pl.pallas_call(
        kern, name=name,
        grid_spec=pltpu.PrefetchScalarGridSpec(
            num_scalar_prefetch=1, grid=(r // tm,),
            in_specs=[pl.BlockSpec((None, tm, c), lambda i, ch: (ch[0], i, 0)),
                      pl.BlockSpec((3, tm, c), lambda i, ch: (0, i, 0))],
            out_specs=pl.BlockSpec((tm, c), lambda i, ch: (i, 0))),
        out_shape=jax.ShapeDtypeStruct((r, c), F32),
        compiler_params=_params(("parallel",)),
    )(chip, own, got)


def _adamw(parts, w, m, v, name):
    nl, r, c = w.shape
    assert len(parts) == nl
    tm = _tile(r, 128, 8)

    def kern(*refs):
        p_refs = refs[:2 * nl]
        w_ref, m_ref, v_ref, g_ref, d_ref, mo_ref, vo_ref = refs[2 * nl:]
        layer = pl.program_id(0)
        g = p_refs[0][...] + p_refs[1][...]
        for ll in range(1, nl):
            g = jnp.where(layer == ll, p_refs[2 * ll][...] + p_refs[2 * ll + 1][...], g)
        g_ref[...] = g
        d_ref[...], mo_ref[...], vo_ref[...] = _adam_math(w_ref[...], g, m_ref[...], v_ref[...])

    def part_spec(ll):
        return pl.BlockSpec((tm, c), lambda l, i: (jnp.where(l == ll, i, 0), 0))

    spec = pl.BlockSpec((None, tm, c), lambda l, i: (l, i, 0))
    return pl.pallas_call(
        kern, name=name, grid=(nl, r // tm),
        in_specs=[part_spec(ll) for ll in range(nl) for _ in range(2)] + [spec] * 3, out_specs=[spec] * 4,
        out_shape=[jax.ShapeDtypeStruct((nl, r, c), F32)] * 4,
        compiler_params=_params(("arbitrary", "parallel")),
    )(*[p for pair in parts for p in pair], w, m, v)


def _adamw_small(g, w, m, v, name):
    def kern(g_ref, w_ref, m_ref, v_ref, d_ref, mo_ref, vo_ref):
        d_ref[...], mo_ref[...], vo_ref[...] = _adam_math(w_ref[...], g_ref[...], m_ref[...], v_ref[...])

    return pl.pallas_call(
        kern, name=name,
        out_shape=[jax.ShapeDtypeStruct(g.shape, F32)] * 3,
        compiler_params=pltpu.CompilerParams(vmem_limit_bytes=VMEM_LIMIT),
    )(g, w, m, v)


def _pack(arrays, rows):
    flat = jnp.concatenate([a.reshape(-1).astype(F32) for a in arrays])
    return jnp.pad(flat, (0, rows * 128 - flat.shape[0])).reshape(rows, 128)


def _unpack(packed, shapes):
    flat = packed.reshape(-1)
    out, off = [], 0
    for shp in shapes:
        size = math.prod(shp)
        out.append(flat[off:off + size].reshape(shp))
        off += size
    return out


_REPLICATED = ["s5_lam_re", "s5_lam_im", "s5_log_dt", "s5_b_re", "s5_b_im", "s5_c_re", "s5_c_im",
               "ffn_norm", "b_norm_mix", "kv_norm", "final_norm"]
_CHIP_VECTORS = ["s5_d", "a_norm_mix"]
_BIG = ["s5_w_glu", "ffn_w_in", "ffn_w_out", "attn_w_q", "attn_w_o", "w_kv"]
_WEIGHT_ORDER = ["s5_lam_re", "s5_lam_im", "s5_log_dt", "s5_b_re", "s5_b_im", "s5_c_re", "s5_c_im", "s5_d",
                 "s5_w_glu", "a_norm_mix", "ffn_norm", "ffn_w_in", "ffn_w_out", "b_norm_mix", "attn_w_q",
                 "attn_w_o", "kv_norm", "w_kv", "final_norm"]


def _step(x, loss_target, w, m, v):
    s, d = x.shape[1], x.shape[2]
    chip = 2 * lax.axis_index("x") + lax.axis_index("y")

    col_sharded = dict(w_glu=("s5_w_glu", 0), w_in0=("ffn_w_in", 0), w_in1=("ffn_w_in", 1), w_q=("attn_w_q", 0))
    row_sharded = dict(w_out0=("ffn_w_out", 0), w_out1=("ffn_w_out", 1), w_o=("attn_w_o", 0), w_kv=("w_kv", 0))
    names = ["w_glu", "w_in0", "w_out0", "w_kv", "w_q", "w_o", "w_in1", "w_out1"]
    local = {**col_sharded, **row_sharded}

    def layers(a):
        return a.reshape((-1,) + a.shape[-2:])

    chip_arr = jnp.reshape(chip, (1,)).astype(jnp.int32)
    vector_lands = [lax.dynamic_update_slice(jnp.zeros((N_CHIPS,) + w[n].shape, F32), w[n][None], (chip, 0, 0))
                    for n in _CHIP_VECTORS]
    v_send, v_recv, v_land, _ = _gather_start(vector_lands, "gather_start_vectors", False)
    first = 2
    lands = [_cast_place(layers(w[local[n][0]]), local[n][1], chip_arr, f"cast_{n}") for n in names[:first]]
    send, recv, land_thru, first_token = _gather_start(lands, "gather_start_first", True)
    lands = [_cast_place(layers(w[local[n][0]]), local[n][1], chip_arr, f"cast_{n}") for n in names[first:]]
    *later, rest_token = _gather_start(lands, "gather_start_rest", True)
    send, recv, land_thru = [list(a) + list(b) for a, b in zip((send, recv, land_thru), later)]
    vectors = _gather_wait(v_land, v_send, v_recv, jnp.zeros((8, 128), F32), "gather_wait_vectors", False)
    s5_d_full = vectors[0].reshape(1, d)
    a_norm_full = vectors[1].reshape(1, d) + (first_token[0, 0] + rest_token[0, 0])

    batches = dict(glu=["w_glu", "w_in0", "w_out0"], attention=["w_kv", "w_q", "w_o"], ffn1=["w_in1", "w_out1"])
    forwarded, arrived = {}, {}

    def stage(batch, after):
        idx = [names.index(n) for n in batches[batch]]
        got = _gather_wait([land_thru[i] for i in idx], [send[i] for i in idx], [recv[i] for i in idx], after,
                           f"gather_wait_{batch}", True)
        f_send, f_recv, f_land, token = _forward_start(got, f"forward_start_{batch}")
        forwarded.update(zip(batches[batch], zip(f_send, f_recv, f_land)))
        return token[0, 0]

    def weight(name, after):
        if name not in arrived:
            f_send, f_recv, f_land = forwarded[name]
            land = _forward_wait(f_land, f_send, f_recv, after, f"forward_wait_{name}")
            arrived[name] = land if name in col_sharded else land.reshape(1, -1, land.shape[-1])
        return arrived[name]

    started = {}

    def emit(name, dw):
        outs = _scatter_start(dw.reshape(N_CHIPS, -1, dw.shape[-1]), f"scatter_start_{name}")
        started[name] = outs[:4]
        return outs[4][0, 0]

    small = dict(lam_re=w["s5_lam_re"][0], lam_im=w["s5_lam_im"][0], log_dt=w["s5_log_dt"][0],
                 b_re=w["s5_b_re"][0], b_im=w["s5_b_im"][0], c_re=w["s5_c_re"][0], c_im=w["s5_c_im"][0],
                 s5_d=s5_d_full, a_norm=a_norm_full, ffn_norm0=w["ffn_norm"][0:1], ffn_norm1=w["ffn_norm"][1:2],
                 b_norm=w["b_norm_mix"], kv_norm=w["kv_norm"].reshape(1, d), final_norm=w["final_norm"].reshape(1, d))

    loss_rows, grad_x, d_small = _local_step(x[0], loss_target[0], small, weight, emit, stage)

    g4, got = _scatter_wait([started[n] for n in names], grad_x)
    partial = [_sum_blocks(o, r, chip_arr, f"sum_{n}") for n, o, r in zip(names, g4, got)]
    swap_send, swap_recv, partial, swap_land, swap_token = _swap_start(partial)
    result = {}

    rep_grads = [d_small["lam_re"], d_small["lam_im"], d_small["log_dt"], d_small["b_re"], d_small["b_im"],
                 d_small["c_re"], d_small["c_im"],
                 jnp.concatenate([d_small["ffn_norm0"], d_small["ffn_norm1"]], axis=0),
                 d_small["b_norm"], d_small["kv_norm"], d_small["final_norm"]]
    vec_grads = [d_small["s5_d"], d_small["a_norm"]]
    to_reduce = rep_grads + vec_grads + [jnp.sum(loss_rows).reshape(1) + swap_token[0, 0]]
    total = sum(math.prod(a.shape) for a in to_reduce)
    rows_per = -(-total // (N_DEV * 128 * 8)) * 8
    reduced = _all_reduce_small(_pack(to_reduce, N_DEV * rows_per).reshape(N_DEV, rows_per, 128))
    red = _unpack(reduced, [a.shape for a in to_reduce])
    loss = red[-1][0]
    g_small = dict(zip(_REPLICATED, [r.reshape(w[n].shape) for r, n in zip(red[:len(rep_grads)], _REPLICATED)]))
    for n, r in zip(_CHIP_VECTORS, red[len(rep_grads):-1]):
        g_small[n] = lax.dynamic_slice_in_dim(r.reshape(1, d), chip * (d // N_CHIPS), d // N_CHIPS, axis=1)
    small_names = _REPLICATED + _CHIP_VECTORS
    n_small = sum(math.prod(w[n].shape) for n in small_names)
    rows_small = -(-n_small // (128 * 8)) * 8
    packed = [_pack([src[n] for n in small_names], rows_small) for src in (g_small, w, m, v)]
    upd = _adamw_small(*packed, "adamw_small")
    shapes = [w[n].shape for n in small_names]
    for n, dl, mo, vo in zip(small_names, *[_unpack(u, shapes) for u in upd]):
        result[n] = [g_small[n], dl, mo, vo]

    partial, other = _swap_wait(partial, swap_land, swap_send, swap_recv, upd[0])
    part = dict(zip(names, zip(partial, other)))
    for name in _BIG:
        parts = [part[n] for n in sorted((n for n in names if local[n][0] == name), key=lambda n: local[n][1])]
        outs = _adamw(parts, layers(w[name]), layers(m[name]), layers(v[name]), f"adamw_{name}")
        result[name] = [o.reshape(w[name].shape) for o in outs]

    cols = [[result[n][t] for n in _WEIGHT_ORDER] for t in range(4)]
    return (loss, grad_x.reshape(x.shape), *cols[0], *cols[1], *cols[2], *cols[3])


def kernel(x, s5_lam_re, s5_lam_im, s5_log_dt, s5_b_re, s5_b_im, s5_c_re, s5_c_im, s5_d, s5_w_glu, a_norm_mix, ffn_norm, ffn_w_in, ffn_w_out, b_norm_mix, attn_w_q, attn_w_o, kv_norm, w_kv, final_norm, loss_target, m_s5_lam_re, m_s5_lam_im, m_s5_log_dt, m_s5_b_re, m_s5_b_im, m_s5_c_re, m_s5_c_im, m_s5_d, m_s5_w_glu, m_a_norm_mix, m_ffn_norm, m_ffn_w_in, m_ffn_w_out, m_b_norm_mix, m_attn_w_q, m_attn_w_o, m_kv_norm, m_w_kv, m_final_norm, v_s5_lam_re, v_s5_lam_im, v_s5_log_dt, v_s5_b_re, v_s5_b_im, v_s5_c_re, v_s5_c_im, v_s5_d, v_s5_w_glu, v_a_norm_mix, v_ffn_norm, v_ffn_w_in, v_ffn_w_out, v_b_norm_mix, v_attn_w_q, v_attn_w_o, v_kv_norm, v_w_kv, v_final_norm):
    w = dict(zip(_WEIGHT_ORDER, (s5_lam_re, s5_lam_im, s5_log_dt, s5_b_re, s5_b_im, s5_c_re, s5_c_im, s5_d, s5_w_glu, a_norm_mix, ffn_norm, ffn_w_in, ffn_w_out, b_norm_mix, attn_w_q, attn_w_o, kv_norm, w_kv, final_norm)))
    m = dict(zip(_WEIGHT_ORDER, (m_s5_lam_re, m_s5_lam_im, m_s5_log_dt, m_s5_b_re, m_s5_b_im, m_s5_c_re, m_s5_c_im, m_s5_d, m_s5_w_glu, m_a_norm_mix, m_ffn_norm, m_ffn_w_in, m_ffn_w_out, m_b_norm_mix, m_attn_w_q, m_attn_w_o, m_kv_norm, m_w_kv, m_final_norm)))
    v = dict(zip(_WEIGHT_ORDER, (v_s5_lam_re, v_s5_lam_im, v_s5_log_dt, v_s5_b_re, v_s5_b_im, v_s5_c_re, v_s5_c_im, v_s5_d, v_s5_w_glu, v_a_norm_mix, v_ffn_norm, v_ffn_w_in, v_ffn_w_out, v_b_norm_mix, v_attn_w_q, v_attn_w_o, v_kv_norm, v_w_kv, v_final_norm)))
    return _step(x, loss_target, w, m, v)
```

```python
import functools
import math

import jax
import jax.numpy as jnp
from jax import lax
from jax.experimental import pallas as pl
from jax.experimental.pallas import tpu as pltpu

F32 = jnp.float32
BF16 = jnp.bfloat16

S5_GROUP_CH = 16
S5_STATE = 64
GROUPS_PER_TILE = 8
HEAD_DIM = 128
N_KV_HEADS = 4
PATTERNS = ((128, 1), (512, 4), (2048, 16))
ATT_BLK = 128
EPS = 1e-6
NEG_INF = -1e30
SCAN_T = 256
ADAM_LR = 0.001
ADAM_B1 = 0.9
ADAM_B2 = 0.999
ADAM_EPS = 1e-08
ADAM_WD = 0.01
ADAM_STEP = 10
N_CHIPS = 4
N_DEV = 8
VMEM_LIMIT = 56 * 1024 * 1024
MESH = pl.DeviceIdType.MESH
GELU_K = math.sqrt(2.0 / math.pi)
GELU_C = 0.044715


def _tile(n, pref, unit=128):
    if n <= pref:
        return n
    best = None
    t = unit
    while t <= pref:
        if n % t == 0:
            best = t
        t += unit
    assert best is not None, (n, pref, unit)
    return best


def _params(sem):
    return pltpu.CompilerParams(dimension_semantics=sem, vmem_limit_bytes=VMEM_LIMIT)


def _dot(a, b):
    return jnp.dot(a, b, preferred_element_type=F32)


def _dot_nt(a, b):
    return lax.dot_general(a, b, (((1,), (1,)), ((), ())), preferred_element_type=F32)


def _dot_tn(a, b):
    return lax.dot_general(a, b, (((0,), (0,)), ((), ())), preferred_element_type=F32)


def _mm_nn(a, w, out_dtype, name, tm=512, tn=1536, tk=2048):
    m, k = a.shape
    nb, k2, nq = w.shape
    assert k == k2
    tm, tn, tk = _tile(m, tm, 8), _tile(nq, tn), _tile(k, tk)
    per, nk = nq // tn, k // tk

    def kern(a_ref, w_ref, o_ref, *acc):
        p = _dot(a_ref[...], w_ref[...])
        if nk == 1:
            o_ref[...] = p.astype(o_ref.dtype)
        else:
            acc_ref, = acc
            kk = pl.program_id(2)

            @pl.when(kk == 0)
            def _():
                acc_ref[...] = p

            @pl.when(kk > 0)
            def _():
                acc_ref[...] += p

            @pl.when(kk == nk - 1)
            def _():
                o_ref[...] = acc_ref[...].astype(o_ref.dtype)

    return pl.pallas_call(
        kern, name=name, grid=(nb * per, m // tm, nk),
        in_specs=[pl.BlockSpec((tm, tk), lambda j, i, kk: (i, kk)),
                  pl.BlockSpec((None, tk, tn), lambda j, i, kk: (j // per, kk, j % per))],
        out_specs=pl.BlockSpec((tm, tn), lambda j, i, kk: (i, j)),
        out_shape=jax.ShapeDtypeStruct((m, nb * nq), out_dtype),
        scratch_shapes=[] if nk == 1 else [pltpu.VMEM((tm, tn), F32)],
        compiler_params=_params(("parallel", "parallel", "arbitrary")),
    )(a, w)


def _paired_block(r, per, nb):
    j = r // 2
    return (r % 2) * (nb // 2) + j // per, j % per


def _mm_nt(a, w, name, tm=512, tr=1536, tkc=1024, paired=False):
    m, n = a.shape
    nb, k, nq = w.shape
    assert n == nb * nq
    tm, tr, tkc = _tile(m, tm, 8), _tile(nq, tr), _tile(k, tkc)
    per = nq // tr

    def kern(a_ref, w_ref, o_ref):
        acc = None
        for r in range(nb * per):
            blk, tile = _paired_block(r, per, nb) if paired else (r // per, r % per)
            p = _dot_nt(a_ref[:, r * tr:(r + 1) * tr], w_ref[blk, :, tile * tr:(tile + 1) * tr])
            acc = p if acc is None else acc + p
        o_ref[...] = acc

    return pl.pallas_call(
        kern, name=name, grid=(k // tkc, m // tm),
        in_specs=[pl.BlockSpec((tm, n), lambda kc, i: (i, 0)),
                  pl.BlockSpec((nb, tkc, nq), lambda kc, i: (0, kc, 0))],
        out_specs=pl.BlockSpec((tm, tkc), lambda kc, i: (i, kc)),
        out_shape=jax.ShapeDtypeStruct((m, k), F32),
        compiler_params=_params(("parallel", "parallel")),
    )(a, w)


def _mm_tn(a, dy, nb, name, tkk=512, tn=1536, paired=False):
    s, k = a.shape
    s2, n = dy.shape
    assert s == s2 and n % nb == 0
    nq = n // nb
    tkk, tn = _tile(k, tkk), _tile(nq, tn)
    per = nq // tn

    def w_block(j):
        return _paired_block(j, per, nb) if paired else (j // per, j % per)

    def kern(a_ref, dy_ref, o_ref):
        o_ref[...] = _dot_tn(a_ref[...], dy_ref[...]).astype(o_ref.dtype)

    return pl.pallas_call(
        kern, name=name, grid=(nb * per, k // tkk),
        in_specs=[pl.BlockSpec((s, tkk), lambda j, kk: (0, kk)),
                  pl.BlockSpec((s, tn), lambda j, kk: (0, j))],
        out_specs=pl.BlockSpec((None, tkk, tn), lambda j, kk: (w_block(j)[0], kk, w_block(j)[1])),
        out_shape=jax.ShapeDtypeStruct((nb, k, nq), BF16),
        compiler_params=_params(("parallel", "parallel")),
    )(a, dy)


ROWS = 256


def _rms(x, g):
    r = lax.rsqrt(jnp.mean(x * x, axis=-1, keepdims=True) + EPS)
    return x * r * g


def _rms_bwd(x, g, dh):
    r = lax.rsqrt(jnp.mean(x * x, axis=-1, keepdims=True) + EPS)
    xh = x * r
    dgx = dh * g
    dx = r * (dgx - xh * jnp.mean(dgx * xh, axis=-1, keepdims=True))
    return dx, dh * xh


def _sigmoid(x):
    return 1.0 / (1.0 + jnp.exp(-x))


def _gelu(y):
    return 0.5 * y * (1.0 + jnp.tanh(GELU_K * (y + GELU_C * y * y * y)))


def _gelu_grad(y):
    t = jnp.tanh(GELU_K * (y + GELU_C * y * y * y))
    return 0.5 * (1.0 + t) + 0.5 * y * (1.0 - t * t) * GELU_K * (1.0 + 3.0 * GELU_C * y * y)


def _row_spec(tm, d, col=0):
    return pl.BlockSpec((tm, d), lambda i: (i, col))


def _vec_spec(d):
    return pl.BlockSpec((1, d), lambda i: (0, 0))


def _acc_rows(ref, val, i):
    s = jnp.sum(val, axis=0, keepdims=True)

    @pl.when(i == 0)
    def _():
        ref[...] = s

    @pl.when(i > 0)
    def _():
        ref[...] += s


def _norm_f32(x, g, name):
    s, d = x.shape
    tm = _tile(s, ROWS, 8)

    def kern(x_ref, g_ref, h_ref):
        h_ref[...] = _rms(x_ref[...], g_ref[...])

    return pl.pallas_call(
        kern, name=name, grid=(s // tm,),
        in_specs=[_row_spec(tm, d), _vec_spec(d)],
        out_specs=_row_spec(tm, d),
        out_shape=jax.ShapeDtypeStruct((s, d), F32),
        compiler_params=_params(("parallel",)),
    )(x, g)


def _glu_res_norm(vg, x, g, name):
    s, d = x.shape
    tm = _tile(s, ROWS, 16)

    def kern(val_ref, gate_ref, x_ref, g_ref, x1_ref, hf_ref):
        x1 = x_ref[...] + val_ref[...] * _sigmoid(gate_ref[...])
        x1_ref[...] = x1
        hf_ref[...] = _rms(x1, g_ref[...]).astype(BF16)

    return pl.pallas_call(
        kern, name=name, grid=(s // tm,),
        in_specs=[_row_spec(tm, d, 0), _row_spec(tm, d, 1), _row_spec(tm, d), _vec_spec(d)],
        out_specs=[_row_spec(tm, d), _row_spec(tm, d)],
        out_shape=[jax.ShapeDtypeStruct((s, d), F32), jax.ShapeDtypeStruct((s, d), BF16)],
        compiler_params=_params(("parallel",)),
    )(vg, vg, x, g)


FFN_TILE = 1408


def _ffn_in_act(hf, w_in, name, tm=512):
    s, k = hf.shape
    nb, _, nq = w_in.shape
    tm, tn = _tile(s, tm, 16), _tile(nq, FFN_TILE)
    per = nq // tn
    nf = (nb // 2) * per

    def kern(h_ref, wg_ref, wu_ref, a_ref, gu_ref):
        h = h_ref[...]
        g = _dot(h, wg_ref[...])
        u = _dot(h, wu_ref[...])
        a_ref[...] = (g * _sigmoid(g) * u).astype(BF16)
        gu_ref[:, :tn] = g.astype(BF16)
        gu_ref[:, tn:] = u.astype(BF16)

    return pl.pallas_call(
        kern, name=name, grid=(nf, s // tm),
        in_specs=[pl.BlockSpec((tm, k), lambda j, i: (i, 0)),
                  pl.BlockSpec((None, k, tn), lambda j, i: (j // per, 0, j % per)),
                  pl.BlockSpec((None, k, tn), lambda j, i: (nb // 2 + j // per, 0, j % per))],
        out_specs=[pl.BlockSpec((tm, tn), lambda j, i: (i, j)),
                   pl.BlockSpec((tm, 2 * tn), lambda j, i: (i, j))],
        out_shape=[jax.ShapeDtypeStruct((s, nf * tn), BF16), jax.ShapeDtypeStruct((s, 2 * nf * tn), BF16)],
        compiler_params=_params(("parallel", "parallel")),
    )(hf, w_in, w_in)


def _ffn_out_dx_act(dresb, w_out, gu, name, tm=512):
    s, d = dresb.shape
    f = w_out.shape[1]
    tm = _tile(s, tm, 16)
    tn = _tile(f // 2, FFN_TILE)

    def kern(d_ref, w_ref, gu_ref, o_ref):
        da = _dot_nt(d_ref[...], w_ref[...])
        g = gu_ref[:, :tn].astype(F32)
        u = gu_ref[:, tn:].astype(F32)
        sg = _sigmoid(g)
        o_ref[:, :tn] = (da * u * sg * (1.0 + g * (1.0 - sg))).astype(BF16)
        o_ref[:, tn:] = (da * g * sg).astype(BF16)

    pair = pl.BlockSpec((tm, 2 * tn), lambda j, i: (i, j))
    return pl.pallas_call(
        kern, name=name, grid=(f // tn, s // tm),
        in_specs=[pl.BlockSpec((tm, d), lambda j, i: (i, 0)),
                  pl.BlockSpec((None, tn, d), lambda j, i: (0, j, 0)), pair],
        out_specs=pair,
        out_shape=jax.ShapeDtypeStruct((s, 2 * f), BF16),
        compiler_params=_params(("parallel", "parallel")),
    )(dresb, w_out, gu)


def _res_norm(o, x, gains, name):
    s, d = x.shape
    tm = _tile(s, ROWS, 16)
    ng = len(gains)

    def kern(o_ref, x_ref, *refs):
        xn = x_ref[...] + o_ref[...]
        refs[ng][...] = xn
        for t in range(ng):
            refs[ng + 1 + t][...] = _rms(xn, refs[t][...]).astype(BF16)

    return pl.pallas_call(
        kern, name=name, grid=(s // tm,),
        in_specs=[_row_spec(tm, d), _row_spec(tm, d)] + [_vec_spec(d)] * ng,
        out_specs=[_row_spec(tm, d)] * (1 + ng),
        out_shape=[jax.ShapeDtypeStruct((s, d), F32)] + [jax.ShapeDtypeStruct((s, d), BF16)] * ng,
        compiler_params=_params(("parallel",)),
    )(o, x, *gains)


def _loss_head(o, x, g, target, name):
    s, d = x.shape
    tm = _tile(s, ROWS, 16)

    def kern(o_ref, x_ref, g_ref, t_ref, dx_ref, dxb_ref, loss_ref, dg_ref):
        i = pl.program_id(0)
        x4 = x_ref[...] + o_ref[...]
        gg = g_ref[...]
        diff = _rms(x4, gg) - t_ref[...]
        dx, dgr = _rms_bwd(x4, gg, diff * (1.0 / d))
        dx_ref[...] = dx
        dxb_ref[...] = dx.astype(BF16)
        _acc_rows(loss_ref, diff * diff * (0.5 / d), i)
        _acc_rows(dg_ref, dgr, i)

    return pl.pallas_call(
        kern, name=name, grid=(s // tm,),
        in_specs=[_row_spec(tm, d), _row_spec(tm, d), _vec_spec(d), _row_spec(tm, d)],
        out_specs=[_row_spec(tm, d), _row_spec(tm, d), _vec_spec(d), _vec_spec(d)],
        out_shape=[jax.ShapeDtypeStruct((s, d), F32), jax.ShapeDtypeStruct((s, d), BF16),
                   jax.ShapeDtypeStruct((1, d), F32), jax.ShapeDtypeStruct((1, d), F32)],
        compiler_params=_params(("arbitrary",)),
    )(o, x, g, target)


def _norm_bwd(x, gains, dhs, dres, name):
    s, d = x.shape
    tm = _tile(s, ROWS, 16)
    ng = len(gains)

    def kern(x_ref, dres_ref, *refs):
        i = pl.program_id(0)
        x_ = x_ref[...]
        acc = dres_ref[...]
        for t in range(ng):
            dx, dgr = _rms_bwd(x_, refs[t][...], refs[ng + t][...])
            acc = acc + dx
            _acc_rows(refs[2 * ng + 2 + t], dgr, i)
        refs[2 * ng][...] = acc
        refs[2 * ng + 1][...] = acc.astype(BF16)

    return pl.pallas_call(
        kern, name=name, grid=(s // tm,),
        in_specs=[_row_spec(tm, d), _row_spec(tm, d)] + [_vec_spec(d)] * ng + [_row_spec(tm, d)] * ng,
        out_specs=[_row_spec(tm, d), _row_spec(tm, d)] + [_vec_spec(d)] * ng,
        out_shape=[jax.ShapeDtypeStruct((s, d), F32), jax.ShapeDtypeStruct((s, d), BF16)]
        + [jax.ShapeDtypeStruct((1, d), F32)] * ng,
        compiler_params=_params(("arbitrary",)),
    )(x, dres, *gains, *dhs)


def _glu_bwd(dmix, vg, name):
    s, d = dmix.shape
    tm = _tile(s, ROWS, 16)

    def kern(dm_ref, val_ref, gate_ref, o_ref):
        dm = dm_ref[...]
        sg = _sigmoid(gate_ref[...])
        o_ref[:, :d] = (dm * sg).astype(BF16)
        o_ref[:, d:] = (dm * val_ref[...] * sg * (1.0 - sg)).astype(BF16)

    return pl.pallas_call(
        kern, name=name, grid=(s // tm,),
        in_specs=[_row_spec(tm, d), _row_spec(tm, d, 0), _row_spec(tm, d, 1)],
        out_specs=_row_spec(tm, 2 * d),
        out_shape=jax.ShapeDtypeStruct((s, 2 * d), BF16),
        compiler_params=_params(("parallel",)),
    )(dmix, vg, vg)


SUB = 8


def _local_scan(vr, vi, pre_ref, pim_ref, reverse):
    sub = lax.broadcasted_iota(jnp.int32, vr.shape, 1)
    sign = -1.0 if reverse else 1.0
    for sh in (1, 2, 4):
        ar = pre_ref[sh - 1:sh, :][None]
        ai = sign * pim_ref[sh - 1:sh, :][None]
        keep = sub < SUB - sh if reverse else sub >= sh
        sr = jnp.where(keep, pltpu.roll(vr, SUB - sh if reverse else sh, 1), 0.0)
        si = jnp.where(keep, pltpu.roll(vi, SUB - sh if reverse else sh, 1), 0.0)
        vr, vi = vr + ar * sr - ai * si, vi + ar * si + ai * sr
    return vr, vi


def _s5_fwd(u, bbd_re, bbd_im, cbd_re, cbd_im, pw_re, pw_im, dskip, name):
    s, d = u.shape
    nt, cw, lw = bbd_re.shape
    t = _tile(s, SCAN_T, SUB)
    nc = s // t
    ng = t // SUB

    def kern(u_ref, bre_ref, bim_ref, cre_ref, cim_ref, pre_ref, pim_ref, d_ref,
             xr_ref, xi_ref, y_ref, z_ref, car_ref, cai_ref):
        c = pl.program_id(1)

        @pl.when(c == 0)
        def _():
            car_ref[...] = jnp.zeros_like(car_ref)
            cai_ref[...] = jnp.zeros_like(cai_ref)

        u_ = u_ref[...]
        ub = u_.astype(BF16)
        vr, vi = _local_scan(_dot(ub, bre_ref[...]).reshape(ng, SUB, lw),
                             _dot(ub, bim_ref[...]).reshape(ng, SUB, lw), pre_ref, pim_ref, False)
        cr = car_ref[...]
        ci = cai_ref[...]
        pr = pre_ref[...]
        pi = pim_ref[...]
        for gidx in range(ng):
            rows = slice(gidx * SUB, (gidx + 1) * SUB)
            gr = vr[gidx] + pr * cr - pi * ci
            gi = vi[gidx] + pr * ci + pi * cr
            xr_ref[rows, :] = gr
            xi_ref[rows, :] = gi
            cr, ci = gr[SUB - 1:SUB, :], gi[SUB - 1:SUB, :]
        car_ref[...] = cr
        cai_ref[...] = ci
        y = (_dot(xr_ref[...].astype(BF16), cre_ref[...]) - _dot(xi_ref[...].astype(BF16), cim_ref[...])
             + d_ref[...] * u_)
        y_ref[...] = y
        z_ref[...] = _gelu(y).astype(BF16)

    tok = pl.BlockSpec((t, cw), lambda j, c: (c, j))
    st = pl.BlockSpec((t, lw), lambda j, c: (c, j))
    return pl.pallas_call(
        kern, name=name, grid=(nt, nc),
        in_specs=[tok,
                  pl.BlockSpec((None, cw, lw), lambda j, c: (j, 0, 0)),
                  pl.BlockSpec((None, cw, lw), lambda j, c: (j, 0, 0)),
                  pl.BlockSpec((None, lw, cw), lambda j, c: (j, 0, 0)),
                  pl.BlockSpec((None, lw, cw), lambda j, c: (j, 0, 0)),
                  pl.BlockSpec((SUB, lw), lambda j, c: (0, j)),
                  pl.BlockSpec((SUB, lw), lambda j, c: (0, j)),
                  pl.BlockSpec((1, cw), lambda j, c: (0, j))],
        out_specs=[st, st, tok, tok],
        out_shape=[jax.ShapeDtypeStruct((s, nt * lw), F32), jax.ShapeDtypeStruct((s, nt * lw), F32),
                   jax.ShapeDtypeStruct((s, d), F32), jax.ShapeDtypeStruct((s, d), BF16)],
        scratch_shapes=[pltpu.VMEM((1, lw), F32), pltpu.VMEM((1, lw), F32)],
        compiler_params=_params(("parallel", "arbitrary")),
    )(u, bbd_re, bbd_im, cbd_re, cbd_im, pw_re, pw_im, dskip)


def _s5_bwd(dz, y, u, xs_re, xs_im, bbd_re, bbd_im, cbd_re, cbd_im, pw_re, pw_im, pf_re, pf_im, dskip, name):
    s, d = u.shape
    nt, cw, lw = bbd_re.shape
    t = _tile(s, SCAN_T, SUB)
    nc = s // t
    ng = t // SUB

    def kern(dz_ref, y_ref, u_ref, xr_ref, xi_ref, bre_ref, bim_ref, cre_ref, cim_ref,
             pre_ref, pim_ref, fre_ref, fim_ref, d_ref,
             du_ref, dd_ref, dcr_ref, dci_ref, dbr_ref, dbi_ref, dar_ref, dai_ref,
             car_ref, cai_ref, gr_ref, gi_ref):
        c = pl.program_id(1)

        @pl.when(c == 0)
        def _():
            car_ref[...] = jnp.zeros_like(car_ref)
            cai_ref[...] = jnp.zeros_like(cai_ref)

        u_ = u_ref[...]
        ub = u_.astype(BF16)
        dy = dz_ref[...] * _gelu_grad(y_ref[...])
        dyb = dy.astype(BF16)
        vr, vi = _local_scan(_dot_nt(dyb, cre_ref[...]).reshape(ng, SUB, lw),
                             (-_dot_nt(dyb, cim_ref[...])).reshape(ng, SUB, lw), pre_ref, pim_ref, True)
        later_r = car_ref[...]
        later_i = cai_ref[...]
        cr, ci = later_r, later_i
        fr = fre_ref[...]
        fi = fim_ref[...]
        for gidx in reversed(range(ng)):
            rows = slice(gidx * SUB, (gidx + 1) * SUB)
            ar = vr[gidx] + fr * cr + fi * ci
            ai = vi[gidx] + fr * ci - fi * cr
            gr_ref[rows, :] = ar
            gi_ref[rows, :] = ai
            cr, ci = ar[0:1, :], ai[0:1, :]
        car_ref[...] = cr
        cai_ref[...] = ci
        gr = gr_ref[...]
        gi = gi_ref[...]
        rows = lax.broadcasted_iota(jnp.int32, (t, lw), 0)
        gsr = jnp.where(rows < t - 1, pltpu.roll(gr, t - 1, 0), later_r)
        gsi = jnp.where(rows < t - 1, pltpu.roll(gi, t - 1, 0), later_i)
        xr = xr_ref[...]
        xi = xi_ref[...]
        dar = jnp.sum(gsr * xr + gsi * xi, axis=0, keepdims=True)
        dai = jnp.sum(gsi * xr - gsr * xi, axis=0, keepdims=True)
        grb = gr.astype(BF16)
        gib = gi.astype(BF16)
        dbr = _dot_tn(ub, grb)
        dbi = _dot_tn(ub, gib)
        dcr = _dot_tn(dyb, xr.astype(BF16))
        dci = _dot_tn(dyb, xi.astype(BF16))
        du_ref[...] = dy * d_ref[...] + _dot_nt(grb, bre_ref[...]) + _dot_nt(gib, bim_ref[...])
        ddv = jnp.sum(dy * u_, axis=0, keepdims=True)

        @pl.when(c == 0)
        def _():
            dd_ref[...] = ddv
            dcr_ref[...] = dcr
            dci_ref[...] = dci
            dbr_ref[...] = dbr
            dbi_ref[...] = dbi
            dar_ref[...] = dar
            dai_ref[...] = dai

        @pl.when(c > 0)
        def _():
            dd_ref[...] += ddv
            dcr_ref[...] += dcr
            dci_ref[...] += dci
            dbr_ref[...] += dbr
            dbi_ref[...] += dbi
            dar_ref[...] += dar
            dai_ref[...] += dai

    tok = pl.BlockSpec((t, cw), lambda j, c: (nc - 1 - c, j))
    st = pl.BlockSpec((t, lw), lambda j, c: (nc - 1 - c, j))
    wb = pl.BlockSpec((None, cw, lw), lambda j, c: (j, 0, 0))
    wc = pl.BlockSpec((None, lw, cw), lambda j, c: (j, 0, 0))
    pw = pl.BlockSpec((SUB, lw), lambda j, c: (0, j))
    vec_c = pl.BlockSpec((1, cw), lambda j, c: (0, j))
    vec_l = pl.BlockSpec((1, lw), lambda j, c: (0, j))
    return pl.pallas_call(
        kern, name=name, grid=(nt, nc),
        in_specs=[tok, tok, tok, st, st, wb, wb, wc, wc, pw, pw, pw, pw, vec_c],
        out_specs=[tok, vec_c, wb, wb, wb, wb, vec_l, vec_l],
        out_shape=[jax.ShapeDtypeStruct((s, d), F32), jax.ShapeDtypeStruct((1, d), F32)]
        + [jax.ShapeDtypeStruct((nt, cw, lw), F32)] * 4
        + [jax.ShapeDtypeStruct((1, nt * lw), F32)] * 2,
        scratch_shapes=[pltpu.VMEM((1, lw), F32), pltpu.VMEM((1, lw), F32),
                        pltpu.VMEM((t, lw), F32), pltpu.VMEM((t, lw), F32)],
        compiler_params=_params(("parallel", "arbitrary")),
    )(dz, y, u, xs_re, xs_im, bbd_re, bbd_im, cbd_re, cbd_im, pw_re, pw_im, pf_re, pf_im, dskip)


def _s5_discretize(lam_re, lam_im, log_dt, b_re, b_im):
    dt = jnp.exp(log_dt)[:, None]
    mag = jnp.exp(lam_re * dt)
    ang = lam_im * dt
    lb_re = mag * jnp.cos(ang)
    lb_im = mag * jnp.sin(ang)
    nr = lb_re - 1.0
    den = lam_re * lam_re + lam_im * lam_im
    f_re = (nr * lam_re + lb_im * lam_im) / den
    f_im = (lb_im * lam_re - nr * lam_im) / den
    bb_re = f_re[..., None] * b_re - f_im[..., None] * b_im
    bb_im = f_re[..., None] * b_im + f_im[..., None] * b_re
    return lb_re, lb_im, bb_re, bb_im


def _block_diag(w):
    nt, ng, a, b = w.shape
    eye = jnp.eye(ng, dtype=w.dtype)
    return (w[:, :, :, None, :] * eye[None, :, None, :, None]).reshape(nt, ng * a, ng * b)


def _block_diag_take(w, a, b):
    nt = w.shape[0]
    ng = GROUPS_PER_TILE
    w5 = w.reshape(nt, ng, a, ng, b)
    on_diagonal = jnp.eye(ng, dtype=bool)[None, :, None, :, None]
    return jnp.sum(jnp.where(on_diagonal, w5, 0.0), axis=3).reshape(nt * ng, a, b)


def _att_combine(outs, lses, name):
    s, d = outs[0].shape
    tm = _tile(s, ROWS, 16)
    ng = len(outs)

    def kern(*refs):
        for h in range(d // HEAD_DIM):
            cols = slice(h * HEAD_DIM, (h + 1) * HEAD_DIM)
            ls = [refs[ng + t][:, h:h + 1] for t in range(ng)]
            mx = functools.reduce(jnp.maximum, ls)
            es = [jnp.exp(l - mx) for l in ls]
            den = functools.reduce(lambda a, b: a + b, es)
            o = functools.reduce(lambda a, b: a + b, [es[t] / den * refs[t][:, cols] for t in range(ng)])
            refs[2 * ng][:, cols] = o
            refs[2 * ng + 1][:, cols] = o.astype(BF16)

    return pl.pallas_call(
        kern, name=name, grid=(s // tm,),
        in_specs=[_row_spec(tm, d)] * ng + [_row_spec(tm, d // HEAD_DIM)] * ng,
        out_specs=[_row_spec(tm, d)] * 2,
        out_shape=[jax.ShapeDtypeStruct((s, d), F32), jax.ShapeDtypeStruct((s, d), BF16)],
        compiler_params=_params(("parallel",)),
    )(*outs, *lses)


DIL_MAX = 16


def _slab(r):
    return 4 * (r % 4) + r // 4


def _to_slabs(xs, name):
    s, w = xs[0].shape
    n = s // DIL_MAX
    nx = len(xs)

    def kern(*refs):
        o_ref = refs[nx]
        for r in range(DIL_MAX):
            rows = [x_ref[pl.ds(r, n, stride=DIL_MAX), :] for x_ref in refs[:nx]]
            o_ref[_slab(r) * n:(_slab(r) + 1) * n, :] = functools.reduce(lambda a, b: a + b, rows)

    spec = pl.BlockSpec((s, 128), lambda i: (0, i))
    return pl.pallas_call(
        kern, name=name, grid=(w // 128,), in_specs=[spec] * nx, out_specs=spec,
        out_shape=jax.ShapeDtypeStruct((s, w), F32), compiler_params=_params(("parallel",)),
    )(*xs)


def _from_slabs(x, name):
    s, w = x.shape
    n = s // DIL_MAX

    def kern(x_ref, o_ref, ob_ref):
        for r in range(DIL_MAX):
            o_ref[pl.ds(r, n, stride=DIL_MAX), :] = x_ref[_slab(r) * n:(_slab(r) + 1) * n, :]
        ob_ref[...] = o_ref[...].astype(BF16)

    spec = pl.BlockSpec((s, 128), lambda i: (0, i))
    return pl.pallas_call(
        kern, name=name, grid=(w // 128,), in_specs=[spec], out_specs=[spec, spec],
        out_shape=[jax.ShapeDtypeStruct((s, w), F32), jax.ShapeDtypeStruct((s, w), BF16)],
        compiler_params=_params(("parallel",)),
    )(x)


def _norm_bf16(x, gains, name):
    s, d = x.shape
    tm = _tile(s, ROWS, 16)
    ng = len(gains)

    def kern(x_ref, *refs):
        x_ = x_ref[...]
        for t in range(ng):
            refs[ng + t][...] = _rms(x_, refs[t][...]).astype(BF16)

    return pl.pallas_call(
        kern, name=name, grid=(s // tm,),
        in_specs=[_row_spec(tm, d)] + [_vec_spec(d)] * ng, out_specs=[_row_spec(tm, d)] * ng,
        out_shape=[jax.ShapeDtypeStruct((s, d), BF16)] * ng,
        compiler_params=_params(("parallel",)),
    )(x, *gains)


def _att_geometry(dil, s):
    g = DIL_MAX // dil
    return s // DIL_MAX, g, max(ATT_BLK // g, 16)


def _att_mask(g, j, jb, rep):
    b = g * j

    def pos(i):
        sl, jj = i // j, i % j
        off = {1: 0, 4: sl, 16: sl // 4 + 4 * (sl % 4)}[g]
        return g * jj + off

    qi = lax.broadcasted_iota(jnp.int32, (rep * b, 2 * b), 0) % b
    ki = lax.broadcasted_iota(jnp.int32, (rep * b, 2 * b), 1)
    prev = ki < b
    dist = pos(qi) - pos(ki % b) + jnp.where(prev, b, 0)
    return (dist >= 0) & (dist <= ATT_BLK) & (jnp.logical_not(prev) | (jb > 0))


def _stack_heads(x, rep):
    return jnp.concatenate([x[:, h * HEAD_DIM:(h + 1) * HEAD_DIM] for h in range(rep)], axis=0)


def _att_fwd(q, kv, gi, dil, nh, name):
    s = q.shape[0]
    n, g, j = _att_geometry(dil, s)
    b = g * j
    ng = q.shape[1] // (nh * HEAD_DIM)
    rep = nh // N_KV_HEADS
    qw = rep * HEAD_DIM
    scale = HEAD_DIM ** -0.5
    q3 = q.reshape(DIL_MAX, n, q.shape[1])
    kv3 = kv.reshape(DIL_MAX, n, kv.shape[1])

    kw = N_KV_HEADS * HEAD_DIM

    def kern(q_ref, kc_ref, kp_ref, vc_ref, vp_ref, o_ref, l_ref, lw_ref):
        jb = pl.program_id(1)
        mask = _att_mask(g, j, jb, rep)
        for kh in range(N_KV_HEADS):
            hs = slice(kh * HEAD_DIM, (kh + 1) * HEAD_DIM)
            kc, kp, vc, vp = [r[:, :, hs].reshape(b, HEAD_DIM) for r in (kc_ref, kp_ref, vc_ref, vp_ref)]
            k2 = jnp.concatenate([kp, kc], axis=0)
            v2 = jnp.concatenate([vp, vc], axis=0)
            qs = _stack_heads(q_ref[:, :, kh * qw:(kh + 1) * qw].reshape(b, qw), rep)
            sc = jnp.where(mask, _dot_nt(qs, k2) * scale, NEG_INF)
            m = jnp.max(sc, axis=-1, keepdims=True)
            p = jnp.exp(sc - m)
            l = jnp.sum(p, axis=-1, keepdims=True)
            out = _dot((p / l).astype(BF16), v2)
            lse = m + jnp.log(l)
            wide = jnp.broadcast_to(lse, (rep * b, HEAD_DIM))
            for h in range(rep):
                cols = slice(kh * qw + h * HEAD_DIM, kh * qw + (h + 1) * HEAD_DIM)
                o_ref[:, :, cols] = out[h * b:(h + 1) * b].reshape(g, j, HEAD_DIM)
                lw_ref[:, :, cols] = wide[h * b:(h + 1) * b].reshape(g, j, HEAD_DIM)
                l_ref[:, :, kh * rep + h:kh * rep + h + 1] = lse[h * b:(h + 1) * b].reshape(g, j, 1)

    def kv_spec(col, prev):
        if prev:
            return pl.BlockSpec((g, j, kw), lambda c, jb: (c, jnp.maximum(jb - 1, 0), col))
        return pl.BlockSpec((g, j, kw), lambda c, jb: (c, jb, col))

    out_spec = pl.BlockSpec((g, j, nh * HEAD_DIM), lambda c, jb: (c, jb, 0))
    out, lse, lse_wide = pl.pallas_call(
        kern, name=name, grid=(dil, n // j),
        in_specs=[pl.BlockSpec((g, j, nh * HEAD_DIM), lambda c, jb: (c, jb, gi)),
                  kv_spec(0, False), kv_spec(0, True), kv_spec(1, False), kv_spec(1, True)],
        out_specs=[out_spec, pl.BlockSpec((g, j, nh), lambda c, jb: (c, jb, 0)), out_spec],
        out_shape=[jax.ShapeDtypeStruct((DIL_MAX, n, nh * HEAD_DIM), F32),
                   jax.ShapeDtypeStruct((DIL_MAX, n, nh), F32),
                   jax.ShapeDtypeStruct((DIL_MAX, n, nh * HEAD_DIM), F32)],
        compiler_params=_params(("parallel", "arbitrary")),
    )(q3, kv3, kv3, kv3, kv3)
    del ng
    return out.reshape(s, nh * HEAD_DIM), lse.reshape(s, nh), lse_wide.reshape(s, nh * HEAD_DIM)


def _att_bwd(q, kv, do, o, lses, dq_all, gi, dil, nh, name):
    s = q.shape[0]
    n, g, j = _att_geometry(dil, s)
    b = g * j
    nb = n // j
    ng = len(lses)
    rep = nh // N_KV_HEADS
    qw = rep * HEAD_DIM
    dm = nh * HEAD_DIM
    scale = HEAD_DIM ** -0.5
    q3 = q.reshape(DIL_MAX, n, q.shape[1])
    kv3 = kv.reshape(DIL_MAX, n, kv.shape[1])
    wide = [a.reshape(DIL_MAX, n, dm) for a in (do, o, *lses)]
    kw = N_KV_HEADS * HEAD_DIM

    def kern(q_ref, kc_ref, kp_ref, vc_ref, vp_ref, do_ref, o_ref, *refs):
        l_refs = refs[:ng]
        dq_ref, dk_ref, dv_ref, ck_ref, cv_ref = refs[ng + 1:]
        jb = pl.program_id(1)

        @pl.when(jb == 0)
        def _():
            ck_ref[...] = jnp.zeros_like(ck_ref)
            cv_ref[...] = jnp.zeros_like(cv_ref)

        @pl.when(jb < nb)
        def _():
            mask = _att_mask(g, j, jb, rep)
            for kh in range(N_KV_HEADS):
                hs = slice(kh * HEAD_DIM, (kh + 1) * HEAD_DIM)
                ws = slice(kh * qw, (kh + 1) * qw)

                def stacked(ref):
                    return _stack_heads(ref[:, :, ws].reshape(b, qw), rep)

                kc, kp, vc, vp = [r[:, :, hs].reshape(b, HEAD_DIM) for r in (kc_ref, kp_ref, vc_ref, vp_ref)]
                k2 = jnp.concatenate([kp, kc], axis=0)
                v2 = jnp.concatenate([vp, vc], axis=0)
                qs = stacked(q_ref)
                ls = [stacked(r)[:, 0:1] for r in l_refs]
                mx = functools.reduce(jnp.maximum, ls)
                den = functools.reduce(lambda a, c: a + c, [jnp.exp(l - mx) for l in ls])
                lse_g = ls[gi]
                w = jnp.exp(lse_g - mx) / den
                do_ = stacked(do_ref)
                ct = w * jnp.sum(do_ * stacked(o_ref), axis=-1, keepdims=True)
                dob = (w * do_).astype(BF16)
                p = jnp.exp(jnp.where(mask, _dot_nt(qs, k2) * scale, NEG_INF) - lse_g)
                ds = (p * (_dot_nt(dob, v2) - ct) * scale).astype(BF16)
                dq = (_dot(ds, k2)).astype(BF16)
                for h in range(rep):
                    cols = slice(kh * qw + h * HEAD_DIM, kh * qw + (h + 1) * HEAD_DIM)
                    dq_ref[:, :, cols] = dq[h * b:(h + 1) * b].reshape(g, j, HEAD_DIM)
                dk2 = _dot_tn(ds, qs)
                dv2 = _dot_tn(p.astype(BF16), dob)
                dk_ref[:, :, hs] = (ck_ref[:, hs] + dk2[:b]).reshape(g, j, HEAD_DIM)
                dv_ref[:, :, hs] = (cv_ref[:, hs] + dv2[:b]).reshape(g, j, HEAD_DIM)
                ck_ref[:, hs] = dk2[b:]
                cv_ref[:, hs] = dv2[b:]

        @pl.when(jb == nb)
        def _():
            dk_ref[...] = ck_ref[...].reshape(g, j, kw)
            dv_ref[...] = cv_ref[...].reshape(g, j, kw)

    def jq(jb):
        return jnp.minimum(jb, nb - 1)

    def kv_spec(col, prev):
        if prev:
            return pl.BlockSpec((g, j, kw), lambda c, jb: (c, jnp.maximum(jq(jb) - 1, 0), col))
        return pl.BlockSpec((g, j, kw), lambda c, jb: (c, jq(jb), col))

    wide_spec = pl.BlockSpec((g, j, dm), lambda c, jb: (c, jq(jb), 0))
    dkv_spec = pl.BlockSpec((g, j, kw), lambda c, jb: (c, jnp.maximum(jb - 1, 0), 0))
    dq, dk, dv = pl.pallas_call(
        kern, name=name, grid=(dil, nb + 1),
        in_specs=[pl.BlockSpec((g, j, dm), lambda c, jb: (c, jq(jb), gi)),
                  kv_spec(0, False), kv_spec(0, True), kv_spec(1, False), kv_spec(1, True)]
        + [wide_spec] * (2 + ng) + [_ANY],
        out_specs=[pl.BlockSpec((g, j, dm), lambda c, jb: (c, jq(jb), gi)), dkv_spec, dkv_spec],
        out_shape=[jax.ShapeDtypeStruct((DIL_MAX, n, dq_all.shape[1]), BF16),
                   jax.ShapeDtypeStruct((DIL_MAX, n, kw), F32),
                   jax.ShapeDtypeStruct((DIL_MAX, n, kw), F32)],
        scratch_shapes=[pltpu.VMEM((b, kw), F32), pltpu.VMEM((b, kw), F32)],
        input_output_aliases={7 + ng: 0},
        compiler_params=_params(("parallel", "arbitrary")),
    )(q3, kv3, kv3, kv3, kv3, *wide, dq_all.reshape(DIL_MAX, n, dq_all.shape[1]))
    return dq.reshape(s, -1), dk.reshape(s, -1), dv.reshape(s, -1)


def _dkv_sum(dks, dvs, name):
    s, w = dks[0].shape
    tm = _tile(s, ROWS, 16)
    ng = len(dks)

    def kern(*refs):
        o_ref = refs[2 * ng]
        o_ref[:, :w] = functools.reduce(lambda a, b: a + b, [refs[t][...] for t in range(ng)]).astype(BF16)
        o_ref[:, w:] = functools.reduce(lambda a, b: a + b, [refs[ng + t][...] for t in range(ng)]).astype(BF16)

    return pl.pallas_call(
        kern, name=name, grid=(s // tm,),
        in_specs=[_row_spec(tm, w)] * (2 * ng),
        out_specs=_row_spec(tm, 2 * w),
        out_shape=jax.ShapeDtypeStruct((s, 2 * w), BF16),
        compiler_params=_params(("parallel",)),
    )(*dks, *dvs)


def _ffn_fwd(hf, weight, layer, tag):
    a, gu = _ffn_in_act(hf, weight(f"w_in{layer}", hf), f"{tag}_in")
    o = _mm_nn(a, weight(f"w_out{layer}", a), F32, f"{tag}_out", tn=1024, tk=5632)
    return gu, a, o


def _ffn_bwd(dresb, hf, gu, a, weight, emit, layer, tag):
    w_in, w_out = weight(f"w_in{layer}", None), weight(f"w_out{layer}", None)
    zero = emit(f"w_out{layer}", _mm_tn(a, dresb, 1, f"{tag}_out_dw", tn=1024))
    dgu = _ffn_out_dx_act(dresb, w_out, gu, f"{tag}_out_dx")
    zero = zero + emit(f"w_in{layer}",
                       _mm_tn(hf, dgu, w_in.shape[0], f"{tag}_in_dw", tn=FFN_TILE, paired=True))
    return _mm_nt(dgu, w_in, f"{tag}_in_dx", tr=FFN_TILE, tkc=512, paired=True), zero


def _local_step(x, target, small, weight, emit, stage):
    s, d = x.shape
    nh = d // HEAD_DIM
    n_groups = d // S5_GROUP_CH
    nt = n_groups // GROUPS_PER_TILE
    p_, c_ = S5_STATE, S5_GROUP_CH

    disc_in = (small["lam_re"], small["lam_im"], small["log_dt"], small["b_re"], small["b_im"])
    (lb_re, lb_im, bb_re, bb_im), disc_vjp = jax.vjp(_s5_discretize, *disc_in)
    del lb_re, lb_im
    dt = jnp.exp(small["log_dt"])[:, None]
    def pole_powers(exponents):
        k = exponents[:, None, None]
        mag = jnp.exp(k * (small["lam_re"] * dt)[None])
        ang = k * (small["lam_im"] * dt)[None]
        return ((mag * jnp.cos(ang)).reshape(SUB, n_groups * p_),
                (mag * jnp.sin(ang)).reshape(SUB, n_groups * p_))

    pw_re, pw_im = pole_powers(jnp.arange(1, SUB + 1, dtype=F32))
    pf_re, pf_im = pole_powers(jnp.arange(SUB, 0, -1, dtype=F32))
    bbd_re = _block_diag(bb_re.transpose(0, 2, 1).reshape(nt, GROUPS_PER_TILE, c_, p_)).astype(BF16)
    bbd_im = _block_diag(bb_im.transpose(0, 2, 1).reshape(nt, GROUPS_PER_TILE, c_, p_)).astype(BF16)
    cbd_re = _block_diag(small["c_re"].transpose(0, 2, 1).reshape(nt, GROUPS_PER_TILE, p_, c_)).astype(BF16)
    cbd_im = _block_diag(small["c_im"].transpose(0, 2, 1).reshape(nt, GROUPS_PER_TILE, p_, c_)).astype(BF16)

    h0 = _norm_f32(x, small["a_norm"], "s5_norm")
    xs_re, xs_im, y0, z = _s5_fwd(h0, bbd_re, bbd_im, cbd_re, cbd_im, pw_re, pw_im, small["s5_d"], "s5_fwd")
    zero = stage("glu", z)
    vg = _mm_nn(z, weight("w_glu", z), F32, "glu_mm", tn=1024)
    x1, hf0 = _glu_res_norm(vg, x, small["ffn_norm0"] + zero, "glu_res_norm")
    gu0, a0, o0 = _ffn_fwd(hf0, weight, 0, "ffn0")
    zero = stage("attention", a0)
    x2 = _to_slabs([o0, x1], "to_slabs")
    kvn, h1 = _norm_bf16(x2, [small["kv_norm"] + zero, small["b_norm"]], "att_norms")
    kv = _mm_nn(kvn, weight("w_kv", kvn), BF16, "kv_mm", tn=1024)
    q = _mm_nn(h1, weight("w_q", kv), BF16, "q_mm", tn=1536)
    outs, lses_narrow, lses = [], [], []
    for gi, (window, dil) in enumerate(PATTERNS):
        assert window // dil == ATT_BLK
        og, lg, lg_wide = _att_fwd(q, kv, gi, dil, nh, f"att_fwd{gi}")
        outs.append(og)
        lses_narrow.append(lg)
        lses.append(lg_wide)
    oatt, oattb = _att_combine(outs, lses_narrow, "att_combine")
    ao = _mm_nn(oattb, weight("w_o", oattb), F32, "o_mm", tn=1024)
    x3, hf1 = _res_norm(ao, x2, [small["ffn_norm1"] + stage("ffn1", oattb)], "att_res_norm")
    gu1, a1, o1 = _ffn_fwd(hf1, weight, 1, "ffn1")
    dres, dresb, loss_rows, d_final = _loss_head(
        o1, x3, small["final_norm"], _to_slabs([target], "target_to_slabs"), "loss_head")

    dhf1, zero = _ffn_bwd(dresb, hf1, gu1, a1, weight, emit, 1, "ffn1")
    dres, dresb, d_ffn_norm1 = _norm_bwd(x3, [small["ffn_norm1"] + zero], [dhf1], dres, "ffn1_norm_bwd")
    zero = emit("w_o", _mm_tn(oattb, dresb, 1, "o_dw", tn=1024))
    doatt = _mm_nt(dresb, weight("w_o", None), "o_dx", tr=2048)
    dks, dvs = [], []
    dq = lax.empty(q.shape, BF16)
    for gi, (window, dil) in enumerate(PATTERNS):
        dq, dk_g, dv_g = _att_bwd(q, kv, doatt, oatt, lses, dq, gi, dil, nh, f"att_bwd{gi}")
        dks.append(dk_g)
        dvs.append(dv_g)
    dkv = _dkv_sum(dks, dvs, "dkv_sum")
    w_q = weight("w_q", None)
    zero = zero + emit("w_q", _mm_tn(h1, dq, w_q.shape[0], "q_dw", tn=1536))
    zero = zero + emit("w_kv", _mm_tn(kvn, dkv, 1, "kv_dw", tn=1024))
    dh1 = _mm_nt(dq, w_q, "q_dx", tr=1536)
    dkvn = _mm_nt(dkv, weight("w_kv", None), "kv_dx", tr=1024)
    dres, dresb, d_b_norm, d_kv_norm = _norm_bwd(
        x2, [small["b_norm"] + zero, small["kv_norm"]], [dh1, dkvn], dres, "att_norm_bwd")
    dres, dresb = _from_slabs(dres, "from_slabs")
    dhf0, zero = _ffn_bwd(dresb, hf0, gu0, a0, weight, emit, 0, "ffn0")
    dres, dresb, d_ffn_norm0 = _norm_bwd(x1, [small["ffn_norm0"] + zero], [dhf0], dres, "ffn0_norm_bwd")
    del dresb
    dvg = _glu_bwd(dres, vg, "glu_bwd")
    w_glu = weight("w_glu", None)
    zero = emit("w_glu", _mm_tn(z, dvg, w_glu.shape[0], "glu_dw", tn=1024))
    dz = _mm_nt(dvg, w_glu, "glu_dx", tr=1024)
    dh0, d_s5_d, dcr, dci_neg, dbr, dbi, dar, dai = _s5_bwd(
        dz, y0, h0, xs_re, xs_im, bbd_re, bbd_im, cbd_re, cbd_im, pw_re, pw_im, pf_re, pf_im,
        small["s5_d"] + zero, "s5_bwd")
    grad_x, _, d_a_norm = _norm_bwd(x, [small["a_norm"]], [dh0], dres, "s5_norm_bwd")

    d_bb_re = _block_diag_take(dbr, c_, p_).transpose(0, 2, 1)
    d_bb_im = _block_diag_take(dbi, c_, p_).transpose(0, 2, 1)
    d_c_re = _block_diag_take(dcr, c_, p_)
    d_c_im = -_block_diag_take(dci_neg, c_, p_)
    d_lam_re, d_lam_im, d_log_dt, d_b_re, d_b_im = disc_vjp(
        (dar.reshape(n_groups, p_), dai.reshape(n_groups, p_), d_bb_re, d_bb_im))

    d_small = dict(lam_re=d_lam_re, lam_im=d_lam_im, log_dt=d_log_dt, b_re=d_b_re, b_im=d_b_im,
                   c_re=d_c_re, c_im=d_c_im, s5_d=d_s5_d, a_norm=d_a_norm, ffn_norm0=d_ffn_norm0,
                   ffn_norm1=d_ffn_norm1, b_norm=d_b_norm, kv_norm=d_kv_norm, final_norm=d_final)
    return loss_rows, grad_x, d_small


def _place():
    x, y, c = lax.axis_index("x"), lax.axis_index("y"), lax.axis_index("c")
    return x, y, c, [(1 - x, y), (x, 1 - y), (1 - x, 1 - y)]


_ANY = pl.BlockSpec(memory_space=pl.ANY)


_HBM = pl.BlockSpec(memory_space=pltpu.HBM)
_SEM = pl.BlockSpec(memory_space=pltpu.SEMAPHORE)
_EFFECT = pltpu.SideEffectType.DATAFLOW_SIDE_EFFECTING


def _in_hbm(a):
    return pltpu.with_memory_space_constraint(a, pltpu.HBM)


def _cast_place(shards, layer, chip, name):
    _, r, c = shards.shape
    tm = _tile(r, ROWS, 16)

    def kern(chip_ref, x_ref, o_ref):
        del chip_ref
        o_ref[...] = x_ref[...].astype(BF16)

    return pl.pallas_call(
        kern, name=name,
        grid_spec=pltpu.PrefetchScalarGridSpec(
            num_scalar_prefetch=1, grid=(r // tm,),
            in_specs=[pl.BlockSpec((None, tm, c), lambda i, ch: (layer, i, 0))],
            out_specs=pl.BlockSpec((None, tm, c), lambda i, ch: (ch[0], i, 0))),
        out_shape=jax.ShapeDtypeStruct((N_CHIPS, r, c), BF16),
        compiler_params=_params(("parallel",)),
    )(chip, shards)


def _my_part(land, block, c, halves):
    if not halves:
        return land.at[block]
    half = land.shape[1] // 2
    return land.at[block, pl.ds(c * half, half)]


def _gather_start(lands, name, halves, after):
    n = len(lands)

    def body(*refs):
        land = refs[:n]
        send, recv = refs[n + 1:2 * n + 1], refs[2 * n + 1:3 * n + 1]
        token = refs[4 * n + 1]
        x, y, c, peers = _place()
        me = 2 * x + y
        for a in range(n):
            for k, (px, py) in enumerate(peers):
                part = _my_part(land[a], me, c, halves)
                pltpu.make_async_remote_copy(
                    src_ref=part, dst_ref=part, send_sem=send[a].at[k], recv_sem=recv[a].at[k],
                    device_id=(px, py, c), device_id_type=MESH).start()
        token[...] = jnp.zeros_like(token)

    outs = pl.pallas_call(
        body, name=name,
        out_shape=[pltpu.SemaphoreType.DMA((3,))] * (2 * n) + [pltpu.HBM(a.shape, a.dtype) for a in lands]
        + [jax.ShapeDtypeStruct((8, 128), F32)],
        in_specs=[_HBM] * n + [_ANY],
        out_specs=[_SEM] * (2 * n) + [_HBM] * n + [pl.BlockSpec(memory_space=pltpu.VMEM)],
        input_output_aliases={i: 2 * n + i for i in range(n)},
        compiler_params=pltpu.CompilerParams(has_side_effects=_EFFECT),
    )(*[_in_hbm(a) for a in lands], after)
    return outs[:n], outs[n:2 * n], outs[2 * n:3 * n], outs[3 * n]


def _gather_wait(lands, sends, recvs, after, name, halves):
    n = len(lands)

    def body(*refs):
        land, send, recv = refs[:n], refs[n:2 * n], refs[2 * n:3 * n]
        x, y, c, peers = _place()
        me = 2 * x + y
        for a in range(n):
            for k, (px, py) in enumerate(peers):
                cp = pltpu.make_async_remote_copy(
                    src_ref=_my_part(land[a], me, c, halves), dst_ref=_my_part(land[a], 2 * px + py, c, halves),
                    send_sem=send[a].at[k], recv_sem=recv[a].at[k], device_id=(px, py, c), device_id_type=MESH)
                cp.wait_send()
                cp.wait_recv()

    return pl.pallas_call(
        body, name=name,
        out_shape=[pltpu.HBM(a.shape, a.dtype) for a in lands],
        in_specs=[_HBM] * n + [_SEM] * (2 * n) + [_ANY], out_specs=[_HBM] * n,
        input_output_aliases={i: i for i in range(n)},
        compiler_params=pltpu.CompilerParams(has_side_effects=_EFFECT),
    )(*lands, *sends, *recvs, after)


def _forward_start(lands, name):
    n = len(lands)

    def body(*refs):
        land = refs[:n]
        send, recv = refs[n:2 * n], refs[2 * n:3 * n]
        token = refs[4 * n]
        x, y, c, peers = _place()
        for a in range(n):
            for k, (px, py) in enumerate(peers):
                part = _my_part(land[a], 2 * px + py, c, True)
                pltpu.make_async_remote_copy(
                    src_ref=part, dst_ref=part, send_sem=send[a].at[k], recv_sem=recv[a].at[k],
                    device_id=(x, y, 1 - c), device_id_type=MESH).start()
        token[...] = jnp.zeros_like(token)

    outs = pl.pallas_call(
        body, name=name,
        out_shape=[pltpu.SemaphoreType.DMA((3,))] * (2 * n) + [pltpu.HBM(a.shape, a.dtype) for a in lands]
        + [jax.ShapeDtypeStruct((8, 128), F32)],
        in_specs=[_HBM] * n,
        out_specs=[_SEM] * (2 * n) + [_HBM] * n + [pl.BlockSpec(memory_space=pltpu.VMEM)],
        input_output_aliases={i: 2 * n + i for i in range(n)},
        compiler_params=pltpu.CompilerParams(has_side_effects=_EFFECT),
    )(*[_in_hbm(a) for a in lands])
    return outs[:n], outs[n:2 * n], outs[2 * n:3 * n], outs[3 * n]


def _forward_wait(land, send, recv, after, name):
    def body(land_ref, send_sem, recv_sem, after_ref, land_out):
        del after_ref, land_out
        x, y, c, peers = _place()
        for k, (px, py) in enumerate(peers):
            cp = pltpu.make_async_remote_copy(
                src_ref=_my_part(land_ref, 2 * px + py, c, True), dst_ref=_my_part(land_ref, 2 * px + py, 1 - c, True),
                send_sem=send_sem.at[k], recv_sem=recv_sem.at[k], device_id=(x, y, 1 - c), device_id_type=MESH)
            cp.wait_send()
            cp.wait_recv()

    return pl.pallas_call(
        body, name=name,
        out_shape=pltpu.HBM(land.shape, land.dtype),
        in_specs=[_HBM, _SEM, _SEM, _ANY], out_specs=_HBM,
        input_output_aliases={0: 0},
        compiler_params=pltpu.CompilerParams(has_side_effects=_EFFECT),
    )(land, send, recv, after)


def _scatter_start(g, name):
    def body(g_ref, land_ref, send, recv, g_out, land_out, token):
        del g_out, land_out
        x, y, c, peers = _place()
        for k, (px, py) in enumerate(peers):
            pltpu.make_async_remote_copy(
                src_ref=g_ref.at[2 * px + py], dst_ref=land_ref.at[k], send_sem=send.at[k], recv_sem=recv.at[k],
                device_id=(px, py, c), device_id_type=MESH).start()
        token[...] = jnp.zeros_like(token)

    land = lax.empty((3,) + g.shape[1:], g.dtype)
    return pl.pallas_call(
        body, name=name,
        out_shape=(pltpu.SemaphoreType.DMA((3,)), pltpu.SemaphoreType.DMA((3,)),
                   pltpu.HBM(g.shape, g.dtype), pltpu.HBM(land.shape, land.dtype),
                   jax.ShapeDtypeStruct((8, 128), F32)),
        in_specs=(_HBM, _HBM), out_specs=(_SEM, _SEM, _HBM, _HBM, pl.BlockSpec(memory_space=pltpu.VMEM)),
        input_output_aliases={0: 2, 1: 3},
        compiler_params=pltpu.CompilerParams(has_side_effects=_EFFECT),
    )(_in_hbm(g), _in_hbm(land))


def _scatter_wait(started, after):
    n = len(started)

    def body(*refs):
        gs, lands = refs[:n], refs[n:2 * n]
        sends, recvs = refs[2 * n:3 * n], refs[3 * n:4 * n]
        x, y, c, peers = _place()
        for a in range(n):
            for k, (px, py) in enumerate(peers):
                cp = pltpu.make_async_remote_copy(
                    src_ref=gs[a].at[2 * px + py], dst_ref=lands[a].at[k], send_sem=sends[a].at[k],
                    recv_sem=recvs[a].at[k], device_id=(px, py, c), device_id_type=MESH)
                cp.wait_send()
                cp.wait_recv()

    gs = [s[2] for s in started]
    lands = [s[3] for s in started]
    outs = pl.pallas_call(
        body, name="scatter_wait",
        out_shape=[pltpu.HBM(a.shape, a.dtype) for a in gs + lands],
        in_specs=[_HBM] * (2 * n) + [_SEM] * (2 * n) + [_ANY], out_specs=[_HBM] * (2 * n),
        input_output_aliases={i: i for i in range(2 * n)},
        compiler_params=pltpu.CompilerParams(has_side_effects=_EFFECT),
    )(*gs, *lands, *[s[0] for s in started], *[s[1] for s in started], after)
    return outs[:n], outs[n:]


def _sibling():
    return lax.axis_index("x"), lax.axis_index("y"), 1 - lax.axis_index("c")


def _swap_start(parts):
    n = len(parts)

    def body(*refs):
        ins, land = refs[:n], refs[n:2 * n]
        send, recv = refs[2 * n], refs[2 * n + 1]
        token = refs[4 * n + 2]
        for a in range(n):
            pltpu.make_async_remote_copy(
                src_ref=ins[a], dst_ref=land[a], send_sem=send.at[a], recv_sem=recv.at[a],
                device_id=_sibling(), device_id_type=MESH).start()
        token[...] = jnp.zeros_like(token)

    lands = [lax.empty(a.shape, a.dtype) for a in parts]
    outs = pl.pallas_call(
        body, name="swap_start",
        out_shape=[pltpu.SemaphoreType.DMA((n,))] * 2 + [pltpu.HBM(a.shape, a.dtype) for a in parts + lands]
        + [jax.ShapeDtypeStruct((8, 128), F32)],
        in_specs=[_HBM] * (2 * n),
        out_specs=[_SEM] * 2 + [_HBM] * (2 * n) + [pl.BlockSpec(memory_space=pltpu.VMEM)],
        input_output_aliases={i: 2 + i for i in range(2 * n)},
        compiler_params=pltpu.CompilerParams(has_side_effects=_EFFECT),
    )(*[_in_hbm(a) for a in parts + lands])
    return outs[0], outs[1], outs[2:2 + n], outs[2 + n:2 + 2 * n], outs[2 + 2 * n]


def _swap_wait(parts, lands, send, recv, after):
    n = len(parts)

    def body(*refs):
        ins, land = refs[:n], refs[n:2 * n]
        send_sem, recv_sem = refs[2 * n], refs[2 * n + 1]
        for a in range(n):
            cp = pltpu.make_async_remote_copy(
                src_ref=ins[a], dst_ref=land[a], send_sem=send_sem.at[a], recv_sem=recv_sem.at[a],
                device_id=_sibling(), device_id_type=MESH)
            cp.wait_send()
            cp.wait_recv()

    outs = pl.pallas_call(
        body, name="swap_wait",
        out_shape=[pltpu.HBM(a.shape, a.dtype) for a in list(parts) + list(lands)],
        in_specs=[_HBM] * (2 * n) + [_SEM] * 2 + [_ANY], out_specs=[_HBM] * (2 * n),
        input_output_aliases={i: i for i in range(2 * n)},
        compiler_params=pltpu.CompilerParams(has_side_effects=_EFFECT),
    )(*parts, *lands, send, recv, after)
    return outs[:n], outs[n:]


def _all_reduce_small(v):
    nd, r, w = v.shape
    assert nd == N_DEV

    def body(v_ref, out_ref, land_ref, red_ref, send1, recv1, send2, recv2):
        x, y, c = lax.axis_index("x"), lax.axis_index("y"), lax.axis_index("c")
        me = 4 * x + 2 * y + c
        peers = []
        for k in range(1, N_DEV):
            kx, ky, kc = (k >> 2) & 1, (k >> 1) & 1, k & 1
            peers.append((1 - x if kx else x, 1 - y if ky else y, 1 - c if kc else c))
        first = []
        for k, (px, py, pc) in enumerate(peers):
            cp = pltpu.make_async_remote_copy(
                src_ref=v_ref.at[4 * px + 2 * py + pc], dst_ref=land_ref.at[me], send_sem=send1.at[k],
                recv_sem=recv1.at[k], device_id=(px, py, pc), device_id_type=MESH)
            cp.start()
            first.append(cp)
        land_ref[me] = v_ref[me]
        for cp in first:
            cp.wait()
        acc = land_ref[0]
        for j in range(1, N_DEV):
            acc = acc + land_ref[j]
        red_ref[...] = acc
        second = []
        for k, (px, py, pc) in enumerate(peers):
            cp = pltpu.make_async_remote_copy(
                src_ref=red_ref, dst_ref=out_ref.at[me], send_sem=send2.at[k],
                recv_sem=recv2.at[k], device_id=(px, py, pc), device_id_type=MESH)
            cp.start()
            second.append(cp)
        out_ref[me] = acc
        for cp in second:
            cp.wait()

    vmem = pl.BlockSpec(memory_space=pltpu.VMEM)
    return pl.pallas_call(
        body, name="all_reduce_small",
        in_specs=[vmem], out_specs=vmem,
        out_shape=jax.ShapeDtypeStruct((nd, r, w), F32),
        scratch_shapes=[pltpu.VMEM((nd, r, w), F32), pltpu.VMEM((r, w), F32)]
        + [pltpu.SemaphoreType.DMA((N_DEV - 1,))] * 4,
        compiler_params=pltpu.CompilerParams(vmem_limit_bytes=VMEM_LIMIT),
    )(v)


def _adam_math(w, g, m, v):
    m = ADAM_B1 * m + (1.0 - ADAM_B1) * g
    v = ADAM_B2 * v + (1.0 - ADAM_B2) * (g * g)
    m_hat = m / (1.0 - ADAM_B1 ** ADAM_STEP)
    v_hat = v / (1.0 - ADAM_B2 ** ADAM_STEP)
    delta = -ADAM_LR * (m_hat / (jnp.sqrt(v_hat) + ADAM_EPS) + ADAM_WD * w)
    return delta, m, v


def _sum_blocks(own, got, chip, name):
    _, r, c = own.shape
    tm = _tile(r, ROWS, 16)

    def kern(chip_ref, own_ref, got_ref, o_ref):
        del chip_ref
        acc = own_ref[...].astype(F32)
        for k in range(3):
            acc = acc + got_ref[k].astype(F32)
        o_ref[...] = acc

    return pl.pallas_call(
        kern, name=name,
        grid_spec=pltpu.PrefetchScalarGridSpec(
            num_scalar_prefetch=1, grid=(r // tm,),
            in_specs=[pl.BlockSpec((None, tm, c), lambda i, ch: (ch[0], i, 0)),
                      pl.BlockSpec((3, tm, c), lambda i, ch: (0, i, 0))],
            out_specs=pl.BlockSpec((tm, c), lambda i, ch: (i, 0))),
        out_shape=jax.ShapeDtypeStruct((r, c), F32),
        compiler_params=_params(("parallel",)),
    )(chip, own, got)


def _adamw(parts, w, m, v, name):
    nl, r, c = w.shape
    assert len(parts) == nl
    tm = _tile(r, 128, 8)

    def kern(*refs):
        p_refs = refs[:2 * nl]
        w_ref, m_ref, v_ref, g_ref, d_ref, mo_ref, vo_ref = refs[2 * nl:]
        layer = pl.program_id(0)
        g = p_refs[0][...] + p_refs[1][...]
        for ll in range(1, nl):
            g = jnp.where(layer == ll, p_refs[2 * ll][...] + p_refs[2 * ll + 1][...], g)
        g_ref[...] = g
        d_ref[...], mo_ref[...], vo_ref[...] = _adam_math(w_ref[...], g, m_ref[...], v_ref[...])

    def part_spec(ll):
        return pl.BlockSpec((tm, c), lambda l, i: (jnp.where(l == ll, i, 0), 0))

    spec = pl.BlockSpec((None, tm, c), lambda l, i: (l, i, 0))
    return pl.pallas_call(
        kern, name=name, grid=(nl, r // tm),
        in_specs=[part_spec(ll) for ll in range(nl) for _ in range(2)] + [spec] * 3, out_specs=[spec] * 4,
        out_shape=[jax.ShapeDtypeStruct((nl, r, c), F32)] * 4,
        compiler_params=_params(("arbitrary", "parallel")),
    )(*[p for pair in parts for p in pair], w, m, v)


def _adamw_small(g, w, m, v, name):
    def kern(g_ref, w_ref, m_ref, v_ref, d_ref, mo_ref, vo_ref):
        d_ref[...], mo_ref[...], vo_ref[...] = _adam_math(w_ref[...], g_ref[...], m_ref[...], v_ref[...])

    return pl.pallas_call(
        kern, name=name,
        out_shape=[jax.ShapeDtypeStruct(g.shape, F32)] * 3,
        compiler_params=pltpu.CompilerParams(vmem_limit_bytes=VMEM_LIMIT),
    )(g, w, m, v)


def _pack(arrays, rows):
    flat = jnp.concatenate([a.reshape(-1).astype(F32) for a in arrays])
    return jnp.pad(flat, (0, rows * 128 - flat.shape[0])).reshape(rows, 128)


def _unpack(packed, shapes):
    flat = packed.reshape(-1)
    out, off = [], 0
    for shp in shapes:
        size = math.prod(shp)
        out.append(flat[off:off + size].reshape(shp))
        off += size
    return out


_REPLICATED = ["s5_lam_re", "s5_lam_im", "s5_log_dt", "s5_b_re", "s5_b_im", "s5_c_re", "s5_c_im",
               "ffn_norm", "b_norm_mix", "kv_norm", "final_norm"]
_CHIP_VECTORS = ["s5_d", "a_norm_mix"]
_BIG = ["s5_w_glu", "ffn_w_in", "ffn_w_out", "attn_w_q", "attn_w_o", "w_kv"]
_WEIGHT_ORDER = ["s5_lam_re", "s5_lam_im", "s5_log_dt", "s5_b_re", "s5_b_im", "s5_c_re", "s5_c_im", "s5_d",
                 "s5_w_glu", "a_norm_mix", "ffn_norm", "ffn_w_in", "ffn_w_out", "b_norm_mix", "attn_w_q",
                 "attn_w_o", "kv_norm", "w_kv", "final_norm"]


def _step(x, loss_target, w, m, v):
    s, d = x.shape[1], x.shape[2]
    chip = 2 * lax.axis_index("x") + lax.axis_index("y")

    col_sharded = dict(w_glu=("s5_w_glu", 0), w_in0=("ffn_w_in", 0), w_in1=("ffn_w_in", 1), w_q=("attn_w_q", 0))
    row_sharded = dict(w_out0=("ffn_w_out", 0), w_out1=("ffn_w_out", 1), w_o=("attn_w_o", 0), w_kv=("w_kv", 0))
    names = ["w_glu", "w_in0", "w_out0", "w_kv", "w_q", "w_o", "w_in1", "w_out1"]
    local = {**col_sharded, **row_sharded}

    def layers(a):
        return a.reshape((-1,) + a.shape[-2:])

    chip_arr = jnp.reshape(chip, (1,)).astype(jnp.int32)
    vector_lands = [lax.dynamic_update_slice(jnp.zeros((N_CHIPS,) + w[n].shape, F32), w[n][None], (chip, 0, 0))
                    for n in _CHIP_VECTORS]
    v_send, v_recv, v_land, v_token = _gather_start(vector_lands, "gather_start_vectors", False,
                                                    jnp.zeros((8, 128), F32))
    first = 2
    lands = [_cast_place(layers(w[local[n][0]]), local[n][1], chip_arr, f"cast_{n}") for n in names[:first]]
    send, recv, land_thru, first_token = _gather_start(lands, "gather_start_first", True, v_token)
    lands = [_cast_place(layers(w[local[n][0]]), local[n][1], chip_arr, f"cast_{n}") for n in names[first:]]
    *later, rest_token = _gather_start(lands, "gather_start_rest", True, first_token)
    send, recv, land_thru = [list(a) + list(b) for a, b in zip((send, recv, land_thru), later)]
    vectors = _gather_wait(v_land, v_send, v_recv, jnp.zeros((8, 128), F32), "gather_wait_vectors", False)
    s5_d_full = vectors[0].reshape(1, d)
    a_norm_full = vectors[1].reshape(1, d) + rest_token[0, 0]

    batches = dict(glu=["w_glu", "w_in0", "w_out0"], attention=["w_kv", "w_q", "w_o"], ffn1=["w_in1", "w_out1"])
    forwarded, arrived = {}, {}

    def stage(batch, after):
        idx = [names.index(n) for n in batches[batch]]
        got = _gather_wait([land_thru[i] for i in idx], [send[i] for i in idx], [recv[i] for i in idx], after,
                           f"gather_wait_{batch}", True)
        f_send, f_recv, f_land, token = _forward_start(got, f"forward_start_{batch}")
        forwarded.update(zip(batches[batch], zip(f_send, f_recv, f_land)))
        return token[0, 0]

    def weight(name, after):
        if name not in arrived:
            f_send, f_recv, f_land = forwarded[name]
            land = _forward_wait(f_land, f_send, f_recv, after, f"forward_wait_{name}")
            arrived[name] = land if name in col_sharded else land.reshape(1, -1, land.shape[-1])
        return arrived[name]

    started = {}

    def emit(name, dw):
        outs = _scatter_start(dw.reshape(N_CHIPS, -1, dw.shape[-1]), f"scatter_start_{name}")
        started[name] = outs[:4]
        return outs[4][0, 0]

    small = dict(lam_re=w["s5_lam_re"][0], lam_im=w["s5_lam_im"][0], log_dt=w["s5_log_dt"][0],
                 b_re=w["s5_b_re"][0], b_im=w["s5_b_im"][0], c_re=w["s5_c_re"][0], c_im=w["s5_c_im"][0],
                 s5_d=s5_d_full, a_norm=a_norm_full, ffn_norm0=w["ffn_norm"][0:1], ffn_norm1=w["ffn_norm"][1:2],
                 b_norm=w["b_norm_mix"], kv_norm=w["kv_norm"].reshape(1, d), final_norm=w["final_norm"].reshape(1, d))

    loss_rows, grad_x, d_small = _local_step(x[0], loss_target[0], small, weight, emit, stage)

    g4, got = _scatter_wait([started[n] for n in names], grad_x)
    partial = [_sum_blocks(o, r, chip_arr, f"sum_{n}") for n, o, r in zip(names, g4, got)]
    swap_send, swap_recv, partial, swap_land, swap_token = _swap_start(partial)
    result = {}

    rep_grads = [d_small["lam_re"], d_small["lam_im"], d_small["log_dt"], d_small["b_re"], d_small["b_im"],
                 d_small["c_re"], d_small["c_im"],
                 jnp.concatenate([d_small["ffn_norm0"], d_small["ffn_norm1"]], axis=0),
                 d_small["b_norm"], d_small["kv_norm"], d_small["final_norm"]]
    vec_grads = [d_small["s5_d"], d_small["a_norm"]]
    to_reduce = rep_grads + vec_grads + [jnp.sum(loss_rows).reshape(1) + swap_token[0, 0]]
    total = sum(math.prod(a.shape) for a in to_reduce)
    rows_per = -(-total // (N_DEV * 128 * 8)) * 8
    reduced = _all_reduce_small(_pack(to_reduce, N_DEV * rows_per).reshape(N_DEV, rows_per, 128))
    red = _unpack(reduced, [a.shape for a in to_reduce])
    loss = red[-1][0]
    g_small = dict(zip(_REPLICATED, [r.reshape(w[n].shape) for r, n in zip(red[:len(rep_grads)], _REPLICATED)]))
    for n, r in zip(_CHIP_VECTORS, red[len(rep_grads):-1]):
        g_small[n] = lax.dynamic_slice_in_dim(r.reshape(1, d), chip * (d // N_CHIPS), d // N_CHIPS, axis=1)
    small_names = _REPLICATED + _CHIP_VECTORS
    n_small = sum(math.prod(w[n].shape) for n in small_names)
    rows_small = -(-n_small // (128 * 8)) * 8
    packed = [_pack([src[n] for n in small_names], rows_small) for src in (g_small, w, m, v)]
    upd = _adamw_small(*packed, "adamw_small")
    shapes = [w[n].shape for n in small_names]
    for n, dl, mo, vo in zip(small_names, *[_unpack(u, shapes) for u in upd]):
        result[n] = [g_small[n], dl, mo, vo]

    partial, other = _swap_wait(partial, swap_land, swap_send, swap_recv, upd[0])
    part = dict(zip(names, zip(partial, other)))
    for name in _BIG:
        parts = [part[n] for n in sorted((n for n in names if local[n][0] == name), key=lambda n: local[n][1])]
        outs = _adamw(parts, layers(w[name]), layers(m[name]), layers(v[name]), f"adamw_{name}")
        result[name] = [o.reshape(w[name].shape) for o in outs]

    cols = [[result[n][t] for n in _WEIGHT_ORDER] for t in range(4)]
    return (loss, grad_x.reshape(x.shape), *cols[0], *cols[1], *cols[2], *cols[3])


def kernel(x, s5_lam_re, s5_lam_im, s5_log_dt, s5_b_re, s5_b_im, s5_c_re, s5_c_im, s5_d, s5_w_glu, a_norm_mix, ffn_norm, ffn_w_in, ffn_w_out, b_norm_mix, attn_w_q, attn_w_o, kv_norm, w_kv, final_norm, loss_target, m_s5_lam_re, m_s5_lam_im, m_s5_log_dt, m_s5_b_re, m_s5_b_im, m_s5_c_re, m_s5_c_im, m_s5_d, m_s5_w_glu, m_a_norm_mix, m_ffn_norm, m_ffn_w_in, m_ffn_w_out, m_b_norm_mix, m_attn_w_q, m_attn_w_o, m_kv_norm, m_w_kv, m_final_norm, v_s5_lam_re, v_s5_lam_im, v_s5_log_dt, v_s5_b_re, v_s5_b_im, v_s5_c_re, v_s5_c_im, v_s5_d, v_s5_w_glu, v_a_norm_mix, v_ffn_norm, v_ffn_w_in, v_ffn_w_out, v_b_norm_mix, v_attn_w_q, v_attn_w_o, v_kv_norm, v_w_kv, v_final_norm):
    w = dict(zip(_WEIGHT_ORDER, (s5_lam_re, s5_lam_im, s5_log_dt, s5_b_re, s5_b_im, s5_c_re, s5_c_im, s5_d, s5_w_glu, a_norm_mix, ffn_norm, ffn_w_in, ffn_w_out, b_norm_mix, attn_w_q, attn_w_o, kv_norm, w_kv, final_norm)))
    m = dict(zip(_WEIGHT_ORDER, (m_s5_lam_re, m_s5_lam_im, m_s5_log_dt, m_s5_b_re, m_s5_b_im, m_s5_c_re, m_s5_c_im, m_s5_d, m_s5_w_glu, m_a_norm_mix, m_ffn_norm, m_ffn_w_in, m_ffn_w_out, m_b_norm_mix, m_attn_w_q, m_attn_w_o, m_kv_norm, m_w_kv, m_final_norm)))
    v = dict(zip(_WEIGHT_ORDER, (v_s5_lam_re, v_s5_lam_im, v_s5_log_dt, v_s5_b_re, v_s5_b_im, v_s5_c_re, v_s5_c_im, v_s5_d, v_s5_w_glu, v_a_norm_mix, v_ffn_norm, v_ffn_w_in, v_ffn_w_out, v_b_norm_mix, v_attn_w_q, v_attn_w_o, v_kv_norm, v_w_kv, v_final_norm)))
    return _step(x, loss_target, w, m, v)
```

```python
import functools
import math

import jax
import jax.numpy as jnp
from jax import lax
from jax.experimental import pallas as pl
from jax.experimental.pallas import tpu as pltpu

F32 = jnp.float32
BF16 = jnp.bfloat16

S5_GROUP_CH = 16
S5_STATE = 64
GROUPS_PER_TILE = 8
HEAD_DIM = 128
N_KV_HEADS = 4
PATTERNS = ((128, 1), (512, 4), (2048, 16))
ATT_BLK = 128
EPS = 1e-6
NEG_INF = -1e30
SCAN_T = 512
ADAM_LR = 0.001
ADAM_B1 = 0.9
ADAM_B2 = 0.999
ADAM_EPS = 1e-08
ADAM_WD = 0.01
ADAM_STEP = 10
N_CHIPS = 4
N_DEV = 8
VMEM_LIMIT = 56 * 1024 * 1024
MESH = pl.DeviceIdType.MESH
GELU_K = math.sqrt(2.0 / math.pi)
GELU_C = 0.044715


def _tile(n, pref, unit=128):
    if n <= pref:
        return n
    best = None
    t = unit
    while t <= pref:
        if n % t == 0:
            best = t
        t += unit
    assert best is not None, (n, pref, unit)
    return best


def _params(sem):
    return pltpu.CompilerParams(dimension_semantics=sem, vmem_limit_bytes=VMEM_LIMIT)


def _dot(a, b):
    return jnp.dot(a, b, preferred_element_type=F32)


def _dot_nt(a, b):
    return lax.dot_general(a, b, (((1,), (1,)), ((), ())), preferred_element_type=F32)


def _dot_tn(a, b):
    return lax.dot_general(a, b, (((0,), (0,)), ((), ())), preferred_element_type=F32)


def _mm_nn(a, w, out_dtype, name, tm=512, tn=1536, tk=2048):
    m, k = a.shape
    nb, k2, nq = w.shape
    assert k == k2
    tm, tn, tk = _tile(m, tm, 8), _tile(nq, tn), _tile(k, tk)
    per, nk = nq // tn, k // tk

    def kern(a_ref, w_ref, o_ref, *acc):
        p = _dot(a_ref[...], w_ref[...])
        if nk == 1:
            o_ref[...] = p.astype(o_ref.dtype)
        else:
            acc_ref, = acc
            kk = pl.program_id(2)

            @pl.when(kk == 0)
            def _():
                acc_ref[...] = p

            @pl.when(kk > 0)
            def _():
                acc_ref[...] += p

            @pl.when(kk == nk - 1)
            def _():
                o_ref[...] = acc_ref[...].astype(o_ref.dtype)

    return pl.pallas_call(
        kern, name=name, grid=(nb * per, m // tm, nk),
        in_specs=[pl.BlockSpec((tm, tk), lambda j, i, kk: (i, kk)),
                  pl.BlockSpec((None, tk, tn), lambda j, i, kk: (j // per, kk, j % per))],
        out_specs=pl.BlockSpec((tm, tn), lambda j, i, kk: (i, j)),
        out_shape=jax.ShapeDtypeStruct((m, nb * nq), out_dtype),
        scratch_shapes=[] if nk == 1 else [pltpu.VMEM((tm, tn), F32)],
        compiler_params=_params(("parallel", "parallel", "arbitrary")),
    )(a, w)


def _paired_block(r, per, nb):
    j = r // 2
    return (r % 2) * (nb // 2) + j // per, j % per


def _mm_nt(a, w, name, tm=512, tr=1536, tkc=1024, paired=False):
    m, n = a.shape
    nb, k, nq = w.shape
    assert n == nb * nq
    tm, tr, tkc = _tile(m, tm, 8), _tile(nq, tr), _tile(k, tkc)
    per = nq // tr

    def kern(a_ref, w_ref, o_ref):
        acc = None
        for r in range(nb * per):
            blk, tile = _paired_block(r, per, nb) if paired else (r // per, r % per)
            p = _dot_nt(a_ref[:, r * tr:(r + 1) * tr], w_ref[blk, :, tile * tr:(tile + 1) * tr])
            acc = p if acc is None else acc + p
        o_ref[...] = acc

    return pl.pallas_call(
        kern, name=name, grid=(k // tkc, m // tm),
        in_specs=[pl.BlockSpec((tm, n), lambda kc, i: (i, 0)),
                  pl.BlockSpec((nb, tkc, nq), lambda kc, i: (0, kc, 0))],
        out_specs=pl.BlockSpec((tm, tkc), lambda kc, i: (i, kc)),
        out_shape=jax.ShapeDtypeStruct((m, k), F32),
        compiler_params=_params(("parallel", "parallel")),
    )(a, w)


def _mm_tn(a, dy, nb, name, tkk=512, tn=1536, paired=False):
    s, k = a.shape
    s2, n = dy.shape
    assert s == s2 and n % nb == 0
    nq = n // nb
    tkk, tn = _tile(k, tkk), _tile(nq, tn)
    per = nq // tn

    def w_block(j):
        return _paired_block(j, per, nb) if paired else (j // per, j % per)

    def kern(a_ref, dy_ref, o_ref):
        o_ref[...] = _dot_tn(a_ref[...], dy_ref[...]).astype(o_ref.dtype)

    return pl.pallas_call(
        kern, name=name, grid=(nb * per, k // tkk),
        in_specs=[pl.BlockSpec((s, tkk), lambda j, kk: (0, kk)),
                  pl.BlockSpec((s, tn), lambda j, kk: (0, j))],
        out_specs=pl.BlockSpec((None, tkk, tn), lambda j, kk: (w_block(j)[0], kk, w_block(j)[1])),
        out_shape=jax.ShapeDtypeStruct((nb, k, nq), BF16),
        compiler_params=_params(("parallel", "parallel")),
    )(a, dy)


ROWS = 256


def _rms(x, g):
    r = lax.rsqrt(jnp.mean(x * x, axis=-1, keepdims=True) + EPS)
    return x * r * g


def _rms_bwd(x, g, dh):
    r = lax.rsqrt(jnp.mean(x * x, axis=-1, keepdims=True) + EPS)
    xh = x * r
    dgx = dh * g
    dx = r * (dgx - xh * jnp.mean(dgx * xh, axis=-1, keepdims=True))
    return dx, dh * xh


def _sigmoid(x):
    return 1.0 / (1.0 + jnp.exp(-x))


def _gelu(y):
    return 0.5 * y * (1.0 + jnp.tanh(GELU_K * (y + GELU_C * y * y * y)))


def _gelu_grad(y):
    t = jnp.tanh(GELU_K * (y + GELU_C * y * y * y))
    return 0.5 * (1.0 + t) + 0.5 * y * (1.0 - t * t) * GELU_K * (1.0 + 3.0 * GELU_C * y * y)


def _row_spec(tm, d, col=0):
    return pl.BlockSpec((tm, d), lambda i: (i, col))


def _vec_spec(d):
    return pl.BlockSpec((1, d), lambda i: (0, 0))


def _acc_rows(ref, val, i):
    s = jnp.sum(val, axis=0, keepdims=True)

    @pl.when(i == 0)
    def _():
        ref[...] = s

    @pl.when(i > 0)
    def _():
        ref[...] += s


def _norm_f32(x, g, name):
    s, d = x.shape
    tm = _tile(s, ROWS, 8)

    def kern(x_ref, g_ref, h_ref):
        h_ref[...] = _rms(x_ref[...], g_ref[...])

    return pl.pallas_call(
        kern, name=name, grid=(s // tm,),
        in_specs=[_row_spec(tm, d), _vec_spec(d)],
        out_specs=_row_spec(tm, d),
        out_shape=jax.ShapeDtypeStruct((s, d), F32),
        compiler_params=_params(("parallel",)),
    )(x, g)


def _glu_res_norm(vg, x, g, name):
    s, d = x.shape
    tm = _tile(s, ROWS, 16)

    def kern(val_ref, gate_ref, x_ref, g_ref, x1_ref, hf_ref):
        x1 = x_ref[...] + val_ref[...] * _sigmoid(gate_ref[...])
        x1_ref[...] = x1
        hf_ref[...] = _rms(x1, g_ref[...]).astype(BF16)

    return pl.pallas_call(
        kern, name=name, grid=(s // tm,),
        in_specs=[_row_spec(tm, d, 0), _row_spec(tm, d, 1), _row_spec(tm, d), _vec_spec(d)],
        out_specs=[_row_spec(tm, d), _row_spec(tm, d)],
        out_shape=[jax.ShapeDtypeStruct((s, d), F32), jax.ShapeDtypeStruct((s, d), BF16)],
        compiler_params=_params(("parallel",)),
    )(vg, vg, x, g)


FFN_TILE = 1408


def _ffn_in_act(hf, w_in, name, tm=512):
    s, k = hf.shape
    nb, _, nq = w_in.shape
    tm, tn = _tile(s, tm, 16), _tile(nq, FFN_TILE)
    per = nq // tn
    nf = (nb // 2) * per

    def kern(h_ref, wg_ref, wu_ref, a_ref, gu_ref):
        h = h_ref[...]
        g = _dot(h, wg_ref[...])
        u = _dot(h, wu_ref[...])
        a_ref[...] = (g * _sigmoid(g) * u).astype(BF16)
        gu_ref[:, :tn] = g.astype(BF16)
        gu_ref[:, tn:] = u.astype(BF16)

    return pl.pallas_call(
        kern, name=name, grid=(nf, s // tm),
        in_specs=[pl.BlockSpec((tm, k), lambda j, i: (i, 0)),
                  pl.BlockSpec((None, k, tn), lambda j, i: (j // per, 0, j % per)),
                  pl.BlockSpec((None, k, tn), lambda j, i: (nb // 2 + j // per, 0, j % per))],
        out_specs=[pl.BlockSpec((tm, tn), lambda j, i: (i, j)),
                   pl.BlockSpec((tm, 2 * tn), lambda j, i: (i, j))],
        out_shape=[jax.ShapeDtypeStruct((s, nf * tn), BF16), jax.ShapeDtypeStruct((s, 2 * nf * tn), BF16)],
        compiler_params=_params(("parallel", "parallel")),
    )(hf, w_in, w_in)


def _ffn_out_dx_act(dresb, w_out, gu, name, tm=512):
    s, d = dresb.shape
    f = w_out.shape[1]
    tm = _tile(s, tm, 16)
    tn = _tile(f // 2, FFN_TILE)

    def kern(d_ref, w_ref, gu_ref, o_ref):
        da = _dot_nt(d_ref[...], w_ref[...])
        g = gu_ref[:, :tn].astype(F32)
        u = gu_ref[:, tn:].astype(F32)
        sg = _sigmoid(g)
        o_ref[:, :tn] = (da * u * sg * (1.0 + g * (1.0 - sg))).astype(BF16)
        o_ref[:, tn:] = (da * g * sg).astype(BF16)

    pair = pl.BlockSpec((tm, 2 * tn), lambda j, i: (i, j))
    return pl.pallas_call(
        kern, name=name, grid=(f // tn, s // tm),
        in_specs=[pl.BlockSpec((tm, d), lambda j, i: (i, 0)),
                  pl.BlockSpec((None, tn, d), lambda j, i: (0, j, 0)), pair],
        out_specs=pair,
        out_shape=jax.ShapeDtypeStruct((s, 2 * f), BF16),
        compiler_params=_params(("parallel", "parallel")),
    )(dresb, w_out, gu)


def _res_norm(o, x, gains, name):
    s, d = x.shape
    tm = _tile(s, ROWS, 16)
    ng = len(gains)

    def kern(o_ref, x_ref, *refs):
        xn = x_ref[...] + o_ref[...]
        refs[ng][...] = xn
        for t in range(ng):
            refs[ng + 1 + t][...] = _rms(xn, refs[t][...]).astype(BF16)

    return pl.pallas_call(
        kern, name=name, grid=(s // tm,),
        in_specs=[_row_spec(tm, d), _row_spec(tm, d)] + [_vec_spec(d)] * ng,
        out_specs=[_row_spec(tm, d)] * (1 + ng),
        out_shape=[jax.ShapeDtypeStruct((s, d), F32)] + [jax.ShapeDtypeStruct((s, d), BF16)] * ng,
        compiler_params=_params(("parallel",)),
    )(o, x, *gains)


def _loss_head(o, x, g, target, name):
    s, d = x.shape
    tm = _tile(s, ROWS, 16)

    def kern(o_ref, x_ref, g_ref, t_ref, dx_ref, dxb_ref, loss_ref, dg_ref):
        i = pl.program_id(0)
        x4 = x_ref[...] + o_ref[...]
        gg = g_ref[...]
        diff = _rms(x4, gg) - t_ref[...]
        dx, dgr = _rms_bwd(x4, gg, diff * (1.0 / d))
        dx_ref[...] = dx
        dxb_ref[...] = dx.astype(BF16)
        _acc_rows(loss_ref, diff * diff * (0.5 / d), i)
        _acc_rows(dg_ref, dgr, i)

    return pl.pallas_call(
        kern, name=name, grid=(s // tm,),
        in_specs=[_row_spec(tm, d), _row_spec(tm, d), _vec_spec(d), _row_spec(tm, d)],
        out_specs=[_row_spec(tm, d), _row_spec(tm, d), _vec_spec(d), _vec_spec(d)],
        out_shape=[jax.ShapeDtypeStruct((s, d), F32), jax.ShapeDtypeStruct((s, d), BF16),
                   jax.ShapeDtypeStruct((1, d), F32), jax.ShapeDtypeStruct((1, d), F32)],
        compiler_params=_params(("arbitrary",)),
    )(o, x, g, target)


def _norm_bwd(x, gains, dhs, dres, name):
    s, d = x.shape
    tm = _tile(s, ROWS, 16)
    ng = len(gains)

    def kern(x_ref, dres_ref, *refs):
        i = pl.program_id(0)
        x_ = x_ref[...]
        acc = dres_ref[...]
        for t in range(ng):
            dx, dgr = _rms_bwd(x_, refs[t][...], refs[ng + t][...])
            acc = acc + dx
            _acc_rows(refs[2 * ng + 2 + t], dgr, i)
        refs[2 * ng][...] = acc
        refs[2 * ng + 1][...] = acc.astype(BF16)

    return pl.pallas_call(
        kern, name=name, grid=(s // tm,),
        in_specs=[_row_spec(tm, d), _row_spec(tm, d)] + [_vec_spec(d)] * ng + [_row_spec(tm, d)] * ng,
        out_specs=[_row_spec(tm, d), _row_spec(tm, d)] + [_vec_spec(d)] * ng,
        out_shape=[jax.ShapeDtypeStruct((s, d), F32), jax.ShapeDtypeStruct((s, d), BF16)]
        + [jax.ShapeDtypeStruct((1, d), F32)] * ng,
        compiler_params=_params(("arbitrary",)),
    )(x, dres, *gains, *dhs)


def _glu_bwd(dmix, vg, name):
    s, d = dmix.shape
    tm = _tile(s, ROWS, 16)

    def kern(dm_ref, val_ref, gate_ref, o_ref):
        dm = dm_ref[...]
        sg = _sigmoid(gate_ref[...])
        o_ref[:, :d] = (dm * sg).astype(BF16)
        o_ref[:, d:] = (dm * val_ref[...] * sg * (1.0 - sg)).astype(BF16)

    return pl.pallas_call(
        kern, name=name, grid=(s // tm,),
        in_specs=[_row_spec(tm, d), _row_spec(tm, d, 0), _row_spec(tm, d, 1)],
        out_specs=_row_spec(tm, 2 * d),
        out_shape=jax.ShapeDtypeStruct((s, 2 * d), BF16),
        compiler_params=_params(("parallel",)),
    )(dmix, vg, vg)


SUB = 8


def _local_scan(vr, vi, pre_ref, pim_ref, reverse):
    sub = lax.broadcasted_iota(jnp.int32, vr.shape, 1)
    sign = -1.0 if reverse else 1.0
    for sh in (1, 2, 4):
        ar = pre_ref[sh - 1:sh, :][None]
        ai = sign * pim_ref[sh - 1:sh, :][None]
        keep = sub < SUB - sh if reverse else sub >= sh
        sr = jnp.where(keep, pltpu.roll(vr, SUB - sh if reverse else sh, 1), 0.0)
        si = jnp.where(keep, pltpu.roll(vi, SUB - sh if reverse else sh, 1), 0.0)
        vr, vi = vr + ar * sr - ai * si, vi + ar * si + ai * sr
    return vr, vi


def _s5_fwd(u, bbd_re, bbd_im, cbd_re, cbd_im, pw_re, pw_im, dskip, name):
    s, d = u.shape
    nt, cw, lw = bbd_re.shape
    t = _tile(s, SCAN_T, SUB)
    nc = s // t
    ng = t // SUB

    def kern(u_ref, bre_ref, bim_ref, cre_ref, cim_ref, pre_ref, pim_ref, d_ref,
             xr_ref, xi_ref, y_ref, z_ref, car_ref, cai_ref):
        c = pl.program_id(1)

        @pl.when(c == 0)
        def _():
            car_ref[...] = jnp.zeros_like(car_ref)
            cai_ref[...] = jnp.zeros_like(cai_ref)

        u_ = u_ref[...]
        ub = u_.astype(BF16)
        vr, vi = _local_scan(_dot(ub, bre_ref[...]).reshape(ng, SUB, lw),
                             _dot(ub, bim_ref[...]).reshape(ng, SUB, lw), pre_ref, pim_ref, False)
        cr = car_ref[...]
        ci = cai_ref[...]
        pr = pre_ref[...]
        pi = pim_ref[...]
        for gidx in range(ng):
            rows = slice(gidx * SUB, (gidx + 1) * SUB)
            gr = vr[gidx] + pr * cr - pi * ci
            gi = vi[gidx] + pr * ci + pi * cr
            xr_ref[rows, :] = gr
            xi_ref[rows, :] = gi
            cr, ci = gr[SUB - 1:SUB, :], gi[SUB - 1:SUB, :]
        car_ref[...] = cr
        cai_ref[...] = ci
        y = (_dot(xr_ref[...].astype(BF16), cre_ref[...]) - _dot(xi_ref[...].astype(BF16), cim_ref[...])
             + d_ref[...] * u_)
        y_ref[...] = y
        z_ref[...] = _gelu(y).astype(BF16)

    tok = pl.BlockSpec((t, cw), lambda j, c: (c, j))
    st = pl.BlockSpec((t, lw), lambda j, c: (c, j))
    return pl.pallas_call(
        kern, name=name, grid=(nt, nc),
        in_specs=[tok,
                  pl.BlockSpec((None, cw, lw), lambda j, c: (j, 0, 0)),
                  pl.BlockSpec((None, cw, lw), lambda j, c: (j, 0, 0)),
                  pl.BlockSpec((None, lw, cw), lambda j, c: (j, 0, 0)),
                  pl.BlockSpec((None, lw, cw), lambda j, c: (j, 0, 0)),
                  pl.BlockSpec((SUB, lw), lambda j, c: (0, j)),
                  pl.BlockSpec((SUB, lw), lambda j, c: (0, j)),
                  pl.BlockSpec((1, cw), lambda j, c: (0, j))],
        out_specs=[st, st, tok, tok],
        out_shape=[jax.ShapeDtypeStruct((s, nt * lw), F32), jax.ShapeDtypeStruct((s, nt * lw), F32),
                   jax.ShapeDtypeStruct((s, d), F32), jax.ShapeDtypeStruct((s, d), BF16)],
        scratch_shapes=[pltpu.VMEM((1, lw), F32), pltpu.VMEM((1, lw), F32)],
        compiler_params=_params(("parallel", "arbitrary")),
    )(u, bbd_re, bbd_im, cbd_re, cbd_im, pw_re, pw_im, dskip)


def _s5_bwd(dz, y, u, xs_re, xs_im, bbd_re, bbd_im, cbd_re, cbd_im, pw_re, pw_im, pf_re, pf_im, dskip, name):
    s, d = u.shape
    nt, cw, lw = bbd_re.shape
    t = _tile(s, SCAN_T, SUB)
    nc = s // t
    ng = t // SUB

    def kern(dz_ref, y_ref, u_ref, xr_ref, xi_ref, bre_ref, bim_ref, cre_ref, cim_ref,
             pre_ref, pim_ref, fre_ref, fim_ref, d_ref,
             du_ref, dd_ref, dcr_ref, dci_ref, dbr_ref, dbi_ref, dar_ref, dai_ref,
             car_ref, cai_ref, gr_ref, gi_ref):
        c = pl.program_id(1)

        @pl.when(c == 0)
        def _():
            car_ref[...] = jnp.zeros_like(car_ref)
            cai_ref[...] = jnp.zeros_like(cai_ref)

        u_ = u_ref[...]
        ub = u_.astype(BF16)
        dy = dz_ref[...] * _gelu_grad(y_ref[...])
        dyb = dy.astype(BF16)
        vr, vi = _local_scan(_dot_nt(dyb, cre_ref[...]).reshape(ng, SUB, lw),
                             (-_dot_nt(dyb, cim_ref[...])).reshape(ng, SUB, lw), pre_ref, pim_ref, True)
        later_r = car_ref[...]
        later_i = cai_ref[...]
        cr, ci = later_r, later_i
        fr = fre_ref[...]
        fi = fim_ref[...]
        for gidx in reversed(range(ng)):
            rows = slice(gidx * SUB, (gidx + 1) * SUB)
            ar = vr[gidx] + fr * cr + fi * ci
            ai = vi[gidx] + fr * ci - fi * cr
            gr_ref[rows, :] = ar
            gi_ref[rows, :] = ai
            cr, ci = ar[0:1, :], ai[0:1, :]
        car_ref[...] = cr
        cai_ref[...] = ci
        gr = gr_ref[...]
        gi = gi_ref[...]
        rows = lax.broadcasted_iota(jnp.int32, (t, lw), 0)
        gsr = jnp.where(rows < t - 1, pltpu.roll(gr, t - 1, 0), later_r)
        gsi = jnp.where(rows < t - 1, pltpu.roll(gi, t - 1, 0), later_i)
        xr = xr_ref[...]
        xi = xi_ref[...]
        dar = jnp.sum(gsr * xr + gsi * xi, axis=0, keepdims=True)
        dai = jnp.sum(gsi * xr - gsr * xi, axis=0, keepdims=True)
        grb = gr.astype(BF16)
        gib = gi.astype(BF16)
        dbr = _dot_tn(ub, grb)
        dbi = _dot_tn(ub, gib)
        dcr = _dot_tn(dyb, xr.astype(BF16))
        dci = _dot_tn(dyb, xi.astype(BF16))
        du_ref[...] = dy * d_ref[...] + _dot_nt(grb, bre_ref[...]) + _dot_nt(gib, bim_ref[...])
        ddv = jnp.sum(dy * u_, axis=0, keepdims=True)

        @pl.when(c == 0)
        def _():
            dd_ref[...] = ddv
            dcr_ref[...] = dcr
            dci_ref[...] = dci
            dbr_ref[...] = dbr
            dbi_ref[...] = dbi
            dar_ref[...] = dar
            dai_ref[...] = dai

        @pl.when(c > 0)
        def _():
            dd_ref[...] += ddv
            dcr_ref[...] += dcr
            dci_ref[...] += dci
            dbr_ref[...] += dbr
            dbi_ref[...] += dbi
            dar_ref[...] += dar
            dai_ref[...] += dai

    tok = pl.BlockSpec((t, cw), lambda j, c: (nc - 1 - c, j))
    st = pl.BlockSpec((t, lw), lambda j, c: (nc - 1 - c, j))
    wb = pl.BlockSpec((None, cw, lw), lambda j, c: (j, 0, 0))
    wc = pl.BlockSpec((None, lw, cw), lambda j, c: (j, 0, 0))
    pw = pl.BlockSpec((SUB, lw), lambda j, c: (0, j))
    vec_c = pl.BlockSpec((1, cw), lambda j, c: (0, j))
    vec_l = pl.BlockSpec((1, lw), lambda j, c: (0, j))
    return pl.pallas_call(
        kern, name=name, grid=(nt, nc),
        in_specs=[tok, tok, tok, st, st, wb, wb, wc, wc, pw, pw, pw, pw, vec_c],
        out_specs=[tok, vec_c, wb, wb, wb, wb, vec_l, vec_l],
        out_shape=[jax.ShapeDtypeStruct((s, d), F32), jax.ShapeDtypeStruct((1, d), F32)]
        + [jax.ShapeDtypeStruct((nt, cw, lw), F32)] * 4
        + [jax.ShapeDtypeStruct((1, nt * lw), F32)] * 2,
        scratch_shapes=[pltpu.VMEM((1, lw), F32), pltpu.VMEM((1, lw), F32),
                        pltpu.VMEM((t, lw), F32), pltpu.VMEM((t, lw), F32)],
        compiler_params=_params(("parallel", "arbitrary")),
    )(dz, y, u, xs_re, xs_im, bbd_re, bbd_im, cbd_re, cbd_im, pw_re, pw_im, pf_re, pf_im, dskip)


def _s5_discretize(lam_re, lam_im, log_dt, b_re, b_im):
    dt = jnp.exp(log_dt)[:, None]
    mag = jnp.exp(lam_re * dt)
    ang = lam_im * dt
    lb_re = mag * jnp.cos(ang)
    lb_im = mag * jnp.sin(ang)
    nr = lb_re - 1.0
    den = lam_re * lam_re + lam_im * lam_im
    f_re = (nr * lam_re + lb_im * lam_im) / den
    f_im = (lb_im * lam_re - nr * lam_im) / den
    bb_re = f_re[..., None] * b_re - f_im[..., None] * b_im
    bb_im = f_re[..., None] * b_im + f_im[..., None] * b_re
    return lb_re, lb_im, bb_re, bb_im


def _block_diag(w):
    nt, ng, a, b = w.shape
    eye = jnp.eye(ng, dtype=w.dtype)
    return (w[:, :, :, None, :] * eye[None, :, None, :, None]).reshape(nt, ng * a, ng * b)


def _block_diag_take(w, a, b):
    nt = w.shape[0]
    ng = GROUPS_PER_TILE
    w5 = w.reshape(nt, ng, a, ng, b)
    on_diagonal = jnp.eye(ng, dtype=bool)[None, :, None, :, None]
    return jnp.sum(jnp.where(on_diagonal, w5, 0.0), axis=3).reshape(nt * ng, a, b)


def _att_combine(outs, lses, name):
    s, d = outs[0].shape
    tm = _tile(s, ROWS, 16)
    ng = len(outs)

    def kern(*refs):
        for h in range(d // HEAD_DIM):
            cols = slice(h * HEAD_DIM, (h + 1) * HEAD_DIM)
            ls = [refs[ng + t][:, h:h + 1] for t in range(ng)]
            mx = functools.reduce(jnp.maximum, ls)
            es = [jnp.exp(l - mx) for l in ls]
            den = functools.reduce(lambda a, b: a + b, es)
            o = functools.reduce(lambda a, b: a + b, [es[t] / den * refs[t][:, cols] for t in range(ng)])
            refs[2 * ng][:, cols] = o
            refs[2 * ng + 1][:, cols] = o.astype(BF16)

    return pl.pallas_call(
        kern, name=name, grid=(s // tm,),
        in_specs=[_row_spec(tm, d)] * ng + [_row_spec(tm, d // HEAD_DIM)] * ng,
        out_specs=[_row_spec(tm, d)] * 2,
        out_shape=[jax.ShapeDtypeStruct((s, d), F32), jax.ShapeDtypeStruct((s, d), BF16)],
        compiler_params=_params(("parallel",)),
    )(*outs, *lses)


DIL_MAX = 16


def _slab(r):
    return 4 * (r % 4) + r // 4


def _to_slabs(xs, name):
    s, w = xs[0].shape
    n = s // DIL_MAX
    nx = len(xs)

    def kern(*refs):
        o_ref = refs[nx]
        for r in range(DIL_MAX):
            rows = [x_ref[pl.ds(r, n, stride=DIL_MAX), :] for x_ref in refs[:nx]]
            o_ref[_slab(r) * n:(_slab(r) + 1) * n, :] = functools.reduce(lambda a, b: a + b, rows)

    spec = pl.BlockSpec((s, 128), lambda i: (0, i))
    return pl.pallas_call(
        kern, name=name, grid=(w // 128,), in_specs=[spec] * nx, out_specs=spec,
        out_shape=jax.ShapeDtypeStruct((s, w), F32), compiler_params=_params(("parallel",)),
    )(*xs)


def _from_slabs(x, name):
    s, w = x.shape
    n = s // DIL_MAX

    def kern(x_ref, o_ref, ob_ref):
        for r in range(DIL_MAX):
            o_ref[pl.ds(r, n, stride=DIL_MAX), :] = x_ref[_slab(r) * n:(_slab(r) + 1) * n, :]
        ob_ref[...] = o_ref[...].astype(BF16)

    spec = pl.BlockSpec((s, 128), lambda i: (0, i))
    return pl.pallas_call(
        kern, name=name, grid=(w // 128,), in_specs=[spec], out_specs=[spec, spec],
        out_shape=[jax.ShapeDtypeStruct((s, w), F32), jax.ShapeDtypeStruct((s, w), BF16)],
        compiler_params=_params(("parallel",)),
    )(x)


def _norm_bf16(x, gains, name):
    s, d = x.shape
    tm = _tile(s, ROWS, 16)
    ng = len(gains)

    def kern(x_ref, *refs):
        x_ = x_ref[...]
        for t in range(ng):
            refs[ng + t][...] = _rms(x_, refs[t][...]).astype(BF16)

    return pl.pallas_call(
        kern, name=name, grid=(s // tm,),
        in_specs=[_row_spec(tm, d)] + [_vec_spec(d)] * ng, out_specs=[_row_spec(tm, d)] * ng,
        out_shape=[jax.ShapeDtypeStruct((s, d), BF16)] * ng,
        compiler_params=_params(("parallel",)),
    )(x, *gains)


def _att_geometry(dil, s):
    g = DIL_MAX // dil
    return s // DIL_MAX, g, max(ATT_BLK // g, 16)


def _att_mask(g, j, jb, rep):
    b = g * j

    def pos(i):
        sl, jj = i // j, i % j
        off = {1: 0, 4: sl, 16: sl // 4 + 4 * (sl % 4)}[g]
        return g * jj + off

    qi = lax.broadcasted_iota(jnp.int32, (rep * b, 2 * b), 0) % b
    ki = lax.broadcasted_iota(jnp.int32, (rep * b, 2 * b), 1)
    prev = ki < b
    dist = pos(qi) - pos(ki % b) + jnp.where(prev, b, 0)
    return (dist >= 0) & (dist <= ATT_BLK) & (jnp.logical_not(prev) | (jb > 0))


def _stack_heads(x, rep):
    return jnp.concatenate([x[:, h * HEAD_DIM:(h + 1) * HEAD_DIM] for h in range(rep)], axis=0)


def _att_fwd(q, kv, gi, dil, nh, name):
    s = q.shape[0]
    n, g, j = _att_geometry(dil, s)
    b = g * j
    ng = q.shape[1] // (nh * HEAD_DIM)
    rep = nh // N_KV_HEADS
    qw = rep * HEAD_DIM
    scale = HEAD_DIM ** -0.5
    q3 = q.reshape(DIL_MAX, n, q.shape[1])
    kv3 = kv.reshape(DIL_MAX, n, kv.shape[1])

    kw = N_KV_HEADS * HEAD_DIM

    def kern(q_ref, kc_ref, kp_ref, vc_ref, vp_ref, o_ref, l_ref, lw_ref):
        jb = pl.program_id(1)
        mask = _att_mask(g, j, jb, rep)
        for kh in range(N_KV_HEADS):
            hs = slice(kh * HEAD_DIM, (kh + 1) * HEAD_DIM)
            kc, kp, vc, vp = [r[:, :, hs].reshape(b, HEAD_DIM) for r in (kc_ref, kp_ref, vc_ref, vp_ref)]
            k2 = jnp.concatenate([kp, kc], axis=0)
            v2 = jnp.concatenate([vp, vc], axis=0)
            qs = _stack_heads(q_ref[:, :, kh * qw:(kh + 1) * qw].reshape(b, qw), rep)
            sc = jnp.where(mask, _dot_nt(qs, k2) * scale, NEG_INF)
            m = jnp.max(sc, axis=-1, keepdims=True)
            p = jnp.exp(sc - m)
            l = jnp.sum(p, axis=-1, keepdims=True)
            out = _dot((p / l).astype(BF16), v2)
            lse = m + jnp.log(l)
            wide = jnp.broadcast_to(lse, (rep * b, HEAD_DIM))
            for h in range(rep):
                cols = slice(kh * qw + h * HEAD_DIM, kh * qw + (h + 1) * HEAD_DIM)
                o_ref[:, :, cols] = out[h * b:(h + 1) * b].reshape(g, j, HEAD_DIM)
                lw_ref[:, :, cols] = wide[h * b:(h + 1) * b].reshape(g, j, HEAD_DIM)
                l_ref[:, :, kh * rep + h:kh * rep + h + 1] = lse[h * b:(h + 1) * b].reshape(g, j, 1)

    def kv_spec(col, prev):
        if prev:
            return pl.BlockSpec((g, j, kw), lambda c, jb: (c, jnp.maximum(jb - 1, 0), col))
        return pl.BlockSpec((g, j, kw), lambda c, jb: (c, jb, col))

    out_spec = pl.BlockSpec((g, j, nh * HEAD_DIM), lambda c, jb: (c, jb, 0))
    out, lse, lse_wide = pl.pallas_call(
        kern, name=name, grid=(dil, n // j),
        in_specs=[pl.BlockSpec((g, j, nh * HEAD_DIM), lambda c, jb: (c, jb, gi)),
                  kv_spec(0, False), kv_spec(0, True), kv_spec(1, False), kv_spec(1, True)],
        out_specs=[out_spec, pl.BlockSpec((g, j, nh), lambda c, jb: (c, jb, 0)), out_spec],
        out_shape=[jax.ShapeDtypeStruct((DIL_MAX, n, nh * HEAD_DIM), F32),
                   jax.ShapeDtypeStruct((DIL_MAX, n, nh), F32),
                   jax.ShapeDtypeStruct((DIL_MAX, n, nh * HEAD_DIM), F32)],
        compiler_params=_params(("parallel", "arbitrary")),
    )(q3, kv3, kv3, kv3, kv3)
    del ng
    return out.reshape(s, nh * HEAD_DIM), lse.reshape(s, nh), lse_wide.reshape(s, nh * HEAD_DIM)


def _att_bwd(q, kv, do, o, lses, dq_all, gi, dil, nh, name):
    s = q.shape[0]
    n, g, j = _att_geometry(dil, s)
    b = g * j
    nb = n // j
    ng = len(lses)
    rep = nh // N_KV_HEADS
    qw = rep * HEAD_DIM
    dm = nh * HEAD_DIM
    scale = HEAD_DIM ** -0.5
    q3 = q.reshape(DIL_MAX, n, q.shape[1])
    kv3 = kv.reshape(DIL_MAX, n, kv.shape[1])
    wide = [a.reshape(DIL_MAX, n, dm) for a in (do, o, *lses)]
    kw = N_KV_HEADS * HEAD_DIM

    def kern(q_ref, kc_ref, kp_ref, vc_ref, vp_ref, do_ref, o_ref, *refs):
        l_refs = refs[:ng]
        dq_ref, dk_ref, dv_ref, ck_ref, cv_ref = refs[ng + 1:]
        jb = pl.program_id(1)

        @pl.when(jb == 0)
        def _():
            ck_ref[...] = jnp.zeros_like(ck_ref)
            cv_ref[...] = jnp.zeros_like(cv_ref)

        @pl.when(jb < nb)
        def _():
            mask = _att_mask(g, j, jb, rep)
            for kh in range(N_KV_HEADS):
                hs = slice(kh * HEAD_DIM, (kh + 1) * HEAD_DIM)
                ws = slice(kh * qw, (kh + 1) * qw)

                def stacked(ref):
                    return _stack_heads(ref[:, :, ws].reshape(b, qw), rep)

                kc, kp, vc, vp = [r[:, :, hs].reshape(b, HEAD_DIM) for r in (kc_ref, kp_ref, vc_ref, vp_ref)]
                k2 = jnp.concatenate([kp, kc], axis=0)
                v2 = jnp.concatenate([vp, vc], axis=0)
                qs = stacked(q_ref)
                ls = [stacked(r)[:, 0:1] for r in l_refs]
                mx = functools.reduce(jnp.maximum, ls)
                den = functools.reduce(lambda a, c: a + c, [jnp.exp(l - mx) for l in ls])
                lse_g = ls[gi]
                w = jnp.exp(lse_g - mx) / den
                do_ = stacked(do_ref)
                ct = w * jnp.sum(do_ * stacked(o_ref), axis=-1, keepdims=True)
                dob = (w * do_).astype(BF16)
                p = jnp.exp(jnp.where(mask, _dot_nt(qs, k2) * scale, NEG_INF) - lse_g)
                ds = (p * (_dot_nt(dob, v2) - ct) * scale).astype(BF16)
                dq = (_dot(ds, k2)).astype(BF16)
                for h in range(rep):
                    cols = slice(kh * qw + h * HEAD_DIM, kh * qw + (h + 1) * HEAD_DIM)
                    dq_ref[:, :, cols] = dq[h * b:(h + 1) * b].reshape(g, j, HEAD_DIM)
                dk2 = _dot_tn(ds, qs)
                dv2 = _dot_tn(p.astype(BF16), dob)
                dk_ref[:, :, hs] = (ck_ref[:, hs] + dk2[:b]).reshape(g, j, HEAD_DIM)
                dv_ref[:, :, hs] = (cv_ref[:, hs] + dv2[:b]).reshape(g, j, HEAD_DIM)
                ck_ref[:, hs] = dk2[b:]
                cv_ref[:, hs] = dv2[b:]

        @pl.when(jb == nb)
        def _():
            dk_ref[...] = ck_ref[...].reshape(g, j, kw)
            dv_ref[...] = cv_ref[...].reshape(g, j, kw)

    def jq(jb):
        return jnp.minimum(jb, nb - 1)

    def kv_spec(col, prev):
        if prev:
            return pl.BlockSpec((g, j, kw), lambda c, jb: (c, jnp.maximum(jq(jb) - 1, 0), col))
        return pl.BlockSpec((g, j, kw), lambda c, jb: (c, jq(jb), col))

    wide_spec = pl.BlockSpec((g, j, dm), lambda c, jb: (c, jq(jb), 0))
    dkv_spec = pl.BlockSpec((g, j, kw), lambda c, jb: (c, jnp.maximum(jb - 1, 0), 0))
    dq, dk, dv = pl.pallas_call(
        kern, name=name, grid=(dil, nb + 1),
        in_specs=[pl.BlockSpec((g, j, dm), lambda c, jb: (c, jq(jb), gi)),
                  kv_spec(0, False), kv_spec(0, True), kv_spec(1, False), kv_spec(1, True)]
        + [wide_spec] * (2 + ng) + [_ANY],
        out_specs=[pl.BlockSpec((g, j, dm), lambda c, jb: (c, jq(jb), gi)), dkv_spec, dkv_spec],
        out_shape=[jax.ShapeDtypeStruct((DIL_MAX, n, dq_all.shape[1]), BF16),
                   jax.ShapeDtypeStruct((DIL_MAX, n, kw), F32),
                   jax.ShapeDtypeStruct((DIL_MAX, n, kw), F32)],
        scratch_shapes=[pltpu.VMEM((b, kw), F32), pltpu.VMEM((b, kw), F32)],
        input_output_aliases={7 + ng: 0},
        compiler_params=_params(("parallel", "arbitrary")),
    )(q3, kv3, kv3, kv3, kv3, *wide, dq_all.reshape(DIL_MAX, n, dq_all.shape[1]))
    return dq.reshape(s, -1), dk.reshape(s, -1), dv.reshape(s, -1)


def _dkv_sum(dks, dvs, name):
    s, w = dks[0].shape
    tm = _tile(s, ROWS, 16)
    ng = len(dks)

    def kern(*refs):
        o_ref = refs[2 * ng]
        o_ref[:, :w] = functools.reduce(lambda a, b: a + b, [refs[t][...] for t in range(ng)]).astype(BF16)
        o_ref[:, w:] = functools.reduce(lambda a, b: a + b, [refs[ng + t][...] for t in range(ng)]).astype(BF16)

    return pl.pallas_call(
        kern, name=name, grid=(s // tm,),
        in_specs=[_row_spec(tm, w)] * (2 * ng),
        out_specs=_row_spec(tm, 2 * w),
        out_shape=jax.ShapeDtypeStruct((s, 2 * w), BF16),
        compiler_params=_params(("parallel",)),
    )(*dks, *dvs)


def _ffn_fwd(hf, weight, layer, tag):
    a, gu = _ffn_in_act(hf, weight(f"w_in{layer}", hf), f"{tag}_in")
    o = _mm_nn(a, weight(f"w_out{layer}", a), F32, f"{tag}_out", tn=1024, tk=5632)
    return gu, a, o


def _ffn_bwd(dresb, hf, gu, a, weight, emit, layer, tag):
    w_in, w_out = weight(f"w_in{layer}", None), weight(f"w_out{layer}", None)
    zero = emit(f"w_out{layer}", _mm_tn(a, dresb, 1, f"{tag}_out_dw", tn=1024))
    dgu = _ffn_out_dx_act(dresb, w_out, gu, f"{tag}_out_dx")
    zero = zero + emit(f"w_in{layer}",
                       _mm_tn(hf, dgu, w_in.shape[0], f"{tag}_in_dw", tn=FFN_TILE, paired=True))
    return _mm_nt(dgu, w_in, f"{tag}_in_dx", tr=FFN_TILE, tkc=512, paired=True), zero


def _local_step(x, target, small, weight, emit, stage):
    s, d = x.shape
    nh = d // HEAD_DIM
    n_groups = d // S5_GROUP_CH
    nt = n_groups // GROUPS_PER_TILE
    p_, c_ = S5_STATE, S5_GROUP_CH

    disc_in = (small["lam_re"], small["lam_im"], small["log_dt"], small["b_re"], small["b_im"])
    (lb_re, lb_im, bb_re, bb_im), disc_vjp = jax.vjp(_s5_discretize, *disc_in)
    del lb_re, lb_im
    dt = jnp.exp(small["log_dt"])[:, None]
    def pole_powers(exponents):
        k = exponents[:, None, None]
        mag = jnp.exp(k * (small["lam_re"] * dt)[None])
        ang = k * (small["lam_im"] * dt)[None]
        return ((mag * jnp.cos(ang)).reshape(SUB, n_groups * p_),
                (mag * jnp.sin(ang)).reshape(SUB, n_groups * p_))

    pw_re, pw_im = pole_powers(jnp.arange(1, SUB + 1, dtype=F32))
    pf_re, pf_im = pole_powers(jnp.arange(SUB, 0, -1, dtype=F32))
    bbd_re = _block_diag(bb_re.transpose(0, 2, 1).reshape(nt, GROUPS_PER_TILE, c_, p_)).astype(BF16)
    bbd_im = _block_diag(bb_im.transpose(0, 2, 1).reshape(nt, GROUPS_PER_TILE, c_, p_)).astype(BF16)
    cbd_re = _block_diag(small["c_re"].transpose(0, 2, 1).reshape(nt, GROUPS_PER_TILE, p_, c_)).astype(BF16)
    cbd_im = _block_diag(small["c_im"].transpose(0, 2, 1).reshape(nt, GROUPS_PER_TILE, p_, c_)).astype(BF16)

    h0 = _norm_f32(x, small["a_norm"], "s5_norm")
    xs_re, xs_im, y0, z = _s5_fwd(h0, bbd_re, bbd_im, cbd_re, cbd_im, pw_re, pw_im, small["s5_d"], "s5_fwd")
    zero = stage("glu", z)
    vg = _mm_nn(z, weight("w_glu", z), F32, "glu_mm", tn=1024)
    x1, hf0 = _glu_res_norm(vg, x, small["ffn_norm0"] + zero, "glu_res_norm")
    gu0, a0, o0 = _ffn_fwd(hf0, weight, 0, "ffn0")
    zero = stage("attention", a0)
    x2 = _to_slabs([o0, x1], "to_slabs")
    kvn, h1 = _norm_bf16(x2, [small["kv_norm"] + zero, small["b_norm"]], "att_norms")
    kv = _mm_nn(kvn, weight("w_kv", kvn), BF16, "kv_mm", tn=1024)
    q = _mm_nn(h1, weight("w_q", kv), BF16, "q_mm", tn=1536)
    outs, lses_narrow, lses = [], [], []
    for gi, (window, dil) in enumerate(PATTERNS):
        assert window // dil == ATT_BLK
        og, lg, lg_wide = _att_fwd(q, kv, gi, dil, nh, f"att_fwd{gi}")
        outs.append(og)
        lses_narrow.append(lg)
        lses.append(lg_wide)
    oatt, oattb = _att_combine(outs, lses_narrow, "att_combine")
    ao = _mm_nn(oattb, weight("w_o", oattb), F32, "o_mm", tn=1024)
    x3, hf1 = _res_norm(ao, x2, [small["ffn_norm1"] + stage("ffn1", oattb)], "att_res_norm")
    gu1, a1, o1 = _ffn_fwd(hf1, weight, 1, "ffn1")
    dres, dresb, loss_rows, d_final = _loss_head(
        o1, x3, small["final_norm"], _to_slabs([target], "target_to_slabs"), "loss_head")

    dhf1, zero = _ffn_bwd(dresb, hf1, gu1, a1, weight, emit, 1, "ffn1")
    dres, dresb, d_ffn_norm1 = _norm_bwd(x3, [small["ffn_norm1"] + zero], [dhf1], dres, "ffn1_norm_bwd")
    zero = emit("w_o", _mm_tn(oattb, dresb, 1, "o_dw", tn=1024))
    doatt = _mm_nt(dresb, weight("w_o", None), "o_dx", tr=2048)
    dks, dvs = [], []
    dq = lax.empty(q.shape, BF16)
    for gi, (window, dil) in enumerate(PATTERNS):
        dq, dk_g, dv_g = _att_bwd(q, kv, doatt, oatt, lses, dq, gi, dil, nh, f"att_bwd{gi}")
        dks.append(dk_g)
        dvs.append(dv_g)
    dkv = _dkv_sum(dks, dvs, "dkv_sum")
    w_q = weight("w_q", None)
    zero = zero + emit("w_q", _mm_tn(h1, dq, w_q.shape[0], "q_dw", tn=1536))
    zero = zero + emit("w_kv", _mm_tn(kvn, dkv, 1, "kv_dw", tn=1024))
    dh1 = _mm_nt(dq, w_q, "q_dx", tr=1536)
    dkvn = _mm_nt(dkv, weight("w_kv", None), "kv_dx", tr=1024)
    dres, dresb, d_b_norm, d_kv_norm = _norm_bwd(
        x2, [small["b_norm"] + zero, small["kv_norm"]], [dh1, dkvn], dres, "att_norm_bwd")
    dres, dresb = _from_slabs(dres, "from_slabs")
    dhf0, zero = _ffn_bwd(dresb, hf0, gu0, a0, weight, emit, 0, "ffn0")
    dres, dresb, d_ffn_norm0 = _norm_bwd(x1, [small["ffn_norm0"] + zero], [dhf0], dres, "ffn0_norm_bwd")
    del dresb
    dvg = _glu_bwd(dres, vg, "glu_bwd")
    w_glu = weight("w_glu", None)
    zero = emit("w_glu", _mm_tn(z, dvg, w_glu.shape[0], "glu_dw", tn=1024))
    dz = _mm_nt(dvg, w_glu, "glu_dx", tr=1024)
    dh0, d_s5_d, dcr, dci_neg, dbr, dbi, dar, dai = _s5_bwd(
        dz, y0, h0, xs_re, xs_im, bbd_re, bbd_im, cbd_re, cbd_im, pw_re, pw_im, pf_re, pf_im,
        small["s5_d"] + zero, "s5_bwd")
    grad_x, _, d_a_norm = _norm_bwd(x, [small["a_norm"]], [dh0], dres, "s5_norm_bwd")

    d_bb_re = _block_diag_take(dbr, c_, p_).transpose(0, 2, 1)
    d_bb_im = _block_diag_take(dbi, c_, p_).transpose(0, 2, 1)
    d_c_re = _block_diag_take(dcr, c_, p_)
    d_c_im = -_block_diag_take(dci_neg, c_, p_)
    d_lam_re, d_lam_im, d_log_dt, d_b_re, d_b_im = disc_vjp(
        (dar.reshape(n_groups, p_), dai.reshape(n_groups, p_), d_bb_re, d_bb_im))

    d_small = dict(lam_re=d_lam_re, lam_im=d_lam_im, log_dt=d_log_dt, b_re=d_b_re, b_im=d_b_im,
                   c_re=d_c_re, c_im=d_c_im, s5_d=d_s5_d, a_norm=d_a_norm, ffn_norm0=d_ffn_norm0,
                   ffn_norm1=d_ffn_norm1, b_norm=d_b_norm, kv_norm=d_kv_norm, final_norm=d_final)
    return loss_rows, grad_x, d_small


def _place():
    x, y, c = lax.axis_index("x"), lax.axis_index("y"), lax.axis_index("c")
    return x, y, c, [(1 - x, y), (x, 1 - y), (1 - x, 1 - y)]


_ANY = pl.BlockSpec(memory_space=pl.ANY)


_HBM = pl.BlockSpec(memory_space=pltpu.HBM)
_SEM = pl.BlockSpec(memory_space=pltpu.SEMAPHORE)
_EFFECT = pltpu.SideEffectType.DATAFLOW_SIDE_EFFECTING


def _in_hbm(a):
    return pltpu.with_memory_space_constraint(a, pltpu.HBM)


def _cast_place(shards, layer, chip, name):
    _, r, c = shards.shape
    tm = _tile(r, ROWS, 16)

    def kern(chip_ref, x_ref, o_ref):
        del chip_ref
        o_ref[...] = x_ref[...].astype(BF16)

    return pl.pallas_call(
        kern, name=name,
        grid_spec=pltpu.PrefetchScalarGridSpec(
            num_scalar_prefetch=1, grid=(r // tm,),
            in_specs=[pl.BlockSpec((None, tm, c), lambda i, ch: (layer, i, 0))],
            out_specs=pl.BlockSpec((None, tm, c), lambda i, ch: (ch[0], i, 0))),
        out_shape=jax.ShapeDtypeStruct((N_CHIPS, r, c), BF16),
        compiler_params=_params(("parallel",)),
    )(chip, shards)


def _my_part(land, block, c, halves):
    if not halves:
        return land.at[block]
    half = land.shape[1] // 2
    return land.at[block, pl.ds(c * half, half)]


def _gather_start(lands, name, halves, after):
    n = len(lands)

    def body(*refs):
        land = refs[:n]
        send, recv = refs[n + 1:2 * n + 1], refs[2 * n + 1:3 * n + 1]
        token = refs[4 * n + 1]
        x, y, c, peers = _place()
        me = 2 * x + y
        for a in range(n):
            for k, (px, py) in enumerate(peers):
                part = _my_part(land[a], me, c, halves)
                pltpu.make_async_remote_copy(
                    src_ref=part, dst_ref=part, send_sem=send[a].at[k], recv_sem=recv[a].at[k],
                    device_id=(px, py, c), device_id_type=MESH).start()
        token[...] = jnp.zeros_like(token)

    outs = pl.pallas_call(
        body, name=name,
        out_shape=[pltpu.SemaphoreType.DMA((3,))] * (2 * n) + [pltpu.HBM(a.shape, a.dtype) for a in lands]
        + [jax.ShapeDtypeStruct((8, 128), F32)],
        in_specs=[_HBM] * n + [_ANY],
        out_specs=[_SEM] * (2 * n) + [_HBM] * n + [pl.BlockSpec(memory_space=pltpu.VMEM)],
        input_output_aliases={i: 2 * n + i for i in range(n)},
        compiler_params=pltpu.CompilerParams(has_side_effects=_EFFECT),
    )(*[_in_hbm(a) for a in lands], after)
    return outs[:n], outs[n:2 * n], outs[2 * n:3 * n], outs[3 * n]


def _gather_wait(lands, sends, recvs, after, name, halves):
    n = len(lands)

    def body(*refs):
        land, send, recv = refs[:n], refs[n:2 * n], refs[2 * n:3 * n]
        x, y, c, peers = _place()
        me = 2 * x + y
        for a in range(n):
            for k, (px, py) in enumerate(peers):
                cp = pltpu.make_async_remote_copy(
                    src_ref=_my_part(land[a], me, c, halves), dst_ref=_my_part(land[a], 2 * px + py, c, halves),
                    send_sem=send[a].at[k], recv_sem=recv[a].at[k], device_id=(px, py, c), device_id_type=MESH)
                cp.wait_send()
                cp.wait_recv()

    return pl.pallas_call(
        body, name=name,
        out_shape=[pltpu.HBM(a.shape, a.dtype) for a in lands],
        in_specs=[_HBM] * n + [_SEM] * (2 * n) + [_ANY], out_specs=[_HBM] * n,
        input_output_aliases={i: i for i in range(n)},
        compiler_params=pltpu.CompilerParams(has_side_effects=_EFFECT),
    )(*lands, *sends, *recvs, after)


def _forward_start(lands, name):
    n = len(lands)

    def body(*refs):
        land = refs[:n]
        send, recv = refs[n:2 * n], refs[2 * n:3 * n]
        token = refs[4 * n]
        x, y, c, peers = _place()
        for a in range(n):
            for k, (px, py) in enumerate(peers):
                part = _my_part(land[a], 2 * px + py, c, True)
                pltpu.make_async_remote_copy(
                    src_ref=part, dst_ref=part, send_sem=send[a].at[k], recv_sem=recv[a].at[k],
                    device_id=(x, y, 1 - c), device_id_type=MESH).start()
        token[...] = jnp.zeros_like(token)

    outs = pl.pallas_call(
        body, name=name,
        out_shape=[pltpu.SemaphoreType.DMA((3,))] * (2 * n) + [pltpu.HBM(a.shape, a.dtype) for a in lands]
        + [jax.ShapeDtypeStruct((8, 128), F32)],
        in_specs=[_HBM] * n,
        out_specs=[_SEM] * (2 * n) + [_HBM] * n + [pl.BlockSpec(memory_space=pltpu.VMEM)],
        input_output_aliases={i: 2 * n + i for i in range(n)},
        compiler_params=pltpu.CompilerParams(has_side_effects=_EFFECT),
    )(*[_in_hbm(a) for a in lands])
    return outs[:n], outs[n:2 * n], outs[2 * n:3 * n], outs[3 * n]


def _forward_wait(land, send, recv, after, name):
    def body(land_ref, send_sem, recv_sem, after_ref, land_out):
        del after_ref, land_out
        x, y, c, peers = _place()
        for k, (px, py) in enumerate(peers):
            cp = pltpu.make_async_remote_copy(
                src_ref=_my_part(land_ref, 2 * px + py, c, True), dst_ref=_my_part(land_ref, 2 * px + py, 1 - c, True),
                send_sem=send_sem.at[k], recv_sem=recv_sem.at[k], device_id=(x, y, 1 - c), device_id_type=MESH)
            cp.wait_send()
            cp.wait_recv()

    return pl.pallas_call(
        body, name=name,
        out_shape=pltpu.HBM(land.shape, land.dtype),
        in_specs=[_HBM, _SEM, _SEM, _ANY], out_specs=_HBM,
        input_output_aliases={0: 0},
        compiler_params=pltpu.CompilerParams(has_side_effects=_EFFECT),
    )(land, send, recv, after)


def _scatter_start(g, name):
    def body(g_ref, land_ref, send, recv, g_out, land_out, token):
        del g_out, land_out
        x, y, c, peers = _place()
        for k, (px, py) in enumerate(peers):
            pltpu.make_async_remote_copy(
                src_ref=g_ref.at[2 * px + py], dst_ref=land_ref.at[k], send_sem=send.at[k], recv_sem=recv.at[k],
                device_id=(px, py, c), device_id_type=MESH).start()
        token[...] = jnp.zeros_like(token)

    land = lax.empty((3,) + g.shape[1:], g.dtype)
    return pl.pallas_call(
        body, name=name,
        out_shape=(pltpu.SemaphoreType.DMA((3,)), pltpu.SemaphoreType.DMA((3,)),
                   pltpu.HBM(g.shape, g.dtype), pltpu.HBM(land.shape, land.dtype),
                   jax.ShapeDtypeStruct((8, 128), F32)),
        in_specs=(_HBM, _HBM), out_specs=(_SEM, _SEM, _HBM, _HBM, pl.BlockSpec(memory_space=pltpu.VMEM)),
        input_output_aliases={0: 2, 1: 3},
        compiler_params=pltpu.CompilerParams(has_side_effects=_EFFECT),
    )(_in_hbm(g), _in_hbm(land))


def _scatter_wait(started, after):
    n = len(started)

    def body(*refs):
        gs, lands = refs[:n], refs[n:2 * n]
        sends, recvs = refs[2 * n:3 * n], refs[3 * n:4 * n]
        x, y, c, peers = _place()
        for a in range(n):
            for k, (px, py) in enumerate(peers):
                cp = pltpu.make_async_remote_copy(
                    src_ref=gs[a].at[2 * px + py], dst_ref=lands[a].at[k], send_sem=sends[a].at[k],
                    recv_sem=recvs[a].at[k], device_id=(px, py, c), device_id_type=MESH)
                cp.wait_send()
                cp.wait_recv()

    gs = [s[2] for s in started]
    lands = [s[3] for s in started]
    outs = pl.pallas_call(
        body, name="scatter_wait",
        out_shape=[pltpu.HBM(a.shape, a.dtype) for a in gs + lands],
        in_specs=[_HBM] * (2 * n) + [_SEM] * (2 * n) + [_ANY], out_specs=[_HBM] * (2 * n),
        input_output_aliases={i: i for i in range(2 * n)},
        compiler_params=pltpu.CompilerParams(has_side_effects=_EFFECT),
    )(*gs, *lands, *[s[0] for s in started], *[s[1] for s in started], after)
    return outs[:n], outs[n:]


def _sibling():
    return lax.axis_index("x"), lax.axis_index("y"), 1 - lax.axis_index("c")


def _swap_start(parts):
    n = len(parts)

    def body(*refs):
        ins, land = refs[:n], refs[n:2 * n]
        send, recv = refs[2 * n], refs[2 * n + 1]
        token = refs[4 * n + 2]
        for a in range(n):
            pltpu.make_async_remote_copy(
                src_ref=ins[a], dst_ref=land[a], send_sem=send.at[a], recv_sem=recv.at[a],
                device_id=_sibling(), device_id_type=MESH).start()
        token[...] = jnp.zeros_like(token)

    lands = [lax.empty(a.shape, a.dtype) for a in parts]
    outs = pl.pallas_call(
        body, name="swap_start",
        out_shape=[pltpu.SemaphoreType.DMA((n,))] * 2 + [pltpu.HBM(a.shape, a.dtype) for a in parts + lands]
        + [jax.ShapeDtypeStruct((8, 128), F32)],
        in_specs=[_HBM] * (2 * n),
        out_specs=[_SEM] * 2 + [_HBM] * (2 * n) + [pl.BlockSpec(memory_space=pltpu.VMEM)],
        input_output_aliases={i: 2 + i for i in range(2 * n)},
        compiler_params=pltpu.CompilerParams(has_side_effects=_EFFECT),
    )(*[_in_hbm(a) for a in parts + lands])
    return outs[0], outs[1], outs[2:2 + n], outs[2 + n:2 + 2 * n], outs[2 + 2 * n]


def _swap_wait(parts, lands, send, recv, after):
    n = len(parts)

    def body(*refs):
        ins, land = refs[:n], refs[n:2 * n]
        send_sem, recv_sem = refs[2 * n], refs[2 * n + 1]
        for a in range(n):
            cp = pltpu.make_async_remote_copy(
                src_ref=ins[a], dst_ref=land[a], send_sem=send_sem.at[a], recv_sem=recv_sem.at[a],
                device_id=_sibling(), device_id_type=MESH)
            cp.wait_send()
            cp.wait_recv()

    outs = pl.pallas_call(
        body, name="swap_wait",
        out_shape=[pltpu.HBM(a.shape, a.dtype) for a in list(parts) + list(lands)],
        in_specs=[_HBM] * (2 * n) + [_SEM] * 2 + [_ANY], out_specs=[_HBM] * (2 * n),
        input_output_aliases={i: i for i in range(2 * n)},
        compiler_params=pltpu.CompilerParams(has_side_effects=_EFFECT),
    )(*parts, *lands, send, recv, after)
    return outs[:n], outs[n:]


def _all_reduce_small(v):
    nd, r, w = v.shape
    assert nd == N_DEV

    def body(v_ref, out_ref, land_ref, red_ref, send1, recv1, send2, recv2):
        x, y, c = lax.axis_index("x"), lax.axis_index("y"), lax.axis_index("c")
        me = 4 * x + 2 * y + c
        peers = []
        for k in range(1, N_DEV):
            kx, ky, kc = (k >> 2) & 1, (k >> 1) & 1, k & 1
            peers.append((1 - x if kx else x, 1 - y if ky else y, 1 - c if kc else c))
        first = []
        for k, (px, py, pc) in enumerate(peers):
            cp = pltpu.make_async_remote_copy(
                src_ref=v_ref.at[4 * px + 2 * py + pc], dst_ref=land_ref.at[me], send_sem=send1.at[k],
                recv_sem=recv1.at[k], device_id=(px, py, pc), device_id_type=MESH)
            cp.start()
            first.append(cp)
        land_ref[me] = v_ref[me]
        for cp in first:
            cp.wait()
        acc = land_ref[0]
        for j in range(1, N_DEV):
            acc = acc + land_ref[j]
        red_ref[...] = acc
        second = []
        for k, (px, py, pc) in enumerate(peers):
            cp = pltpu.make_async_remote_copy(
                src_ref=red_ref, dst_ref=out_ref.at[me], send_sem=send2.at[k],
                recv_sem=recv2.at[k], device_id=(px, py, pc), device_id_type=MESH)
            cp.start()
            second.append(cp)
        out_ref[me] = acc
        for cp in second:
            cp.wait()

    vmem = pl.BlockSpec(memory_space=pltpu.VMEM)
    return pl.pallas_call(
        body, name="all_reduce_small",
        in_specs=[vmem], out_specs=vmem,
        out_shape=jax.ShapeDtypeStruct((nd, r, w), F32),
        scratch_shapes=[pltpu.VMEM((nd, r, w), F32), pltpu.VMEM((r, w), F32)]
        + [pltpu.SemaphoreType.DMA((N_DEV - 1,))] * 4,
        compiler_params=pltpu.CompilerParams(vmem_limit_bytes=VMEM_LIMIT),
    )(v)


def _adam_math(w, g, m, v):
    m = ADAM_B1 * m + (1.0 - ADAM_B1) * g
    v = ADAM_B2 * v + (1.0 - ADAM_B2) * (g * g)
    m_hat = m / (1.0 - ADAM_B1 ** ADAM_STEP)
    v_hat = v / (1.0 - ADAM_B2 ** ADAM_STEP)
    delta = -ADAM_LR * (m_hat / (jnp.sqrt(v_hat) + ADAM_EPS) + ADAM_WD * w)
    return delta, m, v


def _sum_blocks(own, got, chip, name):
    _, r, c = own.shape
    tm = _tile(r, ROWS, 16)

    def kern(chip_ref, own_ref, got_ref, o_ref):
        del chip_ref
        acc = own_ref[...].astype(F32)
        for k in range(3):
            acc = acc + got_ref[k].astype(F32)
        o_ref[...] = acc

    return pl.pallas_call(
        kern, name=name,
        grid_spec=pltpu.PrefetchScalarGridSpec(
            num_scalar_prefetch=1, grid=(r // tm,),
            in_specs=[pl.BlockSpec((None, tm, c), lambda i, ch: (ch[0], i, 0)),
                      pl.BlockSpec((3, tm, c), lambda i, ch: (0, i, 0))],
            out_specs=pl.BlockSpec((tm, c), lambda i, ch: (i, 0))),
        out_shape=jax.ShapeDtypeStruct((r, c), F32),
        compiler_params=_params(("parallel",)),
    )(chip, own, got)


def _adamw(parts, w, m, v, name):
    nl, r, c = w.shape
    assert len(parts) == nl
    tm = _tile(r, 128, 8)

    def kern(*refs):
        p_refs = refs[:2 * nl]
        w_ref, m_ref, v_ref, g_ref, d_ref, mo_ref, vo_ref = refs[2 * nl:]
        layer = pl.program_id(0)
        g = p_refs[0][...] + p_refs[1][...]
        for ll in range(1, nl):
            g = jnp.where(layer == ll, p_refs[2 * ll][...] + p_refs[2 * ll + 1][...], g)
        g_ref[...] = g
        d_ref[...], mo_ref[...], vo_ref[...] = _adam_math(w_ref[...], g, m_ref[...], v_ref[...])

    def part_spec(ll):
        return pl.BlockSpec((tm, c), lambda l, i: (jnp.where(l == ll, i, 0), 0))

    spec = pl.BlockSpec((None, tm, c), lambda l, i: (l, i, 0))
    return pl.pallas_call(
        kern, name=name, grid=(nl, r // tm),
        in_specs=[part_spec(ll) for ll in range(nl) for _ in range(2)] + [spec] * 3, out_specs=[spec] * 4,
        out_shape=[jax.ShapeDtypeStruct((nl, r, c), F32)] * 4,
        compiler_params=_params(("arbitrary", "parallel")),
    )(*[p for pair in parts for p in pair], w, m, v)


def _adamw_small(g, w, m, v, name):
    def kern(g_ref, w_ref, m_ref, v_ref, d_ref, mo_ref, vo_ref):
        d_ref[...], mo_ref[...], vo_ref[...] = _adam_math(w_ref[...], g_ref[...], m_ref[...], v_ref[...])

    return pl.pallas_call(
        kern, name=name,
        out_shape=[jax.ShapeDtypeStruct(g.shape, F32)] * 3,
        compiler_params=pltpu.CompilerParams(vmem_limit_bytes=VMEM_LIMIT),
    )(g, w, m, v)


def _pack(arrays, rows):
    flat = jnp.concatenate([a.reshape(-1).astype(F32) for a in arrays])
    return jnp.pad(flat, (0, rows * 128 - flat.shape[0])).reshape(rows, 128)


def _unpack(packed, shapes):
    flat = packed.reshape(-1)
    out, off = [], 0
    for shp in shapes:
        size = math.prod(shp)
        out.append(flat[off:off + size].reshape(shp))
        off += size
    return out


_REPLICATED = ["s5_lam_re", "s5_lam_im", "s5_log_dt", "s5_b_re", "s5_b_im", "s5_c_re", "s5_c_im",
               "ffn_norm", "b_norm_mix", "kv_norm", "final_norm"]
_CHIP_VECTORS = ["s5_d", "a_norm_mix"]
_BIG = ["s5_w_glu", "ffn_w_in", "ffn_w_out", "attn_w_q", "attn_w_o", "w_kv"]
_WEIGHT_ORDER = ["s5_lam_re", "s5_lam_im", "s5_log_dt", "s5_b_re", "s5_b_im", "s5_c_re", "s5_c_im", "s5_d",
                 "s5_w_glu", "a_norm_mix", "ffn_norm", "ffn_w_in", "ffn_w_out", "b_norm_mix", "attn_w_q",
                 "attn_w_o", "kv_norm", "w_kv", "final_norm"]


def _step(x, loss_target, w, m, v):
    s, d = x.shape[1], x.shape[2]
    chip = 2 * lax.axis_index("x") + lax.axis_index("y")

    col_sharded = dict(w_glu=("s5_w_glu", 0), w_in0=("ffn_w_in", 0), w_in1=("ffn_w_in", 1), w_q=("attn_w_q", 0))
    row_sharded = dict(w_out0=("ffn_w_out", 0), w_out1=("ffn_w_out", 1), w_o=("attn_w_o", 0), w_kv=("w_kv", 0))
    names = ["w_glu", "w_in0", "w_out0", "w_kv", "w_q", "w_o", "w_in1", "w_out1"]
    local = {**col_sharded, **row_sharded}

    def layers(a):
        return a.reshape((-1,) + a.shape[-2:])

    chip_arr = jnp.reshape(chip, (1,)).astype(jnp.int32)
    vector_lands = [lax.dynamic_update_slice(jnp.zeros((N_CHIPS,) + w[n].shape, F32), w[n][None], (chip, 0, 0))
                    for n in _CHIP_VECTORS]
    v_send, v_recv, v_land, v_token = _gather_start(vector_lands, "gather_start_vectors", False,
                                                    jnp.zeros((8, 128), F32))
    first = 2
    lands = [_cast_place(layers(w[local[n][0]]), local[n][1], chip_arr, f"cast_{n}") for n in names[:first]]
    send, recv, land_thru, first_token = _gather_start(lands, "gather_start_first", True, v_token)
    lands = [_cast_place(layers(w[local[n][0]]), local[n][1], chip_arr, f"cast_{n}") for n in names[first:]]
    *later, rest_token = _gather_start(lands, "gather_start_rest", True, first_token)
    send, recv, land_thru = [list(a) + list(b) for a, b in zip((send, recv, land_thru), later)]
    vectors = _gather_wait(v_land, v_send, v_recv, jnp.zeros((8, 128), F32), "gather_wait_vectors", False)
    s5_d_full = vectors[0].reshape(1, d)
    a_norm_full = vectors[1].reshape(1, d) + rest_token[0, 0]

    batches = dict(glu=["w_glu", "w_in0", "w_out0"], attention=["w_kv", "w_q", "w_o"], ffn1=["w_in1", "w_out1"])
    forwarded, arrived = {}, {}

    def stage(batch, after):
        idx = [names.index(n) for n in batches[batch]]
        got = _gather_wait([land_thru[i] for i in idx], [send[i] for i in idx], [recv[i] for i in idx], after,
                           f"gather_wait_{batch}", True)
        f_send, f_recv, f_land, token = _forward_start(got, f"forward_start_{batch}")
        forwarded.update(zip(batches[batch], zip(f_send, f_recv, f_land)))
        return token[0, 0]

    def weight(name, after):
        if name not in arrived:
            f_send, f_recv, f_land = forwarded[name]
            land = _forward_wait(f_land, f_send, f_recv, after, f"forward_wait_{name}")
            arrived[name] = land if name in col_sharded else land.reshape(1, -1, land.shape[-1])
        return arrived[name]

    started = {}

    def emit(name, dw):
        outs = _scatter_start(dw.reshape(N_CHIPS, -1, dw.shape[-1]), f"scatter_start_{name}")
        started[name] = outs[:4]
        return outs[4][0, 0]

    small = dict(lam_re=w["s5_lam_re"][0], lam_im=w["s5_lam_im"][0], log_dt=w["s5_log_dt"][0],
                 b_re=w["s5_b_re"][0], b_im=w["s5_b_im"][0], c_re=w["s5_c_re"][0], c_im=w["s5_c_im"][0],
                 s5_d=s5_d_full, a_norm=a_norm_full, ffn_norm0=w["ffn_norm"][0:1], ffn_norm1=w["ffn_norm"][1:2],
                 b_norm=w["b_norm_mix"], kv_norm=w["kv_norm"].reshape(1, d), final_norm=w["final_norm"].reshape(1, d))

    loss_rows, grad_x, d_small = _local_step(x[0], loss_target[0], small, weight, emit, stage)

    g4, got = _scatter_wait([started[n] for n in names], grad_x)
    partial = [_sum_blocks(o, r, chip_arr, f"sum_{n}") for n, o, r in zip(names, g4, got)]
    swap_send, swap_recv, partial, swap_land, swap_token = _swap_start(partial)
    result = {}

    rep_grads = [d_small["lam_re"], d_small["lam_im"], d_small["log_dt"], d_small["b_re"], d_small["b_im"],
                 d_small["c_re"], d_small["c_im"],
                 jnp.concatenate([d_small["ffn_norm0"], d_small["ffn_norm1"]], axis=0),
                 d_small["b_norm"], d_small["kv_norm"], d_small["final_norm"]]
    vec_grads = [d_small["s5_d"], d_small["a_norm"]]
    to_reduce = rep_grads + vec_grads + [jnp.sum(loss_rows).reshape(1) + swap_token[0, 0]]
    total = sum(math.prod(a.shape) for a in to_reduce)
    rows_per = -(-total // (N_DEV * 128 * 8)) * 8
    reduced = _all_reduce_small(_pack(to_reduce, N_DEV * rows_per).reshape(N_DEV, rows_per, 128))
    red = _unpack(reduced, [a.shape for a in to_reduce])
    loss = red[-1][0]
    g_small = dict(zip(_REPLICATED, [r.reshape(w[n].shape) for r, n in zip(red[:len(rep_grads)], _REPLICATED)]))
    for n, r in zip(_CHIP_VECTORS, red[len(rep_grads):-1]):
        g_small[n] = lax.dynamic_slice_in_dim(r.reshape(1, d), chip * (d // N_CHIPS), d // N_CHIPS, axis=1)
    small_names = _REPLICATED + _CHIP_VECTORS
    n_small = sum(math.prod(w[n].shape) for n in small_names)
    rows_small = -(-n_small // (128 * 8)) * 8
    packed = [_pack([src[n] for n in small_names], rows_small) for src in (g_small, w, m, v)]
    upd = _adamw_small(*packed, "adamw_small")
    shapes = [w[n].shape for n in small_names]
    for n, dl, mo, vo in zip(small_names, *[_unpack(u, shapes) for u in upd]):
        result[n] = [g_small[n], dl, mo, vo]

    partial, other = _swap_wait(partial, swap_land, swap_send, swap_recv, upd[0])
    part = dict(zip(names, zip(partial, other)))
    for name in _BIG:
        parts = [part[n] for n in sorted((n for n in names if local[n][0] == name), key=lambda n: local[n][1])]
        outs = _adamw(parts, layers(w[name]), layers(m[name]), layers(v[name]), f"adamw_{name}")
        result[name] = [o.reshape(w[name].shape) for o in outs]

    cols = [[result[n][t] for n in _WEIGHT_ORDER] for t in range(4)]
    return (loss, grad_x.reshape(x.shape), *cols[0], *cols[1], *cols[2], *cols[3])


def kernel(x, s5_lam_re, s5_lam_im, s5_log_dt, s5_b_re, s5_b_im, s5_c_re, s5_c_im, s5_d, s5_w_glu, a_norm_mix, ffn_norm, ffn_w_in, ffn_w_out, b_norm_mix, attn_w_q, attn_w_o, kv_norm, w_kv, final_norm, loss_target, m_s5_lam_re, m_s5_lam_im, m_s5_log_dt, m_s5_b_re, m_s5_b_im, m_s5_c_re, m_s5_c_im, m_s5_d, m_s5_w_glu, m_a_norm_mix, m_ffn_norm, m_ffn_w_in, m_ffn_w_out, m_b_norm_mix, m_attn_w_q, m_attn_w_o, m_kv_norm, m_w_kv, m_final_norm, v_s5_lam_re, v_s5_lam_im, v_s5_log_dt, v_s5_b_re, v_s5_b_im, v_s5_c_re, v_s5_c_im, v_s5_d, v_s5_w_glu, v_a_norm_mix, v_ffn_norm, v_ffn_w_in, v_ffn_w_out, v_b_norm_mix, v_attn_w_q, v_attn_w_o, v_kv_norm, v_w_kv, v_final_norm):
    w = dict(zip(_WEIGHT_ORDER, (s5_lam_re, s5_lam_im, s5_log_dt, s5_b_re, s5_b_im, s5_c_re, s5_c_im, s5_d, s5_w_glu, a_norm_mix, ffn_norm, ffn_w_in, ffn_w_out, b_norm_mix, attn_w_q, attn_w_o, kv_norm, w_kv, final_norm)))
    m = dict(zip(_WEIGHT_ORDER, (m_s5_lam_re, m_s5_lam_im, m_s5_log_dt, m_s5_b_re, m_s5_b_im, m_s5_c_re, m_s5_c_im, m_s5_d, m_s5_w_glu, m_a_norm_mix, m_ffn_norm, m_ffn_w_in, m_ffn_w_out, m_b_norm_mix, m_attn_w_q, m_attn_w_o, m_kv_norm, m_w_kv, m_final_norm)))
    v = dict(zip(_WEIGHT_ORDER, (v_s5_lam_re, v_s5_lam_im, v_s5_log_dt, v_s5_b_re, v_s5_b_im, v_s5_c_re, v_s5_c_im, v_s5_d, v_s5_w_glu, v_a_norm_mix, v_ffn_norm, v_ffn_w_in, v_ffn_w_out, v_b_norm_mix, v_attn_w_q, v_attn_w_o, v_kv_norm, v_w_kv, v_final_norm)))
    return _step(x, loss_target, w, m, v)
```

```python
import functools
import math

import jax
import jax.numpy as jnp
from jax import lax
from jax.experimental import pallas as pl
from jax.experimental.pallas import tpu as pltpu

F32 = jnp.float32
BF16 = jnp.bfloat16

S5_GROUP_CH = 16
S5_STATE = 64
GROUPS_PER_TILE = 8
HEAD_DIM = 128
N_KV_HEADS = 4
PATTERNS = ((128, 1), (512, 4), (2048, 16))
ATT_BLK = 128
EPS = 1e-6
NEG_INF = -1e30
SCAN_T = 1024
ADAM_LR = 0.001
ADAM_B1 = 0.9
ADAM_B2 = 0.999
ADAM_EPS = 1e-08
ADAM_WD = 0.01
ADAM_STEP = 10
N_CHIPS = 4
N_DEV = 8
VMEM_LIMIT = 56 * 1024 * 1024
MESH = pl.DeviceIdType.MESH
GELU_K = math.sqrt(2.0 / math.pi)
GELU_C = 0.044715


def _tile(n, pref, unit=128):
    if n <= pref:
        return n
    best = None
    t = unit
    while t <= pref:
        if n % t == 0:
            best = t
        t += unit
    assert best is not None, (n, pref, unit)
    return best


def _params(sem):
    return pltpu.CompilerParams(dimension_semantics=sem, vmem_limit_bytes=VMEM_LIMIT)


def _dot(a, b):
    return jnp.dot(a, b, preferred_element_type=F32)


def _dot_nt(a, b):
    return lax.dot_general(a, b, (((1,), (1,)), ((), ())), preferred_element_type=F32)


def _dot_tn(a, b):
    return lax.dot_general(a, b, (((0,), (0,)), ((), ())), preferred_element_type=F32)


def _mm_nn(a, w, out_dtype, name, tm=512, tn=1536, tk=2048):
    m, k = a.shape
    nb, k2, nq = w.shape
    assert k == k2
    tm, tn, tk = _tile(m, tm, 8), _tile(nq, tn), _tile(k, tk)
    per, nk = nq // tn, k // tk

    def kern(a_ref, w_ref, o_ref, *acc):
        p = _dot(a_ref[...], w_ref[...])
        if nk == 1:
            o_ref[...] = p.astype(o_ref.dtype)
        else:
            acc_ref, = acc
            kk = pl.program_id(2)

            @pl.when(kk == 0)
            def _():
                acc_ref[...] = p

            @pl.when(kk > 0)
            def _():
                acc_ref[...] += p

            @pl.when(kk == nk - 1)
            def _():
                o_ref[...] = acc_ref[...].astype(o_ref.dtype)

    return pl.pallas_call(
        kern, name=name, grid=(nb * per, m // tm, nk),
        in_specs=[pl.BlockSpec((tm, tk), lambda j, i, kk: (i, kk)),
                  pl.BlockSpec((None, tk, tn), lambda j, i, kk: (j // per, kk, j % per))],
        out_specs=pl.BlockSpec((tm, tn), lambda j, i, kk: (i, j)),
        out_shape=jax.ShapeDtypeStruct((m, nb * nq), out_dtype),
        scratch_shapes=[] if nk == 1 else [pltpu.VMEM((tm, tn), F32)],
        compiler_params=_params(("parallel", "parallel", "arbitrary")),
    )(a, w)


def _paired_block(r, per, nb):
    j = r // 2
    return (r % 2) * (nb // 2) + j // per, j % per


def _mm_nt(a, w, name, tm=512, tr=1536, tkc=1024, paired=False):
    m, n = a.shape
    nb, k, nq = w.shape
    assert n == nb * nq
    tm, tr, tkc = _tile(m, tm, 8), _tile(nq, tr), _tile(k, tkc)
    per = nq // tr

    def kern(a_ref, w_ref, o_ref):
        acc = None
        for r in range(nb * per):
            blk, tile = _paired_block(r, per, nb) if paired else (r // per, r % per)
            p = _dot_nt(a_ref[:, r * tr:(r + 1) * tr], w_ref[blk, :, tile * tr:(tile + 1) * tr])
            acc = p if acc is None else acc + p
        o_ref[...] = acc

    return pl.pallas_call(
        kern, name=name, grid=(k // tkc, m // tm),
        in_specs=[pl.BlockSpec((tm, n), lambda kc, i: (i, 0)),
                  pl.BlockSpec((nb, tkc, nq), lambda kc, i: (0, kc, 0))],
        out_specs=pl.BlockSpec((tm, tkc), lambda kc, i: (i, kc)),
        out_shape=jax.ShapeDtypeStruct((m, k), F32),
        compiler_params=_params(("parallel", "parallel")),
    )(a, w)


def _mm_tn(a, dy, nb, name, tkk=512, tn=1536, paired=False):
    s, k = a.shape
    s2, n = dy.shape
    assert s == s2 and n % nb == 0
    nq = n // nb
    tkk, tn = _tile(k, tkk), _tile(nq, tn)
    per = nq // tn

    def w_block(j):
        return _paired_block(j, per, nb) if paired else (j // per, j % per)

    def kern(a_ref, dy_ref, o_ref):
        o_ref[...] = _dot_tn(a_ref[...], dy_ref[...]).astype(o_ref.dtype)

    return pl.pallas_call(
        kern, name=name, grid=(nb * per, k // tkk),
        in_specs=[pl.BlockSpec((s, tkk), lambda j, kk: (0, kk)),
                  pl.BlockSpec((s, tn), lambda j, kk: (0, j))],
        out_specs=pl.BlockSpec((None, tkk, tn), lambda j, kk: (w_block(j)[0], kk, w_block(j)[1])),
        out_shape=jax.ShapeDtypeStruct((nb, k, nq), BF16),
        compiler_params=_params(("parallel", "parallel")),
    )(a, dy)


ROWS = 256


def _rms(x, g):
    r = lax.rsqrt(jnp.mean(x * x, axis=-1, keepdims=True) + EPS)
    return x * r * g


def _rms_bwd(x, g, dh):
    r = lax.rsqrt(jnp.mean(x * x, axis=-1, keepdims=True) + EPS)
    xh = x * r
    dgx = dh * g
    dx = r * (dgx - xh * jnp.mean(dgx * xh, axis=-1, keepdims=True))
    return dx, dh * xh


def _sigmoid(x):
    return 1.0 / (1.0 + jnp.exp(-x))


def _gelu(y):
    return 0.5 * y * (1.0 + jnp.tanh(GELU_K * (y + GELU_C * y * y * y)))


def _gelu_grad(y):
    t = jnp.tanh(GELU_K * (y + GELU_C * y * y * y))
    return 0.5 * (1.0 + t) + 0.5 * y * (1.0 - t * t) * GELU_K * (1.0 + 3.0 * GELU_C * y * y)


def _row_spec(tm, d, col=0):
    return pl.BlockSpec((tm, d), lambda i: (i, col))


def _vec_spec(d):
    return pl.BlockSpec((1, d), lambda i: (0, 0))


def _acc_rows(ref, val, i):
    s = jnp.sum(val, axis=0, keepdims=True)

    @pl.when(i == 0)
    def _():
        ref[...] = s

    @pl.when(i > 0)
    def _():
        ref[...] += s


def _norm_f32(x, g, name):
    s, d = x.shape
    tm = _tile(s, ROWS, 8)

    def kern(x_ref, g_ref, h_ref):
        h_ref[...] = _rms(x_ref[...], g_ref[...])

    return pl.pallas_call(
        kern, name=name, grid=(s // tm,),
        in_specs=[_row_spec(tm, d), _vec_spec(d)],
        out_specs=_row_spec(tm, d),
        out_shape=jax.ShapeDtypeStruct((s, d), F32),
        compiler_params=_params(("parallel",)),
    )(x, g)


def _glu_res_norm(vg, x, g, name):
    s, d = x.shape
    tm = _tile(s, ROWS, 16)

    def kern(val_ref, gate_ref, x_ref, g_ref, x1_ref, hf_ref):
        x1 = x_ref[...] + val_ref[...] * _sigmoid(gate_ref[...])
        x1_ref[...] = x1
        hf_ref[...] = _rms(x1, g_ref[...]).astype(BF16)

    return pl.pallas_call(
        kern, name=name, grid=(s // tm,),
        in_specs=[_row_spec(tm, d, 0), _row_spec(tm, d, 1), _row_spec(tm, d), _vec_spec(d)],
        out_specs=[_row_spec(tm, d), _row_spec(tm, d)],
        out_shape=[jax.ShapeDtypeStruct((s, d), F32), jax.ShapeDtypeStruct((s, d), BF16)],
        compiler_params=_params(("parallel",)),
    )(vg, vg, x, g)


FFN_TILE = 1408


def _ffn_in_act(hf, w_in, name, tm=512):
    s, k = hf.shape
    nb, _, nq = w_in.shape
    tm, tn = _tile(s, tm, 16), _tile(nq, FFN_TILE)
    per = nq // tn
    nf = (nb // 2) * per

    def kern(h_ref, wg_ref, wu_ref, a_ref, gu_ref):
        h = h_ref[...]
        g = _dot(h, wg_ref[...])
        u = _dot(h, wu_ref[...])
        a_ref[...] = (g * _sigmoid(g) * u).astype(BF16)
        gu_ref[:, :tn] = g.astype(BF16)
        gu_ref[:, tn:] = u.astype(BF16)

    return pl.pallas_call(
        kern, name=name, grid=(nf, s // tm),
        in_specs=[pl.BlockSpec((tm, k), lambda j, i: (i, 0)),
                  pl.BlockSpec((None, k, tn), lambda j, i: (j // per, 0, j % per)),
                  pl.BlockSpec((None, k, tn), lambda j, i: (nb // 2 + j // per, 0, j % per))],
        out_specs=[pl.BlockSpec((tm, tn), lambda j, i: (i, j)),
                   pl.BlockSpec((tm, 2 * tn), lambda j, i: (i, j))],
        out_shape=[jax.ShapeDtypeStruct((s, nf * tn), BF16), jax.ShapeDtypeStruct((s, 2 * nf * tn), BF16)],
        compiler_params=_params(("parallel", "parallel")),
    )(hf, w_in, w_in)


def _ffn_out_dx_act(dresb, w_out, gu, name, tm=512):
    s, d = dresb.shape
    f = w_out.shape[1]
    tm = _tile(s, tm, 16)
    tn = _tile(f // 2, FFN_TILE)

    def kern(d_ref, w_ref, gu_ref, o_ref):
        da = _dot_nt(d_ref[...], w_ref[...])
        g = gu_ref[:, :tn].astype(F32)
        u = gu_ref[:, tn:].astype(F32)
        sg = _sigmoid(g)
        o_ref[:, :tn] = (da * u * sg * (1.0 + g * (1.0 - sg))).astype(BF16)
        o_ref[:, tn:] = (da * g * sg).astype(BF16)

    pair = pl.BlockSpec((tm, 2 * tn), lambda j, i: (i, j))
    return pl.pallas_call(
        kern, name=name, grid=(f // tn, s // tm),
        in_specs=[pl.BlockSpec((tm, d), lambda j, i: (i, 0)),
                  pl.BlockSpec((None, tn, d), lambda j, i: (0, j, 0)), pair],
        out_specs=pair,
        out_shape=jax.ShapeDtypeStruct((s, 2 * f), BF16),
        compiler_params=_params(("parallel", "parallel")),
    )(dresb, w_out, gu)


def _res_norm(o, x, gains, name):
    s, d = x.shape
    tm = _tile(s, ROWS, 16)
    ng = len(gains)

    def kern(o_ref, x_ref, *refs):
        xn = x_ref[...] + o_ref[...]
        refs[ng][...] = xn
        for t in range(ng):
            refs[ng + 1 + t][...] = _rms(xn, refs[t][...]).astype(BF16)

    return pl.pallas_call(
        kern, name=name, grid=(s // tm,),
        in_specs=[_row_spec(tm, d), _row_spec(tm, d)] + [_vec_spec(d)] * ng,
        out_specs=[_row_spec(tm, d)] * (1 + ng),
        out_shape=[jax.ShapeDtypeStruct((s, d), F32)] + [jax.ShapeDtypeStruct((s, d), BF16)] * ng,
        compiler_params=_params(("parallel",)),
    )(o, x, *gains)


def _loss_head(o, x, g, target, name):
    s, d = x.shape
    tm = _tile(s, ROWS, 16)

    def kern(o_ref, x_ref, g_ref, t_ref, dx_ref, dxb_ref, loss_ref, dg_ref):
        i = pl.program_id(0)
        x4 = x_ref[...] + o_ref[...]
        gg = g_ref[...]
        diff = _rms(x4, gg) - t_ref[...]
        dx, dgr = _rms_bwd(x4, gg, diff * (1.0 / d))
        dx_ref[...] = dx
        dxb_ref[...] = dx.astype(BF16)
        _acc_rows(loss_ref, diff * diff * (0.5 / d), i)
        _acc_rows(dg_ref, dgr, i)

    return pl.pallas_call(
        kern, name=name, grid=(s // tm,),
        in_specs=[_row_spec(tm, d), _row_spec(tm, d), _vec_spec(d), _row_spec(tm, d)],
        out_specs=[_row_spec(tm, d), _row_spec(tm, d), _vec_spec(d), _vec_spec(d)],
        out_shape=[jax.ShapeDtypeStruct((s, d), F32), jax.ShapeDtypeStruct((s, d), BF16),
                   jax.ShapeDtypeStruct((1, d), F32), jax.ShapeDtypeStruct((1, d), F32)],
        compiler_params=_params(("arbitrary",)),
    )(o, x, g, target)


def _norm_bwd(x, gains, dhs, dres, name):
    s, d = x.shape
    tm = _tile(s, ROWS, 16)
    ng = len(gains)

    def kern(x_ref, dres_ref, *refs):
        i = pl.program_id(0)
        x_ = x_ref[...]
        acc = dres_ref[...]
        for t in range(ng):
            dx, dgr = _rms_bwd(x_, refs[t][...], refs[ng + t][...])
            acc = acc + dx
            _acc_rows(refs[2 * ng + 2 + t], dgr, i)
        refs[2 * ng][...] = acc
        refs[2 * ng + 1][...] = acc.astype(BF16)

    return pl.pallas_call(
        kern, name=name, grid=(s // tm,),
        in_specs=[_row_spec(tm, d), _row_spec(tm, d)] + [_vec_spec(d)] * ng + [_row_spec(tm, d)] * ng,
        out_specs=[_row_spec(tm, d), _row_spec(tm, d)] + [_vec_spec(d)] * ng,
        out_shape=[jax.ShapeDtypeStruct((s, d), F32), jax.ShapeDtypeStruct((s, d), BF16)]
        + [jax.ShapeDtypeStruct((1, d), F32)] * ng,
        compiler_params=_params(("arbitrary",)),
    )(x, dres, *gains, *dhs)


def _glu_bwd(dmix, vg, name):
    s, d = dmix.shape
    tm = _tile(s, ROWS, 16)

    def kern(dm_ref, val_ref, gate_ref, o_ref):
        dm = dm_ref[...]
        sg = _sigmoid(gate_ref[...])
        o_ref[:, :d] = (dm * sg).astype(BF16)
        o_ref[:, d:] = (dm * val_ref[...] * sg * (1.0 - sg)).astype(BF16)

    return pl.pallas_call(
        kern, name=name, grid=(s // tm,),
        in_specs=[_row_spec(tm, d), _row_spec(tm, d, 0), _row_spec(tm, d, 1)],
        out_specs=_row_spec(tm, 2 * d),
        out_shape=jax.ShapeDtypeStruct((s, 2 * d), BF16),
        compiler_params=_params(("parallel",)),
    )(dmix, vg, vg)


SUB = 8


def _local_scan(vr, vi, pre_ref, pim_ref, reverse):
    sub = lax.broadcasted_iota(jnp.int32, vr.shape, 1)
    sign = -1.0 if reverse else 1.0
    for sh in (1, 2, 4):
        ar = pre_ref[sh - 1:sh, :][None]
        ai = sign * pim_ref[sh - 1:sh, :][None]
        keep = sub < SUB - sh if reverse else sub >= sh
        sr = jnp.where(keep, pltpu.roll(vr, SUB - sh if reverse else sh, 1), 0.0)
        si = jnp.where(keep, pltpu.roll(vi, SUB - sh if reverse else sh, 1), 0.0)
        vr, vi = vr + ar * sr - ai * si, vi + ar * si + ai * sr
    return vr, vi


def _s5_fwd(u, bbd_re, bbd_im, cbd_re, cbd_im, pw_re, pw_im, dskip, name):
    s, d = u.shape
    nt, cw, lw = bbd_re.shape
    t = _tile(s, SCAN_T, SUB)
    nc = s // t
    ng = t // SUB

    def kern(u_ref, bre_ref, bim_ref, cre_ref, cim_ref, pre_ref, pim_ref, d_ref,
             xr_ref, xi_ref, y_ref, z_ref, car_ref, cai_ref):
        c = pl.program_id(1)

        @pl.when(c == 0)
        def _():
            car_ref[...] = jnp.zeros_like(car_ref)
            cai_ref[...] = jnp.zeros_like(cai_ref)

        u_ = u_ref[...]
        ub = u_.astype(BF16)
        vr, vi = _local_scan(_dot(ub, bre_ref[...]).reshape(ng, SUB, lw),
                             _dot(ub, bim_ref[...]).reshape(ng, SUB, lw), pre_ref, pim_ref, False)
        cr = car_ref[...]
        ci = cai_ref[...]
        pr = pre_ref[...]
        pi = pim_ref[...]
        for gidx in range(ng):
            rows = slice(gidx * SUB, (gidx + 1) * SUB)
            gr = vr[gidx] + pr * cr - pi * ci
            gi = vi[gidx] + pr * ci + pi * cr
            xr_ref[rows, :] = gr
            xi_ref[rows, :] = gi
            cr, ci = gr[SUB - 1:SUB, :], gi[SUB - 1:SUB, :]
        car_ref[...] = cr
        cai_ref[...] = ci
        y = (_dot(xr_ref[...].astype(BF16), cre_ref[...]) - _dot(xi_ref[...].astype(BF16), cim_ref[...])
             + d_ref[...] * u_)
        y_ref[...] = y
        z_ref[...] = _gelu(y).astype(BF16)

    tok = pl.BlockSpec((t, cw), lambda j, c: (c, j))
    st = pl.BlockSpec((t, lw), lambda j, c: (c, j))
    return pl.pallas_call(
        kern, name=name, grid=(nt, nc),
        in_specs=[tok,
                  pl.BlockSpec((None, cw, lw), lambda j, c: (j, 0, 0)),
                  pl.BlockSpec((None, cw, lw), lambda j, c: (j, 0, 0)),
                  pl.BlockSpec((None, lw, cw), lambda j, c: (j, 0, 0)),
                  pl.BlockSpec((None, lw, cw), lambda j, c: (j, 0, 0)),
                  pl.BlockSpec((SUB, lw), lambda j, c: (0, j)),
                  pl.BlockSpec((SUB, lw), lambda j, c: (0, j)),
                  pl.BlockSpec((1, cw), lambda j, c: (0, j))],
        out_specs=[st, st, tok, tok],
        out_shape=[jax.ShapeDtypeStruct((s, nt * lw), F32), jax.ShapeDtypeStruct((s, nt * lw), F32),
                   jax.ShapeDtypeStruct((s, d), F32), jax.ShapeDtypeStruct((s, d), BF16)],
        scratch_shapes=[pltpu.VMEM((1, lw), F32), pltpu.VMEM((1, lw), F32)],
        compiler_params=_params(("parallel", "arbitrary")),
    )(u, bbd_re, bbd_im, cbd_re, cbd_im, pw_re, pw_im, dskip)


def _s5_bwd(dz, y, u, xs_re, xs_im, bbd_re, bbd_im, cbd_re, cbd_im, pw_re, pw_im, pf_re, pf_im, dskip, name):
    s, d = u.shape
    nt, cw, lw = bbd_re.shape
    t = _tile(s, SCAN_T, SUB)
    nc = s // t
    ng = t // SUB

    def kern(dz_ref, y_ref, u_ref, xr_ref, xi_ref, bre_ref, bim_ref, cre_ref, cim_ref,
             pre_ref, pim_ref, fre_ref, fim_ref, d_ref,
             du_ref, dd_ref, dcr_ref, dci_ref, dbr_ref, dbi_ref, dar_ref, dai_ref,
             car_ref, cai_ref, gr_ref, gi_ref):
        c = pl.program_id(1)

        @pl.when(c == 0)
        def _():
            car_ref[...] = jnp.zeros_like(car_ref)
            cai_ref[...] = jnp.zeros_like(cai_ref)

        u_ = u_ref[...]
        ub = u_.astype(BF16)
        dy = dz_ref[...] * _gelu_grad(y_ref[...])
        dyb = dy.astype(BF16)
        vr, vi = _local_scan(_dot_nt(dyb, cre_ref[...]).reshape(ng, SUB, lw),
                             (-_dot_nt(dyb, cim_ref[...])).reshape(ng, SUB, lw), pre_ref, pim_ref, True)
        later_r = car_ref[...]
        later_i = cai_ref[...]
        cr, ci = later_r, later_i
        fr = fre_ref[...]
        fi = fim_ref[...]
        for gidx in reversed(range(ng)):
            rows = slice(gidx * SUB, (gidx + 1) * SUB)
            ar = vr[gidx] + fr * cr + fi * ci
            ai = vi[gidx] + fr * ci - fi * cr
            gr_ref[rows, :] = ar
            gi_ref[rows, :] = ai
            cr, ci = ar[0:1, :], ai[0:1, :]
        car_ref[...] = cr
        cai_ref[...] = ci
        gr = gr_ref[...]
        gi = gi_ref[...]
        rows = lax.broadcasted_iota(jnp.int32, (t, lw), 0)
        gsr = jnp.where(rows < t - 1, pltpu.roll(gr, t - 1, 0), later_r)
        gsi = jnp.where(rows < t - 1, pltpu.roll(gi, t - 1, 0), later_i)
        xr = xr_ref[...]
        xi = xi_ref[...]
        dar = jnp.sum(gsr * xr + gsi * xi, axis=0, keepdims=True)
        dai = jnp.sum(gsi * xr - gsr * xi, axis=0, keepdims=True)
        grb = gr.astype(BF16)
        gib = gi.astype(BF16)
        dbr = _dot_tn(ub, grb)
        dbi = _dot_tn(ub, gib)
        dcr = _dot_tn(dyb, xr.astype(BF16))
        dci = _dot_tn(dyb, xi.astype(BF16))
        du_ref[...] = dy * d_ref[...] + _dot_nt(grb, bre_ref[...]) + _dot_nt(gib, bim_ref[...])
        ddv = jnp.sum(dy * u_, axis=0, keepdims=True)

        @pl.when(c == 0)
        def _():
            dd_ref[...] = ddv
            dcr_ref[...] = dcr
            dci_ref[...] = dci
            dbr_ref[...] = dbr
            dbi_ref[...] = dbi
            dar_ref[...] = dar
            dai_ref[...] = dai

        @pl.when(c > 0)
        def _():
            dd_ref[...] += ddv
            dcr_ref[...] += dcr
            dci_ref[...] += dci
            dbr_ref[...] += dbr
            dbi_ref[...] += dbi
            dar_ref[...] += dar
            dai_ref[...] += dai

    tok = pl.BlockSpec((t, cw), lambda j, c: (nc - 1 - c, j))
    st = pl.BlockSpec((t, lw), lambda j, c: (nc - 1 - c, j))
    wb = pl.BlockSpec((None, cw, lw), lambda j, c: (j, 0, 0))
    wc = pl.BlockSpec((None, lw, cw), lambda j, c: (j, 0, 0))
    pw = pl.BlockSpec((SUB, lw), lambda j, c: (0, j))
    vec_c = pl.BlockSpec((1, cw), lambda j, c: (0, j))
    vec_l = pl.BlockSpec((1, lw), lambda j, c: (0, j))
    return pl.pallas_call(
        kern, name=name, grid=(nt, nc),
        in_specs=[tok, tok, tok, st, st, wb, wb, wc, wc, pw, pw, pw, pw, vec_c],
        out_specs=[tok, vec_c, wb, wb, wb, wb, vec_l, vec_l],
        out_shape=[jax.ShapeDtypeStruct((s, d), F32), jax.ShapeDtypeStruct((1, d), F32)]
        + [jax.ShapeDtypeStruct((nt, cw, lw), F32)] * 4
        + [jax.ShapeDtypeStruct((1, nt * lw), F32)] * 2,
        scratch_shapes=[pltpu.VMEM((1, lw), F32), pltpu.VMEM((1, lw), F32),
                        pltpu.VMEM((t, lw), F32), pltpu.VMEM((t, lw), F32)],
        compiler_params=_params(("parallel", "arbitrary")),
    )(dz, y, u, xs_re, xs_im, bbd_re, bbd_im, cbd_re, cbd_im, pw_re, pw_im, pf_re, pf_im, dskip)


def _s5_discretize(lam_re, lam_im, log_dt, b_re, b_im):
    dt = jnp.exp(log_dt)[:, None]
    mag = jnp.exp(lam_re * dt)
    ang = lam_im * dt
    lb_re = mag * jnp.cos(ang)
    lb_im = mag * jnp.sin(ang)
    nr = lb_re - 1.0
    den = lam_re * lam_re + lam_im * lam_im
    f_re = (nr * lam_re + lb_im * lam_im) / den
    f_im = (lb_im * lam_re - nr * lam_im) / den
    bb_re = f_re[..., None] * b_re - f_im[..., None] * b_im
    bb_im = f_re[..., None] * b_im + f_im[..., None] * b_re
    return lb_re, lb_im, bb_re, bb_im


def _block_diag(w):
    nt, ng, a, b = w.shape
    eye = jnp.eye(ng, dtype=w.dtype)
    return (w[:, :, :, None, :] * eye[None, :, None, :, None]).reshape(nt, ng * a, ng * b)


def _block_diag_take(w, a, b):
    nt = w.shape[0]
    ng = GROUPS_PER_TILE
    w5 = w.reshape(nt, ng, a, ng, b)
    on_diagonal = jnp.eye(ng, dtype=bool)[None, :, None, :, None]
    return jnp.sum(jnp.where(on_diagonal, w5, 0.0), axis=3).reshape(nt * ng, a, b)


def _att_combine(outs, lses, name):
    s, d = outs[0].shape
    tm = _tile(s, ROWS, 16)
    ng = len(outs)

    def kern(*refs):
        for h in range(d // HEAD_DIM):
            cols = slice(h * HEAD_DIM, (h + 1) * HEAD_DIM)
            ls = [refs[ng + t][:, h:h + 1] for t in range(ng)]
            mx = functools.reduce(jnp.maximum, ls)
            es = [jnp.exp(l - mx) for l in ls]
            den = functools.reduce(lambda a, b: a + b, es)
            o = functools.reduce(lambda a, b: a + b, [es[t] / den * refs[t][:, cols] for t in range(ng)])
            refs[2 * ng][:, cols] = o
            refs[2 * ng + 1][:, cols] = o.astype(BF16)

    return pl.pallas_call(
        kern, name=name, grid=(s // tm,),
        in_specs=[_row_spec(tm, d)] * ng + [_row_spec(tm, d // HEAD_DIM)] * ng,
        out_specs=[_row_spec(tm, d)] * 2,
        out_shape=[jax.ShapeDtypeStruct((s, d), F32), jax.ShapeDtypeStruct((s, d), BF16)],
        compiler_params=_params(("parallel",)),
    )(*outs, *lses)


DIL_MAX = 16


def _slab(r):
    return 4 * (r % 4) + r // 4


def _to_slabs(xs, name):
    s, w = xs[0].shape
    n = s // DIL_MAX
    nx = len(xs)

    def kern(*refs):
        o_ref = refs[nx]
        for r in range(DIL_MAX):
            rows = [x_ref[pl.ds(r, n, stride=DIL_MAX), :] for x_ref in refs[:nx]]
            o_ref[_slab(r) * n:(_slab(r) + 1) * n, :] = functools.reduce(lambda a, b: a + b, rows)

    spec = pl.BlockSpec((s, 128), lambda i: (0, i))
    return pl.pallas_call(
        kern, name=name, grid=(w // 128,), in_specs=[spec] * nx, out_specs=spec,
        out_shape=jax.ShapeDtypeStruct((s, w), F32), compiler_params=_params(("parallel",)),
    )(*xs)


def _from_slabs(x, name):
    s, w = x.shape
    n = s // DIL_MAX

    def kern(x_ref, o_ref, ob_ref):
        for r in range(DIL_MAX):
            o_ref[pl.ds(r, n, stride=DIL_MAX), :] = x_ref[_slab(r) * n:(_slab(r) + 1) * n, :]
        ob_ref[...] = o_ref[...].astype(BF16)

    spec = pl.BlockSpec((s, 128), lambda i: (0, i))
    return pl.pallas_call(
        kern, name=name, grid=(w // 128,), in_specs=[spec], out_specs=[spec, spec],
        out_shape=[jax.ShapeDtypeStruct((s, w), F32), jax.ShapeDtypeStruct((s, w), BF16)],
        compiler_params=_params(("parallel",)),
    )(x)


def _norm_bf16(x, gains, name):
    s, d = x.shape
    tm = _tile(s, ROWS, 16)
    ng = len(gains)

    def kern(x_ref, *refs):
        x_ = x_ref[...]
        for t in range(ng):
            refs[ng + t][...] = _rms(x_, refs[t][...]).astype(BF16)

    return pl.pallas_call(
        kern, name=name, grid=(s // tm,),
        in_specs=[_row_spec(tm, d)] + [_vec_spec(d)] * ng, out_specs=[_row_spec(tm, d)] * ng,
        out_shape=[jax.ShapeDtypeStruct((s, d), BF16)] * ng,
        compiler_params=_params(("parallel",)),
    )(x, *gains)


def _att_geometry(dil, s):
    g = DIL_MAX // dil
    return s // DIL_MAX, g, max(ATT_BLK // g, 16)


def _att_mask(g, j, jb, rep):
    b = g * j

    def pos(i):
        sl, jj = i // j, i % j
        off = {1: 0, 4: sl, 16: sl // 4 + 4 * (sl % 4)}[g]
        return g * jj + off

    qi = lax.broadcasted_iota(jnp.int32, (rep * b, 2 * b), 0) % b
    ki = lax.broadcasted_iota(jnp.int32, (rep * b, 2 * b), 1)
    prev = ki < b
    dist = pos(qi) - pos(ki % b) + jnp.where(prev, b, 0)
    return (dist >= 0) & (dist <= ATT_BLK) & (jnp.logical_not(prev) | (jb > 0))


def _stack_heads(x, rep):
    return jnp.concatenate([x[:, h * HEAD_DIM:(h + 1) * HEAD_DIM] for h in range(rep)], axis=0)


def _att_fwd(q, kv, gi, dil, nh, name):
    s = q.shape[0]
    n, g, j = _att_geometry(dil, s)
    b = g * j
    ng = q.shape[1] // (nh * HEAD_DIM)
    rep = nh // N_KV_HEADS
    qw = rep * HEAD_DIM
    scale = HEAD_DIM ** -0.5
    q3 = q.reshape(DIL_MAX, n, q.shape[1])
    kv3 = kv.reshape(DIL_MAX, n, kv.shape[1])

    kw = N_KV_HEADS * HEAD_DIM

    def kern(q_ref, kc_ref, kp_ref, vc_ref, vp_ref, o_ref, l_ref, lw_ref):
        jb = pl.program_id(1)
        mask = _att_mask(g, j, jb, rep)
        for kh in range(N_KV_HEADS):
            hs = slice(kh * HEAD_DIM, (kh + 1) * HEAD_DIM)
            kc, kp, vc, vp = [r[:, :, hs].reshape(b, HEAD_DIM) for r in (kc_ref, kp_ref, vc_ref, vp_ref)]
            k2 = jnp.concatenate([kp, kc], axis=0)
            v2 = jnp.concatenate([vp, vc], axis=0)
            qs = _stack_heads(q_ref[:, :, kh * qw:(kh + 1) * qw].reshape(b, qw), rep)
            sc = jnp.where(mask, _dot_nt(qs, k2) * scale, NEG_INF)
            m = jnp.max(sc, axis=-1, keepdims=True)
            p = jnp.exp(sc - m)
            l = jnp.sum(p, axis=-1, keepdims=True)
            out = _dot((p / l).astype(BF16), v2)
            lse = m + jnp.log(l)
            wide = jnp.broadcast_to(lse, (rep * b, HEAD_DIM))
            for h in range(rep):
                cols = slice(kh * qw + h * HEAD_DIM, kh * qw + (h + 1) * HEAD_DIM)
                o_ref[:, :, cols] = out[h * b:(h + 1) * b].reshape(g, j, HEAD_DIM)
                lw_ref[:, :, cols] = wide[h * b:(h + 1) * b].reshape(g, j, HEAD_DIM)
                l_ref[:, :, kh * rep + h:kh * rep + h + 1] = lse[h * b:(h + 1) * b].reshape(g, j, 1)

    def kv_spec(col, prev):
        if prev:
            return pl.BlockSpec((g, j, kw), lambda c, jb: (c, jnp.maximum(jb - 1, 0), col))
        return pl.BlockSpec((g, j, kw), lambda c, jb: (c, jb, col))

    out_spec = pl.BlockSpec((g, j, nh * HEAD_DIM), lambda c, jb: (c, jb, 0))
    out, lse, lse_wide = pl.pallas_call(
        kern, name=name, grid=(dil, n // j),
        in_specs=[pl.BlockSpec((g, j, nh * HEAD_DIM), lambda c, jb: (c, jb, gi)),
                  kv_spec(0, False), kv_spec(0, True), kv_spec(1, False), kv_spec(1, True)],
        out_specs=[out_spec, pl.BlockSpec((g, j, nh), lambda c, jb: (c, jb, 0)), out_spec],
        out_shape=[jax.ShapeDtypeStruct((DIL_MAX, n, nh * HEAD_DIM), F32),
                   jax.ShapeDtypeStruct((DIL_MAX, n, nh), F32),
                   jax.ShapeDtypeStruct((DIL_MAX, n, nh * HEAD_DIM), F32)],
        compiler_params=_params(("parallel", "arbitrary")),
    )(q3, kv3, kv3, kv3, kv3)
    del ng
    return out.reshape(s, nh * HEAD_DIM), lse.reshape(s, nh), lse_wide.reshape(s, nh * HEAD_DIM)


def _att_bwd(q, kv, do, o, lses, dq_all, gi, dil, nh, name):
    s = q.shape[0]
    n, g, j = _att_geometry(dil, s)
    b = g * j
    nb = n // j
    ng = len(lses)
    rep = nh // N_KV_HEADS
    qw = rep * HEAD_DIM
    dm = nh * HEAD_DIM
    scale = HEAD_DIM ** -0.5
    q3 = q.reshape(DIL_MAX, n, q.shape[1])
    kv3 = kv.reshape(DIL_MAX, n, kv.shape[1])
    wide = [a.reshape(DIL_MAX, n, dm) for a in (do, o, *lses)]
    kw = N_KV_HEADS * HEAD_DIM

    def kern(q_ref, kc_ref, kp_ref, vc_ref, vp_ref, do_ref, o_ref, *refs):
        l_refs = refs[:ng]
        dq_ref, dk_ref, dv_ref, ck_ref, cv_ref = refs[ng + 1:]
        jb = pl.program_id(1)

        @pl.when(jb == 0)
        def _():
            ck_ref[...] = jnp.zeros_like(ck_ref)
            cv_ref[...] = jnp.zeros_like(cv_ref)

        @pl.when(jb < nb)
        def _():
            mask = _att_mask(g, j, jb, rep)
            for kh in range(N_KV_HEADS):
                hs = slice(kh * HEAD_DIM, (kh + 1) * HEAD_DIM)
                ws = slice(kh * qw, (kh + 1) * qw)

                def stacked(ref):
                    return _stack_heads(ref[:, :, ws].reshape(b, qw), rep)

                kc, kp, vc, vp = [r[:, :, hs].reshape(b, HEAD_DIM) for r in (kc_ref, kp_ref, vc_ref, vp_ref)]
                k2 = jnp.concatenate([kp, kc], axis=0)
                v2 = jnp.concatenate([vp, vc], axis=0)
                qs = stacked(q_ref)
                ls = [stacked(r)[:, 0:1] for r in l_refs]
                mx = functools.reduce(jnp.maximum, ls)
                den = functools.reduce(lambda a, c: a + c, [jnp.exp(l - mx) for l in ls])
                lse_g = ls[gi]
                w = jnp.exp(lse_g - mx) / den
                do_ = stacked(do_ref)
                ct = w * jnp.sum(do_ * stacked(o_ref), axis=-1, keepdims=True)
                dob = (w * do_).astype(BF16)
                p = jnp.exp(jnp.where(mask, _dot_nt(qs, k2) * scale, NEG_INF) - lse_g)
                ds = (p * (_dot_nt(dob, v2) - ct) * scale).astype(BF16)
                dq = (_dot(ds, k2)).astype(BF16)
                for h in range(rep):
                    cols = slice(kh * qw + h * HEAD_DIM, kh * qw + (h + 1) * HEAD_DIM)
                    dq_ref[:, :, cols] = dq[h * b:(h + 1) * b].reshape(g, j, HEAD_DIM)
                dk2 = _dot_tn(ds, qs)
                dv2 = _dot_tn(p.astype(BF16), dob)
                dk_ref[:, :, hs] = (ck_ref[:, hs] + dk2[:b]).reshape(g, j, HEAD_DIM)
                dv_ref[:, :, hs] = (cv_ref[:, hs] + dv2[:b]).reshape(g, j, HEAD_DIM)
                ck_ref[:, hs] = dk2[b:]
                cv_ref[:, hs] = dv2[b:]

        @pl.when(jb == nb)
        def _():
            dk_ref[...] = ck_ref[...].reshape(g, j, kw)
            dv_ref[...] = cv_ref[...].reshape(g, j, kw)

    def jq(jb):
        return jnp.minimum(jb, nb - 1)

    def kv_spec(col, prev):
        if prev:
            return pl.BlockSpec((g, j, kw), lambda c, jb: (c, jnp.maximum(jq(jb) - 1, 0), col))
        return pl.BlockSpec((g, j, kw), lambda c, jb: (c, jq(jb), col))

    wide_spec = pl.BlockSpec((g, j, dm), lambda c, jb: (c, jq(jb), 0))
    dkv_spec = pl.BlockSpec((g, j, kw), lambda c, jb: (c, jnp.maximum(jb - 1, 0), 0))
    dq, dk, dv = pl.pallas_call(
        kern, name=name, grid=(dil, nb + 1),
        in_specs=[pl.BlockSpec((g, j, dm), lambda c, jb: (c, jq(jb), gi)),
                  kv_spec(0, False), kv_spec(0, True), kv_spec(1, False), kv_spec(1, True)]
        + [wide_spec] * (2 + ng) + [_ANY],
        out_specs=[pl.BlockSpec((g, j, dm), lambda c, jb: (c, jq(jb), gi)), dkv_spec, dkv_spec],
        out_shape=[jax.ShapeDtypeStruct((DIL_MAX, n, dq_all.shape[1]), BF16),
                   jax.ShapeDtypeStruct((DIL_MAX, n, kw), F32),
                   jax.ShapeDtypeStruct((DIL_MAX, n, kw), F32)],
        scratch_shapes=[pltpu.VMEM((b, kw), F32), pltpu.VMEM((b, kw), F32)],
        input_output_aliases={7 + ng: 0},
        compiler_params=_params(("parallel", "arbitrary")),
    )(q3, kv3, kv3, kv3, kv3, *wide, dq_all.reshape(DIL_MAX, n, dq_all.shape[1]))
    return dq.reshape(s, -1), dk.reshape(s, -1), dv.reshape(s, -1)


def _dkv_sum(dks, dvs, name):
    s, w = dks[0].shape
    tm = _tile(s, ROWS, 16)
    ng = len(dks)

    def kern(*refs):
        o_ref = refs[2 * ng]
        o_ref[:, :w] = functools.reduce(lambda a, b: a + b, [refs[t][...] for t in range(ng)]).astype(BF16)
        o_ref[:, w:] = functools.reduce(lambda a, b: a + b, [refs[ng + t][...] for t in range(ng)]).astype(BF16)

    return pl.pallas_call(
        kern, name=name, grid=(s // tm,),
        in_specs=[_row_spec(tm, w)] * (2 * ng),
        out_specs=_row_spec(tm, 2 * w),
        out_shape=jax.ShapeDtypeStruct((s, 2 * w), BF16),
        compiler_params=_params(("parallel",)),
    )(*dks, *dvs)


def _ffn_fwd(hf, weight, layer, tag):
    a, gu = _ffn_in_act(hf, weight(f"w_in{layer}", hf), f"{tag}_in")
    o = _mm_nn(a, weight(f"w_out{layer}", a), F32, f"{tag}_out", tn=1024, tk=5632)
    return gu, a, o


def _ffn_bwd(dresb, hf, gu, a, weight, emit, layer, tag):
    w_in, w_out = weight(f"w_in{layer}", None), weight(f"w_out{layer}", None)
    zero = emit(f"w_out{layer}", _mm_tn(a, dresb, 1, f"{tag}_out_dw", tn=1024))
    dgu = _ffn_out_dx_act(dresb, w_out, gu, f"{tag}_out_dx")
    zero = zero + emit(f"w_in{layer}",
                       _mm_tn(hf, dgu, w_in.shape[0], f"{tag}_in_dw", tn=FFN_TILE, paired=True))
    return _mm_nt(dgu, w_in, f"{tag}_in_dx", tr=FFN_TILE, tkc=512, paired=True), zero


def _local_step(x, target, small, weight, emit, stage):
    s, d = x.shape
    nh = d // HEAD_DIM
    n_groups = d // S5_GROUP_CH
    nt = n_groups // GROUPS_PER_TILE
    p_, c_ = S5_STATE, S5_GROUP_CH

    disc_in = (small["lam_re"], small["lam_im"], small["log_dt"], small["b_re"], small["b_im"])
    (lb_re, lb_im, bb_re, bb_im), disc_vjp = jax.vjp(_s5_discretize, *disc_in)
    del lb_re, lb_im
    dt = jnp.exp(small["log_dt"])[:, None]
    def pole_powers(exponents):
        k = exponents[:, None, None]
        mag = jnp.exp(k * (small["lam_re"] * dt)[None])
        ang = k * (small["lam_im"] * dt)[None]
        return ((mag * jnp.cos(ang)).reshape(SUB, n_groups * p_),
                (mag * jnp.sin(ang)).reshape(SUB, n_groups * p_))

    pw_re, pw_im = pole_powers(jnp.arange(1, SUB + 1, dtype=F32))
    pf_re, pf_im = pole_powers(jnp.arange(SUB, 0, -1, dtype=F32))
    bbd_re = _block_diag(bb_re.transpose(0, 2, 1).reshape(nt, GROUPS_PER_TILE, c_, p_)).astype(BF16)
    bbd_im = _block_diag(bb_im.transpose(0, 2, 1).reshape(nt, GROUPS_PER_TILE, c_, p_)).astype(BF16)
    cbd_re = _block_diag(small["c_re"].transpose(0, 2, 1).reshape(nt, GROUPS_PER_TILE, p_, c_)).astype(BF16)
    cbd_im = _block_diag(small["c_im"].transpose(0, 2, 1).reshape(nt, GROUPS_PER_TILE, p_, c_)).astype(BF16)

    h0 = _norm_f32(x, small["a_norm"], "s5_norm")
    xs_re, xs_im, y0, z = _s5_fwd(h0, bbd_re, bbd_im, cbd_re, cbd_im, pw_re, pw_im, small["s5_d"], "s5_fwd")
    zero = stage("glu", z)
    vg = _mm_nn(z, weight("w_glu", z), F32, "glu_mm", tn=1024)
    x1, hf0 = _glu_res_norm(vg, x, small["ffn_norm0"] + zero, "glu_res_norm")
    gu0, a0, o0 = _ffn_fwd(hf0, weight, 0, "ffn0")
    zero = stage("attention", a0)
    x2 = _to_slabs([o0, x1], "to_slabs")
    kvn, h1 = _norm_bf16(x2, [small["kv_norm"] + zero, small["b_norm"]], "att_norms")
    kv = _mm_nn(kvn, weight("w_kv", kvn), BF16, "kv_mm", tn=1024)
    q = _mm_nn(h1, weight("w_q", kv), BF16, "q_mm", tn=1536)
    outs, lses_narrow, lses = [], [], []
    for gi, (window, dil) in enumerate(PATTERNS):
        assert window // dil == ATT_BLK
        og, lg, lg_wide = _att_fwd(q, kv, gi, dil, nh, f"att_fwd{gi}")
        outs.append(og)
        lses_narrow.append(lg)
        lses.append(lg_wide)
    oatt, oattb = _att_combine(outs, lses_narrow, "att_combine")
    ao = _mm_nn(oattb, weight("w_o", oattb), F32, "o_mm", tn=1024)
    x3, hf1 = _res_norm(ao, x2, [small["ffn_norm1"] + stage("ffn1", oattb)], "att_res_norm")
    gu1, a1, o1 = _ffn_fwd(hf1, weight, 1, "ffn1")
    dres, dresb, loss_rows, d_final = _loss_head(
        o1, x3, small["final_norm"], _to_slabs([target], "target_to_slabs"), "loss_head")

    dhf1, zero = _ffn_bwd(dresb, hf1, gu1, a1, weight, emit, 1, "ffn1")
    dres, dresb, d_ffn_norm1 = _norm_bwd(x3, [small["ffn_norm1"] + zero], [dhf1], dres, "ffn1_norm_bwd")
    zero = emit("w_o", _mm_tn(oattb, dresb, 1, "o_dw", tn=1024))
    doatt = _mm_nt(dresb, weight("w_o", None), "o_dx", tr=2048)
    dks, dvs = [], []
    dq = lax.empty(q.shape, BF16)
    for gi, (window, dil) in enumerate(PATTERNS):
        dq, dk_g, dv_g = _att_bwd(q, kv, doatt, oatt, lses, dq, gi, dil, nh, f"att_bwd{gi}")
        dks.append(dk_g)
        dvs.append(dv_g)
    dkv = _dkv_sum(dks, dvs, "dkv_sum")
    w_q = weight("w_q", None)
    zero = zero + emit("w_q", _mm_tn(h1, dq, w_q.shape[0], "q_dw", tn=1536))
    zero = zero + emit("w_kv", _mm_tn(kvn, dkv, 1, "kv_dw", tn=1024))
    dh1 = _mm_nt(dq, w_q, "q_dx", tr=1536)
    dkvn = _mm_nt(dkv, weight("w_kv", None), "kv_dx", tr=1024)
    dres, dresb, d_b_norm, d_kv_norm = _norm_bwd(
        x2, [small["b_norm"] + zero, small["kv_norm"]], [dh1, dkvn], dres, "att_norm_bwd")
    dres, dresb = _from_slabs(dres, "from_slabs")
    dhf0, zero = _ffn_bwd(dresb, hf0, gu0, a0, weight, emit, 0, "ffn0")
    dres, dresb, d_ffn_norm0 = _norm_bwd(x1, [small["ffn_norm0"] + zero], [dhf0], dres, "ffn0_norm_bwd")
    del dresb
    dvg = _glu_bwd(dres, vg, "glu_bwd")
    w_glu = weight("w_glu", None)
    zero = emit("w_glu", _mm_tn(z, dvg, w_glu.shape[0], "glu_dw", tn=1024))
    dz = _mm_nt(dvg, w_glu, "glu_dx", tr=1024)
    dh0, d_s5_d, dcr, dci_neg, dbr, dbi, dar, dai = _s5_bwd(
        dz, y0, h0, xs_re, xs_im, bbd_re, bbd_im, cbd_re, cbd_im, pw_re, pw_im, pf_re, pf_im,
        small["s5_d"] + zero, "s5_bwd")
    grad_x, _, d_a_norm = _norm_bwd(x, [small["a_norm"]], [dh0], dres, "s5_norm_bwd")

    d_bb_re = _block_diag_take(dbr, c_, p_).transpose(0, 2, 1)
    d_bb_im = _block_diag_take(dbi, c_, p_).transpose(0, 2, 1)
    d_c_re = _block_diag_take(dcr, c_, p_)
    d_c_im = -_block_diag_take(dci_neg, c_, p_)
    d_lam_re, d_lam_im, d_log_dt, d_b_re, d_b_im = disc_vjp(
        (dar.reshape(n_groups, p_), dai.reshape(n_groups, p_), d_bb_re, d_bb_im))

    d_small = dict(lam_re=d_lam_re, lam_im=d_lam_im, log_dt=d_log_dt, b_re=d_b_re, b_im=d_b_im,
                   c_re=d_c_re, c_im=d_c_im, s5_d=d_s5_d, a_norm=d_a_norm, ffn_norm0=d_ffn_norm0,
                   ffn_norm1=d_ffn_norm1, b_norm=d_b_norm, kv_norm=d_kv_norm, final_norm=d_final)
    return loss_rows, grad_x, d_small


def _place():
    x, y, c = lax.axis_index("x"), lax.axis_index("y"), lax.axis_index("c")
    return x, y, c, [(1 - x, y), (x, 1 - y), (1 - x, 1 - y)]


_ANY = pl.BlockSpec(memory_space=pl.ANY)


_HBM = pl.BlockSpec(memory_space=pltpu.HBM)
_SEM = pl.BlockSpec(memory_space=pltpu.SEMAPHORE)
_EFFECT = pltpu.SideEffectType.DATAFLOW_SIDE_EFFECTING


def _in_hbm(a):
    return pltpu.with_memory_space_constraint(a, pltpu.HBM)


def _cast_place(shards, layer, chip, name):
    _, r, c = shards.shape
    tm = _tile(r, ROWS, 16)

    def kern(chip_ref, x_ref, o_ref):
        del chip_ref
        o_ref[...] = x_ref[...].astype(BF16)

    return pl.pallas_call(
        kern, name=name,
        grid_spec=pltpu.PrefetchScalarGridSpec(
            num_scalar_prefetch=1, grid=(r // tm,),
            in_specs=[pl.BlockSpec((None, tm, c), lambda i, ch: (layer, i, 0))],
            out_specs=pl.BlockSpec((None, tm, c), lambda i, ch: (ch[0], i, 0))),
        out_shape=jax.ShapeDtypeStruct((N_CHIPS, r, c), BF16),
        compiler_params=_params(("parallel",)),
    )(chip, shards)


def _my_part(land, block, c, halves):
    if not halves:
        return land.at[block]
    half = land.shape[1] // 2
    return land.at[block, pl.ds(c * half, half)]


def _gather_start(lands, name, halves, after):
    n = len(lands)

    def body(*refs):
        land = refs[:n]
        send, recv = refs[n + 1:2 * n + 1], refs[2 * n + 1:3 * n + 1]
        token = refs[4 * n + 1]
        x, y, c, peers = _place()
        me = 2 * x + y
        for a in range(n):
            for k, (px, py) in enumerate(peers):
                part = _my_part(land[a], me, c, halves)
                pltpu.make_async_remote_copy(
                    src_ref=part, dst_ref=part, send_sem=send[a].at[k], recv_sem=recv[a].at[k],
                    device_id=(px, py, c), device_id_type=MESH).start()
        token[...] = jnp.zeros_like(token)

    outs = pl.pallas_call(
        body, name=name,
        out_shape=[pltpu.SemaphoreType.DMA((3,))] * (2 * n) + [pltpu.HBM(a.shape, a.dtype) for a in lands]
        + [jax.ShapeDtypeStruct((8, 128), F32)],
        in_specs=[_HBM] * n + [_ANY],
        out_specs=[_SEM] * (2 * n) + [_HBM] * n + [pl.BlockSpec(memory_space=pltpu.VMEM)],
        input_output_aliases={i: 2 * n + i for i in range(n)},
        compiler_params=pltpu.CompilerParams(has_side_effects=_EFFECT),
    )(*[_in_hbm(a) for a in lands], after)
    return outs[:n], outs[n:2 * n], outs[2 * n:3 * n], outs[3 * n]


def _gather_wait(lands, sends, recvs, after, name, halves):
    n = len(lands)

    def body(*refs):
        land, send, recv = refs[:n], refs[n:2 * n], refs[2 * n:3 * n]
        x, y, c, peers = _place()
        me = 2 * x + y
        for a in range(n):
            for k, (px, py) in enumerate(peers):
                cp = pltpu.make_async_remote_copy(
                    src_ref=_my_part(land[a], me, c, halves), dst_ref=_my_part(land[a], 2 * px + py, c, halves),
                    send_sem=send[a].at[k], recv_sem=recv[a].at[k], device_id=(px, py, c), device_id_type=MESH)
                cp.wait_send()
                cp.wait_recv()

    return pl.pallas_call(
        body, name=name,
        out_shape=[pltpu.HBM(a.shape, a.dtype) for a in lands],
        in_specs=[_HBM] * n + [_SEM] * (2 * n) + [_ANY], out_specs=[_HBM] * n,
        input_output_aliases={i: i for i in range(n)},
        compiler_params=pltpu.CompilerParams(has_side_effects=_EFFECT),
    )(*lands, *sends, *recvs, after)


def _forward_start(lands, name):
    n = len(lands)

    def body(*refs):
        land = refs[:n]
        send, recv = refs[n:2 * n], refs[2 * n:3 * n]
        token = refs[4 * n]
        x, y, c, peers = _place()
        for a in range(n):
            for k, (px, py) in enumerate(peers):
                part = _my_part(land[a], 2 * px + py, c, True)
                pltpu.make_async_remote_copy(
                    src_ref=part, dst_ref=part, send_sem=send[a].at[k], recv_sem=recv[a].at[k],
                    device_id=(x, y, 1 - c), device_id_type=MESH).start()
        token[...] = jnp.zeros_like(token)

    outs = pl.pallas_call(
        body, name=name,
        out_shape=[pltpu.SemaphoreType.DMA((3,))] * (2 * n) + [pltpu.HBM(a.shape, a.dtype) for a in lands]
        + [jax.ShapeDtypeStruct((8, 128), F32)],
        in_specs=[_HBM] * n,
        out_specs=[_SEM] * (2 * n) + [_HBM] * n + [pl.BlockSpec(memory_space=pltpu.VMEM)],
        input_output_aliases={i: 2 * n + i for i in range(n)},
        compiler_params=pltpu.CompilerParams(has_side_effects=_EFFECT),
    )(*[_in_hbm(a) for a in lands])
    return outs[:n], outs[n:2 * n], outs[2 * n:3 * n], outs[3 * n]


def _forward_wait(land, send, recv, after, name):
    def body(land_ref, send_sem, recv_sem, after_ref, land_out):
        del after_ref, land_out
        x, y, c, peers = _place()
        for k, (px, py) in enumerate(peers):
            cp = pltpu.make_async_remote_copy(
                src_ref=_my_part(land_ref, 2 * px + py, c, True), dst_ref=_my_part(land_ref, 2 * px + py, 1 - c, True),
                send_sem=send_sem.at[k], recv_sem=recv_sem.at[k], device_id=(x, y, 1 - c), device_id_type=MESH)
            cp.wait_send()
            cp.wait_recv()

    return pl.pallas_call(
        body, name=name,
        out_shape=pltpu.HBM(land.shape, land.dtype),
        in_specs=[_HBM, _SEM, _SEM, _ANY], out_specs=_HBM,
        input_output_aliases={0: 0},
        compiler_params=pltpu.CompilerParams(has_side_effects=_EFFECT),
    )(land, send, recv, after)


def _scatter_start(g, name):
    def body(g_ref, land_ref, send, recv, g_out, land_out, token):
        del g_out, land_out
        x, y, c, peers = _place()
        for k, (px, py) in enumerate(peers):
            pltpu.make_async_remote_copy(
                src_ref=g_ref.at[2 * px + py], dst_ref=land_ref.at[k], send_sem=send.at[k], recv_sem=recv.at[k],
                device_id=(px, py, c), device_id_type=MESH).start()
        token[...] = jnp.zeros_like(token)

    land = lax.empty((3,) + g.shape[1:], g.dtype)
    return pl.pallas_call(
        body, name=name,
        out_shape=(pltpu.SemaphoreType.DMA((3,)), pltpu.SemaphoreType.DMA((3,)),
                   pltpu.HBM(g.shape, g.dtype), pltpu.HBM(land.shape, land.dtype),
                   jax.ShapeDtypeStruct((8, 128), F32)),
        in_specs=(_HBM, _HBM), out_specs=(_SEM, _SEM, _HBM, _HBM, pl.BlockSpec(memory_space=pltpu.VMEM)),
        input_output_aliases={0: 2, 1: 3},
        compiler_params=pltpu.CompilerParams(has_side_effects=_EFFECT),
    )(_in_hbm(g), _in_hbm(land))


def _scatter_wait(started, after):
    n = len(started)

    def body(*refs):
        gs, lands = refs[:n], refs[n:2 * n]
        sends, recvs = refs[2 * n:3 * n], refs[3 * n:4 * n]
        x, y, c, peers = _place()
        for a in range(n):
            for k, (px, py) in enumerate(peers):
                cp = pltpu.make_async_remote_copy(
                    src_ref=gs[a].at[2 * px + py], dst_ref=lands[a].at[k], send_sem=sends[a].at[k],
                    recv_sem=recvs[a].at[k], device_id=(px, py, c), device_id_type=MESH)
                cp.wait_send()
                cp.wait_recv()

    gs = [s[2] for s in started]
    lands = [s[3] for s in started]
    outs = pl.pallas_call(
        body, name="scatter_wait",
        out_shape=[pltpu.HBM(a.shape, a.dtype) for a in gs + lands],
        in_specs=[_HBM] * (2 * n) + [_SEM] * (2 * n) + [_ANY], out_specs=[_HBM] * (2 * n),
        input_output_aliases={i: i for i in range(2 * n)},
        compiler_params=pltpu.CompilerParams(has_side_effects=_EFFECT),
    )(*gs, *lands, *[s[0] for s in started], *[s[1] for s in started], after)
    return outs[:n], outs[n:]


def _sibling():
    return lax.axis_index("x"), lax.axis_index("y"), 1 - lax.axis_index("c")


def _swap_start(parts):
    n = len(parts)

    def body(*refs):
        ins, land = refs[:n], refs[n:2 * n]
        send, recv = refs[2 * n], refs[2 * n + 1]
        token = refs[4 * n + 2]
        for a in range(n):
            pltpu.make_async_remote_copy(
                src_ref=ins[a], dst_ref=land[a], send_sem=send.at[a], recv_sem=recv.at[a],
                device_id=_sibling(), device_id_type=MESH).start()
        token[...] = jnp.zeros_like(token)

    lands = [lax.empty(a.shape, a.dtype) for a in parts]
    outs = pl.pallas_call(
        body, name="swap_start",
        out_shape=[pltpu.SemaphoreType.DMA((n,))] * 2 + [pltpu.HBM(a.shape, a.dtype) for a in parts + lands]
        + [jax.ShapeDtypeStruct((8, 128), F32)],
        in_specs=[_HBM] * (2 * n),
        out_specs=[_SEM] * 2 + [_HBM] * (2 * n) + [pl.BlockSpec(memory_space=pltpu.VMEM)],
        input_output_aliases={i: 2 + i for i in range(2 * n)},
        compiler_params=pltpu.CompilerParams(has_side_effects=_EFFECT),
    )(*[_in_hbm(a) for a in parts + lands])
    return outs[0], outs[1], outs[2:2 + n], outs[2 + n:2 + 2 * n], outs[2 + 2 * n]


def _swap_wait(parts, lands, send, recv, after):
    n = len(parts)

    def body(*refs):
        ins, land = refs[:n], refs[n:2 * n]
        send_sem, recv_sem = refs[2 * n], refs[2 * n + 1]
        for a in range(n):
            cp = pltpu.make_async_remote_copy(
                src_ref=ins[a], dst_ref=land[a], send_sem=send_sem.at[a], recv_sem=recv_sem.at[a],
                device_id=_sibling(), device_id_type=MESH)
            cp.wait_send()
            cp.wait_recv()

    outs = pl.pallas_call(
        body, name="swap_wait",
        out_shape=[pltpu.HBM(a.shape, a.dtype) for a in list(parts) + list(lands)],
        in_specs=[_HBM] * (2 * n) + [_SEM] * 2 + [_ANY], out_specs=[_HBM] * (2 * n),
        input_output_aliases={i: i for i in range(2 * n)},
        compiler_params=pltpu.CompilerParams(has_side_effects=_EFFECT),
    )(*parts, *lands, send, recv, after)
    return outs[:n], outs[n:]


def _all_reduce_small(v):
    nd, r, w = v.shape
    assert nd == N_DEV

    def body(v_ref, out_ref, land_ref, red_ref, send1, recv1, send2, recv2):
        x, y, c = lax.axis_index("x"), lax.axis_index("y"), lax.axis_index("c")
        me = 4 * x + 2 * y + c
        peers = []
        for k in range(1, N_DEV):
            kx, ky, kc = (k >> 2) & 1, (k >> 1) & 1, k & 1
            peers.append((1 - x if kx else x, 1 - y if ky else y, 1 - c if kc else c))
        first = []
        for k, (px, py, pc) in enumerate(peers):
            cp = pltpu.make_async_remote_copy(
                src_ref=v_ref.at[4 * px + 2 * py + pc], dst_ref=land_ref.at[me], send_sem=send1.at[k],
                recv_sem=recv1.at[k], device_id=(px, py, pc), device_id_type=MESH)
            cp.start()
            first.append(cp)
        land_ref[me] = v_ref[me]
        for cp in first:
            cp.wait()
        acc = land_ref[0]
        for j in range(1, N_DEV):
            acc = acc + land_ref[j]
        red_ref[...] = acc
        second = []
        for k, (px, py, pc) in enumerate(peers):
            cp = pltpu.make_async_remote_copy(
                src_ref=red_ref, dst_ref=out_ref.at[me], send_sem=send2.at[k],
                recv_sem=recv2.at[k], device_id=(px, py, pc), device_id_type=MESH)
            cp.start()
            second.append(cp)
        out_ref[me] = acc
        for cp in second:
            cp.wait()

    vmem = pl.BlockSpec(memory_space=pltpu.VMEM)
    return pl.pallas_call(
        body, name="all_reduce_small",
        in_specs=[vmem], out_specs=vmem,
        out_shape=jax.ShapeDtypeStruct((nd, r, w), F32),
        scratch_shapes=[pltpu.VMEM((nd, r, w), F32), pltpu.VMEM((r, w), F32)]
        + [pltpu.SemaphoreType.DMA((N_DEV - 1,))] * 4,
        compiler_params=pltpu.CompilerParams(vmem_limit_bytes=VMEM_LIMIT),
    )(v)


def _adam_math(w, g, m, v):
    m = ADAM_B1 * m + (1.0 - ADAM_B1) * g
    v = ADAM_B2 * v + (1.0 - ADAM_B2) * (g * g)
    m_hat = m / (1.0 - ADAM_B1 ** ADAM_STEP)
    v_hat = v / (1.0 - ADAM_B2 ** ADAM_STEP)
    delta = -ADAM_LR * (m_hat / (jnp.sqrt(v_hat) + ADAM_EPS) + ADAM_WD * w)
    return delta, m, v


def _sum_blocks(own, got, chip, name):
    _, r, c = own.shape
    tm = _tile(r, ROWS, 16)

    def kern(chip_ref, own_ref, got_ref, o_ref):
        del chip_ref
        acc = own_ref[...].astype(F32)
        for k in range(3):
            acc = acc + got_ref[k].astype(F32)
        o_ref[...] = acc

    return pl.pallas_call(
        kern, name=name,
        grid_spec=pltpu.PrefetchScalarGridSpec(
            num_scalar_prefetch=1, grid=(r // tm,),
            in_specs=[pl.BlockSpec((None, tm, c), lambda i, ch: (ch[0], i, 0)),
                      pl.BlockSpec((3, tm, c), lambda i, ch: (0, i, 0))],
            out_specs=pl.BlockSpec((tm, c), lambda i, ch: (i, 0))),
        out_shape=jax.ShapeDtypeStruct((r, c), F32),
        compiler_params=_params(("parallel",)),
    )(chip, own, got)


def _adamw(parts, w, m, v, name):
    nl, r, c = w.shape
    assert len(parts) == nl
    tm = _tile(r, 128, 8)

    def kern(*refs):
        p_refs = refs[:2 * nl]
        w_ref, m_ref, v_ref, g_ref, d_ref, mo_ref, vo_ref = refs[2 * nl:]
        layer = pl.program_id(0)
        g = p_refs[0][...] + p_refs[1][...]
        for ll in range(1, nl):
            g = jnp.where(layer == ll, p_refs[2 * ll][...] + p_refs[2 * ll + 1][...], g)
        g_ref[...] = g
        d_ref[...], mo_ref[...], vo_ref[...] = _adam_math(w_ref[...], g, m_ref[...], v_ref[...])

    def part_spec(ll):
        return pl.BlockSpec((tm, c), lambda l, i: (jnp.where(l == ll, i, 0), 0))

    spec = pl.BlockSpec((None, tm, c), lambda l, i: (l, i, 0))
    return pl.pallas_call(
        kern, name=name, grid=(nl, r // tm),
        in_specs=[part_spec(ll) for ll in range(nl) for _ in range(2)] + [spec] * 3, out_specs=[spec] * 4,
        out_shape=[jax.ShapeDtypeStruct((nl, r, c), F32)] * 4,
        compiler_params=_params(("arbitrary", "parallel")),
    )(*[p for pair in parts for p in pair], w, m, v)


def _adamw_small(g, w, m, v, name):
    def kern(g_ref, w_ref, m_ref, v_ref, d_ref, mo_ref, vo_ref):
        d_ref[...], mo_ref[...], vo_ref[...] = _adam_math(w_ref[...], g_ref[...], m_ref[...], v_ref[...])

    return pl.pallas_call(
        kern, name=name,
        out_shape=[jax.ShapeDtypeStruct(g.shape, F32)] * 3,
        compiler_params=pltpu.CompilerParams(vmem_limit_bytes=VMEM_LIMIT),
    )(g, w, m, v)


def _pack(arrays, rows):
    flat = jnp.concatenate([a.reshape(-1).astype(F32) for a in arrays])
    return jnp.pad(flat, (0, rows * 128 - flat.shape[0])).reshape(rows, 128)


def _unpack(packed, shapes):
    flat = packed.reshape(-1)
    out, off = [], 0
    for shp in shapes:
        size = math.prod(shp)
        out.append(flat[off:off + size].reshape(shp))
        off += size
    return out


_REPLICATED = ["s5_lam_re", "s5_lam_im", "s5_log_dt", "s5_b_re", "s5_b_im", "s5_c_re", "s5_c_im",
               "ffn_norm", "b_norm_mix", "kv_norm", "final_norm"]
_CHIP_VECTORS = ["s5_d", "a_norm_mix"]
_BIG = ["s5_w_glu", "ffn_w_in", "ffn_w_out", "attn_w_q", "attn_w_o", "w_kv"]
_WEIGHT_ORDER = ["s5_lam_re", "s5_lam_im", "s5_log_dt", "s5_b_re", "s5_b_im", "s5_c_re", "s5_c_im", "s5_d",
                 "s5_w_glu", "a_norm_mix", "ffn_norm", "ffn_w_in", "ffn_w_out", "b_norm_mix", "attn_w_q",
                 "attn_w_o", "kv_norm", "w_kv", "final_norm"]


def _step(x, loss_target, w, m, v):
    s, d = x.shape[1], x.shape[2]
    chip = 2 * lax.axis_index("x") + lax.axis_index("y")

    col_sharded = dict(w_glu=("s5_w_glu", 0), w_in0=("ffn_w_in", 0), w_in1=("ffn_w_in", 1), w_q=("attn_w_q", 0))
    row_sharded = dict(w_out0=("ffn_w_out", 0), w_out1=("ffn_w_out", 1), w_o=("attn_w_o", 0), w_kv=("w_kv", 0))
    names = ["w_glu", "w_in0", "w_out0", "w_kv", "w_q", "w_o", "w_in1", "w_out1"]
    local = {**col_sharded, **row_sharded}

    def layers(a):
        return a.reshape((-1,) + a.shape[-2:])

    chip_arr = jnp.reshape(chip, (1,)).astype(jnp.int32)
    vector_lands = [lax.dynamic_update_slice(jnp.zeros((N_CHIPS,) + w[n].shape, F32), w[n][None], (chip, 0, 0))
                    for n in _CHIP_VECTORS]
    v_send, v_recv, v_land, v_token = _gather_start(vector_lands, "gather_start_vectors", False,
                                                    jnp.zeros((8, 128), F32))
    first = 2
    lands = [_cast_place(layers(w[local[n][0]]), local[n][1], chip_arr, f"cast_{n}") for n in names[:first]]
    send, recv, land_thru, first_token = _gather_start(lands, "gather_start_first", True, v_token)
    lands = [_cast_place(layers(w[local[n][0]]), local[n][1], chip_arr, f"cast_{n}") for n in names[first:]]
    *later, rest_token = _gather_start(lands, "gather_start_rest", True, first_token)
    send, recv, land_thru = [list(a) + list(b) for a, b in zip((send, recv, land_thru), later)]
    vectors = _gather_wait(v_land, v_send, v_recv, jnp.zeros((8, 128), F32), "gather_wait_vectors", False)
    s5_d_full = vectors[0].reshape(1, d)
    a_norm_full = vectors[1].reshape(1, d) + rest_token[0, 0]

    batches = dict(glu=["w_glu", "w_in0", "w_out0"], attention=["w_kv", "w_q", "w_o"], ffn1=["w_in1", "w_out1"])
    forwarded, arrived = {}, {}

    def stage(batch, after):
        idx = [names.index(n) for n in batches[batch]]
        got = _gather_wait([land_thru[i] for i in idx], [send[i] for i in idx], [recv[i] for i in idx], after,
                           f"gather_wait_{batch}", True)
        f_send, f_recv, f_land, token = _forward_start(got, f"forward_start_{batch}")
        forwarded.update(zip(batches[batch], zip(f_send, f_recv, f_land)))
        return token[0, 0]

    def weight(name, after):
        if name not in arrived:
            f_send, f_recv, f_land = forwarded[name]
            land = _forward_wait(f_land, f_send, f_recv, after, f"forward_wait_{name}")
            arrived[name] = land if name in col_sharded else land.reshape(1, -1, land.shape[-1])
        return arrived[name]

    started = {}

    def emit(name, dw):
        outs = _scatter_start(dw.reshape(N_CHIPS, -1, dw.shape[-1]), f"scatter_start_{name}")
        started[name] = outs[:4]
        return outs[4][0, 0]

    small = dict(lam_re=w["s5_lam_re"][0], lam_im=w["s5_lam_im"][0], log_dt=w["s5_log_dt"][0],
                 b_re=w["s5_b_re"][0], b_im=w["s5_b_im"][0], c_re=w["s5_c_re"][0], c_im=w["s5_c_im"][0],
                 s5_d=s5_d_full, a_norm=a_norm_full, ffn_norm0=w["ffn_norm"][0:1], ffn_norm1=w["ffn_norm"][1:2],
                 b_norm=w["b_norm_mix"], kv_norm=w["kv_norm"].reshape(1, d), final_norm=w["final_norm"].reshape(1, d))

    loss_rows, grad_x, d_small = _local_step(x[0], loss_target[0], small, weight, emit, stage)

    g4, got = _scatter_wait([started[n] for n in names], grad_x)
    partial = [_sum_blocks(o, r, chip_arr, f"sum_{n}") for n, o, r in zip(names, g4, got)]
    swap_send, swap_recv, partial, swap_land, swap_token = _swap_start(partial)
    result = {}

    rep_grads = [d_small["lam_re"], d_small["lam_im"], d_small["log_dt"], d_small["b_re"], d_small["b_im"],
                 d_small["c_re"], d_small["c_im"],
                 jnp.concatenate([d_small["ffn_norm0"], d_small["ffn_norm1"]], axis=0),
                 d_small["b_norm"], d_small["kv_norm"], d_small["final_norm"]]
    vec_grads = [d_small["s5_d"], d_small["a_norm"]]
    to_reduce = rep_grads + vec_grads + [jnp.sum(loss_rows).reshape(1) + swap_token[0, 0]]
    total = sum(math.prod(a.shape) for a in to_reduce)
    rows_per = -(-total // (N_DEV * 128 * 8)) * 8
    reduced = _all_reduce_small(_pack(to_reduce, N_DEV * rows_per).reshape(N_DEV, rows_per, 128))
    red = _unpack(reduced, [a.shape for a in to_reduce])
    loss = red[-1][0]
    g_small = dict(zip(_REPLICATED, [r.reshape(w[n].shape) for r, n in zip(red[:len(rep_grads)], _REPLICATED)]))
    for n, r in zip(_CHIP_VECTORS, red[len(rep_grads):-1]):
        g_small[n] = lax.dynamic_slice_in_dim(r.reshape(1, d), chip * (d // N_CHIPS), d // N_CHIPS, axis=1)
    small_names = _REPLICATED + _CHIP_VECTORS
    n_small = sum(math.prod(w[n].shape) for n in small_names)
    rows_small = -(-n_small // (128 * 8)) * 8
    packed = [_pack([src[n] for n in small_names], rows_small) for src in (g_small, w, m, v)]
    upd = _adamw_small(*packed, "adamw_small")
    shapes = [w[n].shape for n in small_names]
    for n, dl, mo, vo in zip(small_names, *[_unpack(u, shapes) for u in upd]):
        result[n] = [g_small[n], dl, mo, vo]

    partial, other = _swap_wait(partial, swap_land, swap_send, swap_recv, upd[0])
    part = dict(zip(names, zip(partial, other)))
    for name in _BIG:
        parts = [part[n] for n in sorted((n for n in names if local[n][0] == name), key=lambda n: local[n][1])]
        outs = _adamw(parts, layers(w[name]), layers(m[name]), layers(v[name]), f"adamw_{name}")
        result[name] = [o.reshape(w[name].shape) for o in outs]

    cols = [[result[n][t] for n in _WEIGHT_ORDER] for t in range(4)]
    return (loss, grad_x.reshape(x.shape), *cols[0], *cols[1], *cols[2], *cols[3])


def kernel(x, s5_lam_re, s5_lam_im, s5_log_dt, s5_b_re, s5_b_im, s5_c_re, s5_c_im, s5_d, s5_w_glu, a_norm_mix, ffn_norm, ffn_w_in, ffn_w_out, b_norm_mix, attn_w_q, attn_w_o, kv_norm, w_kv, final_norm, loss_target, m_s5_lam_re, m_s5_lam_im, m_s5_log_dt, m_s5_b_re, m_s5_b_im, m_s5_c_re, m_s5_c_im, m_s5_d, m_s5_w_glu, m_a_norm_mix, m_ffn_norm, m_ffn_w_in, m_ffn_w_out, m_b_norm_mix, m_attn_w_q, m_attn_w_o, m_kv_norm, m_w_kv, m_final_norm, v_s5_lam_re, v_s5_lam_im, v_s5_log_dt, v_s5_b_re, v_s5_b_im, v_s5_c_re, v_s5_c_im, v_s5_d, v_s5_w_glu, v_a_norm_mix, v_ffn_norm, v_ffn_w_in, v_ffn_w_out, v_b_norm_mix, v_attn_w_q, v_attn_w_o, v_kv_norm, v_w_kv, v_final_norm):
    w = dict(zip(_WEIGHT_ORDER, (s5_lam_re, s5_lam_im, s5_log_dt, s5_b_re, s5_b_im, s5_c_re, s5_c_im, s5_d, s5_w_glu, a_norm_mix, ffn_norm, ffn_w_in, ffn_w_out, b_norm_mix, attn_w_q, attn_w_o, kv_norm, w_kv, final_norm)))
    m = dict(zip(_WEIGHT_ORDER, (m_s5_lam_re, m_s5_lam_im, m_s5_log_dt, m_s5_b_re, m_s5_b_im, m_s5_c_re, m_s5_c_im, m_s5_d, m_s5_w_glu, m_a_norm_mix, m_ffn_norm, m_ffn_w_in, m_ffn_w_out, m_b_norm_mix, m_attn_w_q, m_attn_w_o, m_kv_norm, m_w_kv, m_final_norm)))
    v = dict(zip(_WEIGHT_ORDER, (v_s5_lam_re, v_s5_lam_im, v_s5_log_dt, v_s5_b_re, v_s5_b_im, v_s5_c_re, v_s5_c_im, v_s5_d, v_s5_w_glu, v_a_norm_mix, v_ffn_norm, v_ffn_w_in, v_ffn_w_out, v_b_norm_mix, v_attn_w_q, v_attn_w_o, v_kv_norm, v_w_kv, v_final_norm)))
    return _step(x, loss_target, w, m, v)
```

```python
import functools
import math

import jax
import jax.numpy as jnp
from jax import lax
from jax.experimental import pallas as pl
from jax.experimental.pallas import tpu as pltpu

F32 = jnp.float32
BF16 = jnp.bfloat16

S5_GROUP_CH = 16
S5_STATE = 64
GROUPS_PER_TILE = 8
HEAD_DIM = 128
N_KV_HEADS = 4
PATTERNS = ((128, 1), (512, 4), (2048, 16))
ATT_BLK = 128
EPS = 1e-6
NEG_INF = -1e30
SCAN_T = 1024
ADAM_LR = 0.001
ADAM_B1 = 0.9
ADAM_B2 = 0.999
ADAM_EPS = 1e-08
ADAM_WD = 0.01
ADAM_STEP = 10
N_CHIPS = 4
N_DEV = 8
VMEM_LIMIT = 56 * 1024 * 1024
MESH = pl.DeviceIdType.MESH
GELU_K = math.sqrt(2.0 / math.pi)
GELU_C = 0.044715


def _tile(n, pref, unit=128):
    if n <= pref:
        return n
    best = None
    t = unit
    while t <= pref:
        if n % t == 0:
            best = t
        t += unit
    assert best is not None, (n, pref, unit)
    return best


def _params(sem):
    return pltpu.CompilerParams(dimension_semantics=sem, vmem_limit_bytes=VMEM_LIMIT)


def _dot(a, b):
    return jnp.dot(a, b, preferred_element_type=F32)


def _dot_nt(a, b):
    return lax.dot_general(a, b, (((1,), (1,)), ((), ())), preferred_element_type=F32)


def _dot_tn(a, b):
    return lax.dot_general(a, b, (((0,), (0,)), ((), ())), preferred_element_type=F32)


def _mm_nn(a, w, out_dtype, name, tm=512, tn=1536, tk=2048):
    m, k = a.shape
    nb, k2, nq = w.shape
    assert k == k2
    tm, tn, tk = _tile(m, tm, 8), _tile(nq, tn), _tile(k, tk)
    per, nk = nq // tn, k // tk

    def kern(a_ref, w_ref, o_ref, *acc):
        p = _dot(a_ref[...], w_ref[...])
        if nk == 1:
            o_ref[...] = p.astype(o_ref.dtype)
        else:
            acc_ref, = acc
            kk = pl.program_id(2)

            @pl.when(kk == 0)
            def _():
                acc_ref[...] = p

            @pl.when(kk > 0)
            def _():
                acc_ref[...] += p

            @pl.when(kk == nk - 1)
            def _():
                o_ref[...] = acc_ref[...].astype(o_ref.dtype)

    return pl.pallas_call(
        kern, name=name, grid=(nb * per, m // tm, nk),
        in_specs=[pl.BlockSpec((tm, tk), lambda j, i, kk: (i, kk)),
                  pl.BlockSpec((None, tk, tn), lambda j, i, kk: (j // per, kk, j % per))],
        out_specs=pl.BlockSpec((tm, tn), lambda j, i, kk: (i, j)),
        out_shape=jax.ShapeDtypeStruct((m, nb * nq), out_dtype),
        scratch_shapes=[] if nk == 1 else [pltpu.VMEM((tm, tn), F32)],
        compiler_params=_params(("parallel", "parallel", "arbitrary")),
    )(a, w)


def _paired_block(r, per, nb):
    j = r // 2
    return (r % 2) * (nb // 2) + j // per, j % per


def _mm_nt(a, w, name, tm=512, tr=1536, tkc=1024, paired=False):
    m, n = a.shape
    nb, k, nq = w.shape
    assert n == nb * nq
    tm, tr, tkc = _tile(m, tm, 8), _tile(nq, tr), _tile(k, tkc)
    per = nq // tr

    def kern(a_ref, w_ref, o_ref):
        acc = None
        for r in range(nb * per):
            blk, tile = _paired_block(r, per, nb) if paired else (r // per, r % per)
            p = _dot_nt(a_ref[:, r * tr:(r + 1) * tr], w_ref[blk, :, tile * tr:(tile + 1) * tr])
            acc = p if acc is None else acc + p
        o_ref[...] = acc

    return pl.pallas_call(
        kern, name=name, grid=(k // tkc, m // tm),
        in_specs=[pl.BlockSpec((tm, n), lambda kc, i: (i, 0)),
                  pl.BlockSpec((nb, tkc, nq), lambda kc, i: (0, kc, 0))],
        out_specs=pl.BlockSpec((tm, tkc), lambda kc, i: (i, kc)),
        out_shape=jax.ShapeDtypeStruct((m, k), F32),
        compiler_params=_params(("parallel", "parallel")),
    )(a, w)


def _mm_tn(a, dy, nb, name, tkk=512, tn=1536, paired=False):
    s, k = a.shape
    s2, n = dy.shape
    assert s == s2 and n % nb == 0
    nq = n // nb
    tkk, tn = _tile(k, tkk), _tile(nq, tn)
    per = nq // tn

    def w_block(j):
        return _paired_block(j, per, nb) if paired else (j // per, j % per)

    def kern(a_ref, dy_ref, o_ref):
        o_ref[...] = _dot_tn(a_ref[...], dy_ref[...]).astype(o_ref.dtype)

    return pl.pallas_call(
        kern, name=name, grid=(nb * per, k // tkk),
        in_specs=[pl.BlockSpec((s, tkk), lambda j, kk: (0, kk)),
                  pl.BlockSpec((s, tn), lambda j, kk: (0, j))],
        out_specs=pl.BlockSpec((None, tkk, tn), lambda j, kk: (w_block(j)[0], kk, w_block(j)[1])),
        out_shape=jax.ShapeDtypeStruct((nb, k, nq), BF16),
        compiler_params=_params(("parallel", "parallel")),
    )(a, dy)


ROWS = 256


def _rms(x, g):
    r = lax.rsqrt(jnp.mean(x * x, axis=-1, keepdims=True) + EPS)
    return x * r * g


def _rms_bwd(x, g, dh):
    r = lax.rsqrt(jnp.mean(x * x, axis=-1, keepdims=True) + EPS)
    xh = x * r
    dgx = dh * g
    dx = r * (dgx - xh * jnp.mean(dgx * xh, axis=-1, keepdims=True))
    return dx, dh * xh


def _sigmoid(x):
    return 1.0 / (1.0 + jnp.exp(-x))


def _gelu(y):
    return 0.5 * y * (1.0 + jnp.tanh(GELU_K * (y + GELU_C * y * y * y)))


def _gelu_grad(y):
    t = jnp.tanh(GELU_K * (y + GELU_C * y * y * y))
    return 0.5 * (1.0 + t) + 0.5 * y * (1.0 - t * t) * GELU_K * (1.0 + 3.0 * GELU_C * y * y)


def _row_spec(tm, d, col=0):
    return pl.BlockSpec((tm, d), lambda i: (i, col))


def _vec_spec(d):
    return pl.BlockSpec((1, d), lambda i: (0, 0))


def _acc_rows(ref, val, i):
    s = jnp.sum(val, axis=0, keepdims=True)

    @pl.when(i == 0)
    def _():
        ref[...] = s

    @pl.when(i > 0)
    def _():
        ref[...] += s


def _norm_f32(x, g, name):
    s, d = x.shape
    tm = _tile(s, ROWS, 8)

    def kern(x_ref, g_ref, h_ref):
        h_ref[...] = _rms(x_ref[...], g_ref[...])

    return pl.pallas_call(
        kern, name=name, grid=(s // tm,),
        in_specs=[_row_spec(tm, d), _vec_spec(d)],
        out_specs=_row_spec(tm, d),
        out_shape=jax.ShapeDtypeStruct((s, d), F32),
        compiler_params=_params(("parallel",)),
    )(x, g)


def _glu_res_norm(vg, x, g, name):
    s, d = x.shape
    tm = _tile(s, ROWS, 16)

    def kern(val_ref, gate_ref, x_ref, g_ref, x1_ref, hf_ref):
        x1 = x_ref[...] + val_ref[...] * _sigmoid(gate_ref[...])
        x1_ref[...] = x1
        hf_ref[...] = _rms(x1, g_ref[...]).astype(BF16)

    return pl.pallas_call(
        kern, name=name, grid=(s // tm,),
        in_specs=[_row_spec(tm, d, 0), _row_spec(tm, d, 1), _row_spec(tm, d), _vec_spec(d)],
        out_specs=[_row_spec(tm, d), _row_spec(tm, d)],
        out_shape=[jax.ShapeDtypeStruct((s, d), F32), jax.ShapeDtypeStruct((s, d), BF16)],
        compiler_params=_params(("parallel",)),
    )(vg, vg, x, g)


FFN_TILE = 1408


def _ffn_in_act(hf, w_in, name, tm=512):
    s, k = hf.shape
    nb, _, nq = w_in.shape
    tm, tn = _tile(s, tm, 16), _tile(nq, FFN_TILE)
    per = nq // tn
    nf = (nb // 2) * per

    def kern(h_ref, wg_ref, wu_ref, a_ref, gu_ref):
        h = h_ref[...]
        g = _dot(h, wg_ref[...])
        u = _dot(h, wu_ref[...])
        sg = _sigmoid(g)
        silu = g * sg
        a_ref[...] = (silu * u).astype(BF16)
        gu_ref[:, :tn] = (u * (sg + silu * (1.0 - sg))).astype(BF16)
        gu_ref[:, tn:] = silu.astype(BF16)

    return pl.pallas_call(
        kern, name=name, grid=(nf, s // tm),
        in_specs=[pl.BlockSpec((tm, k), lambda j, i: (i, 0)),
                  pl.BlockSpec((None, k, tn), lambda j, i: (j // per, 0, j % per)),
                  pl.BlockSpec((None, k, tn), lambda j, i: (nb // 2 + j // per, 0, j % per))],
        out_specs=[pl.BlockSpec((tm, tn), lambda j, i: (i, j)),
                   pl.BlockSpec((tm, 2 * tn), lambda j, i: (i, j))],
        out_shape=[jax.ShapeDtypeStruct((s, nf * tn), BF16), jax.ShapeDtypeStruct((s, 2 * nf * tn), BF16)],
        compiler_params=_params(("parallel", "parallel")),
    )(hf, w_in, w_in)


def _ffn_out_dx_act(dresb, w_out, gu, name, tm=512):
    s, d = dresb.shape
    f = w_out.shape[1]
    tm = _tile(s, tm, 16)
    tn = _tile(f // 2, FFN_TILE)

    def kern(d_ref, w_ref, gu_ref, o_ref):
        da = _dot_nt(d_ref[...], w_ref[...])
        o_ref[:, :tn] = (da * gu_ref[:, :tn].astype(F32)).astype(BF16)
        o_ref[:, tn:] = (da * gu_ref[:, tn:].astype(F32)).astype(BF16)

    pair = pl.BlockSpec((tm, 2 * tn), lambda j, i: (i, j))
    return pl.pallas_call(
        kern, name=name, grid=(f // tn, s // tm),
        in_specs=[pl.BlockSpec((tm, d), lambda j, i: (i, 0)),
                  pl.BlockSpec((None, tn, d), lambda j, i: (0, j, 0)), pair],
        out_specs=pair,
        out_shape=jax.ShapeDtypeStruct((s, 2 * f), BF16),
        compiler_params=_params(("parallel", "parallel")),
    )(dresb, w_out, gu)


def _res_norm(o, x, gains, name):
    s, d = x.shape
    tm = _tile(s, ROWS, 16)
    ng = len(gains)

    def kern(o_ref, x_ref, *refs):
        xn = x_ref[...] + o_ref[...]
        refs[ng][...] = xn
        for t in range(ng):
            refs[ng + 1 + t][...] = _rms(xn, refs[t][...]).astype(BF16)

    return pl.pallas_call(
        kern, name=name, grid=(s // tm,),
        in_specs=[_row_spec(tm, d), _row_spec(tm, d)] + [_vec_spec(d)] * ng,
        out_specs=[_row_spec(tm, d)] * (1 + ng),
        out_shape=[jax.ShapeDtypeStruct((s, d), F32)] + [jax.ShapeDtypeStruct((s, d), BF16)] * ng,
        compiler_params=_params(("parallel",)),
    )(o, x, *gains)


def _loss_head(o, x, g, target, name):
    s, d = x.shape
    tm = _tile(s, ROWS, 16)

    def kern(o_ref, x_ref, g_ref, t_ref, dx_ref, dxb_ref, loss_ref, dg_ref):
        i = pl.program_id(0)
        x4 = x_ref[...] + o_ref[...]
        gg = g_ref[...]
        diff = _rms(x4, gg) - t_ref[...]
        dx, dgr = _rms_bwd(x4, gg, diff * (1.0 / d))
        dx_ref[...] = dx
        dxb_ref[...] = dx.astype(BF16)
        _acc_rows(loss_ref, diff * diff * (0.5 / d), i)
        _acc_rows(dg_ref, dgr, i)

    return pl.pallas_call(
        kern, name=name, grid=(s // tm,),
        in_specs=[_row_spec(tm, d), _row_spec(tm, d), _vec_spec(d), _row_spec(tm, d)],
        out_specs=[_row_spec(tm, d), _row_spec(tm, d), _vec_spec(d), _vec_spec(d)],
        out_shape=[jax.ShapeDtypeStruct((s, d), F32), jax.ShapeDtypeStruct((s, d), BF16),
                   jax.ShapeDtypeStruct((1, d), F32), jax.ShapeDtypeStruct((1, d), F32)],
        compiler_params=_params(("arbitrary",)),
    )(o, x, g, target)


def _norm_bwd(x, gains, dhs, dres, name):
    s, d = x.shape
    tm = _tile(s, ROWS, 16)
    ng = len(gains)

    def kern(x_ref, dres_ref, *refs):
        i = pl.program_id(0)
        x_ = x_ref[...]
        acc = dres_ref[...]
        for t in range(ng):
            dx, dgr = _rms_bwd(x_, refs[t][...], refs[ng + t][...])
            acc = acc + dx
            _acc_rows(refs[2 * ng + 2 + t], dgr, i)
        refs[2 * ng][...] = acc
        refs[2 * ng + 1][...] = acc.astype(BF16)

    return pl.pallas_call(
        kern, name=name, grid=(s // tm,),
        in_specs=[_row_spec(tm, d), _row_spec(tm, d)] + [_vec_spec(d)] * ng + [_row_spec(tm, d)] * ng,
        out_specs=[_row_spec(tm, d), _row_spec(tm, d)] + [_vec_spec(d)] * ng,
        out_shape=[jax.ShapeDtypeStruct((s, d), F32), jax.ShapeDtypeStruct((s, d), BF16)]
        + [jax.ShapeDtypeStruct((1, d), F32)] * ng,
        compiler_params=_params(("arbitrary",)),
    )(x, dres, *gains, *dhs)


def _glu_bwd(dmix, vg, name):
    s, d = dmix.shape
    tm = _tile(s, ROWS, 16)

    def kern(dm_ref, val_ref, gate_ref, o_ref):
        dm = dm_ref[...]
        sg = _sigmoid(gate_ref[...])
        o_ref[:, :d] = (dm * sg).astype(BF16)
        o_ref[:, d:] = (dm * val_ref[...] * sg * (1.0 - sg)).astype(BF16)

    return pl.pallas_call(
        kern, name=name, grid=(s // tm,),
        in_specs=[_row_spec(tm, d), _row_spec(tm, d, 0), _row_spec(tm, d, 1)],
        out_specs=_row_spec(tm, 2 * d),
        out_shape=jax.ShapeDtypeStruct((s, 2 * d), BF16),
        compiler_params=_params(("parallel",)),
    )(dmix, vg, vg)


SUB = 8


def _local_scan(vr, vi, pre_ref, pim_ref, reverse):
    sub = lax.broadcasted_iota(jnp.int32, vr.shape, 1)
    sign = -1.0 if reverse else 1.0
    for sh in (1, 2, 4):
        ar = pre_ref[sh - 1:sh, :][None]
        ai = sign * pim_ref[sh - 1:sh, :][None]
        keep = sub < SUB - sh if reverse else sub >= sh
        sr = jnp.where(keep, pltpu.roll(vr, SUB - sh if reverse else sh, 1), 0.0)
        si = jnp.where(keep, pltpu.roll(vi, SUB - sh if reverse else sh, 1), 0.0)
        vr, vi = vr + ar * sr - ai * si, vi + ar * si + ai * sr
    return vr, vi


def _s5_fwd(u, bbd_re, bbd_im, cbd_re, cbd_im, pw_re, pw_im, dskip, name):
    s, d = u.shape
    nt, cw, lw = bbd_re.shape
    t = _tile(s, SCAN_T, SUB)
    nc = s // t
    ng = t // SUB

    def kern(u_ref, bre_ref, bim_ref, cre_ref, cim_ref, pre_ref, pim_ref, d_ref,
             xr_ref, xi_ref, y_ref, z_ref, car_ref, cai_ref):
        c = pl.program_id(1)

        @pl.when(c == 0)
        def _():
            car_ref[...] = jnp.zeros_like(car_ref)
            cai_ref[...] = jnp.zeros_like(cai_ref)

        u_ = u_ref[...]
        ub = u_.astype(BF16)
        vr, vi = _local_scan(_dot(ub, bre_ref[...]).reshape(ng, SUB, lw),
                             _dot(ub, bim_ref[...]).reshape(ng, SUB, lw), pre_ref, pim_ref, False)
        cr = car_ref[...]
        ci = cai_ref[...]
        pr = pre_ref[...]
        pi = pim_ref[...]
        for gidx in range(ng):
            rows = slice(gidx * SUB, (gidx + 1) * SUB)
            gr = vr[gidx] + pr * cr - pi * ci
            gi = vi[gidx] + pr * ci + pi * cr
            xr_ref[rows, :] = gr
            xi_ref[rows, :] = gi
            cr, ci = gr[SUB - 1:SUB, :], gi[SUB - 1:SUB, :]
        car_ref[...] = cr
        cai_ref[...] = ci
        y = (_dot(xr_ref[...].astype(BF16), cre_ref[...]) - _dot(xi_ref[...].astype(BF16), cim_ref[...])
             + d_ref[...] * u_)
        y_ref[...] = y
        z_ref[...] = _gelu(y).astype(BF16)

    tok = pl.BlockSpec((t, cw), lambda j, c: (c, j))
    st = pl.BlockSpec((t, lw), lambda j, c: (c, j))
    return pl.pallas_call(
        kern, name=name, grid=(nt, nc),
        in_specs=[tok,
                  pl.BlockSpec((None, cw, lw), lambda j, c: (j, 0, 0)),
                  pl.BlockSpec((None, cw, lw), lambda j, c: (j, 0, 0)),
                  pl.BlockSpec((None, lw, cw), lambda j, c: (j, 0, 0)),
                  pl.BlockSpec((None, lw, cw), lambda j, c: (j, 0, 0)),
                  pl.BlockSpec((SUB, lw), lambda j, c: (0, j)),
                  pl.BlockSpec((SUB, lw), lambda j, c: (0, j)),
                  pl.BlockSpec((1, cw), lambda j, c: (0, j))],
        out_specs=[st, st, tok, tok],
        out_shape=[jax.ShapeDtypeStruct((s, nt * lw), F32), jax.ShapeDtypeStruct((s, nt * lw), F32),
                   jax.ShapeDtypeStruct((s, d), F32), jax.ShapeDtypeStruct((s, d), BF16)],
        scratch_shapes=[pltpu.VMEM((1, lw), F32), pltpu.VMEM((1, lw), F32)],
        compiler_params=_params(("parallel", "arbitrary")),
    )(u, bbd_re, bbd_im, cbd_re, cbd_im, pw_re, pw_im, dskip)


def _s5_bwd(dz, y, u, xs_re, xs_im, bbd_re, bbd_im, cbd_re, cbd_im, pw_re, pw_im, pf_re, pf_im, dskip, name):
    s, d = u.shape
    nt, cw, lw = bbd_re.shape
    t = _tile(s, SCAN_T, SUB)
    nc = s // t
    ng = t // SUB

    def kern(dz_ref, y_ref, u_ref, xr_ref, xi_ref, bre_ref, bim_ref, cre_ref, cim_ref,
             pre_ref, pim_ref, fre_ref, fim_ref, d_ref,
             du_ref, dd_ref, dcr_ref, dci_ref, dbr_ref, dbi_ref, dar_ref, dai_ref,
             car_ref, cai_ref, gr_ref, gi_ref):
        c = pl.program_id(1)

        @pl.when(c == 0)
        def _():
            car_ref[...] = jnp.zeros_like(car_ref)
            cai_ref[...] = jnp.zeros_like(cai_ref)

        u_ = u_ref[...]
        ub = u_.astype(BF16)
        dy = dz_ref[...] * _gelu_grad(y_ref[...])
        dyb = dy.astype(BF16)
        vr, vi = _local_scan(_dot_nt(dyb, cre_ref[...]).reshape(ng, SUB, lw),
                             (-_dot_nt(dyb, cim_ref[...])).reshape(ng, SUB, lw), pre_ref, pim_ref, True)
        later_r = car_ref[...]
        later_i = cai_ref[...]
        cr, ci = later_r, later_i
        fr = fre_ref[...]
        fi = fim_ref[...]
        for gidx in reversed(range(ng)):
            rows = slice(gidx * SUB, (gidx + 1) * SUB)
            ar = vr[gidx] + fr * cr + fi * ci
            ai = vi[gidx] + fr * ci - fi * cr
            gr_ref[rows, :] = ar
            gi_ref[rows, :] = ai
            cr, ci = ar[0:1, :], ai[0:1, :]
        car_ref[...] = cr
        cai_ref[...] = ci
        gr = gr_ref[...]
        gi = gi_ref[...]
        rows = lax.broadcasted_iota(jnp.int32, (t, lw), 0)
        gsr = jnp.where(rows < t - 1, pltpu.roll(gr, t - 1, 0), later_r)
        gsi = jnp.where(rows < t - 1, pltpu.roll(gi, t - 1, 0), later_i)
        xr = xr_ref[...]
        xi = xi_ref[...]
        dar = jnp.sum(gsr * xr + gsi * xi, axis=0, keepdims=True)
        dai = jnp.sum(gsi * xr - gsr * xi, axis=0, keepdims=True)
        grb = gr.astype(BF16)
        gib = gi.astype(BF16)
        dbr = _dot_tn(ub, grb)
        dbi = _dot_tn(ub, gib)
        dcr = _dot_tn(dyb, xr.astype(BF16))
        dci = _dot_tn(dyb, xi.astype(BF16))
        du_ref[...] = dy * d_ref[...] + _dot_nt(grb, bre_ref[...]) + _dot_nt(gib, bim_ref[...])
        ddv = jnp.sum(dy * u_, axis=0, keepdims=True)

        @pl.when(c == 0)
        def _():
            dd_ref[...] = ddv
            dcr_ref[...] = dcr
            dci_ref[...] = dci
            dbr_ref[...] = dbr
            dbi_ref[...] = dbi
            dar_ref[...] = dar
            dai_ref[...] = dai

        @pl.when(c > 0)
        def _():
            dd_ref[...] += ddv
            dcr_ref[...] += dcr
            dci_ref[...] += dci
            dbr_ref[...] += dbr
            dbi_ref[...] += dbi
            dar_ref[...] += dar
            dai_ref[...] += dai

    tok = pl.BlockSpec((t, cw), lambda j, c: (nc - 1 - c, j))
    st = pl.BlockSpec((t, lw), lambda j, c: (nc - 1 - c, j))
    wb = pl.BlockSpec((None, cw, lw), lambda j, c: (j, 0, 0))
    wc = pl.BlockSpec((None, lw, cw), lambda j, c: (j, 0, 0))
    pw = pl.BlockSpec((SUB, lw), lambda j, c: (0, j))
    vec_c = pl.BlockSpec((1, cw), lambda j, c: (0, j))
    vec_l = pl.BlockSpec((1, lw), lambda j, c: (0, j))
    return pl.pallas_call(
        kern, name=name, grid=(nt, nc),
        in_specs=[tok, tok, tok, st, st, wb, wb, wc, wc, pw, pw, pw, pw, vec_c],
        out_specs=[tok, vec_c, wb, wb, wb, wb, vec_l, vec_l],
        out_shape=[jax.ShapeDtypeStruct((s, d), F32), jax.ShapeDtypeStruct((1, d), F32)]
        + [jax.ShapeDtypeStruct((nt, cw, lw), F32)] * 4
        + [jax.ShapeDtypeStruct((1, nt * lw), F32)] * 2,
        scratch_shapes=[pltpu.VMEM((1, lw), F32), pltpu.VMEM((1, lw), F32),
                        pltpu.VMEM((t, lw), F32), pltpu.VMEM((t, lw), F32)],
        compiler_params=_params(("parallel", "arbitrary")),
    )(dz, y, u, xs_re, xs_im, bbd_re, bbd_im, cbd_re, cbd_im, pw_re, pw_im, pf_re, pf_im, dskip)


def _s5_discretize(lam_re, lam_im, log_dt, b_re, b_im):
    dt = jnp.exp(log_dt)[:, None]
    mag = jnp.exp(lam_re * dt)
    ang = lam_im * dt
    lb_re = mag * jnp.cos(ang)
    lb_im = mag * jnp.sin(ang)
    nr = lb_re - 1.0
    den = lam_re * lam_re + lam_im * lam_im
    f_re = (nr * lam_re + lb_im * lam_im) / den
    f_im = (lb_im * lam_re - nr * lam_im) / den
    bb_re = f_re[..., None] * b_re - f_im[..., None] * b_im
    bb_im = f_re[..., None] * b_im + f_im[..., None] * b_re
    return lb_re, lb_im, bb_re, bb_im


def _block_diag(w):
    nt, ng, a, b = w.shape
    eye = jnp.eye(ng, dtype=w.dtype)
    return (w[:, :, :, None, :] * eye[None, :, None, :, None]).reshape(nt, ng * a, ng * b)


def _block_diag_take(w, a, b):
    nt = w.shape[0]
    ng = GROUPS_PER_TILE
    w5 = w.reshape(nt, ng, a, ng, b)
    on_diagonal = jnp.eye(ng, dtype=bool)[None, :, None, :, None]
    return jnp.sum(jnp.where(on_diagonal, w5, 0.0), axis=3).reshape(nt * ng, a, b)


def _att_combine(outs, lses, name):
    s, d = outs[0].shape
    tm = _tile(s, ROWS, 16)
    ng = len(outs)

    def kern(*refs):
        for h in range(d // HEAD_DIM):
            cols = slice(h * HEAD_DIM, (h + 1) * HEAD_DIM)
            ls = [refs[ng + t][:, h:h + 1] for t in range(ng)]
            mx = functools.reduce(jnp.maximum, ls)
            es = [jnp.exp(l - mx) for l in ls]
            den = functools.reduce(lambda a, b: a + b, es)
            o = functools.reduce(lambda a, b: a + b, [es[t] / den * refs[t][:, cols] for t in range(ng)])
            refs[2 * ng][:, cols] = o
            refs[2 * ng + 1][:, cols] = o.astype(BF16)

    return pl.pallas_call(
        kern, name=name, grid=(s // tm,),
        in_specs=[_row_spec(tm, d)] * ng + [_row_spec(tm, d // HEAD_DIM)] * ng,
        out_specs=[_row_spec(tm, d)] * 2,
        out_shape=[jax.ShapeDtypeStruct((s, d), F32), jax.ShapeDtypeStruct((s, d), BF16)],
        compiler_params=_params(("parallel",)),
    )(*outs, *lses)


DIL_MAX = 16


def _slab(r):
    return 4 * (r % 4) + r // 4


def _to_slabs(xs, name):
    s, w = xs[0].shape
    n = s // DIL_MAX
    nx = len(xs)

    def kern(*refs):
        o_ref = refs[nx]
        for r in range(DIL_MAX):
            rows = [x_ref[pl.ds(r, n, stride=DIL_MAX), :] for x_ref in refs[:nx]]
            o_ref[_slab(r) * n:(_slab(r) + 1) * n, :] = functools.reduce(lambda a, b: a + b, rows)

    spec = pl.BlockSpec((s, 128), lambda i: (0, i))
    return pl.pallas_call(
        kern, name=name, grid=(w // 128,), in_specs=[spec] * nx, out_specs=spec,
        out_shape=jax.ShapeDtypeStruct((s, w), F32), compiler_params=_params(("parallel",)),
    )(*xs)


def _from_slabs(x, name):
    s, w = x.shape
    n = s // DIL_MAX

    def kern(x_ref, o_ref, ob_ref):
        for r in range(DIL_MAX):
            o_ref[pl.ds(r, n, stride=DIL_MAX), :] = x_ref[_slab(r) * n:(_slab(r) + 1) * n, :]
        ob_ref[...] = o_ref[...].astype(BF16)

    spec = pl.BlockSpec((s, 128), lambda i: (0, i))
    return pl.pallas_call(
        kern, name=name, grid=(w // 128,), in_specs=[spec], out_specs=[spec, spec],
        out_shape=[jax.ShapeDtypeStruct((s, w), F32), jax.ShapeDtypeStruct((s, w), BF16)],
        compiler_params=_params(("parallel",)),
    )(x)


def _norm_bf16(x, gains, name):
    s, d = x.shape
    tm = _tile(s, ROWS, 16)
    ng = len(gains)

    def kern(x_ref, *refs):
        x_ = x_ref[...]
        for t in range(ng):
            refs[ng + t][...] = _rms(x_, refs[t][...]).astype(BF16)

    return pl.pallas_call(
        kern, name=name, grid=(s // tm,),
        in_specs=[_row_spec(tm, d)] + [_vec_spec(d)] * ng, out_specs=[_row_spec(tm, d)] * ng,
        out_shape=[jax.ShapeDtypeStruct((s, d), BF16)] * ng,
        compiler_params=_params(("parallel",)),
    )(x, *gains)


def _att_geometry(dil, s):
    g = DIL_MAX // dil
    return s // DIL_MAX, g, max(ATT_BLK // g, 16)


def _att_mask(g, j, jb, rep):
    b = g * j

    def pos(i):
        sl, jj = i // j, i % j
        off = {1: 0, 4: sl, 16: sl // 4 + 4 * (sl % 4)}[g]
        return g * jj + off

    qi = lax.broadcasted_iota(jnp.int32, (rep * b, 2 * b), 0) % b
    ki = lax.broadcasted_iota(jnp.int32, (rep * b, 2 * b), 1)
    prev = ki < b
    dist = pos(qi) - pos(ki % b) + jnp.where(prev, b, 0)
    return (dist >= 0) & (dist <= ATT_BLK) & (jnp.logical_not(prev) | (jb > 0))


def _stack_heads(x, rep):
    return jnp.concatenate([x[:, h * HEAD_DIM:(h + 1) * HEAD_DIM] for h in range(rep)], axis=0)


def _att_fwd(q, kv, gi, dil, nh, name):
    s = q.shape[0]
    n, g, j = _att_geometry(dil, s)
    b = g * j
    ng = q.shape[1] // (nh * HEAD_DIM)
    rep = nh // N_KV_HEADS
    qw = rep * HEAD_DIM
    scale = HEAD_DIM ** -0.5
    q3 = q.reshape(DIL_MAX, n, q.shape[1])
    kv3 = kv.reshape(DIL_MAX, n, kv.shape[1])

    kw = N_KV_HEADS * HEAD_DIM

    def kern(q_ref, kc_ref, kp_ref, vc_ref, vp_ref, o_ref, l_ref, lw_ref):
        jb = pl.program_id(1)
        mask = _att_mask(g, j, jb, rep)
        for kh in range(N_KV_HEADS):
            hs = slice(kh * HEAD_DIM, (kh + 1) * HEAD_DIM)
            kc, kp, vc, vp = [r[:, :, hs].reshape(b, HEAD_DIM) for r in (kc_ref, kp_ref, vc_ref, vp_ref)]
            k2 = jnp.concatenate([kp, kc], axis=0)
            v2 = jnp.concatenate([vp, vc], axis=0)
            qs = _stack_heads(q_ref[:, :, kh * qw:(kh + 1) * qw].reshape(b, qw), rep)
            sc = jnp.where(mask, _dot_nt(qs, k2) * scale, NEG_INF)
            m = jnp.max(sc, axis=-1, keepdims=True)
            p = jnp.exp(sc - m)
            l = jnp.sum(p, axis=-1, keepdims=True)
            out = _dot((p / l).astype(BF16), v2)
            lse = m + jnp.log(l)
            wide = jnp.broadcast_to(lse, (rep * b, HEAD_DIM))
            for h in range(rep):
                cols = slice(kh * qw + h * HEAD_DIM, kh * qw + (h + 1) * HEAD_DIM)
                o_ref[:, :, cols] = out[h * b:(h + 1) * b].reshape(g, j, HEAD_DIM)
                lw_ref[:, :, cols] = wide[h * b:(h + 1) * b].reshape(g, j, HEAD_DIM)
                l_ref[:, :, kh * rep + h:kh * rep + h + 1] = lse[h * b:(h + 1) * b].reshape(g, j, 1)

    def kv_spec(col, prev):
        if prev:
            return pl.BlockSpec((g, j, kw), lambda c, jb: (c, jnp.maximum(jb - 1, 0), col))
        return pl.BlockSpec((g, j, kw), lambda c, jb: (c, jb, col))

    out_spec = pl.BlockSpec((g, j, nh * HEAD_DIM), lambda c, jb: (c, jb, 0))
    out, lse, lse_wide = pl.pallas_call(
        kern, name=name, grid=(dil, n // j),
        in_specs=[pl.BlockSpec((g, j, nh * HEAD_DIM), lambda c, jb: (c, jb, gi)),
                  kv_spec(0, False), kv_spec(0, True), kv_spec(1, False), kv_spec(1, True)],
        out_specs=[out_spec, pl.BlockSpec((g, j, nh), lambda c, jb: (c, jb, 0)), out_spec],
        out_shape=[jax.ShapeDtypeStruct((DIL_MAX, n, nh * HEAD_DIM), F32),
                   jax.ShapeDtypeStruct((DIL_MAX, n, nh), F32),
                   jax.ShapeDtypeStruct((DIL_MAX, n, nh * HEAD_DIM), F32)],
        compiler_params=_params(("parallel", "arbitrary")),
    )(q3, kv3, kv3, kv3, kv3)
    del ng
    return out.reshape(s, nh * HEAD_DIM), lse.reshape(s, nh), lse_wide.reshape(s, nh * HEAD_DIM)


def _att_bwd(q, kv, do, o, lses, dq_all, gi, dil, nh, name):
    s = q.shape[0]
    n, g, j = _att_geometry(dil, s)
    b = g * j
    nb = n // j
    ng = len(lses)
    rep = nh // N_KV_HEADS
    qw = rep * HEAD_DIM
    dm = nh * HEAD_DIM
    scale = HEAD_DIM ** -0.5
    q3 = q.reshape(DIL_MAX, n, q.shape[1])
    kv3 = kv.reshape(DIL_MAX, n, kv.shape[1])
    wide = [a.reshape(DIL_MAX, n, dm) for a in (do, o, *lses)]
    kw = N_KV_HEADS * HEAD_DIM

    def kern(q_ref, kc_ref, kp_ref, vc_ref, vp_ref, do_ref, o_ref, *refs):
        l_refs = refs[:ng]
        dq_ref, dk_ref, dv_ref, ck_ref, cv_ref = refs[ng + 1:]
        jb = pl.program_id(1)

        @pl.when(jb == 0)
        def _():
            ck_ref[...] = jnp.zeros_like(ck_ref)
            cv_ref[...] = jnp.zeros_like(cv_ref)

        @pl.when(jb < nb)
        def _():
            mask = _att_mask(g, j, jb, rep)
            for kh in range(N_KV_HEADS):
                hs = slice(kh * HEAD_DIM, (kh + 1) * HEAD_DIM)
                ws = slice(kh * qw, (kh + 1) * qw)

                def stacked(ref):
                    return _stack_heads(ref[:, :, ws].reshape(b, qw), rep)

                kc, kp, vc, vp = [r[:, :, hs].reshape(b, HEAD_DIM) for r in (kc_ref, kp_ref, vc_ref, vp_ref)]
                k2 = jnp.concatenate([kp, kc], axis=0)
                v2 = jnp.concatenate([vp, vc], axis=0)
                qs = stacked(q_ref)
                ls = [stacked(r)[:, 0:1] for r in l_refs]
                mx = functools.reduce(jnp.maximum, ls)
                den = functools.reduce(lambda a, c: a + c, [jnp.exp(l - mx) for l in ls])
                lse_g = ls[gi]
                w = jnp.exp(lse_g - mx) / den
                do_ = stacked(do_ref)
                ct = w * jnp.sum(do_ * stacked(o_ref), axis=-1, keepdims=True)
                dob = (w * do_).astype(BF16)
                p = jnp.exp(jnp.where(mask, _dot_nt(qs, k2) * scale, NEG_INF) - lse_g)
                ds = (p * (_dot_nt(dob, v2) - ct) * scale).astype(BF16)
                dq = (_dot(ds, k2)).astype(BF16)
                for h in range(rep):
                    cols = slice(kh * qw + h * HEAD_DIM, kh * qw + (h + 1) * HEAD_DIM)
                    dq_ref[:, :, cols] = dq[h * b:(h + 1) * b].reshape(g, j, HEAD_DIM)
                dk2 = _dot_tn(ds, qs)
                dv2 = _dot_tn(p.astype(BF16), dob)
                dk_ref[:, :, hs] = (ck_ref[:, hs] + dk2[:b]).reshape(g, j, HEAD_DIM)
                dv_ref[:, :, hs] = (cv_ref[:, hs] + dv2[:b]).reshape(g, j, HEAD_DIM)
                ck_ref[:, hs] = dk2[b:]
                cv_ref[:, hs] = dv2[b:]

        @pl.when(jb == nb)
        def _():
            dk_ref[...] = ck_ref[...].reshape(g, j, kw)
            dv_ref[...] = cv_ref[...].reshape(g, j, kw)

    def jq(jb):
        return jnp.minimum(jb, nb - 1)

    def kv_spec(col, prev):
        if prev:
            return pl.BlockSpec((g, j, kw), lambda c, jb: (c, jnp.maximum(jq(jb) - 1, 0), col))
        return pl.BlockSpec((g, j, kw), lambda c, jb: (c, jq(jb), col))

    wide_spec = pl.BlockSpec((g, j, dm), lambda c, jb: (c, jq(jb), 0))
    dkv_spec = pl.BlockSpec((g, j, kw), lambda c, jb: (c, jnp.maximum(jb - 1, 0), 0))
    dq, dk, dv = pl.pallas_call(
        kern, name=name, grid=(dil, nb + 1),
        in_specs=[pl.BlockSpec((g, j, dm), lambda c, jb: (c, jq(jb), gi)),
                  kv_spec(0, False), kv_spec(0, True), kv_spec(1, False), kv_spec(1, True)]
        + [wide_spec] * (2 + ng) + [_ANY],
        out_specs=[pl.BlockSpec((g, j, dm), lambda c, jb: (c, jq(jb), gi)), dkv_spec, dkv_spec],
        out_shape=[jax.ShapeDtypeStruct((DIL_MAX, n, dq_all.shape[1]), BF16),
                   jax.ShapeDtypeStruct((DIL_MAX, n, kw), F32),
                   jax.ShapeDtypeStruct((DIL_MAX, n, kw), F32)],
        scratch_shapes=[pltpu.VMEM((b, kw), F32), pltpu.VMEM((b, kw), F32)],
        input_output_aliases={7 + ng: 0},
        compiler_params=_params(("parallel", "arbitrary")),
    )(q3, kv3, kv3, kv3, kv3, *wide, dq_all.reshape(DIL_MAX, n, dq_all.shape[1]))
    return dq.reshape(s, -1), dk.reshape(s, -1), dv.reshape(s, -1)


def _dkv_sum(dks, dvs, name):
    s, w = dks[0].shape
    tm = _tile(s, ROWS, 16)
    ng = len(dks)

    def kern(*refs):
        o_ref = refs[2 * ng]
        o_ref[:, :w] = functools.reduce(lambda a, b: a + b, [refs[t][...] for t in range(ng)]).astype(BF16)
        o_ref[:, w:] = functools.reduce(lambda a, b: a + b, [refs[ng + t][...] for t in range(ng)]).astype(BF16)

    return pl.pallas_call(
        kern, name=name, grid=(s // tm,),
        in_specs=[_row_spec(tm, w)] * (2 * ng),
        out_specs=_row_spec(tm, 2 * w),
        out_shape=jax.ShapeDtypeStruct((s, 2 * w), BF16),
        compiler_params=_params(("parallel",)),
    )(*dks, *dvs)


def _ffn_fwd(hf, weight, layer, tag):
    a, gu = _ffn_in_act(hf, weight(f"w_in{layer}", hf), f"{tag}_in")
    o = _mm_nn(a, weight(f"w_out{layer}", a), F32, f"{tag}_out", tn=1024, tk=5632)
    return gu, a, o


def _ffn_bwd(dresb, hf, gu, a, weight, emit, layer, tag):
    w_in, w_out = weight(f"w_in{layer}", None), weight(f"w_out{layer}", None)
    zero = emit(f"w_out{layer}", _mm_tn(a, dresb, 1, f"{tag}_out_dw", tn=1024))
    dgu = _ffn_out_dx_act(dresb, w_out, gu, f"{tag}_out_dx")
    zero = zero + emit(f"w_in{layer}",
                       _mm_tn(hf, dgu, w_in.shape[0], f"{tag}_in_dw", tn=FFN_TILE, paired=True))
    return _mm_nt(dgu, w_in, f"{tag}_in_dx", tr=FFN_TILE, tkc=512, paired=True), zero


def _local_step(x, target, small, weight, emit, stage):
    s, d = x.shape
    nh = d // HEAD_DIM
    n_groups = d // S5_GROUP_CH
    nt = n_groups // GROUPS_PER_TILE
    p_, c_ = S5_STATE, S5_GROUP_CH

    disc_in = (small["lam_re"], small["lam_im"], small["log_dt"], small["b_re"], small["b_im"])
    (lb_re, lb_im, bb_re, bb_im), disc_vjp = jax.vjp(_s5_discretize, *disc_in)
    del lb_re, lb_im
    dt = jnp.exp(small["log_dt"])[:, None]
    def pole_powers(exponents):
        k = exponents[:, None, None]
        mag = jnp.exp(k * (small["lam_re"] * dt)[None])
        ang = k * (small["lam_im"] * dt)[None]
        return ((mag * jnp.cos(ang)).reshape(SUB, n_groups * p_),
                (mag * jnp.sin(ang)).reshape(SUB, n_groups * p_))

    pw_re, pw_im = pole_powers(jnp.arange(1, SUB + 1, dtype=F32))
    pf_re, pf_im = pole_powers(jnp.arange(SUB, 0, -1, dtype=F32))
    bbd_re = _block_diag(bb_re.transpose(0, 2, 1).reshape(nt, GROUPS_PER_TILE, c_, p_)).astype(BF16)
    bbd_im = _block_diag(bb_im.transpose(0, 2, 1).reshape(nt, GROUPS_PER_TILE, c_, p_)).astype(BF16)
    cbd_re = _block_diag(small["c_re"].transpose(0, 2, 1).reshape(nt, GROUPS_PER_TILE, p_, c_)).astype(BF16)
    cbd_im = _block_diag(small["c_im"].transpose(0, 2, 1).reshape(nt, GROUPS_PER_TILE, p_, c_)).astype(BF16)

    h0 = _norm_f32(x, small["a_norm"], "s5_norm")
    xs_re, xs_im, y0, z = _s5_fwd(h0, bbd_re, bbd_im, cbd_re, cbd_im, pw_re, pw_im, small["s5_d"], "s5_fwd")
    zero = stage("glu", z)
    vg = _mm_nn(z, weight("w_glu", z), F32, "glu_mm", tn=1024)
    x1, hf0 = _glu_res_norm(vg, x, small["ffn_norm0"] + zero, "glu_res_norm")
    gu0, a0, o0 = _ffn_fwd(hf0, weight, 0, "ffn0")
    zero = stage("attention", a0)
    x2 = _to_slabs([o0, x1], "to_slabs")
    kvn, h1 = _norm_bf16(x2, [small["kv_norm"] + zero, small["b_norm"]], "att_norms")
    kv = _mm_nn(kvn, weight("w_kv", kvn), BF16, "kv_mm", tn=1024)
    q = _mm_nn(h1, weight("w_q", kv), BF16, "q_mm", tn=1536)
    outs, lses_narrow, lses = [], [], []
    for gi, (window, dil) in enumerate(PATTERNS):
        assert window // dil == ATT_BLK
        og, lg, lg_wide = _att_fwd(q, kv, gi, dil, nh, f"att_fwd{gi}")
        outs.append(og)
        lses_narrow.append(lg)
        lses.append(lg_wide)
    oatt, oattb = _att_combine(outs, lses_narrow, "att_combine")
    ao = _mm_nn(oattb, weight("w_o", oattb), F32, "o_mm", tn=1024)
    x3, hf1 = _res_norm(ao, x2, [small["ffn_norm1"] + stage("ffn1", oattb)], "att_res_norm")
    gu1, a1, o1 = _ffn_fwd(hf1, weight, 1, "ffn1")
    dres, dresb, loss_rows, d_final = _loss_head(
        o1, x3, small["final_norm"], _to_slabs([target], "target_to_slabs"), "loss_head")

    dhf1, zero = _ffn_bwd(dresb, hf1, gu1, a1, weight, emit, 1, "ffn1")
    dres, dresb, d_ffn_norm1 = _norm_bwd(x3, [small["ffn_norm1"] + zero], [dhf1], dres, "ffn1_norm_bwd")
    zero = emit("w_o", _mm_tn(oattb, dresb, 1, "o_dw", tn=1024))
    doatt = _mm_nt(dresb, weight("w_o", None), "o_dx", tr=2048)
    dks, dvs = [], []
    dq = lax.empty(q.shape, BF16)
    for gi, (window, dil) in enumerate(PATTERNS):
        dq, dk_g, dv_g = _att_bwd(q, kv, doatt, oatt, lses, dq, gi, dil, nh, f"att_bwd{gi}")
        dks.append(dk_g)
        dvs.append(dv_g)
    dkv = _dkv_sum(dks, dvs, "dkv_sum")
    w_q = weight("w_q", None)
    zero = zero + emit("w_q", _mm_tn(h1, dq, w_q.shape[0], "q_dw", tn=1536))
    zero = zero + emit("w_kv", _mm_tn(kvn, dkv, 1, "kv_dw", tn=1024))
    dh1 = _mm_nt(dq, w_q, "q_dx", tr=1536)
    dkvn = _mm_nt(dkv, weight("w_kv", None), "kv_dx", tr=1024)
    dres, dresb, d_b_norm, d_kv_norm = _norm_bwd(
        x2, [small["b_norm"] + zero, small["kv_norm"]], [dh1, dkvn], dres, "att_norm_bwd")
    dres, dresb = _from_slabs(dres, "from_slabs")
    dhf0, zero = _ffn_bwd(dresb, hf0, gu0, a0, weight, emit, 0, "ffn0")
    dres, dresb, d_ffn_norm0 = _norm_bwd(x1, [small["ffn_norm0"] + zero], [dhf0], dres, "ffn0_norm_bwd")
    del dresb
    dvg = _glu_bwd(dres, vg, "glu_bwd")
    w_glu = weight("w_glu", None)
    zero = emit("w_glu", _mm_tn(z, dvg, w_glu.shape[0], "glu_dw", tn=1024))
    dz = _mm_nt(dvg, w_glu, "glu_dx", tr=1024)
    dh0, d_s5_d, dcr, dci_neg, dbr, dbi, dar, dai = _s5_bwd(
        dz, y0, h0, xs_re, xs_im, bbd_re, bbd_im, cbd_re, cbd_im, pw_re, pw_im, pf_re, pf_im,
        small["s5_d"] + zero, "s5_bwd")
    grad_x, _, d_a_norm = _norm_bwd(x, [small["a_norm"]], [dh0], dres, "s5_norm_bwd")

    d_bb_re = _block_diag_take(dbr, c_, p_).transpose(0, 2, 1)
    d_bb_im = _block_diag_take(dbi, c_, p_).transpose(0, 2, 1)
    d_c_re = _block_diag_take(dcr, c_, p_)
    d_c_im = -_block_diag_take(dci_neg, c_, p_)
    d_lam_re, d_lam_im, d_log_dt, d_b_re, d_b_im = disc_vjp(
        (dar.reshape(n_groups, p_), dai.reshape(n_groups, p_), d_bb_re, d_bb_im))

    d_small = dict(lam_re=d_lam_re, lam_im=d_lam_im, log_dt=d_log_dt, b_re=d_b_re, b_im=d_b_im,
                   c_re=d_c_re, c_im=d_c_im, s5_d=d_s5_d, a_norm=d_a_norm, ffn_norm0=d_ffn_norm0,
                   ffn_norm1=d_ffn_norm1, b_norm=d_b_norm, kv_norm=d_kv_norm, final_norm=d_final)
    return loss_rows, grad_x, d_small


def _place():
    x, y, c = lax.axis_index("x"), lax.axis_index("y"), lax.axis_index("c")
    return x, y, c, [(1 - x, y), (x, 1 - y), (1 - x, 1 - y)]


_ANY = pl.BlockSpec(memory_space=pl.ANY)


_HBM = pl.BlockSpec(memory_space=pltpu.HBM)
_SEM = pl.BlockSpec(memory_space=pltpu.SEMAPHORE)
_EFFECT = pltpu.SideEffectType.DATAFLOW_SIDE_EFFECTING


def _in_hbm(a):
    return pltpu.with_memory_space_constraint(a, pltpu.HBM)


def _cast_place(shards, layer, chip, name):
    _, r, c = shards.shape
    tm = _tile(r, ROWS, 16)

    def kern(chip_ref, x_ref, o_ref):
        del chip_ref
        o_ref[...] = x_ref[...].astype(BF16)

    return pl.pallas_call(
        kern, name=name,
        grid_spec=pltpu.PrefetchScalarGridSpec(
            num_scalar_prefetch=1, grid=(r // tm,),
            in_specs=[pl.BlockSpec((None, tm, c), lambda i, ch: (layer, i, 0))],
            out_specs=pl.BlockSpec((None, tm, c), lambda i, ch: (ch[0], i, 0))),
        out_shape=jax.ShapeDtypeStruct((N_CHIPS, r, c), BF16),
        compiler_params=_params(("parallel",)),
    )(chip, shards)


def _my_part(land, block, c, halves):
    if not halves:
        return land.at[block]
    half = land.shape[1] // 2
    return land.at[block, pl.ds(c * half, half)]


def _gather_start(lands, name, halves, after):
    n = len(lands)

    def body(*refs):
        land = refs[:n]
        send, recv = refs[n + 1:2 * n + 1], refs[2 * n + 1:3 * n + 1]
        token = refs[4 * n + 1]
        x, y, c, peers = _place()
        me = 2 * x + y
        for a in range(n):
            for k, (px, py) in enumerate(peers):
                part = _my_part(land[a], me, c, halves)
                pltpu.make_async_remote_copy(
                    src_ref=part, dst_ref=part, send_sem=send[a].at[k], recv_sem=recv[a].at[k],
                    device_id=(px, py, c), device_id_type=MESH).start()
        token[...] = jnp.zeros_like(token)

    outs = pl.pallas_call(
        body, name=name,
        out_shape=[pltpu.SemaphoreType.DMA((3,))] * (2 * n) + [pltpu.HBM(a.shape, a.dtype) for a in lands]
        + [jax.ShapeDtypeStruct((8, 128), F32)],
        in_specs=[_HBM] * n + [_ANY],
        out_specs=[_SEM] * (2 * n) + [_HBM] * n + [pl.BlockSpec(memory_space=pltpu.VMEM)],
        input_output_aliases={i: 2 * n + i for i in range(n)},
        compiler_params=pltpu.CompilerParams(has_side_effects=_EFFECT),
    )(*[_in_hbm(a) for a in lands], after)
    return outs[:n], outs[n:2 * n], outs[2 * n:3 * n], outs[3 * n]


def _gather_wait(lands, sends, recvs, after, name, halves):
    n = len(lands)

    def body(*refs):
        land, send, recv = refs[:n], refs[n:2 * n], refs[2 * n:3 * n]
        x, y, c, peers = _place()
        me = 2 * x + y
        for a in range(n):
            for k, (px, py) in enumerate(peers):
                cp = pltpu.make_async_remote_copy(
                    src_ref=_my_part(land[a], me, c, halves), dst_ref=_my_part(land[a], 2 * px + py, c, halves),
                    send_sem=send[a].at[k], recv_sem=recv[a].at[k], device_id=(px, py, c), device_id_type=MESH)
                cp.wait_send()
                cp.wait_recv()

    return pl.pallas_call(
        body, name=name,
        out_shape=[pltpu.HBM(a.shape, a.dtype) for a in lands],
        in_specs=[_HBM] * n + [_SEM] * (2 * n) + [_ANY], out_specs=[_HBM] * n,
        input_output_aliases={i: i for i in range(n)},
        compiler_params=pltpu.CompilerParams(has_side_effects=_EFFECT),
    )(*lands, *sends, *recvs, after)


def _forward_start(lands, name):
    n = len(lands)

    def body(*refs):
        land = refs[:n]
        send, recv = refs[n:2 * n], refs[2 * n:3 * n]
        token = refs[4 * n]
        x, y, c, peers = _place()
        for a in range(n):
            for k, (px, py) in enumerate(peers):
                part = _my_part(land[a], 2 * px + py, c, True)
                pltpu.make_async_remote_copy(
                    src_ref=part, dst_ref=part, send_sem=send[a].at[k], recv_sem=recv[a].at[k],
                    device_id=(x, y, 1 - c), device_id_type=MESH).start()
        token[...] = jnp.zeros_like(token)

    outs = pl.pallas_call(
        body, name=name,
        out_shape=[pltpu.SemaphoreType.DMA((3,))] * (2 * n) + [pltpu.HBM(a.shape, a.dtype) for a in lands]
        + [jax.ShapeDtypeStruct((8, 128), F32)],
        in_specs=[_HBM] * n,
        out_specs=[_SEM] * (2 * n) + [_HBM] * n + [pl.BlockSpec(memory_space=pltpu.VMEM)],
        input_output_aliases={i: 2 * n + i for i in range(n)},
        compiler_params=pltpu.CompilerParams(has_side_effects=_EFFECT),
    )(*[_in_hbm(a) for a in lands])
    return outs[:n], outs[n:2 * n], outs[2 * n:3 * n], outs[3 * n]


def _forward_wait(land, send, recv, after, name):
    def body(land_ref, send_sem, recv_sem, after_ref, land_out):
        del after_ref, land_out
        x, y, c, peers = _place()
        for k, (px, py) in enumerate(peers):
            cp = pltpu.make_async_remote_copy(
                src_ref=_my_part(land_ref, 2 * px + py, c, True), dst_ref=_my_part(land_ref, 2 * px + py, 1 - c, True),
                send_sem=send_sem.at[k], recv_sem=recv_sem.at[k], device_id=(x, y, 1 - c), device_id_type=MESH)
            cp.wait_send()
            cp.wait_recv()

    return pl.pallas_call(
        body, name=name,
        out_shape=pltpu.HBM(land.shape, land.dtype),
        in_specs=[_HBM, _SEM, _SEM, _ANY], out_specs=_HBM,
        input_output_aliases={0: 0},
        compiler_params=pltpu.CompilerParams(has_side_effects=_EFFECT),
    )(land, send, recv, after)


def _scatter_start(g, name):
    def body(g_ref, land_ref, send, recv, g_out, land_out, token):
        del g_out, land_out
        x, y, c, peers = _place()
        for k, (px, py) in enumerate(peers):
            pltpu.make_async_remote_copy(
                src_ref=g_ref.at[2 * px + py], dst_ref=land_ref.at[k], send_sem=send.at[k], recv_sem=recv.at[k],
                device_id=(px, py, c), device_id_type=MESH).start()
        token[...] = jnp.zeros_like(token)

    land = lax.empty((3,) + g.shape[1:], g.dtype)
    return pl.pallas_call(
        body, name=name,
        out_shape=(pltpu.SemaphoreType.DMA((3,)), pltpu.SemaphoreType.DMA((3,)),
                   pltpu.HBM(g.shape, g.dtype), pltpu.HBM(land.shape, land.dtype),
                   jax.ShapeDtypeStruct((8, 128), F32)),
        in_specs=(_HBM, _HBM), out_specs=(_SEM, _SEM, _HBM, _HBM, pl.BlockSpec(memory_space=pltpu.VMEM)),
        input_output_aliases={0: 2, 1: 3},
        compiler_params=pltpu.CompilerParams(has_side_effects=_EFFECT),
    )(_in_hbm(g), _in_hbm(land))


def _scatter_wait(started, after):
    n = len(started)

    def body(*refs):
        gs, lands = refs[:n], refs[n:2 * n]
        sends, recvs = refs[2 * n:3 * n], refs[3 * n:4 * n]
        x, y, c, peers = _place()
        for a in range(n):
            for k, (px, py) in enumerate(peers):
                cp = pltpu.make_async_remote_copy(
                    src_ref=gs[a].at[2 * px + py], dst_ref=lands[a].at[k], send_sem=sends[a].at[k],
                    recv_sem=recvs[a].at[k], device_id=(px, py, c), device_id_type=MESH)
                cp.wait_send()
                cp.wait_recv()

    gs = [s[2] for s in started]
    lands = [s[3] for s in started]
    outs = pl.pallas_call(
        body, name="scatter_wait",
        out_shape=[pltpu.HBM(a.shape, a.dtype) for a in gs + lands],
        in_specs=[_HBM] * (2 * n) + [_SEM] * (2 * n) + [_ANY], out_specs=[_HBM] * (2 * n),
        input_output_aliases={i: i for i in range(2 * n)},
        compiler_params=pltpu.CompilerParams(has_side_effects=_EFFECT),
    )(*gs, *lands, *[s[0] for s in started], *[s[1] for s in started], after)
    return outs[:n], outs[n:]


def _sibling():
    return lax.axis_index("x"), lax.axis_index("y"), 1 - lax.axis_index("c")


def _swap_start(parts):
    n = len(parts)

    def body(*refs):
        ins, land = refs[:n], refs[n:2 * n]
        send, recv = refs[2 * n], refs[2 * n + 1]
        token = refs[4 * n + 2]
        for a in range(n):
            pltpu.make_async_remote_copy(
                src_ref=ins[a], dst_ref=land[a], send_sem=send.at[a], recv_sem=recv.at[a],
                device_id=_sibling(), device_id_type=MESH).start()
        token[...] = jnp.zeros_like(token)

    lands = [lax.empty(a.shape, a.dtype) for a in parts]
    outs = pl.pallas_call(
        body, name="swap_start",
        out_shape=[pltpu.SemaphoreType.DMA((n,))] * 2 + [pltpu.HBM(a.shape, a.dtype) for a in parts + lands]
        + [jax.ShapeDtypeStruct((8, 128), F32)],
        in_specs=[_HBM] * (2 * n),
        out_specs=[_SEM] * 2 + [_HBM] * (2 * n) + [pl.BlockSpec(memory_space=pltpu.VMEM)],
        input_output_aliases={i: 2 + i for i in range(2 * n)},
        compiler_params=pltpu.CompilerParams(has_side_effects=_EFFECT),
    )(*[_in_hbm(a) for a in parts + lands])
    return outs[0], outs[1], outs[2:2 + n], outs[2 + n:2 + 2 * n], outs[2 + 2 * n]


def _swap_wait(parts, lands, send, recv, after):
    n = len(parts)

    def body(*refs):
        ins, land = refs[:n], refs[n:2 * n]
        send_sem, recv_sem = refs[2 * n], refs[2 * n + 1]
        for a in range(n):
            cp = pltpu.make_async_remote_copy(
                src_ref=ins[a], dst_ref=land[a], send_sem=send_sem.at[a], recv_sem=recv_sem.at[a],
                device_id=_sibling(), device_id_type=MESH)
            cp.wait_send()
            cp.wait_recv()

    outs = pl.pallas_call(
        body, name="swap_wait",
        out_shape=[pltpu.HBM(a.shape, a.dtype) for a in list(parts) + list(lands)],
        in_specs=[_HBM] * (2 * n) + [_SEM] * 2 + [_ANY], out_specs=[_HBM] * (2 * n),
        input_output_aliases={i: i for i in range(2 * n)},
        compiler_params=pltpu.CompilerParams(has_side_effects=_EFFECT),
    )(*parts, *lands, send, recv, after)
    return outs[:n], outs[n:]


def _all_reduce_small(v):
    nd, r, w = v.shape
    assert nd == N_DEV

    def body(v_ref, out_ref, land_ref, red_ref, send1, recv1, send2, recv2):
        x, y, c = lax.axis_index("x"), lax.axis_index("y"), lax.axis_index("c")
        me = 4 * x + 2 * y + c
        peers = []
        for k in range(1, N_DEV):
            kx, ky, kc = (k >> 2) & 1, (k >> 1) & 1, k & 1
            peers.append((1 - x if kx else x, 1 - y if ky else y, 1 - c if kc else c))
        first = []
        for k, (px, py, pc) in enumerate(peers):
            cp = pltpu.make_async_remote_copy(
                src_ref=v_ref.at[4 * px + 2 * py + pc], dst_ref=land_ref.at[me], send_sem=send1.at[k],
                recv_sem=recv1.at[k], device_id=(px, py, pc), device_id_type=MESH)
            cp.start()
            first.append(cp)
        land_ref[me] = v_ref[me]
        for cp in first:
            cp.wait()
        acc = land_ref[0]
        for j in range(1, N_DEV):
            acc = acc + land_ref[j]
        red_ref[...] = acc
        second = []
        for k, (px, py, pc) in enumerate(peers):
            cp = pltpu.make_async_remote_copy(
                src_ref=red_ref, dst_ref=out_ref.at[me], send_sem=send2.at[k],
                recv_sem=recv2.at[k], device_id=(px, py, pc), device_id_type=MESH)
            cp.start()
            second.append(cp)
        out_ref[me] = acc
        for cp in second:
            cp.wait()

    vmem = pl.BlockSpec(memory_space=pltpu.VMEM)
    return pl.pallas_call(
        body, name="all_reduce_small",
        in_specs=[vmem], out_specs=vmem,
        out_shape=jax.ShapeDtypeStruct((nd, r, w), F32),
        scratch_shapes=[pltpu.VMEM((nd, r, w), F32), pltpu.VMEM((r, w), F32)]
        + [pltpu.SemaphoreType.DMA((N_DEV - 1,))] * 4,
        compiler_params=pltpu.CompilerParams(vmem_limit_bytes=VMEM_LIMIT),
    )(v)


def _adam_math(w, g, m, v):
    m = ADAM_B1 * m + (1.0 - ADAM_B1) * g
    v = ADAM_B2 * v + (1.0 - ADAM_B2) * (g * g)
    m_hat = m / (1.0 - ADAM_B1 ** ADAM_STEP)
    v_hat = v / (1.0 - ADAM_B2 ** ADAM_STEP)
    delta = -ADAM_LR * (m_hat / (jnp.sqrt(v_hat) + ADAM_EPS) + ADAM_WD * w)
    return delta, m, v


def _sum_blocks(own, got, chip, name):
    _, r, c = own.shape
    tm = _tile(r, ROWS, 16)

    def kern(chip_ref, own_ref, got_ref, o_ref):
        del chip_ref
        acc = own_ref[...].astype(F32)
        for k in range(3):
            acc = acc + got_ref[k].astype(F32)
        o_ref[...] = acc

    return pl.pallas_call(
        kern, name=name,
        grid_spec=pltpu.PrefetchScalarGridSpec(
            num_scalar_prefetch=1, grid=(r // tm,),
            in_specs=[pl.BlockSpec((None, tm, c), lambda i, ch: (ch[0], i, 0)),
                      pl.BlockSpec((3, tm, c), lambda i, ch: (0, i, 0))],
            out_specs=pl.BlockSpec((tm, c), lambda i, ch: (i, 0))),
        out_shape=jax.ShapeDtypeStruct((r, c), F32),
        compiler_params=_params(("parallel",)),
    )(chip, own, got)


def _adamw(parts, w, m, v, name):
    nl, r, c = w.shape
    assert len(parts) == nl
    tm = _tile(r, 128, 8)

    def kern(*refs):
        p_refs = refs[:2 * nl]
        w_ref, m_ref, v_ref, g_ref, d_ref, mo_ref, vo_ref = refs[2 * nl:]
        layer = pl.program_id(0)
        g = p_refs[0][...] + p_refs[1][...]
        for ll in range(1, nl):
            g = jnp.where(layer == ll, p_refs[2 * ll][...] + p_refs[2 * ll + 1][...], g)
        g_ref[...] = g
        d_ref[...], mo_ref[...], vo_ref[...] = _adam_math(w_ref[...], g, m_ref[...], v_ref[...])

    def part_spec(ll):
        return pl.BlockSpec((tm, c), lambda l, i: (jnp.where(l == ll, i, 0), 0))

    spec = pl.BlockSpec((None, tm, c), lambda l, i: (l, i, 0))
    return pl.pallas_call(
        kern, name=name, grid=(nl, r // tm),
        in_specs=[part_spec(ll) for ll in range(nl) for _ in range(2)] + [spec] * 3, out_specs=[spec] * 4,
        out_shape=[jax.ShapeDtypeStruct((nl, r, c), F32)] * 4,
        compiler_params=_params(("arbitrary", "parallel")),
    )(*[p for pair in parts for p in pair], w, m, v)


def _adamw_small(g, w, m, v, name):
    def kern(g_ref, w_ref, m_ref, v_ref, d_ref, mo_ref, vo_ref):
        d_ref[...], mo_ref[...], vo_ref[...] = _adam_math(w_ref[...], g_ref[...], m_ref[...], v_ref[...])

    return pl.pallas_call(
        kern, name=name,
        out_shape=[jax.ShapeDtypeStruct(g.shape, F32)] * 3,
        compiler_params=pltpu.CompilerParams(vmem_limit_bytes=VMEM_LIMIT),
    )(g, w, m, v)


def _pack(arrays, rows):
    flat = jnp.concatenate([a.reshape(-1).astype(F32) for a in arrays])
    return jnp.pad(flat, (0, rows * 128 - flat.shape[0])).reshape(rows, 128)


def _unpack(packed, shapes):
    flat = packed.reshape(-1)
    out, off = [], 0
    for shp in shapes:
        size = math.prod(shp)
        out.append(flat[off:off + size].reshape(shp))
        off += size
    return out


_REPLICATED = ["s5_lam_re", "s5_lam_im", "s5_log_dt", "s5_b_re", "s5_b_im", "s5_c_re", "s5_c_im",
               "ffn_norm", "b_norm_mix", "kv_norm", "final_norm"]
_CHIP_VECTORS = ["s5_d", "a_norm_mix"]
_BIG = ["s5_w_glu", "ffn_w_in", "ffn_w_out", "attn_w_q", "attn_w_o", "w_kv"]
_WEIGHT_ORDER = ["s5_lam_re", "s5_lam_im", "s5_log_dt", "s5_b_re", "s5_b_im", "s5_c_re", "s5_c_im", "s5_d",
                 "s5_w_glu", "a_norm_mix", "ffn_norm", "ffn_w_in", "ffn_w_out", "b_norm_mix", "attn_w_q",
                 "attn_w_o", "kv_norm", "w_kv", "final_norm"]


def _step(x, loss_target, w, m, v):
    s, d = x.shape[1], x.shape[2]
    chip = 2 * lax.axis_index("x") + lax.axis_index("y")

    col_sharded = dict(w_glu=("s5_w_glu", 0), w_in0=("ffn_w_in", 0), w_in1=("ffn_w_in", 1), w_q=("attn_w_q", 0))
    row_sharded = dict(w_out0=("ffn_w_out", 0), w_out1=("ffn_w_out", 1), w_o=("attn_w_o", 0), w_kv=("w_kv", 0))
    names = ["w_glu", "w_in0", "w_out0", "w_kv", "w_q", "w_o", "w_in1", "w_out1"]
    local = {**col_sharded, **row_sharded}

    def layers(a):
        return a.reshape((-1,) + a.shape[-2:])

    chip_arr = jnp.reshape(chip, (1,)).astype(jnp.int32)
    vector_lands = [lax.dynamic_update_slice(jnp.zeros((N_CHIPS,) + w[n].shape, F32), w[n][None], (chip, 0, 0))
                    for n in _CHIP_VECTORS]
    v_send, v_recv, v_land, v_token = _gather_start(vector_lands, "gather_start_vectors", False,
                                                    jnp.zeros((8, 128), F32))
    first = 2
    lands = [_cast_place(layers(w[local[n][0]]), local[n][1], chip_arr, f"cast_{n}") for n in names[:first]]
    send, recv, land_thru, first_token = _gather_start(lands, "gather_start_first", True, v_token)
    lands = [_cast_place(layers(w[local[n][0]]), local[n][1], chip_arr, f"cast_{n}") for n in names[first:]]
    *later, rest_token = _gather_start(lands, "gather_start_rest", True, first_token)
    send, recv, land_thru = [list(a) + list(b) for a, b in zip((send, recv, land_thru), later)]
    vectors = _gather_wait(v_land, v_send, v_recv, jnp.zeros((8, 128), F32), "gather_wait_vectors", False)
    s5_d_full = vectors[0].reshape(1, d)
    a_norm_full = vectors[1].reshape(1, d) + rest_token[0, 0]

    batches = dict(glu=["w_glu", "w_in0", "w_out0"], attention=["w_kv", "w_q", "w_o"], ffn1=["w_in1", "w_out1"])
    forwarded, arrived = {}, {}

    def stage(batch, after):
        idx = [names.index(n) for n in batches[batch]]
        got = _gather_wait([land_thru[i] for i in idx], [send[i] for i in idx], [recv[i] for i in idx], after,
                           f"gather_wait_{batch}", True)
        f_send, f_recv, f_land, token = _forward_start(got, f"forward_start_{batch}")
        forwarded.update(zip(batches[batch], zip(f_send, f_recv, f_land)))
        return token[0, 0]

    def weight(name, after):
        if name not in arrived:
            f_send, f_recv, f_land = forwarded[name]
            land = _forward_wait(f_land, f_send, f_recv, after, f"forward_wait_{name}")
            arrived[name] = land if name in col_sharded else land.reshape(1, -1, land.shape[-1])
        return arrived[name]

    started = {}

    def emit(name, dw):
        outs = _scatter_start(dw.reshape(N_CHIPS, -1, dw.shape[-1]), f"scatter_start_{name}")
        started[name] = outs[:4]
        return outs[4][0, 0]

    small = dict(lam_re=w["s5_lam_re"][0], lam_im=w["s5_lam_im"][0], log_dt=w["s5_log_dt"][0],
                 b_re=w["s5_b_re"][0], b_im=w["s5_b_im"][0], c_re=w["s5_c_re"][0], c_im=w["s5_c_im"][0],
                 s5_d=s5_d_full, a_norm=a_norm_full, ffn_norm0=w["ffn_norm"][0:1], ffn_norm1=w["ffn_norm"][1:2],
                 b_norm=w["b_norm_mix"], kv_norm=w["kv_norm"].reshape(1, d), final_norm=w["final_norm"].reshape(1, d))

    loss_rows, grad_x, d_small = _local_step(x[0], loss_target[0], small, weight, emit, stage)

    g4, got = _scatter_wait([started[n] for n in names], grad_x)
    partial = [_sum_blocks(o, r, chip_arr, f"sum_{n}") for n, o, r in zip(names, g4, got)]
    swap_send, swap_recv, partial, swap_land, swap_token = _swap_start(partial)
    result = {}

    rep_grads = [d_small["lam_re"], d_small["lam_im"], d_small["log_dt"], d_small["b_re"], d_small["b_im"],
                 d_small["c_re"], d_small["c_im"],
                 jnp.concatenate([d_small["ffn_norm0"], d_small["ffn_norm1"]], axis=0),
                 d_small["b_norm"], d_small["kv_norm"], d_small["final_norm"]]
    vec_grads = [d_small["s5_d"], d_small["a_norm"]]
    to_reduce = rep_grads + vec_grads + [jnp.sum(loss_rows).reshape(1) + swap_token[0, 0]]
    total = sum(math.prod(a.shape) for a in to_reduce)
    rows_per = -(-total // (N_DEV * 128 * 8)) * 8
    reduced = _all_reduce_small(_pack(to_reduce, N_DEV * rows_per).reshape(N_DEV, rows_per, 128))
    red = _unpack(reduced, [a.shape for a in to_reduce])
    loss = red[-1][0]
    g_small = dict(zip(_REPLICATED, [r.reshape(w[n].shape) for r, n in zip(red[:len(rep_grads)], _REPLICATED)]))
    for n, r in zip(_CHIP_VECTORS, red[len(rep_grads):-1]):
        g_small[n] = lax.dynamic_slice_in_dim(r.reshape(1, d), chip * (d // N_CHIPS), d // N_CHIPS, axis=1)
    small_names = _REPLICATED + _CHIP_VECTORS
    n_small = sum(math.prod(w[n].shape) for n in small_names)
    rows_small = -(-n_small // (128 * 8)) * 8
    packed = [_pack([src[n] for n in small_names], rows_small) for src in (g_small, w, m, v)]
    upd = _adamw_small(*packed, "adamw_small")
    shapes = [w[n].shape for n in small_names]
    for n, dl, mo, vo in zip(small_names, *[_unpack(u, shapes) for u in upd]):
        result[n] = [g_small[n], dl, mo, vo]

    partial, other = _swap_wait(partial, swap_land, swap_send, swap_recv, upd[0])
    part = dict(zip(names, zip(partial, other)))
    for name in _BIG:
        parts = [part[n] for n in sorted((n for n in names if local[n][0] == name), key=lambda n: local[n][1])]
        outs = _adamw(parts, layers(w[name]), layers(m[name]), layers(v[name]), f"adamw_{name}")
        result[name] = [o.reshape(w[name].shape) for o in outs]

    cols = [[result[n][t] for n in _WEIGHT_ORDER] for t in range(4)]
    return (loss, grad_x.reshape(x.shape), *cols[0], *cols[1], *cols[2], *cols[3])


def kernel(x, s5_lam_re, s5_lam_im, s5_log_dt, s5_b_re, s5_b_im, s5_c_re, s5_c_im, s5_d, s5_w_glu, a_norm_mix, ffn_norm, ffn_w_in, ffn_w_out, b_norm_mix, attn_w_q, attn_w_o, kv_norm, w_kv, final_norm, loss_target, m_s5_lam_re, m_s5_lam_im, m_s5_log_dt, m_s5_b_re, m_s5_b_im, m_s5_c_re, m_s5_c_im, m_s5_d, m_s5_w_glu, m_a_norm_mix, m_ffn_norm, m_ffn_w_in, m_ffn_w_out, m_b_norm_mix, m_attn_w_q, m_attn_w_o, m_kv_norm, m_w_kv, m_final_norm, v_s5_lam_re, v_s5_lam_im, v_s5_log_dt, v_s5_b_re, v_s5_b_im, v_s5_c_re, v_s5_c_im, v_s5_d, v_s5_w_glu, v_a_norm_mix, v_ffn_norm, v_ffn_w_in, v_ffn_w_out, v_b_norm_mix, v_attn_w_q, v_attn_w_o, v_kv_norm, v_w_kv, v_final_norm):
    w = dict(zip(_WEIGHT_ORDER, (s5_lam_re, s5_lam_im, s5_log_dt, s5_b_re, s5_b_im, s5_c_re, s5_c_im, s5_d, s5_w_glu, a_norm_mix, ffn_norm, ffn_w_in, ffn_w_out, b_norm_mix, attn_w_q, attn_w_o, kv_norm, w_kv, final_norm)))
    m = dict(zip(_WEIGHT_ORDER, (m_s5_lam_re, m_s5_lam_im, m_s5_log_dt, m_s5_b_re, m_s5_b_im, m_s5_c_re, m_s5_c_im, m_s5_d, m_s5_w_glu, m_a_norm_mix, m_ffn_norm, m_ffn_w_in, m_ffn_w_out, m_b_norm_mix, m_attn_w_q, m_attn_w_o, m_kv_norm, m_w_kv, m_final_norm)))
    v = dict(zip(_WEIGHT_ORDER, (v_s5_lam_re, v_s5_lam_im, v_s5_log_dt, v_s5_b_re, v_s5_b_im, v_s5_c_re, v_s5_c_im, v_s5_d, v_s5_w_glu, v_a_norm_mix, v_ffn_norm, v_ffn_w_in, v_ffn_w_out, v_b_norm_mix, v_attn_w_q, v_attn_w_o, v_kv_norm, v_w_kv, v_final_norm)))
    return _step(x, loss_target, w, m, v)
```

```python
import functools
import math

import jax
import jax.numpy as jnp
from jax import lax
from jax.experimental import pallas as pl
from jax.experimental.pallas import tpu as pltpu

F32 = jnp.float32
BF16 = jnp.bfloat16

S5_GROUP_CH = 16
S5_STATE = 64
GROUPS_PER_TILE = 8
HEAD_DIM = 128
N_KV_HEADS = 4
PATTERNS = ((128, 1), (512, 4), (2048, 16))
ATT_BLK = 128
EPS = 1e-6
NEG_INF = -1e30
SCAN_T = 1024
ADAM_LR = 0.001
ADAM_B1 = 0.9
ADAM_B2 = 0.999
ADAM_EPS = 1e-08
ADAM_WD = 0.01
ADAM_STEP = 10
N_CHIPS = 4
N_DEV = 8
VMEM_LIMIT = 56 * 1024 * 1024
MESH = pl.DeviceIdType.MESH
GELU_K = math.sqrt(2.0 / math.pi)
GELU_C = 0.044715


def _tile(n, pref, unit=128):
    if n <= pref:
        return n
    best = None
    t = unit
    while t <= pref:
        if n % t == 0:
            best = t
        t += unit
    assert best is not None, (n, pref, unit)
    return best


def _params(sem):
    return pltpu.CompilerParams(dimension_semantics=sem, vmem_limit_bytes=VMEM_LIMIT)


def _dot(a, b):
    return jnp.dot(a, b, preferred_element_type=F32)


def _dot_nt(a, b):
    return lax.dot_general(a, b, (((1,), (1,)), ((), ())), preferred_element_type=F32)


def _dot_tn(a, b):
    return lax.dot_general(a, b, (((0,), (0,)), ((), ())), preferred_element_type=F32)


def _mm_nn(a, w, out_dtype, name, tm=512, tn=1536, tk=2048):
    m, k = a.shape
    nb, k2, nq = w.shape
    assert k == k2
    tm, tn, tk = _tile(m, tm, 8), _tile(nq, tn), _tile(k, tk)
    per, nk = nq // tn, k // tk

    def kern(a_ref, w_ref, o_ref, *acc):
        p = _dot(a_ref[...], w_ref[...])
        if nk == 1:
            o_ref[...] = p.astype(o_ref.dtype)
        else:
            acc_ref, = acc
            kk = pl.program_id(2)

            @pl.when(kk == 0)
            def _():
                acc_ref[...] = p

            @pl.when(kk > 0)
            def _():
                acc_ref[...] += p

            @pl.when(kk == nk - 1)
            def _():
                o_ref[...] = acc_ref[...].astype(o_ref.dtype)

    return pl.pallas_call(
        kern, name=name, grid=(nb * per, m // tm, nk),
        in_specs=[pl.BlockSpec((tm, tk), lambda j, i, kk: (i, kk)),
                  pl.BlockSpec((None, tk, tn), lambda j, i, kk: (j // per, kk, j % per))],
        out_specs=pl.BlockSpec((tm, tn), lambda j, i, kk: (i, j)),
        out_shape=jax.ShapeDtypeStruct((m, nb * nq), out_dtype),
        scratch_shapes=[] if nk == 1 else [pltpu.VMEM((tm, tn), F32)],
        compiler_params=_params(("parallel", "parallel", "arbitrary")),
    )(a, w)


def _paired_block(r, per, nb):
    j = r // 2
    return (r % 2) * (nb // 2) + j // per, j % per


def _mm_nt(a, w, name, tm=512, tr=1536, tkc=1024, paired=False):
    m, n = a.shape
    nb, k, nq = w.shape
    assert n == nb * nq
    tm, tr, tkc = _tile(m, tm, 8), _tile(nq, tr), _tile(k, tkc)
    per = nq // tr

    def kern(a_ref, w_ref, o_ref):
        acc = None
        for r in range(nb * per):
            blk, tile = _paired_block(r, per, nb) if paired else (r // per, r % per)
            p = _dot_nt(a_ref[:, r * tr:(r + 1) * tr], w_ref[blk, :, tile * tr:(tile + 1) * tr])
            acc = p if acc is None else acc + p
        o_ref[...] = acc

    return pl.pallas_call(
        kern, name=name, grid=(k // tkc, m // tm),
        in_specs=[pl.BlockSpec((tm, n), lambda kc, i: (i, 0)),
                  pl.BlockSpec((nb, tkc, nq), lambda kc, i: (0, kc, 0))],
        out_specs=pl.BlockSpec((tm, tkc), lambda kc, i: (i, kc)),
        out_shape=jax.ShapeDtypeStruct((m, k), F32),
        compiler_params=_params(("parallel", "parallel")),
    )(a, w)


def _mm_tn(a, dy, nb, name, tkk=512, tn=1536, paired=False):
    s, k = a.shape
    s2, n = dy.shape
    assert s == s2 and n % nb == 0
    nq = n // nb
    tkk, tn = _tile(k, tkk), _tile(nq, tn)
    per = nq // tn

    def w_block(j):
        return _paired_block(j, per, nb) if paired else (j // per, j % per)

    def kern(a_ref, dy_ref, o_ref):
        o_ref[...] = _dot_tn(a_ref[...], dy_ref[...]).astype(o_ref.dtype)

    return pl.pallas_call(
        kern, name=name, grid=(nb * per, k // tkk),
        in_specs=[pl.BlockSpec((s, tkk), lambda j, kk: (0, kk)),
                  pl.BlockSpec((s, tn), lambda j, kk: (0, j))],
        out_specs=pl.BlockSpec((None, tkk, tn), lambda j, kk: (w_block(j)[0], kk, w_block(j)[1])),
        out_shape=jax.ShapeDtypeStruct((nb, k, nq), BF16),
        compiler_params=_params(("parallel", "parallel")),
    )(a, dy)


ROWS = 512


def _rms(x, g):
    r = lax.rsqrt(jnp.mean(x * x, axis=-1, keepdims=True) + EPS)
    return x * r * g


def _rms_bwd(x, g, dh):
    r = lax.rsqrt(jnp.mean(x * x, axis=-1, keepdims=True) + EPS)
    xh = x * r
    dgx = dh * g
    dx = r * (dgx - xh * jnp.mean(dgx * xh, axis=-1, keepdims=True))
    return dx, dh * xh


def _sigmoid(x):
    return 1.0 / (1.0 + jnp.exp(-x))


def _gelu(y):
    return 0.5 * y * (1.0 + jnp.tanh(GELU_K * (y + GELU_C * y * y * y)))


def _gelu_grad(y):
    t = jnp.tanh(GELU_K * (y + GELU_C * y * y * y))
    return 0.5 * (1.0 + t) + 0.5 * y * (1.0 - t * t) * GELU_K * (1.0 + 3.0 * GELU_C * y * y)


def _row_spec(tm, d, col=0):
    return pl.BlockSpec((tm, d), lambda i: (i, col))


def _vec_spec(d):
    return pl.BlockSpec((1, d), lambda i: (0, 0))


def _acc_rows(ref, val, i):
    s = jnp.sum(val, axis=0, keepdims=True)

    @pl.when(i == 0)
    def _():
        ref[...] = s

    @pl.when(i > 0)
    def _():
        ref[...] += s


def _norm_f32(x, g, name):
    s, d = x.shape
    tm = _tile(s, ROWS, 8)

    def kern(x_ref, g_ref, h_ref):
        h_ref[...] = _rms(x_ref[...], g_ref[...])

    return pl.pallas_call(
        kern, name=name, grid=(s // tm,),
        in_specs=[_row_spec(tm, d), _vec_spec(d)],
        out_specs=_row_spec(tm, d),
        out_shape=jax.ShapeDtypeStruct((s, d), F32),
        compiler_params=_params(("parallel",)),
    )(x, g)


def _glu_res_norm(vg, x, g, name):
    s, d = x.shape
    tm = _tile(s, ROWS, 16)

    def kern(val_ref, gate_ref, x_ref, g_ref, x1_ref, hf_ref):
        x1 = x_ref[...] + val_ref[...] * _sigmoid(gate_ref[...])
        x1_ref[...] = x1
        hf_ref[...] = _rms(x1, g_ref[...]).astype(BF16)

    return pl.pallas_call(
        kern, name=name, grid=(s // tm,),
        in_specs=[_row_spec(tm, d, 0), _row_spec(tm, d, 1), _row_spec(tm, d), _vec_spec(d)],
        out_specs=[_row_spec(tm, d), _row_spec(tm, d)],
        out_shape=[jax.ShapeDtypeStruct((s, d), F32), jax.ShapeDtypeStruct((s, d), BF16)],
        compiler_params=_params(("parallel",)),
    )(vg, vg, x, g)


FFN_TILE = 1408


def _ffn_in_act(hf, w_in, name, tm=512):
    s, k = hf.shape
    nb, _, nq = w_in.shape
    tm, tn = _tile(s, tm, 16), _tile(nq, FFN_TILE)
    per = nq // tn
    nf = (nb // 2) * per

    def kern(h_ref, wg_ref, wu_ref, a_ref, gu_ref):
        h = h_ref[...]
        g = _dot(h, wg_ref[...])
        u = _dot(h, wu_ref[...])
        sg = _sigmoid(g)
        silu = g * sg
        a_ref[...] = (silu * u).astype(BF16)
        gu_ref[:, :tn] = (u * (sg + silu * (1.0 - sg))).astype(BF16)
        gu_ref[:, tn:] = silu.astype(BF16)

    return pl.pallas_call(
        kern, name=name, grid=(nf, s // tm),
        in_specs=[pl.BlockSpec((tm, k), lambda j, i: (i, 0)),
                  pl.BlockSpec((None, k, tn), lambda j, i: (j // per, 0, j % per)),
                  pl.BlockSpec((None, k, tn), lambda j, i: (nb // 2 + j // per, 0, j % per))],
        out_specs=[pl.BlockSpec((tm, tn), lambda j, i: (i, j)),
                   pl.BlockSpec((tm, 2 * tn), lambda j, i: (i, j))],
        out_shape=[jax.ShapeDtypeStruct((s, nf * tn), BF16), jax.ShapeDtypeStruct((s, 2 * nf * tn), BF16)],
        compiler_params=_params(("parallel", "parallel")),
    )(hf, w_in, w_in)


def _ffn_out_dx_act(dresb, w_out, gu, name, tm=512):
    s, d = dresb.shape
    f = w_out.shape[1]
    tm = _tile(s, tm, 16)
    tn = _tile(f // 2, FFN_TILE)

    def kern(d_ref, w_ref, gu_ref, o_ref):
        da = _dot_nt(d_ref[...], w_ref[...])
        o_ref[:, :tn] = (da * gu_ref[:, :tn].astype(F32)).astype(BF16)
        o_ref[:, tn:] = (da * gu_ref[:, tn:].astype(F32)).astype(BF16)

    pair = pl.BlockSpec((tm, 2 * tn), lambda j, i: (i, j))
    return pl.pallas_call(
        kern, name=name, grid=(f // tn, s // tm),
        in_specs=[pl.BlockSpec((tm, d), lambda j, i: (i, 0)),
                  pl.BlockSpec((None, tn, d), lambda j, i: (0, j, 0)), pair],
        out_specs=pair,
        out_shape=jax.ShapeDtypeStruct((s, 2 * f), BF16),
        compiler_params=_params(("parallel", "parallel")),
    )(dresb, w_out, gu)


def _res_norm(o, x, gains, name):
    s, d = x.shape
    tm = _tile(s, ROWS, 16)
    ng = len(gains)

    def kern(o_ref, x_ref, *refs):
        xn = x_ref[...] + o_ref[...]
        refs[ng][...] = xn
        for t in range(ng):
            refs[ng + 1 + t][...] = _rms(xn, refs[t][...]).astype(BF16)

    return pl.pallas_call(
        kern, name=name, grid=(s // tm,),
        in_specs=[_row_spec(tm, d), _row_spec(tm, d)] + [_vec_spec(d)] * ng,
        out_specs=[_row_spec(tm, d)] * (1 + ng),
        out_shape=[jax.ShapeDtypeStruct((s, d), F32)] + [jax.ShapeDtypeStruct((s, d), BF16)] * ng,
        compiler_params=_params(("parallel",)),
    )(o, x, *gains)


def _loss_head(o, x, g, target, name):
    s, d = x.shape
    tm = _tile(s, ROWS, 16)

    def kern(o_ref, x_ref, g_ref, t_ref, dx_ref, dxb_ref, loss_ref, dg_ref):
        i = pl.program_id(0)
        x4 = x_ref[...] + o_ref[...]
        gg = g_ref[...]
        diff = _rms(x4, gg) - t_ref[...]
        dx, dgr = _rms_bwd(x4, gg, diff * (1.0 / d))
        dx_ref[...] = dx
        dxb_ref[...] = dx.astype(BF16)
        _acc_rows(loss_ref, diff * diff * (0.5 / d), i)
        _acc_rows(dg_ref, dgr, i)

    return pl.pallas_call(
        kern, name=name, grid=(s // tm,),
        in_specs=[_row_spec(tm, d), _row_spec(tm, d), _vec_spec(d), _row_spec(tm, d)],
        out_specs=[_row_spec(tm, d), _row_spec(tm, d), _vec_spec(d), _vec_spec(d)],
        out_shape=[jax.ShapeDtypeStruct((s, d), F32), jax.ShapeDtypeStruct((s, d), BF16),
                   jax.ShapeDtypeStruct((1, d), F32), jax.ShapeDtypeStruct((1, d), F32)],
        compiler_params=_params(("arbitrary",)),
    )(o, x, g, target)


def _norm_bwd(x, gains, dhs, dres, name):
    s, d = x.shape
    ng = len(gains)
    tm = _tile(s, ROWS // ng, 16)

    def kern(x_ref, dres_ref, *refs):
        i = pl.program_id(0)
        x_ = x_ref[...]
        acc = dres_ref[...]
        for t in range(ng):
            dx, dgr = _rms_bwd(x_, refs[t][...], refs[ng + t][...])
            acc = acc + dx
            _acc_rows(refs[2 * ng + 2 + t], dgr, i)
        refs[2 * ng][...] = acc
        refs[2 * ng + 1][...] = acc.astype(BF16)

    return pl.pallas_call(
        kern, name=name, grid=(s // tm,),
        in_specs=[_row_spec(tm, d), _row_spec(tm, d)] + [_vec_spec(d)] * ng + [_row_spec(tm, d)] * ng,
        out_specs=[_row_spec(tm, d), _row_spec(tm, d)] + [_vec_spec(d)] * ng,
        out_shape=[jax.ShapeDtypeStruct((s, d), F32), jax.ShapeDtypeStruct((s, d), BF16)]
        + [jax.ShapeDtypeStruct((1, d), F32)] * ng,
        compiler_params=_params(("arbitrary",)),
    )(x, dres, *gains, *dhs)


def _glu_bwd(dmix, vg, name):
    s, d = dmix.shape
    tm = _tile(s, ROWS, 16)

    def kern(dm_ref, val_ref, gate_ref, o_ref):
        dm = dm_ref[...]
        sg = _sigmoid(gate_ref[...])
        o_ref[:, :d] = (dm * sg).astype(BF16)
        o_ref[:, d:] = (dm * val_ref[...] * sg * (1.0 - sg)).astype(BF16)

    return pl.pallas_call(
        kern, name=name, grid=(s // tm,),
        in_specs=[_row_spec(tm, d), _row_spec(tm, d, 0), _row_spec(tm, d, 1)],
        out_specs=_row_spec(tm, 2 * d),
        out_shape=jax.ShapeDtypeStruct((s, 2 * d), BF16),
        compiler_params=_params(("parallel",)),
    )(dmix, vg, vg)


SUB = 8


def _local_scan(vr, vi, pre_ref, pim_ref, reverse):
    sub = lax.broadcasted_iota(jnp.int32, vr.shape, 1)
    sign = -1.0 if reverse else 1.0
    for sh in (1, 2, 4):
        ar = pre_ref[sh - 1:sh, :][None]
        ai = sign * pim_ref[sh - 1:sh, :][None]
        keep = sub < SUB - sh if reverse else sub >= sh
        sr = jnp.where(keep, pltpu.roll(vr, SUB - sh if reverse else sh, 1), 0.0)
        si = jnp.where(keep, pltpu.roll(vi, SUB - sh if reverse else sh, 1), 0.0)
        vr, vi = vr + ar * sr - ai * si, vi + ar * si + ai * sr
    return vr, vi


def _s5_fwd(u, bbd_re, bbd_im, cbd_re, cbd_im, pw_re, pw_im, dskip, name):
    s, d = u.shape
    nt, cw, lw = bbd_re.shape
    t = _tile(s, SCAN_T, SUB)
    nc = s // t
    ng = t // SUB

    def kern(u_ref, bre_ref, bim_ref, cre_ref, cim_ref, pre_ref, pim_ref, d_ref,
             xr_ref, xi_ref, y_ref, z_ref, car_ref, cai_ref):
        c = pl.program_id(1)

        @pl.when(c == 0)
        def _():
            car_ref[...] = jnp.zeros_like(car_ref)
            cai_ref[...] = jnp.zeros_like(cai_ref)

        u_ = u_ref[...]
        ub = u_.astype(BF16)
        vr, vi = _local_scan(_dot(ub, bre_ref[...]).reshape(ng, SUB, lw),
                             _dot(ub, bim_ref[...]).reshape(ng, SUB, lw), pre_ref, pim_ref, False)
        cr = car_ref[...]
        ci = cai_ref[...]
        pr = pre_ref[...]
        pi = pim_ref[...]
        for gidx in range(ng):
            rows = slice(gidx * SUB, (gidx + 1) * SUB)
            gr = vr[gidx] + pr * cr - pi * ci
            gi = vi[gidx] + pr * ci + pi * cr
            xr_ref[rows, :] = gr
            xi_ref[rows, :] = gi
            cr, ci = gr[SUB - 1:SUB, :], gi[SUB - 1:SUB, :]
        car_ref[...] = cr
        cai_ref[...] = ci
        y = (_dot(xr_ref[...].astype(BF16), cre_ref[...]) - _dot(xi_ref[...].astype(BF16), cim_ref[...])
             + d_ref[...] * u_)
        y_ref[...] = y
        z_ref[...] = _gelu(y).astype(BF16)

    tok = pl.BlockSpec((t, cw), lambda j, c: (c, j))
    st = pl.BlockSpec((t, lw), lambda j, c: (c, j))
    return pl.pallas_call(
        kern, name=name, grid=(nt, nc),
        in_specs=[tok,
                  pl.BlockSpec((None, cw, lw), lambda j, c: (j, 0, 0)),
                  pl.BlockSpec((None, cw, lw), lambda j, c: (j, 0, 0)),
                  pl.BlockSpec((None, lw, cw), lambda j, c: (j, 0, 0)),
                  pl.BlockSpec((None, lw, cw), lambda j, c: (j, 0, 0)),
                  pl.BlockSpec((SUB, lw), lambda j, c: (0, j)),
                  pl.BlockSpec((SUB, lw), lambda j, c: (0, j)),
                  pl.BlockSpec((1, cw), lambda j, c: (0, j))],
        out_specs=[st, st, tok, tok],
        out_shape=[jax.ShapeDtypeStruct((s, nt * lw), F32), jax.ShapeDtypeStruct((s, nt * lw), F32),
                   jax.ShapeDtypeStruct((s, d), F32), jax.ShapeDtypeStruct((s, d), BF16)],
        scratch_shapes=[pltpu.VMEM((1, lw), F32), pltpu.VMEM((1, lw), F32)],
        compiler_params=_params(("parallel", "arbitrary")),
    )(u, bbd_re, bbd_im, cbd_re, cbd_im, pw_re, pw_im, dskip)


def _s5_bwd(dz, y, u, xs_re, xs_im, bbd_re, bbd_im, cbd_re, cbd_im, pw_re, pw_im, pf_re, pf_im, dskip, name):
    s, d = u.shape
    nt, cw, lw = bbd_re.shape
    t = _tile(s, SCAN_T, SUB)
    nc = s // t
    ng = t // SUB

    def kern(dz_ref, y_ref, u_ref, xr_ref, xi_ref, bre_ref, bim_ref, cre_ref, cim_ref,
             pre_ref, pim_ref, fre_ref, fim_ref, d_ref,
             du_ref, dd_ref, dcr_ref, dci_ref, dbr_ref, dbi_ref, dar_ref, dai_ref,
             car_ref, cai_ref, gr_ref, gi_ref):
        c = pl.program_id(1)

        @pl.when(c == 0)
        def _():
            car_ref[...] = jnp.zeros_like(car_ref)
            cai_ref[...] = jnp.zeros_like(cai_ref)

        u_ = u_ref[...]
        ub = u_.astype(BF16)
        dy = dz_ref[...] * _gelu_grad(y_ref[...])
        dyb = dy.astype(BF16)
        vr, vi = _local_scan(_dot_nt(dyb, cre_ref[...]).reshape(ng, SUB, lw),
                             (-_dot_nt(dyb, cim_ref[...])).reshape(ng, SUB, lw), pre_ref, pim_ref, True)
        later_r = car_ref[...]
        later_i = cai_ref[...]
        cr, ci = later_r, later_i
        fr = fre_ref[...]
        fi = fim_ref[...]
        for gidx in reversed(range(ng)):
            rows = slice(gidx * SUB, (gidx + 1) * SUB)
            ar = vr[gidx] + fr * cr + fi * ci
            ai = vi[gidx] + fr * ci - fi * cr
            gr_ref[rows, :] = ar
            gi_ref[rows, :] = ai
            cr, ci = ar[0:1, :], ai[0:1, :]
        car_ref[...] = cr
        cai_ref[...] = ci
        gr = gr_ref[...]
        gi = gi_ref[...]
        rows = lax.broadcasted_iota(jnp.int32, (t, lw), 0)
        gsr = jnp.where(rows < t - 1, pltpu.roll(gr, t - 1, 0), later_r)
        gsi = jnp.where(rows < t - 1, pltpu.roll(gi, t - 1, 0), later_i)
        xr = xr_ref[...]
        xi = xi_ref[...]
        dar = jnp.sum(gsr * xr + gsi * xi, axis=0, keepdims=True)
        dai = jnp.sum(gsi * xr - gsr * xi, axis=0, keepdims=True)
        grb = gr.astype(BF16)
        gib = gi.astype(BF16)
        dbr = _dot_tn(ub, grb)
        dbi = _dot_tn(ub, gib)
        dcr = _dot_tn(dyb, xr.astype(BF16))
        dci = _dot_tn(dyb, xi.astype(BF16))
        du_ref[...] = dy * d_ref[...] + _dot_nt(grb, bre_ref[...]) + _dot_nt(gib, bim_ref[...])
        ddv = jnp.sum(dy * u_, axis=0, keepdims=True)

        @pl.when(c == 0)
        def _():
            dd_ref[...] = ddv
            dcr_ref[...] = dcr
            dci_ref[...] = dci
            dbr_ref[...] = dbr
            dbi_ref[...] = dbi
            dar_ref[...] = dar
            dai_ref[...] = dai

        @pl.when(c > 0)
        def _():
            dd_ref[...] += ddv
            dcr_ref[...] += dcr
            dci_ref[...] += dci
            dbr_ref[...] += dbr
            dbi_ref[...] += dbi
            dar_ref[...] += dar
            dai_ref[...] += dai

    tok = pl.BlockSpec((t, cw), lambda j, c: (nc - 1 - c, j))
    st = pl.BlockSpec((t, lw), lambda j, c: (nc - 1 - c, j))
    wb = pl.BlockSpec((None, cw, lw), lambda j, c: (j, 0, 0))
    wc = pl.BlockSpec((None, lw, cw), lambda j, c: (j, 0, 0))
    pw = pl.BlockSpec((SUB, lw), lambda j, c: (0, j))
    vec_c = pl.BlockSpec((1, cw), lambda j, c: (0, j))
    vec_l = pl.BlockSpec((1, lw), lambda j, c: (0, j))
    return pl.pallas_call(
        kern, name=name, grid=(nt, nc),
        in_specs=[tok, tok, tok, st, st, wb, wb, wc, wc, pw, pw, pw, pw, vec_c],
        out_specs=[tok, vec_c, wb, wb, wb, wb, vec_l, vec_l],
        out_shape=[jax.ShapeDtypeStruct((s, d), F32), jax.ShapeDtypeStruct((1, d), F32)]
        + [jax.ShapeDtypeStruct((nt, cw, lw), F32)] * 4
        + [jax.ShapeDtypeStruct((1, nt * lw), F32)] * 2,
        scratch_shapes=[pltpu.VMEM((1, lw), F32), pltpu.VMEM((1, lw), F32),
                        pltpu.VMEM((t, lw), F32), pltpu.VMEM((t, lw), F32)],
        compiler_params=_params(("parallel", "arbitrary")),
    )(dz, y, u, xs_re, xs_im, bbd_re, bbd_im, cbd_re, cbd_im, pw_re, pw_im, pf_re, pf_im, dskip)


def _s5_discretize(lam_re, lam_im, log_dt, b_re, b_im):
    dt = jnp.exp(log_dt)[:, None]
    mag = jnp.exp(lam_re * dt)
    ang = lam_im * dt
    lb_re = mag * jnp.cos(ang)
    lb_im = mag * jnp.sin(ang)
    nr = lb_re - 1.0
    den = lam_re * lam_re + lam_im * lam_im
    f_re = (nr * lam_re + lb_im * lam_im) / den
    f_im = (lb_im * lam_re - nr * lam_im) / den
    bb_re = f_re[..., None] * b_re - f_im[..., None] * b_im
    bb_im = f_re[..., None] * b_im + f_im[..., None] * b_re
    return lb_re, lb_im, bb_re, bb_im


def _block_diag(w):
    nt, ng, a, b = w.shape
    eye = jnp.eye(ng, dtype=w.dtype)
    return (w[:, :, :, None, :] * eye[None, :, None, :, None]).reshape(nt, ng * a, ng * b)


def _block_diag_take(w, a, b):
    nt = w.shape[0]
    ng = GROUPS_PER_TILE
    w5 = w.reshape(nt, ng, a, ng, b)
    on_diagonal = jnp.eye(ng, dtype=bool)[None, :, None, :, None]
    return jnp.sum(jnp.where(on_diagonal, w5, 0.0), axis=3).reshape(nt * ng, a, b)


def _att_combine(outs, lses, name):
    s, d = outs[0].shape
    tm = _tile(s, ROWS, 16)
    ng = len(outs)

    def kern(*refs):
        for h in range(d // HEAD_DIM):
            cols = slice(h * HEAD_DIM, (h + 1) * HEAD_DIM)
            ls = [refs[ng + t][:, h:h + 1] for t in range(ng)]
            mx = functools.reduce(jnp.maximum, ls)
            es = [jnp.exp(l - mx) for l in ls]
            den = functools.reduce(lambda a, b: a + b, es)
            o = functools.reduce(lambda a, b: a + b, [es[t] / den * refs[t][:, cols] for t in range(ng)])
            refs[2 * ng][:, cols] = o
            refs[2 * ng + 1][:, cols] = o.astype(BF16)

    return pl.pallas_call(
        kern, name=name, grid=(s // tm,),
        in_specs=[_row_spec(tm, d)] * ng + [_row_spec(tm, d // HEAD_DIM)] * ng,
        out_specs=[_row_spec(tm, d)] * 2,
        out_shape=[jax.ShapeDtypeStruct((s, d), F32), jax.ShapeDtypeStruct((s, d), BF16)],
        compiler_params=_params(("parallel",)),
    )(*outs, *lses)


DIL_MAX = 16


def _slab(r):
    return 4 * (r % 4) + r // 4


def _to_slabs(xs, name):
    s, w = xs[0].shape
    n = s // DIL_MAX
    nx = len(xs)

    def kern(*refs):
        o_ref = refs[nx]
        for r in range(DIL_MAX):
            rows = [x_ref[pl.ds(r, n, stride=DIL_MAX), :] for x_ref in refs[:nx]]
            o_ref[_slab(r) * n:(_slab(r) + 1) * n, :] = functools.reduce(lambda a, b: a + b, rows)

    spec = pl.BlockSpec((s, 128), lambda i: (0, i))
    return pl.pallas_call(
        kern, name=name, grid=(w // 128,), in_specs=[spec] * nx, out_specs=spec,
        out_shape=jax.ShapeDtypeStruct((s, w), F32), compiler_params=_params(("parallel",)),
    )(*xs)


def _from_slabs(x, name):
    s, w = x.shape
    n = s // DIL_MAX

    def kern(x_ref, o_ref, ob_ref):
        for r in range(DIL_MAX):
            o_ref[pl.ds(r, n, stride=DIL_MAX), :] = x_ref[_slab(r) * n:(_slab(r) + 1) * n, :]
        ob_ref[...] = o_ref[...].astype(BF16)

    spec = pl.BlockSpec((s, 128), lambda i: (0, i))
    return pl.pallas_call(
        kern, name=name, grid=(w // 128,), in_specs=[spec], out_specs=[spec, spec],
        out_shape=[jax.ShapeDtypeStruct((s, w), F32), jax.ShapeDtypeStruct((s, w), BF16)],
        compiler_params=_params(("parallel",)),
    )(x)


def _norm_bf16(x, gains, name):
    s, d = x.shape
    tm = _tile(s, ROWS, 16)
    ng = len(gains)

    def kern(x_ref, *refs):
        x_ = x_ref[...]
        for t in range(ng):
            refs[ng + t][...] = _rms(x_, refs[t][...]).astype(BF16)

    return pl.pallas_call(
        kern, name=name, grid=(s // tm,),
        in_specs=[_row_spec(tm, d)] + [_vec_spec(d)] * ng, out_specs=[_row_spec(tm, d)] * ng,
        out_shape=[jax.ShapeDtypeStruct((s, d), BF16)] * ng,
        compiler_params=_params(("parallel",)),
    )(x, *gains)


def _att_geometry(dil, s):
    g = DIL_MAX // dil
    return s // DIL_MAX, g, max(ATT_BLK // g, 16)


def _att_mask(g, j, jb, rep):
    b = g * j

    def pos(i):
        sl, jj = i // j, i % j
        off = {1: 0, 4: sl, 16: sl // 4 + 4 * (sl % 4)}[g]
        return g * jj + off

    qi = lax.broadcasted_iota(jnp.int32, (rep * b, 2 * b), 0) % b
    ki = lax.broadcasted_iota(jnp.int32, (rep * b, 2 * b), 1)
    prev = ki < b
    dist = pos(qi) - pos(ki % b) + jnp.where(prev, b, 0)
    return (dist >= 0) & (dist <= ATT_BLK) & (jnp.logical_not(prev) | (jb > 0))


def _stack_heads(x, rep):
    return jnp.concatenate([x[:, h * HEAD_DIM:(h + 1) * HEAD_DIM] for h in range(rep)], axis=0)


def _att_fwd(q, kv, gi, dil, nh, name):
    s = q.shape[0]
    n, g, j = _att_geometry(dil, s)
    b = g * j
    ng = q.shape[1] // (nh * HEAD_DIM)
    rep = nh // N_KV_HEADS
    qw = rep * HEAD_DIM
    scale = HEAD_DIM ** -0.5
    q3 = q.reshape(DIL_MAX, n, q.shape[1])
    kv3 = kv.reshape(DIL_MAX, n, kv.shape[1])

    kw = N_KV_HEADS * HEAD_DIM

    def kern(q_ref, kc_ref, kp_ref, vc_ref, vp_ref, o_ref, l_ref, lw_ref):
        jb = pl.program_id(1)
        mask = _att_mask(g, j, jb, rep)
        for kh in range(N_KV_HEADS):
            hs = slice(kh * HEAD_DIM, (kh + 1) * HEAD_DIM)
            kc, kp, vc, vp = [r[:, :, hs].reshape(b, HEAD_DIM) for r in (kc_ref, kp_ref, vc_ref, vp_ref)]
            k2 = jnp.concatenate([kp, kc], axis=0)
            v2 = jnp.concatenate([vp, vc], axis=0)
            qs = _stack_heads(q_ref[:, :, kh * qw:(kh + 1) * qw].reshape(b, qw), rep)
            sc = jnp.where(mask, _dot_nt(qs, k2) * scale, NEG_INF)
            m = jnp.max(sc, axis=-1, keepdims=True)
            p = jnp.exp(sc - m)
            l = jnp.sum(p, axis=-1, keepdims=True)
            out = _dot((p / l).astype(BF16), v2)
            lse = m + jnp.log(l)
            wide = jnp.broadcast_to(lse, (rep * b, HEAD_DIM))
            for h in range(rep):
                cols = slice(kh * qw + h * HEAD_DIM, kh * qw + (h + 1) * HEAD_DIM)
                o_ref[:, :, cols] = out[h * b:(h + 1) * b].reshape(g, j, HEAD_DIM)
                lw_ref[:, :, cols] = wide[h * b:(h + 1) * b].reshape(g, j, HEAD_DIM)
                l_ref[:, :, kh * rep + h:kh * rep + h + 1] = lse[h * b:(h + 1) * b].reshape(g, j, 1)

    def kv_spec(col, prev):
        if prev:
            return pl.BlockSpec((g, j, kw), lambda c, jb: (c, jnp.maximum(jb - 1, 0), col))
        return pl.BlockSpec((g, j, kw), lambda c, jb: (c, jb, col))

    out_spec = pl.BlockSpec((g, j, nh * HEAD_DIM), lambda c, jb: (c, jb, 0))
    out, lse, lse_wide = pl.pallas_call(
        kern, name=name, grid=(dil, n // j),
        in_specs=[pl.BlockSpec((g, j, nh * HEAD_DIM), lambda c, jb: (c, jb, gi)),
                  kv_spec(0, False), kv_spec(0, True), kv_spec(1, False), kv_spec(1, True)],
        out_specs=[out_spec, pl.BlockSpec((g, j, nh), lambda c, jb: (c, jb, 0)), out_spec],
        out_shape=[jax.ShapeDtypeStruct((DIL_MAX, n, nh * HEAD_DIM), F32),
                   jax.ShapeDtypeStruct((DIL_MAX, n, nh), F32),
                   jax.ShapeDtypeStruct((DIL_MAX, n, nh * HEAD_DIM), F32)],
        compiler_params=_params(("parallel", "arbitrary")),
    )(q3, kv3, kv3, kv3, kv3)
    del ng
    return out.reshape(s, nh * HEAD_DIM), lse.reshape(s, nh), lse_wide.reshape(s, nh * HEAD_DIM)


def _att_bwd(q, kv, do, o, lses, dq_all, gi, dil, nh, name):
    s = q.shape[0]
    n, g, j = _att_geometry(dil, s)
    b = g * j
    nb = n // j
    ng = len(lses)
    rep = nh // N_KV_HEADS
    qw = rep * HEAD_DIM
    dm = nh * HEAD_DIM
    scale = HEAD_DIM ** -0.5
    q3 = q.reshape(DIL_MAX, n, q.shape[1])
    kv3 = kv.reshape(DIL_MAX, n, kv.shape[1])
    wide = [a.reshape(DIL_MAX, n, dm) for a in (do, o, *lses)]
    kw = N_KV_HEADS * HEAD_DIM

    def kern(q_ref, kc_ref, kp_ref, vc_ref, vp_ref, do_ref, o_ref, *refs):
        l_refs = refs[:ng]
        dq_ref, dk_ref, dv_ref, ck_ref, cv_ref = refs[ng + 1:]
        jb = pl.program_id(1)

        @pl.when(jb == 0)
        def _():
            ck_ref[...] = jnp.zeros_like(ck_ref)
            cv_ref[...] = jnp.zeros_like(cv_ref)

        @pl.when(jb < nb)
        def _():
            mask = _att_mask(g, j, jb, rep)
            for kh in range(N_KV_HEADS):
                hs = slice(kh * HEAD_DIM, (kh + 1) * HEAD_DIM)
                ws = slice(kh * qw, (kh + 1) * qw)

                def stacked(ref):
                    return _stack_heads(ref[:, :, ws].reshape(b, qw), rep)

                kc, kp, vc, vp = [r[:, :, hs].reshape(b, HEAD_DIM) for r in (kc_ref, kp_ref, vc_ref, vp_ref)]
                k2 = jnp.concatenate([kp, kc], axis=0)
                v2 = jnp.concatenate([vp, vc], axis=0)
                qs = stacked(q_ref)
                ls = [stacked(r)[:, 0:1] for r in l_refs]
                mx = functools.reduce(jnp.maximum, ls)
                den = functools.reduce(lambda a, c: a + c, [jnp.exp(l - mx) for l in ls])
                lse_g = ls[gi]
                w = jnp.exp(lse_g - mx) / den
                do_ = stacked(do_ref)
                ct = w * jnp.sum(do_ * stacked(o_ref), axis=-1, keepdims=True)
                dob = (w * do_).astype(BF16)
                p = jnp.exp(jnp.where(mask, _dot_nt(qs, k2) * scale, NEG_INF) - lse_g)
                ds = (p * (_dot_nt(dob, v2) - ct) * scale).astype(BF16)
                dq = (_dot(ds, k2)).astype(BF16)
                for h in range(rep):
                    cols = slice(kh * qw + h * HEAD_DIM, kh * qw + (h + 1) * HEAD_DIM)
                    dq_ref[:, :, cols] = dq[h * b:(h + 1) * b].reshape(g, j, HEAD_DIM)
                dk2 = _dot_tn(ds, qs)
                dv2 = _dot_tn(p.astype(BF16), dob)
                dk_ref[:, :, hs] = (ck_ref[:, hs] + dk2[:b]).reshape(g, j, HEAD_DIM)
                dv_ref[:, :, hs] = (cv_ref[:, hs] + dv2[:b]).reshape(g, j, HEAD_DIM)
                ck_ref[:, hs] = dk2[b:]
                cv_ref[:, hs] = dv2[b:]

        @pl.when(jb == nb)
        def _():
            dk_ref[...] = ck_ref[...].reshape(g, j, kw)
            dv_ref[...] = cv_ref[...].reshape(g, j, kw)

    def jq(jb):
        return jnp.minimum(jb, nb - 1)

    def kv_spec(col, prev):
        if prev:
            return pl.BlockSpec((g, j, kw), lambda c, jb: (c, jnp.maximum(jq(jb) - 1, 0), col))
        return pl.BlockSpec((g, j, kw), lambda c, jb: (c, jq(jb), col))

    wide_spec = pl.BlockSpec((g, j, dm), lambda c, jb: (c, jq(jb), 0))
    dkv_spec = pl.BlockSpec((g, j, kw), lambda c, jb: (c, jnp.maximum(jb - 1, 0), 0))
    dq, dk, dv = pl.pallas_call(
        kern, name=name, grid=(dil, nb + 1),
        in_specs=[pl.BlockSpec((g, j, dm), lambda c, jb: (c, jq(jb), gi)),
                  kv_spec(0, False), kv_spec(0, True), kv_spec(1, False), kv_spec(1, True)]
        + [wide_spec] * (2 + ng) + [_ANY],
        out_specs=[pl.BlockSpec((g, j, dm), lambda c, jb: (c, jq(jb), gi)), dkv_spec, dkv_spec],
        out_shape=[jax.ShapeDtypeStruct((DIL_MAX, n, dq_all.shape[1]), BF16),
                   jax.ShapeDtypeStruct((DIL_MAX, n, kw), F32),
                   jax.ShapeDtypeStruct((DIL_MAX, n, kw), F32)],
        scratch_shapes=[pltpu.VMEM((b, kw), F32), pltpu.VMEM((b, kw), F32)],
        input_output_aliases={7 + ng: 0},
        compiler_params=_params(("parallel", "arbitrary")),
    )(q3, kv3, kv3, kv3, kv3, *wide, dq_all.reshape(DIL_MAX, n, dq_all.shape[1]))
    return dq.reshape(s, -1), dk.reshape(s, -1), dv.reshape(s, -1)


def _dkv_sum(dks, dvs, name):
    s, w = dks[0].shape
    tm = _tile(s, ROWS, 16)
    ng = len(dks)

    def kern(*refs):
        o_ref = refs[2 * ng]
        o_ref[:, :w] = functools.reduce(lambda a, b: a + b, [refs[t][...] for t in range(ng)]).astype(BF16)
        o_ref[:, w:] = functools.reduce(lambda a, b: a + b, [refs[ng + t][...] for t in range(ng)]).astype(BF16)

    return pl.pallas_call(
        kern, name=name, grid=(s // tm,),
        in_specs=[_row_spec(tm, w)] * (2 * ng),
        out_specs=_row_spec(tm, 2 * w),
        out_shape=jax.ShapeDtypeStruct((s, 2 * w), BF16),
        compiler_params=_params(("parallel",)),
    )(*dks, *dvs)


def _ffn_fwd(hf, weight, layer, tag):
    a, gu = _ffn_in_act(hf, weight(f"w_in{layer}", hf), f"{tag}_in")
    o = _mm_nn(a, weight(f"w_out{layer}", a), F32, f"{tag}_out", tn=1024, tk=5632)
    return gu, a, o


def _ffn_bwd(dresb, hf, gu, a, weight, emit, layer, tag):
    w_in, w_out = weight(f"w_in{layer}", None), weight(f"w_out{layer}", None)
    zero = emit(f"w_out{layer}", _mm_tn(a, dresb, 1, f"{tag}_out_dw", tn=1024))
    dgu = _ffn_out_dx_act(dresb, w_out, gu, f"{tag}_out_dx")
    zero = zero + emit(f"w_in{layer}",
                       _mm_tn(hf, dgu, w_in.shape[0], f"{tag}_in_dw", tn=FFN_TILE, paired=True))
    return _mm_nt(dgu, w_in, f"{tag}_in_dx", tr=FFN_TILE, tkc=512, paired=True), zero


def _local_step(x, target, small, weight, emit, stage):
    s, d = x.shape
    nh = d // HEAD_DIM
    n_groups = d // S5_GROUP_CH
    nt = n_groups // GROUPS_PER_TILE
    p_, c_ = S5_STATE, S5_GROUP_CH

    disc_in = (small["lam_re"], small["lam_im"], small["log_dt"], small["b_re"], small["b_im"])
    (lb_re, lb_im, bb_re, bb_im), disc_vjp = jax.vjp(_s5_discretize, *disc_in)
    del lb_re, lb_im
    dt = jnp.exp(small["log_dt"])[:, None]
    def pole_powers(exponents):
        k = exponents[:, None, None]
        mag = jnp.exp(k * (small["lam_re"] * dt)[None])
        ang = k * (small["lam_im"] * dt)[None]
        return ((mag * jnp.cos(ang)).reshape(SUB, n_groups * p_),
                (mag * jnp.sin(ang)).reshape(SUB, n_groups * p_))

    pw_re, pw_im = pole_powers(jnp.arange(1, SUB + 1, dtype=F32))
    pf_re, pf_im = pole_powers(jnp.arange(SUB, 0, -1, dtype=F32))
    bbd_re = _block_diag(bb_re.transpose(0, 2, 1).reshape(nt, GROUPS_PER_TILE, c_, p_)).astype(BF16)
    bbd_im = _block_diag(bb_im.transpose(0, 2, 1).reshape(nt, GROUPS_PER_TILE, c_, p_)).astype(BF16)
    cbd_re = _block_diag(small["c_re"].transpose(0, 2, 1).reshape(nt, GROUPS_PER_TILE, p_, c_)).astype(BF16)
    cbd_im = _block_diag(small["c_im"].transpose(0, 2, 1).reshape(nt, GROUPS_PER_TILE, p_, c_)).astype(BF16)

    h0 = _norm_f32(x, small["a_norm"], "s5_norm")
    xs_re, xs_im, y0, z = _s5_fwd(h0, bbd_re, bbd_im, cbd_re, cbd_im, pw_re, pw_im, small["s5_d"], "s5_fwd")
    zero = stage("glu", z)
    vg = _mm_nn(z, weight("w_glu", z), F32, "glu_mm", tn=1024)
    x1, hf0 = _glu_res_norm(vg, x, small["ffn_norm0"] + zero, "glu_res_norm")
    gu0, a0, o0 = _ffn_fwd(hf0, weight, 0, "ffn0")
    zero = stage("attention", a0)
    x2 = _to_slabs([o0, x1], "to_slabs")
    kvn, h1 = _norm_bf16(x2, [small["kv_norm"] + zero, small["b_norm"]], "att_norms")
    kv = _mm_nn(kvn, weight("w_kv", kvn), BF16, "kv_mm", tn=1024)
    q = _mm_nn(h1, weight("w_q", kv), BF16, "q_mm", tn=1536)
    outs, lses_narrow, lses = [], [], []
    for gi, (window, dil) in enumerate(PATTERNS):
        assert window // dil == ATT_BLK
        og, lg, lg_wide = _att_fwd(q, kv, gi, dil, nh, f"att_fwd{gi}")
        outs.append(og)
        lses_narrow.append(lg)
        lses.append(lg_wide)
    oatt, oattb = _att_combine(outs, lses_narrow, "att_combine")
    ao = _mm_nn(oattb, weight("w_o", oattb), F32, "o_mm", tn=1024)
    x3, hf1 = _res_norm(ao, x2, [small["ffn_norm1"] + stage("ffn1", oattb)], "att_res_norm")
    gu1, a1, o1 = _ffn_fwd(hf1, weight, 1, "ffn1")
    dres, dresb, loss_rows, d_final = _loss_head(
        o1, x3, small["final_norm"], _to_slabs([target], "target_to_slabs"), "loss_head")

    dhf1, zero = _ffn_bwd(dresb, hf1, gu1, a1, weight, emit, 1, "ffn1")
    dres, dresb, d_ffn_norm1 = _norm_bwd(x3, [small["ffn_norm1"] + zero], [dhf1], dres, "ffn1_norm_bwd")
    zero = emit("w_o", _mm_tn(oattb, dresb, 1, "o_dw", tn=1024))
    doatt = _mm_nt(dresb, weight("w_o", None), "o_dx", tr=2048)
    dks, dvs = [], []
    dq = lax.empty(q.shape, BF16)
    for gi, (window, dil) in enumerate(PATTERNS):
        dq, dk_g, dv_g = _att_bwd(q, kv, doatt, oatt, lses, dq, gi, dil, nh, f"att_bwd{gi}")
        dks.append(dk_g)
        dvs.append(dv_g)
    dkv = _dkv_sum(dks, dvs, "dkv_sum")
    w_q = weight("w_q", None)
    zero = zero + emit("w_q", _mm_tn(h1, dq, w_q.shape[0], "q_dw", tn=1536))
    zero = zero + emit("w_kv", _mm_tn(kvn, dkv, 1, "kv_dw", tn=1024))
    dh1 = _mm_nt(dq, w_q, "q_dx", tr=1536)
    dkvn = _mm_nt(dkv, weight("w_kv", None), "kv_dx", tr=1024)
    dres, dresb, d_b_norm, d_kv_norm = _norm_bwd(
        x2, [small["b_norm"] + zero, small["kv_norm"]], [dh1, dkvn], dres, "att_norm_bwd")
    dres, dresb = _from_slabs(dres, "from_slabs")
    dhf0, zero = _ffn_bwd(dresb, hf0, gu0, a0, weight, emit, 0, "ffn0")
    dres, dresb, d_ffn_norm0 = _norm_bwd(x1, [small["ffn_norm0"] + zero], [dhf0], dres, "ffn0_norm_bwd")
    del dresb
    dvg = _glu_bwd(dres, vg, "glu_bwd")
    w_glu = weight("w_glu", None)
    zero = emit("w_glu", _mm_tn(z, dvg, w_glu.shape[0], "glu_dw", tn=1024))
    dz = _mm_nt(dvg, w_glu, "glu_dx", tr=1024)
    dh0, d_s5_d, dcr, dci_neg, dbr, dbi, dar, dai = _s5_bwd(
        dz, y0, h0, xs_re, xs_im, bbd_re, bbd_im, cbd_re, cbd_im, pw_re, pw_im, pf_re, pf_im,
        small["s5_d"] + zero, "s5_bwd")
    grad_x, _, d_a_norm = _norm_bwd(x, [small["a_norm"]], [dh0], dres, "s5_norm_bwd")

    d_bb_re = _block_diag_take(dbr, c_, p_).transpose(0, 2, 1)
    d_bb_im = _block_diag_take(dbi, c_, p_).transpose(0, 2, 1)
    d_c_re = _block_diag_take(dcr, c_, p_)
    d_c_im = -_block_diag_take(dci_neg, c_, p_)
    d_lam_re, d_lam_im, d_log_dt, d_b_re, d_b_im = disc_vjp(
        (dar.reshape(n_groups, p_), dai.reshape(n_groups, p_), d_bb_re, d_bb_im))

    d_small = dict(lam_re=d_lam_re, lam_im=d_lam_im, log_dt=d_log_dt, b_re=d_b_re, b_im=d_b_im,
                   c_re=d_c_re, c_im=d_c_im, s5_d=d_s5_d, a_norm=d_a_norm, ffn_norm0=d_ffn_norm0,
                   ffn_norm1=d_ffn_norm1, b_norm=d_b_norm, kv_norm=d_kv_norm, final_norm=d_final)
    return loss_rows, grad_x, d_small


def _place():
    x, y, c = lax.axis_index("x"), lax.axis_index("y"), lax.axis_index("c")
    return x, y, c, [(1 - x, y), (x, 1 - y), (1 - x, 1 - y)]


_ANY = pl.BlockSpec(memory_space=pl.ANY)


_HBM = pl.BlockSpec(memory_space=pltpu.HBM)
_SEM = pl.BlockSpec(memory_space=pltpu.SEMAPHORE)
_EFFECT = pltpu.SideEffectType.DATAFLOW_SIDE_EFFECTING


def _in_hbm(a):
    return pltpu.with_memory_space_constraint(a, pltpu.HBM)


def _cast_place(shards, layer, chip, name):
    _, r, c = shards.shape
    tm = _tile(r, ROWS, 16)

    def kern(chip_ref, x_ref, o_ref):
        del chip_ref
        o_ref[...] = x_ref[...].astype(BF16)

    return pl.pallas_call(
        kern, name=name,
        grid_spec=pltpu.PrefetchScalarGridSpec(
            num_scalar_prefetch=1, grid=(r // tm,),
            in_specs=[pl.BlockSpec((None, tm, c), lambda i, ch: (layer, i, 0))],
            out_specs=pl.BlockSpec((None, tm, c), lambda i, ch: (ch[0], i, 0))),
        out_shape=jax.ShapeDtypeStruct((N_CHIPS, r, c), BF16),
        compiler_params=_params(("parallel",)),
    )(chip, shards)


def _my_part(land, block, c, halves):
    if not halves:
        return land.at[block]
    half = land.shape[1] // 2
    return land.at[block, pl.ds(c * half, half)]


def _gather_start(lands, name, halves, after):
    n = len(lands)

    def body(*refs):
        land = refs[:n]
        send, recv = refs[n + 1:2 * n + 1], refs[2 * n + 1:3 * n + 1]
        token = refs[4 * n + 1]
        x, y, c, peers = _place()
        me = 2 * x + y
        for a in range(n):
            for k, (px, py) in enumerate(peers):
                part = _my_part(land[a], me, c, halves)
                pltpu.make_async_remote_copy(
                    src_ref=part, dst_ref=part, send_sem=send[a].at[k], recv_sem=recv[a].at[k],
                    device_id=(px, py, c), device_id_type=MESH).start()
        token[...] = jnp.zeros_like(token)

    outs = pl.pallas_call(
        body, name=name,
        out_shape=[pltpu.SemaphoreType.DMA((3,))] * (2 * n) + [pltpu.HBM(a.shape, a.dtype) for a in lands]
        + [jax.ShapeDtypeStruct((8, 128), F32)],
        in_specs=[_HBM] * n + [_ANY],
        out_specs=[_SEM] * (2 * n) + [_HBM] * n + [pl.BlockSpec(memory_space=pltpu.VMEM)],
        input_output_aliases={i: 2 * n + i for i in range(n)},
        compiler_params=pltpu.CompilerParams(has_side_effects=_EFFECT),
    )(*[_in_hbm(a) for a in lands], after)
    return outs[:n], outs[n:2 * n], outs[2 * n:3 * n], outs[3 * n]


def _gather_wait(lands, sends, recvs, after, name, halves):
    n = len(lands)

    def body(*refs):
        land, send, recv = refs[:n], refs[n:2 * n], refs[2 * n:3 * n]
        x, y, c, peers = _place()
        me = 2 * x + y
        for a in range(n):
            for k, (px, py) in enumerate(peers):
                cp = pltpu.make_async_remote_copy(
                    src_ref=_my_part(land[a], me, c, halves), dst_ref=_my_part(land[a], 2 * px + py, c, halves),
                    send_sem=send[a].at[k], recv_sem=recv[a].at[k], device_id=(px, py, c), device_id_type=MESH)
                cp.wait_send()
                cp.wait_recv()

    return pl.pallas_call(
        body, name=name,
        out_shape=[pltpu.HBM(a.shape, a.dtype) for a in lands],
        in_specs=[_HBM] * n + [_SEM] * (2 * n) + [_ANY], out_specs=[_HBM] * n,
        input_output_aliases={i: i for i in range(n)},
        compiler_params=pltpu.CompilerParams(has_side_effects=_EFFECT),
    )(*lands, *sends, *recvs, after)


def _forward_start(lands, name):
    n = len(lands)

    def body(*refs):
        land = refs[:n]
        send, recv = refs[n:2 * n], refs[2 * n:3 * n]
        token = refs[4 * n]
        x, y, c, peers = _place()
        for a in range(n):
            for k, (px, py) in enumerate(peers):
                part = _my_part(land[a], 2 * px + py, c, True)
                pltpu.make_async_remote_copy(
                    src_ref=part, dst_ref=part, send_sem=send[a].at[k], recv_sem=recv[a].at[k],
                    device_id=(x, y, 1 - c), device_id_type=MESH).start()
        token[...] = jnp.zeros_like(token)

    outs = pl.pallas_call(
        body, name=name,
        out_shape=[pltpu.SemaphoreType.DMA((3,))] * (2 * n) + [pltpu.HBM(a.shape, a.dtype) for a in lands]
        + [jax.ShapeDtypeStruct((8, 128), F32)],
        in_specs=[_HBM] * n,
        out_specs=[_SEM] * (2 * n) + [_HBM] * n + [pl.BlockSpec(memory_space=pltpu.VMEM)],
        input_output_aliases={i: 2 * n + i for i in range(n)},
        compiler_params=pltpu.CompilerParams(has_side_effects=_EFFECT),
    )(*[_in_hbm(a) for a in lands])
    return outs[:n], outs[n:2 * n], outs[2 * n:3 * n], outs[3 * n]


def _forward_wait(land, send, recv, after, name):
    def body(land_ref, send_sem, recv_sem, after_ref, land_out):
        del after_ref, land_out
        x, y, c, peers = _place()
        for k, (px, py) in enumerate(peers):
            cp = pltpu.make_async_remote_copy(
                src_ref=_my_part(land_ref, 2 * px + py, c, True), dst_ref=_my_part(land_ref, 2 * px + py, 1 - c, True),
                send_sem=send_sem.at[k], recv_sem=recv_sem.at[k], device_id=(x, y, 1 - c), device_id_type=MESH)
            cp.wait_send()
            cp.wait_recv()

    return pl.pallas_call(
        body, name=name,
        out_shape=pltpu.HBM(land.shape, land.dtype),
        in_specs=[_HBM, _SEM, _SEM, _ANY], out_specs=_HBM,
        input_output_aliases={0: 0},
        compiler_params=pltpu.CompilerParams(has_side_effects=_EFFECT),
    )(land, send, recv, after)


def _scatter_start(g, name):
    def body(g_ref, land_ref, send, recv, g_out, land_out, token):
        del g_out, land_out
        x, y, c, peers = _place()
        for k, (px, py) in enumerate(peers):
            pltpu.make_async_remote_copy(
                src_ref=g_ref.at[2 * px + py], dst_ref=land_ref.at[k], send_sem=send.at[k], recv_sem=recv.at[k],
                device_id=(px, py, c), device_id_type=MESH).start()
        token[...] = jnp.zeros_like(token)

    land = lax.empty((3,) + g.shape[1:], g.dtype)
    return pl.pallas_call(
        body, name=name,
        out_shape=(pltpu.SemaphoreType.DMA((3,)), pltpu.SemaphoreType.DMA((3,)),
                   pltpu.HBM(g.shape, g.dtype), pltpu.HBM(land.shape, land.dtype),
                   jax.ShapeDtypeStruct((8, 128), F32)),
        in_specs=(_HBM, _HBM), out_specs=(_SEM, _SEM, _HBM, _HBM, pl.BlockSpec(memory_space=pltpu.VMEM)),
        input_output_aliases={0: 2, 1: 3},
        compiler_params=pltpu.CompilerParams(has_side_effects=_EFFECT),
    )(_in_hbm(g), _in_hbm(land))


def _scatter_wait(started, after):
    n = len(started)

    def body(*refs):
        gs, lands = refs[:n], refs[n:2 * n]
        sends, recvs = refs[2 * n:3 * n], refs[3 * n:4 * n]
        x, y, c, peers = _place()
        for a in range(n):
            for k, (px, py) in enumerate(peers):
                cp = pltpu.make_async_remote_copy(
                    src_ref=gs[a].at[2 * px + py], dst_ref=lands[a].at[k], send_sem=sends[a].at[k],
                    recv_sem=recvs[a].at[k], device_id=(px, py, c), device_id_type=MESH)
                cp.wait_send()
                cp.wait_recv()

    gs = [s[2] for s in started]
    lands = [s[3] for s in started]
    outs = pl.pallas_call(
        body, name="scatter_wait",
        out_shape=[pltpu.HBM(a.shape, a.dtype) for a in gs + lands],
        in_specs=[_HBM] * (2 * n) + [_SEM] * (2 * n) + [_ANY], out_specs=[_HBM] * (2 * n),
        input_output_aliases={i: i for i in range(2 * n)},
        compiler_params=pltpu.CompilerParams(has_side_effects=_EFFECT),
    )(*gs, *lands, *[s[0] for s in started], *[s[1] for s in started], after)
    return outs[:n], outs[n:]


def _sibling():
    return lax.axis_index("x"), lax.axis_index("y"), 1 - lax.axis_index("c")


def _swap_start(parts):
    n = len(parts)

    def body(*refs):
        ins, land = refs[:n], refs[n:2 * n]
        send, recv = refs[2 * n], refs[2 * n + 1]
        token = refs[4 * n + 2]
        for a in range(n):
            pltpu.make_async_remote_copy(
                src_ref=ins[a], dst_ref=land[a], send_sem=send.at[a], recv_sem=recv.at[a],
                device_id=_sibling(), device_id_type=MESH).start()
        token[...] = jnp.zeros_like(token)

    lands = [lax.empty(a.shape, a.dtype) for a in parts]
    outs = pl.pallas_call(
        body, name="swap_start",
        out_shape=[pltpu.SemaphoreType.DMA((n,))] * 2 + [pltpu.HBM(a.shape, a.dtype) for a in parts + lands]
        + [jax.ShapeDtypeStruct((8, 128), F32)],
        in_specs=[_HBM] * (2 * n),
        out_specs=[_SEM] * 2 + [_HBM] * (2 * n) + [pl.BlockSpec(memory_space=pltpu.VMEM)],
        input_output_aliases={i: 2 + i for i in range(2 * n)},
        compiler_params=pltpu.CompilerParams(has_side_effects=_EFFECT),
    )(*[_in_hbm(a) for a in parts + lands])
    return outs[0], outs[1], outs[2:2 + n], outs[2 + n:2 + 2 * n], outs[2 + 2 * n]


def _swap_wait(parts, lands, send, recv, after):
    n = len(parts)

    def body(*refs):
        ins, land = refs[:n], refs[n:2 * n]
        send_sem, recv_sem = refs[2 * n], refs[2 * n + 1]
        for a in range(n):
            cp = pltpu.make_async_remote_copy(
                src_ref=ins[a], dst_ref=land[a], send_sem=send_sem.at[a], recv_sem=recv_sem.at[a],
                device_id=_sibling(), device_id_type=MESH)
            cp.wait_send()
            cp.wait_recv()

    outs = pl.pallas_call(
        body, name="swap_wait",
        out_shape=[pltpu.HBM(a.shape, a.dtype) for a in list(parts) + list(lands)],
        in_specs=[_HBM] * (2 * n) + [_SEM] * 2 + [_ANY], out_specs=[_HBM] * (2 * n),
        input_output_aliases={i: i for i in range(2 * n)},
        compiler_params=pltpu.CompilerParams(has_side_effects=_EFFECT),
    )(*parts, *lands, send, recv, after)
    return outs[:n], outs[n:]


def _all_reduce_small(v):
    nd, r, w = v.shape
    assert nd == N_DEV

    def body(v_ref, out_ref, land_ref, red_ref, send1, recv1, send2, recv2):
        x, y, c = lax.axis_index("x"), lax.axis_index("y"), lax.axis_index("c")
        me = 4 * x + 2 * y + c
        peers = []
        for k in range(1, N_DEV):
            kx, ky, kc = (k >> 2) & 1, (k >> 1) & 1, k & 1
            peers.append((1 - x if kx else x, 1 - y if ky else y, 1 - c if kc else c))
        first = []
        for k, (px, py, pc) in enumerate(peers):
            cp = pltpu.make_async_remote_copy(
                src_ref=v_ref.at[4 * px + 2 * py + pc], dst_ref=land_ref.at[me], send_sem=send1.at[k],
                recv_sem=recv1.at[k], device_id=(px, py, pc), device_id_type=MESH)
            cp.start()
            first.append(cp)
        land_ref[me] = v_ref[me]
        for cp in first:
            cp.wait()
        acc = land_ref[0]
        for j in range(1, N_DEV):
            acc = acc + land_ref[j]
        red_ref[...] = acc
        second = []
        for k, (px, py, pc) in enumerate(peers):
            cp = pltpu.make_async_remote_copy(
                src_ref=red_ref, dst_ref=out_ref.at[me], send_sem=send2.at[k],
                recv_sem=recv2.at[k], device_id=(px, py, pc), device_id_type=MESH)
            cp.start()
            second.append(cp)
        out_ref[me] = acc
        for cp in second:
            cp.wait()

    vmem = pl.BlockSpec(memory_space=pltpu.VMEM)
    return pl.pallas_call(
        body, name="all_reduce_small",
        in_specs=[vmem], out_specs=vmem,
        out_shape=jax.ShapeDtypeStruct((nd, r, w), F32),
        scratch_shapes=[pltpu.VMEM((nd, r, w), F32), pltpu.VMEM((r, w), F32)]
        + [pltpu.SemaphoreType.DMA((N_DEV - 1,))] * 4,
        compiler_params=pltpu.CompilerParams(vmem_limit_bytes=VMEM_LIMIT),
    )(v)


def _adam_math(w, g, m, v):
    m = ADAM_B1 * m + (1.0 - ADAM_B1) * g
    v = ADAM_B2 * v + (1.0 - ADAM_B2) * (g * g)
    m_hat = m / (1.0 - ADAM_B1 ** ADAM_STEP)
    v_hat = v / (1.0 - ADAM_B2 ** ADAM_STEP)
    delta = -ADAM_LR * (m_hat / (jnp.sqrt(v_hat) + ADAM_EPS) + ADAM_WD * w)
    return delta, m, v


def _sum_blocks(own, got, chip, name):
    _, r, c = own.shape
    tm = _tile(r, ROWS, 16)

    def kern(chip_ref, own_ref, got_ref, o_ref):
        del chip_ref
        acc = own_ref[...].astype(F32)
        for k in range(3):
            acc = acc + got_ref[k].astype(F32)
        o_ref[...] = acc

    return pl.pallas_call(
        kern, name=name,
        grid_spec=pltpu.PrefetchScalarGridSpec(
            num_scalar_prefetch=1, grid=(r // tm,),
            in_specs=[pl.BlockSpec((None, tm, c), lambda i, ch: (ch[0], i, 0)),
                      pl.BlockSpec((3, tm, c), lambda i, ch: (0, i, 0))],
            out_specs=pl.BlockSpec((tm, c), lambda i, ch: (i, 0))),
        out_shape=jax.ShapeDtypeStruct((r, c), F32),
        compiler_params=_params(("parallel",)),
    )(chip, own, got)


def _adamw(parts, w, m, v, name):
    nl, r, c = w.shape
    assert len(parts) == nl
    tm = _tile(r, 128, 8)

    def kern(*refs):
        p_refs = refs[:2 * nl]
        w_ref, m_ref, v_ref, g_ref, d_ref, mo_ref, vo_ref = refs[2 * nl:]
        layer = pl.program_id(0)
        g = p_refs[0][...] + p_refs[1][...]
        for ll in range(1, nl):
            g = jnp.where(layer == ll, p_refs[2 * ll][...] + p_refs[2 * ll + 1][...], g)
        g_ref[...] = g
        d_ref[...], mo_ref[...], vo_ref[...] = _adam_math(w_ref[...], g, m_ref[...], v_ref[...])

    def part_spec(ll):
        return pl.BlockSpec((tm, c), lambda l, i: (jnp.where(l == ll, i, 0), 0))

    spec = pl.BlockSpec((None, tm, c), lambda l, i: (l, i, 0))
    return pl.pallas_call(
        kern, name=name, grid=(nl, r // tm),
        in_specs=[part_spec(ll) for ll in range(nl) for _ in range(2)] + [spec] * 3, out_specs=[spec] * 4,
        out_shape=[jax.ShapeDtypeStruct((nl, r, c), F32)] * 4,
        compiler_params=_params(("arbitrary", "parallel")),
    )(*[p for pair in parts for p in pair], w, m, v)


def _adamw_small(g, w, m, v, name):
    def kern(g_ref, w_ref, m_ref, v_ref, d_ref, mo_ref, vo_ref):
        d_ref[...], mo_ref[...], vo_ref[...] = _adam_math(w_ref[...], g_ref[...], m_ref[...], v_ref[...])

    return pl.pallas_call(
        kern, name=name,
        out_shape=[jax.ShapeDtypeStruct(g.shape, F32)] * 3,
        compiler_params=pltpu.CompilerParams(vmem_limit_bytes=VMEM_LIMIT),
    )(g, w, m, v)


def _pack(arrays, rows):
    flat = jnp.concatenate([a.reshape(-1).astype(F32) for a in arrays])
    return jnp.pad(flat, (0, rows * 128 - flat.shape[0])).reshape(rows, 128)


def _unpack(packed, shapes):
    flat = packed.reshape(-1)
    out, off = [], 0
    for shp in shapes:
        size = math.prod(shp)
        out.append(flat[off:off + size].reshape(shp))
        off += size
    return out


_REPLICATED = ["s5_lam_re", "s5_lam_im", "s5_log_dt", "s5_b_re", "s5_b_im", "s5_c_re", "s5_c_im",
               "ffn_norm", "b_norm_mix", "kv_norm", "final_norm"]
_CHIP_VECTORS = ["s5_d", "a_norm_mix"]
_BIG = ["s5_w_glu", "ffn_w_in", "ffn_w_out", "attn_w_q", "attn_w_o", "w_kv"]
_WEIGHT_ORDER = ["s5_lam_re", "s5_lam_im", "s5_log_dt", "s5_b_re", "s5_b_im", "s5_c_re", "s5_c_im", "s5_d",
                 "s5_w_glu", "a_norm_mix", "ffn_norm", "ffn_w_in", "ffn_w_out", "b_norm_mix", "attn_w_q",
                 "attn_w_o", "kv_norm", "w_kv", "final_norm"]


def _step(x, loss_target, w, m, v):
    s, d = x.shape[1], x.shape[2]
    chip = 2 * lax.axis_index("x") + lax.axis_index("y")

    col_sharded = dict(w_glu=("s5_w_glu", 0), w_in0=("ffn_w_in", 0), w_in1=("ffn_w_in", 1), w_q=("attn_w_q", 0))
    row_sharded = dict(w_out0=("ffn_w_out", 0), w_out1=("ffn_w_out", 1), w_o=("attn_w_o", 0), w_kv=("w_kv", 0))
    names = ["w_glu", "w_in0", "w_out0", "w_kv", "w_q", "w_o", "w_in1", "w_out1"]
    local = {**col_sharded, **row_sharded}

    def layers(a):
        return a.reshape((-1,) + a.shape[-2:])

    chip_arr = jnp.reshape(chip, (1,)).astype(jnp.int32)
    vector_lands = [lax.dynamic_update_slice(jnp.zeros((N_CHIPS,) + w[n].shape, F32), w[n][None], (chip, 0, 0))
                    for n in _CHIP_VECTORS]
    v_send, v_recv, v_land, v_token = _gather_start(vector_lands, "gather_start_vectors", False,
                                                    jnp.zeros((8, 128), F32))
    first = 2
    lands = [_cast_place(layers(w[local[n][0]]), local[n][1], chip_arr, f"cast_{n}") for n in names[:first]]
    send, recv, land_thru, first_token = _gather_start(lands, "gather_start_first", True, v_token)
    lands = [_cast_place(layers(w[local[n][0]]), local[n][1], chip_arr, f"cast_{n}") for n in names[first:]]
    *later, rest_token = _gather_start(lands, "gather_start_rest", True, first_token)
    send, recv, land_thru = [list(a) + list(b) for a, b in zip((send, recv, land_thru), later)]
    vectors = _gather_wait(v_land, v_send, v_recv, jnp.zeros((8, 128), F32), "gather_wait_vectors", False)
    s5_d_full = vectors[0].reshape(1, d)
    a_norm_full = vectors[1].reshape(1, d) + rest_token[0, 0]

    batches = dict(glu=["w_glu", "w_in0", "w_out0"], attention=["w_kv", "w_q", "w_o"], ffn1=["w_in1", "w_out1"])
    forwarded, arrived = {}, {}

    def stage(batch, after):
        idx = [names.index(n) for n in batches[batch]]
        got = _gather_wait([land_thru[i] for i in idx], [send[i] for i in idx], [recv[i] for i in idx], after,
                           f"gather_wait_{batch}", True)
        f_send, f_recv, f_land, token = _forward_start(got, f"forward_start_{batch}")
        forwarded.update(zip(batches[batch], zip(f_send, f_recv, f_land)))
        return token[0, 0]

    def weight(name, after):
        if name not in arrived:
            f_send, f_recv, f_land = forwarded[name]
            land = _forward_wait(f_land, f_send, f_recv, after, f"forward_wait_{name}")
            arrived[name] = land if name in col_sharded else land.reshape(1, -1, land.shape[-1])
        return arrived[name]

    started = {}

    def emit(name, dw):
        outs = _scatter_start(dw.reshape(N_CHIPS, -1, dw.shape[-1]), f"scatter_start_{name}")
        started[name] = outs[:4]
        return outs[4][0, 0]

    small = dict(lam_re=w["s5_lam_re"][0], lam_im=w["s5_lam_im"][0], log_dt=w["s5_log_dt"][0],
                 b_re=w["s5_b_re"][0], b_im=w["s5_b_im"][0], c_re=w["s5_c_re"][0], c_im=w["s5_c_im"][0],
                 s5_d=s5_d_full, a_norm=a_norm_full, ffn_norm0=w["ffn_norm"][0:1], ffn_norm1=w["ffn_norm"][1:2],
                 b_norm=w["b_norm_mix"], kv_norm=w["kv_norm"].reshape(1, d), final_norm=w["final_norm"].reshape(1, d))

    loss_rows, grad_x, d_small = _local_step(x[0], loss_target[0], small, weight, emit, stage)

    g4, got = _scatter_wait([started[n] for n in names], grad_x)
    partial = [_sum_blocks(o, r, chip_arr, f"sum_{n}") for n, o, r in zip(names, g4, got)]
    swap_send, swap_recv, partial, swap_land, swap_token = _swap_start(partial)
    result = {}

    rep_grads = [d_small["lam_re"], d_small["lam_im"], d_small["log_dt"], d_small["b_re"], d_small["b_im"],
                 d_small["c_re"], d_small["c_im"],
                 jnp.concatenate([d_small["ffn_norm0"], d_small["ffn_norm1"]], axis=0),
                 d_small["b_norm"], d_small["kv_norm"], d_small["final_norm"]]
    vec_grads = [d_small["s5_d"], d_small["a_norm"]]
    to_reduce = rep_grads + vec_grads + [jnp.sum(loss_rows).reshape(1) + swap_token[0, 0]]
    total = sum(math.prod(a.shape) for a in to_reduce)
    rows_per = -(-total // (N_DEV * 128 * 8)) * 8
    reduced = _all_reduce_small(_pack(to_reduce, N_DEV * rows_per).reshape(N_DEV, rows_per, 128))
    red = _unpack(reduced, [a.shape for a in to_reduce])
    loss = red[-1][0]
    g_small = dict(zip(_REPLICATED, [r.reshape(w[n].shape) for r, n in zip(red[:len(rep_grads)], _REPLICATED)]))
    for n, r in zip(_CHIP_VECTORS, red[len(rep_grads):-1]):
        g_small[n] = lax.dynamic_slice_in_dim(r.reshape(1, d), chip * (d // N_CHIPS), d // N_CHIPS, axis=1)
    small_names = _REPLICATED + _CHIP_VECTORS
    n_small = sum(math.prod(w[n].shape) for n in small_names)
    rows_small = -(-n_small // (128 * 8)) * 8
    packed = [_pack([src[n] for n in small_names], rows_small) for src in (g_small, w, m, v)]
    upd = _adamw_small(*packed, "adamw_small")
    shapes = [w[n].shape for n in small_names]
    for n, dl, mo, vo in zip(small_names, *[_unpack(u, shapes) for u in upd]):
        result[n] = [g_small[n], dl, mo, vo]

    partial, other = _swap_wait(partial, swap_land, swap_send, swap_recv, upd[0])
    part = dict(zip(names, zip(partial, other)))
    for name in _BIG:
        parts = [part[n] for n in sorted((n for n in names if local[n][0] == name), key=lambda n: local[n][1])]
        outs = _adamw(parts, layers(w[name]), layers(m[name]), layers(v[name]), f"adamw_{name}")
        result[name] = [o.reshape(w[name].shape) for o in outs]

    cols = [[result[n][t] for n in _WEIGHT_ORDER] for t in range(4)]
    return (loss, grad_x.reshape(x.shape), *cols[0], *cols[1], *cols[2], *cols[3])


def kernel(x, s5_lam_re, s5_lam_im, s5_log_dt, s5_b_re, s5_b_im, s5_c_re, s5_c_im, s5_d, s5_w_glu, a_norm_mix, ffn_norm, ffn_w_in, ffn_w_out, b_norm_mix, attn_w_q, attn_w_o, kv_norm, w_kv, final_norm, loss_target, m_s5_lam_re, m_s5_lam_im, m_s5_log_dt, m_s5_b_re, m_s5_b_im, m_s5_c_re, m_s5_c_im, m_s5_d, m_s5_w_glu, m_a_norm_mix, m_ffn_norm, m_ffn_w_in, m_ffn_w_out, m_b_norm_mix, m_attn_w_q, m_attn_w_o, m_kv_norm, m_w_kv, m_final_norm, v_s5_lam_re, v_s5_lam_im, v_s5_log_dt, v_s5_b_re, v_s5_b_im, v_s5_c_re, v_s5_c_im, v_s5_d, v_s5_w_glu, v_a_norm_mix, v_ffn_norm, v_ffn_w_in, v_ffn_w_out, v_b_norm_mix, v_attn_w_q, v_attn_w_o, v_kv_norm, v_w_kv, v_final_norm):
    w = dict(zip(_WEIGHT_ORDER, (s5_lam_re, s5_lam_im, s5_log_dt, s5_b_re, s5_b_im, s5_c_re, s5_c_im, s5_d, s5_w_glu, a_norm_mix, ffn_norm, ffn_w_in, ffn_w_out, b_norm_mix, attn_w_q, attn_w_o, kv_norm, w_kv, final_norm)))
    m = dict(zip(_WEIGHT_ORDER, (m_s5_lam_re, m_s5_lam_im, m_s5_log_dt, m_s5_b_re, m_s5_b_im, m_s5_c_re, m_s5_c_im, m_s5_d, m_s5_w_glu, m_a_norm_mix, m_ffn_norm, m_ffn_w_in, m_ffn_w_out, m_b_norm_mix, m_attn_w_q, m_attn_w_o, m_kv_norm, m_w_kv, m_final_norm)))
    v = dict(zip(_WEIGHT_ORDER, (v_s5_lam_re, v_s5_lam_im, v_s5_log_dt, v_s5_b_re, v_s5_b_im, v_s5_c_re, v_s5_c_im, v_s5_d, v_s5_w_glu, v_a_norm_mix, v_ffn_norm, v_ffn_w_in, v_ffn_w_out, v_b_norm_mix, v_attn_w_q, v_attn_w_o, v_kv_norm, v_w_kv, v_final_norm)))
    return _step(x, loss_target, w, m, v)
```

```python
import functools
import math

import jax
import jax.numpy as jnp
from jax import lax
from jax.experimental import pallas as pl
from jax.experimental.pallas import tpu as pltpu

F32 = jnp.float32
BF16 = jnp.bfloat16

S5_GROUP_CH = 16
S5_STATE = 64
GROUPS_PER_TILE = 8
HEAD_DIM = 128
N_KV_HEADS = 4
PATTERNS = ((128, 1), (512, 4), (2048, 16))
ATT_BLK = 128
EPS = 1e-6
NEG_INF = -1e30
SCAN_T = 1024
ADAM_LR = 0.001
ADAM_B1 = 0.9
ADAM_B2 = 0.999
ADAM_EPS = 1e-08
ADAM_WD = 0.01
ADAM_STEP = 10
N_CHIPS = 4
N_DEV = 8
VMEM_LIMIT = 56 * 1024 * 1024
MESH = pl.DeviceIdType.MESH
GELU_K = math.sqrt(2.0 / math.pi)
GELU_C = 0.044715


def _tile(n, pref, unit=128):
    if n <= pref:
        return n
    best = None
    t = unit
    while t <= pref:
        if n % t == 0:
            best = t
        t += unit
    assert best is not None, (n, pref, unit)
    return best


def _params(sem):
    return pltpu.CompilerParams(dimension_semantics=sem, vmem_limit_bytes=VMEM_LIMIT)


def _dot(a, b):
    return jnp.dot(a, b, preferred_element_type=F32)


def _dot_nt(a, b):
    return lax.dot_general(a, b, (((1,), (1,)), ((), ())), preferred_element_type=F32)


def _dot_tn(a, b):
    return lax.dot_general(a, b, (((0,), (0,)), ((), ())), preferred_element_type=F32)


def _mm_nn(a, w, out_dtype, name, tm=512, tn=1536, tk=2048):
    m, k = a.shape
    nb, k2, nq = w.shape
    assert k == k2
    tm, tn, tk = _tile(m, tm, 8), _tile(nq, tn), _tile(k, tk)
    per, nk = nq // tn, k // tk

    def kern(a_ref, w_ref, o_ref, *acc):
        p = _dot(a_ref[...], w_ref[...])
        if nk == 1:
            o_ref[...] = p.astype(o_ref.dtype)
        else:
            acc_ref, = acc
            kk = pl.program_id(2)

            @pl.when(kk == 0)
            def _():
                acc_ref[...] = p

            @pl.when(kk > 0)
            def _():
                acc_ref[...] += p

            @pl.when(kk == nk - 1)
            def _():
                o_ref[...] = acc_ref[...].astype(o_ref.dtype)

    return pl.pallas_call(
        kern, name=name, grid=(nb * per, m // tm, nk),
        in_specs=[pl.BlockSpec((tm, tk), lambda j, i, kk: (i, kk)),
                  pl.BlockSpec((None, tk, tn), lambda j, i, kk: (j // per, kk, j % per))],
        out_specs=pl.BlockSpec((tm, tn), lambda j, i, kk: (i, j)),
        out_shape=jax.ShapeDtypeStruct((m, nb * nq), out_dtype),
        scratch_shapes=[] if nk == 1 else [pltpu.VMEM((tm, tn), F32)],
        compiler_params=_params(("parallel", "parallel", "arbitrary")),
    )(a, w)


def _paired_block(r, per, nb):
    j = r // 2
    return (r % 2) * (nb // 2) + j // per, j % per


def _mm_nt(a, w, name, tm=512, tr=1536, tkc=1024, paired=False):
    m, n = a.shape
    nb, k, nq = w.shape
    assert n == nb * nq
    tm, tr, tkc = _tile(m, tm, 8), _tile(nq, tr), _tile(k, tkc)
    per = nq // tr

    def kern(a_ref, w_ref, o_ref):
        acc = None
        for r in range(nb * per):
            blk, tile = _paired_block(r, per, nb) if paired else (r // per, r % per)
            p = _dot_nt(a_ref[:, r * tr:(r + 1) * tr], w_ref[blk, :, tile * tr:(tile + 1) * tr])
            acc = p if acc is None else acc + p
        o_ref[...] = acc

    return pl.pallas_call(
        kern, name=name, grid=(k // tkc, m // tm),
        in_specs=[pl.BlockSpec((tm, n), lambda kc, i: (i, 0)),
                  pl.BlockSpec((nb, tkc, nq), lambda kc, i: (0, kc, 0))],
        out_specs=pl.BlockSpec((tm, tkc), lambda kc, i: (i, kc)),
        out_shape=jax.ShapeDtypeStruct((m, k), F32),
        compiler_params=_params(("parallel", "parallel")),
    )(a, w)


def _mm_tn(a, dy, nb, name, tkk=512, tn=1536, paired=False):
    s, k = a.shape
    s2, n = dy.shape
    assert s == s2 and n % nb == 0
    nq = n // nb
    tkk, tn = _tile(k, tkk), _tile(nq, tn)
    per = nq // tn

    def w_block(j):
        return _paired_block(j, per, nb) if paired else (j // per, j % per)

    def kern(a_ref, dy_ref, o_ref):
        o_ref[...] = _dot_tn(a_ref[...], dy_ref[...]).astype(o_ref.dtype)

    return pl.pallas_call(
        kern, name=name, grid=(nb * per, k // tkk),
        in_specs=[pl.BlockSpec((s, tkk), lambda j, kk: (0, kk)),
                  pl.BlockSpec((s, tn), lambda j, kk: (0, j))],
        out_specs=pl.BlockSpec((None, tkk, tn), lambda j, kk: (w_block(j)[0], kk, w_block(j)[1])),
        out_shape=jax.ShapeDtypeStruct((nb, k, nq), BF16),
        compiler_params=_params(("parallel", "parallel")),
    )(a, dy)


ROWS = 256


def _rms(x, g):
    r = lax.rsqrt(jnp.mean(x * x, axis=-1, keepdims=True) + EPS)
    return x * r * g


def _rms_bwd(x, g, dh):
    r = lax.rsqrt(jnp.mean(x * x, axis=-1, keepdims=True) + EPS)
    xh = x * r
    dgx = dh * g
    dx = r * (dgx - xh * jnp.mean(dgx * xh, axis=-1, keepdims=True))
    return dx, dh * xh


def _sigmoid(x):
    return 1.0 / (1.0 + jnp.exp(-x))


def _gelu(y):
    return 0.5 * y * (1.0 + jnp.tanh(GELU_K * (y + GELU_C * y * y * y)))


def _gelu_grad(y):
    t = jnp.tanh(GELU_K * (y + GELU_C * y * y * y))
    return 0.5 * (1.0 + t) + 0.5 * y * (1.0 - t * t) * GELU_K * (1.0 + 3.0 * GELU_C * y * y)


def _row_spec(tm, d, col=0):
    return pl.BlockSpec((tm, d), lambda i: (i, col))


def _vec_spec(d):
    return pl.BlockSpec((1, d), lambda i: (0, 0))


def _acc_rows(ref, val, i):
    s = jnp.sum(val, axis=0, keepdims=True)

    @pl.when(i == 0)
    def _():
        ref[...] = s

    @pl.when(i > 0)
    def _():
        ref[...] += s


def _norm_f32(x, g, name):
    s, d = x.shape
    tm = _tile(s, ROWS, 8)

    def kern(x_ref, g_ref, h_ref):
        h_ref[...] = _rms(x_ref[...], g_ref[...])

    return pl.pallas_call(
        kern, name=name, grid=(s // tm,),
        in_specs=[_row_spec(tm, d), _vec_spec(d)],
        out_specs=_row_spec(tm, d),
        out_shape=jax.ShapeDtypeStruct((s, d), F32),
        compiler_params=_params(("parallel",)),
    )(x, g)


def _glu_res_norm(vg, x, g, name):
    s, d = x.shape
    tm = _tile(s, ROWS, 16)

    def kern(val_ref, gate_ref, x_ref, g_ref, x1_ref, hf_ref):
        x1 = x_ref[...] + val_ref[...] * _sigmoid(gate_ref[...])
        x1_ref[...] = x1
        hf_ref[...] = _rms(x1, g_ref[...]).astype(BF16)

    return pl.pallas_call(
        kern, name=name, grid=(s // tm,),
        in_specs=[_row_spec(tm, d, 0), _row_spec(tm, d, 1), _row_spec(tm, d), _vec_spec(d)],
        out_specs=[_row_spec(tm, d), _row_spec(tm, d)],
        out_shape=[jax.ShapeDtypeStruct((s, d), F32), jax.ShapeDtypeStruct((s, d), BF16)],
        compiler_params=_params(("parallel",)),
    )(vg, vg, x, g)


FFN_TILE = 1408


def _ffn_in_act(hf, w_in, name, tm=512):
    s, k = hf.shape
    nb, _, nq = w_in.shape
    tm, tn = _tile(s, tm, 16), _tile(nq, FFN_TILE)
    per = nq // tn
    nf = (nb // 2) * per

    def kern(h_ref, wg_ref, wu_ref, a_ref, gu_ref):
        h = h_ref[...]
        g = _dot(h, wg_ref[...])
        u = _dot(h, wu_ref[...])
        sg = _sigmoid(g)
        silu = g * sg
        a_ref[...] = (silu * u).astype(BF16)
        gu_ref[:, :tn] = (u * (sg + silu * (1.0 - sg))).astype(BF16)
        gu_ref[:, tn:] = silu.astype(BF16)

    return pl.pallas_call(
        kern, name=name, grid=(nf, s // tm),
        in_specs=[pl.BlockSpec((tm, k), lambda j, i: (i, 0)),
                  pl.BlockSpec((None, k, tn), lambda j, i: (j // per, 0, j % per)),
                  pl.BlockSpec((None, k, tn), lambda j, i: (nb // 2 + j // per, 0, j % per))],
        out_specs=[pl.BlockSpec((tm, tn), lambda j, i: (i, j)),
                   pl.BlockSpec((tm, 2 * tn), lambda j, i: (i, j))],
        out_shape=[jax.ShapeDtypeStruct((s, nf * tn), BF16), jax.ShapeDtypeStruct((s, 2 * nf * tn), BF16)],
        compiler_params=_params(("parallel", "parallel")),
    )(hf, w_in, w_in)


def _ffn_out_dx_act(dresb, w_out, gu, name, tm=512):
    s, d = dresb.shape
    f = w_out.shape[1]
    tm = _tile(s, tm, 16)
    tn = _tile(f // 2, FFN_TILE)

    def kern(d_ref, w_ref, gu_ref, o_ref):
        da = _dot_nt(d_ref[...], w_ref[...])
        o_ref[:, :tn] = (da * gu_ref[:, :tn].astype(F32)).astype(BF16)
        o_ref[:, tn:] = (da * gu_ref[:, tn:].astype(F32)).astype(BF16)

    pair = pl.BlockSpec((tm, 2 * tn), lambda j, i: (i, j))
    return pl.pallas_call(
        kern, name=name, grid=(f // tn, s // tm),
        in_specs=[pl.BlockSpec((tm, d), lambda j, i: (i, 0)),
                  pl.BlockSpec((None, tn, d), lambda j, i: (0, j, 0)), pair],
        out_specs=pair,
        out_shape=jax.ShapeDtypeStruct((s, 2 * f), BF16),
        compiler_params=_params(("parallel", "parallel")),
    )(dresb, w_out, gu)


def _mm_res_norm(a, w, x, g, name, tm=512):
    m, k = a.shape
    d = w.shape[2]
    tm = _tile(m, tm, 16)

    def kern(a_ref, w_ref, x_ref, g_ref, xn_ref, hf_ref):
        xn = x_ref[...] + _dot(a_ref[...], w_ref[0])
        xn_ref[...] = xn
        hf_ref[...] = _rms(xn, g_ref[...]).astype(BF16)

    return pl.pallas_call(
        kern, name=name, grid=(m // tm,),
        in_specs=[_row_spec(tm, k), pl.BlockSpec((1, k, d), lambda i: (0, 0, 0)), _row_spec(tm, d), _vec_spec(d)],
        out_specs=[_row_spec(tm, d), _row_spec(tm, d)],
        out_shape=[jax.ShapeDtypeStruct((m, d), F32), jax.ShapeDtypeStruct((m, d), BF16)],
        compiler_params=_params(("parallel",)),
    )(a, w, x, g)


def _loss_head(o, x, g, target, name):
    s, d = x.shape
    tm = _tile(s, ROWS, 16)

    def kern(o_ref, x_ref, g_ref, t_ref, dx_ref, dxb_ref, loss_ref, dg_ref):
        i = pl.program_id(0)
        x4 = x_ref[...] + o_ref[...]
        gg = g_ref[...]
        diff = _rms(x4, gg) - t_ref[...]
        dx, dgr = _rms_bwd(x4, gg, diff * (1.0 / d))
        dx_ref[...] = dx
        dxb_ref[...] = dx.astype(BF16)
        _acc_rows(loss_ref, diff * diff * (0.5 / d), i)
        _acc_rows(dg_ref, dgr, i)

    return pl.pallas_call(
        kern, name=name, grid=(s // tm,),
        in_specs=[_row_spec(tm, d), _row_spec(tm, d), _vec_spec(d), _row_spec(tm, d)],
        out_specs=[_row_spec(tm, d), _row_spec(tm, d), _vec_spec(d), _vec_spec(d)],
        out_shape=[jax.ShapeDtypeStruct((s, d), F32), jax.ShapeDtypeStruct((s, d), BF16),
                   jax.ShapeDtypeStruct((1, d), F32), jax.ShapeDtypeStruct((1, d), F32)],
        compiler_params=_params(("arbitrary",)),
    )(o, x, g, target)


def _norm_bwd(x, gains, dhs, dres, name):
    s, d = x.shape
    tm = _tile(s, ROWS, 16)
    ng = len(gains)

    def kern(x_ref, dres_ref, *refs):
        i = pl.program_id(0)
        x_ = x_ref[...]
        acc = dres_ref[...]
        for t in range(ng):
            dx, dgr = _rms_bwd(x_, refs[t][...], refs[ng + t][...])
            acc = acc + dx
            _acc_rows(refs[2 * ng + 2 + t], dgr, i)
        refs[2 * ng][...] = acc
        refs[2 * ng + 1][...] = acc.astype(BF16)

    return pl.pallas_call(
        kern, name=name, grid=(s // tm,),
        in_specs=[_row_spec(tm, d), _row_spec(tm, d)] + [_vec_spec(d)] * ng + [_row_spec(tm, d)] * ng,
        out_specs=[_row_spec(tm, d), _row_spec(tm, d)] + [_vec_spec(d)] * ng,
        out_shape=[jax.ShapeDtypeStruct((s, d), F32), jax.ShapeDtypeStruct((s, d), BF16)]
        + [jax.ShapeDtypeStruct((1, d), F32)] * ng,
        compiler_params=_params(("arbitrary",)),
    )(x, dres, *gains, *dhs)


def _glu_bwd(dmix, vg, name):
    s, d = dmix.shape
    tm = _tile(s, ROWS, 16)

    def kern(dm_ref, val_ref, gate_ref, o_ref):
        dm = dm_ref[...]
        sg = _sigmoid(gate_ref[...])
        o_ref[:, :d] = (dm * sg).astype(BF16)
        o_ref[:, d:] = (dm * val_ref[...] * sg * (1.0 - sg)).astype(BF16)

    return pl.pallas_call(
        kern, name=name, grid=(s // tm,),
        in_specs=[_row_spec(tm, d), _row_spec(tm, d, 0), _row_spec(tm, d, 1)],
        out_specs=_row_spec(tm, 2 * d),
        out_shape=jax.ShapeDtypeStruct((s, 2 * d), BF16),
        compiler_params=_params(("parallel",)),
    )(dmix, vg, vg)


SUB = 8


def _local_scan(vr, vi, pre_ref, pim_ref, reverse):
    sub = lax.broadcasted_iota(jnp.int32, vr.shape, 1)
    sign = -1.0 if reverse else 1.0
    for sh in (1, 2, 4):
        ar = pre_ref[sh - 1:sh, :][None]
        ai = sign * pim_ref[sh - 1:sh, :][None]
        keep = sub < SUB - sh if reverse else sub >= sh
        sr = jnp.where(keep, pltpu.roll(vr, SUB - sh if reverse else sh, 1), 0.0)
        si = jnp.where(keep, pltpu.roll(vi, SUB - sh if reverse else sh, 1), 0.0)
        vr, vi = vr + ar * sr - ai * si, vi + ar * si + ai * sr
    return vr, vi


def _s5_fwd(u, bbd_re, bbd_im, cbd_re, cbd_im, pw_re, pw_im, dskip, name):
    s, d = u.shape
    nt, cw, lw = bbd_re.shape
    t = _tile(s, SCAN_T, SUB)
    nc = s // t
    ng = t // SUB

    def kern(u_ref, bre_ref, bim_ref, cre_ref, cim_ref, pre_ref, pim_ref, d_ref,
             xr_ref, xi_ref, y_ref, z_ref, car_ref, cai_ref):
        c = pl.program_id(1)

        @pl.when(c == 0)
        def _():
            car_ref[...] = jnp.zeros_like(car_ref)
            cai_ref[...] = jnp.zeros_like(cai_ref)

        u_ = u_ref[...]
        ub = u_.astype(BF16)
        vr, vi = _local_scan(_dot(ub, bre_ref[...]).reshape(ng, SUB, lw),
                             _dot(ub, bim_ref[...]).reshape(ng, SUB, lw), pre_ref, pim_ref, False)
        cr = car_ref[...]
        ci = cai_ref[...]
        pr = pre_ref[...]
        pi = pim_ref[...]
        for gidx in range(ng):
            rows = slice(gidx * SUB, (gidx + 1) * SUB)
            gr = vr[gidx] + pr * cr - pi * ci
            gi = vi[gidx] + pr * ci + pi * cr
            xr_ref[rows, :] = gr
            xi_ref[rows, :] = gi
            cr, ci = gr[SUB - 1:SUB, :], gi[SUB - 1:SUB, :]
        car_ref[...] = cr
        cai_ref[...] = ci
        y = (_dot(xr_ref[...].astype(BF16), cre_ref[...]) - _dot(xi_ref[...].astype(BF16), cim_ref[...])
             + d_ref[...] * u_)
        y_ref[...] = y
        z_ref[...] = _gelu(y).astype(BF16)

    tok = pl.BlockSpec((t, cw), lambda j, c: (c, j))
    st = pl.BlockSpec((t, lw), lambda j, c: (c, j))
    return pl.pallas_call(
        kern, name=name, grid=(nt, nc),
        in_specs=[tok,
                  pl.BlockSpec((None, cw, lw), lambda j, c: (j, 0, 0)),
                  pl.BlockSpec((None, cw, lw), lambda j, c: (j, 0, 0)),
                  pl.BlockSpec((None, lw, cw), lambda j, c: (j, 0, 0)),
                  pl.BlockSpec((None, lw, cw), lambda j, c: (j, 0, 0)),
                  pl.BlockSpec((SUB, lw), lambda j, c: (0, j)),
                  pl.BlockSpec((SUB, lw), lambda j, c: (0, j)),
                  pl.BlockSpec((1, cw), lambda j, c: (0, j))],
        out_specs=[st, st, tok, tok],
        out_shape=[jax.ShapeDtypeStruct((s, nt * lw), F32), jax.ShapeDtypeStruct((s, nt * lw), F32),
                   jax.ShapeDtypeStruct((s, d), F32), jax.ShapeDtypeStruct((s, d), BF16)],
        scratch_shapes=[pltpu.VMEM((1, lw), F32), pltpu.VMEM((1, lw), F32)],
        compiler_params=_params(("parallel", "arbitrary")),
    )(u, bbd_re, bbd_im, cbd_re, cbd_im, pw_re, pw_im, dskip)


def _s5_bwd(dz, y, u, xs_re, xs_im, bbd_re, bbd_im, cbd_re, cbd_im, pw_re, pw_im, pf_re, pf_im, dskip, name):
    s, d = u.shape
    nt, cw, lw = bbd_re.shape
    t = _tile(s, SCAN_T, SUB)
    nc = s // t
    ng = t // SUB

    def kern(dz_ref, y_ref, u_ref, xr_ref, xi_ref, bre_ref, bim_ref, cre_ref, cim_ref,
             pre_ref, pim_ref, fre_ref, fim_ref, d_ref,
             du_ref, dd_ref, dcr_ref, dci_ref, dbr_ref, dbi_ref, dar_ref, dai_ref,
             car_ref, cai_ref, gr_ref, gi_ref):
        c = pl.program_id(1)

        @pl.when(c == 0)
        def _():
            car_ref[...] = jnp.zeros_like(car_ref)
            cai_ref[...] = jnp.zeros_like(cai_ref)

        u_ = u_ref[...]
        ub = u_.astype(BF16)
        dy = dz_ref[...] * _gelu_grad(y_ref[...])
        dyb = dy.astype(BF16)
        vr, vi = _local_scan(_dot_nt(dyb, cre_ref[...]).reshape(ng, SUB, lw),
                             (-_dot_nt(dyb, cim_ref[...])).reshape(ng, SUB, lw), pre_ref, pim_ref, True)
        later_r = car_ref[...]
        later_i = cai_ref[...]
        cr, ci = later_r, later_i
        fr = fre_ref[...]
        fi = fim_ref[...]
        for gidx in reversed(range(ng)):
            rows = slice(gidx * SUB, (gidx + 1) * SUB)
            ar = vr[gidx] + fr * cr + fi * ci
            ai = vi[gidx] + fr * ci - fi * cr
            gr_ref[rows, :] = ar
            gi_ref[rows, :] = ai
            cr, ci = ar[0:1, :], ai[0:1, :]
        car_ref[...] = cr
        cai_ref[...] = ci
        gr = gr_ref[...]
        gi = gi_ref[...]
        rows = lax.broadcasted_iota(jnp.int32, (t, lw), 0)
        gsr = jnp.where(rows < t - 1, pltpu.roll(gr, t - 1, 0), later_r)
        gsi = jnp.where(rows < t - 1, pltpu.roll(gi, t - 1, 0), later_i)
        xr = xr_ref[...]
        xi = xi_ref[...]
        dar = jnp.sum(gsr * xr + gsi * xi, axis=0, keepdims=True)
        dai = jnp.sum(gsi * xr - gsr * xi, axis=0, keepdims=True)
        grb = gr.astype(BF16)
        gib = gi.astype(BF16)
        dbr = _dot_tn(ub, grb)
        dbi = _dot_tn(ub, gib)
        dcr = _dot_tn(dyb, xr.astype(BF16))
        dci = _dot_tn(dyb, xi.astype(BF16))
        du_ref[...] = dy * d_ref[...] + _dot_nt(grb, bre_ref[...]) + _dot_nt(gib, bim_ref[...])
        ddv = jnp.sum(dy * u_, axis=0, keepdims=True)

        @pl.when(c == 0)
        def _():
            dd_ref[...] = ddv
            dcr_ref[...] = dcr
            dci_ref[...] = dci
            dbr_ref[...] = dbr
            dbi_ref[...] = dbi
            dar_ref[...] = dar
            dai_ref[...] = dai

        @pl.when(c > 0)
        def _():
            dd_ref[...] += ddv
            dcr_ref[...] += dcr
            dci_ref[...] += dci
            dbr_ref[...] += dbr
            dbi_ref[...] += dbi
            dar_ref[...] += dar
            dai_ref[...] += dai

    tok = pl.BlockSpec((t, cw), lambda j, c: (nc - 1 - c, j))
    st = pl.BlockSpec((t, lw), lambda j, c: (nc - 1 - c, j))
    wb = pl.BlockSpec((None, cw, lw), lambda j, c: (j, 0, 0))
    wc = pl.BlockSpec((None, lw, cw), lambda j, c: (j, 0, 0))
    pw = pl.BlockSpec((SUB, lw), lambda j, c: (0, j))
    vec_c = pl.BlockSpec((1, cw), lambda j, c: (0, j))
    vec_l = pl.BlockSpec((1, lw), lambda j, c: (0, j))
    return pl.pallas_call(
        kern, name=name, grid=(nt, nc),
        in_specs=[tok, tok, tok, st, st, wb, wb, wc, wc, pw, pw, pw, pw, vec_c],
        out_specs=[tok, vec_c, wb, wb, wb, wb, vec_l, vec_l],
        out_shape=[jax.ShapeDtypeStruct((s, d), F32), jax.ShapeDtypeStruct((1, d), F32)]
        + [jax.ShapeDtypeStruct((nt, cw, lw), F32)] * 4
        + [jax.ShapeDtypeStruct((1, nt * lw), F32)] * 2,
        scratch_shapes=[pltpu.VMEM((1, lw), F32), pltpu.VMEM((1, lw), F32),
                        pltpu.VMEM((t, lw), F32), pltpu.VMEM((t, lw), F32)],
        compiler_params=_params(("parallel", "arbitrary")),
    )(dz, y, u, xs_re, xs_im, bbd_re, bbd_im, cbd_re, cbd_im, pw_re, pw_im, pf_re, pf_im, dskip)


def _s5_discretize(lam_re, lam_im, log_dt, b_re, b_im):
    dt = jnp.exp(log_dt)[:, None]
    mag = jnp.exp(lam_re * dt)
    ang = lam_im * dt
    lb_re = mag * jnp.cos(ang)
    lb_im = mag * jnp.sin(ang)
    nr = lb_re - 1.0
    den = lam_re * lam_re + lam_im * lam_im
    f_re = (nr * lam_re + lb_im * lam_im) / den
    f_im = (lb_im * lam_re - nr * lam_im) / den
    bb_re = f_re[..., None] * b_re - f_im[..., None] * b_im
    bb_im = f_re[..., None] * b_im + f_im[..., None] * b_re
    return lb_re, lb_im, bb_re, bb_im


def _block_diag(w):
    nt, ng, a, b = w.shape
    eye = jnp.eye(ng, dtype=w.dtype)
    return (w[:, :, :, None, :] * eye[None, :, None, :, None]).reshape(nt, ng * a, ng * b)


def _block_diag_take(w, a, b):
    nt = w.shape[0]
    ng = GROUPS_PER_TILE
    w5 = w.reshape(nt, ng, a, ng, b)
    on_diagonal = jnp.eye(ng, dtype=bool)[None, :, None, :, None]
    return jnp.sum(jnp.where(on_diagonal, w5, 0.0), axis=3).reshape(nt * ng, a, b)


def _att_combine(outs, lses, name):
    s, d = outs[0].shape
    tm = _tile(s, ROWS, 16)
    ng = len(outs)

    def kern(*refs):
        for h in range(d // HEAD_DIM):
            cols = slice(h * HEAD_DIM, (h + 1) * HEAD_DIM)
            ls = [refs[ng + t][:, h:h + 1] for t in range(ng)]
            mx = functools.reduce(jnp.maximum, ls)
            es = [jnp.exp(l - mx) for l in ls]
            den = functools.reduce(lambda a, b: a + b, es)
            o = functools.reduce(lambda a, b: a + b,
                                 [es[t] / den * refs[t][:, cols].astype(F32) for t in range(ng)])
            refs[2 * ng][:, cols] = o
            refs[2 * ng + 1][:, cols] = o.astype(BF16)

    return pl.pallas_call(
        kern, name=name, grid=(s // tm,),
        in_specs=[_row_spec(tm, d)] * ng + [_row_spec(tm, d // HEAD_DIM)] * ng,
        out_specs=[_row_spec(tm, d)] * 2,
        out_shape=[jax.ShapeDtypeStruct((s, d), F32), jax.ShapeDtypeStruct((s, d), BF16)],
        compiler_params=_params(("parallel",)),
    )(*outs, *lses)


DIL_MAX = 16


def _slab(r):
    return 4 * (r % 4) + r // 4


def _to_slabs(xs, name):
    s, w = xs[0].shape
    n = s // DIL_MAX
    nx = len(xs)

    def kern(*refs):
        o_ref = refs[nx]
        for r in range(DIL_MAX):
            rows = [x_ref[pl.ds(r, n, stride=DIL_MAX), :] for x_ref in refs[:nx]]
            o_ref[_slab(r) * n:(_slab(r) + 1) * n, :] = functools.reduce(lambda a, b: a + b, rows)

    spec = pl.BlockSpec((s, 128), lambda i: (0, i))
    return pl.pallas_call(
        kern, name=name, grid=(w // 128,), in_specs=[spec] * nx, out_specs=spec,
        out_shape=jax.ShapeDtypeStruct((s, w), F32), compiler_params=_params(("parallel",)),
    )(*xs)


def _from_slabs(x, name):
    s, w = x.shape
    n = s // DIL_MAX

    def kern(x_ref, o_ref, ob_ref):
        for r in range(DIL_MAX):
            o_ref[pl.ds(r, n, stride=DIL_MAX), :] = x_ref[_slab(r) * n:(_slab(r) + 1) * n, :]
        ob_ref[...] = o_ref[...].astype(BF16)

    spec = pl.BlockSpec((s, 128), lambda i: (0, i))
    return pl.pallas_call(
        kern, name=name, grid=(w // 128,), in_specs=[spec], out_specs=[spec, spec],
        out_shape=[jax.ShapeDtypeStruct((s, w), F32), jax.ShapeDtypeStruct((s, w), BF16)],
        compiler_params=_params(("parallel",)),
    )(x)


def _norm_bf16(x, gains, name):
    s, d = x.shape
    tm = _tile(s, ROWS, 16)
    ng = len(gains)

    def kern(x_ref, *refs):
        x_ = x_ref[...]
        for t in range(ng):
            refs[ng + t][...] = _rms(x_, refs[t][...]).astype(BF16)

    return pl.pallas_call(
        kern, name=name, grid=(s // tm,),
        in_specs=[_row_spec(tm, d)] + [_vec_spec(d)] * ng, out_specs=[_row_spec(tm, d)] * ng,
        out_shape=[jax.ShapeDtypeStruct((s, d), BF16)] * ng,
        compiler_params=_params(("parallel",)),
    )(x, *gains)


def _att_geometry(dil, s):
    g = DIL_MAX // dil
    return s // DIL_MAX, g, max(ATT_BLK // g, 16)


def _att_mask(g, j, jb, rep):
    b = g * j

    def pos(i):
        sl, jj = i // j, i % j
        off = {1: 0, 4: sl, 16: sl // 4 + 4 * (sl % 4)}[g]
        return g * jj + off

    qi = lax.broadcasted_iota(jnp.int32, (rep * b, 2 * b), 0) % b
    ki = lax.broadcasted_iota(jnp.int32, (rep * b, 2 * b), 1)
    prev = ki < b
    dist = pos(qi) - pos(ki % b) + jnp.where(prev, b, 0)
    return (dist >= 0) & (dist <= ATT_BLK) & (jnp.logical_not(prev) | (jb > 0))


def _stack_heads(x, rep):
    return jnp.concatenate([x[:, h * HEAD_DIM:(h + 1) * HEAD_DIM] for h in range(rep)], axis=0)


def _att_fwd(q, kv, gi, dil, nh, name):
    s = q.shape[0]
    n, g, j = _att_geometry(dil, s)
    b = g * j
    ng = q.shape[1] // (nh * HEAD_DIM)
    rep = nh // N_KV_HEADS
    qw = rep * HEAD_DIM
    scale = HEAD_DIM ** -0.5
    q3 = q.reshape(DIL_MAX, n, q.shape[1])
    kv3 = kv.reshape(DIL_MAX, n, kv.shape[1])

    kw = N_KV_HEADS * HEAD_DIM

    def kern(q_ref, kc_ref, kp_ref, vc_ref, vp_ref, o_ref, l_ref, lw_ref):
        jb = pl.program_id(1)
        mask = _att_mask(g, j, jb, rep)
        for kh in range(N_KV_HEADS):
            hs = slice(kh * HEAD_DIM, (kh + 1) * HEAD_DIM)
            kc, kp, vc, vp = [r[:, :, hs].reshape(b, HEAD_DIM) for r in (kc_ref, kp_ref, vc_ref, vp_ref)]
            k2 = jnp.concatenate([kp, kc], axis=0)
            v2 = jnp.concatenate([vp, vc], axis=0)
            qs = _stack_heads(q_ref[:, :, kh * qw:(kh + 1) * qw].reshape(b, qw), rep)
            sc = jnp.where(mask, _dot_nt(qs, k2) * scale, NEG_INF)
            m = jnp.max(sc, axis=-1, keepdims=True)
            p = jnp.exp(sc - m)
            l = jnp.sum(p, axis=-1, keepdims=True)
            out = _dot((p / l).astype(BF16), v2)
            lse = m + jnp.log(l)
            wide = jnp.broadcast_to(lse, (rep * b, HEAD_DIM))
            for h in range(rep):
                cols = slice(kh * qw + h * HEAD_DIM, kh * qw + (h + 1) * HEAD_DIM)
                o_ref[:, :, cols] = out[h * b:(h + 1) * b].astype(BF16).reshape(g, j, HEAD_DIM)
                lw_ref[:, :, cols] = wide[h * b:(h + 1) * b].reshape(g, j, HEAD_DIM)
                l_ref[:, :, kh * rep + h:kh * rep + h + 1] = lse[h * b:(h + 1) * b].reshape(g, j, 1)

    def kv_spec(col, prev):
        if prev:
            return pl.BlockSpec((g, j, kw), lambda c, jb: (c, jnp.maximum(jb - 1, 0), col))
        return pl.BlockSpec((g, j, kw), lambda c, jb: (c, jb, col))

    out_spec = pl.BlockSpec((g, j, nh * HEAD_DIM), lambda c, jb: (c, jb, 0))
    out, lse, lse_wide = pl.pallas_call(
        kern, name=name, grid=(dil, n // j),
        in_specs=[pl.BlockSpec((g, j, nh * HEAD_DIM), lambda c, jb: (c, jb, gi)),
                  kv_spec(0, False), kv_spec(0, True), kv_spec(1, False), kv_spec(1, True)],
        out_specs=[out_spec, pl.BlockSpec((g, j, nh), lambda c, jb: (c, jb, 0)), out_spec],
        out_shape=[jax.ShapeDtypeStruct((DIL_MAX, n, nh * HEAD_DIM), BF16),
                   jax.ShapeDtypeStruct((DIL_MAX, n, nh), F32),
                   jax.ShapeDtypeStruct((DIL_MAX, n, nh * HEAD_DIM), F32)],
        compiler_params=_params(("parallel", "arbitrary")),
    )(q3, kv3, kv3, kv3, kv3)
    del ng
    return out.reshape(s, nh * HEAD_DIM), lse.reshape(s, nh), lse_wide.reshape(s, nh * HEAD_DIM)


def _att_bwd(q, kv, do, o, lses, dq_all, gi, dil, nh, name):
    s = q.shape[0]
    n, g, j = _att_geometry(dil, s)
    b = g * j
    nb = n // j
    ng = len(lses)
    rep = nh // N_KV_HEADS
    qw = rep * HEAD_DIM
    dm = nh * HEAD_DIM
    scale = HEAD_DIM ** -0.5
    q3 = q.reshape(DIL_MAX, n, q.shape[1])
    kv3 = kv.reshape(DIL_MAX, n, kv.shape[1])
    wide = [a.reshape(DIL_MAX, n, dm) for a in (do, o, *lses)]
    kw = N_KV_HEADS * HEAD_DIM

    def kern(q_ref, kc_ref, kp_ref, vc_ref, vp_ref, do_ref, o_ref, *refs):
        l_refs = refs[:ng]
        dq_ref, dk_ref, dv_ref, ck_ref, cv_ref = refs[ng + 1:]
        jb = pl.program_id(1)

        @pl.when(jb == 0)
        def _():
            ck_ref[...] = jnp.zeros_like(ck_ref)
            cv_ref[...] = jnp.zeros_like(cv_ref)

        @pl.when(jb < nb)
        def _():
            mask = _att_mask(g, j, jb, rep)
            for kh in range(N_KV_HEADS):
                hs = slice(kh * HEAD_DIM, (kh + 1) * HEAD_DIM)
                ws = slice(kh * qw, (kh + 1) * qw)

                def stacked(ref):
                    return _stack_heads(ref[:, :, ws].reshape(b, qw), rep)

                kc, kp, vc, vp = [r[:, :, hs].reshape(b, HEAD_DIM) for r in (kc_ref, kp_ref, vc_ref, vp_ref)]
                k2 = jnp.concatenate([kp, kc], axis=0)
                v2 = jnp.concatenate([vp, vc], axis=0)
                qs = stacked(q_ref)
                ls = [stacked(r)[:, 0:1] for r in l_refs]
                mx = functools.reduce(jnp.maximum, ls)
                den = functools.reduce(lambda a, c: a + c, [jnp.exp(l - mx) for l in ls])
                lse_g = ls[gi]
                w = jnp.exp(lse_g - mx) / den
                do_ = stacked(do_ref)
                ct = w * jnp.sum(do_ * stacked(o_ref), axis=-1, keepdims=True)
                dob = (w * do_).astype(BF16)
                p = jnp.exp(jnp.where(mask, _dot_nt(qs, k2) * scale, NEG_INF) - lse_g)
                ds = (p * (_dot_nt(dob, v2) - ct) * scale).astype(BF16)
                dq = (_dot(ds, k2)).astype(BF16)
                for h in range(rep):
                    cols = slice(kh * qw + h * HEAD_DIM, kh * qw + (h + 1) * HEAD_DIM)
                    dq_ref[:, :, cols] = dq[h * b:(h + 1) * b].reshape(g, j, HEAD_DIM)
                dk2 = _dot_tn(ds, qs)
                dv2 = _dot_tn(p.astype(BF16), dob)
                dk_ref[:, :, hs] = (ck_ref[:, hs] + dk2[:b]).reshape(g, j, HEAD_DIM)
                dv_ref[:, :, hs] = (cv_ref[:, hs] + dv2[:b]).reshape(g, j, HEAD_DIM)
                ck_ref[:, hs] = dk2[b:]
                cv_ref[:, hs] = dv2[b:]

        @pl.when(jb == nb)
        def _():
            dk_ref[...] = ck_ref[...].reshape(g, j, kw)
            dv_ref[...] = cv_ref[...].reshape(g, j, kw)

    def jq(jb):
        return jnp.minimum(jb, nb - 1)

    def kv_spec(col, prev):
        if prev:
            return pl.BlockSpec((g, j, kw), lambda c, jb: (c, jnp.maximum(jq(jb) - 1, 0), col))
        return pl.BlockSpec((g, j, kw), lambda c, jb: (c, jq(jb), col))

    wide_spec = pl.BlockSpec((g, j, dm), lambda c, jb: (c, jq(jb), 0))
    dkv_spec = pl.BlockSpec((g, j, kw), lambda c, jb: (c, jnp.maximum(jb - 1, 0), 0))
    dq, dk, dv = pl.pallas_call(
        kern, name=name, grid=(dil, nb + 1),
        in_specs=[pl.BlockSpec((g, j, dm), lambda c, jb: (c, jq(jb), gi)),
                  kv_spec(0, False), kv_spec(0, True), kv_spec(1, False), kv_spec(1, True)]
        + [wide_spec] * (2 + ng) + [_ANY],
        out_specs=[pl.BlockSpec((g, j, dm), lambda c, jb: (c, jq(jb), gi)), dkv_spec, dkv_spec],
        out_shape=[jax.ShapeDtypeStruct((DIL_MAX, n, dq_all.shape[1]), BF16),
                   jax.ShapeDtypeStruct((DIL_MAX, n, kw), F32),
                   jax.ShapeDtypeStruct((DIL_MAX, n, kw), F32)],
        scratch_shapes=[pltpu.VMEM((b, kw), F32), pltpu.VMEM((b, kw), F32)],
        input_output_aliases={7 + ng: 0},
        compiler_params=_params(("parallel", "arbitrary")),
    )(q3, kv3, kv3, kv3, kv3, *wide, dq_all.reshape(DIL_MAX, n, dq_all.shape[1]))
    return dq.reshape(s, -1), dk.reshape(s, -1), dv.reshape(s, -1)


def _dkv_sum(dks, dvs, name):
    s, w = dks[0].shape
    tm = _tile(s, ROWS, 16)
    ng = len(dks)

    def kern(*refs):
        o_ref = refs[2 * ng]
        o_ref[:, :w] = functools.reduce(lambda a, b: a + b, [refs[t][...] for t in range(ng)]).astype(BF16)
        o_ref[:, w:] = functools.reduce(lambda a, b: a + b, [refs[ng + t][...] for t in range(ng)]).astype(BF16)

    return pl.pallas_call(
        kern, name=name, grid=(s // tm,),
        in_specs=[_row_spec(tm, w)] * (2 * ng),
        out_specs=_row_spec(tm, 2 * w),
        out_shape=jax.ShapeDtypeStruct((s, 2 * w), BF16),
        compiler_params=_params(("parallel",)),
    )(*dks, *dvs)


def _ffn_fwd(hf, weight, layer, tag):
    a, gu = _ffn_in_act(hf, weight(f"w_in{layer}", hf), f"{tag}_in")
    o = _mm_nn(a, weight(f"w_out{layer}", a), F32, f"{tag}_out", tn=1024, tk=5632)
    return gu, a, o


def _ffn_bwd(dresb, hf, gu, a, weight, emit, layer, tag):
    w_in, w_out = weight(f"w_in{layer}", None), weight(f"w_out{layer}", None)
    zero = emit(f"w_out{layer}", _mm_tn(a, dresb, 1, f"{tag}_out_dw", tn=1024))
    dgu = _ffn_out_dx_act(dresb, w_out, gu, f"{tag}_out_dx")
    zero = zero + emit(f"w_in{layer}",
                       _mm_tn(hf, dgu, w_in.shape[0], f"{tag}_in_dw", tn=FFN_TILE, paired=True))
    return _mm_nt(dgu, w_in, f"{tag}_in_dx", tr=FFN_TILE, tkc=512, paired=True), zero


def _local_step(x, target, small, weight, emit, stage):
    s, d = x.shape
    nh = d // HEAD_DIM
    n_groups = d // S5_GROUP_CH
    nt = n_groups // GROUPS_PER_TILE
    p_, c_ = S5_STATE, S5_GROUP_CH

    disc_in = (small["lam_re"], small["lam_im"], small["log_dt"], small["b_re"], small["b_im"])
    (lb_re, lb_im, bb_re, bb_im), disc_vjp = jax.vjp(_s5_discretize, *disc_in)
    del lb_re, lb_im
    dt = jnp.exp(small["log_dt"])[:, None]
    def pole_powers(exponents):
        k = exponents[:, None, None]
        mag = jnp.exp(k * (small["lam_re"] * dt)[None])
        ang = k * (small["lam_im"] * dt)[None]
        return ((mag * jnp.cos(ang)).reshape(SUB, n_groups * p_),
                (mag * jnp.sin(ang)).reshape(SUB, n_groups * p_))

    pw_re, pw_im = pole_powers(jnp.arange(1, SUB + 1, dtype=F32))
    pf_re, pf_im = pole_powers(jnp.arange(SUB, 0, -1, dtype=F32))
    bbd_re = _block_diag(bb_re.transpose(0, 2, 1).reshape(nt, GROUPS_PER_TILE, c_, p_)).astype(BF16)
    bbd_im = _block_diag(bb_im.transpose(0, 2, 1).reshape(nt, GROUPS_PER_TILE, c_, p_)).astype(BF16)
    cbd_re = _block_diag(small["c_re"].transpose(0, 2, 1).reshape(nt, GROUPS_PER_TILE, p_, c_)).astype(BF16)
    cbd_im = _block_diag(small["c_im"].transpose(0, 2, 1).reshape(nt, GROUPS_PER_TILE, p_, c_)).astype(BF16)

    h0 = _norm_f32(x, small["a_norm"], "s5_norm")
    xs_re, xs_im, y0, z = _s5_fwd(h0, bbd_re, bbd_im, cbd_re, cbd_im, pw_re, pw_im, small["s5_d"], "s5_fwd")
    zero = stage("glu", z)
    vg = _mm_nn(z, weight("w_glu", z), F32, "glu_mm", tn=1024)
    x1, hf0 = _glu_res_norm(vg, x, small["ffn_norm0"] + zero, "glu_res_norm")
    gu0, a0, o0 = _ffn_fwd(hf0, weight, 0, "ffn0")
    zero = stage("attention", a0)
    x2 = _to_slabs([o0, x1], "to_slabs")
    kvn, h1 = _norm_bf16(x2, [small["kv_norm"] + zero, small["b_norm"]], "att_norms")
    kv = _mm_nn(kvn, weight("w_kv", kvn), BF16, "kv_mm", tn=1024)
    q = _mm_nn(h1, weight("w_q", kv), BF16, "q_mm", tn=1536)
    outs, lses_narrow, lses = [], [], []
    for gi, (window, dil) in enumerate(PATTERNS):
        assert window // dil == ATT_BLK
        og, lg, lg_wide = _att_fwd(q, kv, gi, dil, nh, f"att_fwd{gi}")
        outs.append(og)
        lses_narrow.append(lg)
        lses.append(lg_wide)
    oatt, oattb = _att_combine(outs, lses_narrow, "att_combine")
    x3, hf1 = _mm_res_norm(oattb, weight("w_o", oattb), x2, small["ffn_norm1"] + stage("ffn1", oattb), "o_mm")
    gu1, a1, o1 = _ffn_fwd(hf1, weight, 1, "ffn1")
    dres, dresb, loss_rows, d_final = _loss_head(
        o1, x3, small["final_norm"], _to_slabs([target], "target_to_slabs"), "loss_head")

    dhf1, zero = _ffn_bwd(dresb, hf1, gu1, a1, weight, emit, 1, "ffn1")
    dres, dresb, d_ffn_norm1 = _norm_bwd(x3, [small["ffn_norm1"] + zero], [dhf1], dres, "ffn1_norm_bwd")
    zero = emit("w_o", _mm_tn(oattb, dresb, 1, "o_dw", tn=1024))
    doatt = _mm_nt(dresb, weight("w_o", None), "o_dx", tr=2048)
    dks, dvs = [], []
    dq = lax.empty(q.shape, BF16)
    for gi, (window, dil) in enumerate(PATTERNS):
        dq, dk_g, dv_g = _att_bwd(q, kv, doatt, oatt, lses, dq, gi, dil, nh, f"att_bwd{gi}")
        dks.append(dk_g)
        dvs.append(dv_g)
    dkv = _dkv_sum(dks, dvs, "dkv_sum")
    w_q = weight("w_q", None)
    zero = zero + emit("w_q", _mm_tn(h1, dq, w_q.shape[0], "q_dw", tn=1536))
    zero = zero + emit("w_kv", _mm_tn(kvn, dkv, 1, "kv_dw", tn=1024))
    dh1 = _mm_nt(dq, w_q, "q_dx", tr=1536)
    dkvn = _mm_nt(dkv, weight("w_kv", None), "kv_dx", tr=1024)
    dres, dresb, d_b_norm, d_kv_norm = _norm_bwd(
        x2, [small["b_norm"] + zero, small["kv_norm"]], [dh1, dkvn], dres, "att_norm_bwd")
    dres, dresb = _from_slabs(dres, "from_slabs")
    dhf0, zero = _ffn_bwd(dresb, hf0, gu0, a0, weight, emit, 0, "ffn0")
    dres, dresb, d_ffn_norm0 = _norm_bwd(x1, [small["ffn_norm0"] + zero], [dhf0], dres, "ffn0_norm_bwd")
    del dresb
    dvg = _glu_bwd(dres, vg, "glu_bwd")
    w_glu = weight("w_glu", None)
    zero = emit("w_glu", _mm_tn(z, dvg, w_glu.shape[0], "glu_dw", tn=1024))
    dz = _mm_nt(dvg, w_glu, "glu_dx", tr=1024)
    dh0, d_s5_d, dcr, dci_neg, dbr, dbi, dar, dai = _s5_bwd(
        dz, y0, h0, xs_re, xs_im, bbd_re, bbd_im, cbd_re, cbd_im, pw_re, pw_im, pf_re, pf_im,
        small["s5_d"] + zero, "s5_bwd")
    grad_x, _, d_a_norm = _norm_bwd(x, [small["a_norm"]], [dh0], dres, "s5_norm_bwd")

    d_bb_re = _block_diag_take(dbr, c_, p_).transpose(0, 2, 1)
    d_bb_im = _block_diag_take(dbi, c_, p_).transpose(0, 2, 1)
    d_c_re = _block_diag_take(dcr, c_, p_)
    d_c_im = -_block_diag_take(dci_neg, c_, p_)
    d_lam_re, d_lam_im, d_log_dt, d_b_re, d_b_im = disc_vjp(
        (dar.reshape(n_groups, p_), dai.reshape(n_groups, p_), d_bb_re, d_bb_im))

    d_small = dict(lam_re=d_lam_re, lam_im=d_lam_im, log_dt=d_log_dt, b_re=d_b_re, b_im=d_b_im,
                   c_re=d_c_re, c_im=d_c_im, s5_d=d_s5_d, a_norm=d_a_norm, ffn_norm0=d_ffn_norm0,
                   ffn_norm1=d_ffn_norm1, b_norm=d_b_norm, kv_norm=d_kv_norm, final_norm=d_final)
    return loss_rows, grad_x, d_small


def _place():
    x, y, c = lax.axis_index("x"), lax.axis_index("y"), lax.axis_index("c")
    return x, y, c, [(1 - x, y), (x, 1 - y), (1 - x, 1 - y)]


_ANY = pl.BlockSpec(memory_space=pl.ANY)


_HBM = pl.BlockSpec(memory_space=pltpu.HBM)
_SEM = pl.BlockSpec(memory_space=pltpu.SEMAPHORE)
_EFFECT = pltpu.SideEffectType.DATAFLOW_SIDE_EFFECTING


def _in_hbm(a):
    return pltpu.with_memory_space_constraint(a, pltpu.HBM)


def _cast_place(shards, layer, chip, name):
    _, r, c = shards.shape
    tm = _tile(r, ROWS, 16)

    def kern(chip_ref, x_ref, o_ref):
        del chip_ref
        o_ref[...] = x_ref[...].astype(BF16)

    return pl.pallas_call(
        kern, name=name,
        grid_spec=pltpu.PrefetchScalarGridSpec(
            num_scalar_prefetch=1, grid=(r // tm,),
            in_specs=[pl.BlockSpec((None, tm, c), lambda i, ch: (layer, i, 0))],
            out_specs=pl.BlockSpec((None, tm, c), lambda i, ch: (ch[0], i, 0))),
        out_shape=jax.ShapeDtypeStruct((N_CHIPS, r, c), BF16),
        compiler_params=_params(("parallel",)),
    )(chip, shards)


def _my_part(land, block, c, halves):
    if not halves:
        return land.at[block]
    half = land.shape[1] // 2
    return land.at[block, pl.ds(c * half, half)]


def _gather_start(lands, name, halves, after):
    n = len(lands)

    def body(*refs):
        land = refs[:n]
        send, recv = refs[n + 1:2 * n + 1], refs[2 * n + 1:3 * n + 1]
        token = refs[4 * n + 1]
        x, y, c, peers = _place()
        me = 2 * x + y
        for a in range(n):
            for k, (px, py) in enumerate(peers):
                part = _my_part(land[a], me, c, halves)
                pltpu.make_async_remote_copy(
                    src_ref=part, dst_ref=part, send_sem=send[a].at[k], recv_sem=recv[a].at[k],
                    device_id=(px, py, c), device_id_type=MESH).start()
        token[...] = jnp.zeros_like(token)

    outs = pl.pallas_call(
        body, name=name,
        out_shape=[pltpu.SemaphoreType.DMA((3,))] * (2 * n) + [pltpu.HBM(a.shape, a.dtype) for a in lands]
        + [jax.ShapeDtypeStruct((8, 128), F32)],
        in_specs=[_HBM] * n + [_ANY],
        out_specs=[_SEM] * (2 * n) + [_HBM] * n + [pl.BlockSpec(memory_space=pltpu.VMEM)],
        input_output_aliases={i: 2 * n + i for i in range(n)},
        compiler_params=pltpu.CompilerParams(has_side_effects=_EFFECT),
    )(*[_in_hbm(a) for a in lands], after)
    return outs[:n], outs[n:2 * n], outs[2 * n:3 * n], outs[3 * n]


def _gather_wait(lands, sends, recvs, after, name, halves):
    n = len(lands)

    def body(*refs):
        land, send, recv = refs[:n], refs[n:2 * n], refs[2 * n:3 * n]
        x, y, c, peers = _place()
        me = 2 * x + y
        for a in range(n):
            for k, (px, py) in enumerate(peers):
                cp = pltpu.make_async_remote_copy(
                    src_ref=_my_part(land[a], me, c, halves), dst_ref=_my_part(land[a], 2 * px + py, c, halves),
                    send_sem=send[a].at[k], recv_sem=recv[a].at[k], device_id=(px, py, c), device_id_type=MESH)
                cp.wait_send()
                cp.wait_recv()

    return pl.pallas_call(
        body, name=name,
        out_shape=[pltpu.HBM(a.shape, a.dtype) for a in lands],
        in_specs=[_HBM] * n + [_SEM] * (2 * n) + [_ANY], out_specs=[_HBM] * n,
        input_output_aliases={i: i for i in range(n)},
        compiler_params=pltpu.CompilerParams(has_side_effects=_EFFECT),
    )(*lands, *sends, *recvs, after)


def _forward_start(lands, name):
    n = len(lands)

    def body(*refs):
        land = refs[:n]
        send, recv = refs[n:2 * n], refs[2 * n:3 * n]
        token = refs[4 * n]
        x, y, c, peers = _place()
        for a in range(n):
            for k, (px, py) in enumerate(peers):
                part = _my_part(land[a], 2 * px + py, c, True)
                pltpu.make_async_remote_copy(
                    src_ref=part, dst_ref=part, send_sem=send[a].at[k], recv_sem=recv[a].at[k],
                    device_id=(x, y, 1 - c), device_id_type=MESH).start()
        token[...] = jnp.zeros_like(token)

    outs = pl.pallas_call(
        body, name=name,
        out_shape=[pltpu.SemaphoreType.DMA((3,))] * (2 * n) + [pltpu.HBM(a.shape, a.dtype) for a in lands]
        + [jax.ShapeDtypeStruct((8, 128), F32)],
        in_specs=[_HBM] * n,
        out_specs=[_SEM] * (2 * n) + [_HBM] * n + [pl.BlockSpec(memory_space=pltpu.VMEM)],
        input_output_aliases={i: 2 * n + i for i in range(n)},
        compiler_params=pltpu.CompilerParams(has_side_effects=_EFFECT),
    )(*[_in_hbm(a) for a in lands])
    return outs[:n], outs[n:2 * n], outs[2 * n:3 * n], outs[3 * n]


def _forward_wait(land, send, recv, after, name):
    def body(land_ref, send_sem, recv_sem, after_ref, land_out):
        del after_ref, land_out
        x, y, c, peers = _place()
        for k, (px, py) in enumerate(peers):
            cp = pltpu.make_async_remote_copy(
                src_ref=_my_part(land_ref, 2 * px + py, c, True), dst_ref=_my_part(land_ref, 2 * px + py, 1 - c, True),
                send_sem=send_sem.at[k], recv_sem=recv_sem.at[k], device_id=(x, y, 1 - c), device_id_type=MESH)
            cp.wait_send()
            cp.wait_recv()

    return pl.pallas_call(
        body, name=name,
        out_shape=pltpu.HBM(land.shape, land.dtype),
        in_specs=[_HBM, _SEM, _SEM, _ANY], out_specs=_HBM,
        input_output_aliases={0: 0},
        compiler_params=pltpu.CompilerParams(has_side_effects=_EFFECT),
    )(land, send, recv, after)


def _scatter_start(g, name):
    def body(g_ref, land_ref, send, recv, g_out, land_out, token):
        del g_out, land_out
        x, y, c, peers = _place()
        for k, (px, py) in enumerate(peers):
            pltpu.make_async_remote_copy(
                src_ref=g_ref.at[2 * px + py], dst_ref=land_ref.at[k], send_sem=send.at[k], recv_sem=recv.at[k],
                device_id=(px, py, c), device_id_type=MESH).start()
        token[...] = jnp.zeros_like(token)

    land = lax.empty((3,) + g.shape[1:], g.dtype)
    return pl.pallas_call(
        body, name=name,
        out_shape=(pltpu.SemaphoreType.DMA((3,)), pltpu.SemaphoreType.DMA((3,)),
                   pltpu.HBM(g.shape, g.dtype), pltpu.HBM(land.shape, land.dtype),
                   jax.ShapeDtypeStruct((8, 128), F32)),
        in_specs=(_HBM, _HBM), out_specs=(_SEM, _SEM, _HBM, _HBM, pl.BlockSpec(memory_space=pltpu.VMEM)),
        input_output_aliases={0: 2, 1: 3},
        compiler_params=pltpu.CompilerParams(has_side_effects=_EFFECT),
    )(_in_hbm(g), _in_hbm(land))


def _scatter_wait(started, after):
    n = len(started)

    def body(*refs):
        gs, lands = refs[:n], refs[n:2 * n]
        sends, recvs = refs[2 * n:3 * n], refs[3 * n:4 * n]
        x, y, c, peers = _place()
        for a in range(n):
            for k, (px, py) in enumerate(peers):
                cp = pltpu.make_async_remote_copy(
                    src_ref=gs[a].at[2 * px + py], dst_ref=lands[a].at[k], send_sem=sends[a].at[k],
                    recv_sem=recvs[a].at[k], device_id=(px, py, c), device_id_type=MESH)
                cp.wait_send()
                cp.wait_recv()

    gs = [s[2] for s in started]
    lands = [s[3] for s in started]
    outs = pl.pallas_call(
        body, name="scatter_wait",
        out_shape=[pltpu.HBM(a.shape, a.dtype) for a in gs + lands],
        in_specs=[_HBM] * (2 * n) + [_SEM] * (2 * n) + [_ANY], out_specs=[_HBM] * (2 * n),
        input_output_aliases={i: i for i in range(2 * n)},
        compiler_params=pltpu.CompilerParams(has_side_effects=_EFFECT),
    )(*gs, *lands, *[s[0] for s in started], *[s[1] for s in started], after)
    return outs[:n], outs[n:]


def _sibling():
    return lax.axis_index("x"), lax.axis_index("y"), 1 - lax.axis_index("c")


def _swap_start(parts):
    n = len(parts)

    def body(*refs):
        ins, land = refs[:n], refs[n:2 * n]
        send, recv = refs[2 * n], refs[2 * n + 1]
        token = refs[4 * n + 2]
        for a in range(n):
            pltpu.make_async_remote_copy(
                src_ref=ins[a], dst_ref=land[a], send_sem=send.at[a], recv_sem=recv.at[a],
                device_id=_sibling(), device_id_type=MESH).start()
        token[...] = jnp.zeros_like(token)

    lands = [lax.empty(a.shape, a.dtype) for a in parts]
    outs = pl.pallas_call(
        body, name="swap_start",
        out_shape=[pltpu.SemaphoreType.DMA((n,))] * 2 + [pltpu.HBM(a.shape, a.dtype) for a in parts + lands]
        + [jax.ShapeDtypeStruct((8, 128), F32)],
        in_specs=[_HBM] * (2 * n),
        out_specs=[_SEM] * 2 + [_HBM] * (2 * n) + [pl.BlockSpec(memory_space=pltpu.VMEM)],
        input_output_aliases={i: 2 + i for i in range(2 * n)},
        compiler_params=pltpu.CompilerParams(has_side_effects=_EFFECT),
    )(*[_in_hbm(a) for a in parts + lands])
    return outs[0], outs[1], outs[2:2 + n], outs[2 + n:2 + 2 * n], outs[2 + 2 * n]


def _swap_wait(parts, lands, send, recv, after):
    n = len(parts)

    def body(*refs):
        ins, land = refs[:n], refs[n:2 * n]
        send_sem, recv_sem = refs[2 * n], refs[2 * n + 1]
        for a in range(n):
            cp = pltpu.make_async_remote_copy(
                src_ref=ins[a], dst_ref=land[a], send_sem=send_sem.at[a], recv_sem=recv_sem.at[a],
                device_id=_sibling(), device_id_type=MESH)
            cp.wait_send()
            cp.wait_recv()

    outs = pl.pallas_call(
        body, name="swap_wait",
        out_shape=[pltpu.HBM(a.shape, a.dtype) for a in list(parts) + list(lands)],
        in_specs=[_HBM] * (2 * n) + [_SEM] * 2 + [_ANY], out_specs=[_HBM] * (2 * n),
        input_output_aliases={i: i for i in range(2 * n)},
        compiler_params=pltpu.CompilerParams(has_side_effects=_EFFECT),
    )(*parts, *lands, send, recv, after)
    return outs[:n], outs[n:]


def _all_reduce_small(v):
    nd, r, w = v.shape
    assert nd == N_DEV

    def body(v_ref, out_ref, land_ref, red_ref, send1, recv1, send2, recv2):
        x, y, c = lax.axis_index("x"), lax.axis_index("y"), lax.axis_index("c")
        me = 4 * x + 2 * y + c
        peers = []
        for k in range(1, N_DEV):
            kx, ky, kc = (k >> 2) & 1, (k >> 1) & 1, k & 1
            peers.append((1 - x if kx else x, 1 - y if ky else y, 1 - c if kc else c))
        first = []
        for k, (px, py, pc) in enumerate(peers):
            cp = pltpu.make_async_remote_copy(
                src_ref=v_ref.at[4 * px + 2 * py + pc], dst_ref=land_ref.at[me], send_sem=send1.at[k],
                recv_sem=recv1.at[k], device_id=(px, py, pc), device_id_type=MESH)
            cp.start()
            first.append(cp)
        land_ref[me] = v_ref[me]
        for cp in first:
            cp.wait()
        acc = land_ref[0]
        for j in range(1, N_DEV):
            acc = acc + land_ref[j]
        red_ref[...] = acc
        second = []
        for k, (px, py, pc) in enumerate(peers):
            cp = pltpu.make_async_remote_copy(
                src_ref=red_ref, dst_ref=out_ref.at[me], send_sem=send2.at[k],
                recv_sem=recv2.at[k], device_id=(px, py, pc), device_id_type=MESH)
            cp.start()
            second.append(cp)
        out_ref[me] = acc
        for cp in second:
            cp.wait()

    vmem = pl.BlockSpec(memory_space=pltpu.VMEM)
    return pl.pallas_call(
        body, name="all_reduce_small",
        in_specs=[vmem], out_specs=vmem,
        out_shape=jax.ShapeDtypeStruct((nd, r, w), F32),
        scratch_shapes=[pltpu.VMEM((nd, r, w), F32), pltpu.VMEM((r, w), F32)]
        + [pltpu.SemaphoreType.DMA((N_DEV - 1,))] * 4,
        compiler_params=pltpu.CompilerParams(vmem_limit_bytes=VMEM_LIMIT),
    )(v)


def _adam_math(w, g, m, v):
    m = ADAM_B1 * m + (1.0 - ADAM_B1) * g
    v = ADAM_B2 * v + (1.0 - ADAM_B2) * (g * g)
    m_hat = m / (1.0 - ADAM_B1 ** ADAM_STEP)
    v_hat = v / (1.0 - ADAM_B2 ** ADAM_STEP)
    delta = -ADAM_LR * (m_hat / (jnp.sqrt(v_hat) + ADAM_EPS) + ADAM_WD * w)
    return delta, m, v


def _sum_blocks(own, got, chip, name):
    _, r, c = own.shape
    tm = _tile(r, ROWS, 16)

    def kern(chip_ref, own_ref, got_ref, o_ref):
        del chip_ref
        acc = own_ref[...].astype(F32)
        for k in range(3):
            acc = acc + got_ref[k].astype(F32)
        o_ref[...] = acc

    return pl.pallas_call(
        kern, name=name,
        grid_spec=pltpu.PrefetchScalarGridSpec(
            num_scalar_prefetch=1, grid=(r // tm,),
            in_specs=[pl.BlockSpec((None, tm, c), lambda i, ch: (ch[0], i, 0)),
                      pl.BlockSpec((3, tm, c), lambda i, ch: (0, i, 0))],
            out_specs=pl.BlockSpec((tm, c), lambda i, ch: (i, 0))),
        out_shape=jax.ShapeDtypeStruct((r, c), F32),
        compiler_params=_params(("parallel",)),
    )(chip, own, got)


def _adamw(parts, w, m, v, name):
    nl, r, c = w.shape
    assert len(parts) == nl
    tm = _tile(r, 128, 8)

    def kern(*refs):
        p_refs = refs[:2 * nl]
        w_ref, m_ref, v_ref, g_ref, d_ref, mo_ref, vo_ref = refs[2 * nl:]
        layer = pl.program_id(0)
        g = p_refs[0][...] + p_refs[1][...]
        for ll in range(1, nl):
            g = jnp.where(layer == ll, p_refs[2 * ll][...] + p_refs[2 * ll + 1][...], g)
        g_ref[...] = g
        d_ref[...], mo_ref[...], vo_ref[...] = _adam_math(w_ref[...], g, m_ref[...], v_ref[...])

    def part_spec(ll):
        return pl.BlockSpec((tm, c), lambda l, i: (jnp.where(l == ll, i, 0), 0))

    spec = pl.BlockSpec((None, tm, c), lambda l, i: (l, i, 0))
    return pl.pallas_call(
        kern, name=name, grid=(nl, r // tm),
        in_specs=[part_spec(ll) for ll in range(nl) for _ in range(2)] + [spec] * 3, out_specs=[spec] * 4,
        out_shape=[jax.ShapeDtypeStruct((nl, r, c), F32)] * 4,
        compiler_params=_params(("arbitrary", "parallel")),
    )(*[p for pair in parts for p in pair], w, m, v)


def _adamw_small(g, w, m, v, name):
    def kern(g_ref, w_ref, m_ref, v_ref, d_ref, mo_ref, vo_ref):
        d_ref[...], mo_ref[...], vo_ref[...] = _adam_math(w_ref[...], g_ref[...], m_ref[...], v_ref[...])

    return pl.pallas_call(
        kern, name=name,
        out_shape=[jax.ShapeDtypeStruct(g.shape, F32)] * 3,
        compiler_params=pltpu.CompilerParams(vmem_limit_bytes=VMEM_LIMIT),
    )(g, w, m, v)


def _pack(arrays, rows):
    flat = jnp.concatenate([a.reshape(-1).astype(F32) for a in arrays])
    return jnp.pad(flat, (0, rows * 128 - flat.shape[0])).reshape(rows, 128)


def _unpack(packed, shapes):
    flat = packed.reshape(-1)
    out, off = [], 0
    for shp in shapes:
        size = math.prod(shp)
        out.append(flat[off:off + size].reshape(shp))
        off += size
    return out


_REPLICATED = ["s5_lam_re", "s5_lam_im", "s5_log_dt", "s5_b_re", "s5_b_im", "s5_c_re", "s5_c_im",
               "ffn_norm", "b_norm_mix", "kv_norm", "final_norm"]
_CHIP_VECTORS = ["s5_d", "a_norm_mix"]
_BIG = ["s5_w_glu", "ffn_w_in", "ffn_w_out", "attn_w_q", "attn_w_o", "w_kv"]
_WEIGHT_ORDER = ["s5_lam_re", "s5_lam_im", "s5_log_dt", "s5_b_re", "s5_b_im", "s5_c_re", "s5_c_im", "s5_d",
                 "s5_w_glu", "a_norm_mix", "ffn_norm", "ffn_w_in", "ffn_w_out", "b_norm_mix", "attn_w_q",
                 "attn_w_o", "kv_norm", "w_kv", "final_norm"]


def _step(x, loss_target, w, m, v):
    s, d = x.shape[1], x.shape[2]
    chip = 2 * lax.axis_index("x") + lax.axis_index("y")

    col_sharded = dict(w_glu=("s5_w_glu", 0), w_in0=("ffn_w_in", 0), w_in1=("ffn_w_in", 1), w_q=("attn_w_q", 0))
    row_sharded = dict(w_out0=("ffn_w_out", 0), w_out1=("ffn_w_out", 1), w_o=("attn_w_o", 0), w_kv=("w_kv", 0))
    names = ["w_glu", "w_in0", "w_out0", "w_kv", "w_q", "w_o", "w_in1", "w_out1"]
    local = {**col_sharded, **row_sharded}

    def layers(a):
        return a.reshape((-1,) + a.shape[-2:])

    chip_arr = jnp.reshape(chip, (1,)).astype(jnp.int32)
    vector_lands = [lax.dynamic_update_slice(jnp.zeros((N_CHIPS,) + w[n].shape, F32), w[n][None], (chip, 0, 0))
                    for n in _CHIP_VECTORS]
    v_send, v_recv, v_land, v_token = _gather_start(vector_lands, "gather_start_vectors", False,
                                                    jnp.zeros((8, 128), F32))
    first = 2
    lands = [_cast_place(layers(w[local[n][0]]), local[n][1], chip_arr, f"cast_{n}") for n in names[:first]]
    send, recv, land_thru, first_token = _gather_start(lands, "gather_start_first", True, v_token)
    lands = [_cast_place(layers(w[local[n][0]]), local[n][1], chip_arr, f"cast_{n}") for n in names[first:]]
    *later, rest_token = _gather_start(lands, "gather_start_rest", True, first_token)
    send, recv, land_thru = [list(a) + list(b) for a, b in zip((send, recv, land_thru), later)]
    vectors = _gather_wait(v_land, v_send, v_recv, jnp.zeros((8, 128), F32), "gather_wait_vectors", False)
    s5_d_full = vectors[0].reshape(1, d)
    a_norm_full = vectors[1].reshape(1, d) + rest_token[0, 0]

    batches = dict(glu=["w_glu", "w_in0", "w_out0"], attention=["w_kv", "w_q", "w_o"], ffn1=["w_in1", "w_out1"])
    forwarded, arrived = {}, {}

    def stage(batch, after):
        idx = [names.index(n) for n in batches[batch]]
        got = _gather_wait([land_thru[i] for i in idx], [send[i] for i in idx], [recv[i] for i in idx], after,
                           f"gather_wait_{batch}", True)
        f_send, f_recv, f_land, token = _forward_start(got, f"forward_start_{batch}")
        forwarded.update(zip(batches[batch], zip(f_send, f_recv, f_land)))
        return token[0, 0]

    def weight(name, after):
        if name not in arrived:
            f_send, f_recv, f_land = forwarded[name]
            land = _forward_wait(f_land, f_send, f_recv, after, f"forward_wait_{name}")
            arrived[name] = land if name in col_sharded else land.reshape(1, -1, land.shape[-1])
        return arrived[name]

    started = {}

    def emit(name, dw):
        outs = _scatter_start(dw.reshape(N_CHIPS, -1, dw.shape[-1]), f"scatter_start_{name}")
        started[name] = outs[:4]
        return outs[4][0, 0]

    small = dict(lam_re=w["s5_lam_re"][0], lam_im=w["s5_lam_im"][0], log_dt=w["s5_log_dt"][0],
                 b_re=w["s5_b_re"][0], b_im=w["s5_b_im"][0], c_re=w["s5_c_re"][0], c_im=w["s5_c_im"][0],
                 s5_d=s5_d_full, a_norm=a_norm_full, ffn_norm0=w["ffn_norm"][0:1], ffn_norm1=w["ffn_norm"][1:2],
                 b_norm=w["b_norm_mix"], kv_norm=w["kv_norm"].reshape(1, d), final_norm=w["final_norm"].reshape(1, d))

    loss_rows, grad_x, d_small = _local_step(x[0], loss_target[0], small, weight, emit, stage)

    g4, got = _scatter_wait([started[n] for n in names], grad_x)
    partial = [_sum_blocks(o, r, chip_arr, f"sum_{n}") for n, o, r in zip(names, g4, got)]
    swap_send, swap_recv, partial, swap_land, swap_token = _swap_start(partial)
    result = {}

    rep_grads = [d_small["lam_re"], d_small["lam_im"], d_small["log_dt"], d_small["b_re"], d_small["b_im"],
                 d_small["c_re"], d_small["c_im"],
                 jnp.concatenate([d_small["ffn_norm0"], d_small["ffn_norm1"]], axis=0),
                 d_small["b_norm"], d_small["kv_norm"], d_small["final_norm"]]
    vec_grads = [d_small["s5_d"], d_small["a_norm"]]
    to_reduce = rep_grads + vec_grads + [jnp.sum(loss_rows).reshape(1) + swap_token[0, 0]]
    total = sum(math.prod(a.shape) for a in to_reduce)
    rows_per = -(-total // (N_DEV * 128 * 8)) * 8
    reduced = _all_reduce_small(_pack(to_reduce, N_DEV * rows_per).reshape(N_DEV, rows_per, 128))
    red = _unpack(reduced, [a.shape for a in to_reduce])
    loss = red[-1][0]
    g_small = dict(zip(_REPLICATED, [r.reshape(w[n].shape) for r, n in zip(red[:len(rep_grads)], _REPLICATED)]))
    for n, r in zip(_CHIP_VECTORS, red[len(rep_grads):-1]):
        g_small[n] = lax.dynamic_slice_in_dim(r.reshape(1, d), chip * (d // N_CHIPS), d // N_CHIPS, axis=1)
    small_names = _REPLICATED + _CHIP_VECTORS
    n_small = sum(math.prod(w[n].shape) for n in small_names)
    rows_small = -(-n_small // (128 * 8)) * 8
    packed = [_pack([src[n] for n in small_names], rows_small) for src in (g_small, w, m, v)]
    upd = _adamw_small(*packed, "adamw_small")
    shapes = [w[n].shape for n in small_names]
    for n, dl, mo, vo in zip(small_names, *[_unpack(u, shapes) for u in upd]):
        result[n] = [g_small[n], dl, mo, vo]

    partial, other = _swap_wait(partial, swap_land, swap_send, swap_recv, upd[0])
    part = dict(zip(names, zip(partial, other)))
    for name in _BIG:
        parts = [part[n] for n in sorted((n for n in names if local[n][0] == name), key=lambda n: local[n][1])]
        outs = _adamw(parts, layers(w[name]), layers(m[name]), layers(v[name]), f"adamw_{name}")
        result[name] = [o.reshape(w[name].shape) for o in outs]

    cols = [[result[n][t] for n in _WEIGHT_ORDER] for t in range(4)]
    return (loss, grad_x.reshape(x.shape), *cols[0], *cols[1], *cols[2], *cols[3])


def kernel(x, s5_lam_re, s5_lam_im, s5_log_dt, s5_b_re, s5_b_im, s5_c_re, s5_c_im, s5_d, s5_w_glu, a_norm_mix, ffn_norm, ffn_w_in, ffn_w_out, b_norm_mix, attn_w_q, attn_w_o, kv_norm, w_kv, final_norm, loss_target, m_s5_lam_re, m_s5_lam_im, m_s5_log_dt, m_s5_b_re, m_s5_b_im, m_s5_c_re, m_s5_c_im, m_s5_d, m_s5_w_glu, m_a_norm_mix, m_ffn_norm, m_ffn_w_in, m_ffn_w_out, m_b_norm_mix, m_attn_w_q, m_attn_w_o, m_kv_norm, m_w_kv, m_final_norm, v_s5_lam_re, v_s5_lam_im, v_s5_log_dt, v_s5_b_re, v_s5_b_im, v_s5_c_re, v_s5_c_im, v_s5_d, v_s5_w_glu, v_a_norm_mix, v_ffn_norm, v_ffn_w_in, v_ffn_w_out, v_b_norm_mix, v_attn_w_q, v_attn_w_o, v_kv_norm, v_w_kv, v_final_norm):
    w = dict(zip(_WEIGHT_ORDER, (s5_lam_re, s5_lam_im, s5_log_dt, s5_b_re, s5_b_im, s5_c_re, s5_c_im, s5_d, s5_w_glu, a_norm_mix, ffn_norm, ffn_w_in, ffn_w_out, b_norm_mix, attn_w_q, attn_w_o, kv_norm, w_kv, final_norm)))
    m = dict(zip(_WEIGHT_ORDER, (m_s5_lam_re, m_s5_lam_im, m_s5_log_dt, m_s5_b_re, m_s5_b_im, m_s5_c_re, m_s5_c_im, m_s5_d, m_s5_w_glu, m_a_norm_mix, m_ffn_norm, m_ffn_w_in, m_ffn_w_out, m_b_norm_mix, m_attn_w_q, m_attn_w_o, m_kv_norm, m_w_kv, m_final_norm)))
    v = dict(zip(_WEIGHT_ORDER, (v_s5_lam_re, v_s5_lam_im, v_s5_log_dt, v_s5_b_re, v_s5_b_im, v_s5_c_re, v_s5_c_im, v_s5_d, v_s5_w_glu, v_a_norm_mix, v_ffn_norm, v_ffn_w_in, v_ffn_w_out, v_b_norm_mix, v_attn_w_q, v_attn_w_o, v_kv_norm, v_w_kv, v_final_norm)))
    return _step(x, loss_target, w, m, v)
```

```python
import functools
import math

import jax
import jax.numpy as jnp
from jax import lax
from jax.experimental import pallas as pl
from jax.experimental.pallas import tpu as pltpu

F32 = jnp.float32
BF16 = jnp.bfloat16

S5_GROUP_CH = 16
S5_STATE = 64
GROUPS_PER_TILE = 8
HEAD_DIM = 128
N_KV_HEADS = 4
PATTERNS = ((128, 1), (512, 4), (2048, 16))
ATT_BLK = 128
EPS = 1e-6
NEG_INF = -1e30
SCAN_T = 1024
ADAM_LR = 0.001
ADAM_B1 = 0.9
ADAM_B2 = 0.999
ADAM_EPS = 1e-08
ADAM_WD = 0.01
ADAM_STEP = 10
N_CHIPS = 4
N_DEV = 8
VMEM_LIMIT = 56 * 1024 * 1024
MESH = pl.DeviceIdType.MESH
GELU_K = math.sqrt(2.0 / math.pi)
GELU_C = 0.044715


def _tile(n, pref, unit=128):
    if n <= pref:
        return n
    best = None
    t = unit
    while t <= pref:
        if n % t == 0:
            best = t
        t += unit
    assert best is not None, (n, pref, unit)
    return best


def _params(sem):
    return pltpu.CompilerParams(dimension_semantics=sem, vmem_limit_bytes=VMEM_LIMIT)


def _dot(a, b):
    return jnp.dot(a, b, preferred_element_type=F32)


def _dot_nt(a, b):
    return lax.dot_general(a, b, (((1,), (1,)), ((), ())), preferred_element_type=F32)


def _dot_tn(a, b):
    return lax.dot_general(a, b, (((0,), (0,)), ((), ())), preferred_element_type=F32)


def _mm_nn(a, w, out_dtype, name, tm=512, tn=1536, tk=2048):
    m, k = a.shape
    nb, k2, nq = w.shape
    assert k == k2
    tm, tn, tk = _tile(m, tm, 8), _tile(nq, tn), _tile(k, tk)
    per, nk = nq // tn, k // tk

    def kern(a_ref, w_ref, o_ref, *acc):
        p = _dot(a_ref[...], w_ref[...])
        if nk == 1:
            o_ref[...] = p.astype(o_ref.dtype)
        else:
            acc_ref, = acc
            kk = pl.program_id(2)

            @pl.when(kk == 0)
            def _():
                acc_ref[...] = p

            @pl.when(kk > 0)
            def _():
                acc_ref[...] += p

            @pl.when(kk == nk - 1)
            def _():
                o_ref[...] = acc_ref[...].astype(o_ref.dtype)

    return pl.pallas_call(
        kern, name=name, grid=(nb * per, m // tm, nk),
        in_specs=[pl.BlockSpec((tm, tk), lambda j, i, kk: (i, kk)),
                  pl.BlockSpec((None, tk, tn), lambda j, i, kk: (j // per, kk, j % per))],
        out_specs=pl.BlockSpec((tm, tn), lambda j, i, kk: (i, j)),
        out_shape=jax.ShapeDtypeStruct((m, nb * nq), out_dtype),
        scratch_shapes=[] if nk == 1 else [pltpu.VMEM((tm, tn), F32)],
        compiler_params=_params(("parallel", "parallel", "arbitrary")),
    )(a, w)


def _paired_block(r, per, nb):
    j = r // 2
    return (r % 2) * (nb // 2) + j // per, j % per


def _mm_nt(a, w, name, tm=512, tr=1536, tkc=1024, paired=False):
    m, n = a.shape
    nb, k, nq = w.shape
    assert n == nb * nq
    tm, tr, tkc = _tile(m, tm, 8), _tile(nq, tr), _tile(k, tkc)
    per = nq // tr

    def kern(a_ref, w_ref, o_ref):
        acc = None
        for r in range(nb * per):
            blk, tile = _paired_block(r, per, nb) if paired else (r // per, r % per)
            p = _dot_nt(a_ref[:, r * tr:(r + 1) * tr], w_ref[blk, :, tile * tr:(tile + 1) * tr])
            acc = p if acc is None else acc + p
        o_ref[...] = acc

    return pl.pallas_call(
        kern, name=name, grid=(k // tkc, m // tm),
        in_specs=[pl.BlockSpec((tm, n), lambda kc, i: (i, 0)),
                  pl.BlockSpec((nb, tkc, nq), lambda kc, i: (0, kc, 0))],
        out_specs=pl.BlockSpec((tm, tkc), lambda kc, i: (i, kc)),
        out_shape=jax.ShapeDtypeStruct((m, k), F32),
        compiler_params=_params(("parallel", "parallel")),
    )(a, w)


def _mm_tn(a, dy, nb, name, tkk=512, tn=1536, paired=False):
    s, k = a.shape
    s2, n = dy.shape
    assert s == s2 and n % nb == 0
    nq = n // nb
    tkk, tn = _tile(k, tkk), _tile(nq, tn)
    per = nq // tn

    def w_block(j):
        return _paired_block(j, per, nb) if paired else (j // per, j % per)

    def kern(a_ref, dy_ref, o_ref):
        o_ref[...] = _dot_tn(a_ref[...], dy_ref[...]).astype(o_ref.dtype)

    return pl.pallas_call(
        kern, name=name, grid=(nb * per, k // tkk),
        in_specs=[pl.BlockSpec((s, tkk), lambda j, kk: (0, kk)),
                  pl.BlockSpec((s, tn), lambda j, kk: (0, j))],
        out_specs=pl.BlockSpec((None, tkk, tn), lambda j, kk: (w_block(j)[0], kk, w_block(j)[1])),
        out_shape=jax.ShapeDtypeStruct((nb, k, nq), BF16),
        compiler_params=_params(("parallel", "parallel")),
    )(a, dy)


ROWS = 256


def _rms(x, g):
    r = lax.rsqrt(jnp.mean(x * x, axis=-1, keepdims=True) + EPS)
    return x * r * g


def _rms_bwd(x, g, dh):
    r = lax.rsqrt(jnp.mean(x * x, axis=-1, keepdims=True) + EPS)
    xh = x * r
    dgx = dh * g
    dx = r * (dgx - xh * jnp.mean(dgx * xh, axis=-1, keepdims=True))
    return dx, dh * xh


def _sigmoid(x):
    return 1.0 / (1.0 + jnp.exp(-x))


def _gelu(y):
    return 0.5 * y * (1.0 + jnp.tanh(GELU_K * (y + GELU_C * y * y * y)))


def _gelu_grad(y):
    t = jnp.tanh(GELU_K * (y + GELU_C * y * y * y))
    return 0.5 * (1.0 + t) + 0.5 * y * (1.0 - t * t) * GELU_K * (1.0 + 3.0 * GELU_C * y * y)


def _row_spec(tm, d, col=0):
    return pl.BlockSpec((tm, d), lambda i: (i, col))


def _vec_spec(d):
    return pl.BlockSpec((1, d), lambda i: (0, 0))


def _acc_rows(ref, val, i):
    s = jnp.sum(val, axis=0, keepdims=True)

    @pl.when(i == 0)
    def _():
        ref[...] = s

    @pl.when(i > 0)
    def _():
        ref[...] += s


def _norm_f32(x, g, name):
    s, d = x.shape
    tm = _tile(s, ROWS, 8)

    def kern(x_ref, g_ref, h_ref):
        h_ref[...] = _rms(x_ref[...], g_ref[...])

    return pl.pallas_call(
        kern, name=name, grid=(s // tm,),
        in_specs=[_row_spec(tm, d), _vec_spec(d)],
        out_specs=_row_spec(tm, d),
        out_shape=jax.ShapeDtypeStruct((s, d), F32),
        compiler_params=_params(("parallel",)),
    )(x, g)


def _glu_res_norm(vg, x, g, name):
    s, d = x.shape
    tm = _tile(s, ROWS, 16)

    def kern(val_ref, gate_ref, x_ref, g_ref, x1_ref, hf_ref):
        x1 = x_ref[...] + val_ref[...] * _sigmoid(gate_ref[...])
        x1_ref[...] = x1
        hf_ref[...] = _rms(x1, g_ref[...]).astype(BF16)

    return pl.pallas_call(
        kern, name=name, grid=(s // tm,),
        in_specs=[_row_spec(tm, d, 0), _row_spec(tm, d, 1), _row_spec(tm, d), _vec_spec(d)],
        out_specs=[_row_spec(tm, d), _row_spec(tm, d)],
        out_shape=[jax.ShapeDtypeStruct((s, d), F32), jax.ShapeDtypeStruct((s, d), BF16)],
        compiler_params=_params(("parallel",)),
    )(vg, vg, x, g)


FFN_TILE = 1408


def _ffn_in_act(hf, w_in, name, tm=512):
    s, k = hf.shape
    nb, _, nq = w_in.shape
    tm, tn = _tile(s, tm, 16), _tile(nq, FFN_TILE)
    per = nq // tn
    nf = (nb // 2) * per

    def kern(h_ref, wg_ref, wu_ref, a_ref, gu_ref):
        h = h_ref[...]
        g = _dot(h, wg_ref[...])
        u = _dot(h, wu_ref[...])
        sg = _sigmoid(g)
        silu = g * sg
        a_ref[...] = (silu * u).astype(BF16)
        gu_ref[:, :tn] = (u * (sg + silu * (1.0 - sg))).astype(BF16)
        gu_ref[:, tn:] = silu.astype(BF16)

    return pl.pallas_call(
        kern, name=name, grid=(nf, s // tm),
        in_specs=[pl.BlockSpec((tm, k), lambda j, i: (i, 0)),
                  pl.BlockSpec((None, k, tn), lambda j, i: (j // per, 0, j % per)),
                  pl.BlockSpec((None, k, tn), lambda j, i: (nb // 2 + j // per, 0, j % per))],
        out_specs=[pl.BlockSpec((tm, tn), lambda j, i: (i, j)),
                   pl.BlockSpec((tm, 2 * tn), lambda j, i: (i, j))],
        out_shape=[jax.ShapeDtypeStruct((s, nf * tn), BF16), jax.ShapeDtypeStruct((s, 2 * nf * tn), BF16)],
        compiler_params=_params(("parallel", "parallel")),
    )(hf, w_in, w_in)


def _ffn_out_dx_act(dresb, w_out, gu, name, tm=512):
    s, d = dresb.shape
    f = w_out.shape[1]
    tm = _tile(s, tm, 16)
    tn = _tile(f // 2, FFN_TILE)

    def kern(d_ref, w_ref, gu_ref, o_ref):
        da = _dot_nt(d_ref[...], w_ref[...])
        o_ref[:, :tn] = (da * gu_ref[:, :tn].astype(F32)).astype(BF16)
        o_ref[:, tn:] = (da * gu_ref[:, tn:].astype(F32)).astype(BF16)

    pair = pl.BlockSpec((tm, 2 * tn), lambda j, i: (i, j))
    return pl.pallas_call(
        kern, name=name, grid=(f // tn, s // tm),
        in_specs=[pl.BlockSpec((tm, d), lambda j, i: (i, 0)),
                  pl.BlockSpec((None, tn, d), lambda j, i: (0, j, 0)), pair],
        out_specs=pair,
        out_shape=jax.ShapeDtypeStruct((s, 2 * f), BF16),
        compiler_params=_params(("parallel", "parallel")),
    )(dresb, w_out, gu)


def _mm_res_norm(a, w, x, g, name, tm=512):
    m, k = a.shape
    d = w.shape[2]
    tm = _tile(m, tm, 16)

    def kern(a_ref, w_ref, x_ref, g_ref, xn_ref, hf_ref):
        xn = x_ref[...] + _dot(a_ref[...], w_ref[0])
        xn_ref[...] = xn
        hf_ref[...] = _rms(xn, g_ref[...]).astype(BF16)

    return pl.pallas_call(
        kern, name=name, grid=(m // tm,),
        in_specs=[_row_spec(tm, k), pl.BlockSpec((1, k, d), lambda i: (0, 0, 0)), _row_spec(tm, d), _vec_spec(d)],
        out_specs=[_row_spec(tm, d), _row_spec(tm, d)],
        out_shape=[jax.ShapeDtypeStruct((m, d), F32), jax.ShapeDtypeStruct((m, d), BF16)],
        compiler_params=_params(("parallel",)),
    )(a, w, x, g)


def _loss_head(o, x, g, target, name):
    s, d = x.shape
    tm = _tile(s, ROWS, 16)

    def kern(o_ref, x_ref, g_ref, t_ref, dx_ref, dxb_ref, loss_ref, dg_ref):
        i = pl.program_id(0)
        x4 = x_ref[...] + o_ref[...]
        gg = g_ref[...]
        diff = _rms(x4, gg) - t_ref[...]
        dx, dgr = _rms_bwd(x4, gg, diff * (1.0 / d))
        dx_ref[...] = dx
        dxb_ref[...] = dx.astype(BF16)
        _acc_rows(loss_ref, diff * diff * (0.5 / d), i)
        _acc_rows(dg_ref, dgr, i)

    return pl.pallas_call(
        kern, name=name, grid=(s // tm,),
        in_specs=[_row_spec(tm, d), _row_spec(tm, d), _vec_spec(d), _row_spec(tm, d)],
        out_specs=[_row_spec(tm, d), _row_spec(tm, d), _vec_spec(d), _vec_spec(d)],
        out_shape=[jax.ShapeDtypeStruct((s, d), F32), jax.ShapeDtypeStruct((s, d), BF16),
                   jax.ShapeDtypeStruct((1, d), F32), jax.ShapeDtypeStruct((1, d), F32)],
        compiler_params=_params(("arbitrary",)),
    )(o, x, g, target)


def _norm_bwd(x, gains, dhs, dres, name, with_bf16=False):
    s, d = x.shape
    tm = _tile(s, ROWS, 16)
    ng = len(gains)
    nd = 2 if with_bf16 else 1

    def kern(x_ref, dres_ref, *refs):
        i = pl.program_id(0)
        x_ = x_ref[...]
        acc = dres_ref[...]
        for t in range(ng):
            dx, dgr = _rms_bwd(x_, refs[t][...], refs[ng + t][...])
            acc = acc + dx
            _acc_rows(refs[2 * ng + nd + t], dgr, i)
        refs[2 * ng][...] = acc
        if with_bf16:
            refs[2 * ng + 1][...] = acc.astype(BF16)

    return pl.pallas_call(
        kern, name=name, grid=(s // tm,),
        in_specs=[_row_spec(tm, d), _row_spec(tm, d)] + [_vec_spec(d)] * ng + [_row_spec(tm, d)] * ng,
        out_specs=[_row_spec(tm, d)] * nd + [_vec_spec(d)] * ng,
        out_shape=[jax.ShapeDtypeStruct((s, d), F32), jax.ShapeDtypeStruct((s, d), BF16)][:nd]
        + [jax.ShapeDtypeStruct((1, d), F32)] * ng,
        compiler_params=_params(("arbitrary",)),
    )(x, dres, *gains, *dhs)


def _glu_bwd(dmix, vg, name):
    s, d = dmix.shape
    tm = _tile(s, ROWS, 16)

    def kern(dm_ref, val_ref, gate_ref, o_ref):
        dm = dm_ref[...]
        sg = _sigmoid(gate_ref[...])
        o_ref[:, :d] = (dm * sg).astype(BF16)
        o_ref[:, d:] = (dm * val_ref[...] * sg * (1.0 - sg)).astype(BF16)

    return pl.pallas_call(
        kern, name=name, grid=(s // tm,),
        in_specs=[_row_spec(tm, d), _row_spec(tm, d, 0), _row_spec(tm, d, 1)],
        out_specs=_row_spec(tm, 2 * d),
        out_shape=jax.ShapeDtypeStruct((s, 2 * d), BF16),
        compiler_params=_params(("parallel",)),
    )(dmix, vg, vg)


SUB = 8


def _local_scan(vr, vi, pre_ref, pim_ref, reverse):
    sub = lax.broadcasted_iota(jnp.int32, vr.shape, 1)
    sign = -1.0 if reverse else 1.0
    for sh in (1, 2, 4):
        ar = pre_ref[sh - 1:sh, :][None]
        ai = sign * pim_ref[sh - 1:sh, :][None]
        keep = sub < SUB - sh if reverse else sub >= sh
        sr = jnp.where(keep, pltpu.roll(vr, SUB - sh if reverse else sh, 1), 0.0)
        si = jnp.where(keep, pltpu.roll(vi, SUB - sh if reverse else sh, 1), 0.0)
        vr, vi = vr + ar * sr - ai * si, vi + ar * si + ai * sr
    return vr, vi


def _s5_fwd(u, bbd_re, bbd_im, cbd_re, cbd_im, pw_re, pw_im, dskip, name):
    s, d = u.shape
    nt, cw, lw = bbd_re.shape
    t = _tile(s, SCAN_T, SUB)
    nc = s // t
    ng = t // SUB

    def kern(u_ref, bre_ref, bim_ref, cre_ref, cim_ref, pre_ref, pim_ref, d_ref,
             xr_ref, xi_ref, y_ref, z_ref, car_ref, cai_ref):
        c = pl.program_id(1)

        @pl.when(c == 0)
        def _():
            car_ref[...] = jnp.zeros_like(car_ref)
            cai_ref[...] = jnp.zeros_like(cai_ref)

        u_ = u_ref[...]
        ub = u_.astype(BF16)
        vr, vi = _local_scan(_dot(ub, bre_ref[...]).reshape(ng, SUB, lw),
                             _dot(ub, bim_ref[...]).reshape(ng, SUB, lw), pre_ref, pim_ref, False)
        cr = car_ref[...]
        ci = cai_ref[...]
        pr = pre_ref[...]
        pi = pim_ref[...]
        for gidx in range(ng):
            rows = slice(gidx * SUB, (gidx + 1) * SUB)
            gr = vr[gidx] + pr * cr - pi * ci
            gi = vi[gidx] + pr * ci + pi * cr
            xr_ref[rows, :] = gr
            xi_ref[rows, :] = gi
            cr, ci = gr[SUB - 1:SUB, :], gi[SUB - 1:SUB, :]
        car_ref[...] = cr
        cai_ref[...] = ci
        y = (_dot(xr_ref[...].astype(BF16), cre_ref[...]) - _dot(xi_ref[...].astype(BF16), cim_ref[...])
             + d_ref[...] * u_)
        y_ref[...] = y
        z_ref[...] = _gelu(y).astype(BF16)

    tok = pl.BlockSpec((t, cw), lambda j, c: (c, j))
    st = pl.BlockSpec((t, lw), lambda j, c: (c, j))
    return pl.pallas_call(
        kern, name=name, grid=(nt, nc),
        in_specs=[tok,
                  pl.BlockSpec((None, cw, lw), lambda j, c: (j, 0, 0)),
                  pl.BlockSpec((None, cw, lw), lambda j, c: (j, 0, 0)),
                  pl.BlockSpec((None, lw, cw), lambda j, c: (j, 0, 0)),
                  pl.BlockSpec((None, lw, cw), lambda j, c: (j, 0, 0)),
                  pl.BlockSpec((SUB, lw), lambda j, c: (0, j)),
                  pl.BlockSpec((SUB, lw), lambda j, c: (0, j)),
                  pl.BlockSpec((1, cw), lambda j, c: (0, j))],
        out_specs=[st, st, tok, tok],
        out_shape=[jax.ShapeDtypeStruct((s, nt * lw), F32), jax.ShapeDtypeStruct((s, nt * lw), F32),
                   jax.ShapeDtypeStruct((s, d), F32), jax.ShapeDtypeStruct((s, d), BF16)],
        scratch_shapes=[pltpu.VMEM((1, lw), F32), pltpu.VMEM((1, lw), F32)],
        compiler_params=_params(("parallel", "arbitrary")),
    )(u, bbd_re, bbd_im, cbd_re, cbd_im, pw_re, pw_im, dskip)


def _s5_bwd(dz, y, u, xs_re, xs_im, bbd_re, bbd_im, cbd_re, cbd_im, pw_re, pw_im, pf_re, pf_im, dskip, name):
    s, d = u.shape
    nt, cw, lw = bbd_re.shape
    t = _tile(s, SCAN_T, SUB)
    nc = s // t
    ng = t // SUB

    def kern(dz_ref, y_ref, u_ref, xr_ref, xi_ref, bre_ref, bim_ref, cre_ref, cim_ref,
             pre_ref, pim_ref, fre_ref, fim_ref, d_ref,
             du_ref, dd_ref, dcr_ref, dci_ref, dbr_ref, dbi_ref, dar_ref, dai_ref,
             car_ref, cai_ref, gr_ref, gi_ref):
        c = pl.program_id(1)

        @pl.when(c == 0)
        def _():
            car_ref[...] = jnp.zeros_like(car_ref)
            cai_ref[...] = jnp.zeros_like(cai_ref)

        u_ = u_ref[...]
        ub = u_.astype(BF16)
        dy = dz_ref[...] * _gelu_grad(y_ref[...])
        dyb = dy.astype(BF16)
        vr, vi = _local_scan(_dot_nt(dyb, cre_ref[...]).reshape(ng, SUB, lw),
                             (-_dot_nt(dyb, cim_ref[...])).reshape(ng, SUB, lw), pre_ref, pim_ref, True)
        later_r = car_ref[...]
        later_i = cai_ref[...]
        cr, ci = later_r, later_i
        fr = fre_ref[...]
        fi = fim_ref[...]
        for gidx in reversed(range(ng)):
            rows = slice(gidx * SUB, (gidx + 1) * SUB)
            ar = vr[gidx] + fr * cr + fi * ci
            ai = vi[gidx] + fr * ci - fi * cr
            gr_ref[rows, :] = ar
            gi_ref[rows, :] = ai
            cr, ci = ar[0:1, :], ai[0:1, :]
        car_ref[...] = cr
        cai_ref[...] = ci
        gr = gr_ref[...]
        gi = gi_ref[...]
        rows = lax.broadcasted_iota(jnp.int32, (t, lw), 0)
        gsr = jnp.where(rows < t - 1, pltpu.roll(gr, t - 1, 0), later_r)
        gsi = jnp.where(rows < t - 1, pltpu.roll(gi, t - 1, 0), later_i)
        xr = xr_ref[...]
        xi = xi_ref[...]
        dar = jnp.sum(gsr * xr + gsi * xi, axis=0, keepdims=True)
        dai = jnp.sum(gsi * xr - gsr * xi, axis=0, keepdims=True)
        grb = gr.astype(BF16)
        gib = gi.astype(BF16)
        dbr = _dot_tn(ub, grb)
        dbi = _dot_tn(ub, gib)
        dcr = _dot_tn(dyb, xr.astype(BF16))
        dci = _dot_tn(dyb, xi.astype(BF16))
        du_ref[...] = dy * d_ref[...] + _dot_nt(grb, bre_ref[...]) + _dot_nt(gib, bim_ref[...])
        ddv = jnp.sum(dy * u_, axis=0, keepdims=True)

        @pl.when(c == 0)
        def _():
            dd_ref[...] = ddv
            dcr_ref[...] = dcr
            dci_ref[...] = dci
            dbr_ref[...] = dbr
            dbi_ref[...] = dbi
            dar_ref[...] = dar
            dai_ref[...] = dai

        @pl.when(c > 0)
        def _():
            dd_ref[...] += ddv
            dcr_ref[...] += dcr
            dci_ref[...] += dci
            dbr_ref[...] += dbr
            dbi_ref[...] += dbi
            dar_ref[...] += dar
            dai_ref[...] += dai

    tok = pl.BlockSpec((t, cw), lambda j, c: (nc - 1 - c, j))
    st = pl.BlockSpec((t, lw), lambda j, c: (nc - 1 - c, j))
    wb = pl.BlockSpec((None, cw, lw), lambda j, c: (j, 0, 0))
    wc = pl.BlockSpec((None, lw, cw), lambda j, c: (j, 0, 0))
    pw = pl.BlockSpec((SUB, lw), lambda j, c: (0, j))
    vec_c = pl.BlockSpec((1, cw), lambda j, c: (0, j))
    vec_l = pl.BlockSpec((1, lw), lambda j, c: (0, j))
    return pl.pallas_call(
        kern, name=name, grid=(nt, nc),
        in_specs=[tok, tok, tok, st, st, wb, wb, wc, wc, pw, pw, pw, pw, vec_c],
        out_specs=[tok, vec_c, wb, wb, wb, wb, vec_l, vec_l],
        out_shape=[jax.ShapeDtypeStruct((s, d), F32), jax.ShapeDtypeStruct((1, d), F32)]
        + [jax.ShapeDtypeStruct((nt, cw, lw), F32)] * 4
        + [jax.ShapeDtypeStruct((1, nt * lw), F32)] * 2,
        scratch_shapes=[pltpu.VMEM((1, lw), F32), pltpu.VMEM((1, lw), F32),
                        pltpu.VMEM((t, lw), F32), pltpu.VMEM((t, lw), F32)],
        compiler_params=_params(("parallel", "arbitrary")),
    )(dz, y, u, xs_re, xs_im, bbd_re, bbd_im, cbd_re, cbd_im, pw_re, pw_im, pf_re, pf_im, dskip)


def _s5_discretize(lam_re, lam_im, log_dt, b_re, b_im):
    dt = jnp.exp(log_dt)[:, None]
    mag = jnp.exp(lam_re * dt)
    ang = lam_im * dt
    lb_re = mag * jnp.cos(ang)
    lb_im = mag * jnp.sin(ang)
    nr = lb_re - 1.0
    den = lam_re * lam_re + lam_im * lam_im
    f_re = (nr * lam_re + lb_im * lam_im) / den
    f_im = (lb_im * lam_re - nr * lam_im) / den
    bb_re = f_re[..., None] * b_re - f_im[..., None] * b_im
    bb_im = f_re[..., None] * b_im + f_im[..., None] * b_re
    return lb_re, lb_im, bb_re, bb_im


def _block_diag(w):
    nt, ng, a, b = w.shape
    eye = jnp.eye(ng, dtype=w.dtype)
    return (w[:, :, :, None, :] * eye[None, :, None, :, None]).reshape(nt, ng * a, ng * b)


def _block_diag_take(w, a, b):
    nt = w.shape[0]
    ng = GROUPS_PER_TILE
    w5 = w.reshape(nt, ng, a, ng, b)
    on_diagonal = jnp.eye(ng, dtype=bool)[None, :, None, :, None]
    return jnp.sum(jnp.where(on_diagonal, w5, 0.0), axis=3).reshape(nt * ng, a, b)


def _att_combine(outs, lses, name):
    s, d = outs[0].shape
    tm = _tile(s, ROWS, 16)
    ng = len(outs)

    def kern(*refs):
        for h in range(d // HEAD_DIM):
            cols = slice(h * HEAD_DIM, (h + 1) * HEAD_DIM)
            ls = [refs[ng + t][:, h:h + 1] for t in range(ng)]
            mx = functools.reduce(jnp.maximum, ls)
            es = [jnp.exp(l - mx) for l in ls]
            den = functools.reduce(lambda a, b: a + b, es)
            o = functools.reduce(lambda a, b: a + b,
                                 [es[t] / den * refs[t][:, cols].astype(F32) for t in range(ng)])
            refs[2 * ng][:, cols] = o
            refs[2 * ng + 1][:, cols] = o.astype(BF16)

    return pl.pallas_call(
        kern, name=name, grid=(s // tm,),
        in_specs=[_row_spec(tm, d)] * ng + [_row_spec(tm, d // HEAD_DIM)] * ng,
        out_specs=[_row_spec(tm, d)] * 2,
        out_shape=[jax.ShapeDtypeStruct((s, d), F32), jax.ShapeDtypeStruct((s, d), BF16)],
        compiler_params=_params(("parallel",)),
    )(*outs, *lses)


DIL_MAX = 16


def _slab(r):
    return 4 * (r % 4) + r // 4


def _to_slabs(xs, name):
    s, w = xs[0].shape
    n = s // DIL_MAX
    nx = len(xs)

    def kern(*refs):
        o_ref = refs[nx]
        for r in range(DIL_MAX):
            rows = [x_ref[pl.ds(r, n, stride=DIL_MAX), :] for x_ref in refs[:nx]]
            o_ref[_slab(r) * n:(_slab(r) + 1) * n, :] = functools.reduce(lambda a, b: a + b, rows)

    spec = pl.BlockSpec((s, 128), lambda i: (0, i))
    return pl.pallas_call(
        kern, name=name, grid=(w // 128,), in_specs=[spec] * nx, out_specs=spec,
        out_shape=jax.ShapeDtypeStruct((s, w), F32), compiler_params=_params(("parallel",)),
    )(*xs)


def _from_slabs(x, name):
    s, w = x.shape
    n = s // DIL_MAX

    def kern(x_ref, o_ref, ob_ref):
        for r in range(DIL_MAX):
            o_ref[pl.ds(r, n, stride=DIL_MAX), :] = x_ref[_slab(r) * n:(_slab(r) + 1) * n, :]
        ob_ref[...] = o_ref[...].astype(BF16)

    spec = pl.BlockSpec((s, 128), lambda i: (0, i))
    return pl.pallas_call(
        kern, name=name, grid=(w // 128,), in_specs=[spec], out_specs=[spec, spec],
        out_shape=[jax.ShapeDtypeStruct((s, w), F32), jax.ShapeDtypeStruct((s, w), BF16)],
        compiler_params=_params(("parallel",)),
    )(x)


def _norm_bf16(x, gains, name):
    s, d = x.shape
    tm = _tile(s, ROWS, 16)
    ng = len(gains)

    def kern(x_ref, *refs):
        x_ = x_ref[...]
        for t in range(ng):
            refs[ng + t][...] = _rms(x_, refs[t][...]).astype(BF16)

    return pl.pallas_call(
        kern, name=name, grid=(s // tm,),
        in_specs=[_row_spec(tm, d)] + [_vec_spec(d)] * ng, out_specs=[_row_spec(tm, d)] * ng,
        out_shape=[jax.ShapeDtypeStruct((s, d), BF16)] * ng,
        compiler_params=_params(("parallel",)),
    )(x, *gains)


def _att_geometry(dil, s):
    g = DIL_MAX // dil
    return s // DIL_MAX, g, max(ATT_BLK // g, 16)


def _att_mask(g, j, jb, rep):
    b = g * j

    def pos(i):
        sl, jj = i // j, i % j
        off = {1: 0, 4: sl, 16: sl // 4 + 4 * (sl % 4)}[g]
        return g * jj + off

    qi = lax.broadcasted_iota(jnp.int32, (rep * b, 2 * b), 0) % b
    ki = lax.broadcasted_iota(jnp.int32, (rep * b, 2 * b), 1)
    prev = ki < b
    dist = pos(qi) - pos(ki % b) + jnp.where(prev, b, 0)
    return (dist >= 0) & (dist <= ATT_BLK) & (jnp.logical_not(prev) | (jb > 0))


def _stack_heads(x, rep):
    return jnp.concatenate([x[:, h * HEAD_DIM:(h + 1) * HEAD_DIM] for h in range(rep)], axis=0)


def _att_fwd(q, kv, gi, dil, nh, name):
    s = q.shape[0]
    n, g, j = _att_geometry(dil, s)
    b = g * j
    ng = q.shape[1] // (nh * HEAD_DIM)
    rep = nh // N_KV_HEADS
    qw = rep * HEAD_DIM
    scale = HEAD_DIM ** -0.5
    q3 = q.reshape(DIL_MAX, n, q.shape[1])
    kv3 = kv.reshape(DIL_MAX, n, kv.shape[1])

    kw = N_KV_HEADS * HEAD_DIM

    def kern(q_ref, kc_ref, kp_ref, vc_ref, vp_ref, o_ref, l_ref, lw_ref):
        jb = pl.program_id(1)
        mask = _att_mask(g, j, jb, rep)
        for kh in range(N_KV_HEADS):
            hs = slice(kh * HEAD_DIM, (kh + 1) * HEAD_DIM)
            kc, kp, vc, vp = [r[:, :, hs].reshape(b, HEAD_DIM) for r in (kc_ref, kp_ref, vc_ref, vp_ref)]
            k2 = jnp.concatenate([kp, kc], axis=0)
            v2 = jnp.concatenate([vp, vc], axis=0)
            qs = _stack_heads(q_ref[:, :, kh * qw:(kh + 1) * qw].reshape(b, qw), rep)
            sc = jnp.where(mask, _dot_nt(qs, k2) * scale, NEG_INF)
            m = jnp.max(sc, axis=-1, keepdims=True)
            p = jnp.exp(sc - m)
            l = jnp.sum(p, axis=-1, keepdims=True)
            out = _dot((p / l).astype(BF16), v2)
            lse = m + jnp.log(l)
            wide = jnp.broadcast_to(lse, (rep * b, HEAD_DIM))
            for h in range(rep):
                cols = slice(kh * qw + h * HEAD_DIM, kh * qw + (h + 1) * HEAD_DIM)
                o_ref[:, :, cols] = out[h * b:(h + 1) * b].astype(BF16).reshape(g, j, HEAD_DIM)
                lw_ref[:, :, cols] = wide[h * b:(h + 1) * b].reshape(g, j, HEAD_DIM)
                l_ref[:, :, kh * rep + h:kh * rep + h + 1] = lse[h * b:(h + 1) * b].reshape(g, j, 1)

    def kv_spec(col, prev):
        if prev:
            return pl.BlockSpec((g, j, kw), lambda c, jb: (c, jnp.maximum(jb - 1, 0), col))
        return pl.BlockSpec((g, j, kw), lambda c, jb: (c, jb, col))

    out_spec = pl.BlockSpec((g, j, nh * HEAD_DIM), lambda c, jb: (c, jb, 0))
    out, lse, lse_wide = pl.pallas_call(
        kern, name=name, grid=(dil, n // j),
        in_specs=[pl.BlockSpec((g, j, nh * HEAD_DIM), lambda c, jb: (c, jb, gi)),
                  kv_spec(0, False), kv_spec(0, True), kv_spec(1, False), kv_spec(1, True)],
        out_specs=[out_spec, pl.BlockSpec((g, j, nh), lambda c, jb: (c, jb, 0)), out_spec],
        out_shape=[jax.ShapeDtypeStruct((DIL_MAX, n, nh * HEAD_DIM), BF16),
                   jax.ShapeDtypeStruct((DIL_MAX, n, nh), F32),
                   jax.ShapeDtypeStruct((DIL_MAX, n, nh * HEAD_DIM), F32)],
        compiler_params=_params(("parallel", "arbitrary")),
    )(q3, kv3, kv3, kv3, kv3)
    del ng
    return out.reshape(s, nh * HEAD_DIM), lse.reshape(s, nh), lse_wide.reshape(s, nh * HEAD_DIM)


def _att_bwd(q, kv, do, o, lses, dq_all, gi, dil, nh, name):
    s = q.shape[0]
    n, g, j = _att_geometry(dil, s)
    b = g * j
    nb = n // j
    ng = len(lses)
    rep = nh // N_KV_HEADS
    qw = rep * HEAD_DIM
    dm = nh * HEAD_DIM
    scale = HEAD_DIM ** -0.5
    q3 = q.reshape(DIL_MAX, n, q.shape[1])
    kv3 = kv.reshape(DIL_MAX, n, kv.shape[1])
    wide = [a.reshape(DIL_MAX, n, dm) for a in (do, o, *lses)]
    kw = N_KV_HEADS * HEAD_DIM

    def kern(q_ref, kc_ref, kp_ref, vc_ref, vp_ref, do_ref, o_ref, *refs):
        l_refs = refs[:ng]
        dq_ref, dk_ref, dv_ref, ck_ref, cv_ref = refs[ng + 1:]
        jb = pl.program_id(1)

        @pl.when(jb == 0)
        def _():
            ck_ref[...] = jnp.zeros_like(ck_ref)
            cv_ref[...] = jnp.zeros_like(cv_ref)

        @pl.when(jb < nb)
        def _():
            mask = _att_mask(g, j, jb, rep)
            for kh in range(N_KV_HEADS):
                hs = slice(kh * HEAD_DIM, (kh + 1) * HEAD_DIM)
                ws = slice(kh * qw, (kh + 1) * qw)

                def stacked(ref):
                    return _stack_heads(ref[:, :, ws].reshape(b, qw), rep)

                kc, kp, vc, vp = [r[:, :, hs].reshape(b, HEAD_DIM) for r in (kc_ref, kp_ref, vc_ref, vp_ref)]
                k2 = jnp.concatenate([kp, kc], axis=0)
                v2 = jnp.concatenate([vp, vc], axis=0)
                qs = stacked(q_ref)
                ls = [stacked(r)[:, 0:1] for r in l_refs]
                mx = functools.reduce(jnp.maximum, ls)
                den = functools.reduce(lambda a, c: a + c, [jnp.exp(l - mx) for l in ls])
                lse_g = ls[gi]
                w = jnp.exp(lse_g - mx) / den
                do_ = stacked(do_ref)
                ct = w * jnp.sum(do_ * stacked(o_ref), axis=-1, keepdims=True)
                dob = (w * do_).astype(BF16)
                p = jnp.exp(jnp.where(mask, _dot_nt(qs, k2) * scale, NEG_INF) - lse_g)
                ds = (p * (_dot_nt(dob, v2) - ct) * scale).astype(BF16)
                dq = (_dot(ds, k2)).astype(BF16)
                for h in range(rep):
                    cols = slice(kh * qw + h * HEAD_DIM, kh * qw + (h + 1) * HEAD_DIM)
                    dq_ref[:, :, cols] = dq[h * b:(h + 1) * b].reshape(g, j, HEAD_DIM)
                dk2 = _dot_tn(ds, qs)
                dv2 = _dot_tn(p.astype(BF16), dob)
                dk_ref[:, :, hs] = (ck_ref[:, hs] + dk2[:b]).reshape(g, j, HEAD_DIM)
                dv_ref[:, :, hs] = (cv_ref[:, hs] + dv2[:b]).reshape(g, j, HEAD_DIM)
                ck_ref[:, hs] = dk2[b:]
                cv_ref[:, hs] = dv2[b:]

        @pl.when(jb == nb)
        def _():
            dk_ref[...] = ck_ref[...].reshape(g, j, kw)
            dv_ref[...] = cv_ref[...].reshape(g, j, kw)

    def jq(jb):
        return jnp.minimum(jb, nb - 1)

    def kv_spec(col, prev):
        if prev:
            return pl.BlockSpec((g, j, kw), lambda c, jb: (c, jnp.maximum(jq(jb) - 1, 0), col))
        return pl.BlockSpec((g, j, kw), lambda c, jb: (c, jq(jb), col))

    wide_spec = pl.BlockSpec((g, j, dm), lambda c, jb: (c, jq(jb), 0))
    dkv_spec = pl.BlockSpec((g, j, kw), lambda c, jb: (c, jnp.maximum(jb - 1, 0), 0))
    dq, dk, dv = pl.pallas_call(
        kern, name=name, grid=(dil, nb + 1),
        in_specs=[pl.BlockSpec((g, j, dm), lambda c, jb: (c, jq(jb), gi)),
                  kv_spec(0, False), kv_spec(0, True), kv_spec(1, False), kv_spec(1, True)]
        + [wide_spec] * (2 + ng) + [_ANY],
        out_specs=[pl.BlockSpec((g, j, dm), lambda c, jb: (c, jq(jb), gi)), dkv_spec, dkv_spec],
        out_shape=[jax.ShapeDtypeStruct((DIL_MAX, n, dq_all.shape[1]), BF16),
                   jax.ShapeDtypeStruct((DIL_MAX, n, kw), F32),
                   jax.ShapeDtypeStruct((DIL_MAX, n, kw), F32)],
        scratch_shapes=[pltpu.VMEM((b, kw), F32), pltpu.VMEM((b, kw), F32)],
        input_output_aliases={7 + ng: 0},
        compiler_params=_params(("parallel", "arbitrary")),
    )(q3, kv3, kv3, kv3, kv3, *wide, dq_all.reshape(DIL_MAX, n, dq_all.shape[1]))
    return dq.reshape(s, -1), dk.reshape(s, -1), dv.reshape(s, -1)


def _dkv_sum(dks, dvs, name):
    s, w = dks[0].shape
    tm = _tile(s, ROWS, 16)
    ng = len(dks)

    def kern(*refs):
        o_ref = refs[2 * ng]
        o_ref[:, :w] = functools.reduce(lambda a, b: a + b, [refs[t][...] for t in range(ng)]).astype(BF16)
        o_ref[:, w:] = functools.reduce(lambda a, b: a + b, [refs[ng + t][...] for t in range(ng)]).astype(BF16)

    return pl.pallas_call(
        kern, name=name, grid=(s // tm,),
        in_specs=[_row_spec(tm, w)] * (2 * ng),
        out_specs=_row_spec(tm, 2 * w),
        out_shape=jax.ShapeDtypeStruct((s, 2 * w), BF16),
        compiler_params=_params(("parallel",)),
    )(*dks, *dvs)


def _ffn_fwd(hf, weight, layer, tag):
    a, gu = _ffn_in_act(hf, weight(f"w_in{layer}", hf), f"{tag}_in")
    o = _mm_nn(a, weight(f"w_out{layer}", a), F32, f"{tag}_out", tn=1024, tk=5632)
    return gu, a, o


def _ffn_bwd(dresb, hf, gu, a, weight, emit, layer, tag):
    w_in, w_out = weight(f"w_in{layer}", None), weight(f"w_out{layer}", None)
    zero = emit(f"w_out{layer}", _mm_tn(a, dresb, 1, f"{tag}_out_dw", tn=1024))
    dgu = _ffn_out_dx_act(dresb, w_out, gu, f"{tag}_out_dx")
    zero = zero + emit(f"w_in{layer}",
                       _mm_tn(hf, dgu, w_in.shape[0], f"{tag}_in_dw", tn=FFN_TILE, paired=True))
    return _mm_nt(dgu, w_in, f"{tag}_in_dx", tr=FFN_TILE, tkc=512, paired=True), zero


def _local_step(x, target, small, weight, emit, stage):
    s, d = x.shape
    nh = d // HEAD_DIM
    n_groups = d // S5_GROUP_CH
    nt = n_groups // GROUPS_PER_TILE
    p_, c_ = S5_STATE, S5_GROUP_CH

    disc_in = (small["lam_re"], small["lam_im"], small["log_dt"], small["b_re"], small["b_im"])
    (lb_re, lb_im, bb_re, bb_im), disc_vjp = jax.vjp(_s5_discretize, *disc_in)
    del lb_re, lb_im
    dt = jnp.exp(small["log_dt"])[:, None]
    def pole_powers(exponents):
        k = exponents[:, None, None]
        mag = jnp.exp(k * (small["lam_re"] * dt)[None])
        ang = k * (small["lam_im"] * dt)[None]
        return ((mag * jnp.cos(ang)).reshape(SUB, n_groups * p_),
                (mag * jnp.sin(ang)).reshape(SUB, n_groups * p_))

    pw_re, pw_im = pole_powers(jnp.arange(1, SUB + 1, dtype=F32))
    pf_re, pf_im = pole_powers(jnp.arange(SUB, 0, -1, dtype=F32))
    bbd_re = _block_diag(bb_re.transpose(0, 2, 1).reshape(nt, GROUPS_PER_TILE, c_, p_)).astype(BF16)
    bbd_im = _block_diag(bb_im.transpose(0, 2, 1).reshape(nt, GROUPS_PER_TILE, c_, p_)).astype(BF16)
    cbd_re = _block_diag(small["c_re"].transpose(0, 2, 1).reshape(nt, GROUPS_PER_TILE, p_, c_)).astype(BF16)
    cbd_im = _block_diag(small["c_im"].transpose(0, 2, 1).reshape(nt, GROUPS_PER_TILE, p_, c_)).astype(BF16)

    h0 = _norm_f32(x, small["a_norm"], "s5_norm")
    xs_re, xs_im, y0, z = _s5_fwd(h0, bbd_re, bbd_im, cbd_re, cbd_im, pw_re, pw_im, small["s5_d"], "s5_fwd")
    zero = stage("glu", z)
    vg = _mm_nn(z, weight("w_glu", z), F32, "glu_mm", tn=1024)
    x1, hf0 = _glu_res_norm(vg, x, small["ffn_norm0"] + zero, "glu_res_norm")
    gu0, a0, o0 = _ffn_fwd(hf0, weight, 0, "ffn0")
    zero = stage("attention", a0)
    x2 = _to_slabs([o0, x1], "to_slabs")
    kvn, h1 = _norm_bf16(x2, [small["kv_norm"] + zero, small["b_norm"]], "att_norms")
    kv = _mm_nn(kvn, weight("w_kv", kvn), BF16, "kv_mm", tn=1024)
    q = _mm_nn(h1, weight("w_q", kv), BF16, "q_mm", tn=1536)
    outs, lses_narrow, lses = [], [], []
    for gi, (window, dil) in enumerate(PATTERNS):
        assert window // dil == ATT_BLK
        og, lg, lg_wide = _att_fwd(q, kv, gi, dil, nh, f"att_fwd{gi}")
        outs.append(og)
        lses_narrow.append(lg)
        lses.append(lg_wide)
    oatt, oattb = _att_combine(outs, lses_narrow, "att_combine")
    x3, hf1 = _mm_res_norm(oattb, weight("w_o", oattb), x2, small["ffn_norm1"] + stage("ffn1", oattb), "o_mm")
    gu1, a1, o1 = _ffn_fwd(hf1, weight, 1, "ffn1")
    dres, dresb, loss_rows, d_final = _loss_head(
        o1, x3, small["final_norm"], _to_slabs([target], "target_to_slabs"), "loss_head")

    dhf1, zero = _ffn_bwd(dresb, hf1, gu1, a1, weight, emit, 1, "ffn1")
    dres, dresb, d_ffn_norm1 = _norm_bwd(x3, [small["ffn_norm1"] + zero], [dhf1], dres, "ffn1_norm_bwd", True)
    zero = emit("w_o", _mm_tn(oattb, dresb, 1, "o_dw", tn=1024))
    doatt = _mm_nt(dresb, weight("w_o", None), "o_dx", tr=2048)
    dks, dvs = [], []
    dq = lax.empty(q.shape, BF16)
    for gi, (window, dil) in enumerate(PATTERNS):
        dq, dk_g, dv_g = _att_bwd(q, kv, doatt, oatt, lses, dq, gi, dil, nh, f"att_bwd{gi}")
        dks.append(dk_g)
        dvs.append(dv_g)
    dkv = _dkv_sum(dks, dvs, "dkv_sum")
    w_q = weight("w_q", None)
    zero = zero + emit("w_q", _mm_tn(h1, dq, w_q.shape[0], "q_dw", tn=1536))
    zero = zero + emit("w_kv", _mm_tn(kvn, dkv, 1, "kv_dw", tn=1024))
    dh1 = _mm_nt(dq, w_q, "q_dx", tr=1536)
    dkvn = _mm_nt(dkv, weight("w_kv", None), "kv_dx", tr=1024)
    dres, d_b_norm, d_kv_norm = _norm_bwd(
        x2, [small["b_norm"] + zero, small["kv_norm"]], [dh1, dkvn], dres, "att_norm_bwd")
    dres, dresb = _from_slabs(dres, "from_slabs")
    dhf0, zero = _ffn_bwd(dresb, hf0, gu0, a0, weight, emit, 0, "ffn0")
    dres, d_ffn_norm0 = _norm_bwd(x1, [small["ffn_norm0"] + zero], [dhf0], dres, "ffn0_norm_bwd")
    dvg = _glu_bwd(dres, vg, "glu_bwd")
    w_glu = weight("w_glu", None)
    zero = emit("w_glu", _mm_tn(z, dvg, w_glu.shape[0], "glu_dw", tn=1024))
    dz = _mm_nt(dvg, w_glu, "glu_dx", tr=1024)
    dh0, d_s5_d, dcr, dci_neg, dbr, dbi, dar, dai = _s5_bwd(
        dz, y0, h0, xs_re, xs_im, bbd_re, bbd_im, cbd_re, cbd_im, pw_re, pw_im, pf_re, pf_im,
        small["s5_d"] + zero, "s5_bwd")
    grad_x, d_a_norm = _norm_bwd(x, [small["a_norm"]], [dh0], dres, "s5_norm_bwd")

    d_bb_re = _block_diag_take(dbr, c_, p_).transpose(0, 2, 1)
    d_bb_im = _block_diag_take(dbi, c_, p_).transpose(0, 2, 1)
    d_c_re = _block_diag_take(dcr, c_, p_)
    d_c_im = -_block_diag_take(dci_neg, c_, p_)
    d_lam_re, d_lam_im, d_log_dt, d_b_re, d_b_im = disc_vjp(
        (dar.reshape(n_groups, p_), dai.reshape(n_groups, p_), d_bb_re, d_bb_im))

    d_small = dict(lam_re=d_lam_re, lam_im=d_lam_im, log_dt=d_log_dt, b_re=d_b_re, b_im=d_b_im,
                   c_re=d_c_re, c_im=d_c_im, s5_d=d_s5_d, a_norm=d_a_norm, ffn_norm0=d_ffn_norm0,
                   ffn_norm1=d_ffn_norm1, b_norm=d_b_norm, kv_norm=d_kv_norm, final_norm=d_final)
    return loss_rows, grad_x, d_small


def _place():
    x, y, c = lax.axis_index("x"), lax.axis_index("y"), lax.axis_index("c")
    return x, y, c, [(1 - x, y), (x, 1 - y), (1 - x, 1 - y)]


_ANY = pl.BlockSpec(memory_space=pl.ANY)


_HBM = pl.BlockSpec(memory_space=pltpu.HBM)
_SEM = pl.BlockSpec(memory_space=pltpu.SEMAPHORE)
_EFFECT = pltpu.SideEffectType.DATAFLOW_SIDE_EFFECTING


def _in_hbm(a):
    return pltpu.with_memory_space_constraint(a, pltpu.HBM)


def _cast_place(shards, layer, chip, name):
    _, r, c = shards.shape
    tm = _tile(r, ROWS, 16)

    def kern(chip_ref, x_ref, o_ref):
        del chip_ref
        o_ref[...] = x_ref[...].astype(BF16)

    return pl.pallas_call(
        kern, name=name,
        grid_spec=pltpu.PrefetchScalarGridSpec(
            num_scalar_prefetch=1, grid=(r // tm,),
            in_specs=[pl.BlockSpec((None, tm, c), lambda i, ch: (layer, i, 0))],
            out_specs=pl.BlockSpec((None, tm, c), lambda i, ch: (ch[0], i, 0))),
        out_shape=jax.ShapeDtypeStruct((N_CHIPS, r, c), BF16),
        compiler_params=_params(("parallel",)),
    )(chip, shards)


def _my_part(land, block, c, halves):
    if not halves:
        return land.at[block]
    half = land.shape[1] // 2
    return land.at[block, pl.ds(c * half, half)]


def _gather_start(lands, name, halves, after):
    n = len(lands)

    def body(*refs):
        land = refs[:n]
        send, recv = refs[n + 1:2 * n + 1], refs[2 * n + 1:3 * n + 1]
        token = refs[4 * n + 1]
        x, y, c, peers = _place()
        me = 2 * x + y
        for a in range(n):
            for k, (px, py) in enumerate(peers):
                part = _my_part(land[a], me, c, halves)
                pltpu.make_async_remote_copy(
                    src_ref=part, dst_ref=part, send_sem=send[a].at[k], recv_sem=recv[a].at[k],
                    device_id=(px, py, c), device_id_type=MESH).start()
        token[...] = jnp.zeros_like(token)

    outs = pl.pallas_call(
        body, name=name,
        out_shape=[pltpu.SemaphoreType.DMA((3,))] * (2 * n) + [pltpu.HBM(a.shape, a.dtype) for a in lands]
        + [jax.ShapeDtypeStruct((8, 128), F32)],
        in_specs=[_HBM] * n + [_ANY],
        out_specs=[_SEM] * (2 * n) + [_HBM] * n + [pl.BlockSpec(memory_space=pltpu.VMEM)],
        input_output_aliases={i: 2 * n + i for i in range(n)},
        compiler_params=pltpu.CompilerParams(has_side_effects=_EFFECT),
    )(*[_in_hbm(a) for a in lands], after)
    return outs[:n], outs[n:2 * n], outs[2 * n:3 * n], outs[3 * n]


def _gather_wait(lands, sends, recvs, after, name, halves):
    n = len(lands)

    def body(*refs):
        land, send, recv = refs[:n], refs[n:2 * n], refs[2 * n:3 * n]
        x, y, c, peers = _place()
        me = 2 * x + y
        for a in range(n):
            for k, (px, py) in enumerate(peers):
                cp = pltpu.make_async_remote_copy(
                    src_ref=_my_part(land[a], me, c, halves), dst_ref=_my_part(land[a], 2 * px + py, c, halves),
                    send_sem=send[a].at[k], recv_sem=recv[a].at[k], device_id=(px, py, c), device_id_type=MESH)
                cp.wait_send()
                cp.wait_recv()

    return pl.pallas_call(
        body, name=name,
        out_shape=[pltpu.HBM(a.shape, a.dtype) for a in lands],
        in_specs=[_HBM] * n + [_SEM] * (2 * n) + [_ANY], out_specs=[_HBM] * n,
        input_output_aliases={i: i for i in range(n)},
        compiler_params=pltpu.CompilerParams(has_side_effects=_EFFECT),
    )(*lands, *sends, *recvs, after)


def _forward_start(lands, name):
    n = len(lands)

    def body(*refs):
        land = refs[:n]
        send, recv = refs[n:2 * n], refs[2 * n:3 * n]
        token = refs[4 * n]
        x, y, c, peers = _place()
        for a in range(n):
            for k, (px, py) in enumerate(peers):
                part = _my_part(land[a], 2 * px + py, c, True)
                pltpu.make_async_remote_copy(
                    src_ref=part, dst_ref=part, send_sem=send[a].at[k], recv_sem=recv[a].at[k],
                    device_id=(x, y, 1 - c), device_id_type=MESH).start()
        token[...] = jnp.zeros_like(token)

    outs = pl.pallas_call(
        body, name=name,
        out_shape=[pltpu.SemaphoreType.DMA((3,))] * (2 * n) + [pltpu.HBM(a.shape, a.dtype) for a in lands]
        + [jax.ShapeDtypeStruct((8, 128), F32)],
        in_specs=[_HBM] * n,
        out_specs=[_SEM] * (2 * n) + [_HBM] * n + [pl.BlockSpec(memory_space=pltpu.VMEM)],
        input_output_aliases={i: 2 * n + i for i in range(n)},
        compiler_params=pltpu.CompilerParams(has_side_effects=_EFFECT),
    )(*[_in_hbm(a) for a in lands])
    return outs[:n], outs[n:2 * n], outs[2 * n:3 * n], outs[3 * n]


def _forward_wait(land, send, recv, after, name):
    def body(land_ref, send_sem, recv_sem, after_ref, land_out):
        del after_ref, land_out
        x, y, c, peers = _place()
        for k, (px, py) in enumerate(peers):
            cp = pltpu.make_async_remote_copy(
                src_ref=_my_part(land_ref, 2 * px + py, c, True), dst_ref=_my_part(land_ref, 2 * px + py, 1 - c, True),
                send_sem=send_sem.at[k], recv_sem=recv_sem.at[k], device_id=(x, y, 1 - c), device_id_type=MESH)
            cp.wait_send()
            cp.wait_recv()

    return pl.pallas_call(
        body, name=name,
        out_shape=pltpu.HBM(land.shape, land.dtype),
        in_specs=[_HBM, _SEM, _SEM, _ANY], out_specs=_HBM,
        input_output_aliases={0: 0},
        compiler_params=pltpu.CompilerParams(has_side_effects=_EFFECT),
    )(land, send, recv, after)


def _scatter_start(g, name):
    def body(g_ref, land_ref, send, recv, g_out, land_out, token):
        del g_out, land_out
        x, y, c, peers = _place()
        for k, (px, py) in enumerate(peers):
            pltpu.make_async_remote_copy(
                src_ref=g_ref.at[2 * px + py], dst_ref=land_ref.at[k], send_sem=send.at[k], recv_sem=recv.at[k],
                device_id=(px, py, c), device_id_type=MESH).start()
        token[...] = jnp.zeros_like(token)

    land = lax.empty((3,) + g.shape[1:], g.dtype)
    return pl.pallas_call(
        body, name=name,
        out_shape=(pltpu.SemaphoreType.DMA((3,)), pltpu.SemaphoreType.DMA((3,)),
                   pltpu.HBM(g.shape, g.dtype), pltpu.HBM(land.shape, land.dtype),
                   jax.ShapeDtypeStruct((8, 128), F32)),
        in_specs=(_HBM, _HBM), out_specs=(_SEM, _SEM, _HBM, _HBM, pl.BlockSpec(memory_space=pltpu.VMEM)),
        input_output_aliases={0: 2, 1: 3},
        compiler_params=pltpu.CompilerParams(has_side_effects=_EFFECT),
    )(_in_hbm(g), _in_hbm(land))


def _scatter_wait(started, after):
    n = len(started)

    def body(*refs):
        gs, lands = refs[:n], refs[n:2 * n]
        sends, recvs = refs[2 * n:3 * n], refs[3 * n:4 * n]
        x, y, c, peers = _place()
        for a in range(n):
            for k, (px, py) in enumerate(peers):
                cp = pltpu.make_async_remote_copy(
                    src_ref=gs[a].at[2 * px + py], dst_ref=lands[a].at[k], send_sem=sends[a].at[k],
                    recv_sem=recvs[a].at[k], device_id=(px, py, c), device_id_type=MESH)
                cp.wait_send()
                cp.wait_recv()

    gs = [s[2] for s in started]
    lands = [s[3] for s in started]
    outs = pl.pallas_call(
        body, name="scatter_wait",
        out_shape=[pltpu.HBM(a.shape, a.dtype) for a in gs + lands],
        in_specs=[_HBM] * (2 * n) + [_SEM] * (2 * n) + [_ANY], out_specs=[_HBM] * (2 * n),
        input_output_aliases={i: i for i in range(2 * n)},
        compiler_params=pltpu.CompilerParams(has_side_effects=_EFFECT),
    )(*gs, *lands, *[s[0] for s in started], *[s[1] for s in started], after)
    return outs[:n], outs[n:]


def _sibling():
    return lax.axis_index("x"), lax.axis_index("y"), 1 - lax.axis_index("c")


def _swap_start(parts):
    n = len(parts)

    def body(*refs):
        ins, land = refs[:n], refs[n:2 * n]
        send, recv = refs[2 * n], refs[2 * n + 1]
        token = refs[4 * n + 2]
        for a in range(n):
            pltpu.make_async_remote_copy(
                src_ref=ins[a], dst_ref=land[a], send_sem=send.at[a], recv_sem=recv.at[a],
                device_id=_sibling(), device_id_type=MESH).start()
        token[...] = jnp.zeros_like(token)

    lands = [lax.empty(a.shape, a.dtype) for a in parts]
    outs = pl.pallas_call(
        body, name="swap_start",
        out_shape=[pltpu.SemaphoreType.DMA((n,))] * 2 + [pltpu.HBM(a.shape, a.dtype) for a in parts + lands]
        + [jax.ShapeDtypeStruct((8, 128), F32)],
        in_specs=[_HBM] * (2 * n),
        out_specs=[_SEM] * 2 + [_HBM] * (2 * n) + [pl.BlockSpec(memory_space=pltpu.VMEM)],
        input_output_aliases={i: 2 + i for i in range(2 * n)},
        compiler_params=pltpu.CompilerParams(has_side_effects=_EFFECT),
    )(*[_in_hbm(a) for a in parts + lands])
    return outs[0], outs[1], outs[2:2 + n], outs[2 + n:2 + 2 * n], outs[2 + 2 * n]


def _swap_wait(parts, lands, send, recv, after):
    n = len(parts)

    def body(*refs):
        ins, land = refs[:n], refs[n:2 * n]
        send_sem, recv_sem = refs[2 * n], refs[2 * n + 1]
        for a in range(n):
            cp = pltpu.make_async_remote_copy(
                src_ref=ins[a], dst_ref=land[a], send_sem=send_sem.at[a], recv_sem=recv_sem.at[a],
                device_id=_sibling(), device_id_type=MESH)
            cp.wait_send()
            cp.wait_recv()

    outs = pl.pallas_call(
        body, name="swap_wait",
        out_shape=[pltpu.HBM(a.shape, a.dtype) for a in list(parts) + list(lands)],
        in_specs=[_HBM] * (2 * n) + [_SEM] * 2 + [_ANY], out_specs=[_HBM] * (2 * n),
        input_output_aliases={i: i for i in range(2 * n)},
        compiler_params=pltpu.CompilerParams(has_side_effects=_EFFECT),
    )(*parts, *lands, send, recv, after)
    return outs[:n], outs[n:]


def _all_reduce_small(v):
    nd, r, w = v.shape
    assert nd == N_DEV

    def body(v_ref, out_ref, land_ref, red_ref, send1, recv1, send2, recv2):
        x, y, c = lax.axis_index("x"), lax.axis_index("y"), lax.axis_index("c")
        me = 4 * x + 2 * y + c
        peers = []
        for k in range(1, N_DEV):
            kx, ky, kc = (k >> 2) & 1, (k >> 1) & 1, k & 1
            peers.append((1 - x if kx else x, 1 - y if ky else y, 1 - c if kc else c))
        first = []
        for k, (px, py, pc) in enumerate(peers):
            cp = pltpu.make_async_remote_copy(
                src_ref=v_ref.at[4 * px + 2 * py + pc], dst_ref=land_ref.at[me], send_sem=send1.at[k],
                recv_sem=recv1.at[k], device_id=(px, py, pc), device_id_type=MESH)
            cp.start()
            first.append(cp)
        land_ref[me] = v_ref[me]
        for cp in first:
            cp.wait()
        acc = land_ref[0]
        for j in range(1, N_DEV):
            acc = acc + land_ref[j]
        red_ref[...] = acc
        second = []
        for k, (px, py, pc) in enumerate(peers):
            cp = pltpu.make_async_remote_copy(
                src_ref=red_ref, dst_ref=out_ref.at[me], send_sem=send2.at[k],
                recv_sem=recv2.at[k], device_id=(px, py, pc), device_id_type=MESH)
            cp.start()
            second.append(cp)
        out_ref[me] = acc
        for cp in second:
            cp.wait()

    vmem = pl.BlockSpec(memory_space=pltpu.VMEM)
    return pl.pallas_call(
        body, name="all_reduce_small",
        in_specs=[vmem], out_specs=vmem,
        out_shape=jax.ShapeDtypeStruct((nd, r, w), F32),
        scratch_shapes=[pltpu.VMEM((nd, r, w), F32), pltpu.VMEM((r, w), F32)]
        + [pltpu.SemaphoreType.DMA((N_DEV - 1,))] * 4,
        compiler_params=pltpu.CompilerParams(vmem_limit_bytes=VMEM_LIMIT),
    )(v)


def _adam_math(w, g, m, v):
    m = ADAM_B1 * m + (1.0 - ADAM_B1) * g
    v = ADAM_B2 * v + (1.0 - ADAM_B2) * (g * g)
    m_hat = m / (1.0 - ADAM_B1 ** ADAM_STEP)
    v_hat = v / (1.0 - ADAM_B2 ** ADAM_STEP)
    delta = -ADAM_LR * (m_hat / (jnp.sqrt(v_hat) + ADAM_EPS) + ADAM_WD * w)
    return delta, m, v


def _sum_blocks(own, got, chip, name):
    _, r, c = own.shape
    tm = _tile(r, ROWS, 16)

    def kern(chip_ref, own_ref, got_ref, o_ref):
        del chip_ref
        acc = own_ref[...].astype(F32)
        for k in range(3):
            acc = acc + got_ref[k].astype(F32)
        o_ref[...] = acc

    return pl.pallas_call(
        kern, name=name,
        grid_spec=pltpu.PrefetchScalarGridSpec(
            num_scalar_prefetch=1, grid=(r // tm,),
            in_specs=[pl.BlockSpec((None, tm, c), lambda i, ch: (ch[0], i, 0)),
                      pl.BlockSpec((3, tm, c), lambda i, ch: (0, i, 0))],
            out_specs=pl.BlockSpec((tm, c), lambda i, ch: (i, 0))),
        out_shape=jax.ShapeDtypeStruct((r, c), F32),
        compiler_params=_params(("parallel",)),
    )(chip, own, got)


def _adamw(parts, w, m, v, name):
    nl, r, c = w.shape
    assert len(parts) == nl
    tm = _tile(r, 128, 8)

    def kern(*refs):
        p_refs = refs[:2 * nl]
        w_ref, m_ref, v_ref, g_ref, d_ref, mo_ref, vo_ref = refs[2 * nl:]
        layer = pl.program_id(0)
        g = p_refs[0][...] + p_refs[1][...]
        for ll in range(1, nl):
            g = jnp.where(layer == ll, p_refs[2 * ll][...] + p_refs[2 * ll + 1][...], g)
        g_ref[...] = g
        d_ref[...], mo_ref[...], vo_ref[...] = _adam_math(w_ref[...], g, m_ref[...], v_ref[...])

    def part_spec(ll):
        return pl.BlockSpec((tm, c), lambda l, i: (jnp.where(l == ll, i, 0), 0))

    spec = pl.BlockSpec((None, tm, c), lambda l, i: (l, i, 0))
    return pl.pallas_call(
        kern, name=name, grid=(nl, r // tm),
        in_specs=[part_spec(ll) for ll in range(nl) for _ in range(2)] + [spec] * 3, out_specs=[spec] * 4,
        out_shape=[jax.ShapeDtypeStruct((nl, r, c), F32)] * 4,
        compiler_params=_params(("arbitrary", "parallel")),
    )(*[p for pair in parts for p in pair], w, m, v)


def _adamw_small(g, w, m, v, name):
    def kern(g_ref, w_ref, m_ref, v_ref, d_ref, mo_ref, vo_ref):
        d_ref[...], mo_ref[...], vo_ref[...] = _adam_math(w_ref[...], g_ref[...], m_ref[...], v_ref[...])

    return pl.pallas_call(
        kern, name=name,
        out_shape=[jax.ShapeDtypeStruct(g.shape, F32)] * 3,
        compiler_params=pltpu.CompilerParams(vmem_limit_bytes=VMEM_LIMIT),
    )(g, w, m, v)


def _pack(arrays, rows):
    flat = jnp.concatenate([a.reshape(-1).astype(F32) for a in arrays])
    return jnp.pad(flat, (0, rows * 128 - flat.shape[0])).reshape(rows, 128)


def _unpack(packed, shapes):
    flat = packed.reshape(-1)
    out, off = [], 0
    for shp in shapes:
        size = math.prod(shp)
        out.append(flat[off:off + size].reshape(shp))
        off += size
    return out


_REPLICATED = ["s5_lam_re", "s5_lam_im", "s5_log_dt", "s5_b_re", "s5_b_im", "s5_c_re", "s5_c_im",
               "ffn_norm", "b_norm_mix", "kv_norm", "final_norm"]
_CHIP_VECTORS = ["s5_d", "a_norm_mix"]
_BIG = ["s5_w_glu", "ffn_w_in", "ffn_w_out", "attn_w_q", "attn_w_o", "w_kv"]
_WEIGHT_ORDER = ["s5_lam_re", "s5_lam_im", "s5_log_dt", "s5_b_re", "s5_b_im", "s5_c_re", "s5_c_im", "s5_d",
                 "s5_w_glu", "a_norm_mix", "ffn_norm", "ffn_w_in", "ffn_w_out", "b_norm_mix", "attn_w_q",
                 "attn_w_o", "kv_norm", "w_kv", "final_norm"]


def _step(x, loss_target, w, m, v):
    s, d = x.shape[1], x.shape[2]
    chip = 2 * lax.axis_index("x") + lax.axis_index("y")

    col_sharded = dict(w_glu=("s5_w_glu", 0), w_in0=("ffn_w_in", 0), w_in1=("ffn_w_in", 1), w_q=("attn_w_q", 0))
    row_sharded = dict(w_out0=("ffn_w_out", 0), w_out1=("ffn_w_out", 1), w_o=("attn_w_o", 0), w_kv=("w_kv", 0))
    names = ["w_glu", "w_in0", "w_out0", "w_kv", "w_q", "w_o", "w_in1", "w_out1"]
    local = {**col_sharded, **row_sharded}

    def layers(a):
        return a.reshape((-1,) + a.shape[-2:])

    chip_arr = jnp.reshape(chip, (1,)).astype(jnp.int32)
    vector_lands = [lax.dynamic_update_slice(jnp.zeros((N_CHIPS,) + w[n].shape, F32), w[n][None], (chip, 0, 0))
                    for n in _CHIP_VECTORS]
    v_send, v_recv, v_land, v_token = _gather_start(vector_lands, "gather_start_vectors", False,
                                                    jnp.zeros((8, 128), F32))
    first = 2
    lands = [_cast_place(layers(w[local[n][0]]), local[n][1], chip_arr, f"cast_{n}") for n in names[:first]]
    send, recv, land_thru, first_token = _gather_start(lands, "gather_start_first", True, v_token)
    lands = [_cast_place(layers(w[local[n][0]]), local[n][1], chip_arr, f"cast_{n}") for n in names[first:]]
    *later, rest_token = _gather_start(lands, "gather_start_rest", True, first_token)
    send, recv, land_thru = [list(a) + list(b) for a, b in zip((send, recv, land_thru), later)]
    vectors = _gather_wait(v_land, v_send, v_recv, jnp.zeros((8, 128), F32), "gather_wait_vectors", False)
    s5_d_full = vectors[0].reshape(1, d)
    a_norm_full = vectors[1].reshape(1, d) + rest_token[0, 0]

    batches = dict(glu=["w_glu", "w_in0", "w_out0"], attention=["w_kv", "w_q", "w_o"], ffn1=["w_in1", "w_out1"])
    forwarded, arrived = {}, {}

    def stage(batch, after):
        idx = [names.index(n) for n in batches[batch]]
        got = _gather_wait([land_thru[i] for i in idx], [send[i] for i in idx], [recv[i] for i in idx], after,
                           f"gather_wait_{batch}", True)
        f_send, f_recv, f_land, token = _forward_start(got, f"forward_start_{batch}")
        forwarded.update(zip(batches[batch], zip(f_send, f_recv, f_land)))
        return token[0, 0]

    def weight(name, after):
        if name not in arrived:
            f_send, f_recv, f_land = forwarded[name]
            land = _forward_wait(f_land, f_send, f_recv, after, f"forward_wait_{name}")
            arrived[name] = land if name in col_sharded else land.reshape(1, -1, land.shape[-1])
        return arrived[name]

    started = {}

    def emit(name, dw):
        outs = _scatter_start(dw.reshape(N_CHIPS, -1, dw.shape[-1]), f"scatter_start_{name}")
        started[name] = outs[:4]
        return outs[4][0, 0]

    small = dict(lam_re=w["s5_lam_re"][0], lam_im=w["s5_lam_im"][0], log_dt=w["s5_log_dt"][0],
                 b_re=w["s5_b_re"][0], b_im=w["s5_b_im"][0], c_re=w["s5_c_re"][0], c_im=w["s5_c_im"][0],
                 s5_d=s5_d_full, a_norm=a_norm_full, ffn_norm0=w["ffn_norm"][0:1], ffn_norm1=w["ffn_norm"][1:2],
                 b_norm=w["b_norm_mix"], kv_norm=w["kv_norm"].reshape(1, d), final_norm=w["final_norm"].reshape(1, d))

    loss_rows, grad_x, d_small = _local_step(x[0], loss_target[0], small, weight, emit, stage)

    g4, got = _scatter_wait([started[n] for n in names], grad_x)
    partial = [_sum_blocks(o, r, chip_arr, f"sum_{n}") for n, o, r in zip(names, g4, got)]
    swap_send, swap_recv, partial, swap_land, swap_token = _swap_start(partial)
    result = {}

    rep_grads = [d_small["lam_re"], d_small["lam_im"], d_small["log_dt"], d_small["b_re"], d_small["b_im"],
                 d_small["c_re"], d_small["c_im"],
                 jnp.concatenate([d_small["ffn_norm0"], d_small["ffn_norm1"]], axis=0),
                 d_small["b_norm"], d_small["kv_norm"], d_small["final_norm"]]
    vec_grads = [d_small["s5_d"], d_small["a_norm"]]
    to_reduce = rep_grads + vec_grads + [jnp.sum(loss_rows).reshape(1) + swap_token[0, 0]]
    total = sum(math.prod(a.shape) for a in to_reduce)
    rows_per = -(-total // (N_DEV * 128 * 8)) * 8
    reduced = _all_reduce_small(_pack(to_reduce, N_DEV * rows_per).reshape(N_DEV, rows_per, 128))
    red = _unpack(reduced, [a.shape for a in to_reduce])
    loss = red[-1][0]
    g_small = dict(zip(_REPLICATED, [r.reshape(w[n].shape) for r, n in zip(red[:len(rep_grads)], _REPLICATED)]))
    for n, r in zip(_CHIP_VECTORS, red[len(rep_grads):-1]):
        g_small[n] = lax.dynamic_slice_in_dim(r.reshape(1, d), chip * (d // N_CHIPS), d // N_CHIPS, axis=1)
    small_names = _REPLICATED + _CHIP_VECTORS
    n_small = sum(math.prod(w[n].shape) for n in small_names)
    rows_small = -(-n_small // (128 * 8)) * 8
    packed = [_pack([src[n] for n in small_names], rows_small) for src in (g_small, w, m, v)]
    upd = _adamw_small(*packed, "adamw_small")
    shapes = [w[n].shape for n in small_names]
    for n, dl, mo, vo in zip(small_names, *[_unpack(u, shapes) for u in upd]):
        result[n] = [g_small[n], dl, mo, vo]

    partial, other = _swap_wait(partial, swap_land, swap_send, swap_recv, upd[0])
    part = dict(zip(names, zip(partial, other)))
    for name in _BIG:
        parts = [part[n] for n in sorted((n for n in names if local[n][0] == name), key=lambda n: local[n][1])]
        outs = _adamw(parts, layers(w[name]), layers(m[name]), layers(v[name]), f"adamw_{name}")
        result[name] = [o.reshape(w[name].shape) for o in outs]

    cols = [[result[n][t] for n in _WEIGHT_ORDER] for t in range(4)]
    return (loss, grad_x.reshape(x.shape), *cols[0], *cols[1], *cols[2], *cols[3])


def kernel(x, s5_lam_re, s5_lam_im, s5_log_dt, s5_b_re, s5_b_im, s5_c_re, s5_c_im, s5_d, s5_w_glu, a_norm_mix, ffn_norm, ffn_w_in, ffn_w_out, b_norm_mix, attn_w_q, attn_w_o, kv_norm, w_kv, final_norm, loss_target, m_s5_lam_re, m_s5_lam_im, m_s5_log_dt, m_s5_b_re, m_s5_b_im, m_s5_c_re, m_s5_c_im, m_s5_d, m_s5_w_glu, m_a_norm_mix, m_ffn_norm, m_ffn_w_in, m_ffn_w_out, m_b_norm_mix, m_attn_w_q, m_attn_w_o, m_kv_norm, m_w_kv, m_final_norm, v_s5_lam_re, v_s5_lam_im, v_s5_log_dt, v_s5_b_re, v_s5_b_im, v_s5_c_re, v_s5_c_im, v_s5_d, v_s5_w_glu, v_a_norm_mix, v_ffn_norm, v_ffn_w_in, v_ffn_w_out, v_b_norm_mix, v_attn_w_q, v_attn_w_o, v_kv_norm, v_w_kv, v_final_norm):
    w = dict(zip(_WEIGHT_ORDER, (s5_lam_re, s5_lam_im, s5_log_dt, s5_b_re, s5_b_im, s5_c_re, s5_c_im, s5_d, s5_w_glu, a_norm_mix, ffn_norm, ffn_w_in, ffn_w_out, b_norm_mix, attn_w_q, attn_w_o, kv_norm, w_kv, final_norm)))
    m = dict(zip(_WEIGHT_ORDER, (m_s5_lam_re, m_s5_lam_im, m_s5_log_dt, m_s5_b_re, m_s5_b_im, m_s5_c_re, m_s5_c_im, m_s5_d, m_s5_w_glu, m_a_norm_mix, m_ffn_norm, m_ffn_w_in, m_ffn_w_out, m_b_norm_mix, m_attn_w_q, m_attn_w_o, m_kv_norm, m_w_kv, m_final_norm)))
    v = dict(zip(_WEIGHT_ORDER, (v_s5_lam_re, v_s5_lam_im, v_s5_log_dt, v_s5_b_re, v_s5_b_im, v_s5_c_re, v_s5_c_im, v_s5_d, v_s5_w_glu, v_a_norm_mix, v_ffn_norm, v_ffn_w_in, v_ffn_w_out, v_b_norm_mix, v_attn_w_q, v_attn_w_o, v_kv_norm, v_w_kv, v_final_norm)))
    return _step(x, loss_target, w, m, v)
```

```python
import functools
import math

import jax
import jax.numpy as jnp
from jax import lax
from jax.experimental import pallas as pl
from jax.experimental.pallas import tpu as pltpu

F32 = jnp.float32
BF16 = jnp.bfloat16

S5_GROUP_CH = 16
S5_STATE = 64
GROUPS_PER_TILE = 8
HEAD_DIM = 128
N_KV_HEADS = 4
PATTERNS = ((128, 1), (512, 4), (2048, 16))
ATT_BLK = 128
EPS = 1e-6
NEG_INF = -1e30
SCAN_T = 2048
ADAM_LR = 0.001
ADAM_B1 = 0.9
ADAM_B2 = 0.999
ADAM_EPS = 1e-08
ADAM_WD = 0.01
ADAM_STEP = 10
N_CHIPS = 4
N_DEV = 8
VMEM_LIMIT = 56 * 1024 * 1024
MESH = pl.DeviceIdType.MESH
GELU_K = math.sqrt(2.0 / math.pi)
GELU_C = 0.044715


def _tile(n, pref, unit=128):
    if n <= pref:
        return n
    best = None
    t = unit
    while t <= pref:
        if n % t == 0:
            best = t
        t += unit
    assert best is not None, (n, pref, unit)
    return best


def _params(sem):
    return pltpu.CompilerParams(dimension_semantics=sem, vmem_limit_bytes=VMEM_LIMIT)


def _dot(a, b):
    return jnp.dot(a, b, preferred_element_type=F32)


def _dot_nt(a, b):
    return lax.dot_general(a, b, (((1,), (1,)), ((), ())), preferred_element_type=F32)


def _dot_tn(a, b):
    return lax.dot_general(a, b, (((0,), (0,)), ((), ())), preferred_element_type=F32)


def _mm_nn(a, w, out_dtype, name, tm=512, tn=1536, tk=2048):
    m, k = a.shape
    nb, k2, nq = w.shape
    assert k == k2
    tm, tn, tk = _tile(m, tm, 8), _tile(nq, tn), _tile(k, tk)
    per, nk = nq // tn, k // tk

    def kern(a_ref, w_ref, o_ref, *acc):
        p = _dot(a_ref[...], w_ref[...])
        if nk == 1:
            o_ref[...] = p.astype(o_ref.dtype)
        else:
            acc_ref, = acc
            kk = pl.program_id(2)

            @pl.when(kk == 0)
            def _():
                acc_ref[...] = p

            @pl.when(kk > 0)
            def _():
                acc_ref[...] += p

            @pl.when(kk == nk - 1)
            def _():
                o_ref[...] = acc_ref[...].astype(o_ref.dtype)

    return pl.pallas_call(
        kern, name=name, grid=(nb * per, m // tm, nk),
        in_specs=[pl.BlockSpec((tm, tk), lambda j, i, kk: (i, kk)),
                  pl.BlockSpec((None, tk, tn), lambda j, i, kk: (j // per, kk, j % per))],
        out_specs=pl.BlockSpec((tm, tn), lambda j, i, kk: (i, j)),
        out_shape=jax.ShapeDtypeStruct((m, nb * nq), out_dtype),
        scratch_shapes=[] if nk == 1 else [pltpu.VMEM((tm, tn), F32)],
        compiler_params=_params(("parallel", "parallel", "arbitrary")),
    )(a, w)


def _paired_block(r, per, nb):
    j = r // 2
    return (r % 2) * (nb // 2) + j // per, j % per


def _mm_nt(a, w, name, tm=512, tr=1536, tkc=1024, paired=False):
    m, n = a.shape
    nb, k, nq = w.shape
    assert n == nb * nq
    tm, tr, tkc = _tile(m, tm, 8), _tile(nq, tr), _tile(k, tkc)
    per = nq // tr

    def kern(a_ref, w_ref, o_ref):
        acc = None
        for r in range(nb * per):
            blk, tile = _paired_block(r, per, nb) if paired else (r // per, r % per)
            p = _dot_nt(a_ref[:, r * tr:(r + 1) * tr], w_ref[blk, :, tile * tr:(tile + 1) * tr])
            acc = p if acc is None else acc + p
        o_ref[...] = acc

    return pl.pallas_call(
        kern, name=name, grid=(k // tkc, m // tm),
        in_specs=[pl.BlockSpec((tm, n), lambda kc, i: (i, 0)),
                  pl.BlockSpec((nb, tkc, nq), lambda kc, i: (0, kc, 0))],
        out_specs=pl.BlockSpec((tm, tkc), lambda kc, i: (i, kc)),
        out_shape=jax.ShapeDtypeStruct((m, k), F32),
        compiler_params=_params(("parallel", "parallel")),
    )(a, w)


def _mm_tn(a, dy, nb, name, tkk=512, tn=1536, paired=False):
    s, k = a.shape
    s2, n = dy.shape
    assert s == s2 and n % nb == 0
    nq = n // nb
    tkk, tn = _tile(k, tkk), _tile(nq, tn)
    per = nq // tn

    def w_block(j):
        return _paired_block(j, per, nb) if paired else (j // per, j % per)

    def kern(a_ref, dy_ref, o_ref):
        o_ref[...] = _dot_tn(a_ref[...], dy_ref[...]).astype(o_ref.dtype)

    return pl.pallas_call(
        kern, name=name, grid=(nb * per, k // tkk),
        in_specs=[pl.BlockSpec((s, tkk), lambda j, kk: (0, kk)),
                  pl.BlockSpec((s, tn), lambda j, kk: (0, j))],
        out_specs=pl.BlockSpec((None, tkk, tn), lambda j, kk: (w_block(j)[0], kk, w_block(j)[1])),
        out_shape=jax.ShapeDtypeStruct((nb, k, nq), BF16),
        compiler_params=_params(("parallel", "parallel")),
    )(a, dy)


ROWS = 256


def _rms(x, g):
    r = lax.rsqrt(jnp.mean(x * x, axis=-1, keepdims=True) + EPS)
    return x * r * g


def _rms_bwd(x, g, dh):
    r = lax.rsqrt(jnp.mean(x * x, axis=-1, keepdims=True) + EPS)
    xh = x * r
    dgx = dh * g
    dx = r * (dgx - xh * jnp.mean(dgx * xh, axis=-1, keepdims=True))
    return dx, dh * xh


def _sigmoid(x):
    return 1.0 / (1.0 + jnp.exp(-x))


def _gelu(y):
    return 0.5 * y * (1.0 + jnp.tanh(GELU_K * (y + GELU_C * y * y * y)))


def _gelu_grad(y):
    t = jnp.tanh(GELU_K * (y + GELU_C * y * y * y))
    return 0.5 * (1.0 + t) + 0.5 * y * (1.0 - t * t) * GELU_K * (1.0 + 3.0 * GELU_C * y * y)


def _row_spec(tm, d, col=0):
    return pl.BlockSpec((tm, d), lambda i: (i, col))


def _vec_spec(d):
    return pl.BlockSpec((1, d), lambda i: (0, 0))


def _acc_rows(ref, val, i):
    s = jnp.sum(val, axis=0, keepdims=True)

    @pl.when(i == 0)
    def _():
        ref[...] = s

    @pl.when(i > 0)
    def _():
        ref[...] += s


def _norm_f32(x, g, name):
    s, d = x.shape
    tm = _tile(s, ROWS, 8)

    def kern(x_ref, g_ref, h_ref):
        h_ref[...] = _rms(x_ref[...], g_ref[...])

    return pl.pallas_call(
        kern, name=name, grid=(s // tm,),
        in_specs=[_row_spec(tm, d), _vec_spec(d)],
        out_specs=_row_spec(tm, d),
        out_shape=jax.ShapeDtypeStruct((s, d), F32),
        compiler_params=_params(("parallel",)),
    )(x, g)


def _glu_res_norm(vg, x, g, name):
    s, d = x.shape
    tm = _tile(s, ROWS, 16)

    def kern(val_ref, gate_ref, x_ref, g_ref, x1_ref, hf_ref):
        x1 = x_ref[...] + val_ref[...] * _sigmoid(gate_ref[...])
        x1_ref[...] = x1
        hf_ref[...] = _rms(x1, g_ref[...]).astype(BF16)

    return pl.pallas_call(
        kern, name=name, grid=(s // tm,),
        in_specs=[_row_spec(tm, d, 0), _row_spec(tm, d, 1), _row_spec(tm, d), _vec_spec(d)],
        out_specs=[_row_spec(tm, d), _row_spec(tm, d)],
        out_shape=[jax.ShapeDtypeStruct((s, d), F32), jax.ShapeDtypeStruct((s, d), BF16)],
        compiler_params=_params(("parallel",)),
    )(vg, vg, x, g)


FFN_TILE = 1408


def _ffn_in_act(hf, w_in, name, tm=512):
    s, k = hf.shape
    nb, _, nq = w_in.shape
    tm, tn = _tile(s, tm, 16), _tile(nq, FFN_TILE)
    per = nq // tn
    nf = (nb // 2) * per

    def kern(h_ref, wg_ref, wu_ref, a_ref, gu_ref):
        h = h_ref[...]
        g = _dot(h, wg_ref[...])
        u = _dot(h, wu_ref[...])
        sg = _sigmoid(g)
        silu = g * sg
        a_ref[...] = (silu * u).astype(BF16)
        gu_ref[:, :tn] = (u * (sg + silu * (1.0 - sg))).astype(BF16)
        gu_ref[:, tn:] = silu.astype(BF16)

    return pl.pallas_call(
        kern, name=name, grid=(nf, s // tm),
        in_specs=[pl.BlockSpec((tm, k), lambda j, i: (i, 0)),
                  pl.BlockSpec((None, k, tn), lambda j, i: (j // per, 0, j % per)),
                  pl.BlockSpec((None, k, tn), lambda j, i: (nb // 2 + j // per, 0, j % per))],
        out_specs=[pl.BlockSpec((tm, tn), lambda j, i: (i, j)),
                   pl.BlockSpec((tm, 2 * tn), lambda j, i: (i, j))],
        out_shape=[jax.ShapeDtypeStruct((s, nf * tn), BF16), jax.ShapeDtypeStruct((s, 2 * nf * tn), BF16)],
        compiler_params=_params(("parallel", "parallel")),
    )(hf, w_in, w_in)


def _ffn_out_dx_act(dresb, w_out, gu, name, tm=512):
    s, d = dresb.shape
    f = w_out.shape[1]
    tm = _tile(s, tm, 16)
    tn = _tile(f // 2, FFN_TILE)

    def kern(d_ref, w_ref, gu_ref, o_ref):
        da = _dot_nt(d_ref[...], w_ref[...])
        o_ref[:, :tn] = (da * gu_ref[:, :tn].astype(F32)).astype(BF16)
        o_ref[:, tn:] = (da * gu_ref[:, tn:].astype(F32)).astype(BF16)

    pair = pl.BlockSpec((tm, 2 * tn), lambda j, i: (i, j))
    return pl.pallas_call(
        kern, name=name, grid=(f // tn, s // tm),
        in_specs=[pl.BlockSpec((tm, d), lambda j, i: (i, 0)),
                  pl.BlockSpec((None, tn, d), lambda j, i: (0, j, 0)), pair],
        out_specs=pair,
        out_shape=jax.ShapeDtypeStruct((s, 2 * f), BF16),
        compiler_params=_params(("parallel", "parallel")),
    )(dresb, w_out, gu)


def _mm_res_norm(a, w, x, g, name, tm=512):
    m, k = a.shape
    d = w.shape[2]
    tm = _tile(m, tm, 16)

    def kern(a_ref, w_ref, x_ref, g_ref, xn_ref, hf_ref):
        xn = x_ref[...] + _dot(a_ref[...], w_ref[0])
        xn_ref[...] = xn
        hf_ref[...] = _rms(xn, g_ref[...]).astype(BF16)

    return pl.pallas_call(
        kern, name=name, grid=(m // tm,),
        in_specs=[_row_spec(tm, k), pl.BlockSpec((1, k, d), lambda i: (0, 0, 0)), _row_spec(tm, d), _vec_spec(d)],
        out_specs=[_row_spec(tm, d), _row_spec(tm, d)],
        out_shape=[jax.ShapeDtypeStruct((m, d), F32), jax.ShapeDtypeStruct((m, d), BF16)],
        compiler_params=_params(("parallel",)),
    )(a, w, x, g)


def _loss_head(o, x, g, target, name):
    s, d = x.shape
    tm = _tile(s, ROWS, 16)

    def kern(o_ref, x_ref, g_ref, t_ref, dx_ref, dxb_ref, loss_ref, dg_ref):
        i = pl.program_id(0)
        x4 = x_ref[...] + o_ref[...]
        gg = g_ref[...]
        diff = _rms(x4, gg) - t_ref[...]
        dx, dgr = _rms_bwd(x4, gg, diff * (1.0 / d))
        dx_ref[...] = dx
        dxb_ref[...] = dx.astype(BF16)
        _acc_rows(loss_ref, diff * diff * (0.5 / d), i)
        _acc_rows(dg_ref, dgr, i)

    return pl.pallas_call(
        kern, name=name, grid=(s // tm,),
        in_specs=[_row_spec(tm, d), _row_spec(tm, d), _vec_spec(d), _row_spec(tm, d)],
        out_specs=[_row_spec(tm, d), _row_spec(tm, d), _vec_spec(d), _vec_spec(d)],
        out_shape=[jax.ShapeDtypeStruct((s, d), F32), jax.ShapeDtypeStruct((s, d), BF16),
                   jax.ShapeDtypeStruct((1, d), F32), jax.ShapeDtypeStruct((1, d), F32)],
        compiler_params=_params(("arbitrary",)),
    )(o, x, g, target)


def _norm_bwd(x, gains, dhs, dres, name, with_bf16=False):
    s, d = x.shape
    tm = _tile(s, ROWS, 16)
    ng = len(gains)
    nd = 2 if with_bf16 else 1

    def kern(x_ref, dres_ref, *refs):
        i = pl.program_id(0)
        x_ = x_ref[...]
        acc = dres_ref[...]
        for t in range(ng):
            dx, dgr = _rms_bwd(x_, refs[t][...], refs[ng + t][...])
            acc = acc + dx
            _acc_rows(refs[2 * ng + nd + t], dgr, i)
        refs[2 * ng][...] = acc
        if with_bf16:
            refs[2 * ng + 1][...] = acc.astype(BF16)

    return pl.pallas_call(
        kern, name=name, grid=(s // tm,),
        in_specs=[_row_spec(tm, d), _row_spec(tm, d)] + [_vec_spec(d)] * ng + [_row_spec(tm, d)] * ng,
        out_specs=[_row_spec(tm, d)] * nd + [_vec_spec(d)] * ng,
        out_shape=[jax.ShapeDtypeStruct((s, d), F32), jax.ShapeDtypeStruct((s, d), BF16)][:nd]
        + [jax.ShapeDtypeStruct((1, d), F32)] * ng,
        compiler_params=_params(("arbitrary",)),
    )(x, dres, *gains, *dhs)


def _glu_bwd(dmix, vg, name):
    s, d = dmix.shape
    tm = _tile(s, ROWS, 16)

    def kern(dm_ref, val_ref, gate_ref, o_ref):
        dm = dm_ref[...]
        sg = _sigmoid(gate_ref[...])
        o_ref[:, :d] = (dm * sg).astype(BF16)
        o_ref[:, d:] = (dm * val_ref[...] * sg * (1.0 - sg)).astype(BF16)

    return pl.pallas_call(
        kern, name=name, grid=(s // tm,),
        in_specs=[_row_spec(tm, d), _row_spec(tm, d, 0), _row_spec(tm, d, 1)],
        out_specs=_row_spec(tm, 2 * d),
        out_shape=jax.ShapeDtypeStruct((s, 2 * d), BF16),
        compiler_params=_params(("parallel",)),
    )(dmix, vg, vg)


SUB = 8


def _local_scan(vr, vi, pre_ref, pim_ref, reverse):
    sub = lax.broadcasted_iota(jnp.int32, vr.shape, 1)
    sign = -1.0 if reverse else 1.0
    for sh in (1, 2, 4):
        ar = pre_ref[sh - 1:sh, :][None]
        ai = sign * pim_ref[sh - 1:sh, :][None]
        keep = sub < SUB - sh if reverse else sub >= sh
        sr = jnp.where(keep, pltpu.roll(vr, SUB - sh if reverse else sh, 1), 0.0)
        si = jnp.where(keep, pltpu.roll(vi, SUB - sh if reverse else sh, 1), 0.0)
        vr, vi = vr + ar * sr - ai * si, vi + ar * si + ai * sr
    return vr, vi


def _s5_fwd(u, bbd_re, bbd_im, cbd_re, cbd_im, pw_re, pw_im, dskip, name):
    s, d = u.shape
    nt, cw, lw = bbd_re.shape
    t = _tile(s, SCAN_T, SUB)
    nc = s // t
    ng = t // SUB

    def kern(u_ref, bre_ref, bim_ref, cre_ref, cim_ref, pre_ref, pim_ref, d_ref,
             xr_ref, xi_ref, y_ref, z_ref, car_ref, cai_ref):
        c = pl.program_id(1)

        @pl.when(c == 0)
        def _():
            car_ref[...] = jnp.zeros_like(car_ref)
            cai_ref[...] = jnp.zeros_like(cai_ref)

        u_ = u_ref[...]
        ub = u_.astype(BF16)
        vr, vi = _local_scan(_dot(ub, bre_ref[...]).reshape(ng, SUB, lw),
                             _dot(ub, bim_ref[...]).reshape(ng, SUB, lw), pre_ref, pim_ref, False)
        cr = car_ref[...]
        ci = cai_ref[...]
        pr = pre_ref[...]
        pi = pim_ref[...]
        for gidx in range(ng):
            rows = slice(gidx * SUB, (gidx + 1) * SUB)
            gr = vr[gidx] + pr * cr - pi * ci
            gi = vi[gidx] + pr * ci + pi * cr
            xr_ref[rows, :] = gr
            xi_ref[rows, :] = gi
            cr, ci = gr[SUB - 1:SUB, :], gi[SUB - 1:SUB, :]
        car_ref[...] = cr
        cai_ref[...] = ci
        y = (_dot(xr_ref[...].astype(BF16), cre_ref[...]) - _dot(xi_ref[...].astype(BF16), cim_ref[...])
             + d_ref[...] * u_)
        y_ref[...] = y
        z_ref[...] = _gelu(y).astype(BF16)

    tok = pl.BlockSpec((t, cw), lambda j, c: (c, j))
    st = pl.BlockSpec((t, lw), lambda j, c: (c, j))
    return pl.pallas_call(
        kern, name=name, grid=(nt, nc),
        in_specs=[tok,
                  pl.BlockSpec((None, cw, lw), lambda j, c: (j, 0, 0)),
                  pl.BlockSpec((None, cw, lw), lambda j, c: (j, 0, 0)),
                  pl.BlockSpec((None, lw, cw), lambda j, c: (j, 0, 0)),
                  pl.BlockSpec((None, lw, cw), lambda j, c: (j, 0, 0)),
                  pl.BlockSpec((SUB, lw), lambda j, c: (0, j)),
                  pl.BlockSpec((SUB, lw), lambda j, c: (0, j)),
                  pl.BlockSpec((1, cw), lambda j, c: (0, j))],
        out_specs=[st, st, tok, tok],
        out_shape=[jax.ShapeDtypeStruct((s, nt * lw), F32), jax.ShapeDtypeStruct((s, nt * lw), F32),
                   jax.ShapeDtypeStruct((s, d), F32), jax.ShapeDtypeStruct((s, d), BF16)],
        scratch_shapes=[pltpu.VMEM((1, lw), F32), pltpu.VMEM((1, lw), F32)],
        compiler_params=_params(("parallel", "arbitrary")),
    )(u, bbd_re, bbd_im, cbd_re, cbd_im, pw_re, pw_im, dskip)


def _s5_bwd(dz, y, u, xs_re, xs_im, bbd_re, bbd_im, cbd_re, cbd_im, pw_re, pw_im, pf_re, pf_im, dskip, name):
    s, d = u.shape
    nt, cw, lw = bbd_re.shape
    t = _tile(s, SCAN_T, SUB)
    nc = s // t
    ng = t // SUB

    def kern(dz_ref, y_ref, u_ref, xr_ref, xi_ref, bre_ref, bim_ref, cre_ref, cim_ref,
             pre_ref, pim_ref, fre_ref, fim_ref, d_ref,
             du_ref, dd_ref, dcr_ref, dci_ref, dbr_ref, dbi_ref, dar_ref, dai_ref,
             car_ref, cai_ref, gr_ref, gi_ref):
        c = pl.program_id(1)

        @pl.when(c == 0)
        def _():
            car_ref[...] = jnp.zeros_like(car_ref)
            cai_ref[...] = jnp.zeros_like(cai_ref)

        u_ = u_ref[...]
        ub = u_.astype(BF16)
        dy = dz_ref[...] * _gelu_grad(y_ref[...])
        dyb = dy.astype(BF16)
        vr, vi = _local_scan(_dot_nt(dyb, cre_ref[...]).reshape(ng, SUB, lw),
                             (-_dot_nt(dyb, cim_ref[...])).reshape(ng, SUB, lw), pre_ref, pim_ref, True)
        later_r = car_ref[...]
        later_i = cai_ref[...]
        cr, ci = later_r, later_i
        fr = fre_ref[...]
        fi = fim_ref[...]
        for gidx in reversed(range(ng)):
            rows = slice(gidx * SUB, (gidx + 1) * SUB)
            ar = vr[gidx] + fr * cr + fi * ci
            ai = vi[gidx] + fr * ci - fi * cr
            gr_ref[rows, :] = ar
            gi_ref[rows, :] = ai
            cr, ci = ar[0:1, :], ai[0:1, :]
        car_ref[...] = cr
        cai_ref[...] = ci
        gr = gr_ref[...]
        gi = gi_ref[...]
        rows = lax.broadcasted_iota(jnp.int32, (t, lw), 0)
        gsr = jnp.where(rows < t - 1, pltpu.roll(gr, t - 1, 0), later_r)
        gsi = jnp.where(rows < t - 1, pltpu.roll(gi, t - 1, 0), later_i)
        xr = xr_ref[...]
        xi = xi_ref[...]
        dar = jnp.sum(gsr * xr + gsi * xi, axis=0, keepdims=True)
        dai = jnp.sum(gsi * xr - gsr * xi, axis=0, keepdims=True)
        grb = gr.astype(BF16)
        gib = gi.astype(BF16)
        dbr = _dot_tn(ub, grb)
        dbi = _dot_tn(ub, gib)
        dcr = _dot_tn(dyb, xr.astype(BF16))
        dci = _dot_tn(dyb, xi.astype(BF16))
        du_ref[...] = dy * d_ref[...] + _dot_nt(grb, bre_ref[...]) + _dot_nt(gib, bim_ref[...])
        ddv = jnp.sum(dy * u_, axis=0, keepdims=True)

        @pl.when(c == 0)
        def _():
            dd_ref[...] = ddv
            dcr_ref[...] = dcr
            dci_ref[...] = dci
            dbr_ref[...] = dbr
            dbi_ref[...] = dbi
            dar_ref[...] = dar
            dai_ref[...] = dai

        @pl.when(c > 0)
        def _():
            dd_ref[...] += ddv
            dcr_ref[...] += dcr
            dci_ref[...] += dci
            dbr_ref[...] += dbr
            dbi_ref[...] += dbi
            dar_ref[...] += dar
            dai_ref[...] += dai

    tok = pl.BlockSpec((t, cw), lambda j, c: (nc - 1 - c, j))
    st = pl.BlockSpec((t, lw), lambda j, c: (nc - 1 - c, j))
    wb = pl.BlockSpec((None, cw, lw), lambda j, c: (j, 0, 0))
    wc = pl.BlockSpec((None, lw, cw), lambda j, c: (j, 0, 0))
    pw = pl.BlockSpec((SUB, lw), lambda j, c: (0, j))
    vec_c = pl.BlockSpec((1, cw), lambda j, c: (0, j))
    vec_l = pl.BlockSpec((1, lw), lambda j, c: (0, j))
    return pl.pallas_call(
        kern, name=name, grid=(nt, nc),
        in_specs=[tok, tok, tok, st, st, wb, wb, wc, wc, pw, pw, pw, pw, vec_c],
        out_specs=[tok, vec_c, wb, wb, wb, wb, vec_l, vec_l],
        out_shape=[jax.ShapeDtypeStruct((s, d), F32), jax.ShapeDtypeStruct((1, d), F32)]
        + [jax.ShapeDtypeStruct((nt, cw, lw), F32)] * 4
        + [jax.ShapeDtypeStruct((1, nt * lw), F32)] * 2,
        scratch_shapes=[pltpu.VMEM((1, lw), F32), pltpu.VMEM((1, lw), F32),
                        pltpu.VMEM((t, lw), F32), pltpu.VMEM((t, lw), F32)],
        compiler_params=_params(("parallel", "arbitrary")),
    )(dz, y, u, xs_re, xs_im, bbd_re, bbd_im, cbd_re, cbd_im, pw_re, pw_im, pf_re, pf_im, dskip)


def _s5_discretize(lam_re, lam_im, log_dt, b_re, b_im):
    dt = jnp.exp(log_dt)[:, None]
    mag = jnp.exp(lam_re * dt)
    ang = lam_im * dt
    lb_re = mag * jnp.cos(ang)
    lb_im = mag * jnp.sin(ang)
    nr = lb_re - 1.0
    den = lam_re * lam_re + lam_im * lam_im
    f_re = (nr * lam_re + lb_im * lam_im) / den
    f_im = (lb_im * lam_re - nr * lam_im) / den
    bb_re = f_re[..., None] * b_re - f_im[..., None] * b_im
    bb_im = f_re[..., None] * b_im + f_im[..., None] * b_re
    return lb_re, lb_im, bb_re, bb_im


def _block_diag(w):
    nt, ng, a, b = w.shape
    eye = jnp.eye(ng, dtype=w.dtype)
    return (w[:, :, :, None, :] * eye[None, :, None, :, None]).reshape(nt, ng * a, ng * b)


def _block_diag_take(w, a, b):
    nt = w.shape[0]
    ng = GROUPS_PER_TILE
    w5 = w.reshape(nt, ng, a, ng, b)
    on_diagonal = jnp.eye(ng, dtype=bool)[None, :, None, :, None]
    return jnp.sum(jnp.where(on_diagonal, w5, 0.0), axis=3).reshape(nt * ng, a, b)


def _att_combine(outs, lses, name):
    s, d = outs[0].shape
    tm = _tile(s, ROWS, 16)
    ng = len(outs)

    def kern(*refs):
        for h in range(d // HEAD_DIM):
            cols = slice(h * HEAD_DIM, (h + 1) * HEAD_DIM)
            ls = [refs[ng + t][:, h:h + 1] for t in range(ng)]
            mx = functools.reduce(jnp.maximum, ls)
            es = [jnp.exp(l - mx) for l in ls]
            den = functools.reduce(lambda a, b: a + b, es)
            o = functools.reduce(lambda a, b: a + b,
                                 [es[t] / den * refs[t][:, cols].astype(F32) for t in range(ng)])
            refs[2 * ng][:, cols] = o
            refs[2 * ng + 1][:, cols] = o.astype(BF16)

    return pl.pallas_call(
        kern, name=name, grid=(s // tm,),
        in_specs=[_row_spec(tm, d)] * ng + [_row_spec(tm, d // HEAD_DIM)] * ng,
        out_specs=[_row_spec(tm, d)] * 2,
        out_shape=[jax.ShapeDtypeStruct((s, d), F32), jax.ShapeDtypeStruct((s, d), BF16)],
        compiler_params=_params(("parallel",)),
    )(*outs, *lses)


DIL_MAX = 16


def _slab(r):
    return 4 * (r % 4) + r // 4


def _to_slabs(xs, name):
    s, w = xs[0].shape
    n = s // DIL_MAX
    nx = len(xs)

    def kern(*refs):
        o_ref = refs[nx]
        for r in range(DIL_MAX):
            rows = [x_ref[pl.ds(r, n, stride=DIL_MAX), :] for x_ref in refs[:nx]]
            o_ref[_slab(r) * n:(_slab(r) + 1) * n, :] = functools.reduce(lambda a, b: a + b, rows)

    spec = pl.BlockSpec((s, 128), lambda i: (0, i))
    return pl.pallas_call(
        kern, name=name, grid=(w // 128,), in_specs=[spec] * nx, out_specs=spec,
        out_shape=jax.ShapeDtypeStruct((s, w), F32), compiler_params=_params(("parallel",)),
    )(*xs)


def _from_slabs(x, name):
    s, w = x.shape
    n = s // DIL_MAX

    def kern(x_ref, o_ref, ob_ref):
        for r in range(DIL_MAX):
            o_ref[pl.ds(r, n, stride=DIL_MAX), :] = x_ref[_slab(r) * n:(_slab(r) + 1) * n, :]
        ob_ref[...] = o_ref[...].astype(BF16)

    spec = pl.BlockSpec((s, 128), lambda i: (0, i))
    return pl.pallas_call(
        kern, name=name, grid=(w // 128,), in_specs=[spec], out_specs=[spec, spec],
        out_shape=[jax.ShapeDtypeStruct((s, w), F32), jax.ShapeDtypeStruct((s, w), BF16)],
        compiler_params=_params(("parallel",)),
    )(x)


def _norm_bf16(x, gains, name):
    s, d = x.shape
    tm = _tile(s, ROWS, 16)
    ng = len(gains)

    def kern(x_ref, *refs):
        x_ = x_ref[...]
        for t in range(ng):
            refs[ng + t][...] = _rms(x_, refs[t][...]).astype(BF16)

    return pl.pallas_call(
        kern, name=name, grid=(s // tm,),
        in_specs=[_row_spec(tm, d)] + [_vec_spec(d)] * ng, out_specs=[_row_spec(tm, d)] * ng,
        out_shape=[jax.ShapeDtypeStruct((s, d), BF16)] * ng,
        compiler_params=_params(("parallel",)),
    )(x, *gains)


def _att_geometry(dil, s):
    g = DIL_MAX // dil
    return s // DIL_MAX, g, max(ATT_BLK // g, 16)


def _att_mask(g, j, jb, rep):
    b = g * j

    def pos(i):
        sl, jj = i // j, i % j
        off = {1: 0, 4: sl, 16: sl // 4 + 4 * (sl % 4)}[g]
        return g * jj + off

    qi = lax.broadcasted_iota(jnp.int32, (rep * b, 2 * b), 0) % b
    ki = lax.broadcasted_iota(jnp.int32, (rep * b, 2 * b), 1)
    prev = ki < b
    dist = pos(qi) - pos(ki % b) + jnp.where(prev, b, 0)
    return (dist >= 0) & (dist <= ATT_BLK) & (jnp.logical_not(prev) | (jb > 0))


def _stack_heads(x, rep):
    return jnp.concatenate([x[:, h * HEAD_DIM:(h + 1) * HEAD_DIM] for h in range(rep)], axis=0)


def _att_fwd(q, kv, gi, dil, nh, name):
    s = q.shape[0]
    n, g, j = _att_geometry(dil, s)
    b = g * j
    ng = q.shape[1] // (nh * HEAD_DIM)
    rep = nh // N_KV_HEADS
    qw = rep * HEAD_DIM
    scale = HEAD_DIM ** -0.5
    q3 = q.reshape(DIL_MAX, n, q.shape[1])
    kv3 = kv.reshape(DIL_MAX, n, kv.shape[1])

    kw = N_KV_HEADS * HEAD_DIM

    def kern(q_ref, kc_ref, kp_ref, vc_ref, vp_ref, o_ref, l_ref, lw_ref):
        jb = pl.program_id(1)
        mask = _att_mask(g, j, jb, rep)
        for kh in range(N_KV_HEADS):
            hs = slice(kh * HEAD_DIM, (kh + 1) * HEAD_DIM)
            kc, kp, vc, vp = [r[:, :, hs].reshape(b, HEAD_DIM) for r in (kc_ref, kp_ref, vc_ref, vp_ref)]
            k2 = jnp.concatenate([kp, kc], axis=0)
            v2 = jnp.concatenate([vp, vc], axis=0)
            qs = _stack_heads(q_ref[:, :, kh * qw:(kh + 1) * qw].reshape(b, qw), rep)
            sc = jnp.where(mask, _dot_nt(qs, k2) * scale, NEG_INF)
            m = jnp.max(sc, axis=-1, keepdims=True)
            p = jnp.exp(sc - m)
            l = jnp.sum(p, axis=-1, keepdims=True)
            out = _dot((p / l).astype(BF16), v2)
            lse = m + jnp.log(l)
            wide = jnp.broadcast_to(lse, (rep * b, HEAD_DIM))
            for h in range(rep):
                cols = slice(kh * qw + h * HEAD_DIM, kh * qw + (h + 1) * HEAD_DIM)
                o_ref[:, :, cols] = out[h * b:(h + 1) * b].astype(BF16).reshape(g, j, HEAD_DIM)
                lw_ref[:, :, cols] = wide[h * b:(h + 1) * b].reshape(g, j, HEAD_DIM)
                l_ref[:, :, kh * rep + h:kh * rep + h + 1] = lse[h * b:(h + 1) * b].reshape(g, j, 1)

    def kv_spec(col, prev):
        if prev:
            return pl.BlockSpec((g, j, kw), lambda c, jb: (c, jnp.maximum(jb - 1, 0), col))
        return pl.BlockSpec((g, j, kw), lambda c, jb: (c, jb, col))

    out_spec = pl.BlockSpec((g, j, nh * HEAD_DIM), lambda c, jb: (c, jb, 0))
    out, lse, lse_wide = pl.pallas_call(
        kern, name=name, grid=(dil, n // j),
        in_specs=[pl.BlockSpec((g, j, nh * HEAD_DIM), lambda c, jb: (c, jb, gi)),
                  kv_spec(0, False), kv_spec(0, True), kv_spec(1, False), kv_spec(1, True)],
        out_specs=[out_spec, pl.BlockSpec((g, j, nh), lambda c, jb: (c, jb, 0)), out_spec],
        out_shape=[jax.ShapeDtypeStruct((DIL_MAX, n, nh * HEAD_DIM), BF16),
                   jax.ShapeDtypeStruct((DIL_MAX, n, nh), F32),
                   jax.ShapeDtypeStruct((DIL_MAX, n, nh * HEAD_DIM), F32)],
        compiler_params=_params(("parallel", "arbitrary")),
    )(q3, kv3, kv3, kv3, kv3)
    del ng
    return out.reshape(s, nh * HEAD_DIM), lse.reshape(s, nh), lse_wide.reshape(s, nh * HEAD_DIM)


def _att_bwd(q, kv, do, o, lses, dq_all, gi, dil, nh, name):
    s = q.shape[0]
    n, g, j = _att_geometry(dil, s)
    b = g * j
    nb = n // j
    ng = len(lses)
    rep = nh // N_KV_HEADS
    qw = rep * HEAD_DIM
    dm = nh * HEAD_DIM
    scale = HEAD_DIM ** -0.5
    q3 = q.reshape(DIL_MAX, n, q.shape[1])
    kv3 = kv.reshape(DIL_MAX, n, kv.shape[1])
    wide = [a.reshape(DIL_MAX, n, dm) for a in (do, o, *lses)]
    kw = N_KV_HEADS * HEAD_DIM

    def kern(q_ref, kc_ref, kp_ref, vc_ref, vp_ref, do_ref, o_ref, *refs):
        l_refs = refs[:ng]
        dq_ref, dk_ref, dv_ref, ck_ref, cv_ref = refs[ng + 1:]
        jb = pl.program_id(1)

        @pl.when(jb == 0)
        def _():
            ck_ref[...] = jnp.zeros_like(ck_ref)
            cv_ref[...] = jnp.zeros_like(cv_ref)

        @pl.when(jb < nb)
        def _():
            mask = _att_mask(g, j, jb, rep)
            for kh in range(N_KV_HEADS):
                hs = slice(kh * HEAD_DIM, (kh + 1) * HEAD_DIM)
                ws = slice(kh * qw, (kh + 1) * qw)

                def stacked(ref):
                    return _stack_heads(ref[:, :, ws].reshape(b, qw), rep)

                kc, kp, vc, vp = [r[:, :, hs].reshape(b, HEAD_DIM) for r in (kc_ref, kp_ref, vc_ref, vp_ref)]
                k2 = jnp.concatenate([kp, kc], axis=0)
                v2 = jnp.concatenate([vp, vc], axis=0)
                qs = stacked(q_ref)
                ls = [stacked(r)[:, 0:1] for r in l_refs]
                mx = functools.reduce(jnp.maximum, ls)
                den = functools.reduce(lambda a, c: a + c, [jnp.exp(l - mx) for l in ls])
                lse_g = ls[gi]
                w = jnp.exp(lse_g - mx) / den
                do_ = stacked(do_ref)
                ct = w * jnp.sum(do_ * stacked(o_ref), axis=-1, keepdims=True)
                dob = (w * do_).astype(BF16)
                p = jnp.exp(jnp.where(mask, _dot_nt(qs, k2) * scale, NEG_INF) - lse_g)
                ds = (p * (_dot_nt(dob, v2) - ct) * scale).astype(BF16)
                dq = (_dot(ds, k2)).astype(BF16)
                for h in range(rep):
                    cols = slice(kh * qw + h * HEAD_DIM, kh * qw + (h + 1) * HEAD_DIM)
                    dq_ref[:, :, cols] = dq[h * b:(h + 1) * b].reshape(g, j, HEAD_DIM)
                dk2 = _dot_tn(ds, qs)
                dv2 = _dot_tn(p.astype(BF16), dob)
                dk_ref[:, :, hs] = (ck_ref[:, hs] + dk2[:b]).reshape(g, j, HEAD_DIM)
                dv_ref[:, :, hs] = (cv_ref[:, hs] + dv2[:b]).reshape(g, j, HEAD_DIM)
                ck_ref[:, hs] = dk2[b:]
                cv_ref[:, hs] = dv2[b:]

        @pl.when(jb == nb)
        def _():
            dk_ref[...] = ck_ref[...].reshape(g, j, kw)
            dv_ref[...] = cv_ref[...].reshape(g, j, kw)

    def jq(jb):
        return jnp.minimum(jb, nb - 1)

    def kv_spec(col, prev):
        if prev:
            return pl.BlockSpec((g, j, kw), lambda c, jb: (c, jnp.maximum(jq(jb) - 1, 0), col))
        return pl.BlockSpec((g, j, kw), lambda c, jb: (c, jq(jb), col))

    wide_spec = pl.BlockSpec((g, j, dm), lambda c, jb: (c, jq(jb), 0))
    dkv_spec = pl.BlockSpec((g, j, kw), lambda c, jb: (c, jnp.maximum(jb - 1, 0), 0))
    dq, dk, dv = pl.pallas_call(
        kern, name=name, grid=(dil, nb + 1),
        in_specs=[pl.BlockSpec((g, j, dm), lambda c, jb: (c, jq(jb), gi)),
                  kv_spec(0, False), kv_spec(0, True), kv_spec(1, False), kv_spec(1, True)]
        + [wide_spec] * (2 + ng) + [_ANY],
        out_specs=[pl.BlockSpec((g, j, dm), lambda c, jb: (c, jq(jb), gi)), dkv_spec, dkv_spec],
        out_shape=[jax.ShapeDtypeStruct((DIL_MAX, n, dq_all.shape[1]), BF16),
                   jax.ShapeDtypeStruct((DIL_MAX, n, kw), F32),
                   jax.ShapeDtypeStruct((DIL_MAX, n, kw), F32)],
        scratch_shapes=[pltpu.VMEM((b, kw), F32), pltpu.VMEM((b, kw), F32)],
        input_output_aliases={7 + ng: 0},
        compiler_params=_params(("parallel", "arbitrary")),
    )(q3, kv3, kv3, kv3, kv3, *wide, dq_all.reshape(DIL_MAX, n, dq_all.shape[1]))
    return dq.reshape(s, -1), dk.reshape(s, -1), dv.reshape(s, -1)


def _dkv_sum(dks, dvs, name):
    s, w = dks[0].shape
    tm = _tile(s, ROWS, 16)
    ng = len(dks)

    def kern(*refs):
        o_ref = refs[2 * ng]
        o_ref[:, :w] = functools.reduce(lambda a, b: a + b, [refs[t][...] for t in range(ng)]).astype(BF16)
        o_ref[:, w:] = functools.reduce(lambda a, b: a + b, [refs[ng + t][...] for t in range(ng)]).astype(BF16)

    return pl.pallas_call(
        kern, name=name, grid=(s // tm,),
        in_specs=[_row_spec(tm, w)] * (2 * ng),
        out_specs=_row_spec(tm, 2 * w),
        out_shape=jax.ShapeDtypeStruct((s, 2 * w), BF16),
        compiler_params=_params(("parallel",)),
    )(*dks, *dvs)


def _ffn_fwd(hf, weight, layer, tag):
    a, gu = _ffn_in_act(hf, weight(f"w_in{layer}", hf), f"{tag}_in")
    o = _mm_nn(a, weight(f"w_out{layer}", a), F32, f"{tag}_out", tn=1024, tk=5632)
    return gu, a, o


def _ffn_bwd(dresb, hf, gu, a, weight, emit, layer, tag):
    w_in, w_out = weight(f"w_in{layer}", None), weight(f"w_out{layer}", None)
    zero = emit(f"w_out{layer}", _mm_tn(a, dresb, 1, f"{tag}_out_dw", tn=1024))
    dgu = _ffn_out_dx_act(dresb, w_out, gu, f"{tag}_out_dx")
    zero = zero + emit(f"w_in{layer}",
                       _mm_tn(hf, dgu, w_in.shape[0], f"{tag}_in_dw", tn=FFN_TILE, paired=True))
    return _mm_nt(dgu, w_in, f"{tag}_in_dx", tr=FFN_TILE, tkc=512, paired=True), zero


def _local_step(x, target, small, weight, emit, stage):
    s, d = x.shape
    nh = d // HEAD_DIM
    n_groups = d // S5_GROUP_CH
    nt = n_groups // GROUPS_PER_TILE
    p_, c_ = S5_STATE, S5_GROUP_CH

    disc_in = (small["lam_re"], small["lam_im"], small["log_dt"], small["b_re"], small["b_im"])
    (lb_re, lb_im, bb_re, bb_im), disc_vjp = jax.vjp(_s5_discretize, *disc_in)
    del lb_re, lb_im
    dt = jnp.exp(small["log_dt"])[:, None]
    def pole_powers(exponents):
        k = exponents[:, None, None]
        mag = jnp.exp(k * (small["lam_re"] * dt)[None])
        ang = k * (small["lam_im"] * dt)[None]
        return ((mag * jnp.cos(ang)).reshape(SUB, n_groups * p_),
                (mag * jnp.sin(ang)).reshape(SUB, n_groups * p_))

    pw_re, pw_im = pole_powers(jnp.arange(1, SUB + 1, dtype=F32))
    pf_re, pf_im = pole_powers(jnp.arange(SUB, 0, -1, dtype=F32))
    bbd_re = _block_diag(bb_re.transpose(0, 2, 1).reshape(nt, GROUPS_PER_TILE, c_, p_)).astype(BF16)
    bbd_im = _block_diag(bb_im.transpose(0, 2, 1).reshape(nt, GROUPS_PER_TILE, c_, p_)).astype(BF16)
    cbd_re = _block_diag(small["c_re"].transpose(0, 2, 1).reshape(nt, GROUPS_PER_TILE, p_, c_)).astype(BF16)
    cbd_im = _block_diag(small["c_im"].transpose(0, 2, 1).reshape(nt, GROUPS_PER_TILE, p_, c_)).astype(BF16)

    h0 = _norm_f32(x, small["a_norm"], "s5_norm")
    xs_re, xs_im, y0, z = _s5_fwd(h0, bbd_re, bbd_im, cbd_re, cbd_im, pw_re, pw_im, small["s5_d"], "s5_fwd")
    zero = stage("glu", z)
    vg = _mm_nn(z, weight("w_glu", z), F32, "glu_mm", tn=1024)
    x1, hf0 = _glu_res_norm(vg, x, small["ffn_norm0"] + zero, "glu_res_norm")
    gu0, a0, o0 = _ffn_fwd(hf0, weight, 0, "ffn0")
    zero = stage("attention", a0)
    x2 = _to_slabs([o0, x1], "to_slabs")
    kvn, h1 = _norm_bf16(x2, [small["kv_norm"] + zero, small["b_norm"]], "att_norms")
    kv = _mm_nn(kvn, weight("w_kv", kvn), BF16, "kv_mm", tn=1024)
    q = _mm_nn(h1, weight("w_q", kv), BF16, "q_mm", tn=1536)
    outs, lses_narrow, lses = [], [], []
    for gi, (window, dil) in enumerate(PATTERNS):
        assert window // dil == ATT_BLK
        og, lg, lg_wide = _att_fwd(q, kv, gi, dil, nh, f"att_fwd{gi}")
        outs.append(og)
        lses_narrow.append(lg)
        lses.append(lg_wide)
    oatt, oattb = _att_combine(outs, lses_narrow, "att_combine")
    x3, hf1 = _mm_res_norm(oattb, weight("w_o", oattb), x2, small["ffn_norm1"] + stage("ffn1", oattb), "o_mm")
    gu1, a1, o1 = _ffn_fwd(hf1, weight, 1, "ffn1")
    dres, dresb, loss_rows, d_final = _loss_head(
        o1, x3, small["final_norm"], _to_slabs([target], "target_to_slabs"), "loss_head")

    dhf1, zero = _ffn_bwd(dresb, hf1, gu1, a1, weight, emit, 1, "ffn1")
    dres, dresb, d_ffn_norm1 = _norm_bwd(x3, [small["ffn_norm1"] + zero], [dhf1], dres, "ffn1_norm_bwd", True)
    zero = emit("w_o", _mm_tn(oattb, dresb, 1, "o_dw", tn=1024))
    doatt = _mm_nt(dresb, weight("w_o", None), "o_dx", tr=2048)
    dks, dvs = [], []
    dq = lax.empty(q.shape, BF16)
    for gi, (window, dil) in enumerate(PATTERNS):
        dq, dk_g, dv_g = _att_bwd(q, kv, doatt, oatt, lses, dq, gi, dil, nh, f"att_bwd{gi}")
        dks.append(dk_g)
        dvs.append(dv_g)
    dkv = _dkv_sum(dks, dvs, "dkv_sum")
    w_q = weight("w_q", None)
    zero = zero + emit("w_q", _mm_tn(h1, dq, w_q.shape[0], "q_dw", tn=1536))
    zero = zero + emit("w_kv", _mm_tn(kvn, dkv, 1, "kv_dw", tn=1024))
    dh1 = _mm_nt(dq, w_q, "q_dx", tr=1536)
    dkvn = _mm_nt(dkv, weight("w_kv", None), "kv_dx", tr=1024)
    dres, d_b_norm, d_kv_norm = _norm_bwd(
        x2, [small["b_norm"] + zero, small["kv_norm"]], [dh1, dkvn], dres, "att_norm_bwd")
    dres, dresb = _from_slabs(dres, "from_slabs")
    dhf0, zero = _ffn_bwd(dresb, hf0, gu0, a0, weight, emit, 0, "ffn0")
    dres, d_ffn_norm0 = _norm_bwd(x1, [small["ffn_norm0"] + zero], [dhf0], dres, "ffn0_norm_bwd")
    dvg = _glu_bwd(dres, vg, "glu_bwd")
    w_glu = weight("w_glu", None)
    zero = emit("w_glu", _mm_tn(z, dvg, w_glu.shape[0], "glu_dw", tn=1024))
    dz = _mm_nt(dvg, w_glu, "glu_dx", tr=1024)
    dh0, d_s5_d, dcr, dci_neg, dbr, dbi, dar, dai = _s5_bwd(
        dz, y0, h0, xs_re, xs_im, bbd_re, bbd_im, cbd_re, cbd_im, pw_re, pw_im, pf_re, pf_im,
        small["s5_d"] + zero, "s5_bwd")
    grad_x, d_a_norm = _norm_bwd(x, [small["a_norm"]], [dh0], dres, "s5_norm_bwd")

    d_bb_re = _block_diag_take(dbr, c_, p_).transpose(0, 2, 1)
    d_bb_im = _block_diag_take(dbi, c_, p_).transpose(0, 2, 1)
    d_c_re = _block_diag_take(dcr, c_, p_)
    d_c_im = -_block_diag_take(dci_neg, c_, p_)
    d_lam_re, d_lam_im, d_log_dt, d_b_re, d_b_im = disc_vjp(
        (dar.reshape(n_groups, p_), dai.reshape(n_groups, p_), d_bb_re, d_bb_im))

    d_small = dict(lam_re=d_lam_re, lam_im=d_lam_im, log_dt=d_log_dt, b_re=d_b_re, b_im=d_b_im,
                   c_re=d_c_re, c_im=d_c_im, s5_d=d_s5_d, a_norm=d_a_norm, ffn_norm0=d_ffn_norm0,
                   ffn_norm1=d_ffn_norm1, b_norm=d_b_norm, kv_norm=d_kv_norm, final_norm=d_final)
    return loss_rows, grad_x, d_small


def _place():
    x, y, c = lax.axis_index("x"), lax.axis_index("y"), lax.axis_index("c")
    return x, y, c, [(1 - x, y), (x, 1 - y), (1 - x, 1 - y)]


_ANY = pl.BlockSpec(memory_space=pl.ANY)


_HBM = pl.BlockSpec(memory_space=pltpu.HBM)
_SEM = pl.BlockSpec(memory_space=pltpu.SEMAPHORE)
_EFFECT = pltpu.SideEffectType.DATAFLOW_SIDE_EFFECTING


def _in_hbm(a):
    return pltpu.with_memory_space_constraint(a, pltpu.HBM)


def _cast_place(shards, layer, chip, name):
    _, r, c = shards.shape
    tm = _tile(r, ROWS, 16)

    def kern(chip_ref, x_ref, o_ref):
        del chip_ref
        o_ref[...] = x_ref[...].astype(BF16)

    return pl.pallas_call(
        kern, name=name,
        grid_spec=pltpu.PrefetchScalarGridSpec(
            num_scalar_prefetch=1, grid=(r // tm,),
            in_specs=[pl.BlockSpec((None, tm, c), lambda i, ch: (layer, i, 0))],
            out_specs=pl.BlockSpec((None, tm, c), lambda i, ch: (ch[0], i, 0))),
        out_shape=jax.ShapeDtypeStruct((N_CHIPS, r, c), BF16),
        compiler_params=_params(("parallel",)),
    )(chip, shards)


def _my_part(land, block, c, halves):
    if not halves:
        return land.at[block]
    half = land.shape[1] // 2
    return land.at[block, pl.ds(c * half, half)]


def _gather_start(lands, name, halves, after):
    n = len(lands)

    def body(*refs):
        land = refs[:n]
        send, recv = refs[n + 1:2 * n + 1], refs[2 * n + 1:3 * n + 1]
        token = refs[4 * n + 1]
        x, y, c, peers = _place()
        me = 2 * x + y
        for a in range(n):
            for k, (px, py) in enumerate(peers):
                part = _my_part(land[a], me, c, halves)
                pltpu.make_async_remote_copy(
                    src_ref=part, dst_ref=part, send_sem=send[a].at[k], recv_sem=recv[a].at[k],
                    device_id=(px, py, c), device_id_type=MESH).start()
        token[...] = jnp.zeros_like(token)

    outs = pl.pallas_call(
        body, name=name,
        out_shape=[pltpu.SemaphoreType.DMA((3,))] * (2 * n) + [pltpu.HBM(a.shape, a.dtype) for a in lands]
        + [jax.ShapeDtypeStruct((8, 128), F32)],
        in_specs=[_HBM] * n + [_ANY],
        out_specs=[_SEM] * (2 * n) + [_HBM] * n + [pl.BlockSpec(memory_space=pltpu.VMEM)],
        input_output_aliases={i: 2 * n + i for i in range(n)},
        compiler_params=pltpu.CompilerParams(has_side_effects=_EFFECT),
    )(*[_in_hbm(a) for a in lands], after)
    return outs[:n], outs[n:2 * n], outs[2 * n:3 * n], outs[3 * n]


def _gather_wait(lands, sends, recvs, after, name, halves):
    n = len(lands)

    def body(*refs):
        land, send, recv = refs[:n], refs[n:2 * n], refs[2 * n:3 * n]
        x, y, c, peers = _place()
        me = 2 * x + y
        for a in range(n):
            for k, (px, py) in enumerate(peers):
                cp = pltpu.make_async_remote_copy(
                    src_ref=_my_part(land[a], me, c, halves), dst_ref=_my_part(land[a], 2 * px + py, c, halves),
                    send_sem=send[a].at[k], recv_sem=recv[a].at[k], device_id=(px, py, c), device_id_type=MESH)
                cp.wait_send()
                cp.wait_recv()

    return pl.pallas_call(
        body, name=name,
        out_shape=[pltpu.HBM(a.shape, a.dtype) for a in lands],
        in_specs=[_HBM] * n + [_SEM] * (2 * n) + [_ANY], out_specs=[_HBM] * n,
        input_output_aliases={i: i for i in range(n)},
        compiler_params=pltpu.CompilerParams(has_side_effects=_EFFECT),
    )(*lands, *sends, *recvs, after)


def _forward_start(lands, name):
    n = len(lands)

    def body(*refs):
        land = refs[:n]
        send, recv = refs[n:2 * n], refs[2 * n:3 * n]
        token = refs[4 * n]
        x, y, c, peers = _place()
        for a in range(n):
            for k, (px, py) in enumerate(peers):
                part = _my_part(land[a], 2 * px + py, c, True)
                pltpu.make_async_remote_copy(
                    src_ref=part, dst_ref=part, send_sem=send[a].at[k], recv_sem=recv[a].at[k],
                    device_id=(x, y, 1 - c), device_id_type=MESH).start()
        token[...] = jnp.zeros_like(token)

    outs = pl.pallas_call(
        body, name=name,
        out_shape=[pltpu.SemaphoreType.DMA((3,))] * (2 * n) + [pltpu.HBM(a.shape, a.dtype) for a in lands]
        + [jax.ShapeDtypeStruct((8, 128), F32)],
        in_specs=[_HBM] * n,
        out_specs=[_SEM] * (2 * n) + [_HBM] * n + [pl.BlockSpec(memory_space=pltpu.VMEM)],
        input_output_aliases={i: 2 * n + i for i in range(n)},
        compiler_params=pltpu.CompilerParams(has_side_effects=_EFFECT),
    )(*[_in_hbm(a) for a in lands])
    return outs[:n], outs[n:2 * n], outs[2 * n:3 * n], outs[3 * n]


def _forward_wait(land, send, recv, after, name):
    def body(land_ref, send_sem, recv_sem, after_ref, land_out):
        del after_ref, land_out
        x, y, c, peers = _place()
        for k, (px, py) in enumerate(peers):
            cp = pltpu.make_async_remote_copy(
                src_ref=_my_part(land_ref, 2 * px + py, c, True), dst_ref=_my_part(land_ref, 2 * px + py, 1 - c, True),
                send_sem=send_sem.at[k], recv_sem=recv_sem.at[k], device_id=(x, y, 1 - c), device_id_type=MESH)
            cp.wait_send()
            cp.wait_recv()

    return pl.pallas_call(
        body, name=name,
        out_shape=pltpu.HBM(land.shape, land.dtype),
        in_specs=[_HBM, _SEM, _SEM, _ANY], out_specs=_HBM,
        input_output_aliases={0: 0},
        compiler_params=pltpu.CompilerParams(has_side_effects=_EFFECT),
    )(land, send, recv, after)


def _scatter_start(g, name):
    def body(g_ref, land_ref, send, recv, g_out, land_out, token):
        del g_out, land_out
        x, y, c, peers = _place()
        for k, (px, py) in enumerate(peers):
            pltpu.make_async_remote_copy(
                src_ref=g_ref.at[2 * px + py], dst_ref=land_ref.at[k], send_sem=send.at[k], recv_sem=recv.at[k],
                device_id=(px, py, c), device_id_type=MESH).start()
        token[...] = jnp.zeros_like(token)

    land = lax.empty((3,) + g.shape[1:], g.dtype)
    return pl.pallas_call(
        body, name=name,
        out_shape=(pltpu.SemaphoreType.DMA((3,)), pltpu.SemaphoreType.DMA((3,)),
                   pltpu.HBM(g.shape, g.dtype), pltpu.HBM(land.shape, land.dtype),
                   jax.ShapeDtypeStruct((8, 128), F32)),
        in_specs=(_HBM, _HBM), out_specs=(_SEM, _SEM, _HBM, _HBM, pl.BlockSpec(memory_space=pltpu.VMEM)),
        input_output_aliases={0: 2, 1: 3},
        compiler_params=pltpu.CompilerParams(has_side_effects=_EFFECT),
    )(_in_hbm(g), _in_hbm(land))


def _scatter_wait(started, after):
    n = len(started)

    def body(*refs):
        gs, lands = refs[:n], refs[n:2 * n]
        sends, recvs = refs[2 * n:3 * n], refs[3 * n:4 * n]
        x, y, c, peers = _place()
        for a in range(n):
            for k, (px, py) in enumerate(peers):
                cp = pltpu.make_async_remote_copy(
                    src_ref=gs[a].at[2 * px + py], dst_ref=lands[a].at[k], send_sem=sends[a].at[k],
                    recv_sem=recvs[a].at[k], device_id=(px, py, c), device_id_type=MESH)
                cp.wait_send()
                cp.wait_recv()

    gs = [s[2] for s in started]
    lands = [s[3] for s in started]
    outs = pl.pallas_call(
        body, name="scatter_wait",
        out_shape=[pltpu.HBM(a.shape, a.dtype) for a in gs + lands],
        in_specs=[_HBM] * (2 * n) + [_SEM] * (2 * n) + [_ANY], out_specs=[_HBM] * (2 * n),
        input_output_aliases={i: i for i in range(2 * n)},
        compiler_params=pltpu.CompilerParams(has_side_effects=_EFFECT),
    )(*gs, *lands, *[s[0] for s in started], *[s[1] for s in started], after)
    return outs[:n], outs[n:]


def _sibling():
    return lax.axis_index("x"), lax.axis_index("y"), 1 - lax.axis_index("c")


def _swap_start(parts):
    n = len(parts)

    def body(*refs):
        ins, land = refs[:n], refs[n:2 * n]
        send, recv = refs[2 * n], refs[2 * n + 1]
        token = refs[4 * n + 2]
        for a in range(n):
            pltpu.make_async_remote_copy(
                src_ref=ins[a], dst_ref=land[a], send_sem=send.at[a], recv_sem=recv.at[a],
                device_id=_sibling(), device_id_type=MESH).start()
        token[...] = jnp.zeros_like(token)

    lands = [lax.empty(a.shape, a.dtype) for a in parts]
    outs = pl.pallas_call(
        body, name="swap_start",
        out_shape=[pltpu.SemaphoreType.DMA((n,))] * 2 + [pltpu.HBM(a.shape, a.dtype) for a in parts + lands]
        + [jax.ShapeDtypeStruct((8, 128), F32)],
        in_specs=[_HBM] * (2 * n),
        out_specs=[_SEM] * 2 + [_HBM] * (2 * n) + [pl.BlockSpec(memory_space=pltpu.VMEM)],
        input_output_aliases={i: 2 + i for i in range(2 * n)},
        compiler_params=pltpu.CompilerParams(has_side_effects=_EFFECT),
    )(*[_in_hbm(a) for a in parts + lands])
    return outs[0], outs[1], outs[2:2 + n], outs[2 + n:2 + 2 * n], outs[2 + 2 * n]


def _swap_wait(parts, lands, send, recv, after):
    n = len(parts)

    def body(*refs):
        ins, land = refs[:n], refs[n:2 * n]
        send_sem, recv_sem = refs[2 * n], refs[2 * n + 1]
        for a in range(n):
            cp = pltpu.make_async_remote_copy(
                src_ref=ins[a], dst_ref=land[a], send_sem=send_sem.at[a], recv_sem=recv_sem.at[a],
                device_id=_sibling(), device_id_type=MESH)
            cp.wait_send()
            cp.wait_recv()

    outs = pl.pallas_call(
        body, name="swap_wait",
        out_shape=[pltpu.HBM(a.shape, a.dtype) for a in list(parts) + list(lands)],
        in_specs=[_HBM] * (2 * n) + [_SEM] * 2 + [_ANY], out_specs=[_HBM] * (2 * n),
        input_output_aliases={i: i for i in range(2 * n)},
        compiler_params=pltpu.CompilerParams(has_side_effects=_EFFECT),
    )(*parts, *lands, send, recv, after)
    return outs[:n], outs[n:]


def _all_reduce_small(v):
    nd, r, w = v.shape
    assert nd == N_DEV

    def body(v_ref, out_ref, land_ref, red_ref, send1, recv1, send2, recv2):
        x, y, c = lax.axis_index("x"), lax.axis_index("y"), lax.axis_index("c")
        me = 4 * x + 2 * y + c
        peers = []
        for k in range(1, N_DEV):
            kx, ky, kc = (k >> 2) & 1, (k >> 1) & 1, k & 1
            peers.append((1 - x if kx else x, 1 - y if ky else y, 1 - c if kc else c))
        first = []
        for k, (px, py, pc) in enumerate(peers):
            cp = pltpu.make_async_remote_copy(
                src_ref=v_ref.at[4 * px + 2 * py + pc], dst_ref=land_ref.at[me], send_sem=send1.at[k],
                recv_sem=recv1.at[k], device_id=(px, py, pc), device_id_type=MESH)
            cp.start()
            first.append(cp)
        land_ref[me] = v_ref[me]
        for cp in first:
            cp.wait()
        acc = land_ref[0]
        for j in range(1, N_DEV):
            acc = acc + land_ref[j]
        red_ref[...] = acc
        second = []
        for k, (px, py, pc) in enumerate(peers):
            cp = pltpu.make_async_remote_copy(
                src_ref=red_ref, dst_ref=out_ref.at[me], send_sem=send2.at[k],
                recv_sem=recv2.at[k], device_id=(px, py, pc), device_id_type=MESH)
            cp.start()
            second.append(cp)
        out_ref[me] = acc
        for cp in second:
            cp.wait()

    vmem = pl.BlockSpec(memory_space=pltpu.VMEM)
    return pl.pallas_call(
        body, name="all_reduce_small",
        in_specs=[vmem], out_specs=vmem,
        out_shape=jax.ShapeDtypeStruct((nd, r, w), F32),
        scratch_shapes=[pltpu.VMEM((nd, r, w), F32), pltpu.VMEM((r, w), F32)]
        + [pltpu.SemaphoreType.DMA((N_DEV - 1,))] * 4,
        compiler_params=pltpu.CompilerParams(vmem_limit_bytes=VMEM_LIMIT),
    )(v)


def _adam_math(w, g, m, v):
    m = ADAM_B1 * m + (1.0 - ADAM_B1) * g
    v = ADAM_B2 * v + (1.0 - ADAM_B2) * (g * g)
    m_hat = m / (1.0 - ADAM_B1 ** ADAM_STEP)
    v_hat = v / (1.0 - ADAM_B2 ** ADAM_STEP)
    delta = -ADAM_LR * (m_hat / (jnp.sqrt(v_hat) + ADAM_EPS) + ADAM_WD * w)
    return delta, m, v


def _sum_blocks(own, got, chip, name):
    _, r, c = own.shape
    tm = _tile(r, ROWS, 16)

    def kern(chip_ref, own_ref, got_ref, o_ref):
        del chip_ref
        acc = own_ref[...].astype(F32)
        for k in range(3):
            acc = acc + got_ref[k].astype(F32)
        o_ref[...] = acc

    return pl.pallas_call(
        kern, name=name,
        grid_spec=pltpu.PrefetchScalarGridSpec(
            num_scalar_prefetch=1, grid=(r // tm,),
            in_specs=[pl.BlockSpec((None, tm, c), lambda i, ch: (ch[0], i, 0)),
                      pl.BlockSpec((3, tm, c), lambda i, ch: (0, i, 0))],
            out_specs=pl.BlockSpec((tm, c), lambda i, ch: (i, 0))),
        out_shape=jax.ShapeDtypeStruct((r, c), F32),
        compiler_params=_params(("parallel",)),
    )(chip, own, got)


def _adamw(parts, w, m, v, name):
    nl, r, c = w.shape
    assert len(parts) == nl
    tm = _tile(r, 128, 8)

    def kern(*refs):
        p_refs = refs[:2 * nl]
        w_ref, m_ref, v_ref, g_ref, d_ref, mo_ref, vo_ref = refs[2 * nl:]
        layer = pl.program_id(0)
        g = p_refs[0][...] + p_refs[1][...]
        for ll in range(1, nl):
            g = jnp.where(layer == ll, p_refs[2 * ll][...] + p_refs[2 * ll + 1][...], g)
        g_ref[...] = g
        d_ref[...], mo_ref[...], vo_ref[...] = _adam_math(w_ref[...], g, m_ref[...], v_ref[...])

    def part_spec(ll):
        return pl.BlockSpec((tm, c), lambda l, i: (jnp.where(l == ll, i, 0), 0))

    spec = pl.BlockSpec((None, tm, c), lambda l, i: (l, i, 0))
    return pl.pallas_call(
        kern, name=name, grid=(nl, r // tm),
        in_specs=[part_spec(ll) for ll in range(nl) for _ in range(2)] + [spec] * 3, out_specs=[spec] * 4,
        out_shape=[jax.ShapeDtypeStruct((nl, r, c), F32)] * 4,
        compiler_params=_params(("arbitrary", "parallel")),
    )(*[p for pair in parts for p in pair], w, m, v)


def _adamw_small(g, w, m, v, name):
    def kern(g_ref, w_ref, m_ref, v_ref, d_ref, mo_ref, vo_ref):
        d_ref[...], mo_ref[...], vo_ref[...] = _adam_math(w_ref[...], g_ref[...], m_ref[...], v_ref[...])

    return pl.pallas_call(
        kern, name=name,
        out_shape=[jax.ShapeDtypeStruct(g.shape, F32)] * 3,
        compiler_params=pltpu.CompilerParams(vmem_limit_bytes=VMEM_LIMIT),
    )(g, w, m, v)


def _pack(arrays, rows):
    flat = jnp.concatenate([a.reshape(-1).astype(F32) for a in arrays])
    return jnp.pad(flat, (0, rows * 128 - flat.shape[0])).reshape(rows, 128)


def _unpack(packed, shapes):
    flat = packed.reshape(-1)
    out, off = [], 0
    for shp in shapes:
        size = math.prod(shp)
        out.append(flat[off:off + size].reshape(shp))
        off += size
    return out


_REPLICATED = ["s5_lam_re", "s5_lam_im", "s5_log_dt", "s5_b_re", "s5_b_im", "s5_c_re", "s5_c_im",
               "ffn_norm", "b_norm_mix", "kv_norm", "final_norm"]
_CHIP_VECTORS = ["s5_d", "a_norm_mix"]
_BIG = ["s5_w_glu", "ffn_w_in", "ffn_w_out", "attn_w_q", "attn_w_o", "w_kv"]
_WEIGHT_ORDER = ["s5_lam_re", "s5_lam_im", "s5_log_dt", "s5_b_re", "s5_b_im", "s5_c_re", "s5_c_im", "s5_d",
                 "s5_w_glu", "a_norm_mix", "ffn_norm", "ffn_w_in", "ffn_w_out", "b_norm_mix", "attn_w_q",
                 "attn_w_o", "kv_norm", "w_kv", "final_norm"]


def _step(x, loss_target, w, m, v):
    s, d = x.shape[1], x.shape[2]
    chip = 2 * lax.axis_index("x") + lax.axis_index("y")

    col_sharded = dict(w_glu=("s5_w_glu", 0), w_in0=("ffn_w_in", 0), w_in1=("ffn_w_in", 1), w_q=("attn_w_q", 0))
    row_sharded = dict(w_out0=("ffn_w_out", 0), w_out1=("ffn_w_out", 1), w_o=("attn_w_o", 0), w_kv=("w_kv", 0))
    names = ["w_glu", "w_in0", "w_out0", "w_kv", "w_q", "w_o", "w_in1", "w_out1"]
    local = {**col_sharded, **row_sharded}

    def layers(a):
        return a.reshape((-1,) + a.shape[-2:])

    chip_arr = jnp.reshape(chip, (1,)).astype(jnp.int32)
    vector_lands = [lax.dynamic_update_slice(jnp.zeros((N_CHIPS,) + w[n].shape, F32), w[n][None], (chip, 0, 0))
                    for n in _CHIP_VECTORS]
    v_send, v_recv, v_land, v_token = _gather_start(vector_lands, "gather_start_vectors", False,
                                                    jnp.zeros((8, 128), F32))
    first = 2
    lands = [_cast_place(layers(w[local[n][0]]), local[n][1], chip_arr, f"cast_{n}") for n in names[:first]]
    send, recv, land_thru, first_token = _gather_start(lands, "gather_start_first", True, v_token)
    lands = [_cast_place(layers(w[local[n][0]]), local[n][1], chip_arr, f"cast_{n}") for n in names[first:]]
    *later, rest_token = _gather_start(lands, "gather_start_rest", True, first_token)
    send, recv, land_thru = [list(a) + list(b) for a, b in zip((send, recv, land_thru), later)]
    vectors = _gather_wait(v_land, v_send, v_recv, jnp.zeros((8, 128), F32), "gather_wait_vectors", False)
    s5_d_full = vectors[0].reshape(1, d)
    a_norm_full = vectors[1].reshape(1, d) + rest_token[0, 0]

    batches = dict(glu=["w_glu", "w_in0", "w_out0"], attention=["w_kv", "w_q", "w_o"], ffn1=["w_in1", "w_out1"])
    forwarded, arrived = {}, {}

    def stage(batch, after):
        idx = [names.index(n) for n in batches[batch]]
        got = _gather_wait([land_thru[i] for i in idx], [send[i] for i in idx], [recv[i] for i in idx], after,
                           f"gather_wait_{batch}", True)
        f_send, f_recv, f_land, token = _forward_start(got, f"forward_start_{batch}")
        forwarded.update(zip(batches[batch], zip(f_send, f_recv, f_land)))
        return token[0, 0]

    def weight(name, after):
        if name not in arrived:
            f_send, f_recv, f_land = forwarded[name]
            land = _forward_wait(f_land, f_send, f_recv, after, f"forward_wait_{name}")
            arrived[name] = land if name in col_sharded else land.reshape(1, -1, land.shape[-1])
        return arrived[name]

    started = {}

    def emit(name, dw):
        outs = _scatter_start(dw.reshape(N_CHIPS, -1, dw.shape[-1]), f"scatter_start_{name}")
        started[name] = outs[:4]
        return outs[4][0, 0]

    small = dict(lam_re=w["s5_lam_re"][0], lam_im=w["s5_lam_im"][0], log_dt=w["s5_log_dt"][0],
                 b_re=w["s5_b_re"][0], b_im=w["s5_b_im"][0], c_re=w["s5_c_re"][0], c_im=w["s5_c_im"][0],
                 s5_d=s5_d_full, a_norm=a_norm_full, ffn_norm0=w["ffn_norm"][0:1], ffn_norm1=w["ffn_norm"][1:2],
                 b_norm=w["b_norm_mix"], kv_norm=w["kv_norm"].reshape(1, d), final_norm=w["final_norm"].reshape(1, d))

    loss_rows, grad_x, d_small = _local_step(x[0], loss_target[0], small, weight, emit, stage)

    g4, got = _scatter_wait([started[n] for n in names], grad_x)
    partial = [_sum_blocks(o, r, chip_arr, f"sum_{n}") for n, o, r in zip(names, g4, got)]
    swap_send, swap_recv, partial, swap_land, swap_token = _swap_start(partial)
    result = {}

    rep_grads = [d_small["lam_re"], d_small["lam_im"], d_small["log_dt"], d_small["b_re"], d_small["b_im"],
                 d_small["c_re"], d_small["c_im"],
                 jnp.concatenate([d_small["ffn_norm0"], d_small["ffn_norm1"]], axis=0),
                 d_small["b_norm"], d_small["kv_norm"], d_small["final_norm"]]
    vec_grads = [d_small["s5_d"], d_small["a_norm"]]
    to_reduce = rep_grads + vec_grads + [jnp.sum(loss_rows).reshape(1) + swap_token[0, 0]]
    total = sum(math.prod(a.shape) for a in to_reduce)
    rows_per = -(-total // (N_DEV * 128 * 8)) * 8
    reduced = _all_reduce_small(_pack(to_reduce, N_DEV * rows_per).reshape(N_DEV, rows_per, 128))
    red = _unpack(reduced, [a.shape for a in to_reduce])
    loss = red[-1][0]
    g_small = dict(zip(_REPLICATED, [r.reshape(w[n].shape) for r, n in zip(red[:len(rep_grads)], _REPLICATED)]))
    for n, r in zip(_CHIP_VECTORS, red[len(rep_grads):-1]):
        g_small[n] = lax.dynamic_slice_in_dim(r.reshape(1, d), chip * (d // N_CHIPS), d // N_CHIPS, axis=1)
    small_names = _REPLICATED + _CHIP_VECTORS
    n_small = sum(math.prod(w[n].shape) for n in small_names)
    rows_small = -(-n_small // (128 * 8)) * 8
    packed = [_pack([src[n] for n in small_names], rows_small) for src in (g_small, w, m, v)]
    upd = _adamw_small(*packed, "adamw_small")
    shapes = [w[n].shape for n in small_names]
    for n, dl, mo, vo in zip(small_names, *[_unpack(u, shapes) for u in upd]):
        result[n] = [g_small[n], dl, mo, vo]

    partial, other = _swap_wait(partial, swap_land, swap_send, swap_recv, upd[0])
    part = dict(zip(names, zip(partial, other)))
    for name in _BIG:
        parts = [part[n] for n in sorted((n for n in names if local[n][0] == name), key=lambda n: local[n][1])]
        outs = _adamw(parts, layers(w[name]), layers(m[name]), layers(v[name]), f"adamw_{name}")
        result[name] = [o.reshape(w[name].shape) for o in outs]

    cols = [[result[n][t] for n in _WEIGHT_ORDER] for t in range(4)]
    return (loss, grad_x.reshape(x.shape), *cols[0], *cols[1], *cols[2], *cols[3])


def kernel(x, s5_lam_re, s5_lam_im, s5_log_dt, s5_b_re, s5_b_im, s5_c_re, s5_c_im, s5_d, s5_w_glu, a_norm_mix, ffn_norm, ffn_w_in, ffn_w_out, b_norm_mix, attn_w_q, attn_w_o, kv_norm, w_kv, final_norm, loss_target, m_s5_lam_re, m_s5_lam_im, m_s5_log_dt, m_s5_b_re, m_s5_b_im, m_s5_c_re, m_s5_c_im, m_s5_d, m_s5_w_glu, m_a_norm_mix, m_ffn_norm, m_ffn_w_in, m_ffn_w_out, m_b_norm_mix, m_attn_w_q, m_attn_w_o, m_kv_norm, m_w_kv, m_final_norm, v_s5_lam_re, v_s5_lam_im, v_s5_log_dt, v_s5_b_re, v_s5_b_im, v_s5_c_re, v_s5_c_im, v_s5_d, v_s5_w_glu, v_a_norm_mix, v_ffn_norm, v_ffn_w_in, v_ffn_w_out, v_b_norm_mix, v_attn_w_q, v_attn_w_o, v_kv_norm, v_w_kv, v_final_norm):
    w = dict(zip(_WEIGHT_ORDER, (s5_lam_re, s5_lam_im, s5_log_dt, s5_b_re, s5_b_im, s5_c_re, s5_c_im, s5_d, s5_w_glu, a_norm_mix, ffn_norm, ffn_w_in, ffn_w_out, b_norm_mix, attn_w_q, attn_w_o, kv_norm, w_kv, final_norm)))
    m = dict(zip(_WEIGHT_ORDER, (m_s5_lam_re, m_s5_lam_im, m_s5_log_dt, m_s5_b_re, m_s5_b_im, m_s5_c_re, m_s5_c_im, m_s5_d, m_s5_w_glu, m_a_norm_mix, m_ffn_norm, m_ffn_w_in, m_ffn_w_out, m_b_norm_mix, m_attn_w_q, m_attn_w_o, m_kv_norm, m_w_kv, m_final_norm)))
    v = dict(zip(_WEIGHT_ORDER, (v_s5_lam_re, v_s5_lam_im, v_s5_log_dt, v_s5_b_re, v_s5_b_im, v_s5_c_re, v_s5_c_im, v_s5_d, v_s5_w_glu, v_a_norm_mix, v_ffn_norm, v_ffn_w_in, v_ffn_w_out, v_b_norm_mix, v_attn_w_q, v_attn_w_o, v_kv_norm, v_w_kv, v_final_norm)))
    return _step(x, loss_target, w, m, v)
```
